```python
import jax, jax.numpy as jnp
from jax import lax
import numpy as np

D_MODEL = 1024
BATCH = 8
SEQ = 4096
DEPTH = 4

N_MIXERS = 2
N_ATTN_LAYERS = (DEPTH + 1) // 2
N_HGRN_LAYERS = DEPTH // 2
HEAD_DIM = 64
N_Q_HEADS = D_MODEL // HEAD_DIM
N_KV_HEADS = N_Q_HEADS // 4
Q_PER_KV = N_Q_HEADS // N_KV_HEADS
WINDOW = 128
ATTN_BLOCK = WINDOW
ATTN_IN = (N_Q_HEADS + 2 * N_KV_HEADS) * HEAD_DIM
HG_EXPAND = 128
HG_HEADS = D_MODEL // HG_EXPAND
HG_K = HG_EXPAND
HG_V = D_MODEL // HG_HEADS
HG_FDIM = HG_HEADS * HG_K
HG_IDIM = HG_HEADS * HG_V
HG_IN = 2 * HG_FDIM + 2 * HG_IDIM
HG_CHUNK = 64
D_FF = 2816
CONV_W = 3
EPS = 1e-6

kernel_name = 'hybrid_swa_sink_alibi_hgrn2_convffn'


def rmsnorm(x, g):
    xf = x.astype(jnp.float32)
    y = xf * lax.rsqrt(jnp.mean(xf * xf, axis=-1, keepdims=True) + EPS)
    return (y * g.astype(jnp.float32)).astype(x.dtype)


def alibi_slopes():
    h = jnp.arange(1, N_Q_HEADS + 1, dtype=jnp.float32)
    return jnp.exp2(-8.0 * h / N_Q_HEADS)


def sliding_window_attention(h, w_in, w_out, sinks):
    B, S, _ = h.shape
    nblk = S // ATTN_BLOCK
    proj = h @ w_in
    q, k, v = jnp.split(proj, [N_Q_HEADS * HEAD_DIM, (N_Q_HEADS + N_KV_HEADS) * HEAD_DIM], axis=-1)
    q = q.reshape(B, nblk, ATTN_BLOCK, N_KV_HEADS, Q_PER_KV, HEAD_DIM)
    k = k.reshape(B, S, N_KV_HEADS, HEAD_DIM)
    v = v.reshape(B, S, N_KV_HEADS, HEAD_DIM)
    pad = jnp.zeros((B, ATTN_BLOCK, N_KV_HEADS, HEAD_DIM), k.dtype)
    kb = jnp.concatenate([pad, k], axis=1).reshape(B, nblk + 1, ATTN_BLOCK, N_KV_HEADS, HEAD_DIM)
    vb = jnp.concatenate([pad, v], axis=1).reshape(B, nblk + 1, ATTN_BLOCK, N_KV_HEADS, HEAD_DIM)
    kw = jnp.concatenate([kb[:, :-1], kb[:, 1:]], axis=2)
    vw = jnp.concatenate([vb[:, :-1], vb[:, 1:]], axis=2)
    scale = HEAD_DIM ** -0.5
    scores = jnp.einsum('bnqhgd,bnkhd->bnhgqk', q, kw).astype(jnp.float32) * scale
    qi = jnp.arange(ATTN_BLOCK)[:, None]
    ki = jnp.arange(2 * ATTN_BLOCK)[None, :]
    dist = qi + ATTN_BLOCK - ki
    in_window = (dist >= 0) & (dist < WINDOW)
    key_pos = (jnp.arange(nblk) * ATTN_BLOCK - ATTN_BLOCK)[:, None, None] + ki[None]
    valid = in_window[None] & (key_pos >= 0)
    slopes = alibi_slopes().reshape(N_KV_HEADS, Q_PER_KV)
    bias = -slopes[:, :, None, None] * dist.astype(jnp.float32)
    scores = jnp.where(valid[None, :, None, None], scores + bias, -jnp.inf)
    sink = jnp.broadcast_to(sinks.astype(jnp.float32).reshape(1, 1, N_KV_HEADS, Q_PER_KV, 1, 1),
                            scores.shape[:-1] + (1,))
    p = jax.nn.softmax(jnp.concatenate([scores, sink], axis=-1), axis=-1)[..., :-1]
    o = jnp.einsum('bnhgqk,bnkhd->bnqhgd', p.astype(vw.dtype), vw)
    o = o.reshape(B, S, N_Q_HEADS * HEAD_DIM)
    return o @ w_out


def hgrn2(h, w_in, w_out, norm_g, lb):
    B, S, _ = h.shape
    nc = S // HG_CHUNK
    proj = h @ w_in
    q, f, i, g = jnp.split(proj, [HG_FDIM, 2 * HG_FDIM, 2 * HG_FDIM + HG_IDIM], axis=-1)
    q = jax.nn.silu(q.astype(jnp.float32))
    f = f.astype(jnp.float32)
    lb = lb.astype(jnp.float32)
    log_f = jnp.logaddexp(jnp.log(lb), jnp.log1p(-lb) + jax.nn.log_sigmoid(f))
    k = (1.0 - lb) * jax.nn.sigmoid(-f)

    def to_chunks(t, d):
        return t.reshape(B, nc, HG_CHUNK, HG_HEADS, d).transpose(1, 0, 3, 2, 4)

    qc = to_chunks(q, HG_K)
    kc = to_chunks(k, HG_K)
    gc = to_chunks(log_f, HG_K)
    vc = to_chunks(i.astype(jnp.float32), HG_V)
    causal = jnp.tril(jnp.ones((HG_CHUNK, HG_CHUNK), bool))

    def step(state, inp):
        qt, kt, gt, vt = inp
        b = jnp.cumsum(gt, axis=2)
        rel = b[:, :, :, None, :] - b[:, :, None, :, :]
        decay = jnp.exp(jnp.where(causal[:, :, None], rel, -jnp.inf))
        a = jnp.einsum('bhtk,bhsk,bhtsk->bhts', qt, kt, decay)
        o = jnp.einsum('bhts,bhsv->bhtv', a, vt) + jnp.einsum('bhtk,bhkv->bhtv', qt * jnp.exp(b), state)
        b_last = b[:, :, -1:, :]
        new_state = jnp.exp(b_last[:, :, 0, :])[..., None] * state + \
            jnp.einsum('bhsk,bhsv->bhkv', kt * jnp.exp(b_last - b), vt)
        return new_state, o

    state0 = jnp.zeros((B, HG_HEADS, HG_K, HG_V), jnp.float32)
    _, o = lax.scan(step, state0, (qc, kc, gc, vc))
    o = o.transpose(1, 0, 3, 2, 4).reshape(B, S, HG_HEADS, HG_V)
    gate = g.reshape(B, S, HG_HEADS, HG_V).astype(jnp.float32)
    o = rmsnorm(o, norm_g) * jax.nn.silu(gate)
    return o.reshape(B, S, HG_IDIM).astype(h.dtype) @ w_out


def conv_ffn(h, w_up, conv_w, conv_b, w_down):
    S = h.shape[1]
    u = h @ w_up
    up = jnp.pad(u, ((0, 0), (CONV_W - 1, 0), (0, 0)))
    c = conv_b + conv_w[0] * up[:, 0:S]
    for j in range(1, CONV_W):
        c = c + conv_w[j] * up[:, j:j + S]
    gate, val = jnp.split(c, 2, axis=-1)
    return (jax.nn.silu(gate) * val) @ w_down


def _fwd_setup_inputs(seed: int = 0) -> dict:
    key = jax.random.key(seed)
    ks = jax.random.split(key, 16)
    f32 = jnp.float32
    nrm = lambda k, shape, s: jax.random.normal(k, shape, f32) * s
    return {
        'x': nrm(ks[0], (BATCH, SEQ, D_MODEL), 1.0),
        'norm_mix': 1.0 + nrm(ks[1], (DEPTH, D_MODEL), 0.02),
        'norm_ffn': 1.0 + nrm(ks[2], (DEPTH, D_MODEL), 0.02),
        'norm_final': 1.0 + nrm(ks[3], (D_MODEL,), 0.02),
        'attn_w_in': nrm(ks[4], (N_ATTN_LAYERS, D_MODEL, ATTN_IN), D_MODEL ** -0.5),
        'attn_w_out': nrm(ks[5], (N_ATTN_LAYERS, N_Q_HEADS * HEAD_DIM, D_MODEL), (N_Q_HEADS * HEAD_DIM) ** -0.5),
        'attn_sinks': nrm(ks[6], (N_ATTN_LAYERS, N_Q_HEADS), 1.0),
        'hgrn_w_in': nrm(ks[7], (N_HGRN_LAYERS, D_MODEL, HG_IN), D_MODEL ** -0.5),
        'hgrn_w_out': nrm(ks[8], (N_HGRN_LAYERS, HG_IDIM, D_MODEL), HG_IDIM ** -0.5),
        'hgrn_norm': 1.0 + nrm(ks[9], (N_HGRN_LAYERS, HG_V), 0.02),
        'hgrn_lb_logits': 1.0 + nrm(ks[10], (DEPTH, HG_FDIM), 0.1),
        'ffn_w_up': nrm(ks[11], (DEPTH, D_MODEL, 2 * D_FF), D_MODEL ** -0.5),
        'ffn_conv_w': nrm(ks[12], (DEPTH, CONV_W, 2 * D_FF), CONV_W ** -0.5),
        'ffn_conv_b': nrm(ks[13], (DEPTH, 2 * D_FF), 0.01),
        'ffn_w_down': nrm(ks[14], (DEPTH, D_FF, D_MODEL), D_FF ** -0.5),
    }


def _fwd_reference(x, norm_mix, norm_ffn, norm_final, attn_w_in, attn_w_out, attn_sinks,
              hgrn_w_in, hgrn_w_out, hgrn_norm, hgrn_lb_logits,
              ffn_w_up, ffn_conv_w, ffn_conv_b, ffn_w_down):
    s = jax.nn.softmax(hgrn_lb_logits.astype(jnp.float32), axis=0)
    lower_bounds = jnp.cumsum(s, axis=0) - s[0]
    h = x
    for layer in range(DEPTH):
        idx = layer // N_MIXERS
        hn = rmsnorm(h, norm_mix[layer])
        if layer % N_MIXERS == 0:
            h = h + sliding_window_attention(hn, attn_w_in[idx], attn_w_out[idx], attn_sinks[idx])
        else:
            h = h + hgrn2(hn, hgrn_w_in[idx], hgrn_w_out[idx], hgrn_norm[idx], lower_bounds[layer])
        h = h + conv_ffn(rmsnorm(h, norm_ffn[layer]), ffn_w_up[layer], ffn_conv_w[layer],
                         ffn_conv_b[layer], ffn_w_down[layer])
    return rmsnorm(h, norm_final)


import jax as _jax
import jax.numpy as _jnp

TWIN_FORMAT = 'train_step'
FWD_PARAMS = ['x', 'norm_mix', 'norm_ffn', 'norm_final', 'attn_w_in', 'attn_w_out', 'attn_sinks', 'hgrn_w_in', 'hgrn_w_out', 'hgrn_norm', 'hgrn_lb_logits', 'ffn_w_up', 'ffn_conv_w', 'ffn_conv_b', 'ffn_w_down']
TWIN_WEIGHTS = ['norm_mix', 'norm_ffn', 'norm_final', 'attn_w_in', 'attn_w_out', 'attn_sinks', 'hgrn_w_in', 'hgrn_w_out', 'hgrn_norm', 'hgrn_lb_logits', 'ffn_w_up', 'ffn_conv_w', 'ffn_conv_b', 'ffn_w_down']
TWIN_DIFF_INPUT = 'x'
TWIN_INPUTS = ['x', 'norm_mix', 'norm_ffn', 'norm_final', 'attn_w_in', 'attn_w_out', 'attn_sinks', 'hgrn_w_in', 'hgrn_w_out', 'hgrn_norm', 'hgrn_lb_logits', 'ffn_w_up', 'ffn_conv_w', 'ffn_conv_b', 'ffn_w_down', 'loss_target', 'm_norm_mix', 'm_norm_ffn', 'm_norm_final', 'm_attn_w_in', 'm_attn_w_out', 'm_attn_sinks', 'm_hgrn_w_in', 'm_hgrn_w_out', 'm_hgrn_norm', 'm_hgrn_lb_logits', 'm_ffn_w_up', 'm_ffn_conv_w', 'm_ffn_conv_b', 'm_ffn_w_down', 'v_norm_mix', 'v_norm_ffn', 'v_norm_final', 'v_attn_w_in', 'v_attn_w_out', 'v_attn_sinks', 'v_hgrn_w_in', 'v_hgrn_w_out', 'v_hgrn_norm', 'v_hgrn_lb_logits', 'v_ffn_w_up', 'v_ffn_conv_w', 'v_ffn_conv_b', 'v_ffn_w_down']
TWIN_OUTPUTS = ['loss', 'grad_x', 'grad_norm_mix', 'grad_norm_ffn', 'grad_norm_final', 'grad_attn_w_in', 'grad_attn_w_out', 'grad_attn_sinks', 'grad_hgrn_w_in', 'grad_hgrn_w_out', 'grad_hgrn_norm', 'grad_hgrn_lb_logits', 'grad_ffn_w_up', 'grad_ffn_conv_w', 'grad_ffn_conv_b', 'grad_ffn_w_down', 'delta_norm_mix', 'delta_norm_ffn', 'delta_norm_final', 'delta_attn_w_in', 'delta_attn_w_out', 'delta_attn_sinks', 'delta_hgrn_w_in', 'delta_hgrn_w_out', 'delta_hgrn_norm', 'delta_hgrn_lb_logits', 'delta_ffn_w_up', 'delta_ffn_conv_w', 'delta_ffn_conv_b', 'delta_ffn_w_down', 'new_m_norm_mix', 'new_m_norm_ffn', 'new_m_norm_final', 'new_m_attn_w_in', 'new_m_attn_w_out', 'new_m_attn_sinks', 'new_m_hgrn_w_in', 'new_m_hgrn_w_out', 'new_m_hgrn_norm', 'new_m_hgrn_lb_logits', 'new_m_ffn_w_up', 'new_m_ffn_conv_w', 'new_m_ffn_conv_b', 'new_m_ffn_w_down', 'new_v_norm_mix', 'new_v_norm_ffn', 'new_v_norm_final', 'new_v_attn_w_in', 'new_v_attn_w_out', 'new_v_attn_sinks', 'new_v_hgrn_w_in', 'new_v_hgrn_w_out', 'new_v_hgrn_norm', 'new_v_hgrn_lb_logits', 'new_v_ffn_w_up', 'new_v_ffn_conv_w', 'new_v_ffn_conv_b', 'new_v_ffn_w_down']
TWIN_LEAF_KINDS = {'loss': 'loss', 'grad_x': 'grad_x', 'grad_norm_mix': 'grad_w', 'grad_norm_ffn': 'grad_w', 'grad_norm_final': 'grad_w', 'grad_attn_w_in': 'grad_w', 'grad_attn_w_out': 'grad_w', 'grad_attn_sinks': 'grad_w', 'grad_hgrn_w_in': 'grad_w', 'grad_hgrn_w_out': 'grad_w', 'grad_hgrn_norm': 'grad_w', 'grad_hgrn_lb_logits': 'grad_w', 'grad_ffn_w_up': 'grad_w', 'grad_ffn_conv_w': 'grad_w', 'grad_ffn_conv_b': 'grad_w', 'grad_ffn_w_down': 'grad_w', 'delta_norm_mix': 'delta_w', 'delta_norm_ffn': 'delta_w', 'delta_norm_final': 'delta_w', 'delta_attn_w_in': 'delta_w', 'delta_attn_w_out': 'delta_w', 'delta_attn_sinks': 'delta_w', 'delta_hgrn_w_in': 'delta_w', 'delta_hgrn_w_out': 'delta_w', 'delta_hgrn_norm': 'delta_w', 'delta_hgrn_lb_logits': 'delta_w', 'delta_ffn_w_up': 'delta_w', 'delta_ffn_conv_w': 'delta_w', 'delta_ffn_conv_b': 'delta_w', 'delta_ffn_w_down': 'delta_w', 'new_m_norm_mix': 'new_m', 'new_m_norm_ffn': 'new_m', 'new_m_norm_final': 'new_m', 'new_m_attn_w_in': 'new_m', 'new_m_attn_w_out': 'new_m', 'new_m_attn_sinks': 'new_m', 'new_m_hgrn_w_in': 'new_m', 'new_m_hgrn_w_out': 'new_m', 'new_m_hgrn_norm': 'new_m', 'new_m_hgrn_lb_logits': 'new_m', 'new_m_ffn_w_up': 'new_m', 'new_m_ffn_conv_w': 'new_m', 'new_m_ffn_conv_b': 'new_m', 'new_m_ffn_w_down': 'new_m', 'new_v_norm_mix': 'new_v', 'new_v_norm_ffn': 'new_v', 'new_v_norm_final': 'new_v', 'new_v_attn_w_in': 'new_v', 'new_v_attn_w_out': 'new_v', 'new_v_attn_sinks': 'new_v', 'new_v_hgrn_w_in': 'new_v', 'new_v_hgrn_w_out': 'new_v', 'new_v_hgrn_norm': 'new_v', 'new_v_hgrn_lb_logits': 'new_v', 'new_v_ffn_w_up': 'new_v', 'new_v_ffn_conv_w': 'new_v', 'new_v_ffn_conv_b': 'new_v', 'new_v_ffn_w_down': 'new_v'}


def _forward(args):
    return _fwd_reference(*[args[k] for k in FWD_PARAMS])


def _output_shape():
    out = _jax.eval_shape(lambda: _forward(_fwd_setup_inputs(0)))
    return out.shape, out.dtype

N_MICROBATCH = 1
ADAM_LR = 0.001
ADAM_B1 = 0.9
ADAM_B2 = 0.999
ADAM_EPS = 1e-08
ADAM_WD = 0.01
ADAM_STEP = 10
PER_EXAMPLE_BATCH_AXIS = {'x': 0, 'loss_target': 0}
SHARED_INPUTS = []
_WEIGHT_DTYPES = {'norm_mix': _jnp.float32, 'norm_ffn': _jnp.float32, 'norm_final': _jnp.float32, 'attn_w_in': _jnp.float32, 'attn_w_out': _jnp.float32, 'attn_sinks': _jnp.float32, 'hgrn_w_in': _jnp.float32, 'hgrn_w_out': _jnp.float32, 'hgrn_norm': _jnp.float32, 'hgrn_lb_logits': _jnp.float32, 'ffn_w_up': _jnp.float32, 'ffn_conv_w': _jnp.float32, 'ffn_conv_b': _jnp.float32, 'ffn_w_down': _jnp.float32}
MOMENT_SCALE = {'norm_mix': 1.153035e-01, 'norm_ffn': 1.335980e-01, 'norm_final': 3.200964e+01, 'attn_w_in': 8.392627e-02, 'attn_w_out': 7.322207e-02, 'attn_sinks': 8.893869e-02, 'hgrn_w_in': 6.296379e-02, 'hgrn_w_out': 8.745687e-02, 'hgrn_norm': 2.891961e-01, 'hgrn_lb_logits': 5.397449e-03, 'ffn_w_up': 5.711338e-02, 'ffn_conv_w': 5.743518e-02, 'ffn_conv_b': 5.862525e-02, 'ffn_w_down': 9.347344e-02}


def _to_microbatches(a, axis):
    t = _jnp.moveaxis(a, axis, 0)
    t = t.reshape((N_MICROBATCH, t.shape[0] // N_MICROBATCH) + t.shape[1:])
    return _jnp.moveaxis(t, 1, axis + 1)


def setup_inputs(seed: int = 0) -> dict:
    inp = _fwd_setup_inputs(seed)
    key = _jax.random.fold_in(_jax.random.key(seed), 7919)
    shape, _ = _output_shape()
    out = dict(inp)
    out["loss_target"] = _jax.random.normal(_jax.random.fold_in(key, 0), shape, _jnp.float32)
    for i, name in enumerate(TWIN_WEIGHTS):
        w = inp[name].astype(_jnp.float32)
        if MOMENT_SCALE is None:
            s = _jnp.sqrt(_jnp.mean(_jnp.square(w)) + 1e-30)
        else:
            s = MOMENT_SCALE[name]
        km, kv = _jax.random.split(_jax.random.fold_in(key, i + 1))
        out[name] = w
        out["m_" + name] = s * _jax.random.normal(km, w.shape, _jnp.float32)
        out["v_" + name] = (s * s) * _jax.random.uniform(kv, w.shape, _jnp.float32, 0.5, 1.5)
    if N_MICROBATCH > 1:
        for name, axis in PER_EXAMPLE_BATCH_AXIS.items():
            out[name] = _to_microbatches(out[name], axis)
    return {'x': out['x'], 'norm_mix': out['norm_mix'], 'norm_ffn': out['norm_ffn'], 'norm_final': out['norm_final'], 'attn_w_in': out['attn_w_in'], 'attn_w_out': out['attn_w_out'], 'attn_sinks': out['attn_sinks'], 'hgrn_w_in': out['hgrn_w_in'], 'hgrn_w_out': out['hgrn_w_out'], 'hgrn_norm': out['hgrn_norm'], 'hgrn_lb_logits': out['hgrn_lb_logits'], 'ffn_w_up': out['ffn_w_up'], 'ffn_conv_w': out['ffn_conv_w'], 'ffn_conv_b': out['ffn_conv_b'], 'ffn_w_down': out['ffn_w_down'], 'loss_target': out['loss_target'], 'm_norm_mix': out['m_norm_mix'], 'm_norm_ffn': out['m_norm_ffn'], 'm_norm_final': out['m_norm_final'], 'm_attn_w_in': out['m_attn_w_in'], 'm_attn_w_out': out['m_attn_w_out'], 'm_attn_sinks': out['m_attn_sinks'], 'm_hgrn_w_in': out['m_hgrn_w_in'], 'm_hgrn_w_out': out['m_hgrn_w_out'], 'm_hgrn_norm': out['m_hgrn_norm'], 'm_hgrn_lb_logits': out['m_hgrn_lb_logits'], 'm_ffn_w_up': out['m_ffn_w_up'], 'm_ffn_conv_w': out['m_ffn_conv_w'], 'm_ffn_conv_b': out['m_ffn_conv_b'], 'm_ffn_w_down': out['m_ffn_w_down'], 'v_norm_mix': out['v_norm_mix'], 'v_norm_ffn': out['v_norm_ffn'], 'v_norm_final': out['v_norm_final'], 'v_attn_w_in': out['v_attn_w_in'], 'v_attn_w_out': out['v_attn_w_out'], 'v_attn_sinks': out['v_attn_sinks'], 'v_hgrn_w_in': out['v_hgrn_w_in'], 'v_hgrn_w_out': out['v_hgrn_w_out'], 'v_hgrn_norm': out['v_hgrn_norm'], 'v_hgrn_lb_logits': out['v_hgrn_lb_logits'], 'v_ffn_w_up': out['v_ffn_w_up'], 'v_ffn_conv_w': out['v_ffn_conv_w'], 'v_ffn_conv_b': out['v_ffn_conv_b'], 'v_ffn_w_down': out['v_ffn_w_down']}


def _loss(weights, diff, rest, loss_target):
    with _jax.named_scope("forward"):
        args = {**rest, TWIN_DIFF_INPUT: diff, **{k: w.astype(_WEIGHT_DTYPES[k]) for k, w in weights.items()}}
        y = _forward(args)
    with _jax.named_scope("loss_head"):
        err = _jnp.square(y.astype(_jnp.float32) - loss_target)
        return 0.5 * _jnp.sum(_jnp.mean(err, axis=-1)) if err.ndim else 0.5 * err


def _adamw(w, g, m, v):
    m = ADAM_B1 * m + (1.0 - ADAM_B1) * g
    v = ADAM_B2 * v + (1.0 - ADAM_B2) * _jnp.square(g)
    m_hat = m / (1.0 - ADAM_B1 ** ADAM_STEP)
    v_hat = v / (1.0 - ADAM_B2 ** ADAM_STEP)
    delta = -ADAM_LR * (m_hat / (_jnp.sqrt(v_hat) + ADAM_EPS) + ADAM_WD * w)
    return delta, m, v


def reference(x, norm_mix, norm_ffn, norm_final, attn_w_in, attn_w_out, attn_sinks, hgrn_w_in, hgrn_w_out, hgrn_norm, hgrn_lb_logits, ffn_w_up, ffn_conv_w, ffn_conv_b, ffn_w_down, loss_target, m_norm_mix, m_norm_ffn, m_norm_final, m_attn_w_in, m_attn_w_out, m_attn_sinks, m_hgrn_w_in, m_hgrn_w_out, m_hgrn_norm, m_hgrn_lb_logits, m_ffn_w_up, m_ffn_conv_w, m_ffn_conv_b, m_ffn_w_down, v_norm_mix, v_norm_ffn, v_norm_final, v_attn_w_in, v_attn_w_out, v_attn_sinks, v_hgrn_w_in, v_hgrn_w_out, v_hgrn_norm, v_hgrn_lb_logits, v_ffn_w_up, v_ffn_conv_w, v_ffn_conv_b, v_ffn_w_down):
    given = dict(x=x, norm_mix=norm_mix, norm_ffn=norm_ffn, norm_final=norm_final, attn_w_in=attn_w_in, attn_w_out=attn_w_out, attn_sinks=attn_sinks, hgrn_w_in=hgrn_w_in, hgrn_w_out=hgrn_w_out, hgrn_norm=hgrn_norm, hgrn_lb_logits=hgrn_lb_logits, ffn_w_up=ffn_w_up, ffn_conv_w=ffn_conv_w, ffn_conv_b=ffn_conv_b, ffn_w_down=ffn_w_down, loss_target=loss_target, m_norm_mix=m_norm_mix, m_norm_ffn=m_norm_ffn, m_norm_final=m_norm_final, m_attn_w_in=m_attn_w_in, m_attn_w_out=m_attn_w_out, m_attn_sinks=m_attn_sinks, m_hgrn_w_in=m_hgrn_w_in, m_hgrn_w_out=m_hgrn_w_out, m_hgrn_norm=m_hgrn_norm, m_hgrn_lb_logits=m_hgrn_lb_logits, m_ffn_w_up=m_ffn_w_up, m_ffn_conv_w=m_ffn_conv_w, m_ffn_conv_b=m_ffn_conv_b, m_ffn_w_down=m_ffn_w_down, v_norm_mix=v_norm_mix, v_norm_ffn=v_norm_ffn, v_norm_final=v_norm_final, v_attn_w_in=v_attn_w_in, v_attn_w_out=v_attn_w_out, v_attn_sinks=v_attn_sinks, v_hgrn_w_in=v_hgrn_w_in, v_hgrn_w_out=v_hgrn_w_out, v_hgrn_norm=v_hgrn_norm, v_hgrn_lb_logits=v_hgrn_lb_logits, v_ffn_w_up=v_ffn_w_up, v_ffn_conv_w=v_ffn_conv_w, v_ffn_conv_b=v_ffn_conv_b, v_ffn_w_down=v_ffn_w_down)
    weights = {n: given[n] for n in TWIN_WEIGHTS}
    shared = {n: given[n] for n in SHARED_INPUTS}
    per_example = {n: given[n] for n in ['x']}
    grad_fn = _jax.value_and_grad(_loss, argnums=(0, 1))

    def one_microbatch(ex, loss_target):
        ex = dict(ex)
        diff = ex.pop(TWIN_DIFF_INPUT)
        return grad_fn(weights, diff, {**shared, **ex}, loss_target)

    if N_MICROBATCH == 1:
        loss, (grad_w, grad_x) = one_microbatch(per_example, given["loss_target"])
    else:
        def body(carry, xs):
            loss_sum, grad_sum = carry
            l_k, (gw_k, gx_k) = one_microbatch(xs[0], xs[1])
            with _jax.named_scope("update"):
                return (loss_sum + l_k, _jax.tree.map(_jnp.add, grad_sum, gw_k)), gx_k

        init = (_jnp.zeros((), _jnp.float32), _jax.tree.map(_jnp.zeros_like, weights))
        (loss, grad_w), grad_x = _jax.lax.scan(body, init, (per_example, given["loss_target"]))
    with _jax.named_scope("update"):
        delta_w, new_m, new_v = {}, {}, {}
        for n in TWIN_WEIGHTS:
            delta_w[n], new_m[n], new_v[n] = _adamw(weights[n], grad_w[n], given["m_" + n], given["v_" + n])
    return (loss, grad_x, *[grad_w[n] for n in TWIN_WEIGHTS], *[delta_w[n] for n in TWIN_WEIGHTS],
            *[new_m[n] for n in TWIN_WEIGHTS], *[new_v[n] for n in TWIN_WEIGHTS])
```

```python
import functools
import math

import numpy as np
import jax
import jax.numpy as jnp
from jax import lax
from jax.experimental import pallas as pl
from jax.experimental.pallas import tpu as pltpu

F32 = jnp.float32
BF16 = jnp.bfloat16
MESH = pl.DeviceIdType.MESH

HEAD_DIM = 64
Q_PER_KV = 4
WINDOW = 128
HG_EXPAND = 128
HG_CHUNK = 64
EPS = 1e-6
N_CHIPS = 4
N_DEV = 8

ADAM_LR = 0.001
ADAM_B1 = 0.9
ADAM_B2 = 0.999
ADAM_EPS = 1e-08
ADAM_WD = 0.01
ADAM_STEP = 10

VMEM_LIMIT_BYTES = 48 * 1024 * 1024
LANES = 128
SUBLANES = 8


def _params(sem=None):
    return pltpu.CompilerParams(dimension_semantics=sem, vmem_limit_bytes=VMEM_LIMIT_BYTES)


def _pick(n, cands):
    for c in cands:
        if n % c == 0:
            return c
    return n


def _dot(a, b, dims):
    return lax.dot_general(a.astype(BF16), b.astype(BF16), (dims, ((), ())),
                           preferred_element_type=F32)


NN = ((1,), (0,))
NT = ((1,), (1,))
TN = ((0,), (0,))


def mm_nn(a, w, l, *, name, res=None, out_dtype=F32, split_out=False):
    M, K = a.shape
    N = w.shape[2]
    tm = _pick(M, (512, 256, 128))
    tn = _pick(N // 2 if split_out else N, (1408, 1024, 768, 512, 256, 128))
    tk = _pick(K, (1024, 1408, 512, 256, 128))
    nk = K // tk
    nh = (N // 2) // tn

    def body(*refs):
        if res is None:
            a_ref, w_ref, o_ref = refs[:3]
            r_ref = None
        else:
            a_ref, w_ref, r_ref, o_ref = refs[:4]
        part = _dot(a_ref[...], w_ref[...], NN)

        def finish(acc):
            if r_ref is not None:
                acc = acc + r_ref[...]
            o_ref[...] = acc.astype(out_dtype)

        if nk == 1:
            finish(part)
        else:
            acc_ref = refs[-1]
            k = pl.program_id(2)

            @pl.when(k == 0)
            def _():
                acc_ref[...] = part

            @pl.when(k > 0)
            def _():
                acc_ref[...] += part

            @pl.when(k == nk - 1)
            def _():
                finish(acc_ref[...])

    in_specs = [pl.BlockSpec((tm, tk), lambda n, m, k: (m, k)),
                pl.BlockSpec((None, tk, tn), lambda n, m, k: (l, k, n))]
    args = [a, w]
    if res is not None:
        in_specs.append(pl.BlockSpec((tm, tn), lambda n, m, k: (m, n)))
        args.append(res)
    if split_out:
        out_shape = jax.ShapeDtypeStruct((2, M, N // 2), out_dtype)
        out_spec = pl.BlockSpec((None, tm, tn), lambda n, m, k: (n // nh, m, n % nh))
    else:
        out_shape = jax.ShapeDtypeStruct((M, N), out_dtype)
        out_spec = pl.BlockSpec((tm, tn), lambda n, m, k: (m, n))
    return pl.pallas_call(
        body, name=name, grid=(N // tn, M // tm, nk), in_specs=in_specs, out_specs=out_spec,
        out_shape=out_shape,
        scratch_shapes=[] if nk == 1 else [pltpu.VMEM((tm, tn), F32)],
        compiler_params=_params(("parallel", "parallel", "arbitrary")),
    )(*args)


def mm_nt(a, w, l, *, name, split_in=False, out_dtype=F32):
    if split_in:
        _, M, Nh = a.shape
        N = 2 * Nh
    else:
        M, N = a.shape
        Nh = N
    K = w.shape[1]
    tm = _pick(M, (512, 256, 128))
    tko = _pick(K, (1024, 1408, 512, 256, 128))
    tr = _pick(Nh, (1024, 1408, 768, 512, 256, 128))
    nr = N // tr
    nh = Nh // tr

    def body(a_ref, w_ref, o_ref, *scratch):
        part = _dot(a_ref[...], w_ref[...], NT)
        if nr == 1:
            o_ref[...] = part.astype(out_dtype)
        else:
            acc_ref = scratch[0]
            r = pl.program_id(2)

            @pl.when(r == 0)
            def _():
                acc_ref[...] = part

            @pl.when(r > 0)
            def _():
                acc_ref[...] += part

            @pl.when(r == nr - 1)
            def _():
                o_ref[...] = acc_ref[...].astype(out_dtype)

    if split_in:
        a_spec = pl.BlockSpec((None, tm, tr), lambda ko, m, r: (r // nh, m, r % nh))
    else:
        a_spec = pl.BlockSpec((tm, tr), lambda ko, m, r: (m, r))
    return pl.pallas_call(
        body, name=name, grid=(K // tko, M // tm, nr),
        in_specs=[a_spec, pl.BlockSpec((None, tko, tr), lambda ko, m, r: (l, ko, r))],
        out_specs=pl.BlockSpec((tm, tko), lambda ko, m, r: (m, ko)),
        out_shape=jax.ShapeDtypeStruct((M, K), out_dtype),
        scratch_shapes=[] if nr == 1 else [pltpu.VMEM((tm, tko), F32)],
        compiler_params=_params(("parallel", "parallel", "arbitrary")),
    )(a, w)


def mm_tn(a, b, buf, l, n_layers, *, name, split_b=False):
    M, K = a.shape
    if split_b:
        Nh = b.shape[2]
        N = 2 * Nh
    else:
        N = b.shape[1]
        Nh = N
    tm = _pick(M, (512, 256, 128))
    tko = _pick(K, (1024, 1408, 704, 512, 256, 128))
    tn = _pick(Nh, (512, 256, 128))
    nm = M // tm
    nh = Nh // tn

    def body(*refs):
        a_ref, b_ref = refs[:2]
        o_ref = refs[-1]
        part = _dot(a_ref[...], b_ref[...], TN)
        r = pl.program_id(2)

        @pl.when(r == 0)
        def _():
            o_ref[...] = part

        @pl.when(r > 0)
        def _():
            o_ref[...] += part

    if split_b:
        b_spec = pl.BlockSpec((None, tm, tn), lambda ko, n, r: (n // nh, r, n % nh))
    else:
        b_spec = pl.BlockSpec((tm, tn), lambda ko, n, r: (r, n))
    in_specs = [pl.BlockSpec((tm, tko), lambda ko, n, r: (r, ko)), b_spec]
    args = [a, b]
    aliases = {}
    if buf is not None:
        in_specs.append(pl.BlockSpec(memory_space=pl.ANY))
        args.append(buf)
        aliases = {2: 0}
    return pl.pallas_call(
        body, name=name, grid=(K // tko, N // tn, nm), in_specs=in_specs,
        out_specs=pl.BlockSpec((None, tko, tn), lambda ko, n, r: (l, ko, n)),
        out_shape=jax.ShapeDtypeStruct((n_layers, K, N), F32),
        input_output_aliases=aliases,
        compiler_params=_params(("parallel", "parallel", "arbitrary")),
    )(*args)


def _rms(x, g):
    return x * lax.rsqrt(jnp.mean(x * x, axis=-1, keepdims=True) + EPS) * g


def rmsnorm_fwd(h, g, l, *, name):
    S, D = h.shape
    tm = _pick(S, (512, 256, 128))

    def body(h_ref, g_ref, o_ref):
        o_ref[...] = _rms(h_ref[...], g_ref[...]).astype(BF16)

    return pl.pallas_call(
        body, name=name, grid=(S // tm,),
        in_specs=[pl.BlockSpec((tm, D), lambda m: (m, 0)),
                  pl.BlockSpec((None, 1, D), lambda m: (l, 0, 0))],
        out_specs=pl.BlockSpec((tm, D), lambda m: (m, 0)),
        out_shape=jax.ShapeDtypeStruct((S, D), BF16),
        compiler_params=_params(("parallel",)),
    )(h, g)


def rmsnorm_bwd(h, g, l, dhn, dres, *, name):
    S, D = h.shape
    tm = _pick(S, (512, 256, 128))

    def body(h_ref, g_ref, dhn_ref, dres_ref, dh_ref, dg_ref):
        _, vjp = jax.vjp(_rms, h_ref[...], g_ref[...])
        dx, dg = vjp(dhn_ref[...])
        dh_ref[...] = dres_ref[...] + dx

        @pl.when(pl.program_id(0) == 0)
        def _():
            dg_ref[...] = dg

        @pl.when(pl.program_id(0) > 0)
        def _():
            dg_ref[...] += dg

    row = pl.BlockSpec((tm, D), lambda m: (m, 0))
    return pl.pallas_call(
        body, name=name, grid=(S // tm,),
        in_specs=[row, pl.BlockSpec((None, 1, D), lambda m: (l, 0, 0)), row, row],
        out_specs=[row, pl.BlockSpec((1, D), lambda m: (0, 0))],
        out_shape=[jax.ShapeDtypeStruct((S, D), F32), jax.ShapeDtypeStruct((1, D), F32)],
        compiler_params=_params(("arbitrary",)),
    )(h, g, dhn, dres)


def loss_head(h, g, target, *, name):
    S, D = h.shape
    tm = _pick(S, (512, 256, 128))

    def body(h_ref, g_ref, t_ref, loss_ref, dh_ref, dg_ref):
        y, vjp = jax.vjp(_rms, h_ref[...], g_ref[...])
        err = y - t_ref[...]
        part = 0.5 * jnp.sum(jnp.sum(err * err, axis=-1, keepdims=True) / D, axis=0, keepdims=True)
        dx, dg = vjp(err / D)
        dh_ref[...] = dx

        @pl.when(pl.program_id(0) == 0)
        def _():
            dg_ref[...] = dg
            loss_ref[...] = part

        @pl.when(pl.program_id(0) > 0)
        def _():
            dg_ref[...] += dg
            loss_ref[...] += part

    row = pl.BlockSpec((tm, D), lambda m: (m, 0))
    return pl.pallas_call(
        body, name=name, grid=(S // tm,),
        in_specs=[row, pl.BlockSpec((1, D), lambda m: (0, 0)), row],
        out_specs=[pl.BlockSpec((1, 1), lambda m: (0, 0)), row, pl.BlockSpec((1, D), lambda m: (0, 0))],
        out_shape=[jax.ShapeDtypeStruct((1, 1), F32), jax.ShapeDtypeStruct((S, D), F32),
                   jax.ShapeDtypeStruct((1, D), F32)],
        compiler_params=_params(("arbitrary",)),
    )(h, g, target)


def _make_mm(dims_fwd, dims_da, dims_db, swap_da=False, swap_db=False):
    @jax.custom_vjp
    def mm(a, b):
        return _dot(a, b, dims_fwd)

    def fwd(a, b):
        return mm(a, b), (a, b)

    def bwd(resid, g):
        a, b = resid
        da = _dot(b, g, dims_da) if swap_da else _dot(g, b, dims_da)
        db = _dot(g, a, dims_db) if swap_db else _dot(a, g, dims_db)
        return da, db

    mm.defvjp(fwd, bwd)
    return mm


_mm = _make_mm(NN, NT, TN)
_mm_nt = _make_mm(NT, NN, TN, swap_db=True)
_mm_tn = _make_mm(TN, NT, NN, swap_da=True)


def _dot_hi(a, b):
    return jnp.dot(a, b, precision=lax.Precision.HIGHEST, preferred_element_type=F32)


def _sigmoid(x):
    return 1.0 / (1.0 + jnp.exp(-x))


def _silu(x):
    return x * _sigmoid(x)


def _attn_block(q, kw, vw, sink, slope, first):
    rows = q.shape[0]
    s = _mm_nt(q, kw) * (HEAD_DIM ** -0.5)
    qi = lax.broadcasted_iota(jnp.int32, (rows, 2 * WINDOW), 0) % WINDOW
    ki = lax.broadcasted_iota(jnp.int32, (rows, 2 * WINDOW), 1)
    dist = qi + WINDOW - ki
    valid = (dist >= 0) & (dist < WINDOW) & (ki >= WINDOW * first)
    s = jnp.where(valid, s - slope * dist.astype(F32), -jnp.inf)
    m = lax.stop_gradient(jnp.maximum(jnp.max(s, axis=-1, keepdims=True), sink))
    e = jnp.exp(s - m)
    denom = jnp.sum(e, axis=-1, keepdims=True) + jnp.exp(sink - m)
    return _mm(e / denom, vw)


def _attn_specs():
    G = Q_PER_KV
    qspec = pl.BlockSpec((None, G, WINDOW, HEAD_DIM), lambda h, n: (h, 0, n, 0))
    prev = pl.BlockSpec((None, WINDOW, HEAD_DIM), lambda h, n: (h, jnp.maximum(n - 1, 0), 0))
    cur = pl.BlockSpec((None, WINDOW, HEAD_DIM), lambda h, n: (h, n, 0))
    col = pl.BlockSpec((None, G * WINDOW, 1), lambda h, n: (h, 0, 0))
    return qspec, prev, cur, col


def attn_fwd(q, k, v, sink_col, slope_col, *, name):
    Hkv, G, S, _ = q.shape
    qspec, prev, cur, col = _attn_specs()

    def body(q_ref, kp_ref, kc_ref, vp_ref, vc_ref, sink_ref, slope_ref, o_ref):
        first = (pl.program_id(1) == 0).astype(jnp.int32)
        kw = jnp.concatenate([kp_ref[...], kc_ref[...]], axis=0)
        vw = jnp.concatenate([vp_ref[...], vc_ref[...]], axis=0)
        o = _attn_block(q_ref[...].reshape(G * WINDOW, HEAD_DIM), kw, vw, sink_ref[...],
                        slope_ref[...], first)
        o_ref[...] = o.reshape(G, WINDOW, HEAD_DIM).astype(BF16)

    return pl.pallas_call(
        body, name=name, grid=(Hkv, S // WINDOW),
        in_specs=[qspec, prev, cur, prev, cur, col, col], out_specs=qspec,
        out_shape=jax.ShapeDtypeStruct(q.shape, BF16),
        compiler_params=_params(("parallel", "parallel")),
    )(q, k, k, v, v, sink_col, slope_col)


def attn_bwd(q, k, v, sink_col, slope_col, do, *, name):
    Hkv, G, S, _ = q.shape
    qspec, prev, cur, col = _attn_specs()

    def body(q_ref, kp_ref, kc_ref, vp_ref, vc_ref, sink_ref, slope_ref, do_ref,
             dq_ref, dkc_ref, dkp_ref, dvc_ref, dvp_ref, dsink_ref):
        is_first = pl.program_id(1) == 0
        first = is_first.astype(jnp.int32)
        kw = jnp.concatenate([kp_ref[...], kc_ref[...]], axis=0)
        vw = jnp.concatenate([vp_ref[...], vc_ref[...]], axis=0)
        fn = functools.partial(_attn_block, slope=slope_ref[...], first=first)
        _, vjp = jax.vjp(fn, q_ref[...].reshape(G * WINDOW, HEAD_DIM), kw, vw, sink_ref[...])
        dq, dkw, dvw, dsink = vjp(do_ref[...].reshape(G * WINDOW, HEAD_DIM).astype(F32))
        dq_ref[...] = dq.reshape(G, WINDOW, HEAD_DIM)
        dkp_ref[...] = dkw[:WINDOW]
        dkc_ref[...] = dkw[WINDOW:]
        dvp_ref[...] = dvw[:WINDOW]
        dvc_ref[...] = dvw[WINDOW:]

        @pl.when(is_first)
        def _():
            dsink_ref[...] = dsink

        @pl.when(jnp.logical_not(is_first))
        def _():
            dsink_ref[...] += dsink

    kv_shape = jax.ShapeDtypeStruct(k.shape, F32)
    return pl.pallas_call(
        body, name=name, grid=(Hkv, S // WINDOW),
        in_specs=[qspec, prev, cur, prev, cur, col, col, qspec],
        out_specs=[qspec, cur, cur, cur, cur, col],
        out_shape=[jax.ShapeDtypeStruct(q.shape, F32), kv_shape, kv_shape, kv_shape, kv_shape,
                   jax.ShapeDtypeStruct(sink_col.shape, F32)],
        compiler_params=_params(("parallel", "arbitrary")),
    )(q, k, k, v, v, sink_col, slope_col, do)


def _hgrn_consts():
    C = HG_CHUNK
    t = np.arange(C)[:, None]
    j = np.arange(C)[None, :]
    cq, ck, mk = [], [], []
    H = C // 2
    while H >= 1:
        start = (t // H) * H
        cq.append((j > start) & (j <= t))
        ck.append((j > t) & (j <= start + H))
        mk.append(((t // H) % 2 == 1) & ((j // H) == (t // H) - 1))
        H //= 2
    ltri = (j <= t)
    f = lambda xs: jnp.asarray(np.stack(xs).astype(np.float32))
    return f(cq), f(ck), f(mk), jnp.asarray(ltri.astype(np.float32))


def _hgrn_head(qr, fr, iv, gr, st, logits, ng, cq, ck, mk, ltri, layer):
    n_levels = cq.shape[0]
    depth = logits.shape[0]
    e = jnp.exp(logits - lax.stop_gradient(jnp.max(logits, axis=0, keepdims=True)))
    sm = e / jnp.sum(e, axis=0, keepdims=True)
    row = lax.broadcasted_iota(jnp.int32, (depth, HG_EXPAND), 0)
    lb = jnp.sum(jnp.where((row >= 1) & (row <= layer), sm, 0.0), axis=0, keepdims=True)

    q = _silu(qr)
    a = jnp.log(lb)
    c = jnp.log(1.0 - lb) - (jnp.maximum(-fr, 0.0) + jnp.log(1.0 + jnp.exp(-jnp.abs(fr))))
    mx = lax.stop_gradient(jnp.maximum(a, c))
    g = mx + jnp.log(jnp.exp(a - mx) + jnp.exp(c - mx))
    k = (1.0 - lb) * _sigmoid(-fr)

    b = _dot_hi(ltri, g)
    b_last = jnp.sum(g, axis=0, keepdims=True)

    o = jnp.sum(q * k, axis=-1, keepdims=True) * iv
    amat = None
    for lv in range(n_levels):
        eq = jnp.exp(_dot_hi(cq[lv], g))
        ek = jnp.exp(_dot_hi(ck[lv], g))
        part = _mm_nt(q * eq, k * ek) * mk[lv]
        amat = part if amat is None else amat + part
    o = o + _mm(amat, iv) + _mm_nt(q * jnp.exp(b), st)
    st_new = st * jnp.exp(b_last) + _mm_tn(iv, k * jnp.exp(b_last - b))

    on = o * lax.rsqrt(jnp.mean(o * o, axis=-1, keepdims=True) + EPS) * ng
    return on * _silu(gr), st_new


def _hgrn_in_specs(D, depth, order):
    C = HG_CHUNK
    nl = int(math.log2(C))
    return [pl.BlockSpec((C, 4 * D), lambda c: (order(c), 0)),
            pl.BlockSpec((depth, D), lambda c: (0, 0)),
            pl.BlockSpec((1, HG_EXPAND), lambda c: (0, 0)),
            pl.BlockSpec((nl, C, C), lambda c: (0, 0, 0)),
            pl.BlockSpec((nl, C, C), lambda c: (0, 0, 0)),
            pl.BlockSpec((nl, C, C), lambda c: (0, 0, 0)),
            pl.BlockSpec((C, C), lambda c: (0, 0))]


def _hgrn_cols(D, h):
    return [slice(j * D + h * HG_EXPAND, j * D + (h + 1) * HG_EXPAND) for j in range(4)]


def hgrn_fwd(proj, logits, ng, layer, *, name):
    S = proj.shape[0]
    D = proj.shape[1] // 4
    H = D // HG_EXPAND
    C = HG_CHUNK
    nc = S // C
    consts = _hgrn_consts()

    def body(p_ref, lg_ref, ng_ref, cq_ref, ck_ref, mk_ref, lt_ref, o_ref, st_out_ref, st_ref):
        @pl.when(pl.program_id(0) == 0)
        def _():
            st_ref[...] = jnp.zeros_like(st_ref)

        st_out_ref[...] = st_ref[...]
        cq, ck, mk, ltri = cq_ref[...], ck_ref[...], mk_ref[...], lt_ref[...]
        for h in range(H):
            sl = slice(h * HG_EXPAND, (h + 1) * HG_EXPAND)
            cols = _hgrn_cols(D, h)
            out, st_new = _hgrn_head(p_ref[:, cols[0]], p_ref[:, cols[1]], p_ref[:, cols[2]], p_ref[:, cols[3]],
                                     st_ref[h], lg_ref[:, sl], ng_ref[...], cq, ck, mk, ltri, layer)
            o_ref[:, sl] = out.astype(BF16)
            st_ref[h] = st_new

    return pl.pallas_call(
        body, name=name, grid=(nc,),
        in_specs=_hgrn_in_specs(D, logits.shape[0], lambda c: c),
        out_specs=[pl.BlockSpec((C, D), lambda c: (c, 0)),
                   pl.BlockSpec((None, H, HG_EXPAND, HG_EXPAND), lambda c: (c, 0, 0, 0))],
        out_shape=[jax.ShapeDtypeStruct((S, D), BF16),
                   jax.ShapeDtypeStruct((nc, H, HG_EXPAND, HG_EXPAND), F32)],
        scratch_shapes=[pltpu.VMEM((H, HG_EXPAND, HG_EXPAND), F32)],
        compiler_params=_params(("arbitrary",)),
    )(proj, logits, ng, *consts)


def hgrn_bwd(proj, logits, ng, states, dout, layer, *, name):
    S = proj.shape[0]
    D = proj.shape[1] // 4
    H = D // HG_EXPAND
    C = HG_CHUNK
    nc = S // C
    depth = logits.shape[0]
    consts = _hgrn_consts()
    rev = lambda c: nc - 1 - c

    def body(p_ref, lg_ref, ng_ref, cq_ref, ck_ref, mk_ref, lt_ref, st_in_ref, do_ref,
             dp_ref, dlg_ref, dng_ref, dst_ref):
        @pl.when(pl.program_id(0) == 0)
        def _():
            dst_ref[...] = jnp.zeros_like(dst_ref)
            dlg_ref[...] = jnp.zeros_like(dlg_ref)
            dng_ref[...] = jnp.zeros_like(dng_ref)

        cq, ck, mk, ltri = cq_ref[...], ck_ref[...], mk_ref[...], lt_ref[...]
        for h in range(H):
            sl = slice(h * HG_EXPAND, (h + 1) * HG_EXPAND)
            cols = _hgrn_cols(D, h)
            fn = functools.partial(_hgrn_head, cq=cq, ck=ck, mk=mk, ltri=ltri, layer=layer)
            _, vjp = jax.vjp(fn, p_ref[:, cols[0]], p_ref[:, cols[1]], p_ref[:, cols[2]], p_ref[:, cols[3]],
                             st_in_ref[h], lg_ref[:, sl], ng_ref[...])
            dq, df, di, dg, dst, dlg, dng = vjp((do_ref[:, sl], dst_ref[h]))
            dp_ref[:, cols[0]] = dq
            dp_ref[:, cols[1]] = df
            dp_ref[:, cols[2]] = di
            dp_ref[:, cols[3]] = dg
            dst_ref[h] = dst
            dlg_ref[:, sl] += dlg
            dng_ref[...] += dng

    in_specs = _hgrn_in_specs(D, depth, rev)
    in_specs += [pl.BlockSpec((None, H, HG_EXPAND, HG_EXPAND), lambda c: (rev(c), 0, 0, 0)),
                 pl.BlockSpec((C, D), lambda c: (rev(c), 0))]
    return pl.pallas_call(
        body, name=name, grid=(nc,), in_specs=in_specs,
        out_specs=[pl.BlockSpec((C, 4 * D), lambda c: (rev(c), 0)),
                   pl.BlockSpec((depth, D), lambda c: (0, 0)), pl.BlockSpec((1, HG_EXPAND), lambda c: (0, 0))],
        out_shape=[jax.ShapeDtypeStruct((S, 4 * D), F32), jax.ShapeDtypeStruct((depth, D), F32),
                   jax.ShapeDtypeStruct((1, HG_EXPAND), F32)],
        scratch_shapes=[pltpu.VMEM((H, HG_EXPAND, HG_EXPAND), F32)],
        compiler_params=_params(("arbitrary",)),
    )(proj, logits, ng, *consts, states, dout)


def _edge_rows(edge, shape):
    return jnp.broadcast_to(edge[None], (shape[0] // SUBLANES, SUBLANES, shape[1])).reshape(shape)


def _shift_down(u, prev, shift):
    rolled = pltpu.roll(u, shift, axis=0)
    edge = _edge_rows(pltpu.roll(prev, shift, axis=0), u.shape)
    rows = lax.broadcasted_iota(jnp.int32, u.shape, 0)
    return jnp.where(rows < shift, edge, rolled)


def _shift_up(x, nxt, shift):
    tm = x.shape[0]
    rolled = pltpu.roll(x, tm - shift, axis=0)
    edge = _edge_rows(pltpu.roll(nxt, SUBLANES - shift, axis=0), x.shape)
    rows = lax.broadcasted_iota(jnp.int32, x.shape, 0)
    return jnp.where(rows >= tm - shift, edge, rolled)


def _conv(u, prev, w_ref, b_ref, half):
    u1 = _shift_down(u, prev, 1)
    u2 = _shift_down(u, prev, 2)
    c = b_ref[half] + w_ref[half, 0:1, :] * u2 + w_ref[half, 1:2, :] * u1 + w_ref[half, 2:3, :] * u
    return c, u1, u2


def _conv_specs(tm, tc, l):
    nprev = lambda m: jnp.maximum(m * (tm // SUBLANES) - 1, 0)
    tile = pl.BlockSpec((2, tm, tc), lambda j, m: (0, m, j))
    halo = pl.BlockSpec((2, SUBLANES, tc), lambda j, m: (0, nprev(m), j))
    wspec = pl.BlockSpec((None, 2, 3, tc), lambda j, m: (l, 0, 0, j))
    bspec = pl.BlockSpec((None, 2, 1, tc), lambda j, m: (l, 0, 0, j))
    return tile, halo, wspec, bspec


def convact_fwd(u, cw, cb, l, *, name):
    _, S, Fh = u.shape
    tm = _pick(S, (256, 128))
    tc = _pick(Fh, (1408, 512, 256, 128))
    tile, halo, wspec, bspec = _conv_specs(tm, tc, l)

    def body(u_ref, p_ref, w_ref, b_ref, o_ref):
        keep = (pl.program_id(1) > 0).astype(F32)
        cg, _, _ = _conv(u_ref[0], p_ref[0] * keep, w_ref, b_ref, 0)
        cv, _, _ = _conv(u_ref[1], p_ref[1] * keep, w_ref, b_ref, 1)
        o_ref[...] = (_silu(cg) * cv).astype(BF16)

    return pl.pallas_call(
        body, name=name, grid=(Fh // tc, S // tm), in_specs=[tile, halo, wspec, bspec],
        out_specs=pl.BlockSpec((tm, tc), lambda j, m: (m, j)),
        out_shape=jax.ShapeDtypeStruct((S, Fh), BF16),
        compiler_params=_params(("parallel", "parallel")),
    )(u, u, cw, cb)


def convact_bwd(u, cw, cb, l, da, *, name):
    _, S, Fh = u.shape
    tm = _pick(S, (256, 128))
    tc = _pick(Fh, (1408, 512, 256, 128))
    tile, halo, wspec, bspec = _conv_specs(tm, tc, l)

    def body(u_ref, p_ref, w_ref, b_ref, da_ref, dc_ref, dw_ref, db_ref):
        first = pl.program_id(1) == 0
        keep = jnp.logical_not(first).astype(F32)
        ug, uv = u_ref[0], u_ref[1]
        cg, g1, g2 = _conv(ug, p_ref[0] * keep, w_ref, b_ref, 0)
        cv, v1, v2 = _conv(uv, p_ref[1] * keep, w_ref, b_ref, 1)
        da_t = da_ref[...]
        sg = _sigmoid(cg)
        dcv = da_t * (cg * sg)
        dcg = da_t * cv * (sg * (1.0 + cg * (1.0 - sg)))
        dc_ref[0] = dcg
        dc_ref[1] = dcv

        @pl.when(first)
        def _():
            dw_ref[...] = jnp.zeros_like(dw_ref)
            db_ref[...] = jnp.zeros_like(db_ref)

        rs = lambda t: jnp.sum(t, axis=0, keepdims=True)
        for half, dc, taps in ((0, dcg, (g2, g1, ug)), (1, dcv, (v2, v1, uv))):
            for j in range(3):
                dw_ref[half, j:j + 1, :] += rs(dc * taps[j])
            db_ref[half] += rs(dc)

    return pl.pallas_call(
        body, name=name, grid=(Fh // tc, S // tm),
        in_specs=[tile, halo, wspec, bspec, pl.BlockSpec((tm, tc), lambda j, m: (m, j))],
        out_specs=[tile, pl.BlockSpec((2, 3, tc), lambda j, m: (0, 0, j)),
                   pl.BlockSpec((2, 1, tc), lambda j, m: (0, 0, j))],
        out_shape=[jax.ShapeDtypeStruct(u.shape, F32), jax.ShapeDtypeStruct((2, 3, Fh), F32),
                   jax.ShapeDtypeStruct((2, 1, Fh), F32)],
        compiler_params=_params(("parallel", "arbitrary")),
    )(u, u, cw, cb, da)


def conv_input_grad(dc, cw, l, *, name):
    _, S, Fh = dc.shape
    tm = _pick(S, (256, 128))
    tc = _pick(Fh, (1408, 512, 256, 128))
    nblk = S // SUBLANES
    nm = S // tm

    def body(dc_ref, n_ref, w_ref, du_ref):
        keep = (pl.program_id(1) < nm - 1).astype(F32)
        for half in range(2):
            t = dc_ref[half]
            nxt = n_ref[half] * keep
            du = (w_ref[half, 2:3, :] * t + w_ref[half, 1:2, :] * _shift_up(t, nxt, 1)
                  + w_ref[half, 0:1, :] * _shift_up(t, nxt, 2))
            du_ref[half] = du.astype(BF16)

    tile = pl.BlockSpec((2, tm, tc), lambda j, m: (0, m, j))
    halo = pl.BlockSpec((2, SUBLANES, tc),
                        lambda j, m: (0, jnp.minimum((m + 1) * (tm // SUBLANES), nblk - 1), j))
    return pl.pallas_call(
        body, name=name, grid=(Fh // tc, nm),
        in_specs=[tile, halo, pl.BlockSpec((None, 2, 3, tc), lambda j, m: (l, 0, 0, j))],
        out_specs=tile, out_shape=jax.ShapeDtypeStruct(dc.shape, BF16),
        compiler_params=_params(("parallel", "parallel")),
    )(dc, dc, cw)


def _adamw_math(w, g, m, v):
    m = ADAM_B1 * m + (1.0 - ADAM_B1) * g
    v = ADAM_B2 * v + (1.0 - ADAM_B2) * (g * g)
    m_hat = m / (1.0 - ADAM_B1 ** ADAM_STEP)
    v_hat = v / (1.0 - ADAM_B2 ** ADAM_STEP)
    delta = -ADAM_LR * (m_hat / (jnp.sqrt(v_hat) + ADAM_EPS) + ADAM_WD * w)
    return delta, m, v


def _row_tile(R, Ccols, n_streams):
    for tr in (512, 256, 128, 64, 32, 16, 8):
        if R % tr == 0 and tr * Ccols * 4 * n_streams * 2 <= VMEM_LIMIT_BYTES // 2:
            return tr
    return R


def adamw(w, g, m, v, *, name):
    R, Ccols = w.shape
    tr = _row_tile(R, Ccols, 7)

    def body(w_ref, g_ref, m_ref, v_ref, d_ref, mo_ref, vo_ref):
        d, mn, vn = _adamw_math(w_ref[...], g_ref[...], m_ref[...], v_ref[...])
        d_ref[...] = d
        mo_ref[...] = mn
        vo_ref[...] = vn

    spec = pl.BlockSpec((tr, Ccols), lambda i: (i, 0))
    shp = jax.ShapeDtypeStruct((R, Ccols), F32)
    return pl.pallas_call(
        body, name=name, grid=(R // tr,), in_specs=[spec] * 4, out_specs=[spec] * 3,
        out_shape=[shp, shp, shp], compiler_params=_params(("parallel",)),
    )(w, g, m, v)


def add_n(xs, *, name):
    R, Ccols = xs[0].shape
    tr = _row_tile(R, Ccols, len(xs) + 1)

    def body(*refs):
        acc = refs[0][...]
        for r in refs[1:-1]:
            acc = acc + r[...]
        refs[-1][...] = acc

    spec = pl.BlockSpec((tr, Ccols), lambda i: (i, 0))
    return pl.pallas_call(
        body, name=name, grid=(R // tr,), in_specs=[spec] * len(xs), out_specs=spec,
        out_shape=jax.ShapeDtypeStruct((R, Ccols), F32), compiler_params=_params(("parallel",)),
    )(*xs)


def _position():
    return lax.axis_index("x"), lax.axis_index("y"), lax.axis_index("c")


def _other_chips(x, y):
    return [(1 - x, y), (x, 1 - y), (1 - x, 1 - y)]


HBM_ANY = pl.BlockSpec(memory_space=pl.ANY)
COMM_PARAMS = pltpu.CompilerParams(has_side_effects=True)


def _col_half(ref, h, chip, n):
    K = ref.shape[1]
    return ref.at[:, pl.ds(h * (K // 2), K // 2), pl.ds(chip * n, n)]


def _row_half(ref, h, chip, r):
    return ref.at[:, pl.ds(chip * r + h * (r // 2), r // 2), :]


def all_gather_weights(col_shards, row_shards, whole_shards, *, name):
    n_col, n_row, n_whole = len(col_shards), len(row_shards), len(whole_shards)
    n_half = n_col + n_row
    n_in = n_half + n_whole

    def body(*refs):
        ins = refs[:n_in]
        outs = refs[n_in:2 * n_in]
        ici_send, ici_recv, d2d_send, d2d_recv, w_send, w_recv, local_sem = refs[2 * n_in:]
        x, y, c = _position()
        me = 2 * x + y
        chips = _other_chips(x, y)
        chip_ids = [2 * cx + cy for cx, cy in chips]
        sibling = (x, y, 1 - c)

        def half_src(i, h):
            rows = ins[i].shape[1] // 2
            return ins[i].at[:, pl.ds(h * rows, rows), :]

        def half_dst(i, h, chip):
            if i < n_col:
                return _col_half(outs[i], h, chip, ins[i].shape[2])
            return _row_half(outs[i], h, chip, ins[i].shape[1])

        def whole_dst(i, chip):
            n = ins[i].shape[2]
            return outs[i].at[:, :, pl.ds(chip * n, n)]

        local = []
        for i in range(n_half):
            for h in range(2):
                local.append(pltpu.make_async_copy(half_src(i, h), half_dst(i, h, me), local_sem.at[2 * i + h]))
        for i in range(n_half, n_in):
            local.append(pltpu.make_async_copy(ins[i], whole_dst(i, me), local_sem.at[n_half + i]))
        for cp in local:
            cp.start()

        def ici(i, k, src_chip):
            return pltpu.make_async_remote_copy(
                src_ref=half_src(i, c), dst_ref=half_dst(i, c, src_chip),
                send_sem=ici_send.at[i, k], recv_sem=ici_recv.at[i, k],
                device_id=(*chips[k], c), device_id_type=MESH)

        def whole(i, k, src_chip):
            return pltpu.make_async_remote_copy(
                src_ref=ins[i], dst_ref=whole_dst(i, src_chip),
                send_sem=w_send.at[i - n_half, k], recv_sem=w_recv.at[i - n_half, k],
                device_id=(*chips[k], c), device_id_type=MESH)

        def d2d(i, k, h):
            return pltpu.make_async_remote_copy(
                src_ref=half_dst(i, h, chip_ids[k]), dst_ref=half_dst(i, h, chip_ids[k]),
                send_sem=d2d_send.at[i, k], recv_sem=d2d_recv.at[i, k],
                device_id=sibling, device_id_type=MESH)

        sent = [ici(i, k, me) for i in range(n_half) for k in range(3)]
        sent += [whole(i, k, me) for i in range(n_half, n_in) for k in range(3)]
        for cp in sent:
            cp.start()
        passed = []
        for i in range(n_half):
            for k in range(3):
                ici(i, k, chip_ids[k]).wait_recv()
                fwd = d2d(i, k, c)
                fwd.start()
                passed.append(fwd)
        for i in range(n_half):
            for k in range(3):
                d2d(i, k, 1 - c).wait_recv()
        for i in range(n_half, n_in):
            for k in range(3):
                whole(i, k, chip_ids[k]).wait_recv()
        for cp in sent + passed:
            cp.wait_send()
        for cp in local:
            cp.wait()

    ins = list(col_shards) + list(row_shards) + list(whole_shards)
    out_shape = []
    for a in col_shards:
        out_shape.append(jax.ShapeDtypeStruct((a.shape[0], a.shape[1], N_CHIPS * a.shape[2]), a.dtype))
    for a in row_shards:
        out_shape.append(jax.ShapeDtypeStruct((a.shape[0], N_CHIPS * a.shape[1], a.shape[2]), a.dtype))
    for a in whole_shards:
        out_shape.append(jax.ShapeDtypeStruct((a.shape[0], a.shape[1], N_CHIPS * a.shape[2]), a.dtype))
    return pl.pallas_call(
        body, name=name, in_specs=[HBM_ANY] * n_in, out_specs=[HBM_ANY] * n_in, out_shape=out_shape,
        scratch_shapes=[pltpu.SemaphoreType.DMA((n_half, 3)), pltpu.SemaphoreType.DMA((n_half, 3)),
                        pltpu.SemaphoreType.DMA((n_half, 3)), pltpu.SemaphoreType.DMA((n_half, 3)),
                        pltpu.SemaphoreType.DMA((max(n_whole, 1), 3)), pltpu.SemaphoreType.DMA((max(n_whole, 1), 3)),
                        pltpu.SemaphoreType.DMA((2 * n_half + n_in,))],
        compiler_params=COMM_PARAMS,
    )(*ins)


def sibling_swap_halves(col_grads, row_grads, *, name):
    n_col, n_row = len(col_grads), len(row_grads)
    n_in = n_col + n_row

    def body(*refs):
        ins = refs[:n_in]
        outs = refs[n_in:2 * n_in]
        send_sem, recv_sem = refs[2 * n_in:]
        x, y, c = _position()
        sibling = (x, y, 1 - c)
        h = 1 - c
        cps = []
        for i in range(n_in):
            if i < n_col:
                K = ins[i].shape[1]
                cps.append(pltpu.make_async_remote_copy(
                    src_ref=ins[i].at[:, pl.ds(h * (K // 2), K // 2), :], dst_ref=outs[i],
                    send_sem=send_sem.at[i, 0], recv_sem=recv_sem.at[i, 0],
                    device_id=sibling, device_id_type=MESH))
            else:
                r = ins[i].shape[1] // N_CHIPS
                for q in range(N_CHIPS):
                    cps.append(pltpu.make_async_remote_copy(
                        src_ref=ins[i].at[:, pl.ds(q * r + h * (r // 2), r // 2), :], dst_ref=outs[i].at[:, q],
                        send_sem=send_sem.at[i, q], recv_sem=recv_sem.at[i, q],
                        device_id=sibling, device_id_type=MESH))
        for cp in cps:
            cp.start()
        for cp in cps:
            cp.wait()

    out_shape = []
    for a in col_grads:
        out_shape.append(jax.ShapeDtypeStruct((a.shape[0], a.shape[1] // 2, a.shape[2]), a.dtype))
    for a in row_grads:
        r = a.shape[1] // N_CHIPS
        out_shape.append(jax.ShapeDtypeStruct((a.shape[0], N_CHIPS, r // 2, a.shape[2]), a.dtype))
    return pl.pallas_call(
        body, name=name, in_specs=[HBM_ANY] * n_in, out_specs=[HBM_ANY] * n_in, out_shape=out_shape,
        scratch_shapes=[pltpu.SemaphoreType.DMA((n_in, N_CHIPS)), pltpu.SemaphoreType.DMA((n_in, N_CHIPS))],
        compiler_params=COMM_PARAMS,
    )(*col_grads, *row_grads)


def scatter_to_chips(col_halves, row_halves, *, name):
    n_col, n_row = len(col_halves), len(row_halves)
    n_in = n_col + n_row

    def body(*refs):
        ins = refs[:n_in]
        outs = refs[n_in:2 * n_in]
        send_sem, recv_sem = refs[2 * n_in:]
        x, y, c = _position()
        chips = _other_chips(x, y)
        cps = []
        for i in range(n_in):
            for k in range(3):
                to = 2 * chips[k][0] + chips[k][1]
                if i < n_col:
                    n = ins[i].shape[2] // N_CHIPS
                    src = ins[i].at[:, :, pl.ds(to * n, n)]
                else:
                    src = ins[i].at[:, to]
                cps.append(pltpu.make_async_remote_copy(
                    src_ref=src, dst_ref=outs[i].at[k], send_sem=send_sem.at[i, k], recv_sem=recv_sem.at[i, k],
                    device_id=(*chips[k], c), device_id_type=MESH))
        for cp in cps:
            cp.start()
        for cp in cps:
            cp.wait()

    out_shape = []
    for a in col_halves:
        out_shape.append(jax.ShapeDtypeStruct((3, a.shape[0], a.shape[1], a.shape[2] // N_CHIPS), a.dtype))
    for a in row_halves:
        out_shape.append(jax.ShapeDtypeStruct((3, a.shape[0], a.shape[2], a.shape[3]), a.dtype))
    return pl.pallas_call(
        body, name=name, in_specs=[HBM_ANY] * n_in, out_specs=[HBM_ANY] * n_in, out_shape=out_shape,
        scratch_shapes=[pltpu.SemaphoreType.DMA((n_in, 3)), pltpu.SemaphoreType.DMA((n_in, 3))],
        compiler_params=COMM_PARAMS,
    )(*col_halves, *row_halves)


def sibling_join_halves(halves, *, name):
    n_in = len(halves)

    def body(*refs):
        ins = refs[:n_in]
        outs = refs[n_in:2 * n_in]
        send_sem, recv_sem, local_sem = refs[2 * n_in:]
        x, y, c = _position()
        sibling = (x, y, 1 - c)

        def rows(i, h):
            R2 = ins[i].shape[1]
            return outs[i].at[:, pl.ds(h * R2, R2), :]

        local = [pltpu.make_async_copy(ins[i], rows(i, c), local_sem.at[i]) for i in range(n_in)]
        sends = [pltpu.make_async_remote_copy(src_ref=ins[i], dst_ref=rows(i, c), send_sem=send_sem.at[i],
                                              recv_sem=recv_sem.at[i], device_id=sibling, device_id_type=MESH)
                 for i in range(n_in)]
        lands = [pltpu.make_async_remote_copy(src_ref=ins[i], dst_ref=rows(i, 1 - c), send_sem=send_sem.at[i],
                                              recv_sem=recv_sem.at[i], device_id=sibling, device_id_type=MESH)
                 for i in range(n_in)]
        for cp in local + sends:
            cp.start()
        for i in range(n_in):
            sends[i].wait_send()
            lands[i].wait_recv()
        for cp in local:
            cp.wait()

    out_shape = [jax.ShapeDtypeStruct((a.shape[0], 2 * a.shape[1], a.shape[2]), a.dtype) for a in halves]
    return pl.pallas_call(
        body, name=name, in_specs=[HBM_ANY] * n_in, out_specs=[HBM_ANY] * n_in, out_shape=out_shape,
        scratch_shapes=[pltpu.SemaphoreType.DMA((n_in,)), pltpu.SemaphoreType.DMA((n_in,)),
                        pltpu.SemaphoreType.DMA((n_in,))],
        compiler_params=COMM_PARAMS,
    )(*halves)


def all_reduce_small(v, *, name):
    R, Ccols = v.shape

    def body(v_ref, o_ref, slots, send_sem, recv_sem):
        x, y, c = _position()
        me = 4 * x + 2 * y + c
        slots[me] = v_ref[...]

        def peer(d):
            return x ^ (d >> 2), y ^ ((d >> 1) & 1), c ^ (d & 1)

        def copy(d, slot):
            return pltpu.make_async_remote_copy(
                src_ref=v_ref, dst_ref=slots.at[slot], send_sem=send_sem.at[d], recv_sem=recv_sem.at[d],
                device_id=peer(d), device_id_type=MESH)

        sends = [copy(d, me) for d in range(1, N_DEV)]
        for cp in sends:
            cp.start()
        for d in range(1, N_DEV):
            px, py, pc = peer(d)
            copy(d, 4 * px + 2 * py + pc).wait_recv()
        for cp in sends:
            cp.wait_send()
        acc = slots[0]
        for d in range(1, N_DEV):
            acc = acc + slots[d]
        o_ref[...] = acc

    return pl.pallas_call(
        body, name=name,
        in_specs=[pl.BlockSpec(memory_space=pltpu.VMEM)], out_specs=pl.BlockSpec(memory_space=pltpu.VMEM),
        out_shape=jax.ShapeDtypeStruct((R, Ccols), F32),
        scratch_shapes=[pltpu.VMEM((N_DEV, R, Ccols), F32), pltpu.SemaphoreType.DMA((N_DEV,)),
                        pltpu.SemaphoreType.DMA((N_DEV,))],
        compiler_params=COMM_PARAMS,
    )(v)


def _pack_small(parts):
    flat = jnp.concatenate([p.reshape(-1).astype(F32) for p in parts])
    n = flat.shape[0]
    rows = -(-n // LANES)
    rows = -(-rows // SUBLANES) * SUBLANES
    return jnp.pad(flat, (0, rows * LANES - n)).reshape(rows, LANES)


def _unpack_small(packed, like):
    flat = packed.reshape(-1)
    out, off = [], 0
    for p in like:
        out.append(flat[off:off + p.size].reshape(p.shape))
        off += p.size
    return out


def kernel(x, norm_mix, norm_ffn, norm_final, attn_w_in, attn_w_out, attn_sinks, hgrn_w_in, hgrn_w_out, hgrn_norm, hgrn_lb_logits, ffn_w_up, ffn_conv_w, ffn_conv_b, ffn_w_down, loss_target, m_norm_mix, m_norm_ffn, m_norm_final, m_attn_w_in, m_attn_w_out, m_attn_sinks, m_hgrn_w_in, m_hgrn_w_out, m_hgrn_norm, m_hgrn_lb_logits, m_ffn_w_up, m_ffn_conv_w, m_ffn_conv_b, m_ffn_w_down, v_norm_mix, v_norm_ffn, v_norm_final, v_attn_w_in, v_attn_w_out, v_attn_sinks, v_hgrn_w_in, v_hgrn_w_out, v_hgrn_norm, v_hgrn_lb_logits, v_ffn_w_up, v_ffn_conv_w, v_ffn_conv_b, v_ffn_w_down):
    S, D = x.shape[1], x.shape[2]
    depth = norm_mix.shape[0]
    n_q = D // HEAD_DIM
    n_kv = n_q // Q_PER_KV
    Fh = ffn_w_down.shape[1] * N_CHIPS
    h = x.reshape(S, D)
    target = loss_target.reshape(S, D)

    col_names = ["attn_w_in", "hgrn_w_in", "ffn_w_up"]
    row_names = ["attn_w_out", "hgrn_w_out", "ffn_w_down"]
    col_w = [attn_w_in.astype(BF16), hgrn_w_in.astype(BF16), ffn_w_up.astype(BF16)]
    row_w = [attn_w_out.astype(BF16), hgrn_w_out.astype(BF16), ffn_w_down.astype(BF16)]
    gathered = all_gather_weights(col_w, row_w, [ffn_conv_w], name="all_gather_weights")
    W = dict(zip(col_names + row_names + ["ffn_conv_w"], gathered))
    cw = W["ffn_conv_w"].reshape(depth, 3, 2, Fh).transpose(0, 2, 1, 3)
    cb = ffn_conv_b.reshape(depth, 2, 1, Fh)
    g_mix = norm_mix.reshape(depth, 1, D)
    g_ffn = norm_ffn.reshape(depth, 1, D)

    slopes = jnp.exp2(-8.0 * jnp.arange(1, n_q + 1, dtype=F32) / n_q).reshape(n_kv, Q_PER_KV)
    slope_col = jnp.repeat(slopes, WINDOW, axis=1).reshape(n_kv, Q_PER_KV * WINDOW, 1)

    def to_heads(t, nh):
        return t.reshape(S, nh, HEAD_DIM).transpose(1, 0, 2)

    def from_heads(t):
        return t.transpose(1, 0, 2).reshape(S, -1)

    saved = []
    for layer in range(depth):
        idx = layer // 2
        rec = {"h_mix": h}
        hn = rmsnorm_fwd(h, g_mix, layer, name=f"norm_mix_fwd{layer}")
        rec["hn"] = hn
        if layer % 2 == 0:
            proj = mm_nn(hn, W["attn_w_in"], idx, name=f"attn_in_fwd{layer}")
            q = to_heads(proj[:, :n_q * HEAD_DIM], n_q).reshape(n_kv, Q_PER_KV, S, HEAD_DIM)
            k = to_heads(proj[:, n_q * HEAD_DIM:(n_q + n_kv) * HEAD_DIM], n_kv)
            v = to_heads(proj[:, (n_q + n_kv) * HEAD_DIM:], n_kv)
            sink_col = jnp.repeat(attn_sinks[idx].reshape(n_kv, Q_PER_KV), WINDOW, axis=1)
            sink_col = sink_col.reshape(n_kv, Q_PER_KV * WINDOW, 1)
            o_heads = attn_fwd(q, k, v, sink_col, slope_col, name=f"attn_fwd{layer}")
            o = from_heads(o_heads.reshape(n_q, S, HEAD_DIM))
            rec.update(q=q, k=k, v=v, sink_col=sink_col, o=o)
            h = mm_nn(o, W["attn_w_out"], idx, name=f"attn_out_fwd{layer}", res=h)
        else:
            proj = mm_nn(hn, W["hgrn_w_in"], idx, name=f"hgrn_in_fwd{layer}")
            ng = hgrn_norm[idx].reshape(1, HG_EXPAND)
            o, states = hgrn_fwd(proj, hgrn_lb_logits, ng, layer, name=f"hgrn_fwd{layer}")
            rec.update(proj=proj, ng=ng, states=states, o=o)
            h = mm_nn(o, W["hgrn_w_out"], idx, name=f"hgrn_out_fwd{layer}", res=h)
        rec["h_ffn"] = h
        hn2 = rmsnorm_fwd(h, g_ffn, layer, name=f"norm_ffn_fwd{layer}")
        u = mm_nn(hn2, W["ffn_w_up"], layer, name=f"ffn_up_fwd{layer}", split_out=True)
        act = convact_fwd(u, cw, cb, layer, name=f"ffn_act_fwd{layer}")
        rec.update(hn2=hn2, u=u, act=act)
        h = mm_nn(act, W["ffn_w_down"], layer, name=f"ffn_down_fwd{layer}", res=h)
        saved.append(rec)

    loss_part, dh, d_norm_final = loss_head(h, norm_final.reshape(1, D), target, name="loss_head")

    n_attn, n_hgrn = attn_w_in.shape[0], hgrn_w_in.shape[0]
    G = dict(attn_w_in=None, attn_w_out=None, hgrn_w_in=None, hgrn_w_out=None, ffn_w_up=None, ffn_w_down=None)
    d_norm_mix, d_norm_ffn = [None] * depth, [None] * depth
    d_conv_w, d_conv_b = [None] * depth, [None] * depth
    d_sinks, d_hgrn_norm = [None] * n_attn, [None] * n_hgrn
    d_logits = jnp.zeros_like(hgrn_lb_logits)
    for layer in reversed(range(depth)):
        idx = layer // 2
        rec = saved[layer]
        G["ffn_w_down"] = mm_tn(rec["act"], dh, G["ffn_w_down"], layer, depth, name=f"ffn_down_dw{layer}")
        dact = mm_nt(dh, W["ffn_w_down"], layer, name=f"ffn_down_dx{layer}")
        dc, dcw, dcb = convact_bwd(rec["u"], cw, cb, layer, dact, name=f"ffn_act_bwd{layer}")
        d_conv_w[layer] = dcw.transpose(1, 0, 2).reshape(3, 2 * Fh)
        d_conv_b[layer] = dcb.reshape(2 * Fh)
        du = conv_input_grad(dc, cw, layer, name=f"ffn_conv_dx{layer}")
        G["ffn_w_up"] = mm_tn(rec["hn2"], du, G["ffn_w_up"], layer, depth, name=f"ffn_up_dw{layer}", split_b=True)
        dhn2 = mm_nt(du, W["ffn_w_up"], layer, name=f"ffn_up_dx{layer}", split_in=True)
        dh, dg = rmsnorm_bwd(rec["h_ffn"], g_ffn, layer, dhn2, dh, name=f"norm_ffn_bwd{layer}")
        d_norm_ffn[layer] = dg.reshape(D)
        if layer % 2 == 0:
            G["attn_w_out"] = mm_tn(rec["o"], dh, G["attn_w_out"], idx, n_attn, name=f"attn_out_dw{layer}")
            do = mm_nt(dh, W["attn_w_out"], idx, name=f"attn_out_dx{layer}", out_dtype=BF16)
            do_heads = to_heads(do, n_q).reshape(n_kv, Q_PER_KV, S, HEAD_DIM)
            dq, dkc, dkp, dvc, dvp, dsink = attn_bwd(rec["q"], rec["k"], rec["v"], rec["sink_col"], slope_col,
                                                     do_heads, name=f"attn_bwd{layer}")
            shift = lambda t: jnp.concatenate([t[:, WINDOW:], jnp.zeros_like(t[:, :WINDOW])], axis=1)
            dk = dkc + shift(dkp)
            dv = dvc + shift(dvp)
            dproj = jnp.concatenate([from_heads(dq.reshape(n_q, S, HEAD_DIM)), from_heads(dk), from_heads(dv)], axis=1)
            d_sinks[idx] = jnp.sum(dsink.reshape(n_kv, Q_PER_KV, WINDOW), axis=-1).reshape(n_q)
            G["attn_w_in"] = mm_tn(rec["hn"], dproj, G["attn_w_in"], idx, n_attn, name=f"attn_in_dw{layer}")
            dhn = mm_nt(dproj, W["attn_w_in"], idx, name=f"attn_in_dx{layer}")
        else:
            G["hgrn_w_out"] = mm_tn(rec["o"], dh, G["hgrn_w_out"], idx, n_hgrn, name=f"hgrn_out_dw{layer}")
            do = mm_nt(dh, W["hgrn_w_out"], idx, name=f"hgrn_out_dx{layer}")
            dproj, dlg, dng = hgrn_bwd(rec["proj"], hgrn_lb_logits, rec["ng"], rec["states"], do, layer,
                                       name=f"hgrn_bwd{layer}")
            d_logits = d_logits + dlg
            d_hgrn_norm[idx] = dng.reshape(HG_EXPAND)
            G["hgrn_w_in"] = mm_tn(rec["hn"], dproj, G["hgrn_w_in"], idx, n_hgrn, name=f"hgrn_in_dw{layer}")
            dhn = mm_nt(dproj, W["hgrn_w_in"], idx, name=f"hgrn_in_dx{layer}")
        dh, dg = rmsnorm_bwd(rec["h_mix"], g_mix, layer, dhn, dh, name=f"norm_mix_bwd{layer}")
        d_norm_mix[layer] = dg.reshape(D)
    grad_x = dh.reshape(x.shape)

    small_w = [norm_mix, norm_ffn, norm_final, attn_sinks, hgrn_norm, hgrn_lb_logits, ffn_conv_b]
    small_m = [m_norm_mix, m_norm_ffn, m_norm_final, m_attn_sinks, m_hgrn_norm, m_hgrn_lb_logits, m_ffn_conv_b]
    small_v = [v_norm_mix, v_norm_ffn, v_norm_final, v_attn_sinks, v_hgrn_norm, v_hgrn_lb_logits, v_ffn_conv_b]
    small_grads = [jnp.stack(d_norm_mix), jnp.stack(d_norm_ffn), d_norm_final.reshape(D), jnp.stack(d_sinks),
                   jnp.stack(d_hgrn_norm), d_logits, jnp.stack(d_conv_b)]
    reduced = all_reduce_small(_pack_small(small_grads + [loss_part]), name="all_reduce_small")
    small_g = _unpack_small(reduced, small_w + [loss_part])
    loss = small_g.pop().reshape(())
    pad = [jnp.zeros((1, 1), F32)]
    sd, sm_, sv_ = adamw(_pack_small(small_w + pad), reduced, _pack_small(small_m + pad),
                         _pack_small(small_v + pad), name="adamw_small")
    small_d = _unpack_small(sd, small_w)
    small_nm = _unpack_small(sm_, small_w)
    small_nv = _unpack_small(sv_, small_w)

    col_g = [G["attn_w_in"], G["hgrn_w_in"], G["ffn_w_up"]]
    row_g = [G["attn_w_out"], G["hgrn_w_out"], G["ffn_w_down"]]
    theirs = sibling_swap_halves(col_g, row_g, name="rs_sibling_swap")
    x_, y_, c_ = _position()
    me = 2 * x_ + y_
    pair = []
    for i, g in enumerate(col_g):
        L_, K_, N_ = g.shape
        mine = lax.dynamic_slice_in_dim(g, c_ * (K_ // 2), K_ // 2, axis=1)
        pair.append(add_n([mine.reshape(-1, N_), theirs[i].reshape(-1, N_)], name=f"rs_pair_add_col{i}")
                    .reshape(L_, K_ // 2, N_))
    for i, g in enumerate(row_g):
        L_, R4, D_ = g.shape
        r = R4 // N_CHIPS
        mine = lax.dynamic_slice_in_dim(g.reshape(L_, N_CHIPS, 2, r // 2, D_), c_, 1, axis=2)
        pair.append(add_n([mine.reshape(-1, D_), theirs[3 + i].reshape(-1, D_)], name=f"rs_pair_add_row{i}")
                    .reshape(L_, N_CHIPS, r // 2, D_))
    landed = scatter_to_chips(pair[:3], pair[3:], name="rs_scatter")
    halves = []
    for i in range(3):
        L_, K2, N_ = pair[i].shape
        n = N_ // N_CHIPS
        own = lax.dynamic_slice_in_dim(pair[i], me * n, n, axis=2)
        parts = [own.reshape(-1, n)] + [landed[i][k].reshape(-1, n) for k in range(3)]
        halves.append(add_n(parts, name=f"rs_chip_add_col{i}").reshape(L_, K2, n))
    for i in range(3):
        L_, _, r2, D_ = pair[3 + i].shape
        own = lax.dynamic_slice_in_dim(pair[3 + i], me, 1, axis=1)
        parts = [own.reshape(-1, D_)] + [landed[3 + i][k].reshape(-1, D_) for k in range(3)]
        halves.append(add_n(parts, name=f"rs_chip_add_row{i}").reshape(L_, r2, D_))
    shards = sibling_join_halves(halves, name="rs_sibling_join")
    big_names = col_names + row_names
    big_w = dict(attn_w_in=attn_w_in, hgrn_w_in=hgrn_w_in, ffn_w_up=ffn_w_up, attn_w_out=attn_w_out,
                 hgrn_w_out=hgrn_w_out, ffn_w_down=ffn_w_down)
    big_m = dict(attn_w_in=m_attn_w_in, hgrn_w_in=m_hgrn_w_in, ffn_w_up=m_ffn_w_up, attn_w_out=m_attn_w_out,
                 hgrn_w_out=m_hgrn_w_out, ffn_w_down=m_ffn_w_down)
    big_v = dict(attn_w_in=v_attn_w_in, hgrn_w_in=v_hgrn_w_in, ffn_w_up=v_ffn_w_up, attn_w_out=v_attn_w_out,
                 hgrn_w_out=v_hgrn_w_out, ffn_w_down=v_ffn_w_down)
    grads, deltas, new_m, new_v = {}, {}, {}, {}
    for name_, gshard in zip(big_names, shards):
        w_ = big_w[name_]
        cols = w_.shape[-1]
        d_, m_, v_ = adamw(w_.reshape(-1, cols), gshard.reshape(-1, cols), big_m[name_].reshape(-1, cols),
                           big_v[name_].reshape(-1, cols), name=f"adamw_{name_}")
        grads[name_] = gshard.reshape(w_.shape)
        deltas[name_], new_m[name_], new_v[name_] = (t.reshape(w_.shape) for t in (d_, m_, v_))

    gconv = jnp.stack(d_conv_w)
    n_conv = ffn_conv_w.shape[2]
    gconv_all = _unpack_small(all_reduce_small(_pack_small([gconv]), name="all_reduce_conv_w"), [gconv])[0]
    gconv_mine = lax.dynamic_slice_in_dim(gconv_all, me * n_conv, n_conv, axis=2)
    cshape = ffn_conv_w.shape
    d_, m_, v_ = adamw(ffn_conv_w.reshape(-1, n_conv), gconv_mine.reshape(-1, n_conv),
                       m_ffn_conv_w.reshape(-1, n_conv), v_ffn_conv_w.reshape(-1, n_conv), name="adamw_ffn_conv_w")
    grads["ffn_conv_w"] = gconv_mine
    deltas["ffn_conv_w"], new_m["ffn_conv_w"], new_v["ffn_conv_w"] = (t.reshape(cshape) for t in (d_, m_, v_))

    order = ["norm_mix", "norm_ffn", "norm_final", "attn_w_in", "attn_w_out", "attn_sinks", "hgrn_w_in",
             "hgrn_w_out", "hgrn_norm", "hgrn_lb_logits", "ffn_w_up", "ffn_conv_w", "ffn_conv_b", "ffn_w_down"]
    small_names = ["norm_mix", "norm_ffn", "norm_final", "attn_sinks", "hgrn_norm", "hgrn_lb_logits", "ffn_conv_b"]
    for i, name_ in enumerate(small_names):
        grads[name_], deltas[name_], new_m[name_], new_v[name_] = small_g[i], small_d[i], small_nm[i], small_nv[i]
    return (loss, grad_x, *[grads[n] for n in order], *[deltas[n] for n in order],
            *[new_m[n] for n in order], *[new_v[n] for n in order])
```

```python
import functools
import math

import numpy as np
import jax
import jax.numpy as jnp
from jax import lax
from jax.experimental import pallas as pl
from jax.experimental.pallas import tpu as pltpu

F32 = jnp.float32
BF16 = jnp.bfloat16
MESH = pl.DeviceIdType.MESH

HEAD_DIM = 64
Q_PER_KV = 4
WINDOW = 128
HG_EXPAND = 128
HG_CHUNK = 64
EPS = 1e-6
N_CHIPS = 4
N_DEV = 8

ADAM_LR = 0.001
ADAM_B1 = 0.9
ADAM_B2 = 0.999
ADAM_EPS = 1e-08
ADAM_WD = 0.01
ADAM_STEP = 10

VMEM_LIMIT_BYTES = 48 * 1024 * 1024
LANES = 128
SUBLANES = 8


def _params(sem=None):
    return pltpu.CompilerParams(dimension_semantics=sem, vmem_limit_bytes=VMEM_LIMIT_BYTES)


def _pick(n, cands):
    for c in cands:
        if n % c == 0:
            return c
    return n


def _dot(a, b, dims):
    return lax.dot_general(a.astype(BF16), b.astype(BF16), (dims, ((), ())),
                           preferred_element_type=F32)


NN = ((1,), (0,))
NT = ((1,), (1,))
TN = ((0,), (0,))


def mm_nn(a, w, l, *, name, res=None, out_dtype=F32, split_out=False):
    M, K = a.shape
    N = w.shape[2]
    tm = _pick(M, (1024, 512, 256, 128))
    tn = _pick(N // 2 if split_out else N, (1408, 1024, 768, 512, 256, 128))
    tk = _pick(K, (1024, 1408, 512, 256, 128))
    nk = K // tk
    nh = (N // 2) // tn

    def body(*refs):
        if res is None:
            a_ref, w_ref, o_ref = refs[:3]
            r_ref = None
        else:
            a_ref, w_ref, r_ref, o_ref = refs[:4]
        part = _dot(a_ref[...], w_ref[...], NN)

        def finish(acc):
            if r_ref is not None:
                acc = acc + r_ref[...]
            o_ref[...] = acc.astype(out_dtype)

        if nk == 1:
            finish(part)
        else:
            acc_ref = refs[-1]
            k = pl.program_id(2)

            @pl.when(k == 0)
            def _():
                acc_ref[...] = part

            @pl.when(k > 0)
            def _():
                acc_ref[...] += part

            @pl.when(k == nk - 1)
            def _():
                finish(acc_ref[...])

    in_specs = [pl.BlockSpec((tm, tk), lambda n, m, k: (m, k)),
                pl.BlockSpec((None, tk, tn), lambda n, m, k: (l, k, n))]
    args = [a, w]
    if res is not None:
        in_specs.append(pl.BlockSpec((tm, tn), lambda n, m, k: (m, n)))
        args.append(res)
    if split_out:
        out_shape = jax.ShapeDtypeStruct((2, M, N // 2), out_dtype)
        out_spec = pl.BlockSpec((None, tm, tn), lambda n, m, k: (n // nh, m, n % nh))
    else:
        out_shape = jax.ShapeDtypeStruct((M, N), out_dtype)
        out_spec = pl.BlockSpec((tm, tn), lambda n, m, k: (m, n))
    return pl.pallas_call(
        body, name=name, grid=(N // tn, M // tm, nk), in_specs=in_specs, out_specs=out_spec,
        out_shape=out_shape,
        scratch_shapes=[] if nk == 1 else [pltpu.VMEM((tm, tn), F32)],
        compiler_params=_params(("parallel", "parallel", "arbitrary")),
    )(*args)


def mm_nt(a, w, l, *, name, split_in=False, out_dtype=F32):
    if split_in:
        _, M, Nh = a.shape
        N = 2 * Nh
    else:
        M, N = a.shape
        Nh = N
    K = w.shape[1]
    tm = _pick(M, (1024, 512, 256, 128))
    tko = _pick(K, (1024, 1408, 512, 256, 128))
    tr = _pick(Nh, (1024, 1408, 768, 512, 256, 128))
    nr = N // tr
    nh = Nh // tr

    def body(a_ref, w_ref, o_ref, *scratch):
        part = _dot(a_ref[...], w_ref[...], NT)
        if nr == 1:
            o_ref[...] = part.astype(out_dtype)
        else:
            acc_ref = scratch[0]
            r = pl.program_id(2)

            @pl.when(r == 0)
            def _():
                acc_ref[...] = part

            @pl.when(r > 0)
            def _():
                acc_ref[...] += part

            @pl.when(r == nr - 1)
            def _():
                o_ref[...] = acc_ref[...].astype(out_dtype)

    if split_in:
        a_spec = pl.BlockSpec((None, tm, tr), lambda ko, m, r: (r // nh, m, r % nh))
    else:
        a_spec = pl.BlockSpec((tm, tr), lambda ko, m, r: (m, r))
    return pl.pallas_call(
        body, name=name, grid=(K // tko, M // tm, nr),
        in_specs=[a_spec, pl.BlockSpec((None, tko, tr), lambda ko, m, r: (l, ko, r))],
        out_specs=pl.BlockSpec((tm, tko), lambda ko, m, r: (m, ko)),
        out_shape=jax.ShapeDtypeStruct((M, K), out_dtype),
        scratch_shapes=[] if nr == 1 else [pltpu.VMEM((tm, tko), F32)],
        compiler_params=_params(("parallel", "parallel", "arbitrary")),
    )(a, w)


def mm_tn(a, b, buf, l, n_layers, *, name, split_b=False):
    M, K = a.shape
    if split_b:
        Nh = b.shape[2]
        N = 2 * Nh
    else:
        N = b.shape[1]
        Nh = N
    tm = _pick(M, (512, 256, 128))
    tko = _pick(K, (1024, 1408, 704, 512, 256, 128))
    tn = _pick(Nh, (512, 256, 128))
    nm = M // tm
    nh = Nh // tn

    def body(*refs):
        a_ref, b_ref = refs[:2]
        o_ref = refs[-1]
        part = _dot(a_ref[...], b_ref[...], TN)
        r = pl.program_id(2)

        @pl.when(r == 0)
        def _():
            o_ref[...] = part

        @pl.when(r > 0)
        def _():
            o_ref[...] += part

    if split_b:
        b_spec = pl.BlockSpec((None, tm, tn), lambda ko, n, r: (n // nh, r, n % nh))
    else:
        b_spec = pl.BlockSpec((tm, tn), lambda ko, n, r: (r, n))
    in_specs = [pl.BlockSpec((tm, tko), lambda ko, n, r: (r, ko)), b_spec]
    args = [a, b]
    aliases = {}
    if buf is not None:
        in_specs.append(pl.BlockSpec(memory_space=pl.ANY))
        args.append(buf)
        aliases = {2: 0}
    return pl.pallas_call(
        body, name=name, grid=(K // tko, N // tn, nm), in_specs=in_specs,
        out_specs=pl.BlockSpec((None, tko, tn), lambda ko, n, r: (l, ko, n)),
        out_shape=jax.ShapeDtypeStruct((n_layers, K, N), F32),
        input_output_aliases=aliases,
        compiler_params=_params(("parallel", "parallel", "arbitrary")),
    )(*args)


def mm_dw(at, b, buf, l, n_layers, *, name, split_b=False):
    K, M = at.shape
    if split_b:
        Nh = b.shape[2]
        N = 2 * Nh
    else:
        N = b.shape[1]
        Nh = N
    tm = _pick(M, (512, 256, 128))
    tko = _pick(K, (1024, 1408, 704, 512, 256, 128))
    tn = _pick(Nh, (1408, 1024, 768, 512, 256, 128))
    nm = M // tm
    nh = Nh // tn

    def body(*refs):
        a_ref, b_ref = refs[:2]
        o_ref = refs[-1]
        part = _dot(a_ref[...], b_ref[...], NN)
        r = pl.program_id(2)

        @pl.when(r == 0)
        def _():
            o_ref[...] = part

        @pl.when(r > 0)
        def _():
            o_ref[...] += part

    if split_b:
        b_spec = pl.BlockSpec((None, tm, tn), lambda ko, n, r: (n // nh, r, n % nh))
    else:
        b_spec = pl.BlockSpec((tm, tn), lambda ko, n, r: (r, n))
    in_specs = [pl.BlockSpec((tko, tm), lambda ko, n, r: (ko, r)), b_spec]
    args = [at, b]
    aliases = {}
    if buf is not None:
        in_specs.append(pl.BlockSpec(memory_space=pl.ANY))
        args.append(buf)
        aliases = {2: 0}
    return pl.pallas_call(
        body, name=name, grid=(K // tko, N // tn, nm), in_specs=in_specs,
        out_specs=pl.BlockSpec((None, tko, tn), lambda ko, n, r: (l, ko, n)),
        out_shape=jax.ShapeDtypeStruct((n_layers, K, N), F32),
        input_output_aliases=aliases,
        compiler_params=_params(("parallel", "parallel", "arbitrary")),
    )(*args)


def _rms(x, g):
    return x * lax.rsqrt(jnp.mean(x * x, axis=-1, keepdims=True) + EPS) * g


def rmsnorm_fwd(h, g, l, *, name):
    S, D = h.shape
    tm = _pick(S, (512, 256, 128))

    def body(h_ref, g_ref, o_ref, ot_ref):
        y = _rms(h_ref[...], g_ref[...])
        o_ref[...] = y.astype(BF16)
        ot_ref[...] = y.T.astype(BF16)

    return pl.pallas_call(
        body, name=name, grid=(S // tm,),
        in_specs=[pl.BlockSpec((tm, D), lambda m: (m, 0)),
                  pl.BlockSpec((None, 1, D), lambda m: (l, 0, 0))],
        out_specs=[pl.BlockSpec((tm, D), lambda m: (m, 0)), pl.BlockSpec((D, tm), lambda m: (0, m))],
        out_shape=[jax.ShapeDtypeStruct((S, D), BF16), jax.ShapeDtypeStruct((D, S), BF16)],
        compiler_params=_params(("parallel",)),
    )(h, g)


def rmsnorm_bwd(h, g, l, dhn, dres, *, name):
    S, D = h.shape
    tm = _pick(S, (512, 256, 128))

    def body(h_ref, g_ref, dhn_ref, dres_ref, dh_ref, dg_ref):
        _, vjp = jax.vjp(_rms, h_ref[...], g_ref[...])
        dx, dg = vjp(dhn_ref[...])
        dh_ref[...] = dres_ref[...] + dx

        @pl.when(pl.program_id(0) == 0)
        def _():
            dg_ref[...] = dg

        @pl.when(pl.program_id(0) > 0)
        def _():
            dg_ref[...] += dg

    row = pl.BlockSpec((tm, D), lambda m: (m, 0))
    return pl.pallas_call(
        body, name=name, grid=(S // tm,),
        in_specs=[row, pl.BlockSpec((None, 1, D), lambda m: (l, 0, 0)), row, row],
        out_specs=[row, pl.BlockSpec((1, D), lambda m: (0, 0))],
        out_shape=[jax.ShapeDtypeStruct((S, D), F32), jax.ShapeDtypeStruct((1, D), F32)],
        compiler_params=_params(("arbitrary",)),
    )(h, g, dhn, dres)


def loss_head(h, g, target, *, name):
    S, D = h.shape
    tm = _pick(S, (512, 256, 128))

    def body(h_ref, g_ref, t_ref, loss_ref, dh_ref, dg_ref):
        y, vjp = jax.vjp(_rms, h_ref[...], g_ref[...])
        err = y - t_ref[...]
        part = 0.5 * jnp.sum(jnp.sum(err * err, axis=-1, keepdims=True) / D, axis=0, keepdims=True)
        dx, dg = vjp(err / D)
        dh_ref[...] = dx

        @pl.when(pl.program_id(0) == 0)
        def _():
            dg_ref[...] = dg
            loss_ref[...] = part

        @pl.when(pl.program_id(0) > 0)
        def _():
            dg_ref[...] += dg
            loss_ref[...] += part

    row = pl.BlockSpec((tm, D), lambda m: (m, 0))
    return pl.pallas_call(
        body, name=name, grid=(S // tm,),
        in_specs=[row, pl.BlockSpec((1, D), lambda m: (0, 0)), row],
        out_specs=[pl.BlockSpec((1, 1), lambda m: (0, 0)), row, pl.BlockSpec((1, D), lambda m: (0, 0))],
        out_shape=[jax.ShapeDtypeStruct((1, 1), F32), jax.ShapeDtypeStruct((S, D), F32),
                   jax.ShapeDtypeStruct((1, D), F32)],
        compiler_params=_params(("arbitrary",)),
    )(h, g, target)


def _make_mm(dims_fwd, dims_da, dims_db, swap_da=False, swap_db=False):
    @jax.custom_vjp
    def mm(a, b):
        return _dot(a, b, dims_fwd)

    def fwd(a, b):
        return mm(a, b), (a, b)

    def bwd(resid, g):
        a, b = resid
        da = _dot(b, g, dims_da) if swap_da else _dot(g, b, dims_da)
        db = _dot(g, a, dims_db) if swap_db else _dot(a, g, dims_db)
        return da, db

    mm.defvjp(fwd, bwd)
    return mm


_mm = _make_mm(NN, NT, TN)
_mm_nt = _make_mm(NT, NN, TN, swap_db=True)
_mm_tn = _make_mm(TN, NT, NN, swap_da=True)


def _dot_hi(a, b):
    return jnp.dot(a, b, precision=lax.Precision.HIGHEST, preferred_element_type=F32)


def _sigmoid(x):
    return 1.0 / (1.0 + jnp.exp(-x))


def _silu(x):
    return x * _sigmoid(x)


def _attn_block(q, kw, vw, sink, slope, first):
    rows = q.shape[0]
    s = _mm_nt(q, kw) * (HEAD_DIM ** -0.5)
    qi = lax.broadcasted_iota(jnp.int32, (rows, 2 * WINDOW), 0) % WINDOW
    ki = lax.broadcasted_iota(jnp.int32, (rows, 2 * WINDOW), 1)
    dist = qi + WINDOW - ki
    valid = (dist >= 0) & (dist < WINDOW) & (ki >= WINDOW * first)
    s = jnp.where(valid, s - slope * dist.astype(F32), -jnp.inf)
    m = lax.stop_gradient(jnp.maximum(jnp.max(s, axis=-1, keepdims=True), sink))
    e = jnp.exp(s - m)
    denom = jnp.sum(e, axis=-1, keepdims=True) + jnp.exp(sink - m)
    return _mm(e / denom, vw)


def _attn_specs():
    G = Q_PER_KV
    qspec = pl.BlockSpec((None, G, WINDOW, HEAD_DIM), lambda h, n: (h, 0, n, 0))
    prev = pl.BlockSpec((None, WINDOW, HEAD_DIM), lambda h, n: (h, jnp.maximum(n - 1, 0), 0))
    cur = pl.BlockSpec((None, WINDOW, HEAD_DIM), lambda h, n: (h, n, 0))
    col = pl.BlockSpec((None, G * WINDOW, 1), lambda h, n: (h, 0, 0))
    return qspec, prev, cur, col


def attn_fwd(q, k, v, sink_col, slope_col, *, name):
    Hkv, G, S, _ = q.shape
    qspec, prev, cur, col = _attn_specs()

    def body(q_ref, kp_ref, kc_ref, vp_ref, vc_ref, sink_ref, slope_ref, o_ref):
        first = (pl.program_id(1) == 0).astype(jnp.int32)
        kw = jnp.concatenate([kp_ref[...], kc_ref[...]], axis=0)
        vw = jnp.concatenate([vp_ref[...], vc_ref[...]], axis=0)
        o = _attn_block(q_ref[...].reshape(G * WINDOW, HEAD_DIM), kw, vw, sink_ref[...],
                        slope_ref[...], first)
        o_ref[...] = o.reshape(G, WINDOW, HEAD_DIM).astype(BF16)

    return pl.pallas_call(
        body, name=name, grid=(Hkv, S // WINDOW),
        in_specs=[qspec, prev, cur, prev, cur, col, col], out_specs=qspec,
        out_shape=jax.ShapeDtypeStruct(q.shape, BF16),
        compiler_params=_params(("parallel", "parallel")),
    )(q, k, k, v, v, sink_col, slope_col)


def attn_bwd(q, k, v, sink_col, slope_col, do, *, name):
    Hkv, G, S, _ = q.shape
    qspec, prev, cur, col = _attn_specs()

    def body(q_ref, kp_ref, kc_ref, vp_ref, vc_ref, sink_ref, slope_ref, do_ref,
             dq_ref, dkc_ref, dkp_ref, dvc_ref, dvp_ref, dsink_ref):
        is_first = pl.program_id(1) == 0
        first = is_first.astype(jnp.int32)
        kw = jnp.concatenate([kp_ref[...], kc_ref[...]], axis=0)
        vw = jnp.concatenate([vp_ref[...], vc_ref[...]], axis=0)
        fn = functools.partial(_attn_block, slope=slope_ref[...], first=first)
        _, vjp = jax.vjp(fn, q_ref[...].reshape(G * WINDOW, HEAD_DIM), kw, vw, sink_ref[...])
        dq, dkw, dvw, dsink = vjp(do_ref[...].reshape(G * WINDOW, HEAD_DIM).astype(F32))
        dq_ref[...] = dq.reshape(G, WINDOW, HEAD_DIM)
        dkp_ref[...] = dkw[:WINDOW]
        dkc_ref[...] = dkw[WINDOW:]
        dvp_ref[...] = dvw[:WINDOW]
        dvc_ref[...] = dvw[WINDOW:]

        @pl.when(is_first)
        def _():
            dsink_ref[...] = dsink

        @pl.when(jnp.logical_not(is_first))
        def _():
            dsink_ref[...] += dsink

    kv_shape = jax.ShapeDtypeStruct(k.shape, F32)
    return pl.pallas_call(
        body, name=name, grid=(Hkv, S // WINDOW),
        in_specs=[qspec, prev, cur, prev, cur, col, col, qspec],
        out_specs=[qspec, cur, cur, cur, cur, col],
        out_shape=[jax.ShapeDtypeStruct(q.shape, F32), kv_shape, kv_shape, kv_shape, kv_shape,
                   jax.ShapeDtypeStruct(sink_col.shape, F32)],
        compiler_params=_params(("parallel", "arbitrary")),
    )(q, k, k, v, v, sink_col, slope_col, do)


def _hgrn_consts():
    C = HG_CHUNK
    t = np.arange(C)[:, None]
    j = np.arange(C)[None, :]
    cq, ck, mk = [], [], []
    H = C // 2
    while H >= 1:
        start = (t // H) * H
        cq.append((j > start) & (j <= t))
        ck.append((j > t) & (j <= start + H))
        mk.append(((t // H) % 2 == 1) & ((j // H) == (t // H) - 1))
        H //= 2
    ltri = (j <= t)
    ops = np.concatenate([ltri] + cq + ck, axis=0).astype(np.float32)
    ops3 = np.concatenate([ops, ops, ops], axis=1)
    opst3 = np.concatenate([ops.T, ops.T, ops.T], axis=1)
    return (jnp.asarray(ops3, BF16), jnp.asarray(opst3, BF16), jnp.asarray(np.stack(mk).astype(np.float32)))


def _split3(x):
    hi = x.astype(BF16)
    r = x - hi.astype(F32)
    mid = r.astype(BF16)
    lo = (r - mid.astype(F32)).astype(BF16)
    return jnp.concatenate([hi, mid, lo], axis=0)


@jax.custom_vjp
def _seg_sums(g, ops3, opst3):
    C = g.shape[0]
    out = jnp.dot(ops3, _split3(g), preferred_element_type=F32)
    return tuple(out[i * C:(i + 1) * C] for i in range(ops3.shape[0] // C))


def _seg_sums_fwd(g, ops3, opst3):
    return _seg_sums(g, ops3, opst3), opst3


def _seg_sums_bwd(opst3, cts):
    ct = jnp.concatenate(cts, axis=0)
    return jnp.dot(opst3, _split3(ct), preferred_element_type=F32), None, None


_seg_sums.defvjp(_seg_sums_fwd, _seg_sums_bwd)


def _hgrn_head(qr, fr, iv, gr, st, logits, ng, ops3, opst3, mk, layer):
    n_levels = mk.shape[0]
    depth = logits.shape[0]
    e = jnp.exp(logits - lax.stop_gradient(jnp.max(logits, axis=0, keepdims=True)))
    sm = e / jnp.sum(e, axis=0, keepdims=True)
    row = lax.broadcasted_iota(jnp.int32, (depth, HG_EXPAND), 0)
    lb = jnp.sum(jnp.where((row >= 1) & (row <= layer), sm, 0.0), axis=0, keepdims=True)

    q = _silu(qr)
    a = jnp.log(lb)
    c = jnp.log(1.0 - lb) - (jnp.maximum(-fr, 0.0) + jnp.log(1.0 + jnp.exp(-jnp.abs(fr))))
    mx = lax.stop_gradient(jnp.maximum(a, c))
    g = mx + jnp.log(jnp.exp(a - mx) + jnp.exp(c - mx))
    k = (1.0 - lb) * _sigmoid(-fr)

    sums = _seg_sums(g, ops3, opst3)
    b = sums[0]
    b_last = jnp.sum(g, axis=0, keepdims=True)

    o = jnp.sum(q * k, axis=-1, keepdims=True) * iv
    amat = None
    for lv in range(n_levels):
        eq = jnp.exp(sums[1 + lv])
        ek = jnp.exp(sums[1 + n_levels + lv])
        part = _mm_nt(q * eq, k * ek) * mk[lv]
        amat = part if amat is None else amat + part
    o = o + _mm(amat, iv) + _mm_nt(q * jnp.exp(b), st)
    st_new = st * jnp.exp(b_last) + _mm_tn(iv, k * jnp.exp(b_last - b))

    on = o * lax.rsqrt(jnp.mean(o * o, axis=-1, keepdims=True) + EPS) * ng
    return on * _silu(gr), st_new


def _hgrn_in_specs(D, depth, order):
    C = HG_CHUNK
    nl = int(math.log2(C))
    n_ops = 1 + 2 * nl
    return [pl.BlockSpec((C, 4 * D), lambda c: (order(c), 0)),
            pl.BlockSpec((depth, D), lambda c: (0, 0)),
            pl.BlockSpec((1, HG_EXPAND), lambda c: (0, 0)),
            pl.BlockSpec((n_ops * C, 3 * C), lambda c: (0, 0)),
            pl.BlockSpec((C, 3 * n_ops * C), lambda c: (0, 0)),
            pl.BlockSpec((nl, C, C), lambda c: (0, 0, 0))]


def _hgrn_cols(D, h):
    return [slice(j * D + h * HG_EXPAND, j * D + (h + 1) * HG_EXPAND) for j in range(4)]


def hgrn_fwd(proj, logits, ng, layer, *, name):
    S = proj.shape[0]
    D = proj.shape[1] // 4
    H = D // HG_EXPAND
    C = HG_CHUNK
    nc = S // C
    consts = _hgrn_consts()

    def body(p_ref, lg_ref, ng_ref, ops3_ref, opst3_ref, mk_ref, o_ref, st_out_ref, st_ref):
        @pl.when(pl.program_id(0) == 0)
        def _():
            st_ref[...] = jnp.zeros_like(st_ref)

        st_out_ref[...] = st_ref[...]
        ops3, opst3, mk = ops3_ref[...], opst3_ref[...], mk_ref[...]
        for h in range(H):
            sl = slice(h * HG_EXPAND, (h + 1) * HG_EXPAND)
            cols = _hgrn_cols(D, h)
            out, st_new = _hgrn_head(p_ref[:, cols[0]], p_ref[:, cols[1]], p_ref[:, cols[2]], p_ref[:, cols[3]],
                                     st_ref[h], lg_ref[:, sl], ng_ref[...], ops3, opst3, mk, layer)
            o_ref[:, sl] = out.astype(BF16)
            st_ref[h] = st_new

    return pl.pallas_call(
        body, name=name, grid=(nc,),
        in_specs=_hgrn_in_specs(D, logits.shape[0], lambda c: c),
        out_specs=[pl.BlockSpec((C, D), lambda c: (c, 0)),
                   pl.BlockSpec((None, H, HG_EXPAND, HG_EXPAND), lambda c: (c, 0, 0, 0))],
        out_shape=[jax.ShapeDtypeStruct((S, D), BF16),
                   jax.ShapeDtypeStruct((nc, H, HG_EXPAND, HG_EXPAND), F32)],
        scratch_shapes=[pltpu.VMEM((H, HG_EXPAND, HG_EXPAND), F32)],
        compiler_params=_params(("arbitrary",)),
    )(proj, logits, ng, *consts)


def hgrn_bwd(proj, logits, ng, states, dout, layer, *, name):
    S = proj.shape[0]
    D = proj.shape[1] // 4
    H = D // HG_EXPAND
    C = HG_CHUNK
    nc = S // C
    depth = logits.shape[0]
    consts = _hgrn_consts()
    rev = lambda c: nc - 1 - c

    def body(p_ref, lg_ref, ng_ref, ops3_ref, opst3_ref, mk_ref, st_in_ref, do_ref,
             dp_ref, dlg_ref, dng_ref, dst_ref):
        @pl.when(pl.program_id(0) == 0)
        def _():
            dst_ref[...] = jnp.zeros_like(dst_ref)
            dlg_ref[...] = jnp.zeros_like(dlg_ref)
            dng_ref[...] = jnp.zeros_like(dng_ref)

        ops3, opst3, mk = ops3_ref[...], opst3_ref[...], mk_ref[...]
        for h in range(H):
            sl = slice(h * HG_EXPAND, (h + 1) * HG_EXPAND)
            cols = _hgrn_cols(D, h)
            fn = functools.partial(_hgrn_head, ops3=ops3, opst3=opst3, mk=mk, layer=layer)
            _, vjp = jax.vjp(fn, p_ref[:, cols[0]], p_ref[:, cols[1]], p_ref[:, cols[2]], p_ref[:, cols[3]],
                             st_in_ref[h], lg_ref[:, sl], ng_ref[...])
            dq, df, di, dg, dst, dlg, dng = vjp((do_ref[:, sl], dst_ref[h]))
            dp_ref[:, cols[0]] = dq
            dp_ref[:, cols[1]] = df
            dp_ref[:, cols[2]] = di
            dp_ref[:, cols[3]] = dg
            dst_ref[h] = dst
            dlg_ref[:, sl] += dlg
            dng_ref[...] += dng

    in_specs = _hgrn_in_specs(D, depth, rev)
    in_specs += [pl.BlockSpec((None, H, HG_EXPAND, HG_EXPAND), lambda c: (rev(c), 0, 0, 0)),
                 pl.BlockSpec((C, D), lambda c: (rev(c), 0))]
    return pl.pallas_call(
        body, name=name, grid=(nc,), in_specs=in_specs,
        out_specs=[pl.BlockSpec((C, 4 * D), lambda c: (rev(c), 0)),
                   pl.BlockSpec((depth, D), lambda c: (0, 0)), pl.BlockSpec((1, HG_EXPAND), lambda c: (0, 0))],
        out_shape=[jax.ShapeDtypeStruct((S, 4 * D), F32), jax.ShapeDtypeStruct((depth, D), F32),
                   jax.ShapeDtypeStruct((1, HG_EXPAND), F32)],
        scratch_shapes=[pltpu.VMEM((H, HG_EXPAND, HG_EXPAND), F32)],
        compiler_params=_params(("arbitrary",)),
    )(proj, logits, ng, *consts, states, dout)


def _edge_rows(edge, shape):
    return jnp.broadcast_to(edge[None], (shape[0] // SUBLANES, SUBLANES, shape[1])).reshape(shape)


def _shift_down(u, prev, shift):
    rolled = pltpu.roll(u, shift, axis=0)
    edge = _edge_rows(pltpu.roll(prev, shift, axis=0), u.shape)
    rows = lax.broadcasted_iota(jnp.int32, u.shape, 0)
    return jnp.where(rows < shift, edge, rolled)


def _shift_up(x, nxt, shift):
    tm = x.shape[0]
    rolled = pltpu.roll(x, tm - shift, axis=0)
    edge = _edge_rows(pltpu.roll(nxt, SUBLANES - shift, axis=0), x.shape)
    rows = lax.broadcasted_iota(jnp.int32, x.shape, 0)
    return jnp.where(rows >= tm - shift, edge, rolled)


def _conv(u, prev, w_ref, b_ref, half):
    u1 = _shift_down(u, prev, 1)
    u2 = _shift_down(u, prev, 2)
    c = b_ref[half] + w_ref[half, 0:1, :] * u2 + w_ref[half, 1:2, :] * u1 + w_ref[half, 2:3, :] * u
    return c, u1, u2


def _conv_specs(tm, tc, l):
    nprev = lambda m: jnp.maximum(m * (tm // SUBLANES) - 1, 0)
    tile = pl.BlockSpec((2, tm, tc), lambda j, m: (0, m, j))
    halo = pl.BlockSpec((2, SUBLANES, tc), lambda j, m: (0, nprev(m), j))
    wspec = pl.BlockSpec((None, 2, 3, tc), lambda j, m: (l, 0, 0, j))
    bspec = pl.BlockSpec((None, 2, 1, tc), lambda j, m: (l, 0, 0, j))
    return tile, halo, wspec, bspec


def convact_fwd(u, cw, cb, l, *, name):
    _, S, Fh = u.shape
    tm = _pick(S, (256, 128))
    tc = _pick(Fh, (1408, 512, 256, 128))
    tile, halo, wspec, bspec = _conv_specs(tm, tc, l)

    def body(u_ref, p_ref, w_ref, b_ref, o_ref, ot_ref):
        keep = (pl.program_id(1) > 0).astype(F32)
        cg, _, _ = _conv(u_ref[0], p_ref[0] * keep, w_ref, b_ref, 0)
        cv, _, _ = _conv(u_ref[1], p_ref[1] * keep, w_ref, b_ref, 1)
        act = _silu(cg) * cv
        o_ref[...] = act.astype(BF16)
        ot_ref[...] = act.T.astype(BF16)

    return pl.pallas_call(
        body, name=name, grid=(Fh // tc, S // tm), in_specs=[tile, halo, wspec, bspec],
        out_specs=[pl.BlockSpec((tm, tc), lambda j, m: (m, j)), pl.BlockSpec((tc, tm), lambda j, m: (j, m))],
        out_shape=[jax.ShapeDtypeStruct((S, Fh), BF16), jax.ShapeDtypeStruct((Fh, S), BF16)],
        compiler_params=_params(("parallel", "parallel")),
    )(u, u, cw, cb)


def convact_bwd(u, cw, cb, l, da, *, name):
    _, S, Fh = u.shape
    tm = _pick(S, (256, 128))
    tc = _pick(Fh, (1408, 512, 256, 128))
    tile, halo, wspec, bspec = _conv_specs(tm, tc, l)

    def body(u_ref, p_ref, w_ref, b_ref, da_ref, dc_ref, dw_ref, db_ref):
        first = pl.program_id(1) == 0
        keep = jnp.logical_not(first).astype(F32)
        ug, uv = u_ref[0], u_ref[1]
        cg, g1, g2 = _conv(ug, p_ref[0] * keep, w_ref, b_ref, 0)
        cv, v1, v2 = _conv(uv, p_ref[1] * keep, w_ref, b_ref, 1)
        da_t = da_ref[...]
        sg = _sigmoid(cg)
        dcv = da_t * (cg * sg)
        dcg = da_t * cv * (sg * (1.0 + cg * (1.0 - sg)))
        dc_ref[0] = dcg
        dc_ref[1] = dcv

        @pl.when(first)
        def _():
            dw_ref[...] = jnp.zeros_like(dw_ref)
            db_ref[...] = jnp.zeros_like(db_ref)

        rs = lambda t: jnp.sum(t, axis=0, keepdims=True)
        for half, dc, taps in ((0, dcg, (g2, g1, ug)), (1, dcv, (v2, v1, uv))):
            for j in range(3):
                dw_ref[half, j:j + 1, :] += rs(dc * taps[j])
            db_ref[half] += rs(dc)

    return pl.pallas_call(
        body, name=name, grid=(Fh // tc, S // tm),
        in_specs=[tile, halo, wspec, bspec, pl.BlockSpec((tm, tc), lambda j, m: (m, j))],
        out_specs=[tile, pl.BlockSpec((2, 3, tc), lambda j, m: (0, 0, j)),
                   pl.BlockSpec((2, 1, tc), lambda j, m: (0, 0, j))],
        out_shape=[jax.ShapeDtypeStruct(u.shape, F32), jax.ShapeDtypeStruct((2, 3, Fh), F32),
                   jax.ShapeDtypeStruct((2, 1, Fh), F32)],
        compiler_params=_params(("parallel", "arbitrary")),
    )(u, u, cw, cb, da)


def conv_input_grad(dc, cw, l, *, name):
    _, S, Fh = dc.shape
    tm = _pick(S, (256, 128))
    tc = _pick(Fh, (1408, 512, 256, 128))
    nblk = S // SUBLANES
    nm = S // tm

    def body(dc_ref, n_ref, w_ref, du_ref):
        keep = (pl.program_id(1) < nm - 1).astype(F32)
        for half in range(2):
            t = dc_ref[half]
            nxt = n_ref[half] * keep
            du = (w_ref[half, 2:3, :] * t + w_ref[half, 1:2, :] * _shift_up(t, nxt, 1)
                  + w_ref[half, 0:1, :] * _shift_up(t, nxt, 2))
            du_ref[half] = du.astype(BF16)

    tile = pl.BlockSpec((2, tm, tc), lambda j, m: (0, m, j))
    halo = pl.BlockSpec((2, SUBLANES, tc),
                        lambda j, m: (0, jnp.minimum((m + 1) * (tm // SUBLANES), nblk - 1), j))
    return pl.pallas_call(
        body, name=name, grid=(Fh // tc, nm),
        in_specs=[tile, halo, pl.BlockSpec((None, 2, 3, tc), lambda j, m: (l, 0, 0, j))],
        out_specs=tile, out_shape=jax.ShapeDtypeStruct(dc.shape, BF16),
        compiler_params=_params(("parallel", "parallel")),
    )(dc, dc, cw)


def _adamw_math(w, g, m, v):
    m = ADAM_B1 * m + (1.0 - ADAM_B1) * g
    v = ADAM_B2 * v + (1.0 - ADAM_B2) * (g * g)
    m_hat = m / (1.0 - ADAM_B1 ** ADAM_STEP)
    v_hat = v / (1.0 - ADAM_B2 ** ADAM_STEP)
    delta = -ADAM_LR * (m_hat / (jnp.sqrt(v_hat) + ADAM_EPS) + ADAM_WD * w)
    return delta, m, v


def _row_tile(R, Ccols, n_streams):
    for tr in (512, 256, 128, 64, 32, 16, 8):
        if R % tr == 0 and tr * Ccols * 4 * n_streams * 2 <= VMEM_LIMIT_BYTES // 2:
            return tr
    return R


def adamw(w, g, m, v, *, name):
    R, Ccols = w.shape
    tr = _row_tile(R, Ccols, 7)

    def body(w_ref, g_ref, m_ref, v_ref, d_ref, mo_ref, vo_ref):
        d, mn, vn = _adamw_math(w_ref[...], g_ref[...], m_ref[...], v_ref[...])
        d_ref[...] = d
        mo_ref[...] = mn
        vo_ref[...] = vn

    spec = pl.BlockSpec((tr, Ccols), lambda i: (i, 0))
    shp = jax.ShapeDtypeStruct((R, Ccols), F32)
    return pl.pallas_call(
        body, name=name, grid=(R // tr,), in_specs=[spec] * 4, out_specs=[spec] * 3,
        out_shape=[shp, shp, shp], compiler_params=_params(("parallel",)),
    )(w, g, m, v)


def add_n(xs, *, name):
    R, Ccols = xs[0].shape
    tr = _row_tile(R, Ccols, len(xs) + 1)

    def body(*refs):
        acc = refs[0][...]
        for r in refs[1:-1]:
            acc = acc + r[...]
        refs[-1][...] = acc

    spec = pl.BlockSpec((tr, Ccols), lambda i: (i, 0))
    return pl.pallas_call(
        body, name=name, grid=(R // tr,), in_specs=[spec] * len(xs), out_specs=spec,
        out_shape=jax.ShapeDtypeStruct((R, Ccols), F32), compiler_params=_params(("parallel",)),
    )(*xs)


def _position():
    return lax.axis_index("x"), lax.axis_index("y"), lax.axis_index("c")


def _other_chips(x, y):
    return [(1 - x, y), (x, 1 - y), (1 - x, 1 - y)]


HBM_ANY = pl.BlockSpec(memory_space=pl.ANY)
COMM_PARAMS = pltpu.CompilerParams(has_side_effects=True)


def _col_half(ref, h, chip, n):
    K = ref.shape[1]
    return ref.at[:, pl.ds(h * (K // 2), K // 2), pl.ds(chip * n, n)]


def _row_half(ref, h, chip, r):
    return ref.at[:, pl.ds(chip * r + h * (r // 2), r // 2), :]


def all_gather_weights(col_shards, row_shards, whole_shards, *, name):
    n_col, n_row, n_whole = len(col_shards), len(row_shards), len(whole_shards)
    n_half = n_col + n_row
    n_in = n_half + n_whole

    def body(*refs):
        ins = refs[:n_in]
        outs = refs[n_in:2 * n_in]
        ici_send, ici_recv, d2d_send, d2d_recv, w_send, w_recv, local_sem = refs[2 * n_in:]
        x, y, c = _position()
        me = 2 * x + y
        chips = _other_chips(x, y)
        chip_ids = [2 * cx + cy for cx, cy in chips]
        sibling = (x, y, 1 - c)

        def half_src(i, h):
            rows = ins[i].shape[1] // 2
            return ins[i].at[:, pl.ds(h * rows, rows), :]

        def half_dst(i, h, chip):
            if i < n_col:
                return _col_half(outs[i], h, chip, ins[i].shape[2])
            return _row_half(outs[i], h, chip, ins[i].shape[1])

        def whole_dst(i, chip):
            n = ins[i].shape[2]
            return outs[i].at[:, :, pl.ds(chip * n, n)]

        local = []
        for i in range(n_half):
            for h in range(2):
                local.append(pltpu.make_async_copy(half_src(i, h), half_dst(i, h, me), local_sem.at[2 * i + h]))
        for i in range(n_half, n_in):
            local.append(pltpu.make_async_copy(ins[i], whole_dst(i, me), local_sem.at[n_half + i]))
        for cp in local:
            cp.start()

        def ici(i, k, src_chip):
            return pltpu.make_async_remote_copy(
                src_ref=half_src(i, c), dst_ref=half_dst(i, c, src_chip),
                send_sem=ici_send.at[i, k], recv_sem=ici_recv.at[i, k],
                device_id=(*chips[k], c), device_id_type=MESH)

        def whole(i, k, src_chip):
            return pltpu.make_async_remote_copy(
                src_ref=ins[i], dst_ref=whole_dst(i, src_chip),
                send_sem=w_send.at[i - n_half, k], recv_sem=w_recv.at[i - n_half, k],
                device_id=(*chips[k], c), device_id_type=MESH)

        def d2d(i, k, h):
            return pltpu.make_async_remote_copy(
                src_ref=half_dst(i, h, chip_ids[k]), dst_ref=half_dst(i, h, chip_ids[k]),
                send_sem=d2d_send.at[i, k], recv_sem=d2d_recv.at[i, k],
                device_id=sibling, device_id_type=MESH)

        sent = [ici(i, k, me) for i in range(n_half) for k in range(3)]
        sent += [whole(i, k, me) for i in range(n_half, n_in) for k in range(3)]
        for cp in sent:
            cp.start()
        passed = []
        for i in range(n_half):
            for k in range(3):
                ici(i, k, chip_ids[k]).wait_recv()
                fwd = d2d(i, k, c)
                fwd.start()
                passed.append(fwd)
        for i in range(n_half):
            for k in range(3):
                d2d(i, k, 1 - c).wait_recv()
        for i in range(n_half, n_in):
            for k in range(3):
                whole(i, k, chip_ids[k]).wait_recv()
        for cp in sent + passed:
            cp.wait_send()
        for cp in local:
            cp.wait()

    ins = list(col_shards) + list(row_shards) + list(whole_shards)
    out_shape = []
    for a in col_shards:
        out_shape.append(jax.ShapeDtypeStruct((a.shape[0], a.shape[1], N_CHIPS * a.shape[2]), a.dtype))
    for a in row_shards:
        out_shape.append(jax.ShapeDtypeStruct((a.shape[0], N_CHIPS * a.shape[1], a.shape[2]), a.dtype))
    for a in whole_shards:
        out_shape.append(jax.ShapeDtypeStruct((a.shape[0], a.shape[1], N_CHIPS * a.shape[2]), a.dtype))
    return pl.pallas_call(
        body, name=name, in_specs=[HBM_ANY] * n_in, out_specs=[HBM_ANY] * n_in, out_shape=out_shape,
        scratch_shapes=[pltpu.SemaphoreType.DMA((n_half, 3)), pltpu.SemaphoreType.DMA((n_half, 3)),
                        pltpu.SemaphoreType.DMA((n_half, 3)), pltpu.SemaphoreType.DMA((n_half, 3)),
                        pltpu.SemaphoreType.DMA((max(n_whole, 1), 3)), pltpu.SemaphoreType.DMA((max(n_whole, 1), 3)),
                        pltpu.SemaphoreType.DMA((2 * n_half + n_in,))],
        compiler_params=COMM_PARAMS,
    )(*ins)


def sibling_swap_halves(col_grads, row_grads, *, name):
    n_col, n_row = len(col_grads), len(row_grads)
    n_in = n_col + n_row

    def body(*refs):
        ins = refs[:n_in]
        outs = refs[n_in:2 * n_in]
        send_sem, recv_sem = refs[2 * n_in:]
        x, y, c = _position()
        sibling = (x, y, 1 - c)
        h = 1 - c
        cps = []
        for i in range(n_in):
            if i < n_col:
                K = ins[i].shape[1]
                cps.append(pltpu.make_async_remote_copy(
                    src_ref=ins[i].at[:, pl.ds(h * (K // 2), K // 2), :], dst_ref=outs[i],
                    send_sem=send_sem.at[i, 0], recv_sem=recv_sem.at[i, 0],
                    device_id=sibling, device_id_type=MESH))
            else:
                r = ins[i].shape[1] // N_CHIPS
                for q in range(N_CHIPS):
                    cps.append(pltpu.make_async_remote_copy(
                        src_ref=ins[i].at[:, pl.ds(q * r + h * (r // 2), r // 2), :], dst_ref=outs[i].at[:, q],
                        send_sem=send_sem.at[i, q], recv_sem=recv_sem.at[i, q],
                        device_id=sibling, device_id_type=MESH))
        for cp in cps:
            cp.start()
        for cp in cps:
            cp.wait()

    out_shape = []
    for a in col_grads:
        out_shape.append(jax.ShapeDtypeStruct((a.shape[0], a.shape[1] // 2, a.shape[2]), a.dtype))
    for a in row_grads:
        r = a.shape[1] // N_CHIPS
        out_shape.append(jax.ShapeDtypeStruct((a.shape[0], N_CHIPS, r // 2, a.shape[2]), a.dtype))
    return pl.pallas_call(
        body, name=name, in_specs=[HBM_ANY] * n_in, out_specs=[HBM_ANY] * n_in, out_shape=out_shape,
        scratch_shapes=[pltpu.SemaphoreType.DMA((n_in, N_CHIPS)), pltpu.SemaphoreType.DMA((n_in, N_CHIPS))],
        compiler_params=COMM_PARAMS,
    )(*col_grads, *row_grads)


def scatter_to_chips(col_halves, row_halves, *, name):
    n_col, n_row = len(col_halves), len(row_halves)
    n_in = n_col + n_row

    def body(*refs):
        ins = refs[:n_in]
        outs = refs[n_in:2 * n_in]
        send_sem, recv_sem = refs[2 * n_in:]
        x, y, c = _position()
        chips = _other_chips(x, y)
        cps = []
        for i in range(n_in):
            for k in range(3):
                to = 2 * chips[k][0] + chips[k][1]
                if i < n_col:
                    n = ins[i].shape[2] // N_CHIPS
                    src = ins[i].at[:, :, pl.ds(to * n, n)]
                else:
                    src = ins[i].at[:, to]
                cps.append(pltpu.make_async_remote_copy(
                    src_ref=src, dst_ref=outs[i].at[k], send_sem=send_sem.at[i, k], recv_sem=recv_sem.at[i, k],
                    device_id=(*chips[k], c), device_id_type=MESH))
        for cp in cps:
            cp.start()
        for cp in cps:
            cp.wait()

    out_shape = []
    for a in col_halves:
        out_shape.append(jax.ShapeDtypeStruct((3, a.shape[0], a.shape[1], a.shape[2] // N_CHIPS), a.dtype))
    for a in row_halves:
        out_shape.append(jax.ShapeDtypeStruct((3, a.shape[0], a.shape[2], a.shape[3]), a.dtype))
    return pl.pallas_call(
        body, name=name, in_specs=[HBM_ANY] * n_in, out_specs=[HBM_ANY] * n_in, out_shape=out_shape,
        scratch_shapes=[pltpu.SemaphoreType.DMA((n_in, 3)), pltpu.SemaphoreType.DMA((n_in, 3))],
        compiler_params=COMM_PARAMS,
    )(*col_halves, *row_halves)


def sibling_join_halves(halves, *, name):
    n_in = len(halves)

    def body(*refs):
        ins = refs[:n_in]
        outs = refs[n_in:2 * n_in]
        send_sem, recv_sem, local_sem = refs[2 * n_in:]
        x, y, c = _position()
        sibling = (x, y, 1 - c)

        def rows(i, h):
            R2 = ins[i].shape[1]
            return outs[i].at[:, pl.ds(h * R2, R2), :]

        local = [pltpu.make_async_copy(ins[i], rows(i, c), local_sem.at[i]) for i in range(n_in)]
        sends = [pltpu.make_async_remote_copy(src_ref=ins[i], dst_ref=rows(i, c), send_sem=send_sem.at[i],
                                              recv_sem=recv_sem.at[i], device_id=sibling, device_id_type=MESH)
                 for i in range(n_in)]
        lands = [pltpu.make_async_remote_copy(src_ref=ins[i], dst_ref=rows(i, 1 - c), send_sem=send_sem.at[i],
                                              recv_sem=recv_sem.at[i], device_id=sibling, device_id_type=MESH)
                 for i in range(n_in)]
        for cp in local + sends:
            cp.start()
        for i in range(n_in):
            sends[i].wait_send()
            lands[i].wait_recv()
        for cp in local:
            cp.wait()

    out_shape = [jax.ShapeDtypeStruct((a.shape[0], 2 * a.shape[1], a.shape[2]), a.dtype) for a in halves]
    return pl.pallas_call(
        body, name=name, in_specs=[HBM_ANY] * n_in, out_specs=[HBM_ANY] * n_in, out_shape=out_shape,
        scratch_shapes=[pltpu.SemaphoreType.DMA((n_in,)), pltpu.SemaphoreType.DMA((n_in,)),
                        pltpu.SemaphoreType.DMA((n_in,))],
        compiler_params=COMM_PARAMS,
    )(*halves)


def all_reduce_small(v, *, name):
    R, Ccols = v.shape

    def body(v_ref, o_ref, slots, send_sem, recv_sem):
        x, y, c = _position()
        me = 4 * x + 2 * y + c
        slots[me] = v_ref[...]

        def peer(d):
            return x ^ (d >> 2), y ^ ((d >> 1) & 1), c ^ (d & 1)

        def copy(d, slot):
            return pltpu.make_async_remote_copy(
                src_ref=v_ref, dst_ref=slots.at[slot], send_sem=send_sem.at[d], recv_sem=recv_sem.at[d],
                device_id=peer(d), device_id_type=MESH)

        sends = [copy(d, me) for d in range(1, N_DEV)]
        for cp in sends:
            cp.start()
        for d in range(1, N_DEV):
            px, py, pc = peer(d)
            copy(d, 4 * px + 2 * py + pc).wait_recv()
        for cp in sends:
            cp.wait_send()
        acc = slots[0]
        for d in range(1, N_DEV):
            acc = acc + slots[d]
        o_ref[...] = acc

    return pl.pallas_call(
        body, name=name,
        in_specs=[pl.BlockSpec(memory_space=pltpu.VMEM)], out_specs=pl.BlockSpec(memory_space=pltpu.VMEM),
        out_shape=jax.ShapeDtypeStruct((R, Ccols), F32),
        scratch_shapes=[pltpu.VMEM((N_DEV, R, Ccols), F32), pltpu.SemaphoreType.DMA((N_DEV,)),
                        pltpu.SemaphoreType.DMA((N_DEV,))],
        compiler_params=COMM_PARAMS,
    )(v)


def _pack_small(parts):
    flat = jnp.concatenate([p.reshape(-1).astype(F32) for p in parts])
    n = flat.shape[0]
    rows = -(-n // LANES)
    rows = -(-rows // SUBLANES) * SUBLANES
    return jnp.pad(flat, (0, rows * LANES - n)).reshape(rows, LANES)


def _unpack_small(packed, like):
    flat = packed.reshape(-1)
    out, off = [], 0
    for p in like:
        out.append(flat[off:off + p.size].reshape(p.shape))
        off += p.size
    return out


def kernel(x, norm_mix, norm_ffn, norm_final, attn_w_in, attn_w_out, attn_sinks, hgrn_w_in, hgrn_w_out, hgrn_norm, hgrn_lb_logits, ffn_w_up, ffn_conv_w, ffn_conv_b, ffn_w_down, loss_target, m_norm_mix, m_norm_ffn, m_norm_final, m_attn_w_in, m_attn_w_out, m_attn_sinks, m_hgrn_w_in, m_hgrn_w_out, m_hgrn_norm, m_hgrn_lb_logits, m_ffn_w_up, m_ffn_conv_w, m_ffn_conv_b, m_ffn_w_down, v_norm_mix, v_norm_ffn, v_norm_final, v_attn_w_in, v_attn_w_out, v_attn_sinks, v_hgrn_w_in, v_hgrn_w_out, v_hgrn_norm, v_hgrn_lb_logits, v_ffn_w_up, v_ffn_conv_w, v_ffn_conv_b, v_ffn_w_down):
    S, D = x.shape[1], x.shape[2]
    depth = norm_mix.shape[0]
    n_q = D // HEAD_DIM
    n_kv = n_q // Q_PER_KV
    Fh = ffn_w_down.shape[1] * N_CHIPS
    h = x.reshape(S, D)
    target = loss_target.reshape(S, D)

    col_names = ["attn_w_in", "hgrn_w_in", "ffn_w_up"]
    row_names = ["attn_w_out", "hgrn_w_out", "ffn_w_down"]
    col_w = [attn_w_in.astype(BF16), hgrn_w_in.astype(BF16), ffn_w_up.astype(BF16)]
    row_w = [attn_w_out.astype(BF16), hgrn_w_out.astype(BF16), ffn_w_down.astype(BF16)]
    gathered = all_gather_weights(col_w, row_w, [ffn_conv_w], name="all_gather_weights")
    W = dict(zip(col_names + row_names + ["ffn_conv_w"], gathered))
    cw = W["ffn_conv_w"].reshape(depth, 3, 2, Fh).transpose(0, 2, 1, 3)
    cb = ffn_conv_b.reshape(depth, 2, 1, Fh)
    g_mix = norm_mix.reshape(depth, 1, D)
    g_ffn = norm_ffn.reshape(depth, 1, D)

    slopes = jnp.exp2(-8.0 * jnp.arange(1, n_q + 1, dtype=F32) / n_q).reshape(n_kv, Q_PER_KV)
    slope_col = jnp.repeat(slopes, WINDOW, axis=1).reshape(n_kv, Q_PER_KV * WINDOW, 1)

    def to_heads(t, nh):
        return t.reshape(S, nh, HEAD_DIM).transpose(1, 0, 2)

    def from_heads(t):
        return t.transpose(1, 0, 2).reshape(S, -1)

    saved = []
    for layer in range(depth):
        idx = layer // 2
        rec = {"h_mix": h}
        hn, rec["hn_t"] = rmsnorm_fwd(h, g_mix, layer, name=f"norm_mix_fwd{layer}")
        if layer % 2 == 0:
            proj = mm_nn(hn, W["attn_w_in"], idx, name=f"attn_in_fwd{layer}")
            q = to_heads(proj[:, :n_q * HEAD_DIM], n_q).reshape(n_kv, Q_PER_KV, S, HEAD_DIM)
            k = to_heads(proj[:, n_q * HEAD_DIM:(n_q + n_kv) * HEAD_DIM], n_kv)
            v = to_heads(proj[:, (n_q + n_kv) * HEAD_DIM:], n_kv)
            sink_col = jnp.repeat(attn_sinks[idx].reshape(n_kv, Q_PER_KV), WINDOW, axis=1)
            sink_col = sink_col.reshape(n_kv, Q_PER_KV * WINDOW, 1)
            o_heads = attn_fwd(q, k, v, sink_col, slope_col, name=f"attn_fwd{layer}")
            o = from_heads(o_heads.reshape(n_q, S, HEAD_DIM))
            rec.update(q=q, k=k, v=v, sink_col=sink_col, o=o)
            h = mm_nn(o, W["attn_w_out"], idx, name=f"attn_out_fwd{layer}", res=h)
        else:
            proj = mm_nn(hn, W["hgrn_w_in"], idx, name=f"hgrn_in_fwd{layer}")
            ng = hgrn_norm[idx].reshape(1, HG_EXPAND)
            o, states = hgrn_fwd(proj, hgrn_lb_logits, ng, layer, name=f"hgrn_fwd{layer}")
            rec.update(proj=proj, ng=ng, states=states, o=o)
            h = mm_nn(o, W["hgrn_w_out"], idx, name=f"hgrn_out_fwd{layer}", res=h)
        rec["h_ffn"] = h
        hn2, rec["hn2_t"] = rmsnorm_fwd(h, g_ffn, layer, name=f"norm_ffn_fwd{layer}")
        u = mm_nn(hn2, W["ffn_w_up"], layer, name=f"ffn_up_fwd{layer}", split_out=True)
        act, rec["act_t"] = convact_fwd(u, cw, cb, layer, name=f"ffn_act_fwd{layer}")
        rec["u"] = u
        h = mm_nn(act, W["ffn_w_down"], layer, name=f"ffn_down_fwd{layer}", res=h)
        saved.append(rec)

    loss_part, dh, d_norm_final = loss_head(h, norm_final.reshape(1, D), target, name="loss_head")

    n_attn, n_hgrn = attn_w_in.shape[0], hgrn_w_in.shape[0]
    G = dict(attn_w_in=None, attn_w_out=None, hgrn_w_in=None, hgrn_w_out=None, ffn_w_up=None, ffn_w_down=None)
    d_norm_mix, d_norm_ffn = [None] * depth, [None] * depth
    d_conv_w, d_conv_b = [None] * depth, [None] * depth
    d_sinks, d_hgrn_norm = [None] * n_attn, [None] * n_hgrn
    d_logits = jnp.zeros_like(hgrn_lb_logits)
    for layer in reversed(range(depth)):
        idx = layer // 2
        rec = saved[layer]
        G["ffn_w_down"] = mm_dw(rec["act_t"], dh, G["ffn_w_down"], layer, depth, name=f"ffn_down_dw{layer}")
        dact = mm_nt(dh, W["ffn_w_down"], layer, name=f"ffn_down_dx{layer}")
        dc, dcw, dcb = convact_bwd(rec["u"], cw, cb, layer, dact, name=f"ffn_act_bwd{layer}")
        d_conv_w[layer] = dcw.transpose(1, 0, 2).reshape(3, 2 * Fh)
        d_conv_b[layer] = dcb.reshape(2 * Fh)
        du = conv_input_grad(dc, cw, layer, name=f"ffn_conv_dx{layer}")
        G["ffn_w_up"] = mm_dw(rec["hn2_t"], du, G["ffn_w_up"], layer, depth, name=f"ffn_up_dw{layer}", split_b=True)
        dhn2 = mm_nt(du, W["ffn_w_up"], layer, name=f"ffn_up_dx{layer}", split_in=True)
        dh, dg = rmsnorm_bwd(rec["h_ffn"], g_ffn, layer, dhn2, dh, name=f"norm_ffn_bwd{layer}")
        d_norm_ffn[layer] = dg.reshape(D)
        if layer % 2 == 0:
            G["attn_w_out"] = mm_tn(rec["o"], dh, G["attn_w_out"], idx, n_attn, name=f"attn_out_dw{layer}")
            do = mm_nt(dh, W["attn_w_out"], idx, name=f"attn_out_dx{layer}", out_dtype=BF16)
            do_heads = to_heads(do, n_q).reshape(n_kv, Q_PER_KV, S, HEAD_DIM)
            dq, dkc, dkp, dvc, dvp, dsink = attn_bwd(rec["q"], rec["k"], rec["v"], rec["sink_col"], slope_col,
                                                     do_heads, name=f"attn_bwd{layer}")
            shift = lambda t: jnp.concatenate([t[:, WINDOW:], jnp.zeros_like(t[:, :WINDOW])], axis=1)
            dk = dkc + shift(dkp)
            dv = dvc + shift(dvp)
            dproj = jnp.concatenate([from_heads(dq.reshape(n_q, S, HEAD_DIM)), from_heads(dk), from_heads(dv)], axis=1)
            d_sinks[idx] = jnp.sum(dsink.reshape(n_kv, Q_PER_KV, WINDOW), axis=-1).reshape(n_q)
            G["attn_w_in"] = mm_dw(rec["hn_t"], dproj, G["attn_w_in"], idx, n_attn, name=f"attn_in_dw{layer}")
            dhn = mm_nt(dproj, W["attn_w_in"], idx, name=f"attn_in_dx{layer}")
        else:
            G["hgrn_w_out"] = mm_tn(rec["o"], dh, G["hgrn_w_out"], idx, n_hgrn, name=f"hgrn_out_dw{layer}")
            do = mm_nt(dh, W["hgrn_w_out"], idx, name=f"hgrn_out_dx{layer}")
            dproj, dlg, dng = hgrn_bwd(rec["proj"], hgrn_lb_logits, rec["ng"], rec["states"], do, layer,
                                       name=f"hgrn_bwd{layer}")
            d_logits = d_logits + dlg
            d_hgrn_norm[idx] = dng.reshape(HG_EXPAND)
            G["hgrn_w_in"] = mm_dw(rec["hn_t"], dproj, G["hgrn_w_in"], idx, n_hgrn, name=f"hgrn_in_dw{layer}")
            dhn = mm_nt(dproj, W["hgrn_w_in"], idx, name=f"hgrn_in_dx{layer}")
        dh, dg = rmsnorm_bwd(rec["h_mix"], g_mix, layer, dhn, dh, name=f"norm_mix_bwd{layer}")
        d_norm_mix[layer] = dg.reshape(D)
    grad_x = dh.reshape(x.shape)

    small_w = [norm_mix, norm_ffn, norm_final, attn_sinks, hgrn_norm, hgrn_lb_logits, ffn_conv_b]
    small_m = [m_norm_mix, m_norm_ffn, m_norm_final, m_attn_sinks, m_hgrn_norm, m_hgrn_lb_logits, m_ffn_conv_b]
    small_v = [v_norm_mix, v_norm_ffn, v_norm_final, v_attn_sinks, v_hgrn_norm, v_hgrn_lb_logits, v_ffn_conv_b]
    small_grads = [jnp.stack(d_norm_mix), jnp.stack(d_norm_ffn), d_norm_final.reshape(D), jnp.stack(d_sinks),
                   jnp.stack(d_hgrn_norm), d_logits, jnp.stack(d_conv_b)]
    reduced = all_reduce_small(_pack_small(small_grads + [loss_part]), name="all_reduce_small")
    small_g = _unpack_small(reduced, small_w + [loss_part])
    loss = small_g.pop().reshape(())
    pad = [jnp.zeros((1, 1), F32)]
    sd, sm_, sv_ = adamw(_pack_small(small_w + pad), reduced, _pack_small(small_m + pad),
                         _pack_small(small_v + pad), name="adamw_small")
    small_d = _unpack_small(sd, small_w)
    small_nm = _unpack_small(sm_, small_w)
    small_nv = _unpack_small(sv_, small_w)

    col_g = [G["attn_w_in"], G["hgrn_w_in"], G["ffn_w_up"]]
    row_g = [G["attn_w_out"], G["hgrn_w_out"], G["ffn_w_down"]]
    theirs = sibling_swap_halves(col_g, row_g, name="rs_sibling_swap")
    x_, y_, c_ = _position()
    me = 2 * x_ + y_
    pair = []
    for i, g in enumerate(col_g):
        L_, K_, N_ = g.shape
        mine = lax.dynamic_slice_in_dim(g, c_ * (K_ // 2), K_ // 2, axis=1)
        pair.append(add_n([mine.reshape(-1, N_), theirs[i].reshape(-1, N_)], name=f"rs_pair_add_col{i}")
                    .reshape(L_, K_ // 2, N_))
    for i, g in enumerate(row_g):
        L_, R4, D_ = g.shape
        r = R4 // N_CHIPS
        mine = lax.dynamic_slice_in_dim(g.reshape(L_, N_CHIPS, 2, r // 2, D_), c_, 1, axis=2)
        pair.append(add_n([mine.reshape(-1, D_), theirs[3 + i].reshape(-1, D_)], name=f"rs_pair_add_row{i}")
                    .reshape(L_, N_CHIPS, r // 2, D_))
    landed = scatter_to_chips(pair[:3], pair[3:], name="rs_scatter")
    halves = []
    for i in range(3):
        L_, K2, N_ = pair[i].shape
        n = N_ // N_CHIPS
        own = lax.dynamic_slice_in_dim(pair[i], me * n, n, axis=2)
        parts = [own.reshape(-1, n)] + [landed[i][k].reshape(-1, n) for k in range(3)]
        halves.append(add_n(parts, name=f"rs_chip_add_col{i}").reshape(L_, K2, n))
    for i in range(3):
        L_, _, r2, D_ = pair[3 + i].shape
        own = lax.dynamic_slice_in_dim(pair[3 + i], me, 1, axis=1)
        parts = [own.reshape(-1, D_)] + [landed[3 + i][k].reshape(-1, D_) for k in range(3)]
        halves.append(add_n(parts, name=f"rs_chip_add_row{i}").reshape(L_, r2, D_))
    shards = sibling_join_halves(halves, name="rs_sibling_join")
    big_names = col_names + row_names
    big_w = dict(attn_w_in=attn_w_in, hgrn_w_in=hgrn_w_in, ffn_w_up=ffn_w_up, attn_w_out=attn_w_out,
                 hgrn_w_out=hgrn_w_out, ffn_w_down=ffn_w_down)
    big_m = dict(attn_w_in=m_attn_w_in, hgrn_w_in=m_hgrn_w_in, ffn_w_up=m_ffn_w_up, attn_w_out=m_attn_w_out,
                 hgrn_w_out=m_hgrn_w_out, ffn_w_down=m_ffn_w_down)
    big_v = dict(attn_w_in=v_attn_w_in, hgrn_w_in=v_hgrn_w_in, ffn_w_up=v_ffn_w_up, attn_w_out=v_attn_w_out,
                 hgrn_w_out=v_hgrn_w_out, ffn_w_down=v_ffn_w_down)
    grads, deltas, new_m, new_v = {}, {}, {}, {}
    for name_, gshard in zip(big_names, shards):
        w_ = big_w[name_]
        cols = w_.shape[-1]
        d_, m_, v_ = adamw(w_.reshape(-1, cols), gshard.reshape(-1, cols), big_m[name_].reshape(-1, cols),
                           big_v[name_].reshape(-1, cols), name=f"adamw_{name_}")
        grads[name_] = gshard.reshape(w_.shape)
        deltas[name_], new_m[name_], new_v[name_] = (t.reshape(w_.shape) for t in (d_, m_, v_))

    gconv = jnp.stack(d_conv_w)
    n_conv = ffn_conv_w.shape[2]
    gconv_all = _unpack_small(all_reduce_small(_pack_small([gconv]), name="all_reduce_conv_w"), [gconv])[0]
    gconv_mine = lax.dynamic_slice_in_dim(gconv_all, me * n_conv, n_conv, axis=2)
    cshape = ffn_conv_w.shape
    d_, m_, v_ = adamw(ffn_conv_w.reshape(-1, n_conv), gconv_mine.reshape(-1, n_conv),
                       m_ffn_conv_w.reshape(-1, n_conv), v_ffn_conv_w.reshape(-1, n_conv), name="adamw_ffn_conv_w")
    grads["ffn_conv_w"] = gconv_mine
    deltas["ffn_conv_w"], new_m["ffn_conv_w"], new_v["ffn_conv_w"] = (t.reshape(cshape) for t in (d_, m_, v_))

    order = ["norm_mix", "norm_ffn", "norm_final", "attn_w_in", "attn_w_out", "attn_sinks", "hgrn_w_in",
             "hgrn_w_out", "hgrn_norm", "hgrn_lb_logits", "ffn_w_up", "ffn_conv_w", "ffn_conv_b", "ffn_w_down"]
    small_names = ["norm_mix", "norm_ffn", "norm_final", "attn_sinks", "hgrn_norm", "hgrn_lb_logits", "ffn_conv_b"]
    for i, name_ in enumerate(small_names):
        grads[name_], deltas[name_], new_m[name_], new_v[name_] = small_g[i], small_d[i], small_nm[i], small_nv[i]
    return (loss, grad_x, *[grads[n] for n in order], *[deltas[n] for n in order],
            *[new_m[n] for n in order], *[new_v[n] for n in order])
```

```python
import functools
import math

import numpy as np
import jax
import jax.numpy as jnp
from jax import lax
from jax.experimental import pallas as pl
from jax.experimental.pallas import tpu as pltpu

F32 = jnp.float32
BF16 = jnp.bfloat16
MESH = pl.DeviceIdType.MESH

HEAD_DIM = 64
Q_PER_KV = 4
WINDOW = 128
HG_EXPAND = 128
HG_CHUNK = 64
EPS = 1e-6
N_CHIPS = 4
N_DEV = 8

ADAM_LR = 0.001
ADAM_B1 = 0.9
ADAM_B2 = 0.999
ADAM_EPS = 1e-08
ADAM_WD = 0.01
ADAM_STEP = 10

VMEM_LIMIT_BYTES = 48 * 1024 * 1024
LANES = 128
SUBLANES = 8


def _params(sem=None):
    return pltpu.CompilerParams(dimension_semantics=sem, vmem_limit_bytes=VMEM_LIMIT_BYTES)


def _pick(n, cands):
    for c in cands:
        if n % c == 0:
            return c
    return n


def _dot(a, b, dims):
    return lax.dot_general(a.astype(BF16), b.astype(BF16), (dims, ((), ())),
                           preferred_element_type=F32)


NN = ((1,), (0,))
NT = ((1,), (1,))
TN = ((0,), (0,))


def mm_nn(a, w, l, *, name, res=None, out_dtype=F32, split_out=False):
    M, K = a.shape
    N = w.shape[2]
    tm = _pick(M, (1024, 512, 256, 128))
    tn = _pick(N // 2 if split_out else N, (1408, 1024, 768, 512, 256, 128))
    tk = _pick(K, (1024, 1408, 512, 256, 128))
    nk = K // tk
    nh = (N // 2) // tn

    def body(*refs):
        if res is None:
            a_ref, w_ref, o_ref = refs[:3]
            r_ref = None
        else:
            a_ref, w_ref, r_ref, o_ref = refs[:4]
        part = _dot(a_ref[...], w_ref[...], NN)

        def finish(acc):
            if r_ref is not None:
                acc = acc + r_ref[...]
            o_ref[...] = acc.astype(out_dtype)

        if nk == 1:
            finish(part)
        else:
            acc_ref = refs[-1]
            k = pl.program_id(2)

            @pl.when(k == 0)
            def _():
                acc_ref[...] = part

            @pl.when(k > 0)
            def _():
                acc_ref[...] += part

            @pl.when(k == nk - 1)
            def _():
                finish(acc_ref[...])

    in_specs = [pl.BlockSpec((tm, tk), lambda n, m, k: (m, k)),
                pl.BlockSpec((None, tk, tn), lambda n, m, k: (l, k, n))]
    args = [a, w]
    if res is not None:
        in_specs.append(pl.BlockSpec((tm, tn), lambda n, m, k: (m, n)))
        args.append(res)
    if split_out:
        out_shape = jax.ShapeDtypeStruct((2, M, N // 2), out_dtype)
        out_spec = pl.BlockSpec((None, tm, tn), lambda n, m, k: (n // nh, m, n % nh))
    else:
        out_shape = jax.ShapeDtypeStruct((M, N), out_dtype)
        out_spec = pl.BlockSpec((tm, tn), lambda n, m, k: (m, n))
    return pl.pallas_call(
        body, name=name, grid=(N // tn, M // tm, nk), in_specs=in_specs, out_specs=out_spec,
        out_shape=out_shape,
        scratch_shapes=[] if nk == 1 else [pltpu.VMEM((tm, tn), F32)],
        compiler_params=_params(("parallel", "parallel", "arbitrary")),
    )(*args)


def mm_nt(a, w, l, *, name, split_in=False, out_dtype=F32):
    if split_in:
        _, M, Nh = a.shape
        N = 2 * Nh
    else:
        M, N = a.shape
        Nh = N
    K = w.shape[1]
    tm = _pick(M, (1024, 512, 256, 128))
    tko = _pick(K, (1024, 1408, 512, 256, 128))
    tr = _pick(Nh, (1024, 1408, 768, 512, 256, 128))
    nr = N // tr
    nh = Nh // tr

    def body(a_ref, w_ref, o_ref, *scratch):
        part = _dot(a_ref[...], w_ref[...], NT)
        if nr == 1:
            o_ref[...] = part.astype(out_dtype)
        else:
            acc_ref = scratch[0]
            r = pl.program_id(2)

            @pl.when(r == 0)
            def _():
                acc_ref[...] = part

            @pl.when(r > 0)
            def _():
                acc_ref[...] += part

            @pl.when(r == nr - 1)
            def _():
                o_ref[...] = acc_ref[...].astype(out_dtype)

    if split_in:
        a_spec = pl.BlockSpec((None, tm, tr), lambda ko, m, r: (r // nh, m, r % nh))
    else:
        a_spec = pl.BlockSpec((tm, tr), lambda ko, m, r: (m, r))
    return pl.pallas_call(
        body, name=name, grid=(K // tko, M // tm, nr),
        in_specs=[a_spec, pl.BlockSpec((None, tko, tr), lambda ko, m, r: (l, ko, r))],
        out_specs=pl.BlockSpec((tm, tko), lambda ko, m, r: (m, ko)),
        out_shape=jax.ShapeDtypeStruct((M, K), out_dtype),
        scratch_shapes=[] if nr == 1 else [pltpu.VMEM((tm, tko), F32)],
        compiler_params=_params(("parallel", "parallel", "arbitrary")),
    )(a, w)


def mm_tn(a, b, buf, l, n_layers, *, name, split_b=False):
    M, K = a.shape
    if split_b:
        Nh = b.shape[2]
        N = 2 * Nh
    else:
        N = b.shape[1]
        Nh = N
    tm = _pick(M, (512, 256, 128))
    tko = _pick(K, (1024, 1408, 704, 512, 256, 128))
    tn = _pick(Nh, (512, 256, 128))
    nm = M // tm
    nh = Nh // tn

    def body(*refs):
        a_ref, b_ref = refs[:2]
        o_ref = refs[-1]
        part = _dot(a_ref[...], b_ref[...], TN)
        r = pl.program_id(2)

        @pl.when(r == 0)
        def _():
            o_ref[...] = part

        @pl.when(r > 0)
        def _():
            o_ref[...] += part

    if split_b:
        b_spec = pl.BlockSpec((None, tm, tn), lambda ko, n, r: (n // nh, r, n % nh))
    else:
        b_spec = pl.BlockSpec((tm, tn), lambda ko, n, r: (r, n))
    in_specs = [pl.BlockSpec((tm, tko), lambda ko, n, r: (r, ko)), b_spec]
    args = [a, b]
    aliases = {}
    if buf is not None:
        in_specs.append(pl.BlockSpec(memory_space=pl.ANY))
        args.append(buf)
        aliases = {2: 0}
    return pl.pallas_call(
        body, name=name, grid=(K // tko, N // tn, nm), in_specs=in_specs,
        out_specs=pl.BlockSpec((None, tko, tn), lambda ko, n, r: (l, ko, n)),
        out_shape=jax.ShapeDtypeStruct((n_layers, K, N), F32),
        input_output_aliases=aliases,
        compiler_params=_params(("parallel", "parallel", "arbitrary")),
    )(*args)


def mm_dw(at, b, buf, l, n_layers, *, name, split_b=False):
    K, M = at.shape
    if split_b:
        Nh = b.shape[2]
        N = 2 * Nh
    else:
        N = b.shape[1]
        Nh = N
    tm = _pick(M, (512, 256, 128))
    tko = _pick(K, (1024, 1408, 704, 512, 256, 128))
    tn = _pick(Nh, (1408, 1024, 768, 512, 256, 128))
    nm = M // tm
    nh = Nh // tn

    def body(*refs):
        a_ref, b_ref = refs[:2]
        o_ref = refs[-1]
        part = _dot(a_ref[...], b_ref[...], NN)
        r = pl.program_id(2)

        @pl.when(r == 0)
        def _():
            o_ref[...] = part

        @pl.when(r > 0)
        def _():
            o_ref[...] += part

    if split_b:
        b_spec = pl.BlockSpec((None, tm, tn), lambda ko, n, r: (n // nh, r, n % nh))
    else:
        b_spec = pl.BlockSpec((tm, tn), lambda ko, n, r: (r, n))
    in_specs = [pl.BlockSpec((tko, tm), lambda ko, n, r: (ko, r)), b_spec]
    args = [at, b]
    aliases = {}
    if buf is not None:
        in_specs.append(pl.BlockSpec(memory_space=pl.ANY))
        args.append(buf)
        aliases = {2: 0}
    return pl.pallas_call(
        body, name=name, grid=(K // tko, N // tn, nm), in_specs=in_specs,
        out_specs=pl.BlockSpec((None, tko, tn), lambda ko, n, r: (l, ko, n)),
        out_shape=jax.ShapeDtypeStruct((n_layers, K, N), F32),
        input_output_aliases=aliases,
        compiler_params=_params(("parallel", "parallel", "arbitrary")),
    )(*args)


def _rms(x, g):
    return x * lax.rsqrt(jnp.mean(x * x, axis=-1, keepdims=True) + EPS) * g


def rmsnorm_fwd(h, g, l, *, name):
    S, D = h.shape
    tm = _pick(S, (512, 256, 128))

    def body(h_ref, g_ref, o_ref, ot_ref):
        y = _rms(h_ref[...], g_ref[...])
        o_ref[...] = y.astype(BF16)
        ot_ref[...] = y.T.astype(BF16)

    return pl.pallas_call(
        body, name=name, grid=(S // tm,),
        in_specs=[pl.BlockSpec((tm, D), lambda m: (m, 0)),
                  pl.BlockSpec((None, 1, D), lambda m: (l, 0, 0))],
        out_specs=[pl.BlockSpec((tm, D), lambda m: (m, 0)), pl.BlockSpec((D, tm), lambda m: (0, m))],
        out_shape=[jax.ShapeDtypeStruct((S, D), BF16), jax.ShapeDtypeStruct((D, S), BF16)],
        compiler_params=_params(("parallel",)),
    )(h, g)


def rmsnorm_bwd(h, g, l, dhn, dres, *, name):
    S, D = h.shape
    tm = _pick(S, (512, 256, 128))

    def body(h_ref, g_ref, dhn_ref, dres_ref, dh_ref, dg_ref):
        _, vjp = jax.vjp(_rms, h_ref[...], g_ref[...])
        dx, dg = vjp(dhn_ref[...])
        dh_ref[...] = dres_ref[...] + dx

        @pl.when(pl.program_id(0) == 0)
        def _():
            dg_ref[...] = dg

        @pl.when(pl.program_id(0) > 0)
        def _():
            dg_ref[...] += dg

    row = pl.BlockSpec((tm, D), lambda m: (m, 0))
    return pl.pallas_call(
        body, name=name, grid=(S // tm,),
        in_specs=[row, pl.BlockSpec((None, 1, D), lambda m: (l, 0, 0)), row, row],
        out_specs=[row, pl.BlockSpec((1, D), lambda m: (0, 0))],
        out_shape=[jax.ShapeDtypeStruct((S, D), F32), jax.ShapeDtypeStruct((1, D), F32)],
        compiler_params=_params(("arbitrary",)),
    )(h, g, dhn, dres)


def loss_head(h, g, target, *, name):
    S, D = h.shape
    tm = _pick(S, (512, 256, 128))

    def body(h_ref, g_ref, t_ref, loss_ref, dh_ref, dg_ref):
        y, vjp = jax.vjp(_rms, h_ref[...], g_ref[...])
        err = y - t_ref[...]
        part = 0.5 * jnp.sum(jnp.sum(err * err, axis=-1, keepdims=True) / D, axis=0, keepdims=True)
        dx, dg = vjp(err / D)
        dh_ref[...] = dx

        @pl.when(pl.program_id(0) == 0)
        def _():
            dg_ref[...] = dg
            loss_ref[...] = part

        @pl.when(pl.program_id(0) > 0)
        def _():
            dg_ref[...] += dg
            loss_ref[...] += part

    row = pl.BlockSpec((tm, D), lambda m: (m, 0))
    return pl.pallas_call(
        body, name=name, grid=(S // tm,),
        in_specs=[row, pl.BlockSpec((1, D), lambda m: (0, 0)), row],
        out_specs=[pl.BlockSpec((1, 1), lambda m: (0, 0)), row, pl.BlockSpec((1, D), lambda m: (0, 0))],
        out_shape=[jax.ShapeDtypeStruct((1, 1), F32), jax.ShapeDtypeStruct((S, D), F32),
                   jax.ShapeDtypeStruct((1, D), F32)],
        compiler_params=_params(("arbitrary",)),
    )(h, g, target)


def _make_mm(dims_fwd, dims_da, dims_db, swap_da=False, swap_db=False):
    @jax.custom_vjp
    def mm(a, b):
        return _dot(a, b, dims_fwd)

    def fwd(a, b):
        return mm(a, b), (a, b)

    def bwd(resid, g):
        a, b = resid
        da = _dot(b, g, dims_da) if swap_da else _dot(g, b, dims_da)
        db = _dot(g, a, dims_db) if swap_db else _dot(a, g, dims_db)
        return da, db

    mm.defvjp(fwd, bwd)
    return mm


_mm = _make_mm(NN, NT, TN)
_mm_nt = _make_mm(NT, NN, TN, swap_db=True)
_mm_tn = _make_mm(TN, NT, NN, swap_da=True)


def _dot_hi(a, b):
    return jnp.dot(a, b, precision=lax.Precision.HIGHEST, preferred_element_type=F32)


def _sigmoid(x):
    return 0.5 * jnp.tanh(0.5 * x) + 0.5


def _silu(x):
    return x * _sigmoid(x)


def _attn_block(q, kw, vw, sink, slope, first):
    rows = q.shape[0]
    s = _mm_nt(q, kw) * (HEAD_DIM ** -0.5)
    qi = lax.broadcasted_iota(jnp.int32, (rows, 2 * WINDOW), 0) % WINDOW
    ki = lax.broadcasted_iota(jnp.int32, (rows, 2 * WINDOW), 1)
    dist = qi + WINDOW - ki
    valid = (dist >= 0) & (dist < WINDOW) & (ki >= WINDOW * first)
    s = jnp.where(valid, s - slope * dist.astype(F32), -jnp.inf)
    m = lax.stop_gradient(jnp.maximum(jnp.max(s, axis=-1, keepdims=True), sink))
    e = jnp.exp(s - m)
    denom = jnp.sum(e, axis=-1, keepdims=True) + jnp.exp(sink - m)
    return _mm(e * (1.0 / denom), vw)


def _attn_specs():
    G = Q_PER_KV
    qspec = pl.BlockSpec((None, G, WINDOW, HEAD_DIM), lambda h, n: (h, 0, n, 0))
    prev = pl.BlockSpec((None, WINDOW, HEAD_DIM), lambda h, n: (h, jnp.maximum(n - 1, 0), 0))
    cur = pl.BlockSpec((None, WINDOW, HEAD_DIM), lambda h, n: (h, n, 0))
    col = pl.BlockSpec((None, G * WINDOW, 1), lambda h, n: (h, 0, 0))
    return qspec, prev, cur, col


def attn_fwd(q, k, v, sink_col, slope_col, *, name):
    Hkv, G, S, _ = q.shape
    qspec, prev, cur, col = _attn_specs()

    def body(q_ref, kp_ref, kc_ref, vp_ref, vc_ref, sink_ref, slope_ref, o_ref):
        first = (pl.program_id(1) == 0).astype(jnp.int32)
        kw = jnp.concatenate([kp_ref[...], kc_ref[...]], axis=0)
        vw = jnp.concatenate([vp_ref[...], vc_ref[...]], axis=0)
        o = _attn_block(q_ref[...].reshape(G * WINDOW, HEAD_DIM), kw, vw, sink_ref[...],
                        slope_ref[...], first)
        o_ref[...] = o.reshape(G, WINDOW, HEAD_DIM).astype(BF16)

    return pl.pallas_call(
        body, name=name, grid=(Hkv, S // WINDOW),
        in_specs=[qspec, prev, cur, prev, cur, col, col], out_specs=qspec,
        out_shape=jax.ShapeDtypeStruct(q.shape, BF16),
        compiler_params=_params(("parallel", "parallel")),
    )(q, k, k, v, v, sink_col, slope_col)


def attn_bwd(q, k, v, sink_col, slope_col, do, *, name):
    Hkv, G, S, _ = q.shape
    qspec, prev, cur, col = _attn_specs()

    def body(q_ref, kp_ref, kc_ref, vp_ref, vc_ref, sink_ref, slope_ref, do_ref,
             dq_ref, dkc_ref, dkp_ref, dvc_ref, dvp_ref, dsink_ref):
        is_first = pl.program_id(1) == 0
        first = is_first.astype(jnp.int32)
        kw = jnp.concatenate([kp_ref[...], kc_ref[...]], axis=0)
        vw = jnp.concatenate([vp_ref[...], vc_ref[...]], axis=0)
        fn = functools.partial(_attn_block, slope=slope_ref[...], first=first)
        _, vjp = jax.vjp(fn, q_ref[...].reshape(G * WINDOW, HEAD_DIM), kw, vw, sink_ref[...])
        dq, dkw, dvw, dsink = vjp(do_ref[...].reshape(G * WINDOW, HEAD_DIM).astype(F32))
        dq_ref[...] = dq.reshape(G, WINDOW, HEAD_DIM)
        dkp_ref[...] = dkw[:WINDOW]
        dkc_ref[...] = dkw[WINDOW:]
        dvp_ref[...] = dvw[:WINDOW]
        dvc_ref[...] = dvw[WINDOW:]

        @pl.when(is_first)
        def _():
            dsink_ref[...] = dsink

        @pl.when(jnp.logical_not(is_first))
        def _():
            dsink_ref[...] += dsink

    kv_shape = jax.ShapeDtypeStruct(k.shape, F32)
    return pl.pallas_call(
        body, name=name, grid=(Hkv, S // WINDOW),
        in_specs=[qspec, prev, cur, prev, cur, col, col, qspec],
        out_specs=[qspec, cur, cur, cur, cur, col],
        out_shape=[jax.ShapeDtypeStruct(q.shape, F32), kv_shape, kv_shape, kv_shape, kv_shape,
                   jax.ShapeDtypeStruct(sink_col.shape, F32)],
        compiler_params=_params(("parallel", "arbitrary")),
    )(q, k, k, v, v, sink_col, slope_col, do)


def _hgrn_consts():
    C = HG_CHUNK
    t = np.arange(C)[:, None]
    j = np.arange(C)[None, :]
    cq, ck, mk = [], [], []
    H = C // 2
    while H >= 1:
        start = (t // H) * H
        cq.append((j > start) & (j <= t))
        ck.append((j > t) & (j <= start + H))
        mk.append(((t // H) % 2 == 1) & ((j // H) == (t // H) - 1))
        H //= 2
    ltri = (j <= t)
    ops = np.concatenate([ltri] + cq + ck, axis=0).astype(np.float32)
    ops3 = np.concatenate([ops, ops, ops], axis=1)
    opst3 = np.concatenate([ops.T, ops.T, ops.T], axis=1)
    return (jnp.asarray(ops3, BF16), jnp.asarray(opst3, BF16), jnp.asarray(np.stack(mk).astype(np.float32)))


def _split3(x):
    hi = x.astype(BF16)
    r = x - hi.astype(F32)
    mid = r.astype(BF16)
    lo = (r - mid.astype(F32)).astype(BF16)
    return jnp.concatenate([hi, mid, lo], axis=0)


@jax.custom_vjp
def _seg_sums(g, ops3, opst3):
    C = g.shape[0]
    out = jnp.dot(ops3, _split3(g), preferred_element_type=F32)
    return tuple(out[i * C:(i + 1) * C] for i in range(ops3.shape[0] // C))


def _seg_sums_fwd(g, ops3, opst3):
    return _seg_sums(g, ops3, opst3), opst3


def _seg_sums_bwd(opst3, cts):
    ct = jnp.concatenate(cts, axis=0)
    return jnp.dot(opst3, _split3(ct), preferred_element_type=F32), None, None


_seg_sums.defvjp(_seg_sums_fwd, _seg_sums_bwd)


def _hgrn_head(qr, fr, iv, gr, st, logits, ng, ops3, opst3, mk, layer):
    n_levels = mk.shape[0]
    depth = logits.shape[0]
    e = jnp.exp(logits - lax.stop_gradient(jnp.max(logits, axis=0, keepdims=True)))
    sm = e / jnp.sum(e, axis=0, keepdims=True)
    row = lax.broadcasted_iota(jnp.int32, (depth, HG_EXPAND), 0)
    lb = jnp.sum(jnp.where((row >= 1) & (row <= layer), sm, 0.0), axis=0, keepdims=True)

    q = _silu(qr)
    a = jnp.log(lb)
    c = jnp.log(1.0 - lb) - (jnp.maximum(-fr, 0.0) + jnp.log(1.0 + jnp.exp(-jnp.abs(fr))))
    mx = lax.stop_gradient(jnp.maximum(a, c))
    g = mx + jnp.log(jnp.exp(a - mx) + jnp.exp(c - mx))
    k = (1.0 - lb) * _sigmoid(-fr)

    sums = _seg_sums(g, ops3, opst3)
    b = sums[0]
    b_last = jnp.sum(g, axis=0, keepdims=True)

    o = jnp.sum(q * k, axis=-1, keepdims=True) * iv
    amat = None
    for lv in range(n_levels):
        eq = jnp.exp(sums[1 + lv])
        ek = jnp.exp(sums[1 + n_levels + lv])
        part = _mm_nt(q * eq, k * ek) * mk[lv]
        amat = part if amat is None else amat + part
    o = o + _mm(amat, iv) + _mm_nt(q * jnp.exp(b), st)
    st_new = st * jnp.exp(b_last) + _mm_tn(iv, k * jnp.exp(b_last - b))

    on = o * lax.rsqrt(jnp.mean(o * o, axis=-1, keepdims=True) + EPS) * ng
    return on * _silu(gr), st_new


def _hgrn_in_specs(D, depth, order):
    C = HG_CHUNK
    nl = int(math.log2(C))
    n_ops = 1 + 2 * nl
    return [pl.BlockSpec((C, 4 * D), lambda c: (order(c), 0)),
            pl.BlockSpec((depth, D), lambda c: (0, 0)),
            pl.BlockSpec((1, HG_EXPAND), lambda c: (0, 0)),
            pl.BlockSpec((n_ops * C, 3 * C), lambda c: (0, 0)),
            pl.BlockSpec((C, 3 * n_ops * C), lambda c: (0, 0)),
            pl.BlockSpec((nl, C, C), lambda c: (0, 0, 0))]


def _hgrn_cols(D, h):
    return [slice(j * D + h * HG_EXPAND, j * D + (h + 1) * HG_EXPAND) for j in range(4)]


def hgrn_fwd(proj, logits, ng, layer, *, name):
    S = proj.shape[0]
    D = proj.shape[1] // 4
    H = D // HG_EXPAND
    C = HG_CHUNK
    nc = S // C
    consts = _hgrn_consts()

    def body(p_ref, lg_ref, ng_ref, ops3_ref, opst3_ref, mk_ref, o_ref, st_out_ref, st_ref):
        @pl.when(pl.program_id(0) == 0)
        def _():
            st_ref[...] = jnp.zeros_like(st_ref)

        st_out_ref[...] = st_ref[...]
        ops3, opst3, mk = ops3_ref[...], opst3_ref[...], mk_ref[...]
        for h in range(H):
            sl = slice(h * HG_EXPAND, (h + 1) * HG_EXPAND)
            cols = _hgrn_cols(D, h)
            out, st_new = _hgrn_head(p_ref[:, cols[0]], p_ref[:, cols[1]], p_ref[:, cols[2]], p_ref[:, cols[3]],
                                     st_ref[h], lg_ref[:, sl], ng_ref[...], ops3, opst3, mk, layer)
            o_ref[:, sl] = out.astype(BF16)
            st_ref[h] = st_new

    return pl.pallas_call(
        body, name=name, grid=(nc,),
        in_specs=_hgrn_in_specs(D, logits.shape[0], lambda c: c),
        out_specs=[pl.BlockSpec((C, D), lambda c: (c, 0)),
                   pl.BlockSpec((None, H, HG_EXPAND, HG_EXPAND), lambda c: (c, 0, 0, 0))],
        out_shape=[jax.ShapeDtypeStruct((S, D), BF16),
                   jax.ShapeDtypeStruct((nc, H, HG_EXPAND, HG_EXPAND), F32)],
        scratch_shapes=[pltpu.VMEM((H, HG_EXPAND, HG_EXPAND), F32)],
        compiler_params=_params(("arbitrary",)),
    )(proj, logits, ng, *consts)


def hgrn_bwd(proj, logits, ng, states, dout, layer, *, name):
    S = proj.shape[0]
    D = proj.shape[1] // 4
    H = D // HG_EXPAND
    C = HG_CHUNK
    nc = S // C
    depth = logits.shape[0]
    consts = _hgrn_consts()
    rev = lambda c: nc - 1 - c

    def body(p_ref, lg_ref, ng_ref, ops3_ref, opst3_ref, mk_ref, st_in_ref, do_ref,
             dp_ref, dlg_ref, dng_ref, dst_ref):
        @pl.when(pl.program_id(0) == 0)
        def _():
            dst_ref[...] = jnp.zeros_like(dst_ref)
            dlg_ref[...] = jnp.zeros_like(dlg_ref)
            dng_ref[...] = jnp.zeros_like(dng_ref)

        ops3, opst3, mk = ops3_ref[...], opst3_ref[...], mk_ref[...]
        for h in range(H):
            sl = slice(h * HG_EXPAND, (h + 1) * HG_EXPAND)
            cols = _hgrn_cols(D, h)
            fn = functools.partial(_hgrn_head, ops3=ops3, opst3=opst3, mk=mk, layer=layer)
            _, vjp = jax.vjp(fn, p_ref[:, cols[0]], p_ref[:, cols[1]], p_ref[:, cols[2]], p_ref[:, cols[3]],
                             st_in_ref[h], lg_ref[:, sl], ng_ref[...])
            dq, df, di, dg, dst, dlg, dng = vjp((do_ref[:, sl], dst_ref[h]))
            dp_ref[:, cols[0]] = dq
            dp_ref[:, cols[1]] = df
            dp_ref[:, cols[2]] = di
            dp_ref[:, cols[3]] = dg
            dst_ref[h] = dst
            dlg_ref[:, sl] += dlg
            dng_ref[...] += dng

    in_specs = _hgrn_in_specs(D, depth, rev)
    in_specs += [pl.BlockSpec((None, H, HG_EXPAND, HG_EXPAND), lambda c: (rev(c), 0, 0, 0)),
                 pl.BlockSpec((C, D), lambda c: (rev(c), 0))]
    return pl.pallas_call(
        body, name=name, grid=(nc,), in_specs=in_specs,
        out_specs=[pl.BlockSpec((C, 4 * D), lambda c: (rev(c), 0)),
                   pl.BlockSpec((depth, D), lambda c: (0, 0)), pl.BlockSpec((1, HG_EXPAND), lambda c: (0, 0))],
        out_shape=[jax.ShapeDtypeStruct((S, 4 * D), F32), jax.ShapeDtypeStruct((depth, D), F32),
                   jax.ShapeDtypeStruct((1, HG_EXPAND), F32)],
        scratch_shapes=[pltpu.VMEM((H, HG_EXPAND, HG_EXPAND), F32)],
        compiler_params=_params(("arbitrary",)),
    )(proj, logits, ng, *consts, states, dout)


def _edge_rows(edge, shape):
    return jnp.broadcast_to(edge[None], (shape[0] // SUBLANES, SUBLANES, shape[1])).reshape(shape)


def _shift_down(u, prev, shift):
    rolled = pltpu.roll(u, shift, axis=0)
    edge = _edge_rows(pltpu.roll(prev, shift, axis=0), u.shape)
    rows = lax.broadcasted_iota(jnp.int32, u.shape, 0)
    return jnp.where(rows < shift, edge, rolled)


def _shift_up(x, nxt, shift):
    tm = x.shape[0]
    rolled = pltpu.roll(x, tm - shift, axis=0)
    edge = _edge_rows(pltpu.roll(nxt, SUBLANES - shift, axis=0), x.shape)
    rows = lax.broadcasted_iota(jnp.int32, x.shape, 0)
    return jnp.where(rows >= tm - shift, edge, rolled)


def _conv(u, prev, w_ref, b_ref, half):
    u1 = _shift_down(u, prev, 1)
    u2 = _shift_down(u, prev, 2)
    c = b_ref[half] + w_ref[half, 0:1, :] * u2 + w_ref[half, 1:2, :] * u1 + w_ref[half, 2:3, :] * u
    return c, u1, u2


def _conv_specs(tm, tc, l):
    nprev = lambda m: jnp.maximum(m * (tm // SUBLANES) - 1, 0)
    tile = pl.BlockSpec((2, tm, tc), lambda j, m: (0, m, j))
    halo = pl.BlockSpec((2, SUBLANES, tc), lambda j, m: (0, nprev(m), j))
    wspec = pl.BlockSpec((None, 2, 3, tc), lambda j, m: (l, 0, 0, j))
    bspec = pl.BlockSpec((None, 2, 1, tc), lambda j, m: (l, 0, 0, j))
    return tile, halo, wspec, bspec


def convact_fwd(u, cw, cb, l, *, name):
    _, S, Fh = u.shape
    tm = _pick(S, (256, 128))
    tc = _pick(Fh, (1408, 512, 256, 128))
    tile, halo, wspec, bspec = _conv_specs(tm, tc, l)

    def body(u_ref, p_ref, w_ref, b_ref, o_ref, ot_ref):
        keep = (pl.program_id(1) > 0).astype(F32)
        cg, _, _ = _conv(u_ref[0], p_ref[0] * keep, w_ref, b_ref, 0)
        cv, _, _ = _conv(u_ref[1], p_ref[1] * keep, w_ref, b_ref, 1)
        act = _silu(cg) * cv
        o_ref[...] = act.astype(BF16)
        ot_ref[...] = act.T.astype(BF16)

    return pl.pallas_call(
        body, name=name, grid=(Fh // tc, S // tm), in_specs=[tile, halo, wspec, bspec],
        out_specs=[pl.BlockSpec((tm, tc), lambda j, m: (m, j)), pl.BlockSpec((tc, tm), lambda j, m: (j, m))],
        out_shape=[jax.ShapeDtypeStruct((S, Fh), BF16), jax.ShapeDtypeStruct((Fh, S), BF16)],
        compiler_params=_params(("parallel", "parallel")),
    )(u, u, cw, cb)


def convact_bwd(u, cw, cb, l, da, *, name):
    _, S, Fh = u.shape
    tm = _pick(S, (256, 128))
    tc = _pick(Fh, (1408, 512, 256, 128))
    tile, halo, wspec, bspec = _conv_specs(tm, tc, l)

    def body(u_ref, p_ref, w_ref, b_ref, da_ref, dc_ref, dw_ref, db_ref):
        first = pl.program_id(1) == 0
        keep = jnp.logical_not(first).astype(F32)
        ug, uv = u_ref[0], u_ref[1]
        cg, g1, g2 = _conv(ug, p_ref[0] * keep, w_ref, b_ref, 0)
        cv, v1, v2 = _conv(uv, p_ref[1] * keep, w_ref, b_ref, 1)
        da_t = da_ref[...]
        sg = _sigmoid(cg)
        dcv = da_t * (cg * sg)
        dcg = da_t * cv * (sg * (1.0 + cg * (1.0 - sg)))
        dc_ref[0] = dcg
        dc_ref[1] = dcv

        @pl.when(first)
        def _():
            dw_ref[...] = jnp.zeros_like(dw_ref)
            db_ref[...] = jnp.zeros_like(db_ref)

        rs = lambda t: jnp.sum(t, axis=0, keepdims=True)
        for half, dc, taps in ((0, dcg, (g2, g1, ug)), (1, dcv, (v2, v1, uv))):
            for j in range(3):
                dw_ref[half, j:j + 1, :] += rs(dc * taps[j])
            db_ref[half] += rs(dc)

    return pl.pallas_call(
        body, name=name, grid=(Fh // tc, S // tm),
        in_specs=[tile, halo, wspec, bspec, pl.BlockSpec((tm, tc), lambda j, m: (m, j))],
        out_specs=[tile, pl.BlockSpec((2, 3, tc), lambda j, m: (0, 0, j)),
                   pl.BlockSpec((2, 1, tc), lambda j, m: (0, 0, j))],
        out_shape=[jax.ShapeDtypeStruct(u.shape, F32), jax.ShapeDtypeStruct((2, 3, Fh), F32),
                   jax.ShapeDtypeStruct((2, 1, Fh), F32)],
        compiler_params=_params(("parallel", "arbitrary")),
    )(u, u, cw, cb, da)


def conv_input_grad(dc, cw, l, *, name):
    _, S, Fh = dc.shape
    tm = _pick(S, (256, 128))
    tc = _pick(Fh, (1408, 512, 256, 128))
    nblk = S // SUBLANES
    nm = S // tm

    def body(dc_ref, n_ref, w_ref, du_ref):
        keep = (pl.program_id(1) < nm - 1).astype(F32)
        for half in range(2):
            t = dc_ref[half]
            nxt = n_ref[half] * keep
            du = (w_ref[half, 2:3, :] * t + w_ref[half, 1:2, :] * _shift_up(t, nxt, 1)
                  + w_ref[half, 0:1, :] * _shift_up(t, nxt, 2))
            du_ref[half] = du.astype(BF16)

    tile = pl.BlockSpec((2, tm, tc), lambda j, m: (0, m, j))
    halo = pl.BlockSpec((2, SUBLANES, tc),
                        lambda j, m: (0, jnp.minimum((m + 1) * (tm // SUBLANES), nblk - 1), j))
    return pl.pallas_call(
        body, name=name, grid=(Fh // tc, nm),
        in_specs=[tile, halo, pl.BlockSpec((None, 2, 3, tc), lambda j, m: (l, 0, 0, j))],
        out_specs=tile, out_shape=jax.ShapeDtypeStruct(dc.shape, BF16),
        compiler_params=_params(("parallel", "parallel")),
    )(dc, dc, cw)


def _adamw_math(w, g, m, v):
    m = ADAM_B1 * m + (1.0 - ADAM_B1) * g
    v = ADAM_B2 * v + (1.0 - ADAM_B2) * (g * g)
    m_hat = m / (1.0 - ADAM_B1 ** ADAM_STEP)
    v_hat = v / (1.0 - ADAM_B2 ** ADAM_STEP)
    delta = -ADAM_LR * (m_hat / (jnp.sqrt(v_hat) + ADAM_EPS) + ADAM_WD * w)
    return delta, m, v


def _row_tile(R, Ccols, n_streams):
    for tr in (512, 256, 128, 64, 32, 16, 8):
        if R % tr == 0 and tr * Ccols * 4 * n_streams * 2 <= VMEM_LIMIT_BYTES // 2:
            return tr
    return R


def adamw(w, g, m, v, *, name):
    R, Ccols = w.shape
    tr = _row_tile(R, Ccols, 7)

    def body(w_ref, g_ref, m_ref, v_ref, d_ref, mo_ref, vo_ref):
        d, mn, vn = _adamw_math(w_ref[...], g_ref[...], m_ref[...], v_ref[...])
        d_ref[...] = d
        mo_ref[...] = mn
        vo_ref[...] = vn

    spec = pl.BlockSpec((tr, Ccols), lambda i: (i, 0))
    shp = jax.ShapeDtypeStruct((R, Ccols), F32)
    return pl.pallas_call(
        body, name=name, grid=(R // tr,), in_specs=[spec] * 4, out_specs=[spec] * 3,
        out_shape=[shp, shp, shp], compiler_params=_params(("parallel",)),
    )(w, g, m, v)


def _prefetch_call(body, pos, args, *, name, grid, in_specs, out_specs, out_shape, sem):
    return pl.pallas_call(
        lambda pos_ref, *refs: body(*refs), name=name,
        grid_spec=pltpu.PrefetchScalarGridSpec(num_scalar_prefetch=1, grid=grid, in_specs=in_specs,
                                               out_specs=out_specs),
        out_shape=out_shape, compiler_params=_params(sem),
    )(pos, *args)


def cast_place(w, pos, *, row_sharded, name):
    L, R, Ccols = w.shape

    def body(w_ref, o_ref):
        o_ref[...] = w_ref[...].astype(BF16)

    if row_sharded:
        grid = (L,)
        in_spec = pl.BlockSpec((None, R, Ccols), lambda l, pos: (l, 0, 0))
        out_spec = pl.BlockSpec((None, R, Ccols), lambda l, pos: (l, pos[0], 0))
        out_shape = jax.ShapeDtypeStruct((L, N_CHIPS * R, Ccols), BF16)
        sem = ("parallel",)
    else:
        tr = _pick(R, (512, 256, 128))
        grid = (L, R // tr)
        in_spec = pl.BlockSpec((None, tr, Ccols), lambda l, i, pos: (l, i, 0))
        out_spec = pl.BlockSpec((None, tr, Ccols), lambda l, i, pos: (l, i, pos[0]))
        out_shape = jax.ShapeDtypeStruct((L, R, N_CHIPS * Ccols), BF16)
        sem = ("parallel", "parallel")
    return _prefetch_call(body, pos, [w], name=name, grid=grid, in_specs=[in_spec], out_specs=out_spec,
                          out_shape=out_shape, sem=sem)


def pair_add(g, theirs, pos, *, row_sharded, name):
    def body(g_ref, t_ref, o_ref, ob_ref):
        s = g_ref[...] + t_ref[...]
        o_ref[...] = s
        ob_ref[...] = s.astype(BF16)

    if row_sharded:
        L, _, _, r2, D = g.shape
        grid = (L, N_CHIPS)
        g_spec = pl.BlockSpec((None, None, None, r2, D), lambda l, q, pos: (l, q, pos[1], 0, 0))
        t_spec = pl.BlockSpec((None, None, r2, D), lambda l, q, pos: (l, q, 0, 0))
    else:
        L, K, N = g.shape
        n = N // N_CHIPS
        tr = _pick(K // 2, (256, 128))
        nb = (K // 2) // tr
        grid = (L, nb, N_CHIPS)
        g_spec = pl.BlockSpec((None, tr, n), lambda l, i, j, pos: (l, pos[1] * nb + i, j))
        t_spec = pl.BlockSpec((None, tr, n), lambda l, i, j, pos: (l, i, j))
    return _prefetch_call(body, pos, [g, theirs], name=name, grid=grid, in_specs=[g_spec, t_spec],
                          out_specs=[t_spec, t_spec],
                          out_shape=[jax.ShapeDtypeStruct(theirs.shape, F32), jax.ShapeDtypeStruct(theirs.shape, BF16)],
                          sem=("parallel",) * len(grid))


def chip_add(pair, landed, pos, *, row_sharded, name):
    def body(p_ref, a_ref, b_ref, c_ref, o_ref):
        o_ref[...] = ((p_ref[...] + a_ref[...].astype(F32)) + b_ref[...].astype(F32)) + c_ref[...].astype(F32)

    if row_sharded:
        L, _, r2, D = pair.shape
        grid = (L,)
        p_spec = pl.BlockSpec((None, None, r2, D), lambda l, pos: (l, pos[0], 0, 0))
        l_specs = [pl.BlockSpec((None, None, r2, D), functools.partial(lambda l, pos, k: (k, l, 0, 0), k=k))
                   for k in range(3)]
        out_spec = pl.BlockSpec((None, None, r2, D), lambda l, pos: (l, pos[1], 0, 0))
        out_shape = jax.ShapeDtypeStruct((L, 2, r2, D), F32)
    else:
        L, K2, N = pair.shape
        n = N // N_CHIPS
        tr = _pick(K2, (256, 128))
        nb = K2 // tr
        grid = (L, nb)
        p_spec = pl.BlockSpec((None, tr, n), lambda l, i, pos: (l, i, pos[0]))
        l_specs = [pl.BlockSpec((None, None, tr, n), functools.partial(lambda l, i, pos, k: (k, l, i, 0), k=k))
                   for k in range(3)]
        out_spec = pl.BlockSpec((None, tr, n), lambda l, i, pos: (l, pos[1] * nb + i, 0))
        out_shape = jax.ShapeDtypeStruct((L, 2 * K2, n), F32)
    return _prefetch_call(body, pos, [pair, landed, landed, landed], name=name, grid=grid,
                          in_specs=[p_spec] + l_specs, out_specs=out_spec, out_shape=out_shape,
                          sem=("parallel",) * len(grid))


def _position():
    return lax.axis_index("x"), lax.axis_index("y"), lax.axis_index("c")


def _other_chips(x, y):
    return [(1 - x, y), (x, 1 - y), (1 - x, 1 - y)]


HBM_ANY = pl.BlockSpec(memory_space=pl.ANY)
COMM_PARAMS = pltpu.CompilerParams(has_side_effects=True)


def _col_half(ref, h, chip, n):
    K = ref.shape[1]
    return ref.at[:, pl.ds(h * (K // 2), K // 2), pl.ds(chip * n, n)]


def _row_half(ref, h, chip, r):
    return ref.at[:, pl.ds(chip * r + h * (r // 2), r // 2), :]


def all_gather_weights(col_bufs, row_bufs, whole_shards, *, name):
    n_col, n_row, n_whole = len(col_bufs), len(row_bufs), len(whole_shards)
    n_half = n_col + n_row
    n_in = n_half + n_whole

    def body(*refs):
        ins = refs[:n_in]
        outs = refs[n_in:2 * n_in]
        ici_send, ici_recv, d2d_send, d2d_recv, w_send, w_recv, local_sem = refs[2 * n_in:]
        x, y, c = _position()
        me = 2 * x + y
        chips = _other_chips(x, y)
        chip_ids = [2 * cx + cy for cx, cy in chips]
        sibling = (x, y, 1 - c)

        def half(ref, i, h, chip):
            if i < n_col:
                return _col_half(ref, h, chip, ref.shape[2] // N_CHIPS)
            return _row_half(ref, h, chip, ref.shape[1] // N_CHIPS)

        def half_dst(i, h, chip):
            return half(outs[i], i, h, chip)

        def whole_dst(i, chip):
            n = ins[i].shape[2]
            return outs[i].at[:, :, pl.ds(chip * n, n)]

        local = [pltpu.make_async_copy(ins[i], whole_dst(i, me), local_sem.at[i - n_half])
                 for i in range(n_half, n_in)]
        for cp in local:
            cp.start()

        def ici(i, k, src_chip):
            return pltpu.make_async_remote_copy(
                src_ref=half(ins[i], i, c, me), dst_ref=half_dst(i, c, src_chip),
                send_sem=ici_send.at[i, k], recv_sem=ici_recv.at[i, k],
                device_id=(*chips[k], c), device_id_type=MESH)

        def whole(i, k, src_chip):
            return pltpu.make_async_remote_copy(
                src_ref=ins[i], dst_ref=whole_dst(i, src_chip),
                send_sem=w_send.at[i - n_half, k], recv_sem=w_recv.at[i - n_half, k],
                device_id=(*chips[k], c), device_id_type=MESH)

        def d2d(i, k, h):
            return pltpu.make_async_remote_copy(
                src_ref=half_dst(i, h, chip_ids[k]), dst_ref=half_dst(i, h, chip_ids[k]),
                send_sem=d2d_send.at[i, k], recv_sem=d2d_recv.at[i, k],
                device_id=sibling, device_id_type=MESH)

        sent = [ici(i, k, me) for i in range(n_half) for k in range(3)]
        sent += [whole(i, k, me) for i in range(n_half, n_in) for k in range(3)]
        for cp in sent:
            cp.start()
        passed = []
        for i in range(n_half):
            for k in range(3):
                ici(i, k, chip_ids[k]).wait_recv()
                fwd = d2d(i, k, c)
                fwd.start()
                passed.append(fwd)
        for i in range(n_half):
            for k in range(3):
                d2d(i, k, 1 - c).wait_recv()
        for i in range(n_half, n_in):
            for k in range(3):
                whole(i, k, chip_ids[k]).wait_recv()
        for cp in sent + passed:
            cp.wait_send()
        for cp in local:
            cp.wait()

    ins = list(col_bufs) + list(row_bufs) + list(whole_shards)
    out_shape = [jax.ShapeDtypeStruct(a.shape, a.dtype) for a in ins[:n_half]]
    for a in whole_shards:
        out_shape.append(jax.ShapeDtypeStruct((a.shape[0], a.shape[1], N_CHIPS * a.shape[2]), a.dtype))
    return pl.pallas_call(
        body, name=name, in_specs=[HBM_ANY] * n_in, out_specs=[HBM_ANY] * n_in, out_shape=out_shape,
        input_output_aliases={i: i for i in range(n_half)},
        scratch_shapes=[pltpu.SemaphoreType.DMA((n_half, 3)), pltpu.SemaphoreType.DMA((n_half, 3)),
                        pltpu.SemaphoreType.DMA((n_half, 3)), pltpu.SemaphoreType.DMA((n_half, 3)),
                        pltpu.SemaphoreType.DMA((max(n_whole, 1), 3)), pltpu.SemaphoreType.DMA((max(n_whole, 1), 3)),
                        pltpu.SemaphoreType.DMA((max(n_whole, 1),))],
        compiler_params=COMM_PARAMS,
    )(*ins)


def sibling_swap_halves(col_grads, row_grads, *, name):
    n_col, n_row = len(col_grads), len(row_grads)
    n_in = n_col + n_row

    def body(*refs):
        ins = refs[:n_in]
        outs = refs[n_in:2 * n_in]
        send_sem, recv_sem = refs[2 * n_in:]
        x, y, c = _position()
        sibling = (x, y, 1 - c)
        h = 1 - c
        cps = []
        for i in range(n_in):
            if i < n_col:
                K = ins[i].shape[1]
                cps.append(pltpu.make_async_remote_copy(
                    src_ref=ins[i].at[:, pl.ds(h * (K // 2), K // 2), :], dst_ref=outs[i],
                    send_sem=send_sem.at[i, 0], recv_sem=recv_sem.at[i, 0],
                    device_id=sibling, device_id_type=MESH))
            else:
                r = ins[i].shape[1] // N_CHIPS
                for q in range(N_CHIPS):
                    cps.append(pltpu.make_async_remote_copy(
                        src_ref=ins[i].at[:, pl.ds(q * r + h * (r // 2), r // 2), :], dst_ref=outs[i].at[:, q],
                        send_sem=send_sem.at[i, q], recv_sem=recv_sem.at[i, q],
                        device_id=sibling, device_id_type=MESH))
        for cp in cps:
            cp.start()
        for cp in cps:
            cp.wait()

    out_shape = []
    for a in col_grads:
        out_shape.append(jax.ShapeDtypeStruct((a.shape[0], a.shape[1] // 2, a.shape[2]), a.dtype))
    for a in row_grads:
        r = a.shape[1] // N_CHIPS
        out_shape.append(jax.ShapeDtypeStruct((a.shape[0], N_CHIPS, r // 2, a.shape[2]), a.dtype))
    return pl.pallas_call(
        body, name=name, in_specs=[HBM_ANY] * n_in, out_specs=[HBM_ANY] * n_in, out_shape=out_shape,
        scratch_shapes=[pltpu.SemaphoreType.DMA((n_in, N_CHIPS)), pltpu.SemaphoreType.DMA((n_in, N_CHIPS))],
        compiler_params=COMM_PARAMS,
    )(*col_grads, *row_grads)


def scatter_to_chips(col_halves, row_halves, *, name):
    n_col, n_row = len(col_halves), len(row_halves)
    n_in = n_col + n_row

    def body(*refs):
        ins = refs[:n_in]
        outs = refs[n_in:2 * n_in]
        send_sem, recv_sem = refs[2 * n_in:]
        x, y, c = _position()
        chips = _other_chips(x, y)
        cps = []
        for i in range(n_in):
            for k in range(3):
                to = 2 * chips[k][0] + chips[k][1]
                if i < n_col:
                    n = ins[i].shape[2] // N_CHIPS
                    src = ins[i].at[:, :, pl.ds(to * n, n)]
                else:
                    src = ins[i].at[:, to]
                cps.append(pltpu.make_async_remote_copy(
                    src_ref=src, dst_ref=outs[i].at[k], send_sem=send_sem.at[i, k], recv_sem=recv_sem.at[i, k],
                    device_id=(*chips[k], c), device_id_type=MESH))
        for cp in cps:
            cp.start()
        for cp in cps:
            cp.wait()

    out_shape = []
    for a in col_halves:
        out_shape.append(jax.ShapeDtypeStruct((3, a.shape[0], a.shape[1], a.shape[2] // N_CHIPS), a.dtype))
    for a in row_halves:
        out_shape.append(jax.ShapeDtypeStruct((3, a.shape[0], a.shape[2], a.shape[3]), a.dtype))
    return pl.pallas_call(
        body, name=name, in_specs=[HBM_ANY] * n_in, out_specs=[HBM_ANY] * n_in, out_shape=out_shape,
        scratch_shapes=[pltpu.SemaphoreType.DMA((n_in, 3)), pltpu.SemaphoreType.DMA((n_in, 3))],
        compiler_params=COMM_PARAMS,
    )(*col_halves, *row_halves)


def sibling_join_halves(shards, *, name):
    n_in = len(shards)

    def body(*refs):
        ins = refs[:n_in]
        outs = refs[n_in:2 * n_in]
        send_sem, recv_sem = refs[2 * n_in:]
        x, y, c = _position()
        sibling = (x, y, 1 - c)

        def rows(ref, h):
            R2 = ref.shape[1] // 2
            return ref.at[:, pl.ds(h * R2, R2), :]

        sends = [pltpu.make_async_remote_copy(src_ref=rows(ins[i], c), dst_ref=rows(outs[i], c),
                                              send_sem=send_sem.at[i], recv_sem=recv_sem.at[i],
                                              device_id=sibling, device_id_type=MESH) for i in range(n_in)]
        lands = [pltpu.make_async_remote_copy(src_ref=rows(ins[i], c), dst_ref=rows(outs[i], 1 - c),
                                              send_sem=send_sem.at[i], recv_sem=recv_sem.at[i],
                                              device_id=sibling, device_id_type=MESH) for i in range(n_in)]
        for cp in sends:
            cp.start()
        for i in range(n_in):
            sends[i].wait_send()
            lands[i].wait_recv()

    return pl.pallas_call(
        body, name=name, in_specs=[HBM_ANY] * n_in, out_specs=[HBM_ANY] * n_in,
        out_shape=[jax.ShapeDtypeStruct(a.shape, a.dtype) for a in shards],
        input_output_aliases={i: i for i in range(n_in)},
        scratch_shapes=[pltpu.SemaphoreType.DMA((n_in,)), pltpu.SemaphoreType.DMA((n_in,))],
        compiler_params=COMM_PARAMS,
    )(*shards)


def all_reduce_small(v, *, name):
    R, Ccols = v.shape

    def body(v_ref, o_ref, slots, send_sem, recv_sem):
        x, y, c = _position()
        me = 4 * x + 2 * y + c
        slots[me] = v_ref[...]

        def peer(d):
            return x ^ (d >> 2), y ^ ((d >> 1) & 1), c ^ (d & 1)

        def copy(d, slot):
            return pltpu.make_async_remote_copy(
                src_ref=v_ref, dst_ref=slots.at[slot], send_sem=send_sem.at[d], recv_sem=recv_sem.at[d],
                device_id=peer(d), device_id_type=MESH)

        sends = [copy(d, me) for d in range(1, N_DEV)]
        for cp in sends:
            cp.start()
        for d in range(1, N_DEV):
            px, py, pc = peer(d)
            copy(d, 4 * px + 2 * py + pc).wait_recv()
        for cp in sends:
            cp.wait_send()
        acc = slots[0]
        for d in range(1, N_DEV):
            acc = acc + slots[d]
        o_ref[...] = acc

    return pl.pallas_call(
        body, name=name,
        in_specs=[pl.BlockSpec(memory_space=pltpu.VMEM)], out_specs=pl.BlockSpec(memory_space=pltpu.VMEM),
        out_shape=jax.ShapeDtypeStruct((R, Ccols), F32),
        scratch_shapes=[pltpu.VMEM((N_DEV, R, Ccols), F32), pltpu.SemaphoreType.DMA((N_DEV,)),
                        pltpu.SemaphoreType.DMA((N_DEV,))],
        compiler_params=COMM_PARAMS,
    )(v)


def _pack_small(parts):
    flat = jnp.concatenate([p.reshape(-1).astype(F32) for p in parts])
    n = flat.shape[0]
    rows = -(-n // LANES)
    rows = -(-rows // SUBLANES) * SUBLANES
    return jnp.pad(flat, (0, rows * LANES - n)).reshape(rows, LANES)


def _unpack_small(packed, like):
    flat = packed.reshape(-1)
    out, off = [], 0
    for p in like:
        out.append(flat[off:off + p.size].reshape(p.shape))
        off += p.size
    return out


def kernel(x, norm_mix, norm_ffn, norm_final, attn_w_in, attn_w_out, attn_sinks, hgrn_w_in, hgrn_w_out, hgrn_norm, hgrn_lb_logits, ffn_w_up, ffn_conv_w, ffn_conv_b, ffn_w_down, loss_target, m_norm_mix, m_norm_ffn, m_norm_final, m_attn_w_in, m_attn_w_out, m_attn_sinks, m_hgrn_w_in, m_hgrn_w_out, m_hgrn_norm, m_hgrn_lb_logits, m_ffn_w_up, m_ffn_conv_w, m_ffn_conv_b, m_ffn_w_down, v_norm_mix, v_norm_ffn, v_norm_final, v_attn_w_in, v_attn_w_out, v_attn_sinks, v_hgrn_w_in, v_hgrn_w_out, v_hgrn_norm, v_hgrn_lb_logits, v_ffn_w_up, v_ffn_conv_w, v_ffn_conv_b, v_ffn_w_down):
    S, D = x.shape[1], x.shape[2]
    depth = norm_mix.shape[0]
    n_q = D // HEAD_DIM
    n_kv = n_q // Q_PER_KV
    Fh = ffn_w_down.shape[1] * N_CHIPS
    h = x.reshape(S, D)
    target = loss_target.reshape(S, D)

    col_names = ["attn_w_in", "hgrn_w_in", "ffn_w_up"]
    row_names = ["attn_w_out", "hgrn_w_out", "ffn_w_down"]
    x_, y_, c_ = _position()
    me = 2 * x_ + y_
    pos = jnp.stack([me, c_]).astype(jnp.int32)
    col_w = [cast_place(w, pos, row_sharded=False, name=f"cast_place_{n}")
             for n, w in zip(col_names, (attn_w_in, hgrn_w_in, ffn_w_up))]
    row_w = [cast_place(w, pos, row_sharded=True, name=f"cast_place_{n}")
             for n, w in zip(row_names, (attn_w_out, hgrn_w_out, ffn_w_down))]
    gathered = all_gather_weights(col_w, row_w, [ffn_conv_w], name="all_gather_weights")
    W = dict(zip(col_names + row_names + ["ffn_conv_w"], gathered))
    cw = W["ffn_conv_w"].reshape(depth, 3, 2, Fh).transpose(0, 2, 1, 3)
    cb = ffn_conv_b.reshape(depth, 2, 1, Fh)
    g_mix = norm_mix.reshape(depth, 1, D)
    g_ffn = norm_ffn.reshape(depth, 1, D)

    slopes = jnp.exp2(-8.0 * jnp.arange(1, n_q + 1, dtype=F32) / n_q).reshape(n_kv, Q_PER_KV)
    slope_col = jnp.repeat(slopes, WINDOW, axis=1).reshape(n_kv, Q_PER_KV * WINDOW, 1)

    def to_heads(t, nh):
        return t.reshape(S, nh, HEAD_DIM).transpose(1, 0, 2)

    def from_heads(t):
        return t.transpose(1, 0, 2).reshape(S, -1)

    saved = []
    for layer in range(depth):
        idx = layer // 2
        rec = {"h_mix": h}
        hn, rec["hn_t"] = rmsnorm_fwd(h, g_mix, layer, name=f"norm_mix_fwd{layer}")
        if layer % 2 == 0:
            proj = mm_nn(hn, W["attn_w_in"], idx, name=f"attn_in_fwd{layer}")
            q = to_heads(proj[:, :n_q * HEAD_DIM], n_q).reshape(n_kv, Q_PER_KV, S, HEAD_DIM)
            k = to_heads(proj[:, n_q * HEAD_DIM:(n_q + n_kv) * HEAD_DIM], n_kv)
            v = to_heads(proj[:, (n_q + n_kv) * HEAD_DIM:], n_kv)
            sink_col = jnp.repeat(attn_sinks[idx].reshape(n_kv, Q_PER_KV), WINDOW, axis=1)
            sink_col = sink_col.reshape(n_kv, Q_PER_KV * WINDOW, 1)
            o_heads = attn_fwd(q, k, v, sink_col, slope_col, name=f"attn_fwd{layer}")
            o = from_heads(o_heads.reshape(n_q, S, HEAD_DIM))
            rec.update(q=q, k=k, v=v, sink_col=sink_col, o=o)
            h = mm_nn(o, W["attn_w_out"], idx, name=f"attn_out_fwd{layer}", res=h)
        else:
            proj = mm_nn(hn, W["hgrn_w_in"], idx, name=f"hgrn_in_fwd{layer}")
            ng = hgrn_norm[idx].reshape(1, HG_EXPAND)
            o, states = hgrn_fwd(proj, hgrn_lb_logits, ng, layer, name=f"hgrn_fwd{layer}")
            rec.update(proj=proj, ng=ng, states=states, o=o)
            h = mm_nn(o, W["hgrn_w_out"], idx, name=f"hgrn_out_fwd{layer}", res=h)
        rec["h_ffn"] = h
        hn2, rec["hn2_t"] = rmsnorm_fwd(h, g_ffn, layer, name=f"norm_ffn_fwd{layer}")
        u = mm_nn(hn2, W["ffn_w_up"], layer, name=f"ffn_up_fwd{layer}", split_out=True)
        act, rec["act_t"] = convact_fwd(u, cw, cb, layer, name=f"ffn_act_fwd{layer}")
        rec["u"] = u
        h = mm_nn(act, W["ffn_w_down"], layer, name=f"ffn_down_fwd{layer}", res=h)
        saved.append(rec)

    loss_part, dh, d_norm_final = loss_head(h, norm_final.reshape(1, D), target, name="loss_head")

    n_attn, n_hgrn = attn_w_in.shape[0], hgrn_w_in.shape[0]
    G = dict(attn_w_in=None, attn_w_out=None, hgrn_w_in=None, hgrn_w_out=None, ffn_w_up=None, ffn_w_down=None)
    d_norm_mix, d_norm_ffn = [None] * depth, [None] * depth
    d_conv_w, d_conv_b = [None] * depth, [None] * depth
    d_sinks, d_hgrn_norm = [None] * n_attn, [None] * n_hgrn
    d_logits = jnp.zeros_like(hgrn_lb_logits)
    for layer in reversed(range(depth)):
        idx = layer // 2
        rec = saved[layer]
        G["ffn_w_down"] = mm_dw(rec["act_t"], dh, G["ffn_w_down"], layer, depth, name=f"ffn_down_dw{layer}")
        dact = mm_nt(dh, W["ffn_w_down"], layer, name=f"ffn_down_dx{layer}")
        dc, dcw, dcb = convact_bwd(rec["u"], cw, cb, layer, dact, name=f"ffn_act_bwd{layer}")
        d_conv_w[layer] = dcw.transpose(1, 0, 2).reshape(3, 2 * Fh)
        d_conv_b[layer] = dcb.reshape(2 * Fh)
        du = conv_input_grad(dc, cw, layer, name=f"ffn_conv_dx{layer}")
        G["ffn_w_up"] = mm_dw(rec["hn2_t"], du, G["ffn_w_up"], layer, depth, name=f"ffn_up_dw{layer}", split_b=True)
        dhn2 = mm_nt(du, W["ffn_w_up"], layer, name=f"ffn_up_dx{layer}", split_in=True)
        dh, dg = rmsnorm_bwd(rec["h_ffn"], g_ffn, layer, dhn2, dh, name=f"norm_ffn_bwd{layer}")
        d_norm_ffn[layer] = dg.reshape(D)
        if layer % 2 == 0:
            G["attn_w_out"] = mm_tn(rec["o"], dh, G["attn_w_out"], idx, n_attn, name=f"attn_out_dw{layer}")
            do = mm_nt(dh, W["attn_w_out"], idx, name=f"attn_out_dx{layer}", out_dtype=BF16)
            do_heads = to_heads(do, n_q).reshape(n_kv, Q_PER_KV, S, HEAD_DIM)
            dq, dkc, dkp, dvc, dvp, dsink = attn_bwd(rec["q"], rec["k"], rec["v"], rec["sink_col"], slope_col,
                                                     do_heads, name=f"attn_bwd{layer}")
            shift = lambda t: jnp.concatenate([t[:, WINDOW:], jnp.zeros_like(t[:, :WINDOW])], axis=1)
            dk = dkc + shift(dkp)
            dv = dvc + shift(dvp)
            dproj = jnp.concatenate([from_heads(dq.reshape(n_q, S, HEAD_DIM)), from_heads(dk), from_heads(dv)], axis=1)
            d_sinks[idx] = jnp.sum(dsink.reshape(n_kv, Q_PER_KV, WINDOW), axis=-1).reshape(n_q)
            G["attn_w_in"] = mm_dw(rec["hn_t"], dproj, G["attn_w_in"], idx, n_attn, name=f"attn_in_dw{layer}")
            dhn = mm_nt(dproj, W["attn_w_in"], idx, name=f"attn_in_dx{layer}")
        else:
            G["hgrn_w_out"] = mm_tn(rec["o"], dh, G["hgrn_w_out"], idx, n_hgrn, name=f"hgrn_out_dw{layer}")
            do = mm_nt(dh, W["hgrn_w_out"], idx, name=f"hgrn_out_dx{layer}")
            dproj, dlg, dng = hgrn_bwd(rec["proj"], hgrn_lb_logits, rec["ng"], rec["states"], do, layer,
                                       name=f"hgrn_bwd{layer}")
            d_logits = d_logits + dlg
            d_hgrn_norm[idx] = dng.reshape(HG_EXPAND)
            G["hgrn_w_in"] = mm_dw(rec["hn_t"], dproj, G["hgrn_w_in"], idx, n_hgrn, name=f"hgrn_in_dw{layer}")
            dhn = mm_nt(dproj, W["hgrn_w_in"], idx, name=f"hgrn_in_dx{layer}")
        dh, dg = rmsnorm_bwd(rec["h_mix"], g_mix, layer, dhn, dh, name=f"norm_mix_bwd{layer}")
        d_norm_mix[layer] = dg.reshape(D)
    grad_x = dh.reshape(x.shape)

    small_w = [norm_mix, norm_ffn, norm_final, attn_sinks, hgrn_norm, hgrn_lb_logits, ffn_conv_b]
    small_m = [m_norm_mix, m_norm_ffn, m_norm_final, m_attn_sinks, m_hgrn_norm, m_hgrn_lb_logits, m_ffn_conv_b]
    small_v = [v_norm_mix, v_norm_ffn, v_norm_final, v_attn_sinks, v_hgrn_norm, v_hgrn_lb_logits, v_ffn_conv_b]
    small_grads = [jnp.stack(d_norm_mix), jnp.stack(d_norm_ffn), d_norm_final.reshape(D), jnp.stack(d_sinks),
                   jnp.stack(d_hgrn_norm), d_logits, jnp.stack(d_conv_b)]
    reduced = all_reduce_small(_pack_small(small_grads + [loss_part]), name="all_reduce_small")
    small_g = _unpack_small(reduced, small_w + [loss_part])
    loss = small_g.pop().reshape(())
    pad = [jnp.zeros((1, 1), F32)]
    sd, sm_, sv_ = adamw(_pack_small(small_w + pad), reduced, _pack_small(small_m + pad),
                         _pack_small(small_v + pad), name="adamw_small")
    small_d = _unpack_small(sd, small_w)
    small_nm = _unpack_small(sm_, small_w)
    small_nv = _unpack_small(sv_, small_w)

    col_g = [G["attn_w_in"], G["hgrn_w_in"], G["ffn_w_up"]]
    row_g = [G["attn_w_out"], G["hgrn_w_out"], G["ffn_w_down"]]
    theirs = sibling_swap_halves(col_g, row_g, name="rs_sibling_swap")
    pair, pair_bf16 = [], []
    for i, g in enumerate(col_g):
        p, pb = pair_add(g, theirs[i], pos, row_sharded=False, name=f"rs_pair_add_col{i}")
        pair.append(p)
        pair_bf16.append(pb)
    for i, g in enumerate(row_g):
        L_, R4, D_ = g.shape
        r = R4 // N_CHIPS
        p, pb = pair_add(g.reshape(L_, N_CHIPS, 2, r // 2, D_), theirs[3 + i], pos, row_sharded=True,
                         name=f"rs_pair_add_row{i}")
        pair.append(p)
        pair_bf16.append(pb)
    landed = scatter_to_chips(pair_bf16[:3], pair_bf16[3:], name="rs_scatter")
    halves = [chip_add(pair[i], landed[i], pos, row_sharded=False, name=f"rs_chip_add_col{i}") for i in range(3)]
    for i in range(3, 6):
        t = chip_add(pair[i], landed[i], pos, row_sharded=True, name=f"rs_chip_add_row{i - 3}")
        halves.append(t.reshape(t.shape[0], 2 * t.shape[2], t.shape[3]))
    shards = sibling_join_halves(halves, name="rs_sibling_join")
    big_names = col_names + row_names
    big_w = dict(attn_w_in=attn_w_in, hgrn_w_in=hgrn_w_in, ffn_w_up=ffn_w_up, attn_w_out=attn_w_out,
                 hgrn_w_out=hgrn_w_out, ffn_w_down=ffn_w_down)
    big_m = dict(attn_w_in=m_attn_w_in, hgrn_w_in=m_hgrn_w_in, ffn_w_up=m_ffn_w_up, attn_w_out=m_attn_w_out,
                 hgrn_w_out=m_hgrn_w_out, ffn_w_down=m_ffn_w_down)
    big_v = dict(attn_w_in=v_attn_w_in, hgrn_w_in=v_hgrn_w_in, ffn_w_up=v_ffn_w_up, attn_w_out=v_attn_w_out,
                 hgrn_w_out=v_hgrn_w_out, ffn_w_down=v_ffn_w_down)
    grads, deltas, new_m, new_v = {}, {}, {}, {}
    for name_, gshard in zip(big_names, shards):
        w_ = big_w[name_]
        cols = w_.shape[-1]
        d_, m_, v_ = adamw(w_.reshape(-1, cols), gshard.reshape(-1, cols), big_m[name_].reshape(-1, cols),
                           big_v[name_].reshape(-1, cols), name=f"adamw_{name_}")
        grads[name_] = gshard.reshape(w_.shape)
        deltas[name_], new_m[name_], new_v[name_] = (t.reshape(w_.shape) for t in (d_, m_, v_))

    gconv = jnp.stack(d_conv_w)
    n_conv = ffn_conv_w.shape[2]
    gconv_all = _unpack_small(all_reduce_small(_pack_small([gconv]), name="all_reduce_conv_w"), [gconv])[0]
    gconv_mine = lax.dynamic_slice_in_dim(gconv_all, me * n_conv, n_conv, axis=2)
    cshape = ffn_conv_w.shape
    d_, m_, v_ = adamw(ffn_conv_w.reshape(-1, n_conv), gconv_mine.reshape(-1, n_conv),
                       m_ffn_conv_w.reshape(-1, n_conv), v_ffn_conv_w.reshape(-1, n_conv), name="adamw_ffn_conv_w")
    grads["ffn_conv_w"] = gconv_mine
    deltas["ffn_conv_w"], new_m["ffn_conv_w"], new_v["ffn_conv_w"] = (t.reshape(cshape) for t in (d_, m_, v_))

    order = ["norm_mix", "norm_ffn", "norm_final", "attn_w_in", "attn_w_out", "attn_sinks", "hgrn_w_in",
             "hgrn_w_out", "hgrn_norm", "hgrn_lb_logits", "ffn_w_up", "ffn_conv_w", "ffn_conv_b", "ffn_w_down"]
    small_names = ["norm_mix", "norm_ffn", "norm_final", "attn_sinks", "hgrn_norm", "hgrn_lb_logits", "ffn_conv_b"]
    for i, name_ in enumerate(small_names):
        grads[name_], deltas[name_], new_m[name_], new_v[name_] = small_g[i], small_d[i], small_nm[i], small_nv[i]
    return (loss, grad_x, *[grads[n] for n in order], *[deltas[n] for n in order],
            *[new_m[n] for n in order], *[new_v[n] for n in order])
```

```python
import functools
import math

import numpy as np
import jax
import jax.numpy as jnp
from jax import lax
from jax.experimental import pallas as pl
from jax.experimental.pallas import tpu as pltpu

F32 = jnp.float32
BF16 = jnp.bfloat16
MESH = pl.DeviceIdType.MESH

HEAD_DIM = 64
Q_PER_KV = 4
WINDOW = 128
HG_EXPAND = 128
HG_CHUNK = 128
EPS = 1e-6
N_CHIPS = 4
N_DEV = 8

ADAM_LR = 0.001
ADAM_B1 = 0.9
ADAM_B2 = 0.999
ADAM_EPS = 1e-08
ADAM_WD = 0.01
ADAM_STEP = 10

VMEM_LIMIT_BYTES = 48 * 1024 * 1024
LANES = 128
SUBLANES = 8


def _params(sem=None):
    return pltpu.CompilerParams(dimension_semantics=sem, vmem_limit_bytes=VMEM_LIMIT_BYTES)


def _pick(n, cands):
    for c in cands:
        if n % c == 0:
            return c
    return n


def _dot(a, b, dims):
    return lax.dot_general(a.astype(BF16), b.astype(BF16), (dims, ((), ())),
                           preferred_element_type=F32)


NN = ((1,), (0,))
NT = ((1,), (1,))
TN = ((0,), (0,))


def mm_nn(a, w, l, *, name, res=None, out_dtype=F32, split_out=False):
    M, K = a.shape
    N = w.shape[2]
    tm = _pick(M, (1024, 512, 256, 128))
    tn = _pick(N // 2 if split_out else N, (1408, 1024, 768, 512, 256, 128))
    tk = _pick(K, (1024, 1408, 512, 256, 128))
    nk = K // tk
    nh = (N // 2) // tn

    def body(*refs):
        if res is None:
            a_ref, w_ref, o_ref = refs[:3]
            r_ref = None
        else:
            a_ref, w_ref, r_ref, o_ref = refs[:4]
        part = _dot(a_ref[...], w_ref[...], NN)

        def finish(acc):
            if r_ref is not None:
                acc = acc + r_ref[...]
            o_ref[...] = acc.astype(out_dtype)

        if nk == 1:
            finish(part)
        else:
            acc_ref = refs[-1]
            k = pl.program_id(2)

            @pl.when(k == 0)
            def _():
                acc_ref[...] = part

            @pl.when(k > 0)
            def _():
                acc_ref[...] += part

            @pl.when(k == nk - 1)
            def _():
                finish(acc_ref[...])

    in_specs = [pl.BlockSpec((tm, tk), lambda n, m, k: (m, k)),
                pl.BlockSpec((None, tk, tn), lambda n, m, k: (l, k, n))]
    args = [a, w]
    if res is not None:
        in_specs.append(pl.BlockSpec((tm, tn), lambda n, m, k: (m, n)))
        args.append(res)
    if split_out:
        out_shape = jax.ShapeDtypeStruct((2, M, N // 2), out_dtype)
        out_spec = pl.BlockSpec((None, tm, tn), lambda n, m, k: (n // nh, m, n % nh))
    else:
        out_shape = jax.ShapeDtypeStruct((M, N), out_dtype)
        out_spec = pl.BlockSpec((tm, tn), lambda n, m, k: (m, n))
    return pl.pallas_call(
        body, name=name, grid=(N // tn, M // tm, nk), in_specs=in_specs, out_specs=out_spec,
        out_shape=out_shape,
        scratch_shapes=[] if nk == 1 else [pltpu.VMEM((tm, tn), F32)],
        compiler_params=_params(("parallel", "parallel", "arbitrary")),
    )(*args)


def mm_nt(a, w, l, *, name, split_in=False, out_dtype=F32):
    if split_in:
        _, M, Nh = a.shape
        N = 2 * Nh
    else:
        M, N = a.shape
        Nh = N
    K = w.shape[1]
    tm = _pick(M, (1024, 512, 256, 128))
    tko = _pick(K, (1024, 1408, 512, 256, 128))
    tr = _pick(Nh, (1024, 1408, 768, 512, 256, 128))
    nr = N // tr
    nh = Nh // tr

    def body(a_ref, w_ref, o_ref, *scratch):
        part = _dot(a_ref[...], w_ref[...], NT)
        if nr == 1:
            o_ref[...] = part.astype(out_dtype)
        else:
            acc_ref = scratch[0]
            r = pl.program_id(2)

            @pl.when(r == 0)
            def _():
                acc_ref[...] = part

            @pl.when(r > 0)
            def _():
                acc_ref[...] += part

            @pl.when(r == nr - 1)
            def _():
                o_ref[...] = acc_ref[...].astype(out_dtype)

    if split_in:
        a_spec = pl.BlockSpec((None, tm, tr), lambda ko, m, r: (r // nh, m, r % nh))
    else:
        a_spec = pl.BlockSpec((tm, tr), lambda ko, m, r: (m, r))
    return pl.pallas_call(
        body, name=name, grid=(K // tko, M // tm, nr),
        in_specs=[a_spec, pl.BlockSpec((None, tko, tr), lambda ko, m, r: (l, ko, r))],
        out_specs=pl.BlockSpec((tm, tko), lambda ko, m, r: (m, ko)),
        out_shape=jax.ShapeDtypeStruct((M, K), out_dtype),
        scratch_shapes=[] if nr == 1 else [pltpu.VMEM((tm, tko), F32)],
        compiler_params=_params(("parallel", "parallel", "arbitrary")),
    )(a, w)


def mm_tn(a, b, buf, l, n_layers, *, name, split_b=False):
    M, K = a.shape
    if split_b:
        Nh = b.shape[2]
        N = 2 * Nh
    else:
        N = b.shape[1]
        Nh = N
    tm = _pick(M, (512, 256, 128))
    tko = _pick(K, (1024, 1408, 704, 512, 256, 128))
    tn = _pick(Nh, (512, 256, 128))
    nm = M // tm
    nh = Nh // tn

    def body(*refs):
        a_ref, b_ref = refs[:2]
        o_ref = refs[-1]
        part = _dot(a_ref[...], b_ref[...], TN)
        r = pl.program_id(2)

        @pl.when(r == 0)
        def _():
            o_ref[...] = part

        @pl.when(r > 0)
        def _():
            o_ref[...] += part

    if split_b:
        b_spec = pl.BlockSpec((None, tm, tn), lambda ko, n, r: (n // nh, r, n % nh))
    else:
        b_spec = pl.BlockSpec((tm, tn), lambda ko, n, r: (r, n))
    in_specs = [pl.BlockSpec((tm, tko), lambda ko, n, r: (r, ko)), b_spec]
    args = [a, b]
    aliases = {}
    if buf is not None:
        in_specs.append(pl.BlockSpec(memory_space=pl.ANY))
        args.append(buf)
        aliases = {2: 0}
    return pl.pallas_call(
        body, name=name, grid=(K // tko, N // tn, nm), in_specs=in_specs,
        out_specs=pl.BlockSpec((None, tko, tn), lambda ko, n, r: (l, ko, n)),
        out_shape=jax.ShapeDtypeStruct((n_layers, K, N), F32),
        input_output_aliases=aliases,
        compiler_params=_params(("parallel", "parallel", "arbitrary")),
    )(*args)


def mm_dw(at, b, buf, l, n_layers, *, name, split_b=False):
    K, M = at.shape
    if split_b:
        Nh = b.shape[2]
        N = 2 * Nh
    else:
        N = b.shape[1]
        Nh = N
    tm = _pick(M, (1024, 512, 256, 128))
    tko = _pick(K, (1024, 1408, 704, 512, 256, 128))
    tn = _pick(Nh, (1408, 1024, 768, 512, 256, 128))
    nm = M // tm
    nh = Nh // tn

    def body(*refs):
        a_ref, b_ref = refs[:2]
        o_ref = refs[-1]
        part = _dot(a_ref[...], b_ref[...], NN)
        r = pl.program_id(2)

        @pl.when(r == 0)
        def _():
            o_ref[...] = part

        @pl.when(r > 0)
        def _():
            o_ref[...] += part

    if split_b:
        b_spec = pl.BlockSpec((None, tm, tn), lambda ko, n, r: (n // nh, r, n % nh))
    else:
        b_spec = pl.BlockSpec((tm, tn), lambda ko, n, r: (r, n))
    in_specs = [pl.BlockSpec((tko, tm), lambda ko, n, r: (ko, r)), b_spec]
    args = [at, b]
    aliases = {}
    if buf is not None:
        in_specs.append(pl.BlockSpec(memory_space=pl.ANY))
        args.append(buf)
        aliases = {2: 0}
    return pl.pallas_call(
        body, name=name, grid=(K // tko, N // tn, nm), in_specs=in_specs,
        out_specs=pl.BlockSpec((None, tko, tn), lambda ko, n, r: (l, ko, n)),
        out_shape=jax.ShapeDtypeStruct((n_layers, K, N), F32),
        input_output_aliases=aliases,
        compiler_params=_params(("parallel", "parallel", "arbitrary")),
    )(*args)


def _rms(x, g):
    return x * lax.rsqrt(jnp.mean(x * x, axis=-1, keepdims=True) + EPS) * g


def rmsnorm_fwd(h, g, l, *, name):
    S, D = h.shape
    tm = _pick(S, (512, 256, 128))

    def body(h_ref, g_ref, o_ref, ot_ref):
        y = _rms(h_ref[...], g_ref[...])
        o_ref[...] = y.astype(BF16)
        ot_ref[...] = y.T.astype(BF16)

    return pl.pallas_call(
        body, name=name, grid=(S // tm,),
        in_specs=[pl.BlockSpec((tm, D), lambda m: (m, 0)),
                  pl.BlockSpec((None, 1, D), lambda m: (l, 0, 0))],
        out_specs=[pl.BlockSpec((tm, D), lambda m: (m, 0)), pl.BlockSpec((D, tm), lambda m: (0, m))],
        out_shape=[jax.ShapeDtypeStruct((S, D), BF16), jax.ShapeDtypeStruct((D, S), BF16)],
        compiler_params=_params(("parallel",)),
    )(h, g)


def rmsnorm_bwd(h, g, l, dhn, dres, *, name):
    S, D = h.shape
    tm = _pick(S, (512, 256, 128))

    def body(h_ref, g_ref, dhn_ref, dres_ref, dh_ref, dg_ref):
        _, vjp = jax.vjp(_rms, h_ref[...], g_ref[...])
        dx, dg = vjp(dhn_ref[...])
        dh_ref[...] = dres_ref[...] + dx

        @pl.when(pl.program_id(0) == 0)
        def _():
            dg_ref[...] = dg

        @pl.when(pl.program_id(0) > 0)
        def _():
            dg_ref[...] += dg

    row = pl.BlockSpec((tm, D), lambda m: (m, 0))
    return pl.pallas_call(
        body, name=name, grid=(S // tm,),
        in_specs=[row, pl.BlockSpec((None, 1, D), lambda m: (l, 0, 0)), row, row],
        out_specs=[row, pl.BlockSpec((1, D), lambda m: (0, 0))],
        out_shape=[jax.ShapeDtypeStruct((S, D), F32), jax.ShapeDtypeStruct((1, D), F32)],
        compiler_params=_params(("arbitrary",)),
    )(h, g, dhn, dres)


def loss_head(h, g, target, *, name):
    S, D = h.shape
    tm = _pick(S, (512, 256, 128))

    def body(h_ref, g_ref, t_ref, loss_ref, dh_ref, dg_ref):
        y, vjp = jax.vjp(_rms, h_ref[...], g_ref[...])
        err = y - t_ref[...]
        part = 0.5 * jnp.sum(jnp.sum(err * err, axis=-1, keepdims=True) / D, axis=0, keepdims=True)
        dx, dg = vjp(err / D)
        dh_ref[...] = dx

        @pl.when(pl.program_id(0) == 0)
        def _():
            dg_ref[...] = dg
            loss_ref[...] = part

        @pl.when(pl.program_id(0) > 0)
        def _():
            dg_ref[...] += dg
            loss_ref[...] += part

    row = pl.BlockSpec((tm, D), lambda m: (m, 0))
    return pl.pallas_call(
        body, name=name, grid=(S // tm,),
        in_specs=[row, pl.BlockSpec((1, D), lambda m: (0, 0)), row],
        out_specs=[pl.BlockSpec((1, 1), lambda m: (0, 0)), row, pl.BlockSpec((1, D), lambda m: (0, 0))],
        out_shape=[jax.ShapeDtypeStruct((1, 1), F32), jax.ShapeDtypeStruct((S, D), F32),
                   jax.ShapeDtypeStruct((1, D), F32)],
        compiler_params=_params(("arbitrary",)),
    )(h, g, target)


def _make_mm(dims_fwd, dims_da, dims_db, swap_da=False, swap_db=False):
    @jax.custom_vjp
    def mm(a, b):
        return _dot(a, b, dims_fwd)

    def fwd(a, b):
        return mm(a, b), (a, b)

    def bwd(resid, g):
        a, b = resid
        da = _dot(b, g, dims_da) if swap_da else _dot(g, b, dims_da)
        db = _dot(g, a, dims_db) if swap_db else _dot(a, g, dims_db)
        return da, db

    mm.defvjp(fwd, bwd)
    return mm


_mm = _make_mm(NN, NT, TN)
_mm_nt = _make_mm(NT, NN, TN, swap_db=True)
_mm_tn = _make_mm(TN, NT, NN, swap_da=True)


def _dot_hi(a, b):
    return jnp.dot(a, b, precision=lax.Precision.HIGHEST, preferred_element_type=F32)


def _sigmoid(x):
    return 0.5 * jnp.tanh(0.5 * x) + 0.5


def _silu(x):
    return x * _sigmoid(x)


def _attn_block(q, kw, vw, sink, slope, first):
    rows = q.shape[0]
    s = _mm_nt(q, kw) * (HEAD_DIM ** -0.5)
    qi = lax.broadcasted_iota(jnp.int32, (rows, 2 * WINDOW), 0) % WINDOW
    ki = lax.broadcasted_iota(jnp.int32, (rows, 2 * WINDOW), 1)
    dist = qi + WINDOW - ki
    valid = (dist >= 0) & (dist < WINDOW) & (ki >= WINDOW * first)
    s = jnp.where(valid, s - slope * dist.astype(F32), -jnp.inf)
    m = lax.stop_gradient(jnp.maximum(jnp.max(s, axis=-1, keepdims=True), sink))
    e = jnp.exp(s - m)
    denom = jnp.sum(e, axis=-1, keepdims=True) + jnp.exp(sink - m)
    return _mm(e * (1.0 / denom), vw)


def _attn_specs(Hkv, order):
    G = Q_PER_KV
    qspec = pl.BlockSpec((Hkv, G, WINDOW, HEAD_DIM), lambda n: (0, 0, order(n), 0))
    prev = pl.BlockSpec((Hkv, WINDOW, HEAD_DIM), lambda n: (0, jnp.maximum(order(n) - 1, 0), 0))
    cur = pl.BlockSpec((Hkv, WINDOW, HEAD_DIM), lambda n: (0, order(n), 0))
    col = pl.BlockSpec((Hkv, G * WINDOW, 1), lambda n: (0, 0, 0))
    return qspec, prev, cur, col


def attn_fwd(q, k, v, sink_col, slope_col, *, name):
    Hkv, G, S, _ = q.shape
    qspec, prev, cur, col = _attn_specs(Hkv, lambda n: n)

    def body(q_ref, kp_ref, kc_ref, vp_ref, vc_ref, sink_ref, slope_ref, o_ref):
        first = (pl.program_id(0) == 0).astype(jnp.int32)
        outs = []
        for h in range(Hkv):
            kw = jnp.concatenate([kp_ref[h], kc_ref[h]], axis=0)
            vw = jnp.concatenate([vp_ref[h], vc_ref[h]], axis=0)
            o = _attn_block(q_ref[h].reshape(G * WINDOW, HEAD_DIM), kw, vw, sink_ref[h], slope_ref[h], first)
            outs.append(o.reshape(G, WINDOW, HEAD_DIM).astype(BF16))
        o_ref[...] = jnp.stack(outs)

    return pl.pallas_call(
        body, name=name, grid=(S // WINDOW,),
        in_specs=[qspec, prev, cur, prev, cur, col, col], out_specs=qspec,
        out_shape=jax.ShapeDtypeStruct(q.shape, BF16),
        compiler_params=_params(("parallel",)),
    )(q, k, k, v, v, sink_col, slope_col)


def attn_bwd(q, k, v, sink_col, slope_col, do, *, name):
    Hkv, G, S, _ = q.shape
    nb = S // WINDOW
    qspec, prev, cur, col = _attn_specs(Hkv, lambda n: nb - 1 - n)

    def body(q_ref, kp_ref, kc_ref, vp_ref, vc_ref, sink_ref, slope_ref, do_ref,
             dq_ref, dk_ref, dv_ref, dsink_ref, dk_carry, dv_carry):
        @pl.when(pl.program_id(0) == 0)
        def _():
            dk_carry[...] = jnp.zeros_like(dk_carry)
            dv_carry[...] = jnp.zeros_like(dv_carry)
            dsink_ref[...] = jnp.zeros_like(dsink_ref)

        first = (pl.program_id(0) == nb - 1).astype(jnp.int32)
        dqs, dkws, dvws, dsinks = [], [], [], []
        for h in range(Hkv):
            kw = jnp.concatenate([kp_ref[h], kc_ref[h]], axis=0)
            vw = jnp.concatenate([vp_ref[h], vc_ref[h]], axis=0)
            fn = functools.partial(_attn_block, slope=slope_ref[h], first=first)
            _, vjp = jax.vjp(fn, q_ref[h].reshape(G * WINDOW, HEAD_DIM), kw, vw, sink_ref[h])
            dq, dkw, dvw, dsink = vjp(do_ref[h].reshape(G * WINDOW, HEAD_DIM).astype(F32))
            dqs.append(dq.reshape(G, WINDOW, HEAD_DIM))
            dkws.append(dkw)
            dvws.append(dvw)
            dsinks.append(dsink)
        dkw, dvw = jnp.stack(dkws), jnp.stack(dvws)
        dq_ref[...] = jnp.stack(dqs)
        dk_ref[...] = dkw[:, WINDOW:] + dk_carry[...]
        dv_ref[...] = dvw[:, WINDOW:] + dv_carry[...]
        dk_carry[...] = dkw[:, :WINDOW]
        dv_carry[...] = dvw[:, :WINDOW]
        dsink_ref[...] += jnp.stack(dsinks)

    kv_shape = jax.ShapeDtypeStruct(k.shape, F32)
    return pl.pallas_call(
        body, name=name, grid=(nb,),
        in_specs=[qspec, prev, cur, prev, cur, col, col, qspec],
        out_specs=[qspec, cur, cur, col],
        out_shape=[jax.ShapeDtypeStruct(q.shape, F32), kv_shape, kv_shape,
                   jax.ShapeDtypeStruct(sink_col.shape, F32)],
        scratch_shapes=[pltpu.VMEM((Hkv, WINDOW, HEAD_DIM), F32), pltpu.VMEM((Hkv, WINDOW, HEAD_DIM), F32)],
        compiler_params=_params(("arbitrary",)),
    )(q, k, k, v, v, sink_col, slope_col, do)


def _hgrn_consts():
    C = HG_CHUNK
    t = np.arange(C)[:, None]
    j = np.arange(C)[None, :]
    cq, ck, mk = [], [], []
    H = C // 2
    while H >= 1:
        start = (t // H) * H
        cq.append((j > start) & (j <= t))
        ck.append((j > t) & (j <= start + H))
        mk.append(((t // H) % 2 == 1) & ((j // H) == (t // H) - 1))
        H //= 2
    ltri = (j <= t)
    ops = np.concatenate([ltri] + cq + ck, axis=0).astype(np.float32)
    ops3 = np.concatenate([ops, ops, ops], axis=1)
    opst3 = np.concatenate([ops.T, ops.T, ops.T], axis=1)
    return (jnp.asarray(ops3, BF16), jnp.asarray(opst3, BF16), jnp.asarray(np.stack(mk).astype(np.float32)))


def _split3(x):
    hi = x.astype(BF16)
    r = x - hi.astype(F32)
    mid = r.astype(BF16)
    lo = (r - mid.astype(F32)).astype(BF16)
    return jnp.concatenate([hi, mid, lo], axis=0)


@jax.custom_vjp
def _seg_sums(g, ops3, opst3):
    C = g.shape[0]
    out = jnp.dot(ops3, _split3(g), preferred_element_type=F32)
    return tuple(out[i * C:(i + 1) * C] for i in range(ops3.shape[0] // C))


def _seg_sums_fwd(g, ops3, opst3):
    return _seg_sums(g, ops3, opst3), opst3


def _seg_sums_bwd(opst3, cts):
    ct = jnp.concatenate(cts, axis=0)
    return jnp.dot(opst3, _split3(ct), preferred_element_type=F32), None, None


_seg_sums.defvjp(_seg_sums_fwd, _seg_sums_bwd)


def _hgrn_head(qr, fr, iv, gr, st, logits, ng, ops3, opst3, mk, layer):
    n_levels = mk.shape[0]
    depth = logits.shape[0]
    e = jnp.exp(logits - lax.stop_gradient(jnp.max(logits, axis=0, keepdims=True)))
    sm = e / jnp.sum(e, axis=0, keepdims=True)
    row = lax.broadcasted_iota(jnp.int32, (depth, HG_EXPAND), 0)
    lb = jnp.sum(jnp.where((row >= 1) & (row <= layer), sm, 0.0), axis=0, keepdims=True)

    q = _silu(qr)
    a = jnp.log(lb)
    c = jnp.log(1.0 - lb) - (jnp.maximum(-fr, 0.0) + jnp.log(1.0 + jnp.exp(-jnp.abs(fr))))
    mx = lax.stop_gradient(jnp.maximum(a, c))
    g = mx + jnp.log(jnp.exp(a - mx) + jnp.exp(c - mx))
    k = (1.0 - lb) * _sigmoid(-fr)

    sums = _seg_sums(g, ops3, opst3)
    b = sums[0]
    b_last = jnp.sum(g, axis=0, keepdims=True)

    o = jnp.sum(q * k, axis=-1, keepdims=True) * iv
    amat = None
    for lv in range(n_levels):
        eq = jnp.exp(sums[1 + lv])
        ek = jnp.exp(sums[1 + n_levels + lv])
        part = _mm_nt(q * eq, k * ek) * mk[lv]
        amat = part if amat is None else amat + part
    o = o + _mm(amat, iv) + _mm_nt(q * jnp.exp(b), st)
    st_new = st * jnp.exp(b_last) + _mm_tn(iv, k * jnp.exp(b_last - b))

    on = o * lax.rsqrt(jnp.mean(o * o, axis=-1, keepdims=True) + EPS) * ng
    return on * _silu(gr), st_new


def _hgrn_in_specs(D, depth, order):
    C = HG_CHUNK
    nl = int(math.log2(C))
    n_ops = 1 + 2 * nl
    return [pl.BlockSpec((C, 4 * D), lambda c: (order(c), 0)),
            pl.BlockSpec((depth, D), lambda c: (0, 0)),
            pl.BlockSpec((1, HG_EXPAND), lambda c: (0, 0)),
            pl.BlockSpec((n_ops * C, 3 * C), lambda c: (0, 0)),
            pl.BlockSpec((C, 3 * n_ops * C), lambda c: (0, 0)),
            pl.BlockSpec((nl, C, C), lambda c: (0, 0, 0))]


def _hgrn_cols(D, h):
    return [slice(j * D + h * HG_EXPAND, j * D + (h + 1) * HG_EXPAND) for j in range(4)]


def hgrn_fwd(proj, logits, ng, layer, *, name):
    S = proj.shape[0]
    D = proj.shape[1] // 4
    H = D // HG_EXPAND
    C = HG_CHUNK
    nc = S // C
    consts = _hgrn_consts()

    def body(p_ref, lg_ref, ng_ref, ops3_ref, opst3_ref, mk_ref, o_ref, st_out_ref, st_ref):
        @pl.when(pl.program_id(0) == 0)
        def _():
            st_ref[...] = jnp.zeros_like(st_ref)

        st_out_ref[...] = st_ref[...]
        ops3, opst3, mk = ops3_ref[...], opst3_ref[...], mk_ref[...]
        for h in range(H):
            sl = slice(h * HG_EXPAND, (h + 1) * HG_EXPAND)
            cols = _hgrn_cols(D, h)
            out, st_new = _hgrn_head(p_ref[:, cols[0]], p_ref[:, cols[1]], p_ref[:, cols[2]], p_ref[:, cols[3]],
                                     st_ref[h], lg_ref[:, sl], ng_ref[...], ops3, opst3, mk, layer)
            o_ref[:, sl] = out.astype(BF16)
            st_ref[h] = st_new

    return pl.pallas_call(
        body, name=name, grid=(nc,),
        in_specs=_hgrn_in_specs(D, logits.shape[0], lambda c: c),
        out_specs=[pl.BlockSpec((C, D), lambda c: (c, 0)),
                   pl.BlockSpec((None, H, HG_EXPAND, HG_EXPAND), lambda c: (c, 0, 0, 0))],
        out_shape=[jax.ShapeDtypeStruct((S, D), BF16),
                   jax.ShapeDtypeStruct((nc, H, HG_EXPAND, HG_EXPAND), F32)],
        scratch_shapes=[pltpu.VMEM((H, HG_EXPAND, HG_EXPAND), F32)],
        compiler_params=_params(("arbitrary",)),
    )(proj, logits, ng, *consts)


def hgrn_bwd(proj, logits, ng, states, dout, layer, *, name):
    S = proj.shape[0]
    D = proj.shape[1] // 4
    H = D // HG_EXPAND
    C = HG_CHUNK
    nc = S // C
    depth = logits.shape[0]
    consts = _hgrn_consts()
    rev = lambda c: nc - 1 - c

    def body(p_ref, lg_ref, ng_ref, ops3_ref, opst3_ref, mk_ref, st_in_ref, do_ref,
             dp_ref, dlg_ref, dng_ref, dst_ref):
        @pl.when(pl.program_id(0) == 0)
        def _():
            dst_ref[...] = jnp.zeros_like(dst_ref)
            dlg_ref[...] = jnp.zeros_like(dlg_ref)
            dng_ref[...] = jnp.zeros_like(dng_ref)

        ops3, opst3, mk = ops3_ref[...], opst3_ref[...], mk_ref[...]
        for h in range(H):
            sl = slice(h * HG_EXPAND, (h + 1) * HG_EXPAND)
            cols = _hgrn_cols(D, h)
            fn = functools.partial(_hgrn_head, ops3=ops3, opst3=opst3, mk=mk, layer=layer)
            _, vjp = jax.vjp(fn, p_ref[:, cols[0]], p_ref[:, cols[1]], p_ref[:, cols[2]], p_ref[:, cols[3]],
                             st_in_ref[h], lg_ref[:, sl], ng_ref[...])
            dq, df, di, dg, dst, dlg, dng = vjp((do_ref[:, sl], dst_ref[h]))
            dp_ref[:, cols[0]] = dq
            dp_ref[:, cols[1]] = df
            dp_ref[:, cols[2]] = di
            dp_ref[:, cols[3]] = dg
            dst_ref[h] = dst
            dlg_ref[:, sl] += dlg
            dng_ref[...] += dng

    in_specs = _hgrn_in_specs(D, depth, rev)
    in_specs += [pl.BlockSpec((None, H, HG_EXPAND, HG_EXPAND), lambda c: (rev(c), 0, 0, 0)),
                 pl.BlockSpec((C, D), lambda c: (rev(c), 0))]
    return pl.pallas_call(
        body, name=name, grid=(nc,), in_specs=in_specs,
        out_specs=[pl.BlockSpec((C, 4 * D), lambda c: (rev(c), 0)),
                   pl.BlockSpec((depth, D), lambda c: (0, 0)), pl.BlockSpec((1, HG_EXPAND), lambda c: (0, 0))],
        out_shape=[jax.ShapeDtypeStruct((S, 4 * D), F32), jax.ShapeDtypeStruct((depth, D), F32),
                   jax.ShapeDtypeStruct((1, HG_EXPAND), F32)],
        scratch_shapes=[pltpu.VMEM((H, HG_EXPAND, HG_EXPAND), F32)],
        compiler_params=_params(("arbitrary",)),
    )(proj, logits, ng, *consts, states, dout)


def _shift_down(u, prev, shift):
    rolled = pltpu.roll(u, shift, axis=0)
    rows = lax.broadcasted_iota(jnp.int32, prev.shape, 0)
    top = jnp.where(rows < shift, pltpu.roll(prev, shift, axis=0), rolled[:SUBLANES])
    return jnp.concatenate([top, rolled[SUBLANES:]], axis=0)


def _shift_up(x, nxt, shift):
    tm = x.shape[0]
    rolled = pltpu.roll(x, tm - shift, axis=0)
    rows = lax.broadcasted_iota(jnp.int32, nxt.shape, 0)
    bottom = jnp.where(rows >= SUBLANES - shift, pltpu.roll(nxt, SUBLANES - shift, axis=0), rolled[tm - SUBLANES:])
    return jnp.concatenate([rolled[:tm - SUBLANES], bottom], axis=0)


def _conv(u, prev, w_ref, b_ref, half):
    u1 = _shift_down(u, prev, 1)
    u2 = _shift_down(u, prev, 2)
    c = b_ref[half] + w_ref[half, 0:1, :] * u2 + w_ref[half, 1:2, :] * u1 + w_ref[half, 2:3, :] * u
    return c, u1, u2


def _conv_specs(tm, tc, l):
    nprev = lambda m: jnp.maximum(m * (tm // SUBLANES) - 1, 0)
    tile = pl.BlockSpec((2, tm, tc), lambda j, m: (0, m, j))
    halo = pl.BlockSpec((2, SUBLANES, tc), lambda j, m: (0, nprev(m), j))
    wspec = pl.BlockSpec((None, 2, 3, tc), lambda j, m: (l, 0, 0, j))
    bspec = pl.BlockSpec((None, 2, 1, tc), lambda j, m: (l, 0, 0, j))
    return tile, halo, wspec, bspec


def convact_fwd(u, cw, cb, l, *, name):
    _, S, Fh = u.shape
    tm = _pick(S, (256, 128))
    tc = _pick(Fh, (1408, 512, 256, 128))
    tile, halo, wspec, bspec = _conv_specs(tm, tc, l)

    def body(u_ref, p_ref, w_ref, b_ref, o_ref, ot_ref):
        keep = (pl.program_id(1) > 0).astype(F32)
        cg, _, _ = _conv(u_ref[0], p_ref[0] * keep, w_ref, b_ref, 0)
        cv, _, _ = _conv(u_ref[1], p_ref[1] * keep, w_ref, b_ref, 1)
        act = _silu(cg) * cv
        o_ref[...] = act.astype(BF16)
        ot_ref[...] = act.T.astype(BF16)

    return pl.pallas_call(
        body, name=name, grid=(Fh // tc, S // tm), in_specs=[tile, halo, wspec, bspec],
        out_specs=[pl.BlockSpec((tm, tc), lambda j, m: (m, j)), pl.BlockSpec((tc, tm), lambda j, m: (j, m))],
        out_shape=[jax.ShapeDtypeStruct((S, Fh), BF16), jax.ShapeDtypeStruct((Fh, S), BF16)],
        compiler_params=_params(("parallel", "parallel")),
    )(u, u, cw, cb)


def convact_bwd(u, cw, cb, l, da, *, name):
    _, S, Fh = u.shape
    tm = _pick(S, (256, 128))
    tc = _pick(Fh, (1408, 512, 256, 128))
    tile, halo, wspec, bspec = _conv_specs(tm, tc, l)

    def body(u_ref, p_ref, w_ref, b_ref, da_ref, dc_ref, dw_ref, db_ref):
        first = pl.program_id(1) == 0
        keep = jnp.logical_not(first).astype(F32)
        ug, uv = u_ref[0], u_ref[1]
        cg, g1, g2 = _conv(ug, p_ref[0] * keep, w_ref, b_ref, 0)
        cv, v1, v2 = _conv(uv, p_ref[1] * keep, w_ref, b_ref, 1)
        da_t = da_ref[...]
        sg = _sigmoid(cg)
        dcv = da_t * (cg * sg)
        dcg = da_t * cv * (sg * (1.0 + cg * (1.0 - sg)))
        dc_ref[0] = dcg
        dc_ref[1] = dcv

        @pl.when(first)
        def _():
            dw_ref[...] = jnp.zeros_like(dw_ref)
            db_ref[...] = jnp.zeros_like(db_ref)

        rs = lambda t: jnp.sum(t, axis=0, keepdims=True)
        for half, dc, taps in ((0, dcg, (g2, g1, ug)), (1, dcv, (v2, v1, uv))):
            for j in range(3):
                dw_ref[half, j:j + 1, :] += rs(dc * taps[j])
            db_ref[half] += rs(dc)

    return pl.pallas_call(
        body, name=name, grid=(Fh // tc, S // tm),
        in_specs=[tile, halo, wspec, bspec, pl.BlockSpec((tm, tc), lambda j, m: (m, j))],
        out_specs=[tile, pl.BlockSpec((2, 3, tc), lambda j, m: (0, 0, j)),
                   pl.BlockSpec((2, 1, tc), lambda j, m: (0, 0, j))],
        out_shape=[jax.ShapeDtypeStruct(u.shape, F32), jax.ShapeDtypeStruct((2, 3, Fh), F32),
                   jax.ShapeDtypeStruct((2, 1, Fh), F32)],
        compiler_params=_params(("parallel", "arbitrary")),
    )(u, u, cw, cb, da)


def conv_input_grad(dc, cw, l, *, name):
    _, S, Fh = dc.shape
    tm = _pick(S, (256, 128))
    tc = _pick(Fh, (1408, 512, 256, 128))
    nblk = S // SUBLANES
    nm = S // tm

    def body(dc_ref, n_ref, w_ref, du_ref):
        keep = (pl.program_id(1) < nm - 1).astype(F32)
        for half in range(2):
            t = dc_ref[half]
            nxt = n_ref[half] * keep
            du = (w_ref[half, 2:3, :] * t + w_ref[half, 1:2, :] * _shift_up(t, nxt, 1)
                  + w_ref[half, 0:1, :] * _shift_up(t, nxt, 2))
            du_ref[half] = du.astype(BF16)

    tile = pl.BlockSpec((2, tm, tc), lambda j, m: (0, m, j))
    halo = pl.BlockSpec((2, SUBLANES, tc),
                        lambda j, m: (0, jnp.minimum((m + 1) * (tm // SUBLANES), nblk - 1), j))
    return pl.pallas_call(
        body, name=name, grid=(Fh // tc, nm),
        in_specs=[tile, halo, pl.BlockSpec((None, 2, 3, tc), lambda j, m: (l, 0, 0, j))],
        out_specs=tile, out_shape=jax.ShapeDtypeStruct(dc.shape, BF16),
        compiler_params=_params(("parallel", "parallel")),
    )(dc, dc, cw)


def _adamw_math(w, g, m, v):
    m = ADAM_B1 * m + (1.0 - ADAM_B1) * g
    v = ADAM_B2 * v + (1.0 - ADAM_B2) * (g * g)
    m_hat = m / (1.0 - ADAM_B1 ** ADAM_STEP)
    v_hat = v / (1.0 - ADAM_B2 ** ADAM_STEP)
    delta = -ADAM_LR * (m_hat / (jnp.sqrt(v_hat) + ADAM_EPS) + ADAM_WD * w)
    return delta, m, v


def _row_tile(R, Ccols, n_streams):
    for tr in (512, 256, 128, 64, 32, 16, 8):
        if R % tr == 0 and tr * Ccols * 4 * n_streams * 2 <= VMEM_LIMIT_BYTES // 2:
            return tr
    return R


def adamw(w, g, m, v, *, name):
    R, Ccols = w.shape
    tr = _row_tile(R, Ccols, 7)

    def body(w_ref, g_ref, m_ref, v_ref, d_ref, mo_ref, vo_ref):
        d, mn, vn = _adamw_math(w_ref[...], g_ref[...], m_ref[...], v_ref[...])
        d_ref[...] = d
        mo_ref[...] = mn
        vo_ref[...] = vn

    spec = pl.BlockSpec((tr, Ccols), lambda i: (i, 0))
    shp = jax.ShapeDtypeStruct((R, Ccols), F32)
    return pl.pallas_call(
        body, name=name, grid=(R // tr,), in_specs=[spec] * 4, out_specs=[spec] * 3,
        out_shape=[shp, shp, shp], compiler_params=_params(("parallel",)),
    )(w, g, m, v)


def _prefetch_call(body, pos, args, *, name, grid, in_specs, out_specs, out_shape, sem):
    return pl.pallas_call(
        lambda pos_ref, *refs: body(*refs), name=name,
        grid_spec=pltpu.PrefetchScalarGridSpec(num_scalar_prefetch=1, grid=grid, in_specs=in_specs,
                                               out_specs=out_specs),
        out_shape=out_shape, compiler_params=_params(sem),
    )(pos, *args)


def cast_place(w, pos, *, row_sharded, name):
    L, R, Ccols = w.shape

    def body(w_ref, o_ref):
        o_ref[...] = w_ref[...].astype(BF16)

    if row_sharded:
        grid = (L,)
        in_spec = pl.BlockSpec((None, R, Ccols), lambda l, pos: (l, 0, 0))
        out_spec = pl.BlockSpec((None, R, Ccols), lambda l, pos: (l, pos[0], 0))
        out_shape = jax.ShapeDtypeStruct((L, N_CHIPS * R, Ccols), BF16)
        sem = ("parallel",)
    else:
        tr = _pick(R, (512, 256, 128))
        grid = (L, R // tr)
        in_spec = pl.BlockSpec((None, tr, Ccols), lambda l, i, pos: (l, i, 0))
        out_spec = pl.BlockSpec((None, tr, Ccols), lambda l, i, pos: (l, i, pos[0]))
        out_shape = jax.ShapeDtypeStruct((L, R, N_CHIPS * Ccols), BF16)
        sem = ("parallel", "parallel")
    return _prefetch_call(body, pos, [w], name=name, grid=grid, in_specs=[in_spec], out_specs=out_spec,
                          out_shape=out_shape, sem=sem)


def pair_add(g, theirs, pos, *, row_sharded, name):
    def body(g_ref, t_ref, o_ref, ob_ref):
        s = g_ref[...] + t_ref[...]
        o_ref[...] = s
        ob_ref[...] = s.astype(BF16)

    if row_sharded:
        L, _, _, r2, D = g.shape
        grid = (L, N_CHIPS)
        g_spec = pl.BlockSpec((None, None, None, r2, D), lambda l, q, pos: (l, q, pos[1], 0, 0))
        t_spec = pl.BlockSpec((None, None, r2, D), lambda l, q, pos: (l, q, 0, 0))
    else:
        L, K, N = g.shape
        n = N // N_CHIPS
        tr = _pick(K // 2, (256, 128))
        nb = (K // 2) // tr
        grid = (L, nb, N_CHIPS)
        g_spec = pl.BlockSpec((None, tr, n), lambda l, i, j, pos: (l, pos[1] * nb + i, j))
        t_spec = pl.BlockSpec((None, tr, n), lambda l, i, j, pos: (l, i, j))
    return _prefetch_call(body, pos, [g, theirs], name=name, grid=grid, in_specs=[g_spec, t_spec],
                          out_specs=[t_spec, t_spec],
                          out_shape=[jax.ShapeDtypeStruct(theirs.shape, F32), jax.ShapeDtypeStruct(theirs.shape, BF16)],
                          sem=("parallel",) * len(grid))


def chip_add(pair, landed, pos, *, row_sharded, name):
    def body(p_ref, a_ref, b_ref, c_ref, o_ref):
        o_ref[...] = ((p_ref[...] + a_ref[...].astype(F32)) + b_ref[...].astype(F32)) + c_ref[...].astype(F32)

    if row_sharded:
        L, _, r2, D = pair.shape
        grid = (L,)
        p_spec = pl.BlockSpec((None, None, r2, D), lambda l, pos: (l, pos[0], 0, 0))
        l_specs = [pl.BlockSpec((None, None, r2, D), functools.partial(lambda l, pos, k: (k, l, 0, 0), k=k))
                   for k in range(3)]
        out_spec = pl.BlockSpec((None, None, r2, D), lambda l, pos: (l, pos[1], 0, 0))
        out_shape = jax.ShapeDtypeStruct((L, 2, r2, D), F32)
    else:
        L, K2, N = pair.shape
        n = N // N_CHIPS
        tr = _pick(K2, (256, 128))
        nb = K2 // tr
        grid = (L, nb)
        p_spec = pl.BlockSpec((None, tr, n), lambda l, i, pos: (l, i, pos[0]))
        l_specs = [pl.BlockSpec((None, None, tr, n), functools.partial(lambda l, i, pos, k: (k, l, i, 0), k=k))
                   for k in range(3)]
        out_spec = pl.BlockSpec((None, tr, n), lambda l, i, pos: (l, pos[1] * nb + i, 0))
        out_shape = jax.ShapeDtypeStruct((L, 2 * K2, n), F32)
    return _prefetch_call(body, pos, [pair, landed, landed, landed], name=name, grid=grid,
                          in_specs=[p_spec] + l_specs, out_specs=out_spec, out_shape=out_shape,
                          sem=("parallel",) * len(grid))


def _position():
    return lax.axis_index("x"), lax.axis_index("y"), lax.axis_index("c")


def _other_chips(x, y):
    return [(1 - x, y), (x, 1 - y), (1 - x, 1 - y)]


HBM_ANY = pl.BlockSpec(memory_space=pl.ANY)
COMM_PARAMS = pltpu.CompilerParams(has_side_effects=True)


def _col_half(ref, h, chip, n):
    K = ref.shape[1]
    return ref.at[:, pl.ds(h * (K // 2), K // 2), pl.ds(chip * n, n)]


def _row_half(ref, h, chip, r):
    return ref.at[:, pl.ds(chip * r + h * (r // 2), r // 2), :]


def all_gather_weights(col_bufs, row_bufs, whole_shards, *, name):
    n_col, n_row, n_whole = len(col_bufs), len(row_bufs), len(whole_shards)
    n_half = n_col + n_row
    n_in = n_half + n_whole

    def body(*refs):
        ins = refs[:n_in]
        outs = refs[n_in:2 * n_in]
        ici_send, ici_recv, d2d_send, d2d_recv, w_send, w_recv, local_sem = refs[2 * n_in:]
        x, y, c = _position()
        me = 2 * x + y
        chips = _other_chips(x, y)
        chip_ids = [2 * cx + cy for cx, cy in chips]
        sibling = (x, y, 1 - c)

        def half(ref, i, h, chip):
            if i < n_col:
                return _col_half(ref, h, chip, ref.shape[2] // N_CHIPS)
            return _row_half(ref, h, chip, ref.shape[1] // N_CHIPS)

        def half_dst(i, h, chip):
            return half(outs[i], i, h, chip)

        def whole_dst(i, chip):
            n = ins[i].shape[2]
            return outs[i].at[:, :, pl.ds(chip * n, n)]

        local = [pltpu.make_async_copy(ins[i], whole_dst(i, me), local_sem.at[i - n_half])
                 for i in range(n_half, n_in)]
        for cp in local:
            cp.start()

        def ici(i, k, src_chip):
            return pltpu.make_async_remote_copy(
                src_ref=half(ins[i], i, c, me), dst_ref=half_dst(i, c, src_chip),
                send_sem=ici_send.at[i, k], recv_sem=ici_recv.at[i, k],
                device_id=(*chips[k], c), device_id_type=MESH)

        def whole(i, k, src_chip):
            return pltpu.make_async_remote_copy(
                src_ref=ins[i], dst_ref=whole_dst(i, src_chip),
                send_sem=w_send.at[i - n_half, k], recv_sem=w_recv.at[i - n_half, k],
                device_id=(*chips[k], c), device_id_type=MESH)

        def d2d(i, k, h):
            return pltpu.make_async_remote_copy(
                src_ref=half_dst(i, h, chip_ids[k]), dst_ref=half_dst(i, h, chip_ids[k]),
                send_sem=d2d_send.at[i, k], recv_sem=d2d_recv.at[i, k],
                device_id=sibling, device_id_type=MESH)

        sent = [ici(i, k, me) for i in range(n_half) for k in range(3)]
        sent += [whole(i, k, me) for i in range(n_half, n_in) for k in range(3)]
        for cp in sent:
            cp.start()
        passed = []
        for i in range(n_half):
            for k in range(3):
                ici(i, k, chip_ids[k]).wait_recv()
                fwd = d2d(i, k, c)
                fwd.start()
                passed.append(fwd)
        for i in range(n_half):
            for k in range(3):
                d2d(i, k, 1 - c).wait_recv()
        for i in range(n_half, n_in):
            for k in range(3):
                whole(i, k, chip_ids[k]).wait_recv()
        for cp in sent + passed:
            cp.wait_send()
        for cp in local:
            cp.wait()

    ins = list(col_bufs) + list(row_bufs) + list(whole_shards)
    out_shape = [jax.ShapeDtypeStruct(a.shape, a.dtype) for a in ins[:n_half]]
    for a in whole_shards:
        out_shape.append(jax.ShapeDtypeStruct((a.shape[0], a.shape[1], N_CHIPS * a.shape[2]), a.dtype))
    return pl.pallas_call(
        body, name=name, in_specs=[HBM_ANY] * n_in, out_specs=[HBM_ANY] * n_in, out_shape=out_shape,
        input_output_aliases={i: i for i in range(n_half)},
        scratch_shapes=[pltpu.SemaphoreType.DMA((n_half, 3)), pltpu.SemaphoreType.DMA((n_half, 3)),
                        pltpu.SemaphoreType.DMA((n_half, 3)), pltpu.SemaphoreType.DMA((n_half, 3)),
                        pltpu.SemaphoreType.DMA((max(n_whole, 1), 3)), pltpu.SemaphoreType.DMA((max(n_whole, 1), 3)),
                        pltpu.SemaphoreType.DMA((max(n_whole, 1),))],
        compiler_params=COMM_PARAMS,
    )(*ins)


def sibling_swap_halves(col_grads, row_grads, *, name):
    n_col, n_row = len(col_grads), len(row_grads)
    n_in = n_col + n_row

    def body(*refs):
        ins = refs[:n_in]
        outs = refs[n_in:2 * n_in]
        send_sem, recv_sem = refs[2 * n_in:]
        x, y, c = _position()
        sibling = (x, y, 1 - c)
        h = 1 - c
        cps = []
        for i in range(n_in):
            if i < n_col:
                K = ins[i].shape[1]
                cps.append(pltpu.make_async_remote_copy(
                    src_ref=ins[i].at[:, pl.ds(h * (K // 2), K // 2), :], dst_ref=outs[i],
                    send_sem=send_sem.at[i, 0], recv_sem=recv_sem.at[i, 0],
                    device_id=sibling, device_id_type=MESH))
            else:
                r = ins[i].shape[1] // N_CHIPS
                for q in range(N_CHIPS):
                    cps.append(pltpu.make_async_remote_copy(
                        src_ref=ins[i].at[:, pl.ds(q * r + h * (r // 2), r // 2), :], dst_ref=outs[i].at[:, q],
                        send_sem=send_sem.at[i, q], recv_sem=recv_sem.at[i, q],
                        device_id=sibling, device_id_type=MESH))
        for cp in cps:
            cp.start()
        for cp in cps:
            cp.wait()

    out_shape = []
    for a in col_grads:
        out_shape.append(jax.ShapeDtypeStruct((a.shape[0], a.shape[1] // 2, a.shape[2]), a.dtype))
    for a in row_grads:
        r = a.shape[1] // N_CHIPS
        out_shape.append(jax.ShapeDtypeStruct((a.shape[0], N_CHIPS, r // 2, a.shape[2]), a.dtype))
    return pl.pallas_call(
        body, name=name, in_specs=[HBM_ANY] * n_in, out_specs=[HBM_ANY] * n_in, out_shape=out_shape,
        scratch_shapes=[pltpu.SemaphoreType.DMA((n_in, N_CHIPS)), pltpu.SemaphoreType.DMA((n_in, N_CHIPS))],
        compiler_params=COMM_PARAMS,
    )(*col_grads, *row_grads)


def scatter_to_chips(col_halves, row_halves, *, name):
    n_col, n_row = len(col_halves), len(row_halves)
    n_in = n_col + n_row

    def body(*refs):
        ins = refs[:n_in]
        outs = refs[n_in:2 * n_in]
        send_sem, recv_sem = refs[2 * n_in:]
        x, y, c = _position()
        chips = _other_chips(x, y)
        cps = []
        for i in range(n_in):
            for k in range(3):
                to = 2 * chips[k][0] + chips[k][1]
                if i < n_col:
                    n = ins[i].shape[2] // N_CHIPS
                    src = ins[i].at[:, :, pl.ds(to * n, n)]
                else:
                    src = ins[i].at[:, to]
                cps.append(pltpu.make_async_remote_copy(
                    src_ref=src, dst_ref=outs[i].at[k], send_sem=send_sem.at[i, k], recv_sem=recv_sem.at[i, k],
                    device_id=(*chips[k], c), device_id_type=MESH))
        for cp in cps:
            cp.start()
        for cp in cps:
            cp.wait()

    out_shape = []
    for a in col_halves:
        out_shape.append(jax.ShapeDtypeStruct((3, a.shape[0], a.shape[1], a.shape[2] // N_CHIPS), a.dtype))
    for a in row_halves:
        out_shape.append(jax.ShapeDtypeStruct((3, a.shape[0], a.shape[2], a.shape[3]), a.dtype))
    return pl.pallas_call(
        body, name=name, in_specs=[HBM_ANY] * n_in, out_specs=[HBM_ANY] * n_in, out_shape=out_shape,
        scratch_shapes=[pltpu.SemaphoreType.DMA((n_in, 3)), pltpu.SemaphoreType.DMA((n_in, 3))],
        compiler_params=COMM_PARAMS,
    )(*col_halves, *row_halves)


def sibling_join_halves(shards, *, name):
    n_in = len(shards)

    def body(*refs):
        ins = refs[:n_in]
        outs = refs[n_in:2 * n_in]
        send_sem, recv_sem = refs[2 * n_in:]
        x, y, c = _position()
        sibling = (x, y, 1 - c)

        def rows(ref, h):
            R2 = ref.shape[1] // 2
            return ref.at[:, pl.ds(h * R2, R2), :]

        sends = [pltpu.make_async_remote_copy(src_ref=rows(ins[i], c), dst_ref=rows(outs[i], c),
                                              send_sem=send_sem.at[i], recv_sem=recv_sem.at[i],
                                              device_id=sibling, device_id_type=MESH) for i in range(n_in)]
        lands = [pltpu.make_async_remote_copy(src_ref=rows(ins[i], c), dst_ref=rows(outs[i], 1 - c),
                                              send_sem=send_sem.at[i], recv_sem=recv_sem.at[i],
                                              device_id=sibling, device_id_type=MESH) for i in range(n_in)]
        for cp in sends:
            cp.start()
        for i in range(n_in):
            sends[i].wait_send()
            lands[i].wait_recv()

    return pl.pallas_call(
        body, name=name, in_specs=[HBM_ANY] * n_in, out_specs=[HBM_ANY] * n_in,
        out_shape=[jax.ShapeDtypeStruct(a.shape, a.dtype) for a in shards],
        input_output_aliases={i: i for i in range(n_in)},
        scratch_shapes=[pltpu.SemaphoreType.DMA((n_in,)), pltpu.SemaphoreType.DMA((n_in,))],
        compiler_params=COMM_PARAMS,
    )(*shards)


def all_reduce_small(v, *, name):
    R, Ccols = v.shape

    def body(v_ref, o_ref, slots, send_sem, recv_sem):
        x, y, c = _position()
        me = 4 * x + 2 * y + c
        slots[me] = v_ref[...]

        def peer(d):
            return x ^ (d >> 2), y ^ ((d >> 1) & 1), c ^ (d & 1)

        def copy(d, slot):
            return pltpu.make_async_remote_copy(
                src_ref=v_ref, dst_ref=slots.at[slot], send_sem=send_sem.at[d], recv_sem=recv_sem.at[d],
                device_id=peer(d), device_id_type=MESH)

        sends = [copy(d, me) for d in range(1, N_DEV)]
        for cp in sends:
            cp.start()
        for d in range(1, N_DEV):
            px, py, pc = peer(d)
            copy(d, 4 * px + 2 * py + pc).wait_recv()
        for cp in sends:
            cp.wait_send()
        acc = slots[0]
        for d in range(1, N_DEV):
            acc = acc + slots[d]
        o_ref[...] = acc

    return pl.pallas_call(
        body, name=name,
        in_specs=[pl.BlockSpec(memory_space=pltpu.VMEM)], out_specs=pl.BlockSpec(memory_space=pltpu.VMEM),
        out_shape=jax.ShapeDtypeStruct((R, Ccols), F32),
        scratch_shapes=[pltpu.VMEM((N_DEV, R, Ccols), F32), pltpu.SemaphoreType.DMA((N_DEV,)),
                        pltpu.SemaphoreType.DMA((N_DEV,))],
        compiler_params=COMM_PARAMS,
    )(v)


def _pack_small(parts):
    flat = jnp.concatenate([p.reshape(-1).astype(F32) for p in parts])
    n = flat.shape[0]
    rows = -(-n // LANES)
    rows = -(-rows // SUBLANES) * SUBLANES
    return jnp.pad(flat, (0, rows * LANES - n)).reshape(rows, LANES)


def _unpack_small(packed, like):
    flat = packed.reshape(-1)
    out, off = [], 0
    for p in like:
        out.append(flat[off:off + p.size].reshape(p.shape))
        off += p.size
    return out


def kernel(x, norm_mix, norm_ffn, norm_final, attn_w_in, attn_w_out, attn_sinks, hgrn_w_in, hgrn_w_out, hgrn_norm, hgrn_lb_logits, ffn_w_up, ffn_conv_w, ffn_conv_b, ffn_w_down, loss_target, m_norm_mix, m_norm_ffn, m_norm_final, m_attn_w_in, m_attn_w_out, m_attn_sinks, m_hgrn_w_in, m_hgrn_w_out, m_hgrn_norm, m_hgrn_lb_logits, m_ffn_w_up, m_ffn_conv_w, m_ffn_conv_b, m_ffn_w_down, v_norm_mix, v_norm_ffn, v_norm_final, v_attn_w_in, v_attn_w_out, v_attn_sinks, v_hgrn_w_in, v_hgrn_w_out, v_hgrn_norm, v_hgrn_lb_logits, v_ffn_w_up, v_ffn_conv_w, v_ffn_conv_b, v_ffn_w_down):
    S, D = x.shape[1], x.shape[2]
    depth = norm_mix.shape[0]
    n_q = D // HEAD_DIM
    n_kv = n_q // Q_PER_KV
    Fh = ffn_w_down.shape[1] * N_CHIPS
    h = x.reshape(S, D)
    target = loss_target.reshape(S, D)

    col_names = ["attn_w_in", "hgrn_w_in", "ffn_w_up"]
    row_names = ["attn_w_out", "hgrn_w_out", "ffn_w_down"]
    x_, y_, c_ = _position()
    me = 2 * x_ + y_
    pos = jnp.stack([me, c_]).astype(jnp.int32)
    col_w = [cast_place(w, pos, row_sharded=False, name=f"cast_place_{n}")
             for n, w in zip(col_names, (attn_w_in, hgrn_w_in, ffn_w_up))]
    row_w = [cast_place(w, pos, row_sharded=True, name=f"cast_place_{n}")
             for n, w in zip(row_names, (attn_w_out, hgrn_w_out, ffn_w_down))]
    gathered = all_gather_weights(col_w, row_w, [ffn_conv_w], name="all_gather_weights")
    W = dict(zip(col_names + row_names + ["ffn_conv_w"], gathered))
    cw = W["ffn_conv_w"].reshape(depth, 3, 2, Fh).transpose(0, 2, 1, 3)
    cb = ffn_conv_b.reshape(depth, 2, 1, Fh)
    g_mix = norm_mix.reshape(depth, 1, D)
    g_ffn = norm_ffn.reshape(depth, 1, D)

    slopes = jnp.exp2(-8.0 * jnp.arange(1, n_q + 1, dtype=F32) / n_q).reshape(n_kv, Q_PER_KV)
    slope_col = jnp.repeat(slopes, WINDOW, axis=1).reshape(n_kv, Q_PER_KV * WINDOW, 1)

    def to_heads(t, nh):
        return t.reshape(S, nh, HEAD_DIM).transpose(1, 0, 2)

    def from_heads(t):
        return t.transpose(1, 0, 2).reshape(S, -1)

    saved = []
    for layer in range(depth):
        idx = layer // 2
        rec = {"h_mix": h}
        hn, rec["hn_t"] = rmsnorm_fwd(h, g_mix, layer, name=f"norm_mix_fwd{layer}")
        if layer % 2 == 0:
            proj = mm_nn(hn, W["attn_w_in"], idx, name=f"attn_in_fwd{layer}")
            q = to_heads(proj[:, :n_q * HEAD_DIM], n_q).reshape(n_kv, Q_PER_KV, S, HEAD_DIM)
            k = to_heads(proj[:, n_q * HEAD_DIM:(n_q + n_kv) * HEAD_DIM], n_kv)
            v = to_heads(proj[:, (n_q + n_kv) * HEAD_DIM:], n_kv)
            sink_col = jnp.repeat(attn_sinks[idx].reshape(n_kv, Q_PER_KV), WINDOW, axis=1)
            sink_col = sink_col.reshape(n_kv, Q_PER_KV * WINDOW, 1)
            o_heads = attn_fwd(q, k, v, sink_col, slope_col, name=f"attn_fwd{layer}")
            o = from_heads(o_heads.reshape(n_q, S, HEAD_DIM))
            rec.update(q=q, k=k, v=v, sink_col=sink_col, o=o)
            h = mm_nn(o, W["attn_w_out"], idx, name=f"attn_out_fwd{layer}", res=h)
        else:
            proj = mm_nn(hn, W["hgrn_w_in"], idx, name=f"hgrn_in_fwd{layer}")
            ng = hgrn_norm[idx].reshape(1, HG_EXPAND)
            o, states = hgrn_fwd(proj, hgrn_lb_logits, ng, layer, name=f"hgrn_fwd{layer}")
            rec.update(proj=proj, ng=ng, states=states, o=o)
            h = mm_nn(o, W["hgrn_w_out"], idx, name=f"hgrn_out_fwd{layer}", res=h)
        rec["h_ffn"] = h
        hn2, rec["hn2_t"] = rmsnorm_fwd(h, g_ffn, layer, name=f"norm_ffn_fwd{layer}")
        u = mm_nn(hn2, W["ffn_w_up"], layer, name=f"ffn_up_fwd{layer}", split_out=True)
        act, rec["act_t"] = convact_fwd(u, cw, cb, layer, name=f"ffn_act_fwd{layer}")
        rec["u"] = u
        h = mm_nn(act, W["ffn_w_down"], layer, name=f"ffn_down_fwd{layer}", res=h)
        saved.append(rec)

    loss_part, dh, d_norm_final = loss_head(h, norm_final.reshape(1, D), target, name="loss_head")

    n_attn, n_hgrn = attn_w_in.shape[0], hgrn_w_in.shape[0]
    G = dict(attn_w_in=None, attn_w_out=None, hgrn_w_in=None, hgrn_w_out=None, ffn_w_up=None, ffn_w_down=None)
    d_norm_mix, d_norm_ffn = [None] * depth, [None] * depth
    d_conv_w, d_conv_b = [None] * depth, [None] * depth
    d_sinks, d_hgrn_norm = [None] * n_attn, [None] * n_hgrn
    d_logits = jnp.zeros_like(hgrn_lb_logits)
    for layer in reversed(range(depth)):
        idx = layer // 2
        rec = saved[layer]
        G["ffn_w_down"] = mm_dw(rec["act_t"], dh, G["ffn_w_down"], layer, depth, name=f"ffn_down_dw{layer}")
        dact = mm_nt(dh, W["ffn_w_down"], layer, name=f"ffn_down_dx{layer}")
        dc, dcw, dcb = convact_bwd(rec["u"], cw, cb, layer, dact, name=f"ffn_act_bwd{layer}")
        d_conv_w[layer] = dcw.transpose(1, 0, 2).reshape(3, 2 * Fh)
        d_conv_b[layer] = dcb.reshape(2 * Fh)
        du = conv_input_grad(dc, cw, layer, name=f"ffn_conv_dx{layer}")
        G["ffn_w_up"] = mm_dw(rec["hn2_t"], du, G["ffn_w_up"], layer, depth, name=f"ffn_up_dw{layer}", split_b=True)
        dhn2 = mm_nt(du, W["ffn_w_up"], layer, name=f"ffn_up_dx{layer}", split_in=True)
        dh, dg = rmsnorm_bwd(rec["h_ffn"], g_ffn, layer, dhn2, dh, name=f"norm_ffn_bwd{layer}")
        d_norm_ffn[layer] = dg.reshape(D)
        if layer % 2 == 0:
            G["attn_w_out"] = mm_tn(rec["o"], dh, G["attn_w_out"], idx, n_attn, name=f"attn_out_dw{layer}")
            do = mm_nt(dh, W["attn_w_out"], idx, name=f"attn_out_dx{layer}", out_dtype=BF16)
            do_heads = to_heads(do, n_q).reshape(n_kv, Q_PER_KV, S, HEAD_DIM)
            dq, dk, dv, dsink = attn_bwd(rec["q"], rec["k"], rec["v"], rec["sink_col"], slope_col, do_heads,
                                         name=f"attn_bwd{layer}")
            dproj = jnp.concatenate([from_heads(dq.reshape(n_q, S, HEAD_DIM)), from_heads(dk), from_heads(dv)], axis=1)
            d_sinks[idx] = jnp.sum(dsink.reshape(n_kv, Q_PER_KV, WINDOW), axis=-1).reshape(n_q)
            G["attn_w_in"] = mm_dw(rec["hn_t"], dproj, G["attn_w_in"], idx, n_attn, name=f"attn_in_dw{layer}")
            dhn = mm_nt(dproj, W["attn_w_in"], idx, name=f"attn_in_dx{layer}")
        else:
            G["hgrn_w_out"] = mm_tn(rec["o"], dh, G["hgrn_w_out"], idx, n_hgrn, name=f"hgrn_out_dw{layer}")
            do = mm_nt(dh, W["hgrn_w_out"], idx, name=f"hgrn_out_dx{layer}")
            dproj, dlg, dng = hgrn_bwd(rec["proj"], hgrn_lb_logits, rec["ng"], rec["states"], do, layer,
                                       name=f"hgrn_bwd{layer}")
            d_logits = d_logits + dlg
            d_hgrn_norm[idx] = dng.reshape(HG_EXPAND)
            G["hgrn_w_in"] = mm_dw(rec["hn_t"], dproj, G["hgrn_w_in"], idx, n_hgrn, name=f"hgrn_in_dw{layer}")
            dhn = mm_nt(dproj, W["hgrn_w_in"], idx, name=f"hgrn_in_dx{layer}")
        dh, dg = rmsnorm_bwd(rec["h_mix"], g_mix, layer, dhn, dh, name=f"norm_mix_bwd{layer}")
        d_norm_mix[layer] = dg.reshape(D)
    grad_x = dh.reshape(x.shape)

    small_w = [norm_mix, norm_ffn, norm_final, attn_sinks, hgrn_norm, hgrn_lb_logits, ffn_conv_b]
    small_m = [m_norm_mix, m_norm_ffn, m_norm_final, m_attn_sinks, m_hgrn_norm, m_hgrn_lb_logits, m_ffn_conv_b]
    small_v = [v_norm_mix, v_norm_ffn, v_norm_final, v_attn_sinks, v_hgrn_norm, v_hgrn_lb_logits, v_ffn_conv_b]
    small_grads = [jnp.stack(d_norm_mix), jnp.stack(d_norm_ffn), d_norm_final.reshape(D), jnp.stack(d_sinks),
                   jnp.stack(d_hgrn_norm), d_logits, jnp.stack(d_conv_b)]
    reduced = all_reduce_small(_pack_small(small_grads + [loss_part]), name="all_reduce_small")
    small_g = _unpack_small(reduced, small_w + [loss_part])
    loss = small_g.pop().reshape(())
    pad = [jnp.zeros((1, 1), F32)]
    sd, sm_, sv_ = adamw(_pack_small(small_w + pad), reduced, _pack_small(small_m + pad),
                         _pack_small(small_v + pad), name="adamw_small")
    small_d = _unpack_small(sd, small_w)
    small_nm = _unpack_small(sm_, small_w)
    small_nv = _unpack_small(sv_, small_w)

    col_g = [G["attn_w_in"], G["hgrn_w_in"], G["ffn_w_up"]]
    row_g = [G["attn_w_out"], G["hgrn_w_out"], G["ffn_w_down"]]
    theirs = sibling_swap_halves(col_g, row_g, name="rs_sibling_swap")
    pair, pair_bf16 = [], []
    for i, g in enumerate(col_g):
        p, pb = pair_add(g, theirs[i], pos, row_sharded=False, name=f"rs_pair_add_col{i}")
        pair.append(p)
        pair_bf16.append(pb)
    for i, g in enumerate(row_g):
        L_, R4, D_ = g.shape
        r = R4 // N_CHIPS
        p, pb = pair_add(g.reshape(L_, N_CHIPS, 2, r // 2, D_), theirs[3 + i], pos, row_sharded=True,
                         name=f"rs_pair_add_row{i}")
        pair.append(p)
        pair_bf16.append(pb)
    landed = scatter_to_chips(pair_bf16[:3], pair_bf16[3:], name="rs_scatter")
    halves = [chip_add(pair[i], landed[i], pos, row_sharded=False, name=f"rs_chip_add_col{i}") for i in range(3)]
    for i in range(3, 6):
        t = chip_add(pair[i], landed[i], pos, row_sharded=True, name=f"rs_chip_add_row{i - 3}")
        halves.append(t.reshape(t.shape[0], 2 * t.shape[2], t.shape[3]))
    shards = sibling_join_halves(halves, name="rs_sibling_join")
    big_names = col_names + row_names
    big_w = dict(attn_w_in=attn_w_in, hgrn_w_in=hgrn_w_in, ffn_w_up=ffn_w_up, attn_w_out=attn_w_out,
                 hgrn_w_out=hgrn_w_out, ffn_w_down=ffn_w_down)
    big_m = dict(attn_w_in=m_attn_w_in, hgrn_w_in=m_hgrn_w_in, ffn_w_up=m_ffn_w_up, attn_w_out=m_attn_w_out,
                 hgrn_w_out=m_hgrn_w_out, ffn_w_down=m_ffn_w_down)
    big_v = dict(attn_w_in=v_attn_w_in, hgrn_w_in=v_hgrn_w_in, ffn_w_up=v_ffn_w_up, attn_w_out=v_attn_w_out,
                 hgrn_w_out=v_hgrn_w_out, ffn_w_down=v_ffn_w_down)
    grads, deltas, new_m, new_v = {}, {}, {}, {}
    for name_, gshard in zip(big_names, shards):
        w_ = big_w[name_]
        cols = w_.shape[-1]
        d_, m_, v_ = adamw(w_.reshape(-1, cols), gshard.reshape(-1, cols), big_m[name_].reshape(-1, cols),
                           big_v[name_].reshape(-1, cols), name=f"adamw_{name_}")
        grads[name_] = gshard.reshape(w_.shape)
        deltas[name_], new_m[name_], new_v[name_] = (t.reshape(w_.shape) for t in (d_, m_, v_))

    gconv = jnp.stack(d_conv_w)
    n_conv = ffn_conv_w.shape[2]
    gconv_all = _unpack_small(all_reduce_small(_pack_small([gconv]), name="all_reduce_conv_w"), [gconv])[0]
    gconv_mine = lax.dynamic_slice_in_dim(gconv_all, me * n_conv, n_conv, axis=2)
    cshape = ffn_conv_w.shape
    d_, m_, v_ = adamw(ffn_conv_w.reshape(-1, n_conv), gconv_mine.reshape(-1, n_conv),
                       m_ffn_conv_w.reshape(-1, n_conv), v_ffn_conv_w.reshape(-1, n_conv), name="adamw_ffn_conv_w")
    grads["ffn_conv_w"] = gconv_mine
    deltas["ffn_conv_w"], new_m["ffn_conv_w"], new_v["ffn_conv_w"] = (t.reshape(cshape) for t in (d_, m_, v_))

    order = ["norm_mix", "norm_ffn", "norm_final", "attn_w_in", "attn_w_out", "attn_sinks", "hgrn_w_in",
             "hgrn_w_out", "hgrn_norm", "hgrn_lb_logits", "ffn_w_up", "ffn_conv_w", "ffn_conv_b", "ffn_w_down"]
    small_names = ["norm_mix", "norm_ffn", "norm_final", "attn_sinks", "hgrn_norm", "hgrn_lb_logits", "ffn_conv_b"]
    for i, name_ in enumerate(small_names):
        grads[name_], deltas[name_], new_m[name_], new_v[name_] = small_g[i], small_d[i], small_nm[i], small_nv[i]
    return (loss, grad_x, *[grads[n] for n in order], *[deltas[n] for n in order],
            *[new_m[n] for n in order], *[new_v[n] for n in order])
```

```python
import functools
import math

import numpy as np
import jax
import jax.numpy as jnp
from jax import lax
from jax.experimental import pallas as pl
from jax.experimental.pallas import tpu as pltpu

F32 = jnp.float32
BF16 = jnp.bfloat16
MESH = pl.DeviceIdType.MESH

HEAD_DIM = 64
Q_PER_KV = 4
WINDOW = 128
HG_EXPAND = 128
HG_CHUNK = 128
EPS = 1e-6
N_CHIPS = 4
N_DEV = 8

ADAM_LR = 0.001
ADAM_B1 = 0.9
ADAM_B2 = 0.999
ADAM_EPS = 1e-08
ADAM_WD = 0.01
ADAM_STEP = 10

VMEM_LIMIT_BYTES = 48 * 1024 * 1024
LANES = 128
SUBLANES = 8


def _params(sem=None):
    return pltpu.CompilerParams(dimension_semantics=sem, vmem_limit_bytes=VMEM_LIMIT_BYTES)


def _pick(n, cands):
    for c in cands:
        if n % c == 0:
            return c
    return n


def _dot(a, b, dims):
    return lax.dot_general(a.astype(BF16), b.astype(BF16), (dims, ((), ())),
                           preferred_element_type=F32)


NN = ((1,), (0,))
NT = ((1,), (1,))
TN = ((0,), (0,))


def mm_nn(a, w, l, *, name, res=None, out_dtype=F32, split_out=False):
    M, K = a.shape
    N = w.shape[2]
    tm = _pick(M, (1024, 512, 256, 128))
    tn = _pick(N // 2 if split_out else N, (1408, 1024, 768, 512, 256, 128))
    tk = _pick(K, (1024, 1408, 512, 256, 128))
    nk = K // tk
    nh = (N // 2) // tn

    def body(*refs):
        if res is None:
            a_ref, w_ref, o_ref = refs[:3]
            r_ref = None
        else:
            a_ref, w_ref, r_ref, o_ref = refs[:4]
        part = _dot(a_ref[...], w_ref[...], NN)

        def finish(acc):
            if r_ref is not None:
                acc = acc + r_ref[...]
            o_ref[...] = acc.astype(out_dtype)

        if nk == 1:
            finish(part)
        else:
            acc_ref = refs[-1]
            k = pl.program_id(2)

            @pl.when(k == 0)
            def _():
                acc_ref[...] = part

            @pl.when(k > 0)
            def _():
                acc_ref[...] += part

            @pl.when(k == nk - 1)
            def _():
                finish(acc_ref[...])

    in_specs = [pl.BlockSpec((tm, tk), lambda n, m, k: (m, k)),
                pl.BlockSpec((None, tk, tn), lambda n, m, k: (l, k, n))]
    args = [a, w]
    if res is not None:
        in_specs.append(pl.BlockSpec((tm, tn), lambda n, m, k: (m, n)))
        args.append(res)
    if split_out:
        out_shape = jax.ShapeDtypeStruct((2, M, N // 2), out_dtype)
        out_spec = pl.BlockSpec((None, tm, tn), lambda n, m, k: (n // nh, m, n % nh))
    else:
        out_shape = jax.ShapeDtypeStruct((M, N), out_dtype)
        out_spec = pl.BlockSpec((tm, tn), lambda n, m, k: (m, n))
    return pl.pallas_call(
        body, name=name, grid=(N // tn, M // tm, nk), in_specs=in_specs, out_specs=out_spec,
        out_shape=out_shape,
        scratch_shapes=[] if nk == 1 else [pltpu.VMEM((tm, tn), F32)],
        compiler_params=_params(("parallel", "parallel", "arbitrary")),
    )(*args)


def mm_nt(a, w, l, *, name, split_in=False, out_dtype=F32):
    if split_in:
        _, M, Nh = a.shape
        N = 2 * Nh
    else:
        M, N = a.shape
        Nh = N
    K = w.shape[1]
    tm = _pick(M, (1024, 512, 256, 128))
    tko = _pick(K, (1024, 1408, 512, 256, 128))
    tr = _pick(Nh, (1024, 1408, 768, 512, 256, 128))
    nr = N // tr
    nh = Nh // tr

    def body(a_ref, w_ref, o_ref, *scratch):
        part = _dot(a_ref[...], w_ref[...], NT)
        if nr == 1:
            o_ref[...] = part.astype(out_dtype)
        else:
            acc_ref = scratch[0]
            r = pl.program_id(2)

            @pl.when(r == 0)
            def _():
                acc_ref[...] = part

            @pl.when(r > 0)
            def _():
                acc_ref[...] += part

            @pl.when(r == nr - 1)
            def _():
                o_ref[...] = acc_ref[...].astype(out_dtype)

    if split_in:
        a_spec = pl.BlockSpec((None, tm, tr), lambda ko, m, r: (r // nh, m, r % nh))
    else:
        a_spec = pl.BlockSpec((tm, tr), lambda ko, m, r: (m, r))
    return pl.pallas_call(
        body, name=name, grid=(K // tko, M // tm, nr),
        in_specs=[a_spec, pl.BlockSpec((None, tko, tr), lambda ko, m, r: (l, ko, r))],
        out_specs=pl.BlockSpec((tm, tko), lambda ko, m, r: (m, ko)),
        out_shape=jax.ShapeDtypeStruct((M, K), out_dtype),
        scratch_shapes=[] if nr == 1 else [pltpu.VMEM((tm, tko), F32)],
        compiler_params=_params(("parallel", "parallel", "arbitrary")),
    )(a, w)


def mm_tn(a, b, *, name):
    M, K = a.shape
    N = b.shape[1]
    tm = _pick(M, (512, 256, 128))
    tko = _pick(K, (1024, 1408, 704, 512, 256, 128))
    tn = _pick(N, (512, 256, 128))
    nm = M // tm

    def body(a_ref, b_ref, o_ref):
        part = _dot(a_ref[...], b_ref[...], TN)
        r = pl.program_id(2)

        @pl.when(r == 0)
        def _():
            o_ref[...] = part

        @pl.when(r > 0)
        def _():
            o_ref[...] += part

    return pl.pallas_call(
        body, name=name, grid=(K // tko, N // tn, nm),
        in_specs=[pl.BlockSpec((tm, tko), lambda ko, n, r: (r, ko)), pl.BlockSpec((tm, tn), lambda ko, n, r: (r, n))],
        out_specs=pl.BlockSpec((None, tko, tn), lambda ko, n, r: (0, ko, n)),
        out_shape=jax.ShapeDtypeStruct((1, K, N), F32),
        compiler_params=_params(("parallel", "parallel", "arbitrary")),
    )(a, b)


def mm_dw(at, b, *, name, split_b=False):
    K, M = at.shape
    if split_b:
        Nh = b.shape[2]
        N = 2 * Nh
    else:
        N = b.shape[1]
        Nh = N
    tm = _pick(M, (1024, 512, 256, 128))
    tko = _pick(K, (1024, 1408, 704, 512, 256, 128))
    tn = _pick(Nh, (1408, 1024, 768, 512, 256, 128))
    nm = M // tm
    nh = Nh // tn

    def body(a_ref, b_ref, o_ref):
        part = _dot(a_ref[...], b_ref[...], NN)
        r = pl.program_id(2)

        @pl.when(r == 0)
        def _():
            o_ref[...] = part

        @pl.when(r > 0)
        def _():
            o_ref[...] += part

    if split_b:
        b_spec = pl.BlockSpec((None, tm, tn), lambda ko, n, r: (n // nh, r, n % nh))
    else:
        b_spec = pl.BlockSpec((tm, tn), lambda ko, n, r: (r, n))
    return pl.pallas_call(
        body, name=name, grid=(K // tko, N // tn, nm),
        in_specs=[pl.BlockSpec((tko, tm), lambda ko, n, r: (ko, r)), b_spec],
        out_specs=pl.BlockSpec((None, tko, tn), lambda ko, n, r: (0, ko, n)),
        out_shape=jax.ShapeDtypeStruct((1, K, N), F32),
        compiler_params=_params(("parallel", "parallel", "arbitrary")),
    )(at, b)


def _rms(x, g):
    return x * lax.rsqrt(jnp.mean(x * x, axis=-1, keepdims=True) + EPS) * g


def rmsnorm_fwd(h, g, l, *, name):
    S, D = h.shape
    tm = _pick(S, (512, 256, 128))

    def body(h_ref, g_ref, o_ref, ot_ref):
        y = _rms(h_ref[...], g_ref[...])
        o_ref[...] = y.astype(BF16)
        ot_ref[...] = y.T.astype(BF16)

    return pl.pallas_call(
        body, name=name, grid=(S // tm,),
        in_specs=[pl.BlockSpec((tm, D), lambda m: (m, 0)),
                  pl.BlockSpec((None, 1, D), lambda m: (l, 0, 0))],
        out_specs=[pl.BlockSpec((tm, D), lambda m: (m, 0)), pl.BlockSpec((D, tm), lambda m: (0, m))],
        out_shape=[jax.ShapeDtypeStruct((S, D), BF16), jax.ShapeDtypeStruct((D, S), BF16)],
        compiler_params=_params(("parallel",)),
    )(h, g)


def rmsnorm_bwd(h, g, l, dhn, dres, *, name):
    S, D = h.shape
    tm = _pick(S, (512, 256, 128))

    def body(h_ref, g_ref, dhn_ref, dres_ref, dh_ref, dg_ref):
        _, vjp = jax.vjp(_rms, h_ref[...], g_ref[...])
        dx, dg = vjp(dhn_ref[...])
        dh_ref[...] = dres_ref[...] + dx

        @pl.when(pl.program_id(0) == 0)
        def _():
            dg_ref[...] = dg

        @pl.when(pl.program_id(0) > 0)
        def _():
            dg_ref[...] += dg

    row = pl.BlockSpec((tm, D), lambda m: (m, 0))
    return pl.pallas_call(
        body, name=name, grid=(S // tm,),
        in_specs=[row, pl.BlockSpec((None, 1, D), lambda m: (l, 0, 0)), row, row],
        out_specs=[row, pl.BlockSpec((1, D), lambda m: (0, 0))],
        out_shape=[jax.ShapeDtypeStruct((S, D), F32), jax.ShapeDtypeStruct((1, D), F32)],
        compiler_params=_params(("arbitrary",)),
    )(h, g, dhn, dres)


def loss_head(h, g, target, *, name):
    S, D = h.shape
    tm = _pick(S, (512, 256, 128))

    def body(h_ref, g_ref, t_ref, loss_ref, dh_ref, dg_ref):
        y, vjp = jax.vjp(_rms, h_ref[...], g_ref[...])
        err = y - t_ref[...]
        part = 0.5 * jnp.sum(jnp.sum(err * err, axis=-1, keepdims=True) / D, axis=0, keepdims=True)
        dx, dg = vjp(err / D)
        dh_ref[...] = dx

        @pl.when(pl.program_id(0) == 0)
        def _():
            dg_ref[...] = dg
            loss_ref[...] = part

        @pl.when(pl.program_id(0) > 0)
        def _():
            dg_ref[...] += dg
            loss_ref[...] += part

    row = pl.BlockSpec((tm, D), lambda m: (m, 0))
    return pl.pallas_call(
        body, name=name, grid=(S // tm,),
        in_specs=[row, pl.BlockSpec((1, D), lambda m: (0, 0)), row],
        out_specs=[pl.BlockSpec((1, 1), lambda m: (0, 0)), row, pl.BlockSpec((1, D), lambda m: (0, 0))],
        out_shape=[jax.ShapeDtypeStruct((1, 1), F32), jax.ShapeDtypeStruct((S, D), F32),
                   jax.ShapeDtypeStruct((1, D), F32)],
        compiler_params=_params(("arbitrary",)),
    )(h, g, target)


def _make_mm(dims_fwd, dims_da, dims_db, swap_da=False, swap_db=False):
    @jax.custom_vjp
    def mm(a, b):
        return _dot(a, b, dims_fwd)

    def fwd(a, b):
        return mm(a, b), (a, b)

    def bwd(resid, g):
        a, b = resid
        da = _dot(b, g, dims_da) if swap_da else _dot(g, b, dims_da)
        db = _dot(g, a, dims_db) if swap_db else _dot(a, g, dims_db)
        return da, db

    mm.defvjp(fwd, bwd)
    return mm


_mm = _make_mm(NN, NT, TN)
_mm_nt = _make_mm(NT, NN, TN, swap_db=True)
_mm_tn = _make_mm(TN, NT, NN, swap_da=True)


def _dot_hi(a, b):
    return jnp.dot(a, b, precision=lax.Precision.HIGHEST, preferred_element_type=F32)


def _sigmoid(x):
    return 0.5 * jnp.tanh(0.5 * x) + 0.5


def _silu(x):
    return x * _sigmoid(x)


def _attn_block(q, kw, vw, sink, slope, first):
    rows = q.shape[0]
    s = _mm_nt(q, kw) * (HEAD_DIM ** -0.5)
    qi = lax.broadcasted_iota(jnp.int32, (rows, 2 * WINDOW), 0) % WINDOW
    ki = lax.broadcasted_iota(jnp.int32, (rows, 2 * WINDOW), 1)
    dist = qi + WINDOW - ki
    valid = (dist >= 0) & (dist < WINDOW) & (ki >= WINDOW * first)
    s = jnp.where(valid, s - slope * dist.astype(F32), -jnp.inf)
    m = lax.stop_gradient(jnp.maximum(jnp.max(s, axis=-1, keepdims=True), sink))
    e = jnp.exp(s - m)
    denom = jnp.sum(e, axis=-1, keepdims=True) + jnp.exp(sink - m)
    return _mm(e * (1.0 / denom), vw)


def _attn_specs(Hkv, order):
    G = Q_PER_KV
    qspec = pl.BlockSpec((Hkv, G, WINDOW, HEAD_DIM), lambda n: (0, 0, order(n), 0))
    prev = pl.BlockSpec((Hkv, WINDOW, HEAD_DIM), lambda n: (0, jnp.maximum(order(n) - 1, 0), 0))
    cur = pl.BlockSpec((Hkv, WINDOW, HEAD_DIM), lambda n: (0, order(n), 0))
    col = pl.BlockSpec((Hkv, G * WINDOW, 1), lambda n: (0, 0, 0))
    return qspec, prev, cur, col


def attn_fwd(q, k, v, sink_col, slope_col, *, name, comm=None):
    Hkv, G, S, _ = q.shape
    qspec, prev, cur, col = _attn_specs(Hkv, lambda n: n)

    def body(q_ref, kp_ref, kc_ref, vp_ref, vc_ref, sink_ref, slope_ref, o_ref):
        first = (pl.program_id(0) == 0).astype(jnp.int32)
        outs = []
        for h in range(Hkv):
            kw = jnp.concatenate([kp_ref[h], kc_ref[h]], axis=0)
            vw = jnp.concatenate([vp_ref[h], vc_ref[h]], axis=0)
            o = _attn_block(q_ref[h].reshape(G * WINDOW, HEAD_DIM), kw, vw, sink_ref[h], slope_ref[h], first)
            outs.append(o.reshape(G, WINDOW, HEAD_DIM).astype(BF16))
        o_ref[...] = jnp.stack(outs)

    outs, updated, created = _carried_call(
        body, [q, k, k, v, v, sink_col, slope_col], name=name, grid=(S // WINDOW,),
        in_specs=[qspec, prev, cur, prev, cur, col, col], out_specs=[qspec],
        out_shape=[jax.ShapeDtypeStruct(q.shape, BF16)], comm=comm)
    return outs[0] if comm is None else (outs[0], updated, created)


def attn_bwd(q, k, v, sink_col, slope_col, do, *, name, comm=None):
    Hkv, G, S, _ = q.shape
    nb = S // WINDOW
    qspec, prev, cur, col = _attn_specs(Hkv, lambda n: nb - 1 - n)

    def body(q_ref, kp_ref, kc_ref, vp_ref, vc_ref, sink_ref, slope_ref, do_ref,
             dq_ref, dk_ref, dv_ref, dsink_ref, dk_carry, dv_carry):
        @pl.when(pl.program_id(0) == 0)
        def _():
            dk_carry[...] = jnp.zeros_like(dk_carry)
            dv_carry[...] = jnp.zeros_like(dv_carry)
            dsink_ref[...] = jnp.zeros_like(dsink_ref)

        first = (pl.program_id(0) == nb - 1).astype(jnp.int32)
        dqs, dkws, dvws, dsinks = [], [], [], []
        for h in range(Hkv):
            kw = jnp.concatenate([kp_ref[h], kc_ref[h]], axis=0)
            vw = jnp.concatenate([vp_ref[h], vc_ref[h]], axis=0)
            fn = functools.partial(_attn_block, slope=slope_ref[h], first=first)
            _, vjp = jax.vjp(fn, q_ref[h].reshape(G * WINDOW, HEAD_DIM), kw, vw, sink_ref[h])
            dq, dkw, dvw, dsink = vjp(do_ref[h].reshape(G * WINDOW, HEAD_DIM).astype(F32))
            dqs.append(dq.reshape(G, WINDOW, HEAD_DIM))
            dkws.append(dkw)
            dvws.append(dvw)
            dsinks.append(dsink)
        dkw, dvw = jnp.stack(dkws), jnp.stack(dvws)
        dq_ref[...] = jnp.stack(dqs)
        dk_ref[...] = dkw[:, WINDOW:] + dk_carry[...]
        dv_ref[...] = dvw[:, WINDOW:] + dv_carry[...]
        dk_carry[...] = dkw[:, :WINDOW]
        dv_carry[...] = dvw[:, :WINDOW]
        dsink_ref[...] += jnp.stack(dsinks)

    kv_shape = jax.ShapeDtypeStruct(k.shape, F32)
    outs, _, created = _carried_call(
        body, [q, k, k, v, v, sink_col, slope_col, do], name=name, grid=(nb,),
        in_specs=[qspec, prev, cur, prev, cur, col, col, qspec],
        out_specs=[qspec, cur, cur, col],
        out_shape=[jax.ShapeDtypeStruct(q.shape, F32), kv_shape, kv_shape,
                   jax.ShapeDtypeStruct(sink_col.shape, F32)],
        scratch_shapes=[pltpu.VMEM((Hkv, WINDOW, HEAD_DIM), F32), pltpu.VMEM((Hkv, WINDOW, HEAD_DIM), F32)],
        comm=comm)
    return outs if comm is None else (*outs, created)


def _hgrn_consts():
    C = HG_CHUNK
    t = np.arange(C)[:, None]
    j = np.arange(C)[None, :]
    cq, ck, mk = [], [], []
    H = C // 2
    while H >= 1:
        start = (t // H) * H
        cq.append((j > start) & (j <= t))
        ck.append((j > t) & (j <= start + H))
        mk.append(((t // H) % 2 == 1) & ((j // H) == (t // H) - 1))
        H //= 2
    ltri = (j <= t)
    ops = np.concatenate([ltri] + cq + ck, axis=0).astype(np.float32)
    ops3 = np.concatenate([ops, ops, ops], axis=1)
    opst3 = np.concatenate([ops.T, ops.T, ops.T], axis=1)
    return (jnp.asarray(ops3, BF16), jnp.asarray(opst3, BF16), jnp.asarray(np.stack(mk).astype(np.float32)))


def _split3(x):
    hi = x.astype(BF16)
    r = x - hi.astype(F32)
    mid = r.astype(BF16)
    lo = (r - mid.astype(F32)).astype(BF16)
    return jnp.concatenate([hi, mid, lo], axis=0)


@jax.custom_vjp
def _seg_sums(g, ops3, opst3):
    C = g.shape[0]
    out = jnp.dot(ops3, _split3(g), preferred_element_type=F32)
    return tuple(out[i * C:(i + 1) * C] for i in range(ops3.shape[0] // C))


def _seg_sums_fwd(g, ops3, opst3):
    return _seg_sums(g, ops3, opst3), opst3


def _seg_sums_bwd(opst3, cts):
    ct = jnp.concatenate(cts, axis=0)
    return jnp.dot(opst3, _split3(ct), preferred_element_type=F32), None, None


_seg_sums.defvjp(_seg_sums_fwd, _seg_sums_bwd)


def _hgrn_head(qr, fr, iv, gr, st, logits, ng, ops3, opst3, mk, layer):
    n_levels = mk.shape[0]
    depth = logits.shape[0]
    e = jnp.exp(logits - lax.stop_gradient(jnp.max(logits, axis=0, keepdims=True)))
    sm = e / jnp.sum(e, axis=0, keepdims=True)
    row = lax.broadcasted_iota(jnp.int32, (depth, HG_EXPAND), 0)
    lb = jnp.sum(jnp.where((row >= 1) & (row <= layer), sm, 0.0), axis=0, keepdims=True)

    q = _silu(qr)
    a = jnp.log(lb)
    c = jnp.log(1.0 - lb) - (jnp.maximum(-fr, 0.0) + jnp.log(1.0 + jnp.exp(-jnp.abs(fr))))
    mx = lax.stop_gradient(jnp.maximum(a, c))
    g = mx + jnp.log(jnp.exp(a - mx) + jnp.exp(c - mx))
    k = (1.0 - lb) * _sigmoid(-fr)

    sums = _seg_sums(g, ops3, opst3)
    b = sums[0]
    b_last = jnp.sum(g, axis=0, keepdims=True)

    o = jnp.sum(q * k, axis=-1, keepdims=True) * iv
    amat = None
    for lv in range(n_levels):
        eq = jnp.exp(sums[1 + lv])
        ek = jnp.exp(sums[1 + n_levels + lv])
        part = _mm_nt(q * eq, k * ek) * mk[lv]
        amat = part if amat is None else amat + part
    o = o + _mm(amat, iv) + _mm_nt(q * jnp.exp(b), st)
    st_new = st * jnp.exp(b_last) + _mm_tn(iv, k * jnp.exp(b_last - b))

    on = o * lax.rsqrt(jnp.mean(o * o, axis=-1, keepdims=True) + EPS) * ng
    return on * _silu(gr), st_new


def _hgrn_in_specs(D, depth, order):
    C = HG_CHUNK
    nl = int(math.log2(C))
    n_ops = 1 + 2 * nl
    return [pl.BlockSpec((C, 4 * D), lambda c: (order(c), 0)),
            pl.BlockSpec((depth, D), lambda c: (0, 0)),
            pl.BlockSpec((1, HG_EXPAND), lambda c: (0, 0)),
            pl.BlockSpec((n_ops * C, 3 * C), lambda c: (0, 0)),
            pl.BlockSpec((C, 3 * n_ops * C), lambda c: (0, 0)),
            pl.BlockSpec((nl, C, C), lambda c: (0, 0, 0))]


def _hgrn_cols(D, h):
    return [slice(j * D + h * HG_EXPAND, j * D + (h + 1) * HG_EXPAND) for j in range(4)]


def hgrn_fwd(proj, logits, ng, layer, *, name, comm=None):
    S = proj.shape[0]
    D = proj.shape[1] // 4
    H = D // HG_EXPAND
    C = HG_CHUNK
    nc = S // C
    consts = _hgrn_consts()

    def body(p_ref, lg_ref, ng_ref, ops3_ref, opst3_ref, mk_ref, o_ref, st_out_ref, st_ref):
        @pl.when(pl.program_id(0) == 0)
        def _():
            st_ref[...] = jnp.zeros_like(st_ref)

        st_out_ref[...] = st_ref[...]
        ops3, opst3, mk = ops3_ref[...], opst3_ref[...], mk_ref[...]
        for h in range(H):
            sl = slice(h * HG_EXPAND, (h + 1) * HG_EXPAND)
            cols = _hgrn_cols(D, h)
            out, st_new = _hgrn_head(p_ref[:, cols[0]], p_ref[:, cols[1]], p_ref[:, cols[2]], p_ref[:, cols[3]],
                                     st_ref[h], lg_ref[:, sl], ng_ref[...], ops3, opst3, mk, layer)
            o_ref[:, sl] = out.astype(BF16)
            st_ref[h] = st_new

    outs, updated, created = _carried_call(
        body, [proj, logits, ng, *consts], name=name, grid=(nc,),
        in_specs=_hgrn_in_specs(D, logits.shape[0], lambda c: c),
        out_specs=[pl.BlockSpec((C, D), lambda c: (c, 0)),
                   pl.BlockSpec((None, H, HG_EXPAND, HG_EXPAND), lambda c: (c, 0, 0, 0))],
        out_shape=[jax.ShapeDtypeStruct((S, D), BF16),
                   jax.ShapeDtypeStruct((nc, H, HG_EXPAND, HG_EXPAND), F32)],
        scratch_shapes=[pltpu.VMEM((H, HG_EXPAND, HG_EXPAND), F32)], comm=comm)
    return outs if comm is None else (*outs, updated, created)


def hgrn_bwd(proj, logits, ng, states, dout, layer, *, name, comm=None):
    S = proj.shape[0]
    D = proj.shape[1] // 4
    H = D // HG_EXPAND
    C = HG_CHUNK
    nc = S // C
    depth = logits.shape[0]
    consts = _hgrn_consts()
    rev = lambda c: nc - 1 - c

    def body(p_ref, lg_ref, ng_ref, ops3_ref, opst3_ref, mk_ref, st_in_ref, do_ref,
             dp_ref, dlg_ref, dng_ref, dst_ref):
        @pl.when(pl.program_id(0) == 0)
        def _():
            dst_ref[...] = jnp.zeros_like(dst_ref)
            dlg_ref[...] = jnp.zeros_like(dlg_ref)
            dng_ref[...] = jnp.zeros_like(dng_ref)

        ops3, opst3, mk = ops3_ref[...], opst3_ref[...], mk_ref[...]
        for h in range(H):
            sl = slice(h * HG_EXPAND, (h + 1) * HG_EXPAND)
            cols = _hgrn_cols(D, h)
            fn = functools.partial(_hgrn_head, ops3=ops3, opst3=opst3, mk=mk, layer=layer)
            _, vjp = jax.vjp(fn, p_ref[:, cols[0]], p_ref[:, cols[1]], p_ref[:, cols[2]], p_ref[:, cols[3]],
                             st_in_ref[h], lg_ref[:, sl], ng_ref[...])
            dq, df, di, dg, dst, dlg, dng = vjp((do_ref[:, sl], dst_ref[h]))
            dp_ref[:, cols[0]] = dq
            dp_ref[:, cols[1]] = df
            dp_ref[:, cols[2]] = di
            dp_ref[:, cols[3]] = dg
            dst_ref[h] = dst
            dlg_ref[:, sl] += dlg
            dng_ref[...] += dng

    in_specs = _hgrn_in_specs(D, depth, rev)
    in_specs += [pl.BlockSpec((None, H, HG_EXPAND, HG_EXPAND), lambda c: (rev(c), 0, 0, 0)),
                 pl.BlockSpec((C, D), lambda c: (rev(c), 0))]
    outs, _, created = _carried_call(
        body, [proj, logits, ng, *consts, states, dout], name=name, grid=(nc,), in_specs=in_specs,
        out_specs=[pl.BlockSpec((C, 4 * D), lambda c: (rev(c), 0)),
                   pl.BlockSpec((depth, D), lambda c: (0, 0)), pl.BlockSpec((1, HG_EXPAND), lambda c: (0, 0))],
        out_shape=[jax.ShapeDtypeStruct((S, 4 * D), F32), jax.ShapeDtypeStruct((depth, D), F32),
                   jax.ShapeDtypeStruct((1, HG_EXPAND), F32)],
        scratch_shapes=[pltpu.VMEM((H, HG_EXPAND, HG_EXPAND), F32)], comm=comm)
    return outs if comm is None else (*outs, created)


def _shift_down(u, prev, shift):
    rolled = pltpu.roll(u, shift, axis=0)
    rows = lax.broadcasted_iota(jnp.int32, prev.shape, 0)
    top = jnp.where(rows < shift, pltpu.roll(prev, shift, axis=0), rolled[:SUBLANES])
    return jnp.concatenate([top, rolled[SUBLANES:]], axis=0)


def _shift_up(x, nxt, shift):
    tm = x.shape[0]
    rolled = pltpu.roll(x, tm - shift, axis=0)
    rows = lax.broadcasted_iota(jnp.int32, nxt.shape, 0)
    bottom = jnp.where(rows >= SUBLANES - shift, pltpu.roll(nxt, SUBLANES - shift, axis=0), rolled[tm - SUBLANES:])
    return jnp.concatenate([rolled[:tm - SUBLANES], bottom], axis=0)


def _conv(u, prev, w_ref, b_ref, half):
    u1 = _shift_down(u, prev, 1)
    u2 = _shift_down(u, prev, 2)
    c = b_ref[half] + w_ref[half, 0:1, :] * u2 + w_ref[half, 1:2, :] * u1 + w_ref[half, 2:3, :] * u
    return c, u1, u2


def _conv_specs(tm, tc, l):
    nprev = lambda m: jnp.maximum(m * (tm // SUBLANES) - 1, 0)
    tile = pl.BlockSpec((2, tm, tc), lambda j, m: (0, m, j))
    halo = pl.BlockSpec((2, SUBLANES, tc), lambda j, m: (0, nprev(m), j))
    wspec = pl.BlockSpec((None, 2, 3, tc), lambda j, m: (l, 0, 0, j))
    bspec = pl.BlockSpec((None, 2, 1, tc), lambda j, m: (l, 0, 0, j))
    return tile, halo, wspec, bspec


def convact_fwd(u, cw, cb, l, *, name):
    _, S, Fh = u.shape
    tm = _pick(S, (256, 128))
    tc = _pick(Fh, (1408, 512, 256, 128))
    tile, halo, wspec, bspec = _conv_specs(tm, tc, l)

    def body(u_ref, p_ref, w_ref, b_ref, o_ref, ot_ref):
        keep = (pl.program_id(1) > 0).astype(F32)
        cg, _, _ = _conv(u_ref[0], p_ref[0] * keep, w_ref, b_ref, 0)
        cv, _, _ = _conv(u_ref[1], p_ref[1] * keep, w_ref, b_ref, 1)
        act = _silu(cg) * cv
        o_ref[...] = act.astype(BF16)
        ot_ref[...] = act.T.astype(BF16)

    return pl.pallas_call(
        body, name=name, grid=(Fh // tc, S // tm), in_specs=[tile, halo, wspec, bspec],
        out_specs=[pl.BlockSpec((tm, tc), lambda j, m: (m, j)), pl.BlockSpec((tc, tm), lambda j, m: (j, m))],
        out_shape=[jax.ShapeDtypeStruct((S, Fh), BF16), jax.ShapeDtypeStruct((Fh, S), BF16)],
        compiler_params=_params(("parallel", "parallel")),
    )(u, u, cw, cb)


def convact_bwd(u, cw, cb, l, da, *, name):
    _, S, Fh = u.shape
    tm = _pick(S, (256, 128))
    tc = _pick(Fh, (1408, 512, 256, 128))
    tile, halo, wspec, bspec = _conv_specs(tm, tc, l)

    def body(u_ref, p_ref, w_ref, b_ref, da_ref, dc_ref, dw_ref, db_ref):
        first = pl.program_id(1) == 0
        keep = jnp.logical_not(first).astype(F32)
        ug, uv = u_ref[0], u_ref[1]
        cg, g1, g2 = _conv(ug, p_ref[0] * keep, w_ref, b_ref, 0)
        cv, v1, v2 = _conv(uv, p_ref[1] * keep, w_ref, b_ref, 1)
        da_t = da_ref[...]
        sg = _sigmoid(cg)
        dcv = da_t * (cg * sg)
        dcg = da_t * cv * (sg * (1.0 + cg * (1.0 - sg)))
        dc_ref[0] = dcg
        dc_ref[1] = dcv

        @pl.when(first)
        def _():
            dw_ref[...] = jnp.zeros_like(dw_ref)
            db_ref[...] = jnp.zeros_like(db_ref)

        rs = lambda t: jnp.sum(t, axis=0, keepdims=True)
        for half, dc, taps in ((0, dcg, (g2, g1, ug)), (1, dcv, (v2, v1, uv))):
            for j in range(3):
                dw_ref[half, j:j + 1, :] += rs(dc * taps[j])
            db_ref[half] += rs(dc)

    return pl.pallas_call(
        body, name=name, grid=(Fh // tc, S // tm),
        in_specs=[tile, halo, wspec, bspec, pl.BlockSpec((tm, tc), lambda j, m: (m, j))],
        out_specs=[tile, pl.BlockSpec((2, 3, tc), lambda j, m: (0, 0, j)),
                   pl.BlockSpec((2, 1, tc), lambda j, m: (0, 0, j))],
        out_shape=[jax.ShapeDtypeStruct(u.shape, F32), jax.ShapeDtypeStruct((2, 3, Fh), F32),
                   jax.ShapeDtypeStruct((2, 1, Fh), F32)],
        compiler_params=_params(("parallel", "arbitrary")),
    )(u, u, cw, cb, da)


def conv_input_grad(dc, cw, l, *, name):
    _, S, Fh = dc.shape
    tm = _pick(S, (256, 128))
    tc = _pick(Fh, (1408, 512, 256, 128))
    nblk = S // SUBLANES
    nm = S // tm

    def body(dc_ref, n_ref, w_ref, du_ref):
        keep = (pl.program_id(1) < nm - 1).astype(F32)
        for half in range(2):
            t = dc_ref[half]
            nxt = n_ref[half] * keep
            du = (w_ref[half, 2:3, :] * t + w_ref[half, 1:2, :] * _shift_up(t, nxt, 1)
                  + w_ref[half, 0:1, :] * _shift_up(t, nxt, 2))
            du_ref[half] = du.astype(BF16)

    tile = pl.BlockSpec((2, tm, tc), lambda j, m: (0, m, j))
    halo = pl.BlockSpec((2, SUBLANES, tc),
                        lambda j, m: (0, jnp.minimum((m + 1) * (tm // SUBLANES), nblk - 1), j))
    return pl.pallas_call(
        body, name=name, grid=(Fh // tc, nm),
        in_specs=[tile, halo, pl.BlockSpec((None, 2, 3, tc), lambda j, m: (l, 0, 0, j))],
        out_specs=tile, out_shape=jax.ShapeDtypeStruct(dc.shape, BF16),
        compiler_params=_params(("parallel", "parallel")),
    )(dc, dc, cw)


def _adamw_math(w, g, m, v):
    m = ADAM_B1 * m + (1.0 - ADAM_B1) * g
    v = ADAM_B2 * v + (1.0 - ADAM_B2) * (g * g)
    m_hat = m / (1.0 - ADAM_B1 ** ADAM_STEP)
    v_hat = v / (1.0 - ADAM_B2 ** ADAM_STEP)
    delta = -ADAM_LR * (m_hat / (jnp.sqrt(v_hat) + ADAM_EPS) + ADAM_WD * w)
    return delta, m, v


def _row_tile(R, Ccols, n_streams):
    for tr in (512, 256, 128, 64, 32, 16, 8):
        if R % tr == 0 and tr * Ccols * 4 * n_streams * 2 <= VMEM_LIMIT_BYTES // 2:
            return tr
    return R


def adamw(w, g, m, v, *, name):
    R, Ccols = w.shape
    tr = _row_tile(R, Ccols, 7)

    def body(w_ref, g_ref, m_ref, v_ref, d_ref, mo_ref, vo_ref):
        d, mn, vn = _adamw_math(w_ref[...], g_ref[...], m_ref[...], v_ref[...])
        d_ref[...] = d
        mo_ref[...] = mn
        vo_ref[...] = vn

    spec = pl.BlockSpec((tr, Ccols), lambda i: (i, 0))
    shp = jax.ShapeDtypeStruct((R, Ccols), F32)
    return pl.pallas_call(
        body, name=name, grid=(R // tr,), in_specs=[spec] * 4, out_specs=[spec] * 3,
        out_shape=[shp, shp, shp], compiler_params=_params(("parallel",)),
    )(w, g, m, v)


def _prefetch_call(body, pos, args, *, name, grid, in_specs, out_specs, out_shape, sem, aliases=None):
    return pl.pallas_call(
        lambda pos_ref, *refs: body(*refs), name=name,
        grid_spec=pltpu.PrefetchScalarGridSpec(num_scalar_prefetch=1, grid=grid, in_specs=in_specs,
                                               out_specs=out_specs),
        out_shape=out_shape, input_output_aliases={1 + i: o for i, o in (aliases or {}).items()},
        compiler_params=_params(sem),
    )(pos, *args)


def cast_place(w, pos, *, row_sharded, name):
    L, R, Ccols = w.shape

    def body(w_ref, o_ref):
        o_ref[...] = w_ref[...].astype(BF16)

    if row_sharded:
        grid = (L,)
        in_spec = pl.BlockSpec((None, R, Ccols), lambda l, pos: (l, 0, 0))
        out_spec = pl.BlockSpec((None, R, Ccols), lambda l, pos: (l, pos[0], 0))
        out_shape = jax.ShapeDtypeStruct((L, N_CHIPS * R, Ccols), BF16)
        sem = ("parallel",)
    else:
        tr = _pick(R, (512, 256, 128))
        grid = (L, R // tr)
        in_spec = pl.BlockSpec((None, tr, Ccols), lambda l, i, pos: (l, i, 0))
        out_spec = pl.BlockSpec((None, tr, Ccols), lambda l, i, pos: (l, i, pos[0]))
        out_shape = jax.ShapeDtypeStruct((L, R, N_CHIPS * Ccols), BF16)
        sem = ("parallel", "parallel")
    return _prefetch_call(body, pos, [w], name=name, grid=grid, in_specs=[in_spec], out_specs=out_spec,
                          out_shape=out_shape, sem=sem)


def pair_add(g, theirs, pos, *, row_sharded, name):
    def body(g_ref, t_ref, o_ref, ob_ref):
        s = g_ref[...] + t_ref[...]
        o_ref[...] = s
        ob_ref[...] = s.astype(BF16)

    if row_sharded:
        L, _, _, r2, D = g.shape
        grid = (L, N_CHIPS)
        g_spec = pl.BlockSpec((None, None, None, r2, D), lambda l, q, pos: (l, q, pos[1], 0, 0))
        t_spec = pl.BlockSpec((None, None, r2, D), lambda l, q, pos: (l, q, 0, 0))
    else:
        L, K, N = g.shape
        n = N // N_CHIPS
        tr = _pick(K // 2, (256, 128))
        nb = (K // 2) // tr
        grid = (L, nb, N_CHIPS)
        g_spec = pl.BlockSpec((None, tr, n), lambda l, i, j, pos: (l, pos[1] * nb + i, j))
        t_spec = pl.BlockSpec((None, tr, n), lambda l, i, j, pos: (l, i, j))
    return _prefetch_call(body, pos, [g, theirs], name=name, grid=grid, in_specs=[g_spec, t_spec],
                          out_specs=[t_spec, t_spec],
                          out_shape=[jax.ShapeDtypeStruct(theirs.shape, F32), jax.ShapeDtypeStruct(theirs.shape, BF16)],
                          sem=("parallel",) * len(grid))


def chip_add(pair, landed, pos, buf, li, n_layers, *, row_sharded, name):
    def body(p_ref, a_ref, b_ref, c_ref, *rest):
        rest[-1][...] = ((p_ref[...] + a_ref[...].astype(F32)) + b_ref[...].astype(F32)) + c_ref[...].astype(F32)

    if row_sharded:
        _, _, r2, D = pair.shape
        grid = (1,)
        p_spec = pl.BlockSpec((None, None, r2, D), lambda i, pos: (0, pos[0], 0, 0))
        l_specs = [pl.BlockSpec((None, r2, D), functools.partial(lambda i, pos, k: (k, 0, 0), k=k)) for k in range(3)]
        out_spec = pl.BlockSpec((None, None, r2, D), lambda i, pos: (li, pos[1], 0, 0))
        out_shape = jax.ShapeDtypeStruct((n_layers, 2, r2, D), F32)
    else:
        _, K2, N = pair.shape
        n = N // N_CHIPS
        tr = _pick(K2, (256, 128))
        nb = K2 // tr
        grid = (nb,)
        p_spec = pl.BlockSpec((None, tr, n), lambda i, pos: (0, i, pos[0]))
        l_specs = [pl.BlockSpec((None, tr, n), functools.partial(lambda i, pos, k: (k, i, 0), k=k)) for k in range(3)]
        out_spec = pl.BlockSpec((None, tr, n), lambda i, pos: (li, pos[1] * nb + i, 0))
        out_shape = jax.ShapeDtypeStruct((n_layers, 2 * K2, n), F32)
    args, in_specs, aliases = [pair, landed, landed, landed], [p_spec] + l_specs, None
    if buf is not None:
        args.append(buf)
        in_specs.append(pl.BlockSpec(memory_space=pl.ANY))
        aliases = {4: 0}
    return _prefetch_call(body, pos, args, name=name, grid=grid, in_specs=in_specs, out_specs=out_spec,
                          out_shape=out_shape, sem=("parallel",), aliases=aliases)


def _position():
    return lax.axis_index("x"), lax.axis_index("y"), lax.axis_index("c")


def _other_chips(x, y):
    return [(1 - x, y), (x, 1 - y), (1 - x, 1 - y)]


HBM_ANY = pl.BlockSpec(memory_space=pl.ANY)
COMM_PARAMS = pltpu.CompilerParams(has_side_effects=True)


class Comm:
    def __init__(self, ro=(), inout=(), new=(), sems=(), phases=()):
        self.ro, self.inout, self.new, self.sems, self.phases = list(ro), list(inout), list(new), list(sems), list(phases)


def _carried_call(body, args, *, name, grid, in_specs, out_specs, out_shape, scratch_shapes=(), comm=None):
    comm = comm or Comm()
    n_in, n_out, n_sc = len(in_specs), len(out_shape), len(scratch_shapes)
    n_ro, n_io, n_new, n_sem = len(comm.ro), len(comm.inout), len(comm.new), len(comm.sems)
    last = grid[0] - 1

    def wrapped(*refs):
        bounds = np.cumsum([0, n_in, n_ro, n_io, n_out, n_io, n_new, n_sc, n_sem])
        core_in, ro, _, core_out, io, new, core_sc, sems = (refs[a:b] for a, b in zip(bounds[:-1], bounds[1:]))
        step = pl.program_id(0)
        for frac, fn in comm.phases[:-1]:
            pl.when(step == int(frac * last))(functools.partial(fn, ro, io, new, sems))
        body(*core_in, *core_out, *core_sc)
        if comm.phases:
            pl.when(step == last)(functools.partial(comm.phases[-1][1], ro, io, new, sems))

    outs = pl.pallas_call(
        wrapped, name=name, grid=grid,
        in_specs=list(in_specs) + [HBM_ANY] * (n_ro + n_io),
        out_specs=list(out_specs) + [HBM_ANY] * (n_io + n_new),
        out_shape=list(out_shape) + [jax.ShapeDtypeStruct(a.shape, a.dtype) for a in comm.inout] + comm.new,
        input_output_aliases={n_in + n_ro + j: n_out + j for j in range(n_io)},
        scratch_shapes=list(scratch_shapes) + comm.sems,
        compiler_params=pltpu.CompilerParams(dimension_semantics=("arbitrary",), vmem_limit_bytes=VMEM_LIMIT_BYTES,
                                             has_side_effects=bool(comm.phases)),
    )(*args, *comm.ro, *comm.inout)
    return outs[:n_out], outs[n_out:n_out + n_io], outs[n_out + n_io:]


def run_comm(comm, *, name):
    n_ro, n_io, n_new = len(comm.ro), len(comm.inout), len(comm.new)

    def body(*refs):
        bounds = np.cumsum([0, n_ro, n_io, n_io, n_new, len(comm.sems)])
        ro, _, io, new, sems = (refs[a:b] for a, b in zip(bounds[:-1], bounds[1:]))
        for _, fn in comm.phases:
            fn(ro, io, new, sems)

    outs = pl.pallas_call(
        body, name=name, in_specs=[HBM_ANY] * (n_ro + n_io), out_specs=[HBM_ANY] * (n_io + n_new),
        out_shape=[jax.ShapeDtypeStruct(a.shape, a.dtype) for a in comm.inout] + comm.new,
        input_output_aliases={n_ro + j: j for j in range(n_io)},
        scratch_shapes=comm.sems, compiler_params=COMM_PARAMS,
    )(*comm.ro, *comm.inout)
    return outs[:n_io], outs[n_io:]


def _rdma(src, dst, send_sem, recv_sem, device):
    return pltpu.make_async_remote_copy(src_ref=src, dst_ref=dst, send_sem=send_sem, recv_sem=recv_sem,
                                        device_id=device, device_id_type=MESH)


def gather_comm(bufs, slices):
    n = len(slices)

    def region(io, s, h, chip):
        b, li, row_sharded = slices[s]
        ref = io[b]
        if row_sharded:
            r = ref.shape[1] // N_CHIPS
            return ref.at[li, pl.ds(chip * r + h * (r // 2), r // 2), :]
        K, nn = ref.shape[1], ref.shape[2] // N_CHIPS
        return ref.at[li, pl.ds(h * (K // 2), K // 2), pl.ds(chip * nn, nn)]

    def copies(io, sems):
        x, y, c = _position()
        me = 2 * x + y
        chips = _other_chips(x, y)
        ids = [2 * cx + cy for cx, cy in chips]

        def ici(s, k, src):
            return _rdma(region(io, s, c, me), region(io, s, c, src), sems[0].at[s, k], sems[1].at[s, k], (*chips[k], c))

        def d2d(s, k, h):
            return _rdma(region(io, s, h, ids[k]), region(io, s, h, ids[k]), sems[2].at[s, k], sems[3].at[s, k],
                         (x, y, 1 - c))

        return me, ids, c, ici, d2d

    pairs = [(s, k) for s in range(n) for k in range(3)]

    def start(ro, io, new, sems):
        me, _, _, ici, _ = copies(io, sems)
        for s, k in pairs:
            ici(s, k, me).start()

    def pass_on(ro, io, new, sems):
        _, ids, c, ici, d2d = copies(io, sems)
        for s, k in pairs:
            ici(s, k, ids[k]).wait_recv()
            d2d(s, k, c).start()

    def finish(ro, io, new, sems):
        me, _, c, ici, d2d = copies(io, sems)
        for s, k in pairs:
            d2d(s, k, 1 - c).wait_recv()
        for s, k in pairs:
            ici(s, k, me).wait_send()
            d2d(s, k, c).wait_send()

    return Comm(inout=bufs, sems=[pltpu.SemaphoreType.DMA((n, 3))] * 4,
                phases=[(0.0, start), (0.6, pass_on), (1.0, finish)])


def gather_small(shard, *, name):
    n = shard.shape[2]

    def body(s_ref, o_ref, send_sem, recv_sem, local_sem):
        x, y, c = _position()
        me = 2 * x + y
        chips = _other_chips(x, y)
        place = lambda chip: o_ref.at[:, :, pl.ds(chip * n, n)]
        local = pltpu.make_async_copy(s_ref, place(me), local_sem)
        local.start()
        for k in range(3):
            _rdma(s_ref, place(me), send_sem.at[k], recv_sem.at[k], (*chips[k], c)).start()
        for k in range(3):
            _rdma(s_ref, place(2 * chips[k][0] + chips[k][1]), send_sem.at[k], recv_sem.at[k], (*chips[k], c)).wait()
        local.wait()

    return pl.pallas_call(
        body, name=name, in_specs=[HBM_ANY], out_specs=HBM_ANY,
        out_shape=jax.ShapeDtypeStruct((shard.shape[0], shard.shape[1], N_CHIPS * n), shard.dtype),
        scratch_shapes=[pltpu.SemaphoreType.DMA((3,)), pltpu.SemaphoreType.DMA((3,)), pltpu.SemaphoreType.DMA],
        compiler_params=COMM_PARAMS,
    )(shard)


def sibling_swap_halves(col_grads, row_grads, *, name):
    n_col, n_row = len(col_grads), len(row_grads)
    n_in = n_col + n_row

    def body(*refs):
        ins = refs[:n_in]
        outs = refs[n_in:2 * n_in]
        send_sem, recv_sem = refs[2 * n_in:]
        x, y, c = _position()
        sibling = (x, y, 1 - c)
        h = 1 - c
        cps = []
        for i in range(n_in):
            if i < n_col:
                K = ins[i].shape[1]
                cps.append(pltpu.make_async_remote_copy(
                    src_ref=ins[i].at[:, pl.ds(h * (K // 2), K // 2), :], dst_ref=outs[i],
                    send_sem=send_sem.at[i, 0], recv_sem=recv_sem.at[i, 0],
                    device_id=sibling, device_id_type=MESH))
            else:
                r = ins[i].shape[1] // N_CHIPS
                for q in range(N_CHIPS):
                    cps.append(pltpu.make_async_remote_copy(
                        src_ref=ins[i].at[:, pl.ds(q * r + h * (r // 2), r // 2), :], dst_ref=outs[i].at[:, q],
                        send_sem=send_sem.at[i, q], recv_sem=recv_sem.at[i, q],
                        device_id=sibling, device_id_type=MESH))
        for cp in cps:
            cp.start()
        for cp in cps:
            cp.wait()

    out_shape = []
    for a in col_grads:
        out_shape.append(jax.ShapeDtypeStruct((a.shape[0], a.shape[1] // 2, a.shape[2]), a.dtype))
    for a in row_grads:
        r = a.shape[1] // N_CHIPS
        out_shape.append(jax.ShapeDtypeStruct((a.shape[0], N_CHIPS, r // 2, a.shape[2]), a.dtype))
    return pl.pallas_call(
        body, name=name, in_specs=[HBM_ANY] * n_in, out_specs=[HBM_ANY] * n_in, out_shape=out_shape,
        scratch_shapes=[pltpu.SemaphoreType.DMA((n_in, N_CHIPS)), pltpu.SemaphoreType.DMA((n_in, N_CHIPS))],
        compiler_params=COMM_PARAMS,
    )(*col_grads, *row_grads)


def scatter_comm(halves):
    def copies(ro, new, sems):
        x, y, c = _position()
        chips = _other_chips(x, y)
        cps = []
        for s, (a, row_sharded) in enumerate(halves):
            for k in range(3):
                to = 2 * chips[k][0] + chips[k][1]
                if row_sharded:
                    src = ro[s].at[0, to]
                else:
                    n = a.shape[2] // N_CHIPS
                    src = ro[s].at[0, :, pl.ds(to * n, n)]
                cps.append(_rdma(src, new[s].at[k], sems[0].at[s, k], sems[1].at[s, k], (*chips[k], c)))
        return cps

    def start(ro, io, new, sems):
        for cp in copies(ro, new, sems):
            cp.start()

    def finish(ro, io, new, sems):
        for cp in copies(ro, new, sems):
            cp.wait()

    landing = [jax.ShapeDtypeStruct((3, a.shape[2], a.shape[3]) if row_sharded else
                                    (3, a.shape[1], a.shape[2] // N_CHIPS), a.dtype) for a, row_sharded in halves]
    return Comm(ro=[a for a, _ in halves], new=landing, sems=[pltpu.SemaphoreType.DMA((len(halves), 3))] * 2,
                phases=[(0.0, start), (1.0, finish)])


def sibling_join_halves(shards, *, name):
    n_in = len(shards)

    def body(*refs):
        ins = refs[:n_in]
        outs = refs[n_in:2 * n_in]
        send_sem, recv_sem = refs[2 * n_in:]
        x, y, c = _position()
        sibling = (x, y, 1 - c)

        def rows(ref, h):
            R2 = ref.shape[1] // 2
            return ref.at[:, pl.ds(h * R2, R2), :]

        sends = [pltpu.make_async_remote_copy(src_ref=rows(ins[i], c), dst_ref=rows(outs[i], c),
                                              send_sem=send_sem.at[i], recv_sem=recv_sem.at[i],
                                              device_id=sibling, device_id_type=MESH) for i in range(n_in)]
        lands = [pltpu.make_async_remote_copy(src_ref=rows(ins[i], c), dst_ref=rows(outs[i], 1 - c),
                                              send_sem=send_sem.at[i], recv_sem=recv_sem.at[i],
                                              device_id=sibling, device_id_type=MESH) for i in range(n_in)]
        for cp in sends:
            cp.start()
        for i in range(n_in):
            sends[i].wait_send()
            lands[i].wait_recv()

    return pl.pallas_call(
        body, name=name, in_specs=[HBM_ANY] * n_in, out_specs=[HBM_ANY] * n_in,
        out_shape=[jax.ShapeDtypeStruct(a.shape, a.dtype) for a in shards],
        input_output_aliases={i: i for i in range(n_in)},
        scratch_shapes=[pltpu.SemaphoreType.DMA((n_in,)), pltpu.SemaphoreType.DMA((n_in,))],
        compiler_params=COMM_PARAMS,
    )(*shards)


def all_reduce_small(v, *, name):
    R, Ccols = v.shape

    def body(v_ref, o_ref, slots, send_sem, recv_sem):
        x, y, c = _position()
        me = 4 * x + 2 * y + c
        slots[me] = v_ref[...]

        def peer(d):
            return x ^ (d >> 2), y ^ ((d >> 1) & 1), c ^ (d & 1)

        def copy(d, slot):
            return pltpu.make_async_remote_copy(
                src_ref=v_ref, dst_ref=slots.at[slot], send_sem=send_sem.at[d], recv_sem=recv_sem.at[d],
                device_id=peer(d), device_id_type=MESH)

        sends = [copy(d, me) for d in range(1, N_DEV)]
        for cp in sends:
            cp.start()
        for d in range(1, N_DEV):
            px, py, pc = peer(d)
            copy(d, 4 * px + 2 * py + pc).wait_recv()
        for cp in sends:
            cp.wait_send()
        acc = slots[0]
        for d in range(1, N_DEV):
            acc = acc + slots[d]
        o_ref[...] = acc

    return pl.pallas_call(
        body, name=name,
        in_specs=[pl.BlockSpec(memory_space=pltpu.VMEM)], out_specs=pl.BlockSpec(memory_space=pltpu.VMEM),
        out_shape=jax.ShapeDtypeStruct((R, Ccols), F32),
        scratch_shapes=[pltpu.VMEM((N_DEV, R, Ccols), F32), pltpu.SemaphoreType.DMA((N_DEV,)),
                        pltpu.SemaphoreType.DMA((N_DEV,))],
        compiler_params=COMM_PARAMS,
    )(v)


def _pack_small(parts):
    flat = jnp.concatenate([p.reshape(-1).astype(F32) for p in parts])
    n = flat.shape[0]
    rows = -(-n // LANES)
    rows = -(-rows // SUBLANES) * SUBLANES
    return jnp.pad(flat, (0, rows * LANES - n)).reshape(rows, LANES)


def _unpack_small(packed, like):
    flat = packed.reshape(-1)
    out, off = [], 0
    for p in like:
        out.append(flat[off:off + p.size].reshape(p.shape))
        off += p.size
    return out


def kernel(x, norm_mix, norm_ffn, norm_final, attn_w_in, attn_w_out, attn_sinks, hgrn_w_in, hgrn_w_out, hgrn_norm, hgrn_lb_logits, ffn_w_up, ffn_conv_w, ffn_conv_b, ffn_w_down, loss_target, m_norm_mix, m_norm_ffn, m_norm_final, m_attn_w_in, m_attn_w_out, m_attn_sinks, m_hgrn_w_in, m_hgrn_w_out, m_hgrn_norm, m_hgrn_lb_logits, m_ffn_w_up, m_ffn_conv_w, m_ffn_conv_b, m_ffn_w_down, v_norm_mix, v_norm_ffn, v_norm_final, v_attn_w_in, v_attn_w_out, v_attn_sinks, v_hgrn_w_in, v_hgrn_w_out, v_hgrn_norm, v_hgrn_lb_logits, v_ffn_w_up, v_ffn_conv_w, v_ffn_conv_b, v_ffn_w_down):
    S, D = x.shape[1], x.shape[2]
    depth = norm_mix.shape[0]
    n_q = D // HEAD_DIM
    n_kv = n_q // Q_PER_KV
    Fh = ffn_w_down.shape[1] * N_CHIPS
    h = x.reshape(S, D)
    target = loss_target.reshape(S, D)

    col_names = ["attn_w_in", "hgrn_w_in", "ffn_w_up"]
    row_names = ["attn_w_out", "hgrn_w_out", "ffn_w_down"]
    x_, y_, c_ = _position()
    me = 2 * x_ + y_
    pos = jnp.stack([me, c_]).astype(jnp.int32)
    col_w = [cast_place(w, pos, row_sharded=False, name=f"cast_place_{n}")
             for n, w in zip(col_names, (attn_w_in, hgrn_w_in, ffn_w_up))]
    row_w = [cast_place(w, pos, row_sharded=True, name=f"cast_place_{n}")
             for n, w in zip(row_names, (attn_w_out, hgrn_w_out, ffn_w_down))]
    W = dict(zip(col_names + row_names, col_w + row_w))
    is_row = {n: n in row_names for n in W}

    def layer_weights(layer):
        mixer = "attn" if layer % 2 == 0 else "hgrn"
        return [(f"{mixer}_w_in", layer // 2), (f"{mixer}_w_out", layer // 2), ("ffn_w_up", layer), ("ffn_w_down", layer)]

    def gather_layer(layer):
        names = [n for n, _ in layer_weights(layer)]
        slices = [(i, li, is_row[n]) for i, (n, li) in enumerate(layer_weights(layer))]
        return names, gather_comm([W[n] for n in names], slices)

    names_next, comm_next = gather_layer(0)
    filled, _ = run_comm(comm_next, name="gather_layer0")
    W.update(zip(names_next, filled))
    cw = gather_small(ffn_conv_w, name="gather_conv_w").reshape(depth, 3, 2, Fh).transpose(0, 2, 1, 3)
    cb = ffn_conv_b.reshape(depth, 2, 1, Fh)
    g_mix = norm_mix.reshape(depth, 1, D)
    g_ffn = norm_ffn.reshape(depth, 1, D)

    slopes = jnp.exp2(-8.0 * jnp.arange(1, n_q + 1, dtype=F32) / n_q).reshape(n_kv, Q_PER_KV)
    slope_col = jnp.repeat(slopes, WINDOW, axis=1).reshape(n_kv, Q_PER_KV * WINDOW, 1)

    def to_heads(t, nh):
        return t.reshape(S, nh, HEAD_DIM).transpose(1, 0, 2)

    def from_heads(t):
        return t.transpose(1, 0, 2).reshape(S, -1)

    saved = []
    for layer in range(depth):
        idx = layer // 2
        rec = {"h_mix": h}
        hn, rec["hn_t"] = rmsnorm_fwd(h, g_mix, layer, name=f"norm_mix_fwd{layer}")
        comm_next = None
        if layer + 1 < depth:
            names_next, comm_next = gather_layer(layer + 1)
        if layer % 2 == 0:
            proj = mm_nn(hn, W["attn_w_in"], idx, name=f"attn_in_fwd{layer}")
            q = to_heads(proj[:, :n_q * HEAD_DIM], n_q).reshape(n_kv, Q_PER_KV, S, HEAD_DIM)
            k = to_heads(proj[:, n_q * HEAD_DIM:(n_q + n_kv) * HEAD_DIM], n_kv)
            v = to_heads(proj[:, (n_q + n_kv) * HEAD_DIM:], n_kv)
            sink_col = jnp.repeat(attn_sinks[idx].reshape(n_kv, Q_PER_KV), WINDOW, axis=1)
            sink_col = sink_col.reshape(n_kv, Q_PER_KV * WINDOW, 1)
            o_heads = attn_fwd(q, k, v, sink_col, slope_col, name=f"attn_fwd{layer}", comm=comm_next)
            if comm_next is not None:
                o_heads, filled, _ = o_heads
                W.update(zip(names_next, filled))
            o = from_heads(o_heads.reshape(n_q, S, HEAD_DIM))
            rec.update(q=q, k=k, v=v, sink_col=sink_col, o=o)
            h = mm_nn(o, W["attn_w_out"], idx, name=f"attn_out_fwd{layer}", res=h)
        else:
            proj = mm_nn(hn, W["hgrn_w_in"], idx, name=f"hgrn_in_fwd{layer}")
            ng = hgrn_norm[idx].reshape(1, HG_EXPAND)
            res = hgrn_fwd(proj, hgrn_lb_logits, ng, layer, name=f"hgrn_fwd{layer}", comm=comm_next)
            o, states = res[:2]
            if comm_next is not None:
                W.update(zip(names_next, res[2]))
            rec.update(proj=proj, ng=ng, states=states, o=o)
            h = mm_nn(o, W["hgrn_w_out"], idx, name=f"hgrn_out_fwd{layer}", res=h)
        rec["h_ffn"] = h
        hn2, rec["hn2_t"] = rmsnorm_fwd(h, g_ffn, layer, name=f"norm_ffn_fwd{layer}")
        u = mm_nn(hn2, W["ffn_w_up"], layer, name=f"ffn_up_fwd{layer}", split_out=True)
        act, rec["act_t"] = convact_fwd(u, cw, cb, layer, name=f"ffn_act_fwd{layer}")
        rec["u"] = u
        h = mm_nn(act, W["ffn_w_down"], layer, name=f"ffn_down_fwd{layer}", res=h)
        saved.append(rec)

    loss_part, dh, d_norm_final = loss_head(h, norm_final.reshape(1, D), target, name="loss_head")

    n_attn, n_hgrn = attn_w_in.shape[0], hgrn_w_in.shape[0]
    reduced_parts = []
    pending = None
    d_norm_mix, d_norm_ffn = [None] * depth, [None] * depth
    d_conv_w, d_conv_b = [None] * depth, [None] * depth
    d_sinks, d_hgrn_norm = [None] * n_attn, [None] * n_hgrn
    d_logits = jnp.zeros_like(hgrn_lb_logits)
    for layer in reversed(range(depth)):
        idx = layer // 2
        rec = saved[layer]
        g = {"ffn_w_down": mm_dw(rec["act_t"], dh, name=f"ffn_down_dw{layer}")}
        carried = None if pending is None else pending[2]
        dact = mm_nt(dh, W["ffn_w_down"], layer, name=f"ffn_down_dx{layer}")
        dc, dcw, dcb = convact_bwd(rec["u"], cw, cb, layer, dact, name=f"ffn_act_bwd{layer}")
        d_conv_w[layer] = dcw.transpose(1, 0, 2).reshape(3, 2 * Fh)
        d_conv_b[layer] = dcb.reshape(2 * Fh)
        du = conv_input_grad(dc, cw, layer, name=f"ffn_conv_dx{layer}")
        g["ffn_w_up"] = mm_dw(rec["hn2_t"], du, name=f"ffn_up_dw{layer}", split_b=True)
        dhn2 = mm_nt(du, W["ffn_w_up"], layer, name=f"ffn_up_dx{layer}", split_in=True)
        dh, dg = rmsnorm_bwd(rec["h_ffn"], g_ffn, layer, dhn2, dh, name=f"norm_ffn_bwd{layer}")
        d_norm_ffn[layer] = dg.reshape(D)
        if layer % 2 == 0:
            g["attn_w_out"] = mm_tn(rec["o"], dh, name=f"attn_out_dw{layer}")
            do = mm_nt(dh, W["attn_w_out"], idx, name=f"attn_out_dx{layer}", out_dtype=BF16)
            do_heads = to_heads(do, n_q).reshape(n_kv, Q_PER_KV, S, HEAD_DIM)
            res = attn_bwd(rec["q"], rec["k"], rec["v"], rec["sink_col"], slope_col, do_heads,
                           name=f"attn_bwd{layer}", comm=carried)
            dq, dk, dv, dsink = res[:4]
            dproj = jnp.concatenate([from_heads(dq.reshape(n_q, S, HEAD_DIM)), from_heads(dk), from_heads(dv)], axis=1)
            d_sinks[idx] = jnp.sum(dsink.reshape(n_kv, Q_PER_KV, WINDOW), axis=-1).reshape(n_q)
            g["attn_w_in"] = mm_dw(rec["hn_t"], dproj, name=f"attn_in_dw{layer}")
            dhn = mm_nt(dproj, W["attn_w_in"], idx, name=f"attn_in_dx{layer}")
        else:
            g["hgrn_w_out"] = mm_tn(rec["o"], dh, name=f"hgrn_out_dw{layer}")
            do = mm_nt(dh, W["hgrn_w_out"], idx, name=f"hgrn_out_dx{layer}")
            res = hgrn_bwd(rec["proj"], hgrn_lb_logits, rec["ng"], rec["states"], do, layer,
                           name=f"hgrn_bwd{layer}", comm=carried)
            dproj, dlg, dng = res[:3]
            d_logits = d_logits + dlg
            d_hgrn_norm[idx] = dng.reshape(HG_EXPAND)
            g["hgrn_w_in"] = mm_dw(rec["hn_t"], dproj, name=f"hgrn_in_dw{layer}")
            dhn = mm_nt(dproj, W["hgrn_w_in"], idx, name=f"hgrn_in_dx{layer}")
        if pending is not None:
            reduced_parts.append((pending[0], pending[1], res[-1]))
        dh, dg = rmsnorm_bwd(rec["h_mix"], g_mix, layer, dhn, dh, name=f"norm_mix_bwd{layer}")
        d_norm_mix[layer] = dg.reshape(D)

        ordered = sorted(layer_weights(layer), key=lambda nl: is_row[nl[0]])
        theirs = sibling_swap_halves([g[n] for n, _ in ordered if not is_row[n]],
                                     [g[n] for n, _ in ordered if is_row[n]], name=f"rs_swap{layer}")
        pair, pair_bf16 = {}, {}
        for (n, _), t in zip(ordered, theirs):
            gn = g[n]
            if is_row[n]:
                r = gn.shape[1] // N_CHIPS
                gn = gn.reshape(1, N_CHIPS, 2, r // 2, gn.shape[2])
            pair[n], pair_bf16[n] = pair_add(gn, t, pos, row_sharded=is_row[n], name=f"rs_pair_add_{n}{layer}")
        pending = (ordered, pair, scatter_comm([(pair_bf16[n], is_row[n]) for n, _ in ordered]))
    grad_x = dh.reshape(x.shape)
    _, landed = run_comm(pending[2], name="rs_scatter_last")
    reduced_parts.append((pending[0], pending[1], landed))

    small_w = [norm_mix, norm_ffn, norm_final, attn_sinks, hgrn_norm, hgrn_lb_logits, ffn_conv_b]
    small_m = [m_norm_mix, m_norm_ffn, m_norm_final, m_attn_sinks, m_hgrn_norm, m_hgrn_lb_logits, m_ffn_conv_b]
    small_v = [v_norm_mix, v_norm_ffn, v_norm_final, v_attn_sinks, v_hgrn_norm, v_hgrn_lb_logits, v_ffn_conv_b]
    small_grads = [jnp.stack(d_norm_mix), jnp.stack(d_norm_ffn), d_norm_final.reshape(D), jnp.stack(d_sinks),
                   jnp.stack(d_hgrn_norm), d_logits, jnp.stack(d_conv_b)]
    reduced = all_reduce_small(_pack_small(small_grads + [loss_part]), name="all_reduce_small")
    small_g = _unpack_small(reduced, small_w + [loss_part])
    loss = small_g.pop().reshape(())
    pad = [jnp.zeros((1, 1), F32)]
    sd, sm_, sv_ = adamw(_pack_small(small_w + pad), reduced, _pack_small(small_m + pad),
                         _pack_small(small_v + pad), name="adamw_small")
    small_d = _unpack_small(sd, small_w)
    small_nm = _unpack_small(sm_, small_w)
    small_nv = _unpack_small(sv_, small_w)

    big_names = col_names + row_names
    big_w = dict(attn_w_in=attn_w_in, hgrn_w_in=hgrn_w_in, ffn_w_up=ffn_w_up, attn_w_out=attn_w_out,
                 hgrn_w_out=hgrn_w_out, ffn_w_down=ffn_w_down)
    half_shard = dict.fromkeys(big_names)
    for ordered, pair, landed in reduced_parts:
        for (n, li), ld in zip(ordered, landed):
            half_shard[n] = chip_add(pair[n], ld, pos, half_shard[n], li, big_w[n].shape[0], row_sharded=is_row[n],
                                     name=f"rs_chip_add_{n}{li}")
    shards = sibling_join_halves([half_shard[n].reshape(big_w[n].shape) for n in big_names], name="rs_sibling_join")
    big_m = dict(attn_w_in=m_attn_w_in, hgrn_w_in=m_hgrn_w_in, ffn_w_up=m_ffn_w_up, attn_w_out=m_attn_w_out,
                 hgrn_w_out=m_hgrn_w_out, ffn_w_down=m_ffn_w_down)
    big_v = dict(attn_w_in=v_attn_w_in, hgrn_w_in=v_hgrn_w_in, ffn_w_up=v_ffn_w_up, attn_w_out=v_attn_w_out,
                 hgrn_w_out=v_hgrn_w_out, ffn_w_down=v_ffn_w_down)
    grads, deltas, new_m, new_v = {}, {}, {}, {}
    for name_, gshard in zip(big_names, shards):
        w_ = big_w[name_]
        cols = w_.shape[-1]
        d_, m_, v_ = adamw(w_.reshape(-1, cols), gshard.reshape(-1, cols), big_m[name_].reshape(-1, cols),
                           big_v[name_].reshape(-1, cols), name=f"adamw_{name_}")
        grads[name_] = gshard.reshape(w_.shape)
        deltas[name_], new_m[name_], new_v[name_] = (t.reshape(w_.shape) for t in (d_, m_, v_))

    gconv = jnp.stack(d_conv_w)
    n_conv = ffn_conv_w.shape[2]
    gconv_all = _unpack_small(all_reduce_small(_pack_small([gconv]), name="all_reduce_conv_w"), [gconv])[0]
    gconv_mine = lax.dynamic_slice_in_dim(gconv_all, me * n_conv, n_conv, axis=2)
    cshape = ffn_conv_w.shape
    d_, m_, v_ = adamw(ffn_conv_w.reshape(-1, n_conv), gconv_mine.reshape(-1, n_conv),
                       m_ffn_conv_w.reshape(-1, n_conv), v_ffn_conv_w.reshape(-1, n_conv), name="adamw_ffn_conv_w")
    grads["ffn_conv_w"] = gconv_mine
    deltas["ffn_conv_w"], new_m["ffn_conv_w"], new_v["ffn_conv_w"] = (t.reshape(cshape) for t in (d_, m_, v_))

    order = ["norm_mix", "norm_ffn", "norm_final", "attn_w_in", "attn_w_out", "attn_sinks", "hgrn_w_in",
             "hgrn_w_out", "hgrn_norm", "hgrn_lb_logits", "ffn_w_up", "ffn_conv_w", "ffn_conv_b", "ffn_w_down"]
    small_names = ["norm_mix", "norm_ffn", "norm_final", "attn_sinks", "hgrn_norm", "hgrn_lb_logits", "ffn_conv_b"]
    for i, name_ in enumerate(small_names):
        grads[name_], deltas[name_], new_m[name_], new_v[name_] = small_g[i], small_d[i], small_nm[i], small_nv[i]
    return (loss, grad_x, *[grads[n] for n in order], *[deltas[n] for n in order],
            *[new_m[n] for n in order], *[new_v[n] for n in order])
```

```python
import functools
import math

import numpy as np
import jax
import jax.numpy as jnp
from jax import lax
from jax.experimental import pallas as pl
from jax.experimental.pallas import tpu as pltpu

F32 = jnp.float32
BF16 = jnp.bfloat16
MESH = pl.DeviceIdType.MESH

HEAD_DIM = 64
Q_PER_KV = 4
WINDOW = 128
HG_EXPAND = 128
HG_CHUNK = 128
EPS = 1e-6
N_CHIPS = 4
N_DEV = 8

ADAM_LR = 0.001
ADAM_B1 = 0.9
ADAM_B2 = 0.999
ADAM_EPS = 1e-08
ADAM_WD = 0.01
ADAM_STEP = 10

VMEM_LIMIT_BYTES = 48 * 1024 * 1024
LANES = 128
SUBLANES = 8


def _params(sem=None):
    return pltpu.CompilerParams(dimension_semantics=sem, vmem_limit_bytes=VMEM_LIMIT_BYTES)


def _pick(n, cands):
    for c in cands:
        if n % c == 0:
            return c
    return n


def _dot(a, b, dims):
    return lax.dot_general(a.astype(BF16), b.astype(BF16), (dims, ((), ())),
                           preferred_element_type=F32)


NN = ((1,), (0,))
NT = ((1,), (1,))
TN = ((0,), (0,))


def mm_nn(a, w, l, *, name, res=None, out_dtype=F32, split_out=False, comm=None):
    M, K = a.shape
    N = w.shape[2]
    tm = _pick(M, (1024, 512, 256, 128))
    tn = _pick(N // 2 if split_out else N, (1408, 1024, 768, 512, 256, 128))
    tk = _pick(K, (1024, 1408, 512, 256, 128))
    nk = K // tk
    nh = (N // 2) // tn

    def body(*refs):
        if res is None:
            a_ref, w_ref, o_ref = refs[:3]
            r_ref = None
        else:
            a_ref, w_ref, r_ref, o_ref = refs[:4]
        part = _dot(a_ref[...], w_ref[...], NN)

        def finish(acc):
            if r_ref is not None:
                acc = acc + r_ref[...]
            o_ref[...] = acc.astype(out_dtype)

        if nk == 1:
            finish(part)
        else:
            acc_ref = refs[-1]
            k = pl.program_id(2)

            @pl.when(k == 0)
            def _():
                acc_ref[...] = part

            @pl.when(k > 0)
            def _():
                acc_ref[...] += part

            @pl.when(k == nk - 1)
            def _():
                finish(acc_ref[...])

    in_specs = [pl.BlockSpec((tm, tk), lambda n, m, k: (m, k)),
                pl.BlockSpec((None, tk, tn), lambda n, m, k: (l, k, n))]
    args = [a, w]
    if res is not None:
        in_specs.append(pl.BlockSpec((tm, tn), lambda n, m, k: (m, n)))
        args.append(res)
    if split_out:
        out_shape = jax.ShapeDtypeStruct((2, M, N // 2), out_dtype)
        out_spec = pl.BlockSpec((None, tm, tn), lambda n, m, k: (n // nh, m, n % nh))
    else:
        out_shape = jax.ShapeDtypeStruct((M, N), out_dtype)
        out_spec = pl.BlockSpec((tm, tn), lambda n, m, k: (m, n))
    outs, updated, _ = _carried_call(
        body, args, name=name, grid=(N // tn, M // tm, nk), in_specs=in_specs, out_specs=[out_spec],
        out_shape=[out_shape], scratch_shapes=[] if nk == 1 else [pltpu.VMEM((tm, tn), F32)], comm=comm)
    return outs[0] if comm is None else (outs[0], updated)


def mm_nt(a, w, l, *, name, split_in=False, out_dtype=F32, comm=None):
    if split_in:
        _, M, Nh = a.shape
        N = 2 * Nh
    else:
        M, N = a.shape
        Nh = N
    K = w.shape[1]
    tm = _pick(M, (1024, 512, 256, 128))
    tko = _pick(K, (1024, 1408, 512, 256, 128))
    tr = _pick(Nh, (1024, 1408, 768, 512, 256, 128))
    nr = N // tr
    nh = Nh // tr

    def body(a_ref, w_ref, o_ref, *scratch):
        part = _dot(a_ref[...], w_ref[...], NT)
        if nr == 1:
            o_ref[...] = part.astype(out_dtype)
        else:
            acc_ref = scratch[0]
            r = pl.program_id(2)

            @pl.when(r == 0)
            def _():
                acc_ref[...] = part

            @pl.when(r > 0)
            def _():
                acc_ref[...] += part

            @pl.when(r == nr - 1)
            def _():
                o_ref[...] = acc_ref[...].astype(out_dtype)

    if split_in:
        a_spec = pl.BlockSpec((None, tm, tr), lambda ko, m, r: (r // nh, m, r % nh))
    else:
        a_spec = pl.BlockSpec((tm, tr), lambda ko, m, r: (m, r))
    outs, _, created = _carried_call(
        body, [a, w], name=name, grid=(K // tko, M // tm, nr),
        in_specs=[a_spec, pl.BlockSpec((None, tko, tr), lambda ko, m, r: (l, ko, r))],
        out_specs=[pl.BlockSpec((tm, tko), lambda ko, m, r: (m, ko))],
        out_shape=[jax.ShapeDtypeStruct((M, K), out_dtype)],
        scratch_shapes=[] if nr == 1 else [pltpu.VMEM((tm, tko), F32)], comm=comm)
    return outs[0] if comm is None else (outs[0], created)


def mm_tn(a, b, *, name):
    M, K = a.shape
    N = b.shape[1]
    tm = _pick(M, (512, 256, 128))
    tko = _pick(K, (1024, 1408, 704, 512, 256, 128))
    tn = _pick(N, (512, 256, 128))
    nm = M // tm

    def body(a_ref, b_ref, o_ref):
        part = _dot(a_ref[...], b_ref[...], TN)
        r = pl.program_id(2)

        @pl.when(r == 0)
        def _():
            o_ref[...] = part

        @pl.when(r > 0)
        def _():
            o_ref[...] += part

    return pl.pallas_call(
        body, name=name, grid=(K // tko, N // tn, nm),
        in_specs=[pl.BlockSpec((tm, tko), lambda ko, n, r: (r, ko)), pl.BlockSpec((tm, tn), lambda ko, n, r: (r, n))],
        out_specs=pl.BlockSpec((None, tko, tn), lambda ko, n, r: (0, ko, n)),
        out_shape=jax.ShapeDtypeStruct((1, K, N), F32),
        compiler_params=_params(("parallel", "parallel", "arbitrary")),
    )(a, b)


def mm_dw(at, b, *, name, split_b=False):
    K, M = at.shape
    if split_b:
        Nh = b.shape[2]
        N = 2 * Nh
    else:
        N = b.shape[1]
        Nh = N
    tm = _pick(M, (1024, 512, 256, 128))
    tko = _pick(K, (1024, 1408, 704, 512, 256, 128))
    tn = _pick(Nh, (1408, 1024, 768, 512, 256, 128))
    nm = M // tm
    nh = Nh // tn

    def body(a_ref, b_ref, o_ref):
        part = _dot(a_ref[...], b_ref[...], NN)
        r = pl.program_id(2)

        @pl.when(r == 0)
        def _():
            o_ref[...] = part

        @pl.when(r > 0)
        def _():
            o_ref[...] += part

    if split_b:
        b_spec = pl.BlockSpec((None, tm, tn), lambda ko, n, r: (n // nh, r, n % nh))
    else:
        b_spec = pl.BlockSpec((tm, tn), lambda ko, n, r: (r, n))
    return pl.pallas_call(
        body, name=name, grid=(K // tko, N // tn, nm),
        in_specs=[pl.BlockSpec((tko, tm), lambda ko, n, r: (ko, r)), b_spec],
        out_specs=pl.BlockSpec((None, tko, tn), lambda ko, n, r: (0, ko, n)),
        out_shape=jax.ShapeDtypeStruct((1, K, N), F32),
        compiler_params=_params(("parallel", "parallel", "arbitrary")),
    )(at, b)


def _rms(x, g):
    return x * lax.rsqrt(jnp.mean(x * x, axis=-1, keepdims=True) + EPS) * g


def rmsnorm_fwd(h, g, l, *, name):
    S, D = h.shape
    tm = _pick(S, (512, 256, 128))

    def body(h_ref, g_ref, o_ref, ot_ref):
        y = _rms(h_ref[...], g_ref[...])
        o_ref[...] = y.astype(BF16)
        ot_ref[...] = y.T.astype(BF16)

    return pl.pallas_call(
        body, name=name, grid=(S // tm,),
        in_specs=[pl.BlockSpec((tm, D), lambda m: (m, 0)),
                  pl.BlockSpec((None, 1, D), lambda m: (l, 0, 0))],
        out_specs=[pl.BlockSpec((tm, D), lambda m: (m, 0)), pl.BlockSpec((D, tm), lambda m: (0, m))],
        out_shape=[jax.ShapeDtypeStruct((S, D), BF16), jax.ShapeDtypeStruct((D, S), BF16)],
        compiler_params=_params(("parallel",)),
    )(h, g)


def rmsnorm_bwd(h, g, l, dhn, dres, *, name):
    S, D = h.shape
    tm = _pick(S, (512, 256, 128))

    def body(h_ref, g_ref, dhn_ref, dres_ref, dh_ref, dg_ref):
        _, vjp = jax.vjp(_rms, h_ref[...], g_ref[...])
        dx, dg = vjp(dhn_ref[...])
        dh_ref[...] = dres_ref[...] + dx

        @pl.when(pl.program_id(0) == 0)
        def _():
            dg_ref[...] = dg

        @pl.when(pl.program_id(0) > 0)
        def _():
            dg_ref[...] += dg

    row = pl.BlockSpec((tm, D), lambda m: (m, 0))
    return pl.pallas_call(
        body, name=name, grid=(S // tm,),
        in_specs=[row, pl.BlockSpec((None, 1, D), lambda m: (l, 0, 0)), row, row],
        out_specs=[row, pl.BlockSpec((1, D), lambda m: (0, 0))],
        out_shape=[jax.ShapeDtypeStruct((S, D), F32), jax.ShapeDtypeStruct((1, D), F32)],
        compiler_params=_params(("arbitrary",)),
    )(h, g, dhn, dres)


def loss_head(h, g, target, *, name):
    S, D = h.shape
    tm = _pick(S, (512, 256, 128))

    def body(h_ref, g_ref, t_ref, loss_ref, dh_ref, dg_ref):
        y, vjp = jax.vjp(_rms, h_ref[...], g_ref[...])
        err = y - t_ref[...]
        part = 0.5 * jnp.sum(jnp.sum(err * err, axis=-1, keepdims=True) / D, axis=0, keepdims=True)
        dx, dg = vjp(err / D)
        dh_ref[...] = dx

        @pl.when(pl.program_id(0) == 0)
        def _():
            dg_ref[...] = dg
            loss_ref[...] = part

        @pl.when(pl.program_id(0) > 0)
        def _():
            dg_ref[...] += dg
            loss_ref[...] += part

    row = pl.BlockSpec((tm, D), lambda m: (m, 0))
    return pl.pallas_call(
        body, name=name, grid=(S // tm,),
        in_specs=[row, pl.BlockSpec((1, D), lambda m: (0, 0)), row],
        out_specs=[pl.BlockSpec((1, 1), lambda m: (0, 0)), row, pl.BlockSpec((1, D), lambda m: (0, 0))],
        out_shape=[jax.ShapeDtypeStruct((1, 1), F32), jax.ShapeDtypeStruct((S, D), F32),
                   jax.ShapeDtypeStruct((1, D), F32)],
        compiler_params=_params(("arbitrary",)),
    )(h, g, target)


def _make_mm(dims_fwd, dims_da, dims_db, swap_da=False, swap_db=False):
    @jax.custom_vjp
    def mm(a, b):
        return _dot(a, b, dims_fwd)

    def fwd(a, b):
        return mm(a, b), (a, b)

    def bwd(resid, g):
        a, b = resid
        da = _dot(b, g, dims_da) if swap_da else _dot(g, b, dims_da)
        db = _dot(g, a, dims_db) if swap_db else _dot(a, g, dims_db)
        return da, db

    mm.defvjp(fwd, bwd)
    return mm


_mm = _make_mm(NN, NT, TN)
_mm_nt = _make_mm(NT, NN, TN, swap_db=True)
_mm_tn = _make_mm(TN, NT, NN, swap_da=True)


def _dot_hi(a, b):
    return jnp.dot(a, b, precision=lax.Precision.HIGHEST, preferred_element_type=F32)


def _sigmoid(x):
    return 0.5 * jnp.tanh(0.5 * x) + 0.5


def _silu(x):
    return x * _sigmoid(x)


def _attn_block(q, kw, vw, sink, slope, first):
    rows = q.shape[0]
    s = _mm_nt(q, kw) * (HEAD_DIM ** -0.5)
    qi = lax.broadcasted_iota(jnp.int32, (rows, 2 * WINDOW), 0) % WINDOW
    ki = lax.broadcasted_iota(jnp.int32, (rows, 2 * WINDOW), 1)
    dist = qi + WINDOW - ki
    valid = (dist >= 0) & (dist < WINDOW) & (ki >= WINDOW * first)
    s = jnp.where(valid, s - slope * dist.astype(F32), -jnp.inf)
    m = lax.stop_gradient(jnp.maximum(jnp.max(s, axis=-1, keepdims=True), sink))
    e = jnp.exp(s - m)
    denom = jnp.sum(e, axis=-1, keepdims=True) + jnp.exp(sink - m)
    return _mm(e * (1.0 / denom), vw)


def _attn_specs(Hkv, order):
    G = Q_PER_KV
    qspec = pl.BlockSpec((Hkv, G, WINDOW, HEAD_DIM), lambda n: (0, 0, order(n), 0))
    prev = pl.BlockSpec((Hkv, WINDOW, HEAD_DIM), lambda n: (0, jnp.maximum(order(n) - 1, 0), 0))
    cur = pl.BlockSpec((Hkv, WINDOW, HEAD_DIM), lambda n: (0, order(n), 0))
    col = pl.BlockSpec((Hkv, G * WINDOW, 1), lambda n: (0, 0, 0))
    return qspec, prev, cur, col


def attn_fwd(q, k, v, sink_col, slope_col, *, name, comm=None):
    Hkv, G, S, _ = q.shape
    qspec, prev, cur, col = _attn_specs(Hkv, lambda n: n)

    def body(q_ref, kp_ref, kc_ref, vp_ref, vc_ref, sink_ref, slope_ref, o_ref):
        first = (pl.program_id(0) == 0).astype(jnp.int32)
        outs = []
        for h in range(Hkv):
            kw = jnp.concatenate([kp_ref[h], kc_ref[h]], axis=0)
            vw = jnp.concatenate([vp_ref[h], vc_ref[h]], axis=0)
            o = _attn_block(q_ref[h].reshape(G * WINDOW, HEAD_DIM), kw, vw, sink_ref[h], slope_ref[h], first)
            outs.append(o.reshape(G, WINDOW, HEAD_DIM).astype(BF16))
        o_ref[...] = jnp.stack(outs)

    outs, updated, created = _carried_call(
        body, [q, k, k, v, v, sink_col, slope_col], name=name, grid=(S // WINDOW,),
        in_specs=[qspec, prev, cur, prev, cur, col, col], out_specs=[qspec],
        out_shape=[jax.ShapeDtypeStruct(q.shape, BF16)], comm=comm)
    return outs[0] if comm is None else (outs[0], updated, created)


def attn_bwd(q, k, v, sink_col, slope_col, do, *, name, comm=None):
    Hkv, G, S, _ = q.shape
    nb = S // WINDOW
    qspec, prev, cur, col = _attn_specs(Hkv, lambda n: nb - 1 - n)

    def body(q_ref, kp_ref, kc_ref, vp_ref, vc_ref, sink_ref, slope_ref, do_ref,
             dq_ref, dk_ref, dv_ref, dsink_ref, dk_carry, dv_carry):
        @pl.when(pl.program_id(0) == 0)
        def _():
            dk_carry[...] = jnp.zeros_like(dk_carry)
            dv_carry[...] = jnp.zeros_like(dv_carry)
            dsink_ref[...] = jnp.zeros_like(dsink_ref)

        first = (pl.program_id(0) == nb - 1).astype(jnp.int32)
        dqs, dkws, dvws, dsinks = [], [], [], []
        for h in range(Hkv):
            kw = jnp.concatenate([kp_ref[h], kc_ref[h]], axis=0).astype(F32)
            vw = jnp.concatenate([vp_ref[h], vc_ref[h]], axis=0).astype(F32)
            fn = functools.partial(_attn_block, slope=slope_ref[h], first=first)
            _, vjp = jax.vjp(fn, q_ref[h].reshape(G * WINDOW, HEAD_DIM).astype(F32), kw, vw, sink_ref[h])
            dq, dkw, dvw, dsink = vjp(do_ref[h].reshape(G * WINDOW, HEAD_DIM).astype(F32))
            dqs.append(dq.reshape(G, WINDOW, HEAD_DIM))
            dkws.append(dkw)
            dvws.append(dvw)
            dsinks.append(dsink)
        dkw, dvw = jnp.stack(dkws), jnp.stack(dvws)
        dq_ref[...] = jnp.stack(dqs).astype(BF16)
        dk_ref[...] = (dkw[:, WINDOW:] + dk_carry[...]).astype(BF16)
        dv_ref[...] = (dvw[:, WINDOW:] + dv_carry[...]).astype(BF16)
        dk_carry[...] = dkw[:, :WINDOW]
        dv_carry[...] = dvw[:, :WINDOW]
        dsink_ref[...] += jnp.stack(dsinks)

    kv_shape = jax.ShapeDtypeStruct(k.shape, BF16)
    outs, _, created = _carried_call(
        body, [q, k, k, v, v, sink_col, slope_col, do], name=name, grid=(nb,),
        in_specs=[qspec, prev, cur, prev, cur, col, col, qspec],
        out_specs=[qspec, cur, cur, col],
        out_shape=[jax.ShapeDtypeStruct(q.shape, BF16), kv_shape, kv_shape,
                   jax.ShapeDtypeStruct(sink_col.shape, F32)],
        scratch_shapes=[pltpu.VMEM((Hkv, WINDOW, HEAD_DIM), F32), pltpu.VMEM((Hkv, WINDOW, HEAD_DIM), F32)],
        comm=comm)
    return outs if comm is None else (*outs, created)


def _hgrn_consts():
    C = HG_CHUNK
    t = np.arange(C)[:, None]
    j = np.arange(C)[None, :]
    cq, ck, mk = [], [], []
    H = C // 2
    while H >= 1:
        start = (t // H) * H
        cq.append((j > start) & (j <= t))
        ck.append((j > t) & (j <= start + H))
        mk.append(((t // H) % 2 == 1) & ((j // H) == (t // H) - 1))
        H //= 2
    ltri = (j <= t)
    ops = np.concatenate([ltri] + cq + ck, axis=0).astype(np.float32)
    ops3 = np.concatenate([ops, ops, ops], axis=1)
    opst3 = np.concatenate([ops.T, ops.T, ops.T], axis=1)
    return (jnp.asarray(ops3, BF16), jnp.asarray(opst3, BF16), jnp.asarray(np.stack(mk).astype(np.float32)))


def _split3(x):
    hi = x.astype(BF16)
    r = x - hi.astype(F32)
    mid = r.astype(BF16)
    lo = (r - mid.astype(F32)).astype(BF16)
    return jnp.concatenate([hi, mid, lo], axis=0)


@jax.custom_vjp
def _seg_sums(g, ops3, opst3):
    C = g.shape[0]
    out = jnp.dot(ops3, _split3(g), preferred_element_type=F32)
    return tuple(out[i * C:(i + 1) * C] for i in range(ops3.shape[0] // C))


def _seg_sums_fwd(g, ops3, opst3):
    return _seg_sums(g, ops3, opst3), opst3


def _seg_sums_bwd(opst3, cts):
    ct = jnp.concatenate(cts, axis=0)
    return jnp.dot(opst3, _split3(ct), preferred_element_type=F32), None, None


_seg_sums.defvjp(_seg_sums_fwd, _seg_sums_bwd)


def _hgrn_head(qr, fr, iv, gr, st, logits, ng, ops3, opst3, mk, layer):
    n_levels = mk.shape[0]
    depth = logits.shape[0]
    e = jnp.exp(logits - lax.stop_gradient(jnp.max(logits, axis=0, keepdims=True)))
    sm = e / jnp.sum(e, axis=0, keepdims=True)
    row = lax.broadcasted_iota(jnp.int32, (depth, HG_EXPAND), 0)
    lb = jnp.sum(jnp.where((row >= 1) & (row <= layer), sm, 0.0), axis=0, keepdims=True)

    q = _silu(qr)
    a = jnp.log(lb)
    c = jnp.log(1.0 - lb) - (jnp.maximum(-fr, 0.0) + jnp.log(1.0 + jnp.exp(-jnp.abs(fr))))
    mx = lax.stop_gradient(jnp.maximum(a, c))
    g = mx + jnp.log(jnp.exp(a - mx) + jnp.exp(c - mx))
    k = (1.0 - lb) * _sigmoid(-fr)

    sums = _seg_sums(g, ops3, opst3)
    b = sums[0]
    b_last = jnp.sum(g, axis=0, keepdims=True)

    o = jnp.sum(q * k, axis=-1, keepdims=True) * iv
    amat = None
    for lv in range(n_levels):
        eq = jnp.exp(sums[1 + lv])
        ek = jnp.exp(sums[1 + n_levels + lv])
        part = _mm_nt(q * eq, k * ek) * mk[lv]
        amat = part if amat is None else amat + part
    o = o + _mm(amat, iv) + _mm_nt(q * jnp.exp(b), st)
    st_new = st * jnp.exp(b_last) + _mm_tn(iv, k * jnp.exp(b_last - b))

    on = o * lax.rsqrt(jnp.mean(o * o, axis=-1, keepdims=True) + EPS) * ng
    return on * _silu(gr), st_new


def _hgrn_in_specs(D, depth, order):
    C = HG_CHUNK
    nl = int(math.log2(C))
    n_ops = 1 + 2 * nl
    return [pl.BlockSpec((C, 4 * D), lambda c: (order(c), 0)),
            pl.BlockSpec((depth, D), lambda c: (0, 0)),
            pl.BlockSpec((1, HG_EXPAND), lambda c: (0, 0)),
            pl.BlockSpec((n_ops * C, 3 * C), lambda c: (0, 0)),
            pl.BlockSpec((C, 3 * n_ops * C), lambda c: (0, 0)),
            pl.BlockSpec((nl, C, C), lambda c: (0, 0, 0))]


def _hgrn_cols(D, h):
    return [slice(j * D + h * HG_EXPAND, j * D + (h + 1) * HG_EXPAND) for j in range(4)]


def hgrn_fwd(proj, logits, ng, layer, *, name, comm=None):
    S = proj.shape[0]
    D = proj.shape[1] // 4
    H = D // HG_EXPAND
    C = HG_CHUNK
    nc = S // C
    consts = _hgrn_consts()

    def body(p_ref, lg_ref, ng_ref, ops3_ref, opst3_ref, mk_ref, o_ref, st_out_ref, st_ref):
        @pl.when(pl.program_id(0) == 0)
        def _():
            st_ref[...] = jnp.zeros_like(st_ref)

        st_out_ref[...] = st_ref[...]
        ops3, opst3, mk = ops3_ref[...], opst3_ref[...], mk_ref[...]
        for h in range(H):
            sl = slice(h * HG_EXPAND, (h + 1) * HG_EXPAND)
            cols = _hgrn_cols(D, h)
            out, st_new = _hgrn_head(p_ref[:, cols[0]], p_ref[:, cols[1]], p_ref[:, cols[2]], p_ref[:, cols[3]],
                                     st_ref[h], lg_ref[:, sl], ng_ref[...], ops3, opst3, mk, layer)
            o_ref[:, sl] = out.astype(BF16)
            st_ref[h] = st_new

    outs, updated, created = _carried_call(
        body, [proj, logits, ng, *consts], name=name, grid=(nc,),
        in_specs=_hgrn_in_specs(D, logits.shape[0], lambda c: c),
        out_specs=[pl.BlockSpec((C, D), lambda c: (c, 0)),
                   pl.BlockSpec((None, H, HG_EXPAND, HG_EXPAND), lambda c: (c, 0, 0, 0))],
        out_shape=[jax.ShapeDtypeStruct((S, D), BF16),
                   jax.ShapeDtypeStruct((nc, H, HG_EXPAND, HG_EXPAND), F32)],
        scratch_shapes=[pltpu.VMEM((H, HG_EXPAND, HG_EXPAND), F32)], comm=comm)
    return outs if comm is None else (*outs, updated, created)


def hgrn_bwd(proj, logits, ng, states, dout, layer, *, name, comm=None):
    S = proj.shape[0]
    D = proj.shape[1] // 4
    H = D // HG_EXPAND
    C = HG_CHUNK
    nc = S // C
    depth = logits.shape[0]
    consts = _hgrn_consts()
    rev = lambda c: nc - 1 - c

    def body(p_ref, lg_ref, ng_ref, ops3_ref, opst3_ref, mk_ref, st_in_ref, do_ref,
             dp_ref, dlg_ref, dng_ref, dst_ref):
        @pl.when(pl.program_id(0) == 0)
        def _():
            dst_ref[...] = jnp.zeros_like(dst_ref)
            dlg_ref[...] = jnp.zeros_like(dlg_ref)
            dng_ref[...] = jnp.zeros_like(dng_ref)

        ops3, opst3, mk = ops3_ref[...], opst3_ref[...], mk_ref[...]
        for h in range(H):
            sl = slice(h * HG_EXPAND, (h + 1) * HG_EXPAND)
            cols = _hgrn_cols(D, h)
            fn = functools.partial(_hgrn_head, ops3=ops3, opst3=opst3, mk=mk, layer=layer)
            _, vjp = jax.vjp(fn, p_ref[:, cols[0]], p_ref[:, cols[1]], p_ref[:, cols[2]], p_ref[:, cols[3]],
                             st_in_ref[h], lg_ref[:, sl], ng_ref[...])
            dq, df, di, dg, dst, dlg, dng = vjp((do_ref[:, sl], dst_ref[h]))
            dp_ref[:, cols[0]] = dq.astype(BF16)
            dp_ref[:, cols[1]] = df.astype(BF16)
            dp_ref[:, cols[2]] = di.astype(BF16)
            dp_ref[:, cols[3]] = dg.astype(BF16)
            dst_ref[h] = dst
            dlg_ref[:, sl] += dlg
            dng_ref[...] += dng

    in_specs = _hgrn_in_specs(D, depth, rev)
    in_specs += [pl.BlockSpec((None, H, HG_EXPAND, HG_EXPAND), lambda c: (rev(c), 0, 0, 0)),
                 pl.BlockSpec((C, D), lambda c: (rev(c), 0))]
    outs, _, created = _carried_call(
        body, [proj, logits, ng, *consts, states, dout], name=name, grid=(nc,), in_specs=in_specs,
        out_specs=[pl.BlockSpec((C, 4 * D), lambda c: (rev(c), 0)),
                   pl.BlockSpec((depth, D), lambda c: (0, 0)), pl.BlockSpec((1, HG_EXPAND), lambda c: (0, 0))],
        out_shape=[jax.ShapeDtypeStruct((S, 4 * D), BF16), jax.ShapeDtypeStruct((depth, D), F32),
                   jax.ShapeDtypeStruct((1, HG_EXPAND), F32)],
        scratch_shapes=[pltpu.VMEM((H, HG_EXPAND, HG_EXPAND), F32)], comm=comm)
    return outs if comm is None else (*outs, created)


def _shift_down(u, prev, shift):
    rolled = pltpu.roll(u, shift, axis=0)
    rows = lax.broadcasted_iota(jnp.int32, prev.shape, 0)
    top = jnp.where(rows < shift, pltpu.roll(prev, shift, axis=0), rolled[:SUBLANES])
    return jnp.concatenate([top, rolled[SUBLANES:]], axis=0)


def _shift_up(x, nxt, shift):
    tm = x.shape[0]
    rolled = pltpu.roll(x, tm - shift, axis=0)
    rows = lax.broadcasted_iota(jnp.int32, nxt.shape, 0)
    bottom = jnp.where(rows >= SUBLANES - shift, pltpu.roll(nxt, SUBLANES - shift, axis=0), rolled[tm - SUBLANES:])
    return jnp.concatenate([rolled[:tm - SUBLANES], bottom], axis=0)


def _conv(u, prev, w_ref, b_ref, half):
    u1 = _shift_down(u, prev, 1)
    u2 = _shift_down(u, prev, 2)
    c = b_ref[half] + w_ref[half, 0:1, :] * u2 + w_ref[half, 1:2, :] * u1 + w_ref[half, 2:3, :] * u
    return c, u1, u2


def _conv_specs(tm, tc, l):
    nprev = lambda m: jnp.maximum(m * (tm // SUBLANES) - 1, 0)
    tile = pl.BlockSpec((2, tm, tc), lambda j, m: (0, m, j))
    halo = pl.BlockSpec((2, SUBLANES, tc), lambda j, m: (0, nprev(m), j))
    wspec = pl.BlockSpec((None, 2, 3, tc), lambda j, m: (l, 0, 0, j))
    bspec = pl.BlockSpec((None, 2, 1, tc), lambda j, m: (l, 0, 0, j))
    return tile, halo, wspec, bspec


def convact_fwd(u, cw, cb, l, *, name):
    _, S, Fh = u.shape
    tm = _pick(S, (256, 128))
    tc = _pick(Fh, (1408, 512, 256, 128))
    tile, halo, wspec, bspec = _conv_specs(tm, tc, l)

    def body(u_ref, p_ref, w_ref, b_ref, o_ref, ot_ref):
        keep = (pl.program_id(1) > 0).astype(F32)
        cg, _, _ = _conv(u_ref[0], p_ref[0] * keep, w_ref, b_ref, 0)
        cv, _, _ = _conv(u_ref[1], p_ref[1] * keep, w_ref, b_ref, 1)
        act = _silu(cg) * cv
        o_ref[...] = act.astype(BF16)
        ot_ref[...] = act.T.astype(BF16)

    return pl.pallas_call(
        body, name=name, grid=(Fh // tc, S // tm), in_specs=[tile, halo, wspec, bspec],
        out_specs=[pl.BlockSpec((tm, tc), lambda j, m: (m, j)), pl.BlockSpec((tc, tm), lambda j, m: (j, m))],
        out_shape=[jax.ShapeDtypeStruct((S, Fh), BF16), jax.ShapeDtypeStruct((Fh, S), BF16)],
        compiler_params=_params(("parallel", "parallel")),
    )(u, u, cw, cb)


def convact_bwd(u, cw, cb, l, da, *, name):
    _, S, Fh = u.shape
    tm = _pick(S, (256, 128))
    tc = _pick(Fh, (1408, 512, 256, 128))
    tile, halo, wspec, bspec = _conv_specs(tm, tc, l)

    def body(u_ref, p_ref, w_ref, b_ref, da_ref, dc_ref, dw_ref, db_ref):
        first = pl.program_id(1) == 0
        keep = jnp.logical_not(first).astype(F32)
        ug, uv = u_ref[0], u_ref[1]
        cg, g1, g2 = _conv(ug, p_ref[0] * keep, w_ref, b_ref, 0)
        cv, v1, v2 = _conv(uv, p_ref[1] * keep, w_ref, b_ref, 1)
        da_t = da_ref[...]
        sg = _sigmoid(cg)
        dcv = da_t * (cg * sg)
        dcg = da_t * cv * (sg * (1.0 + cg * (1.0 - sg)))
        dc_ref[0] = dcg
        dc_ref[1] = dcv

        @pl.when(first)
        def _():
            dw_ref[...] = jnp.zeros_like(dw_ref)
            db_ref[...] = jnp.zeros_like(db_ref)

        rs = lambda t: jnp.sum(t, axis=0, keepdims=True)
        for half, dc, taps in ((0, dcg, (g2, g1, ug)), (1, dcv, (v2, v1, uv))):
            for j in range(3):
                dw_ref[half, j:j + 1, :] += rs(dc * taps[j])
            db_ref[half] += rs(dc)

    return pl.pallas_call(
        body, name=name, grid=(Fh // tc, S // tm),
        in_specs=[tile, halo, wspec, bspec, pl.BlockSpec((tm, tc), lambda j, m: (m, j))],
        out_specs=[tile, pl.BlockSpec((2, 3, tc), lambda j, m: (0, 0, j)),
                   pl.BlockSpec((2, 1, tc), lambda j, m: (0, 0, j))],
        out_shape=[jax.ShapeDtypeStruct(u.shape, F32), jax.ShapeDtypeStruct((2, 3, Fh), F32),
                   jax.ShapeDtypeStruct((2, 1, Fh), F32)],
        compiler_params=_params(("parallel", "arbitrary")),
    )(u, u, cw, cb, da)


def conv_input_grad(dc, cw, l, *, name):
    _, S, Fh = dc.shape
    tm = _pick(S, (256, 128))
    tc = _pick(Fh, (1408, 512, 256, 128))
    nblk = S // SUBLANES
    nm = S // tm

    def body(dc_ref, n_ref, w_ref, du_ref):
        keep = (pl.program_id(1) < nm - 1).astype(F32)
        for half in range(2):
            t = dc_ref[half]
            nxt = n_ref[half] * keep
            du = (w_ref[half, 2:3, :] * t + w_ref[half, 1:2, :] * _shift_up(t, nxt, 1)
                  + w_ref[half, 0:1, :] * _shift_up(t, nxt, 2))
            du_ref[half] = du.astype(BF16)

    tile = pl.BlockSpec((2, tm, tc), lambda j, m: (0, m, j))
    halo = pl.BlockSpec((2, SUBLANES, tc),
                        lambda j, m: (0, jnp.minimum((m + 1) * (tm // SUBLANES), nblk - 1), j))
    return pl.pallas_call(
        body, name=name, grid=(Fh // tc, nm),
        in_specs=[tile, halo, pl.BlockSpec((None, 2, 3, tc), lambda j, m: (l, 0, 0, j))],
        out_specs=tile, out_shape=jax.ShapeDtypeStruct(dc.shape, BF16),
        compiler_params=_params(("parallel", "parallel")),
    )(dc, dc, cw)


def _adamw_math(w, g, m, v):
    m = ADAM_B1 * m + (1.0 - ADAM_B1) * g
    v = ADAM_B2 * v + (1.0 - ADAM_B2) * (g * g)
    m_hat = m / (1.0 - ADAM_B1 ** ADAM_STEP)
    v_hat = v / (1.0 - ADAM_B2 ** ADAM_STEP)
    delta = -ADAM_LR * (m_hat / (jnp.sqrt(v_hat) + ADAM_EPS) + ADAM_WD * w)
    return delta, m, v


def _row_tile(R, Ccols, n_streams):
    for tr in (512, 256, 128, 64, 32, 16, 8):
        if R % tr == 0 and tr * Ccols * 4 * n_streams * 2 <= VMEM_LIMIT_BYTES // 2:
            return tr
    return R


def adamw(w, g, m, v, *, name):
    R, Ccols = w.shape
    tr = _row_tile(R, Ccols, 7)

    def body(w_ref, g_ref, m_ref, v_ref, d_ref, mo_ref, vo_ref):
        d, mn, vn = _adamw_math(w_ref[...], g_ref[...], m_ref[...], v_ref[...])
        d_ref[...] = d
        mo_ref[...] = mn
        vo_ref[...] = vn

    spec = pl.BlockSpec((tr, Ccols), lambda i: (i, 0))
    shp = jax.ShapeDtypeStruct((R, Ccols), F32)
    return pl.pallas_call(
        body, name=name, grid=(R // tr,), in_specs=[spec] * 4, out_specs=[spec] * 3,
        out_shape=[shp, shp, shp], compiler_params=_params(("parallel",)),
    )(w, g, m, v)


def _prefetch_call(body, pos, args, *, name, grid, in_specs, out_specs, out_shape, sem, aliases=None):
    return pl.pallas_call(
        lambda pos_ref, *refs: body(*refs), name=name,
        grid_spec=pltpu.PrefetchScalarGridSpec(num_scalar_prefetch=1, grid=grid, in_specs=in_specs,
                                               out_specs=out_specs),
        out_shape=out_shape, input_output_aliases={1 + i: o for i, o in (aliases or {}).items()},
        compiler_params=_params(sem),
    )(pos, *args)


def cast_place(w, pos, *, row_sharded, name):
    L, R, Ccols = w.shape

    def body(w_ref, o_ref):
        o_ref[...] = w_ref[...].astype(BF16)

    if row_sharded:
        grid = (L,)
        in_spec = pl.BlockSpec((None, R, Ccols), lambda l, pos: (l, 0, 0))
        out_spec = pl.BlockSpec((None, R, Ccols), lambda l, pos: (l, pos[0], 0))
        out_shape = jax.ShapeDtypeStruct((L, N_CHIPS * R, Ccols), BF16)
        sem = ("parallel",)
    else:
        tr = _pick(R, (512, 256, 128))
        grid = (L, R // tr)
        in_spec = pl.BlockSpec((None, tr, Ccols), lambda l, i, pos: (l, i, 0))
        out_spec = pl.BlockSpec((None, tr, Ccols), lambda l, i, pos: (l, i, pos[0]))
        out_shape = jax.ShapeDtypeStruct((L, R, N_CHIPS * Ccols), BF16)
        sem = ("parallel", "parallel")
    return _prefetch_call(body, pos, [w], name=name, grid=grid, in_specs=[in_spec], out_specs=out_spec,
                          out_shape=out_shape, sem=sem)


def pair_add(g, theirs, pos, *, row_sharded, name):
    def body(g_ref, t_ref, o_ref, ob_ref):
        s = g_ref[...] + t_ref[...]
        o_ref[...] = s
        ob_ref[...] = s.astype(BF16)

    if row_sharded:
        L, _, _, r2, D = g.shape
        grid = (L, N_CHIPS)
        g_spec = pl.BlockSpec((None, None, None, r2, D), lambda l, q, pos: (l, q, pos[1], 0, 0))
        t_spec = pl.BlockSpec((None, None, r2, D), lambda l, q, pos: (l, q, 0, 0))
    else:
        L, K, N = g.shape
        n = N // N_CHIPS
        tr = _pick(K // 2, (256, 128))
        nb = (K // 2) // tr
        grid = (L, nb, N_CHIPS)
        g_spec = pl.BlockSpec((None, tr, n), lambda l, i, j, pos: (l, pos[1] * nb + i, j))
        t_spec = pl.BlockSpec((None, tr, n), lambda l, i, j, pos: (l, i, j))
    return _prefetch_call(body, pos, [g, theirs], name=name, grid=grid, in_specs=[g_spec, t_spec],
                          out_specs=[t_spec, t_spec],
                          out_shape=[jax.ShapeDtypeStruct(theirs.shape, F32), jax.ShapeDtypeStruct(theirs.shape, BF16)],
                          sem=("parallel",) * len(grid))


def chip_add(pair, landed, pos, buf, li, n_layers, *, row_sharded, name):
    def body(p_ref, a_ref, b_ref, c_ref, *rest):
        rest[-1][...] = ((p_ref[...] + a_ref[...].astype(F32)) + b_ref[...].astype(F32)) + c_ref[...].astype(F32)

    if row_sharded:
        _, _, r2, D = pair.shape
        grid = (1,)
        p_spec = pl.BlockSpec((None, None, r2, D), lambda i, pos: (0, pos[0], 0, 0))
        l_specs = [pl.BlockSpec((None, r2, D), functools.partial(lambda i, pos, k: (k, 0, 0), k=k)) for k in range(3)]
        out_spec = pl.BlockSpec((None, None, r2, D), lambda i, pos: (li, pos[1], 0, 0))
        out_shape = jax.ShapeDtypeStruct((n_layers, 2, r2, D), F32)
    else:
        _, K2, N = pair.shape
        n = N // N_CHIPS
        tr = _pick(K2, (256, 128))
        nb = K2 // tr
        grid = (nb,)
        p_spec = pl.BlockSpec((None, tr, n), lambda i, pos: (0, i, pos[0]))
        l_specs = [pl.BlockSpec((None, tr, n), functools.partial(lambda i, pos, k: (k, i, 0), k=k)) for k in range(3)]
        out_spec = pl.BlockSpec((None, tr, n), lambda i, pos: (li, pos[1] * nb + i, 0))
        out_shape = jax.ShapeDtypeStruct((n_layers, 2 * K2, n), F32)
    args, in_specs, aliases = [pair, landed, landed, landed], [p_spec] + l_specs, None
    if buf is not None:
        args.append(buf)
        in_specs.append(pl.BlockSpec(memory_space=pl.ANY))
        aliases = {4: 0}
    return _prefetch_call(body, pos, args, name=name, grid=grid, in_specs=in_specs, out_specs=out_spec,
                          out_shape=out_shape, sem=("parallel",), aliases=aliases)


def _position():
    return lax.axis_index("x"), lax.axis_index("y"), lax.axis_index("c")


def _other_chips(x, y):
    return [(1 - x, y), (x, 1 - y), (1 - x, 1 - y)]


HBM_ANY = pl.BlockSpec(memory_space=pl.ANY)
COMM_PARAMS = pltpu.CompilerParams(has_side_effects=True)


class Comm:
    def __init__(self, ro=(), inout=(), new=(), sems=(), phases=()):
        self.ro, self.inout, self.new, self.sems, self.phases = list(ro), list(inout), list(new), list(sems), list(phases)


def _carried_call(body, args, *, name, grid, in_specs, out_specs, out_shape, scratch_shapes=(), comm=None):
    comm = comm or Comm()
    n_in, n_out, n_sc = len(in_specs), len(out_shape), len(scratch_shapes)
    n_ro, n_io, n_new, n_sem = len(comm.ro), len(comm.inout), len(comm.new), len(comm.sems)
    last = int(np.prod(grid)) - 1

    def wrapped(*refs):
        bounds = np.cumsum([0, n_in, n_ro, n_io, n_out, n_io, n_new, n_sc, n_sem])
        core_in, ro, _, core_out, io, new, core_sc, sems = (refs[a:b] for a, b in zip(bounds[:-1], bounds[1:]))
        step = pl.program_id(0)
        for d in range(1, len(grid)):
            step = step * grid[d] + pl.program_id(d)
        for frac, fn in comm.phases[:-1]:
            pl.when(step == int(frac * last))(functools.partial(fn, ro, io, new, sems))
        body(*core_in, *core_out, *core_sc)
        if comm.phases:
            pl.when(step == last)(functools.partial(comm.phases[-1][1], ro, io, new, sems))

    outs = pl.pallas_call(
        wrapped, name=name, grid=grid,
        in_specs=list(in_specs) + [HBM_ANY] * (n_ro + n_io),
        out_specs=list(out_specs) + [HBM_ANY] * (n_io + n_new),
        out_shape=list(out_shape) + [jax.ShapeDtypeStruct(a.shape, a.dtype) for a in comm.inout] + comm.new,
        input_output_aliases={n_in + n_ro + j: n_out + j for j in range(n_io)},
        scratch_shapes=list(scratch_shapes) + comm.sems,
        compiler_params=pltpu.CompilerParams(dimension_semantics=("arbitrary",) * len(grid),
                                             vmem_limit_bytes=VMEM_LIMIT_BYTES, has_side_effects=bool(comm.phases)),
    )(*args, *comm.ro, *comm.inout)
    return outs[:n_out], outs[n_out:n_out + n_io], outs[n_out + n_io:]


def run_comm(comm, *, name):
    n_ro, n_io, n_new = len(comm.ro), len(comm.inout), len(comm.new)

    def body(*refs):
        bounds = np.cumsum([0, n_ro, n_io, n_io, n_new, len(comm.sems)])
        ro, _, io, new, sems = (refs[a:b] for a, b in zip(bounds[:-1], bounds[1:]))
        for _, fn in comm.phases:
            fn(ro, io, new, sems)

    outs = pl.pallas_call(
        body, name=name, in_specs=[HBM_ANY] * (n_ro + n_io), out_specs=[HBM_ANY] * (n_io + n_new),
        out_shape=[jax.ShapeDtypeStruct(a.shape, a.dtype) for a in comm.inout] + comm.new,
        input_output_aliases={n_ro + j: j for j in range(n_io)},
        scratch_shapes=comm.sems, compiler_params=COMM_PARAMS,
    )(*comm.ro, *comm.inout)
    return outs[:n_io], outs[n_io:]


def _rdma(src, dst, send_sem, recv_sem, device):
    return pltpu.make_async_remote_copy(src_ref=src, dst_ref=dst, send_sem=send_sem, recv_sem=recv_sem,
                                        device_id=device, device_id_type=MESH)


def gather_comm(bufs, slices):
    n = len(slices)

    def region(io, s, h, chip):
        b, li, row_sharded = slices[s]
        ref = io[b]
        if row_sharded:
            r = ref.shape[1] // N_CHIPS
            return ref.at[li, pl.ds(chip * r + h * (r // 2), r // 2), :]
        K, nn = ref.shape[1], ref.shape[2] // N_CHIPS
        return ref.at[li, pl.ds(h * (K // 2), K // 2), pl.ds(chip * nn, nn)]

    def copies(io, sems):
        x, y, c = _position()
        me = 2 * x + y
        chips = _other_chips(x, y)
        ids = [2 * cx + cy for cx, cy in chips]

        def ici(s, k, src):
            return _rdma(region(io, s, c, me), region(io, s, c, src), sems[0].at[s, k], sems[1].at[s, k], (*chips[k], c))

        def d2d(s, k, h):
            return _rdma(region(io, s, h, ids[k]), region(io, s, h, ids[k]), sems[2].at[s, k], sems[3].at[s, k],
                         (x, y, 1 - c))

        return me, ids, c, ici, d2d

    pairs = [(s, k) for s in range(n) for k in range(3)]

    def start(ro, io, new, sems):
        me, _, _, ici, _ = copies(io, sems)
        for s, k in pairs:
            ici(s, k, me).start()

    def pass_on(ro, io, new, sems):
        _, ids, c, ici, d2d = copies(io, sems)
        for s, k in pairs:
            ici(s, k, ids[k]).wait_recv()
            d2d(s, k, c).start()

    def finish(ro, io, new, sems):
        me, _, c, ici, d2d = copies(io, sems)
        for s, k in pairs:
            d2d(s, k, 1 - c).wait_recv()
        for s, k in pairs:
            ici(s, k, me).wait_send()
            d2d(s, k, c).wait_send()

    return Comm(inout=bufs, sems=[pltpu.SemaphoreType.DMA((n, 3))] * 4,
                phases=[(0.0, start), (0.6, pass_on), (1.0, finish)])


def gather_small(shard, *, name):
    n = shard.shape[2]

    def body(s_ref, o_ref, send_sem, recv_sem, local_sem):
        x, y, c = _position()
        me = 2 * x + y
        chips = _other_chips(x, y)
        place = lambda chip: o_ref.at[:, :, pl.ds(chip * n, n)]
        local = pltpu.make_async_copy(s_ref, place(me), local_sem)
        local.start()
        for k in range(3):
            _rdma(s_ref, place(me), send_sem.at[k], recv_sem.at[k], (*chips[k], c)).start()
        for k in range(3):
            _rdma(s_ref, place(2 * chips[k][0] + chips[k][1]), send_sem.at[k], recv_sem.at[k], (*chips[k], c)).wait()
        local.wait()

    return pl.pallas_call(
        body, name=name, in_specs=[HBM_ANY], out_specs=HBM_ANY,
        out_shape=jax.ShapeDtypeStruct((shard.shape[0], shard.shape[1], N_CHIPS * n), shard.dtype),
        scratch_shapes=[pltpu.SemaphoreType.DMA((3,)), pltpu.SemaphoreType.DMA((3,)), pltpu.SemaphoreType.DMA],
        compiler_params=COMM_PARAMS,
    )(shard)


def swap_comm(grads):
    def copies(ro, new, sems):
        x, y, c = _position()
        h = 1 - c
        cps = []
        for s, (a, row_sharded) in enumerate(grads):
            if row_sharded:
                r = a.shape[1] // N_CHIPS
                for q in range(N_CHIPS):
                    cps.append(_rdma(ro[s].at[:, pl.ds(q * r + h * (r // 2), r // 2), :], new[s].at[:, q],
                                     sems[0].at[s, q], sems[1].at[s, q], (x, y, 1 - c)))
            else:
                K = a.shape[1]
                cps.append(_rdma(ro[s].at[:, pl.ds(h * (K // 2), K // 2), :], new[s],
                                 sems[0].at[s, 0], sems[1].at[s, 0], (x, y, 1 - c)))
        return cps

    def start(ro, io, new, sems):
        for cp in copies(ro, new, sems):
            cp.start()

    def finish(ro, io, new, sems):
        for cp in copies(ro, new, sems):
            cp.wait()

    theirs = []
    for a, row_sharded in grads:
        if row_sharded:
            theirs.append(jax.ShapeDtypeStruct((1, N_CHIPS, a.shape[1] // N_CHIPS // 2, a.shape[2]), a.dtype))
        else:
            theirs.append(jax.ShapeDtypeStruct((1, a.shape[1] // 2, a.shape[2]), a.dtype))
    return Comm(ro=[a for a, _ in grads], new=theirs, sems=[pltpu.SemaphoreType.DMA((len(grads), N_CHIPS))] * 2,
                phases=[(0.0, start), (1.0, finish)])


def merge_comms(comms):
    spans, ro, io, new, sems = [], [], [], [], []
    for cm in comms:
        spans.append((len(ro), len(io), len(new), len(sems)))
        ro, io, new, sems = ro + cm.ro, io + cm.inout, new + cm.new, sems + cm.sems

    def phase(frac):
        def run(R, I, N, S):
            for cm, (a, b, d, e) in zip(comms, spans):
                for f, fn in cm.phases:
                    if f == frac:
                        fn(R[a:a + len(cm.ro)], I[b:b + len(cm.inout)], N[d:d + len(cm.new)], S[e:e + len(cm.sems)])
        return run

    fracs = sorted({f for cm in comms for f, _ in cm.phases})
    merged = Comm(ro=ro, inout=io, new=new, sems=sems, phases=[(f, phase(f)) for f in fracs])
    cut = lambda created: [created[d:d + len(cm.new)] for cm, (_, _, d, _) in zip(comms, spans)]
    return merged, cut


def scatter_comm(halves):
    def copies(ro, new, sems):
        x, y, c = _position()
        chips = _other_chips(x, y)
        cps = []
        for s, (a, row_sharded) in enumerate(halves):
            for k in range(3):
                to = 2 * chips[k][0] + chips[k][1]
                if row_sharded:
                    src = ro[s].at[0, to]
                else:
                    n = a.shape[2] // N_CHIPS
                    src = ro[s].at[0, :, pl.ds(to * n, n)]
                cps.append(_rdma(src, new[s].at[k], sems[0].at[s, k], sems[1].at[s, k], (*chips[k], c)))
        return cps

    def start(ro, io, new, sems):
        for cp in copies(ro, new, sems):
            cp.start()

    def finish(ro, io, new, sems):
        for cp in copies(ro, new, sems):
            cp.wait()

    landing = [jax.ShapeDtypeStruct((3, a.shape[2], a.shape[3]) if row_sharded else
                                    (3, a.shape[1], a.shape[2] // N_CHIPS), a.dtype) for a, row_sharded in halves]
    return Comm(ro=[a for a, _ in halves], new=landing, sems=[pltpu.SemaphoreType.DMA((len(halves), 3))] * 2,
                phases=[(0.0, start), (1.0, finish)])


def sibling_join_halves(shards, *, name):
    n_in = len(shards)

    def body(*refs):
        ins = refs[:n_in]
        outs = refs[n_in:2 * n_in]
        send_sem, recv_sem = refs[2 * n_in:]
        x, y, c = _position()
        sibling = (x, y, 1 - c)

        def rows(ref, h):
            R2 = ref.shape[1] // 2
            return ref.at[:, pl.ds(h * R2, R2), :]

        sends = [pltpu.make_async_remote_copy(src_ref=rows(ins[i], c), dst_ref=rows(outs[i], c),
                                              send_sem=send_sem.at[i], recv_sem=recv_sem.at[i],
                                              device_id=sibling, device_id_type=MESH) for i in range(n_in)]
        lands = [pltpu.make_async_remote_copy(src_ref=rows(ins[i], c), dst_ref=rows(outs[i], 1 - c),
                                              send_sem=send_sem.at[i], recv_sem=recv_sem.at[i],
                                              device_id=sibling, device_id_type=MESH) for i in range(n_in)]
        for cp in sends:
            cp.start()
        for i in range(n_in):
            sends[i].wait_send()
            lands[i].wait_recv()

    return pl.pallas_call(
        body, name=name, in_specs=[HBM_ANY] * n_in, out_specs=[HBM_ANY] * n_in,
        out_shape=[jax.ShapeDtypeStruct(a.shape, a.dtype) for a in shards],
        input_output_aliases={i: i for i in range(n_in)},
        scratch_shapes=[pltpu.SemaphoreType.DMA((n_in,)), pltpu.SemaphoreType.DMA((n_in,))],
        compiler_params=COMM_PARAMS,
    )(*shards)


def all_reduce_small(v, *, name):
    R, Ccols = v.shape

    def body(v_ref, o_ref, slots, send_sem, recv_sem):
        x, y, c = _position()
        me = 4 * x + 2 * y + c
        slots[me] = v_ref[...]

        def peer(d):
            return x ^ (d >> 2), y ^ ((d >> 1) & 1), c ^ (d & 1)

        def copy(d, slot):
            return pltpu.make_async_remote_copy(
                src_ref=v_ref, dst_ref=slots.at[slot], send_sem=send_sem.at[d], recv_sem=recv_sem.at[d],
                device_id=peer(d), device_id_type=MESH)

        sends = [copy(d, me) for d in range(1, N_DEV)]
        for cp in sends:
            cp.start()
        for d in range(1, N_DEV):
            px, py, pc = peer(d)
            copy(d, 4 * px + 2 * py + pc).wait_recv()
        for cp in sends:
            cp.wait_send()
        acc = slots[0]
        for d in range(1, N_DEV):
            acc = acc + slots[d]
        o_ref[...] = acc

    return pl.pallas_call(
        body, name=name,
        in_specs=[pl.BlockSpec(memory_space=pltpu.VMEM)], out_specs=pl.BlockSpec(memory_space=pltpu.VMEM),
        out_shape=jax.ShapeDtypeStruct((R, Ccols), F32),
        scratch_shapes=[pltpu.VMEM((N_DEV, R, Ccols), F32), pltpu.SemaphoreType.DMA((N_DEV,)),
                        pltpu.SemaphoreType.DMA((N_DEV,))],
        compiler_params=COMM_PARAMS,
    )(v)


def _pack_small(parts):
    flat = jnp.concatenate([p.reshape(-1).astype(F32) for p in parts])
    n = flat.shape[0]
    rows = -(-n // LANES)
    rows = -(-rows // SUBLANES) * SUBLANES
    return jnp.pad(flat, (0, rows * LANES - n)).reshape(rows, LANES)


def _unpack_small(packed, like):
    flat = packed.reshape(-1)
    out, off = [], 0
    for p in like:
        out.append(flat[off:off + p.size].reshape(p.shape))
        off += p.size
    return out


def kernel(x, norm_mix, norm_ffn, norm_final, attn_w_in, attn_w_out, attn_sinks, hgrn_w_in, hgrn_w_out, hgrn_norm, hgrn_lb_logits, ffn_w_up, ffn_conv_w, ffn_conv_b, ffn_w_down, loss_target, m_norm_mix, m_norm_ffn, m_norm_final, m_attn_w_in, m_attn_w_out, m_attn_sinks, m_hgrn_w_in, m_hgrn_w_out, m_hgrn_norm, m_hgrn_lb_logits, m_ffn_w_up, m_ffn_conv_w, m_ffn_conv_b, m_ffn_w_down, v_norm_mix, v_norm_ffn, v_norm_final, v_attn_w_in, v_attn_w_out, v_attn_sinks, v_hgrn_w_in, v_hgrn_w_out, v_hgrn_norm, v_hgrn_lb_logits, v_ffn_w_up, v_ffn_conv_w, v_ffn_conv_b, v_ffn_w_down):
    S, D = x.shape[1], x.shape[2]
    depth = norm_mix.shape[0]
    n_q = D // HEAD_DIM
    n_kv = n_q // Q_PER_KV
    Fh = ffn_w_down.shape[1] * N_CHIPS
    h = x.reshape(S, D)
    target = loss_target.reshape(S, D)

    col_names = ["attn_w_in", "hgrn_w_in", "ffn_w_up"]
    row_names = ["attn_w_out", "hgrn_w_out", "ffn_w_down"]
    x_, y_, c_ = _position()
    me = 2 * x_ + y_
    pos = jnp.stack([me, c_]).astype(jnp.int32)
    col_w = [cast_place(w, pos, row_sharded=False, name=f"cast_place_{n}")
             for n, w in zip(col_names, (attn_w_in, hgrn_w_in, ffn_w_up))]
    row_w = [cast_place(w, pos, row_sharded=True, name=f"cast_place_{n}")
             for n, w in zip(row_names, (attn_w_out, hgrn_w_out, ffn_w_down))]
    W = dict(zip(col_names + row_names, col_w + row_w))
    is_row = {n: n in row_names for n in W}

    def layer_weights(layer):
        mixer = "attn" if layer % 2 == 0 else "hgrn"
        return [(f"{mixer}_w_in", layer // 2), (f"{mixer}_w_out", layer // 2), ("ffn_w_up", layer), ("ffn_w_down", layer)]

    def gather_of(group):
        names = [n for n, _ in group]
        return names, gather_comm([W[n] for n in names], [(i, li, is_row[n]) for i, (n, li) in enumerate(group)])

    def mixer_weights(layer):
        return layer_weights(layer)[:2] if layer < depth else []

    names_next, comm_next = gather_of(mixer_weights(0))
    filled, _ = run_comm(comm_next, name="gather_mixer0")
    W.update(zip(names_next, filled))
    cw = gather_small(ffn_conv_w, name="gather_conv_w").reshape(depth, 3, 2, Fh).transpose(0, 2, 1, 3)
    cb = ffn_conv_b.reshape(depth, 2, 1, Fh)
    g_mix = norm_mix.reshape(depth, 1, D)
    g_ffn = norm_ffn.reshape(depth, 1, D)

    slopes = jnp.exp2(-8.0 * jnp.arange(1, n_q + 1, dtype=F32) / n_q).reshape(n_kv, Q_PER_KV)
    slope_col = jnp.repeat(slopes, WINDOW, axis=1).reshape(n_kv, Q_PER_KV * WINDOW, 1)

    def to_heads(t, nh):
        return t.reshape(S, nh, HEAD_DIM).transpose(1, 0, 2)

    def from_heads(t):
        return t.transpose(1, 0, 2).reshape(S, -1)

    saved = []
    for layer in range(depth):
        idx = layer // 2
        rec = {"h_mix": h}
        hn, rec["hn_t"] = rmsnorm_fwd(h, g_mix, layer, name=f"norm_mix_fwd{layer}")
        is_attn = layer % 2 == 0
        names_next, comm_next = gather_of(layer_weights(layer)[2:] + ([] if is_attn else mixer_weights(layer + 1)))
        if is_attn:
            proj = mm_nn(hn, W["attn_w_in"], idx, name=f"attn_in_fwd{layer}", out_dtype=BF16)
            q = to_heads(proj[:, :n_q * HEAD_DIM], n_q).reshape(n_kv, Q_PER_KV, S, HEAD_DIM)
            k = to_heads(proj[:, n_q * HEAD_DIM:(n_q + n_kv) * HEAD_DIM], n_kv)
            v = to_heads(proj[:, (n_q + n_kv) * HEAD_DIM:], n_kv)
            sink_col = jnp.repeat(attn_sinks[idx].reshape(n_kv, Q_PER_KV), WINDOW, axis=1)
            sink_col = sink_col.reshape(n_kv, Q_PER_KV * WINDOW, 1)
            o_heads, filled, _ = attn_fwd(q, k, v, sink_col, slope_col, name=f"attn_fwd{layer}", comm=comm_next)
            W.update(zip(names_next, filled))
            o = from_heads(o_heads.reshape(n_q, S, HEAD_DIM))
            rec.update(q=q, k=k, v=v, sink_col=sink_col, o=o)
            h = mm_nn(o, W["attn_w_out"], idx, name=f"attn_out_fwd{layer}", res=h)
        else:
            proj = mm_nn(hn, W["hgrn_w_in"], idx, name=f"hgrn_in_fwd{layer}")
            ng = hgrn_norm[idx].reshape(1, HG_EXPAND)
            o, states, filled, _ = hgrn_fwd(proj, hgrn_lb_logits, ng, layer, name=f"hgrn_fwd{layer}", comm=comm_next)
            W.update(zip(names_next, filled))
            rec.update(proj=proj, ng=ng, states=states, o=o)
            h = mm_nn(o, W["hgrn_w_out"], idx, name=f"hgrn_out_fwd{layer}", res=h)
        rec["h_ffn"] = h
        hn2, rec["hn2_t"] = rmsnorm_fwd(h, g_ffn, layer, name=f"norm_ffn_fwd{layer}")
        if is_attn and layer + 1 < depth:
            names_next, comm_next = gather_of(mixer_weights(layer + 1))
            u, filled = mm_nn(hn2, W["ffn_w_up"], layer, name=f"ffn_up_fwd{layer}", split_out=True, comm=comm_next)
            W.update(zip(names_next, filled))
        else:
            u = mm_nn(hn2, W["ffn_w_up"], layer, name=f"ffn_up_fwd{layer}", split_out=True)
        act, rec["act_t"] = convact_fwd(u, cw, cb, layer, name=f"ffn_act_fwd{layer}")
        rec["u"] = u
        h = mm_nn(act, W["ffn_w_down"], layer, name=f"ffn_down_fwd{layer}", res=h)
        saved.append(rec)

    loss_part, dh, d_norm_final = loss_head(h, norm_final.reshape(1, D), target, name="loss_head")

    n_attn, n_hgrn = attn_w_in.shape[0], hgrn_w_in.shape[0]
    reduced_parts = []
    pending = []

    def swap_of(group, g):
        ordered = sorted(group, key=lambda nl: is_row[nl[0]])
        return ordered, swap_comm([(g[n], is_row[n]) for n, _ in ordered])

    def pair_up(tag, ordered, g, theirs):
        pair, pair_bf16 = {}, {}
        for (n, _), t in zip(ordered, theirs):
            gn = g[n]
            if is_row[n]:
                r = gn.shape[1] // N_CHIPS
                gn = gn.reshape(1, N_CHIPS, 2, r // 2, gn.shape[2])
            pair[n], pair_bf16[n] = pair_add(gn, t, pos, row_sharded=is_row[n], name=f"rs_pair_add_{n}{tag}")
        pending.append((ordered, pair, scatter_comm([(pair_bf16[n], is_row[n]) for n, _ in ordered])))

    def landed_behind(created, cut):
        for (ordered, pair, _), landed in zip(pending, cut(created)):
            reduced_parts.append((ordered, pair, landed))
        pending.clear()

    d_norm_mix, d_norm_ffn = [None] * depth, [None] * depth
    d_conv_w, d_conv_b = [None] * depth, [None] * depth
    d_sinks, d_hgrn_norm = [None] * n_attn, [None] * n_hgrn
    d_logits = jnp.zeros_like(hgrn_lb_logits)
    for layer in reversed(range(depth)):
        idx = layer // 2
        rec = saved[layer]
        g = {"ffn_w_down": mm_dw(rec["act_t"], dh, name=f"ffn_down_dw{layer}")}
        dact = mm_nt(dh, W["ffn_w_down"], layer, name=f"ffn_down_dx{layer}")
        dc, dcw, dcb = convact_bwd(rec["u"], cw, cb, layer, dact, name=f"ffn_act_bwd{layer}")
        d_conv_w[layer] = dcw.transpose(1, 0, 2).reshape(3, 2 * Fh)
        d_conv_b[layer] = dcb.reshape(2 * Fh)
        du = conv_input_grad(dc, cw, layer, name=f"ffn_conv_dx{layer}")
        g["ffn_w_up"] = mm_dw(rec["hn2_t"], du, name=f"ffn_up_dw{layer}", split_b=True)
        ordered, swap = swap_of(layer_weights(layer)[2:], g)
        dhn2, theirs = mm_nt(du, W["ffn_w_up"], layer, name=f"ffn_up_dx{layer}", split_in=True, comm=swap)
        pair_up(layer, ordered, g, theirs)
        dh, dg = rmsnorm_bwd(rec["h_ffn"], g_ffn, layer, dhn2, dh, name=f"norm_ffn_bwd{layer}")
        d_norm_ffn[layer] = dg.reshape(D)
        carried, cut = merge_comms([p[2] for p in pending])
        if layer % 2 == 0:
            g["attn_w_out"] = mm_tn(rec["o"], dh, name=f"attn_out_dw{layer}")
            do = mm_nt(dh, W["attn_w_out"], idx, name=f"attn_out_dx{layer}", out_dtype=BF16)
            do_heads = to_heads(do, n_q).reshape(n_kv, Q_PER_KV, S, HEAD_DIM)
            res = attn_bwd(rec["q"], rec["k"], rec["v"], rec["sink_col"], slope_col, do_heads,
                           name=f"attn_bwd{layer}", comm=carried)
            dq, dk, dv, dsink = res[:4]
            landed_behind(res[-1], cut)
            dproj = jnp.concatenate([from_heads(dq.reshape(n_q, S, HEAD_DIM)), from_heads(dk), from_heads(dv)], axis=1)
            d_sinks[idx] = jnp.sum(dsink.reshape(n_kv, Q_PER_KV, WINDOW), axis=-1).reshape(n_q)
            g["attn_w_in"] = mm_dw(rec["hn_t"], dproj, name=f"attn_in_dw{layer}")
            ordered, swap = swap_of(layer_weights(layer)[:2], g)
            dhn, theirs = mm_nt(dproj, W["attn_w_in"], idx, name=f"attn_in_dx{layer}", comm=swap)
        else:
            g["hgrn_w_out"] = mm_tn(rec["o"], dh, name=f"hgrn_out_dw{layer}")
            do = mm_nt(dh, W["hgrn_w_out"], idx, name=f"hgrn_out_dx{layer}")
            res = hgrn_bwd(rec["proj"], hgrn_lb_logits, rec["ng"], rec["states"], do, layer,
                           name=f"hgrn_bwd{layer}", comm=carried)
            dproj, dlg, dng = res[:3]
            landed_behind(res[-1], cut)
            d_logits = d_logits + dlg
            d_hgrn_norm[idx] = dng.reshape(HG_EXPAND)
            g["hgrn_w_in"] = mm_dw(rec["hn_t"], dproj, name=f"hgrn_in_dw{layer}")
            ordered, swap = swap_of(layer_weights(layer)[:2], g)
            dhn, theirs = mm_nt(dproj, W["hgrn_w_in"], idx, name=f"hgrn_in_dx{layer}", comm=swap)
        pair_up(layer, ordered, g, theirs)
        dh, dg = rmsnorm_bwd(rec["h_mix"], g_mix, layer, dhn, dh, name=f"norm_mix_bwd{layer}")
        d_norm_mix[layer] = dg.reshape(D)
    grad_x = dh.reshape(x.shape)
    carried, cut = merge_comms([p[2] for p in pending])
    landed_behind(run_comm(carried, name="rs_scatter_last")[1], cut)

    small_w = [norm_mix, norm_ffn, norm_final, attn_sinks, hgrn_norm, hgrn_lb_logits, ffn_conv_b]
    small_m = [m_norm_mix, m_norm_ffn, m_norm_final, m_attn_sinks, m_hgrn_norm, m_hgrn_lb_logits, m_ffn_conv_b]
    small_v = [v_norm_mix, v_norm_ffn, v_norm_final, v_attn_sinks, v_hgrn_norm, v_hgrn_lb_logits, v_ffn_conv_b]
    small_grads = [jnp.stack(d_norm_mix), jnp.stack(d_norm_ffn), d_norm_final.reshape(D), jnp.stack(d_sinks),
                   jnp.stack(d_hgrn_norm), d_logits, jnp.stack(d_conv_b)]
    reduced = all_reduce_small(_pack_small(small_grads + [loss_part]), name="all_reduce_small")
    small_g = _unpack_small(reduced, small_w + [loss_part])
    loss = small_g.pop().reshape(())
    pad = [jnp.zeros((1, 1), F32)]
    sd, sm_, sv_ = adamw(_pack_small(small_w + pad), reduced, _pack_small(small_m + pad),
                         _pack_small(small_v + pad), name="adamw_small")
    small_d = _unpack_small(sd, small_w)
    small_nm = _unpack_small(sm_, small_w)
    small_nv = _unpack_small(sv_, small_w)

    big_names = col_names + row_names
    big_w = dict(attn_w_in=attn_w_in, hgrn_w_in=hgrn_w_in, ffn_w_up=ffn_w_up, attn_w_out=attn_w_out,
                 hgrn_w_out=hgrn_w_out, ffn_w_down=ffn_w_down)
    half_shard = dict.fromkeys(big_names)
    for ordered, pair, landed in reduced_parts:
        for (n, li), ld in zip(ordered, landed):
            half_shard[n] = chip_add(pair[n], ld, pos, half_shard[n], li, big_w[n].shape[0], row_sharded=is_row[n],
                                     name=f"rs_chip_add_{n}{li}")
    shards = sibling_join_halves([half_shard[n].reshape(big_w[n].shape) for n in big_names], name="rs_sibling_join")
    big_m = dict(attn_w_in=m_attn_w_in, hgrn_w_in=m_hgrn_w_in, ffn_w_up=m_ffn_w_up, attn_w_out=m_attn_w_out,
                 hgrn_w_out=m_hgrn_w_out, ffn_w_down=m_ffn_w_down)
    big_v = dict(attn_w_in=v_attn_w_in, hgrn_w_in=v_hgrn_w_in, ffn_w_up=v_ffn_w_up, attn_w_out=v_attn_w_out,
                 hgrn_w_out=v_hgrn_w_out, ffn_w_down=v_ffn_w_down)
    grads, deltas, new_m, new_v = {}, {}, {}, {}
    for name_, gshard in zip(big_names, shards):
        w_ = big_w[name_]
        cols = w_.shape[-1]
        d_, m_, v_ = adamw(w_.reshape(-1, cols), gshard.reshape(-1, cols), big_m[name_].reshape(-1, cols),
                           big_v[name_].reshape(-1, cols), name=f"adamw_{name_}")
        grads[name_] = gshard.reshape(w_.shape)
        deltas[name_], new_m[name_], new_v[name_] = (t.reshape(w_.shape) for t in (d_, m_, v_))

    gconv = jnp.stack(d_conv_w)
    n_conv = ffn_conv_w.shape[2]
    gconv_all = _unpack_small(all_reduce_small(_pack_small([gconv]), name="all_reduce_conv_w"), [gconv])[0]
    gconv_mine = lax.dynamic_slice_in_dim(gconv_all, me * n_conv, n_conv, axis=2)
    cshape = ffn_conv_w.shape
    d_, m_, v_ = adamw(ffn_conv_w.reshape(-1, n_conv), gconv_mine.reshape(-1, n_conv),
                       m_ffn_conv_w.reshape(-1, n_conv), v_ffn_conv_w.reshape(-1, n_conv), name="adamw_ffn_conv_w")
    grads["ffn_conv_w"] = gconv_mine
    deltas["ffn_conv_w"], new_m["ffn_conv_w"], new_v["ffn_conv_w"] = (t.reshape(cshape) for t in (d_, m_, v_))

    order = ["norm_mix", "norm_ffn", "norm_final", "attn_w_in", "attn_w_out", "attn_sinks", "hgrn_w_in",
             "hgrn_w_out", "hgrn_norm", "hgrn_lb_logits", "ffn_w_up", "ffn_conv_w", "ffn_conv_b", "ffn_w_down"]
    small_names = ["norm_mix", "norm_ffn", "norm_final", "attn_sinks", "hgrn_norm", "hgrn_lb_logits", "ffn_conv_b"]
    for i, name_ in enumerate(small_names):
        grads[name_], deltas[name_], new_m[name_], new_v[name_] = small_g[i], small_d[i], small_nm[i], small_nv[i]
    return (loss, grad_x, *[grads[n] for n in order], *[deltas[n] for n in order],
            *[new_m[n] for n in order], *[new_v[n] for n in order])
```

```python
import functools
import math

import numpy as np
import jax
import jax.numpy as jnp
from jax import lax
from jax.experimental import pallas as pl
from jax.experimental.pallas import tpu as pltpu

F32 = jnp.float32
BF16 = jnp.bfloat16
MESH = pl.DeviceIdType.MESH

HEAD_DIM = 64
Q_PER_KV = 4
WINDOW = 128
HG_EXPAND = 128
HG_CHUNK = 128
EPS = 1e-6
N_CHIPS = 4
N_DEV = 8

ADAM_LR = 0.001
ADAM_B1 = 0.9
ADAM_B2 = 0.999
ADAM_EPS = 1e-08
ADAM_WD = 0.01
ADAM_STEP = 10

VMEM_LIMIT_BYTES = 48 * 1024 * 1024
LANES = 128
SUBLANES = 8
HALO_ROWS = 16


def _params(sem=None):
    return pltpu.CompilerParams(dimension_semantics=sem, vmem_limit_bytes=VMEM_LIMIT_BYTES)


def _pick(n, cands):
    for c in cands:
        if n % c == 0:
            return c
    return n


def _dot(a, b, dims):
    return lax.dot_general(a.astype(BF16), b.astype(BF16), (dims, ((), ())),
                           preferred_element_type=F32)


NN = ((1,), (0,))
NT = ((1,), (1,))
TN = ((0,), (0,))


def mm_nn(a, w, l, *, name, res=None, out_dtype=F32, split_out=False, comm=None):
    M, K = a.shape
    N = w.shape[2]
    tm = _pick(M, (1024, 512, 256, 128))
    tn = _pick(N // 2 if split_out else N, (1408, 1024, 768, 512, 256, 128))
    tk = _pick(K, (1024, 1408, 512, 256, 128))
    nk = K // tk
    nh = (N // 2) // tn

    def body(*refs):
        if res is None:
            a_ref, w_ref, o_ref = refs[:3]
            r_ref = None
        else:
            a_ref, w_ref, r_ref, o_ref = refs[:4]
        part = _dot(a_ref[...], w_ref[...], NN)

        def finish(acc):
            if r_ref is not None:
                acc = acc + r_ref[...]
            o_ref[...] = acc.astype(out_dtype)

        if nk == 1:
            finish(part)
        else:
            acc_ref = refs[-1]
            k = pl.program_id(2)

            @pl.when(k == 0)
            def _():
                acc_ref[...] = part

            @pl.when(k > 0)
            def _():
                acc_ref[...] += part

            @pl.when(k == nk - 1)
            def _():
                finish(acc_ref[...])

    in_specs = [pl.BlockSpec((tm, tk), lambda n, m, k: (m, k)),
                pl.BlockSpec((None, tk, tn), lambda n, m, k: (l, k, n))]
    args = [a, w]
    if res is not None:
        in_specs.append(pl.BlockSpec((tm, tn), lambda n, m, k: (m, n)))
        args.append(res)
    if split_out:
        out_shape = jax.ShapeDtypeStruct((2, M, N // 2), out_dtype)
        out_spec = pl.BlockSpec((None, tm, tn), lambda n, m, k: (n // nh, m, n % nh))
    else:
        out_shape = jax.ShapeDtypeStruct((M, N), out_dtype)
        out_spec = pl.BlockSpec((tm, tn), lambda n, m, k: (m, n))
    outs, updated, _ = _carried_call(
        body, args, name=name, grid=(N // tn, M // tm, nk), in_specs=in_specs, out_specs=[out_spec],
        out_shape=[out_shape], scratch_shapes=[] if nk == 1 else [pltpu.VMEM((tm, tn), F32)], comm=comm)
    return outs[0] if comm is None else (outs[0], updated)


def mm_nt(a, w, l, *, name, split_in=False, out_dtype=F32, comm=None):
    if split_in:
        _, M, Nh = a.shape
        N = 2 * Nh
    else:
        M, N = a.shape
        Nh = N
    K = w.shape[1]
    tm = _pick(M, (1024, 512, 256, 128))
    tko = _pick(K, (1024, 1408, 512, 256, 128))
    tr = _pick(Nh, (1024, 1408, 768, 512, 256, 128))
    nr = N // tr
    nh = Nh // tr

    def body(a_ref, w_ref, o_ref, *scratch):
        part = _dot(a_ref[...], w_ref[...], NT)
        if nr == 1:
            o_ref[...] = part.astype(out_dtype)
        else:
            acc_ref = scratch[0]
            r = pl.program_id(2)

            @pl.when(r == 0)
            def _():
                acc_ref[...] = part

            @pl.when(r > 0)
            def _():
                acc_ref[...] += part

            @pl.when(r == nr - 1)
            def _():
                o_ref[...] = acc_ref[...].astype(out_dtype)

    if split_in:
        a_spec = pl.BlockSpec((None, tm, tr), lambda ko, m, r: (r // nh, m, r % nh))
    else:
        a_spec = pl.BlockSpec((tm, tr), lambda ko, m, r: (m, r))
    outs, _, created = _carried_call(
        body, [a, w], name=name, grid=(K // tko, M // tm, nr),
        in_specs=[a_spec, pl.BlockSpec((None, tko, tr), lambda ko, m, r: (l, ko, r))],
        out_specs=[pl.BlockSpec((tm, tko), lambda ko, m, r: (m, ko))],
        out_shape=[jax.ShapeDtypeStruct((M, K), out_dtype)],
        scratch_shapes=[] if nr == 1 else [pltpu.VMEM((tm, tko), F32)], comm=comm)
    return outs[0] if comm is None else (outs[0], created)


def mm_tn(a, b, *, name):
    M, K = a.shape
    N = b.shape[1]
    tm = _pick(M, (512, 256, 128))
    tko = _pick(K, (1024, 1408, 704, 512, 256, 128))
    tn = _pick(N, (512, 256, 128))
    nm = M // tm

    def body(a_ref, b_ref, o_ref):
        part = _dot(a_ref[...], b_ref[...], TN)
        r = pl.program_id(2)

        @pl.when(r == 0)
        def _():
            o_ref[...] = part

        @pl.when(r > 0)
        def _():
            o_ref[...] += part

    return pl.pallas_call(
        body, name=name, grid=(K // tko, N // tn, nm),
        in_specs=[pl.BlockSpec((tm, tko), lambda ko, n, r: (r, ko)), pl.BlockSpec((tm, tn), lambda ko, n, r: (r, n))],
        out_specs=pl.BlockSpec((None, tko, tn), lambda ko, n, r: (0, ko, n)),
        out_shape=jax.ShapeDtypeStruct((1, K, N), F32),
        compiler_params=_params(("parallel", "parallel", "arbitrary")),
    )(a, b)


def mm_dw(at, b, *, name, split_b=False):
    K, M = at.shape
    if split_b:
        Nh = b.shape[2]
        N = 2 * Nh
    else:
        N = b.shape[1]
        Nh = N
    tm = _pick(M, (1024, 512, 256, 128))
    tko = _pick(K, (1024, 1408, 704, 512, 256, 128))
    tn = _pick(Nh, (1408, 1024, 768, 512, 256, 128))
    nm = M // tm
    nh = Nh // tn

    def body(a_ref, b_ref, o_ref):
        part = _dot(a_ref[...], b_ref[...], NN)
        r = pl.program_id(2)

        @pl.when(r == 0)
        def _():
            o_ref[...] = part

        @pl.when(r > 0)
        def _():
            o_ref[...] += part

    if split_b:
        b_spec = pl.BlockSpec((None, tm, tn), lambda ko, n, r: (n // nh, r, n % nh))
    else:
        b_spec = pl.BlockSpec((tm, tn), lambda ko, n, r: (r, n))
    return pl.pallas_call(
        body, name=name, grid=(K // tko, N // tn, nm),
        in_specs=[pl.BlockSpec((tko, tm), lambda ko, n, r: (ko, r)), b_spec],
        out_specs=pl.BlockSpec((None, tko, tn), lambda ko, n, r: (0, ko, n)),
        out_shape=jax.ShapeDtypeStruct((1, K, N), F32),
        compiler_params=_params(("parallel", "parallel", "arbitrary")),
    )(at, b)


def _rms(x, g):
    return x * lax.rsqrt(jnp.mean(x * x, axis=-1, keepdims=True) + EPS) * g


def rmsnorm_fwd(h, g, l, *, name):
    S, D = h.shape
    tm = _pick(S, (512, 256, 128))

    def body(h_ref, g_ref, o_ref, ot_ref):
        y = _rms(h_ref[...], g_ref[...])
        o_ref[...] = y.astype(BF16)
        ot_ref[...] = y.T.astype(BF16)

    return pl.pallas_call(
        body, name=name, grid=(S // tm,),
        in_specs=[pl.BlockSpec((tm, D), lambda m: (m, 0)),
                  pl.BlockSpec((None, 1, D), lambda m: (l, 0, 0))],
        out_specs=[pl.BlockSpec((tm, D), lambda m: (m, 0)), pl.BlockSpec((D, tm), lambda m: (0, m))],
        out_shape=[jax.ShapeDtypeStruct((S, D), BF16), jax.ShapeDtypeStruct((D, S), BF16)],
        compiler_params=_params(("parallel",)),
    )(h, g)


def rmsnorm_bwd(h, g, l, dhn, dres, *, name):
    S, D = h.shape
    tm = _pick(S, (512, 256, 128))

    def body(h_ref, g_ref, dhn_ref, dres_ref, dh_ref, dg_ref):
        _, vjp = jax.vjp(_rms, h_ref[...], g_ref[...])
        dx, dg = vjp(dhn_ref[...])
        dh_ref[...] = dres_ref[...] + dx

        @pl.when(pl.program_id(0) == 0)
        def _():
            dg_ref[...] = dg

        @pl.when(pl.program_id(0) > 0)
        def _():
            dg_ref[...] += dg

    row = pl.BlockSpec((tm, D), lambda m: (m, 0))
    return pl.pallas_call(
        body, name=name, grid=(S // tm,),
        in_specs=[row, pl.BlockSpec((None, 1, D), lambda m: (l, 0, 0)), row, row],
        out_specs=[row, pl.BlockSpec((1, D), lambda m: (0, 0))],
        out_shape=[jax.ShapeDtypeStruct((S, D), F32), jax.ShapeDtypeStruct((1, D), F32)],
        compiler_params=_params(("arbitrary",)),
    )(h, g, dhn, dres)


def loss_head(h, g, target, *, name):
    S, D = h.shape
    tm = _pick(S, (512, 256, 128))

    def body(h_ref, g_ref, t_ref, loss_ref, dh_ref, dg_ref):
        y, vjp = jax.vjp(_rms, h_ref[...], g_ref[...])
        err = y - t_ref[...]
        part = 0.5 * jnp.sum(jnp.sum(err * err, axis=-1, keepdims=True) / D, axis=0, keepdims=True)
        dx, dg = vjp(err / D)
        dh_ref[...] = dx

        @pl.when(pl.program_id(0) == 0)
        def _():
            dg_ref[...] = dg
            loss_ref[...] = part

        @pl.when(pl.program_id(0) > 0)
        def _():
            dg_ref[...] += dg
            loss_ref[...] += part

    row = pl.BlockSpec((tm, D), lambda m: (m, 0))
    return pl.pallas_call(
        body, name=name, grid=(S // tm,),
        in_specs=[row, pl.BlockSpec((1, D), lambda m: (0, 0)), row],
        out_specs=[pl.BlockSpec((1, 1), lambda m: (0, 0)), row, pl.BlockSpec((1, D), lambda m: (0, 0))],
        out_shape=[jax.ShapeDtypeStruct((1, 1), F32), jax.ShapeDtypeStruct((S, D), F32),
                   jax.ShapeDtypeStruct((1, D), F32)],
        compiler_params=_params(("arbitrary",)),
    )(h, g, target)


def _make_mm(dims_fwd, dims_da, dims_db, swap_da=False, swap_db=False):
    @jax.custom_vjp
    def mm(a, b):
        return _dot(a, b, dims_fwd)

    def fwd(a, b):
        return mm(a, b), (a, b)

    def bwd(resid, g):
        a, b = resid
        da = _dot(b, g, dims_da) if swap_da else _dot(g, b, dims_da)
        db = _dot(g, a, dims_db) if swap_db else _dot(a, g, dims_db)
        return da, db

    mm.defvjp(fwd, bwd)
    return mm


_mm = _make_mm(NN, NT, TN)
_mm_nt = _make_mm(NT, NN, TN, swap_db=True)
_mm_tn = _make_mm(TN, NT, NN, swap_da=True)


def _dot_hi(a, b):
    return jnp.dot(a, b, precision=lax.Precision.HIGHEST, preferred_element_type=F32)


def _sigmoid(x):
    return 0.5 * jnp.tanh(0.5 * x) + 0.5


def _silu(x):
    return x * _sigmoid(x)


def _attn_block(q, kw, vw, sink, slope, first):
    rows = q.shape[0]
    s = _mm_nt(q, kw) * (HEAD_DIM ** -0.5)
    qi = lax.broadcasted_iota(jnp.int32, (rows, 2 * WINDOW), 0) % WINDOW
    ki = lax.broadcasted_iota(jnp.int32, (rows, 2 * WINDOW), 1)
    dist = qi + WINDOW - ki
    valid = (dist >= 0) & (dist < WINDOW) & (ki >= WINDOW * first)
    s = jnp.where(valid, s - slope * dist.astype(F32), -jnp.inf)
    m = lax.stop_gradient(jnp.maximum(jnp.max(s, axis=-1, keepdims=True), sink))
    e = jnp.exp(s - m)
    denom = jnp.sum(e, axis=-1, keepdims=True) + jnp.exp(sink - m)
    return _mm(e * (1.0 / denom), vw)


def _attn_specs(Hkv, order):
    G = Q_PER_KV
    qspec = pl.BlockSpec((Hkv, G, WINDOW, HEAD_DIM), lambda n: (0, 0, order(n), 0))
    prev = pl.BlockSpec((Hkv, WINDOW, HEAD_DIM), lambda n: (0, jnp.maximum(order(n) - 1, 0), 0))
    cur = pl.BlockSpec((Hkv, WINDOW, HEAD_DIM), lambda n: (0, order(n), 0))
    col = pl.BlockSpec((Hkv, G * WINDOW, 1), lambda n: (0, 0, 0))
    return qspec, prev, cur, col


def attn_fwd(q, k, v, sink_col, slope_col, *, name, comm=None):
    Hkv, G, S, _ = q.shape
    qspec, prev, cur, col = _attn_specs(Hkv, lambda n: n)

    def body(q_ref, kp_ref, kc_ref, vp_ref, vc_ref, sink_ref, slope_ref, o_ref):
        first = (pl.program_id(0) == 0).astype(jnp.int32)
        outs = []
        for h in range(Hkv):
            kw = jnp.concatenate([kp_ref[h], kc_ref[h]], axis=0)
            vw = jnp.concatenate([vp_ref[h], vc_ref[h]], axis=0)
            o = _attn_block(q_ref[h].reshape(G * WINDOW, HEAD_DIM), kw, vw, sink_ref[h], slope_ref[h], first)
            outs.append(o.reshape(G, WINDOW, HEAD_DIM).astype(BF16))
        o_ref[...] = jnp.stack(outs)

    outs, updated, created = _carried_call(
        body, [q, k, k, v, v, sink_col, slope_col], name=name, grid=(S // WINDOW,),
        in_specs=[qspec, prev, cur, prev, cur, col, col], out_specs=[qspec],
        out_shape=[jax.ShapeDtypeStruct(q.shape, BF16)], comm=comm)
    return outs[0] if comm is None else (outs[0], updated, created)


def attn_bwd(q, k, v, sink_col, slope_col, do, *, name, comm=None):
    Hkv, G, S, _ = q.shape
    nb = S // WINDOW
    qspec, prev, cur, col = _attn_specs(Hkv, lambda n: nb - 1 - n)

    def body(q_ref, kp_ref, kc_ref, vp_ref, vc_ref, sink_ref, slope_ref, do_ref,
             dq_ref, dk_ref, dv_ref, dsink_ref, dk_carry, dv_carry):
        @pl.when(pl.program_id(0) == 0)
        def _():
            dk_carry[...] = jnp.zeros_like(dk_carry)
            dv_carry[...] = jnp.zeros_like(dv_carry)
            dsink_ref[...] = jnp.zeros_like(dsink_ref)

        first = (pl.program_id(0) == nb - 1).astype(jnp.int32)
        dqs, dkws, dvws, dsinks = [], [], [], []
        for h in range(Hkv):
            kw = jnp.concatenate([kp_ref[h], kc_ref[h]], axis=0).astype(F32)
            vw = jnp.concatenate([vp_ref[h], vc_ref[h]], axis=0).astype(F32)
            fn = functools.partial(_attn_block, slope=slope_ref[h], first=first)
            _, vjp = jax.vjp(fn, q_ref[h].reshape(G * WINDOW, HEAD_DIM).astype(F32), kw, vw, sink_ref[h])
            dq, dkw, dvw, dsink = vjp(do_ref[h].reshape(G * WINDOW, HEAD_DIM).astype(F32))
            dqs.append(dq.reshape(G, WINDOW, HEAD_DIM))
            dkws.append(dkw)
            dvws.append(dvw)
            dsinks.append(dsink)
        dkw, dvw = jnp.stack(dkws), jnp.stack(dvws)
        dq_ref[...] = jnp.stack(dqs).astype(BF16)
        dk_ref[...] = (dkw[:, WINDOW:] + dk_carry[...]).astype(BF16)
        dv_ref[...] = (dvw[:, WINDOW:] + dv_carry[...]).astype(BF16)
        dk_carry[...] = dkw[:, :WINDOW]
        dv_carry[...] = dvw[:, :WINDOW]
        dsink_ref[...] += jnp.stack(dsinks)

    kv_shape = jax.ShapeDtypeStruct(k.shape, BF16)
    outs, _, created = _carried_call(
        body, [q, k, k, v, v, sink_col, slope_col, do], name=name, grid=(nb,),
        in_specs=[qspec, prev, cur, prev, cur, col, col, qspec],
        out_specs=[qspec, cur, cur, col],
        out_shape=[jax.ShapeDtypeStruct(q.shape, BF16), kv_shape, kv_shape,
                   jax.ShapeDtypeStruct(sink_col.shape, F32)],
        scratch_shapes=[pltpu.VMEM((Hkv, WINDOW, HEAD_DIM), F32), pltpu.VMEM((Hkv, WINDOW, HEAD_DIM), F32)],
        comm=comm)
    return outs if comm is None else (*outs, created)


def _hgrn_consts():
    C = HG_CHUNK
    t = np.arange(C)[:, None]
    j = np.arange(C)[None, :]
    cq, ck, mk = [], [], []
    H = C // 2
    while H >= 1:
        start = (t // H) * H
        cq.append((j > start) & (j <= t))
        ck.append((j > t) & (j <= start + H))
        mk.append(((t // H) % 2 == 1) & ((j // H) == (t // H) - 1))
        H //= 2
    ltri = (j <= t)
    ops = np.concatenate([ltri] + cq + ck, axis=0).astype(np.float32)
    ops3 = np.concatenate([ops, ops, ops], axis=1)
    opst3 = np.concatenate([ops.T, ops.T, ops.T], axis=1)
    return (jnp.asarray(ops3, BF16), jnp.asarray(opst3, BF16), jnp.asarray(np.stack(mk).astype(np.float32)))


def _split3(x):
    hi = x.astype(BF16)
    r = x - hi.astype(F32)
    mid = r.astype(BF16)
    lo = (r - mid.astype(F32)).astype(BF16)
    return jnp.concatenate([hi, mid, lo], axis=0)


@jax.custom_vjp
def _seg_sums(g, ops3, opst3):
    C = g.shape[0]
    out = jnp.dot(ops3, _split3(g), preferred_element_type=F32)
    return tuple(out[i * C:(i + 1) * C] for i in range(ops3.shape[0] // C))


def _seg_sums_fwd(g, ops3, opst3):
    return _seg_sums(g, ops3, opst3), opst3


def _seg_sums_bwd(opst3, cts):
    ct = jnp.concatenate(cts, axis=0)
    return jnp.dot(opst3, _split3(ct), preferred_element_type=F32), None, None


_seg_sums.defvjp(_seg_sums_fwd, _seg_sums_bwd)


def _hgrn_head(qr, fr, iv, gr, st, logits, ng, ops3, opst3, mk, layer):
    n_levels = mk.shape[0]
    depth = logits.shape[0]
    e = jnp.exp(logits - lax.stop_gradient(jnp.max(logits, axis=0, keepdims=True)))
    sm = e / jnp.sum(e, axis=0, keepdims=True)
    row = lax.broadcasted_iota(jnp.int32, (depth, HG_EXPAND), 0)
    lb = jnp.sum(jnp.where((row >= 1) & (row <= layer), sm, 0.0), axis=0, keepdims=True)

    q = _silu(qr)
    a = jnp.log(lb)
    c = jnp.log(1.0 - lb) - (jnp.maximum(-fr, 0.0) + jnp.log(1.0 + jnp.exp(-jnp.abs(fr))))
    mx = lax.stop_gradient(jnp.maximum(a, c))
    g = mx + jnp.log(jnp.exp(a - mx) + jnp.exp(c - mx))
    k = (1.0 - lb) * _sigmoid(-fr)

    sums = _seg_sums(g, ops3, opst3)
    b = sums[0]
    b_last = jnp.sum(g, axis=0, keepdims=True)

    o = jnp.sum(q * k, axis=-1, keepdims=True) * iv
    amat = None
    for lv in range(n_levels):
        eq = jnp.exp(sums[1 + lv])
        ek = jnp.exp(sums[1 + n_levels + lv])
        part = _mm_nt(q * eq, k * ek) * mk[lv]
        amat = part if amat is None else amat + part
    o = o + _mm(amat, iv) + _mm_nt(q * jnp.exp(b), st)
    st_new = st * jnp.exp(b_last) + _mm_tn(iv, k * jnp.exp(b_last - b))

    on = o * lax.rsqrt(jnp.mean(o * o, axis=-1, keepdims=True) + EPS) * ng
    return on * _silu(gr), st_new


def _hgrn_in_specs(D, depth, order):
    C = HG_CHUNK
    nl = int(math.log2(C))
    n_ops = 1 + 2 * nl
    return [pl.BlockSpec((C, 4 * D), lambda c: (order(c), 0)),
            pl.BlockSpec((depth, D), lambda c: (0, 0)),
            pl.BlockSpec((1, HG_EXPAND), lambda c: (0, 0)),
            pl.BlockSpec((n_ops * C, 3 * C), lambda c: (0, 0)),
            pl.BlockSpec((C, 3 * n_ops * C), lambda c: (0, 0)),
            pl.BlockSpec((nl, C, C), lambda c: (0, 0, 0))]


def _hgrn_cols(D, h):
    return [slice(j * D + h * HG_EXPAND, j * D + (h + 1) * HG_EXPAND) for j in range(4)]


def hgrn_fwd(proj, logits, ng, layer, *, name, comm=None):
    S = proj.shape[0]
    D = proj.shape[1] // 4
    H = D // HG_EXPAND
    C = HG_CHUNK
    nc = S // C
    consts = _hgrn_consts()

    def body(p_ref, lg_ref, ng_ref, ops3_ref, opst3_ref, mk_ref, o_ref, st_out_ref, st_ref):
        @pl.when(pl.program_id(0) == 0)
        def _():
            st_ref[...] = jnp.zeros_like(st_ref)

        st_out_ref[...] = st_ref[...]
        ops3, opst3, mk = ops3_ref[...], opst3_ref[...], mk_ref[...]
        for h in range(H):
            sl = slice(h * HG_EXPAND, (h + 1) * HG_EXPAND)
            cols = _hgrn_cols(D, h)
            out, st_new = _hgrn_head(p_ref[:, cols[0]], p_ref[:, cols[1]], p_ref[:, cols[2]], p_ref[:, cols[3]],
                                     st_ref[h], lg_ref[:, sl], ng_ref[...], ops3, opst3, mk, layer)
            o_ref[:, sl] = out.astype(BF16)
            st_ref[h] = st_new

    outs, updated, created = _carried_call(
        body, [proj, logits, ng, *consts], name=name, grid=(nc,),
        in_specs=_hgrn_in_specs(D, logits.shape[0], lambda c: c),
        out_specs=[pl.BlockSpec((C, D), lambda c: (c, 0)),
                   pl.BlockSpec((None, H, HG_EXPAND, HG_EXPAND), lambda c: (c, 0, 0, 0))],
        out_shape=[jax.ShapeDtypeStruct((S, D), BF16),
                   jax.ShapeDtypeStruct((nc, H, HG_EXPAND, HG_EXPAND), F32)],
        scratch_shapes=[pltpu.VMEM((H, HG_EXPAND, HG_EXPAND), F32)], comm=comm)
    return outs if comm is None else (*outs, updated, created)


def hgrn_bwd(proj, logits, ng, states, dout, layer, *, name, comm=None):
    S = proj.shape[0]
    D = proj.shape[1] // 4
    H = D // HG_EXPAND
    C = HG_CHUNK
    nc = S // C
    depth = logits.shape[0]
    consts = _hgrn_consts()
    rev = lambda c: nc - 1 - c

    def body(p_ref, lg_ref, ng_ref, ops3_ref, opst3_ref, mk_ref, st_in_ref, do_ref,
             dp_ref, dlg_ref, dng_ref, dst_ref):
        @pl.when(pl.program_id(0) == 0)
        def _():
            dst_ref[...] = jnp.zeros_like(dst_ref)
            dlg_ref[...] = jnp.zeros_like(dlg_ref)
            dng_ref[...] = jnp.zeros_like(dng_ref)

        ops3, opst3, mk = ops3_ref[...], opst3_ref[...], mk_ref[...]
        for h in range(H):
            sl = slice(h * HG_EXPAND, (h + 1) * HG_EXPAND)
            cols = _hgrn_cols(D, h)
            fn = functools.partial(_hgrn_head, ops3=ops3, opst3=opst3, mk=mk, layer=layer)
            _, vjp = jax.vjp(fn, p_ref[:, cols[0]], p_ref[:, cols[1]], p_ref[:, cols[2]], p_ref[:, cols[3]],
                             st_in_ref[h], lg_ref[:, sl], ng_ref[...])
            dq, df, di, dg, dst, dlg, dng = vjp((do_ref[:, sl], dst_ref[h]))
            dp_ref[:, cols[0]] = dq.astype(BF16)
            dp_ref[:, cols[1]] = df.astype(BF16)
            dp_ref[:, cols[2]] = di.astype(BF16)
            dp_ref[:, cols[3]] = dg.astype(BF16)
            dst_ref[h] = dst
            dlg_ref[:, sl] += dlg
            dng_ref[...] += dng

    in_specs = _hgrn_in_specs(D, depth, rev)
    in_specs += [pl.BlockSpec((None, H, HG_EXPAND, HG_EXPAND), lambda c: (rev(c), 0, 0, 0)),
                 pl.BlockSpec((C, D), lambda c: (rev(c), 0))]
    outs, _, created = _carried_call(
        body, [proj, logits, ng, *consts, states, dout], name=name, grid=(nc,), in_specs=in_specs,
        out_specs=[pl.BlockSpec((C, 4 * D), lambda c: (rev(c), 0)),
                   pl.BlockSpec((depth, D), lambda c: (0, 0)), pl.BlockSpec((1, HG_EXPAND), lambda c: (0, 0))],
        out_shape=[jax.ShapeDtypeStruct((S, 4 * D), BF16), jax.ShapeDtypeStruct((depth, D), F32),
                   jax.ShapeDtypeStruct((1, HG_EXPAND), F32)],
        scratch_shapes=[pltpu.VMEM((H, HG_EXPAND, HG_EXPAND), F32)], comm=comm)
    return outs if comm is None else (*outs, created)


def _shift_down(u, prev, shift):
    rolled = pltpu.roll(u, shift, axis=0)
    rows = lax.broadcasted_iota(jnp.int32, prev.shape, 0)
    top = jnp.where(rows < shift, pltpu.roll(prev, shift, axis=0), rolled[:SUBLANES])
    return jnp.concatenate([top, rolled[SUBLANES:]], axis=0)


def _shift_up(x, nxt, shift):
    tm = x.shape[0]
    rolled = pltpu.roll(x, tm - shift, axis=0)
    rows = lax.broadcasted_iota(jnp.int32, nxt.shape, 0)
    bottom = jnp.where(rows >= SUBLANES - shift, pltpu.roll(nxt, SUBLANES - shift, axis=0), rolled[tm - SUBLANES:])
    return jnp.concatenate([rolled[:tm - SUBLANES], bottom], axis=0)


def _conv(u, prev, w_ref, b_ref, half):
    u1 = _shift_down(u, prev, 1)
    u2 = _shift_down(u, prev, 2)
    c = b_ref[half] + w_ref[half, 0:1, :] * u2 + w_ref[half, 1:2, :] * u1 + w_ref[half, 2:3, :] * u
    return c, u1, u2


def _halo_rows(ref, keep, last):
    full = ref.astype(F32)
    return (full[HALO_ROWS - SUBLANES:] if last else full[:SUBLANES]) * keep


def _conv_specs(tm, tc, l):
    nprev = lambda m: jnp.maximum(m * (tm // HALO_ROWS) - 1, 0)
    tile = pl.BlockSpec((2, tm, tc), lambda j, m: (0, m, j))
    halo = pl.BlockSpec((2, HALO_ROWS, tc), lambda j, m: (0, nprev(m), j))
    wspec = pl.BlockSpec((None, 2, 3, tc), lambda j, m: (l, 0, 0, j))
    bspec = pl.BlockSpec((None, 2, 1, tc), lambda j, m: (l, 0, 0, j))
    return tile, halo, wspec, bspec


def convact_fwd(u, cw, cb, l, *, name):
    _, S, Fh = u.shape
    tm = _pick(S, (256, 128))
    tc = _pick(Fh, (1408, 512, 256, 128))
    tile, halo, wspec, bspec = _conv_specs(tm, tc, l)

    def body(u_ref, p_ref, w_ref, b_ref, o_ref, ot_ref):
        keep = (pl.program_id(1) > 0).astype(F32)
        cg, _, _ = _conv(u_ref[0].astype(F32), _halo_rows(p_ref[0], keep, True), w_ref, b_ref, 0)
        cv, _, _ = _conv(u_ref[1].astype(F32), _halo_rows(p_ref[1], keep, True), w_ref, b_ref, 1)
        act = _silu(cg) * cv
        o_ref[...] = act.astype(BF16)
        ot_ref[...] = act.T.astype(BF16)

    return pl.pallas_call(
        body, name=name, grid=(Fh // tc, S // tm), in_specs=[tile, halo, wspec, bspec],
        out_specs=[pl.BlockSpec((tm, tc), lambda j, m: (m, j)), pl.BlockSpec((tc, tm), lambda j, m: (j, m))],
        out_shape=[jax.ShapeDtypeStruct((S, Fh), BF16), jax.ShapeDtypeStruct((Fh, S), BF16)],
        compiler_params=_params(("parallel", "parallel")),
    )(u, u, cw, cb)


def convact_bwd(u, cw, cb, l, da, *, name):
    _, S, Fh = u.shape
    tm = _pick(S, (256, 128))
    tc = _pick(Fh, (1408, 512, 256, 128))
    tile, halo, wspec, bspec = _conv_specs(tm, tc, l)

    def body(u_ref, p_ref, w_ref, b_ref, da_ref, dc_ref, dw_ref, db_ref):
        first = pl.program_id(1) == 0
        keep = jnp.logical_not(first).astype(F32)
        ug, uv = u_ref[0].astype(F32), u_ref[1].astype(F32)
        cg, g1, g2 = _conv(ug, _halo_rows(p_ref[0], keep, True), w_ref, b_ref, 0)
        cv, v1, v2 = _conv(uv, _halo_rows(p_ref[1], keep, True), w_ref, b_ref, 1)
        da_t = da_ref[...]
        sg = _sigmoid(cg)
        dcv = da_t * (cg * sg)
        dcg = da_t * cv * (sg * (1.0 + cg * (1.0 - sg)))
        dc_ref[0] = dcg.astype(BF16)
        dc_ref[1] = dcv.astype(BF16)

        @pl.when(first)
        def _():
            dw_ref[...] = jnp.zeros_like(dw_ref)
            db_ref[...] = jnp.zeros_like(db_ref)

        rs = lambda t: jnp.sum(t, axis=0, keepdims=True)
        for half, dc, taps in ((0, dcg, (g2, g1, ug)), (1, dcv, (v2, v1, uv))):
            for j in range(3):
                dw_ref[half, j:j + 1, :] += rs(dc * taps[j])
            db_ref[half] += rs(dc)

    return pl.pallas_call(
        body, name=name, grid=(Fh // tc, S // tm),
        in_specs=[tile, halo, wspec, bspec, pl.BlockSpec((tm, tc), lambda j, m: (m, j))],
        out_specs=[tile, pl.BlockSpec((2, 3, tc), lambda j, m: (0, 0, j)),
                   pl.BlockSpec((2, 1, tc), lambda j, m: (0, 0, j))],
        out_shape=[jax.ShapeDtypeStruct(u.shape, BF16), jax.ShapeDtypeStruct((2, 3, Fh), F32),
                   jax.ShapeDtypeStruct((2, 1, Fh), F32)],
        compiler_params=_params(("parallel", "arbitrary")),
    )(u, u, cw, cb, da)


def conv_input_grad(dc, cw, l, *, name):
    _, S, Fh = dc.shape
    tm = _pick(S, (256, 128))
    tc = _pick(Fh, (1408, 512, 256, 128))
    nblk = S // HALO_ROWS
    nm = S // tm

    def body(dc_ref, n_ref, w_ref, du_ref):
        keep = (pl.program_id(1) < nm - 1).astype(F32)
        for half in range(2):
            t = dc_ref[half].astype(F32)
            nxt = _halo_rows(n_ref[half], keep, False)
            du = (w_ref[half, 2:3, :] * t + w_ref[half, 1:2, :] * _shift_up(t, nxt, 1)
                  + w_ref[half, 0:1, :] * _shift_up(t, nxt, 2))
            du_ref[half] = du.astype(BF16)

    tile = pl.BlockSpec((2, tm, tc), lambda j, m: (0, m, j))
    halo = pl.BlockSpec((2, HALO_ROWS, tc),
                        lambda j, m: (0, jnp.minimum((m + 1) * (tm // HALO_ROWS), nblk - 1), j))
    return pl.pallas_call(
        body, name=name, grid=(Fh // tc, nm),
        in_specs=[tile, halo, pl.BlockSpec((None, 2, 3, tc), lambda j, m: (l, 0, 0, j))],
        out_specs=tile, out_shape=jax.ShapeDtypeStruct(dc.shape, BF16),
        compiler_params=_params(("parallel", "parallel")),
    )(dc, dc, cw)


def _adamw_math(w, g, m, v):
    m = ADAM_B1 * m + (1.0 - ADAM_B1) * g
    v = ADAM_B2 * v + (1.0 - ADAM_B2) * (g * g)
    m_hat = m / (1.0 - ADAM_B1 ** ADAM_STEP)
    v_hat = v / (1.0 - ADAM_B2 ** ADAM_STEP)
    delta = -ADAM_LR * (m_hat / (jnp.sqrt(v_hat) + ADAM_EPS) + ADAM_WD * w)
    return delta, m, v


def _row_tile(R, Ccols, n_streams):
    for tr in (512, 256, 128, 64, 32, 16, 8):
        if R % tr == 0 and tr * Ccols * 4 * n_streams * 2 <= VMEM_LIMIT_BYTES // 2:
            return tr
    return R


def adamw(w, g, m, v, *, name):
    R, Ccols = w.shape
    tr = _row_tile(R, Ccols, 7)

    def body(w_ref, g_ref, m_ref, v_ref, d_ref, mo_ref, vo_ref):
        d, mn, vn = _adamw_math(w_ref[...], g_ref[...], m_ref[...], v_ref[...])
        d_ref[...] = d
        mo_ref[...] = mn
        vo_ref[...] = vn

    spec = pl.BlockSpec((tr, Ccols), lambda i: (i, 0))
    shp = jax.ShapeDtypeStruct((R, Ccols), F32)
    return pl.pallas_call(
        body, name=name, grid=(R // tr,), in_specs=[spec] * 4, out_specs=[spec] * 3,
        out_shape=[shp, shp, shp], compiler_params=_params(("parallel",)),
    )(w, g, m, v)


def _prefetch_call(body, pos, args, *, name, grid, in_specs, out_specs, out_shape, sem, aliases=None):
    return pl.pallas_call(
        lambda pos_ref, *refs: body(*refs), name=name,
        grid_spec=pltpu.PrefetchScalarGridSpec(num_scalar_prefetch=1, grid=grid, in_specs=in_specs,
                                               out_specs=out_specs),
        out_shape=out_shape, input_output_aliases={1 + i: o for i, o in (aliases or {}).items()},
        compiler_params=_params(sem),
    )(pos, *args)


def cast_place(w, pos, *, row_sharded, name):
    L, R, Ccols = w.shape

    def body(w_ref, o_ref):
        o_ref[...] = w_ref[...].astype(BF16)

    if row_sharded:
        grid = (L,)
        in_spec = pl.BlockSpec((None, R, Ccols), lambda l, pos: (l, 0, 0))
        out_spec = pl.BlockSpec((None, R, Ccols), lambda l, pos: (l, pos[0], 0))
        out_shape = jax.ShapeDtypeStruct((L, N_CHIPS * R, Ccols), BF16)
        sem = ("parallel",)
    else:
        tr = _pick(R, (512, 256, 128))
        grid = (L, R // tr)
        in_spec = pl.BlockSpec((None, tr, Ccols), lambda l, i, pos: (l, i, 0))
        out_spec = pl.BlockSpec((None, tr, Ccols), lambda l, i, pos: (l, i, pos[0]))
        out_shape = jax.ShapeDtypeStruct((L, R, N_CHIPS * Ccols), BF16)
        sem = ("parallel", "parallel")
    return _prefetch_call(body, pos, [w], name=name, grid=grid, in_specs=[in_spec], out_specs=out_spec,
                          out_shape=out_shape, sem=sem)


def pair_add(g, theirs, pos, *, row_sharded, name):
    def body(g_ref, t_ref, o_ref, ob_ref):
        s = g_ref[...] + t_ref[...]
        o_ref[...] = s
        ob_ref[...] = s.astype(BF16)

    if row_sharded:
        L, _, _, r2, D = g.shape
        grid = (L, N_CHIPS)
        g_spec = pl.BlockSpec((None, None, None, r2, D), lambda l, q, pos: (l, q, pos[1], 0, 0))
        t_spec = pl.BlockSpec((None, None, r2, D), lambda l, q, pos: (l, q, 0, 0))
    else:
        L, K, N = g.shape
        n = N // N_CHIPS
        tr = _pick(K // 2, (256, 128))
        nb = (K // 2) // tr
        grid = (L, nb, N_CHIPS)
        g_spec = pl.BlockSpec((None, tr, n), lambda l, i, j, pos: (l, pos[1] * nb + i, j))
        t_spec = pl.BlockSpec((None, tr, n), lambda l, i, j, pos: (l, i, j))
    return _prefetch_call(body, pos, [g, theirs], name=name, grid=grid, in_specs=[g_spec, t_spec],
                          out_specs=[t_spec, t_spec],
                          out_shape=[jax.ShapeDtypeStruct(theirs.shape, F32), jax.ShapeDtypeStruct(theirs.shape, BF16)],
                          sem=("parallel",) * len(grid))


def chip_add(pair, landed, pos, buf, li, n_layers, *, row_sharded, name):
    def body(p_ref, a_ref, b_ref, c_ref, *rest):
        rest[-1][...] = ((p_ref[...] + a_ref[...].astype(F32)) + b_ref[...].astype(F32)) + c_ref[...].astype(F32)

    if row_sharded:
        _, _, r2, D = pair.shape
        grid = (1,)
        p_spec = pl.BlockSpec((None, None, r2, D), lambda i, pos: (0, pos[0], 0, 0))
        l_specs = [pl.BlockSpec((None, r2, D), functools.partial(lambda i, pos, k: (k, 0, 0), k=k)) for k in range(3)]
        out_spec = pl.BlockSpec((None, None, r2, D), lambda i, pos: (li, pos[1], 0, 0))
        out_shape = jax.ShapeDtypeStruct((n_layers, 2, r2, D), F32)
    else:
        _, K2, N = pair.shape
        n = N // N_CHIPS
        tr = _pick(K2, (256, 128))
        nb = K2 // tr
        grid = (nb,)
        p_spec = pl.BlockSpec((None, tr, n), lambda i, pos: (0, i, pos[0]))
        l_specs = [pl.BlockSpec((None, tr, n), functools.partial(lambda i, pos, k: (k, i, 0), k=k)) for k in range(3)]
        out_spec = pl.BlockSpec((None, tr, n), lambda i, pos: (li, pos[1] * nb + i, 0))
        out_shape = jax.ShapeDtypeStruct((n_layers, 2 * K2, n), F32)
    args, in_specs, aliases = [pair, landed, landed, landed], [p_spec] + l_specs, None
    if buf is not None:
        args.append(buf)
        in_specs.append(pl.BlockSpec(memory_space=pl.ANY))
        aliases = {4: 0}
    return _prefetch_call(body, pos, args, name=name, grid=grid, in_specs=in_specs, out_specs=out_spec,
                          out_shape=out_shape, sem=("parallel",), aliases=aliases)


def _position():
    return lax.axis_index("x"), lax.axis_index("y"), lax.axis_index("c")


def _other_chips(x, y):
    return [(1 - x, y), (x, 1 - y), (1 - x, 1 - y)]


HBM_ANY = pl.BlockSpec(memory_space=pl.ANY)
COMM_PARAMS = pltpu.CompilerParams(has_side_effects=True)


class Comm:
    def __init__(self, ro=(), inout=(), new=(), sems=(), phases=()):
        self.ro, self.inout, self.new, self.sems, self.phases = list(ro), list(inout), list(new), list(sems), list(phases)


def _carried_call(body, args, *, name, grid, in_specs, out_specs, out_shape, scratch_shapes=(), comm=None):
    comm = comm or Comm()
    n_in, n_out, n_sc = len(in_specs), len(out_shape), len(scratch_shapes)
    n_ro, n_io, n_new, n_sem = len(comm.ro), len(comm.inout), len(comm.new), len(comm.sems)
    last = int(np.prod(grid)) - 1

    def wrapped(*refs):
        bounds = np.cumsum([0, n_in, n_ro, n_io, n_out, n_io, n_new, n_sc, n_sem])
        core_in, ro, _, core_out, io, new, core_sc, sems = (refs[a:b] for a, b in zip(bounds[:-1], bounds[1:]))
        step = pl.program_id(0)
        for d in range(1, len(grid)):
            step = step * grid[d] + pl.program_id(d)
        for frac, fn in comm.phases[:-1]:
            pl.when(step == int(frac * last))(functools.partial(fn, ro, io, new, sems))
        body(*core_in, *core_out, *core_sc)
        if comm.phases:
            pl.when(step == last)(functools.partial(comm.phases[-1][1], ro, io, new, sems))

    outs = pl.pallas_call(
        wrapped, name=name, grid=grid,
        in_specs=list(in_specs) + [HBM_ANY] * (n_ro + n_io),
        out_specs=list(out_specs) + [HBM_ANY] * (n_io + n_new),
        out_shape=list(out_shape) + [jax.ShapeDtypeStruct(a.shape, a.dtype) for a in comm.inout] + comm.new,
        input_output_aliases={n_in + n_ro + j: n_out + j for j in range(n_io)},
        scratch_shapes=list(scratch_shapes) + comm.sems,
        compiler_params=pltpu.CompilerParams(dimension_semantics=("arbitrary",) * len(grid),
                                             vmem_limit_bytes=VMEM_LIMIT_BYTES, has_side_effects=bool(comm.phases)),
    )(*args, *comm.ro, *comm.inout)
    return outs[:n_out], outs[n_out:n_out + n_io], outs[n_out + n_io:]


def run_comm(comm, *, name):
    n_ro, n_io, n_new = len(comm.ro), len(comm.inout), len(comm.new)

    def body(*refs):
        bounds = np.cumsum([0, n_ro, n_io, n_io, n_new, len(comm.sems)])
        ro, _, io, new, sems = (refs[a:b] for a, b in zip(bounds[:-1], bounds[1:]))
        for _, fn in comm.phases:
            fn(ro, io, new, sems)

    outs = pl.pallas_call(
        body, name=name, in_specs=[HBM_ANY] * (n_ro + n_io), out_specs=[HBM_ANY] * (n_io + n_new),
        out_shape=[jax.ShapeDtypeStruct(a.shape, a.dtype) for a in comm.inout] + comm.new,
        input_output_aliases={n_ro + j: j for j in range(n_io)},
        scratch_shapes=comm.sems, compiler_params=COMM_PARAMS,
    )(*comm.ro, *comm.inout)
    return outs[:n_io], outs[n_io:]


def _rdma(src, dst, send_sem, recv_sem, device):
    return pltpu.make_async_remote_copy(src_ref=src, dst_ref=dst, send_sem=send_sem, recv_sem=recv_sem,
                                        device_id=device, device_id_type=MESH)


def gather_comm(bufs, slices):
    n = len(slices)

    def region(io, s, h, chip):
        b, li, row_sharded = slices[s]
        ref = io[b]
        if row_sharded:
            r = ref.shape[1] // N_CHIPS
            return ref.at[li, pl.ds(chip * r + h * (r // 2), r // 2), :]
        K, nn = ref.shape[1], ref.shape[2] // N_CHIPS
        return ref.at[li, pl.ds(h * (K // 2), K // 2), pl.ds(chip * nn, nn)]

    def copies(io, sems):
        x, y, c = _position()
        me = 2 * x + y
        chips = _other_chips(x, y)
        ids = [2 * cx + cy for cx, cy in chips]

        def ici(s, k, src):
            return _rdma(region(io, s, c, me), region(io, s, c, src), sems[0].at[s, k], sems[1].at[s, k], (*chips[k], c))

        def d2d(s, k, h):
            return _rdma(region(io, s, h, ids[k]), region(io, s, h, ids[k]), sems[2].at[s, k], sems[3].at[s, k],
                         (x, y, 1 - c))

        return me, ids, c, ici, d2d

    pairs = [(s, k) for s in range(n) for k in range(3)]

    def start(ro, io, new, sems):
        me, _, _, ici, _ = copies(io, sems)
        for s, k in pairs:
            ici(s, k, me).start()

    def pass_on(ro, io, new, sems):
        _, ids, c, ici, d2d = copies(io, sems)
        for s, k in pairs:
            ici(s, k, ids[k]).wait_recv()
            d2d(s, k, c).start()

    def finish(ro, io, new, sems):
        me, _, c, ici, d2d = copies(io, sems)
        for s, k in pairs:
            d2d(s, k, 1 - c).wait_recv()
        for s, k in pairs:
            ici(s, k, me).wait_send()
            d2d(s, k, c).wait_send()

    return Comm(inout=bufs, sems=[pltpu.SemaphoreType.DMA((n, 3))] * 4,
                phases=[(0.0, start), (0.6, pass_on), (1.0, finish)])


def gather_small(shard, *, name):
    n = shard.shape[2]

    def body(s_ref, o_ref, send_sem, recv_sem, local_sem):
        x, y, c = _position()
        me = 2 * x + y
        chips = _other_chips(x, y)
        place = lambda chip: o_ref.at[:, :, pl.ds(chip * n, n)]
        local = pltpu.make_async_copy(s_ref, place(me), local_sem)
        local.start()
        for k in range(3):
            _rdma(s_ref, place(me), send_sem.at[k], recv_sem.at[k], (*chips[k], c)).start()
        for k in range(3):
            _rdma(s_ref, place(2 * chips[k][0] + chips[k][1]), send_sem.at[k], recv_sem.at[k], (*chips[k], c)).wait()
        local.wait()

    return pl.pallas_call(
        body, name=name, in_specs=[HBM_ANY], out_specs=HBM_ANY,
        out_shape=jax.ShapeDtypeStruct((shard.shape[0], shard.shape[1], N_CHIPS * n), shard.dtype),
        scratch_shapes=[pltpu.SemaphoreType.DMA((3,)), pltpu.SemaphoreType.DMA((3,)), pltpu.SemaphoreType.DMA],
        compiler_params=COMM_PARAMS,
    )(shard)


def swap_comm(grads):
    def copies(ro, new, sems):
        x, y, c = _position()
        h = 1 - c
        cps = []
        for s, (a, row_sharded) in enumerate(grads):
            if row_sharded:
                r = a.shape[1] // N_CHIPS
                for q in range(N_CHIPS):
                    cps.append(_rdma(ro[s].at[:, pl.ds(q * r + h * (r // 2), r // 2), :], new[s].at[:, q],
                                     sems[0].at[s, q], sems[1].at[s, q], (x, y, 1 - c)))
            else:
                K = a.shape[1]
                cps.append(_rdma(ro[s].at[:, pl.ds(h * (K // 2), K // 2), :], new[s],
                                 sems[0].at[s, 0], sems[1].at[s, 0], (x, y, 1 - c)))
        return cps

    def start(ro, io, new, sems):
        for cp in copies(ro, new, sems):
            cp.start()

    def finish(ro, io, new, sems):
        for cp in copies(ro, new, sems):
            cp.wait()

    theirs = []
    for a, row_sharded in grads:
        if row_sharded:
            theirs.append(jax.ShapeDtypeStruct((1, N_CHIPS, a.shape[1] // N_CHIPS // 2, a.shape[2]), a.dtype))
        else:
            theirs.append(jax.ShapeDtypeStruct((1, a.shape[1] // 2, a.shape[2]), a.dtype))
    return Comm(ro=[a for a, _ in grads], new=theirs, sems=[pltpu.SemaphoreType.DMA((len(grads), N_CHIPS))] * 2,
                phases=[(0.0, start), (1.0, finish)])


def merge_comms(comms):
    spans, ro, io, new, sems = [], [], [], [], []
    for cm in comms:
        spans.append((len(ro), len(io), len(new), len(sems)))
        ro, io, new, sems = ro + cm.ro, io + cm.inout, new + cm.new, sems + cm.sems

    def phase(frac):
        def run(R, I, N, S):
            for cm, (a, b, d, e) in zip(comms, spans):
                for f, fn in cm.phases:
                    if f == frac:
                        fn(R[a:a + len(cm.ro)], I[b:b + len(cm.inout)], N[d:d + len(cm.new)], S[e:e + len(cm.sems)])
        return run

    fracs = sorted({f for cm in comms for f, _ in cm.phases})
    merged = Comm(ro=ro, inout=io, new=new, sems=sems, phases=[(f, phase(f)) for f in fracs])
    cut = lambda created: [created[d:d + len(cm.new)] for cm, (_, _, d, _) in zip(comms, spans)]
    return merged, cut


def scatter_comm(halves):
    def copies(ro, new, sems):
        x, y, c = _position()
        chips = _other_chips(x, y)
        cps = []
        for s, (a, row_sharded) in enumerate(halves):
            for k in range(3):
                to = 2 * chips[k][0] + chips[k][1]
                if row_sharded:
                    src = ro[s].at[0, to]
                else:
                    n = a.shape[2] // N_CHIPS
                    src = ro[s].at[0, :, pl.ds(to * n, n)]
                cps.append(_rdma(src, new[s].at[k], sems[0].at[s, k], sems[1].at[s, k], (*chips[k], c)))
        return cps

    def start(ro, io, new, sems):
        for cp in copies(ro, new, sems):
            cp.start()

    def finish(ro, io, new, sems):
        for cp in copies(ro, new, sems):
            cp.wait()

    landing = [jax.ShapeDtypeStruct((3, a.shape[2], a.shape[3]) if row_sharded else
                                    (3, a.shape[1], a.shape[2] // N_CHIPS), a.dtype) for a, row_sharded in halves]
    return Comm(ro=[a for a, _ in halves], new=landing, sems=[pltpu.SemaphoreType.DMA((len(halves), 3))] * 2,
                phases=[(0.0, start), (1.0, finish)])


def sibling_join_halves(shards, *, name):
    n_in = len(shards)

    def body(*refs):
        ins = refs[:n_in]
        outs = refs[n_in:2 * n_in]
        send_sem, recv_sem = refs[2 * n_in:]
        x, y, c = _position()
        sibling = (x, y, 1 - c)

        def rows(ref, h):
            R2 = ref.shape[1] // 2
            return ref.at[:, pl.ds(h * R2, R2), :]

        sends = [pltpu.make_async_remote_copy(src_ref=rows(ins[i], c), dst_ref=rows(outs[i], c),
                                              send_sem=send_sem.at[i], recv_sem=recv_sem.at[i],
                                              device_id=sibling, device_id_type=MESH) for i in range(n_in)]
        lands = [pltpu.make_async_remote_copy(src_ref=rows(ins[i], c), dst_ref=rows(outs[i], 1 - c),
                                              send_sem=send_sem.at[i], recv_sem=recv_sem.at[i],
                                              device_id=sibling, device_id_type=MESH) for i in range(n_in)]
        for cp in sends:
            cp.start()
        for i in range(n_in):
            sends[i].wait_send()
            lands[i].wait_recv()

    return pl.pallas_call(
        body, name=name, in_specs=[HBM_ANY] * n_in, out_specs=[HBM_ANY] * n_in,
        out_shape=[jax.ShapeDtypeStruct(a.shape, a.dtype) for a in shards],
        input_output_aliases={i: i for i in range(n_in)},
        scratch_shapes=[pltpu.SemaphoreType.DMA((n_in,)), pltpu.SemaphoreType.DMA((n_in,))],
        compiler_params=COMM_PARAMS,
    )(*shards)


def all_reduce_small(v, *, name):
    R, Ccols = v.shape

    def body(v_ref, o_ref, slots, send_sem, recv_sem):
        x, y, c = _position()
        me = 4 * x + 2 * y + c
        slots[me] = v_ref[...]

        def peer(d):
            return x ^ (d >> 2), y ^ ((d >> 1) & 1), c ^ (d & 1)

        def copy(d, slot):
            return pltpu.make_async_remote_copy(
                src_ref=v_ref, dst_ref=slots.at[slot], send_sem=send_sem.at[d], recv_sem=recv_sem.at[d],
                device_id=peer(d), device_id_type=MESH)

        sends = [copy(d, me) for d in range(1, N_DEV)]
        for cp in sends:
            cp.start()
        for d in range(1, N_DEV):
            px, py, pc = peer(d)
            copy(d, 4 * px + 2 * py + pc).wait_recv()
        for cp in sends:
            cp.wait_send()
        acc = slots[0]
        for d in range(1, N_DEV):
            acc = acc + slots[d]
        o_ref[...] = acc

    return pl.pallas_call(
        body, name=name,
        in_specs=[pl.BlockSpec(memory_space=pltpu.VMEM)], out_specs=pl.BlockSpec(memory_space=pltpu.VMEM),
        out_shape=jax.ShapeDtypeStruct((R, Ccols), F32),
        scratch_shapes=[pltpu.VMEM((N_DEV, R, Ccols), F32), pltpu.SemaphoreType.DMA((N_DEV,)),
                        pltpu.SemaphoreType.DMA((N_DEV,))],
        compiler_params=COMM_PARAMS,
    )(v)


def _pack_small(parts):
    flat = jnp.concatenate([p.reshape(-1).astype(F32) for p in parts])
    n = flat.shape[0]
    rows = -(-n // LANES)
    rows = -(-rows // SUBLANES) * SUBLANES
    return jnp.pad(flat, (0, rows * LANES - n)).reshape(rows, LANES)


def _unpack_small(packed, like):
    flat = packed.reshape(-1)
    out, off = [], 0
    for p in like:
        out.append(flat[off:off + p.size].reshape(p.shape))
        off += p.size
    return out


def kernel(x, norm_mix, norm_ffn, norm_final, attn_w_in, attn_w_out, attn_sinks, hgrn_w_in, hgrn_w_out, hgrn_norm, hgrn_lb_logits, ffn_w_up, ffn_conv_w, ffn_conv_b, ffn_w_down, loss_target, m_norm_mix, m_norm_ffn, m_norm_final, m_attn_w_in, m_attn_w_out, m_attn_sinks, m_hgrn_w_in, m_hgrn_w_out, m_hgrn_norm, m_hgrn_lb_logits, m_ffn_w_up, m_ffn_conv_w, m_ffn_conv_b, m_ffn_w_down, v_norm_mix, v_norm_ffn, v_norm_final, v_attn_w_in, v_attn_w_out, v_attn_sinks, v_hgrn_w_in, v_hgrn_w_out, v_hgrn_norm, v_hgrn_lb_logits, v_ffn_w_up, v_ffn_conv_w, v_ffn_conv_b, v_ffn_w_down):
    S, D = x.shape[1], x.shape[2]
    depth = norm_mix.shape[0]
    n_q = D // HEAD_DIM
    n_kv = n_q // Q_PER_KV
    Fh = ffn_w_down.shape[1] * N_CHIPS
    h = x.reshape(S, D)
    target = loss_target.reshape(S, D)

    col_names = ["attn_w_in", "hgrn_w_in", "ffn_w_up"]
    row_names = ["attn_w_out", "hgrn_w_out", "ffn_w_down"]
    x_, y_, c_ = _position()
    me = 2 * x_ + y_
    pos = jnp.stack([me, c_]).astype(jnp.int32)
    col_w = [cast_place(w, pos, row_sharded=False, name=f"cast_place_{n}")
             for n, w in zip(col_names, (attn_w_in, hgrn_w_in, ffn_w_up))]
    row_w = [cast_place(w, pos, row_sharded=True, name=f"cast_place_{n}")
             for n, w in zip(row_names, (attn_w_out, hgrn_w_out, ffn_w_down))]
    W = dict(zip(col_names + row_names, col_w + row_w))
    is_row = {n: n in row_names for n in W}

    def layer_weights(layer):
        mixer = "attn" if layer % 2 == 0 else "hgrn"
        return [(f"{mixer}_w_in", layer // 2), (f"{mixer}_w_out", layer // 2), ("ffn_w_up", layer), ("ffn_w_down", layer)]

    def gather_of(group):
        names = [n for n, _ in group]
        return names, gather_comm([W[n] for n in names], [(i, li, is_row[n]) for i, (n, li) in enumerate(group)])

    def mixer_weights(layer):
        return layer_weights(layer)[:2] if layer < depth else []

    names_next, comm_next = gather_of(mixer_weights(0))
    filled, _ = run_comm(comm_next, name="gather_mixer0")
    W.update(zip(names_next, filled))
    cw = gather_small(ffn_conv_w, name="gather_conv_w").reshape(depth, 3, 2, Fh).transpose(0, 2, 1, 3)
    cb = ffn_conv_b.reshape(depth, 2, 1, Fh)
    g_mix = norm_mix.reshape(depth, 1, D)
    g_ffn = norm_ffn.reshape(depth, 1, D)

    slopes = jnp.exp2(-8.0 * jnp.arange(1, n_q + 1, dtype=F32) / n_q).reshape(n_kv, Q_PER_KV)
    slope_col = jnp.repeat(slopes, WINDOW, axis=1).reshape(n_kv, Q_PER_KV * WINDOW, 1)

    def to_heads(t, nh):
        return t.reshape(S, nh, HEAD_DIM).transpose(1, 0, 2)

    def from_heads(t):
        return t.transpose(1, 0, 2).reshape(S, -1)

    saved = []
    for layer in range(depth):
        idx = layer // 2
        rec = {"h_mix": h}
        hn, rec["hn_t"] = rmsnorm_fwd(h, g_mix, layer, name=f"norm_mix_fwd{layer}")
        is_attn = layer % 2 == 0
        names_next, comm_next = gather_of(layer_weights(layer)[2:] + ([] if is_attn else mixer_weights(layer + 1)))
        if is_attn:
            proj = mm_nn(hn, W["attn_w_in"], idx, name=f"attn_in_fwd{layer}", out_dtype=BF16)
            q = to_heads(proj[:, :n_q * HEAD_DIM], n_q).reshape(n_kv, Q_PER_KV, S, HEAD_DIM)
            k = to_heads(proj[:, n_q * HEAD_DIM:(n_q + n_kv) * HEAD_DIM], n_kv)
            v = to_heads(proj[:, (n_q + n_kv) * HEAD_DIM:], n_kv)
            sink_col = jnp.repeat(attn_sinks[idx].reshape(n_kv, Q_PER_KV), WINDOW, axis=1)
            sink_col = sink_col.reshape(n_kv, Q_PER_KV * WINDOW, 1)
            o_heads, filled, _ = attn_fwd(q, k, v, sink_col, slope_col, name=f"attn_fwd{layer}", comm=comm_next)
            W.update(zip(names_next, filled))
            o = from_heads(o_heads.reshape(n_q, S, HEAD_DIM))
            rec.update(q=q, k=k, v=v, sink_col=sink_col, o=o)
            h = mm_nn(o, W["attn_w_out"], idx, name=f"attn_out_fwd{layer}", res=h)
        else:
            proj = mm_nn(hn, W["hgrn_w_in"], idx, name=f"hgrn_in_fwd{layer}")
            ng = hgrn_norm[idx].reshape(1, HG_EXPAND)
            o, states, filled, _ = hgrn_fwd(proj, hgrn_lb_logits, ng, layer, name=f"hgrn_fwd{layer}", comm=comm_next)
            W.update(zip(names_next, filled))
            rec.update(proj=proj, ng=ng, states=states, o=o)
            h = mm_nn(o, W["hgrn_w_out"], idx, name=f"hgrn_out_fwd{layer}", res=h)
        rec["h_ffn"] = h
        hn2, rec["hn2_t"] = rmsnorm_fwd(h, g_ffn, layer, name=f"norm_ffn_fwd{layer}")
        if is_attn and layer + 1 < depth:
            names_next, comm_next = gather_of(mixer_weights(layer + 1))
            u, filled = mm_nn(hn2, W["ffn_w_up"], layer, name=f"ffn_up_fwd{layer}", split_out=True, out_dtype=BF16,
                              comm=comm_next)
            W.update(zip(names_next, filled))
        else:
            u = mm_nn(hn2, W["ffn_w_up"], layer, name=f"ffn_up_fwd{layer}", split_out=True, out_dtype=BF16)
        act, rec["act_t"] = convact_fwd(u, cw, cb, layer, name=f"ffn_act_fwd{layer}")
        rec["u"] = u
        h = mm_nn(act, W["ffn_w_down"], layer, name=f"ffn_down_fwd{layer}", res=h)
        saved.append(rec)

    loss_part, dh, d_norm_final = loss_head(h, norm_final.reshape(1, D), target, name="loss_head")

    n_attn, n_hgrn = attn_w_in.shape[0], hgrn_w_in.shape[0]
    reduced_parts = []
    pending = []

    def swap_of(group, g):
        ordered = sorted(group, key=lambda nl: is_row[nl[0]])
        return ordered, swap_comm([(g[n], is_row[n]) for n, _ in ordered])

    def pair_up(tag, ordered, g, theirs):
        pair, pair_bf16 = {}, {}
        for (n, _), t in zip(ordered, theirs):
            gn = g[n]
            if is_row[n]:
                r = gn.shape[1] // N_CHIPS
                gn = gn.reshape(1, N_CHIPS, 2, r // 2, gn.shape[2])
            pair[n], pair_bf16[n] = pair_add(gn, t, pos, row_sharded=is_row[n], name=f"rs_pair_add_{n}{tag}")
        pending.append((ordered, pair, scatter_comm([(pair_bf16[n], is_row[n]) for n, _ in ordered])))

    def landed_behind(created, cut):
        for (ordered, pair, _), landed in zip(pending, cut(created)):
            reduced_parts.append((ordered, pair, landed))
        pending.clear()

    d_norm_mix, d_norm_ffn = [None] * depth, [None] * depth
    d_conv_w, d_conv_b = [None] * depth, [None] * depth
    d_sinks, d_hgrn_norm = [None] * n_attn, [None] * n_hgrn
    d_logits = jnp.zeros_like(hgrn_lb_logits)
    for layer in reversed(range(depth)):
        idx = layer // 2
        rec = saved[layer]
        g = {"ffn_w_down": mm_dw(rec["act_t"], dh, name=f"ffn_down_dw{layer}")}
        dact = mm_nt(dh, W["ffn_w_down"], layer, name=f"ffn_down_dx{layer}")
        dc, dcw, dcb = convact_bwd(rec["u"], cw, cb, layer, dact, name=f"ffn_act_bwd{layer}")
        d_conv_w[layer] = dcw.transpose(1, 0, 2).reshape(3, 2 * Fh)
        d_conv_b[layer] = dcb.reshape(2 * Fh)
        du = conv_input_grad(dc, cw, layer, name=f"ffn_conv_dx{layer}")
        g["ffn_w_up"] = mm_dw(rec["hn2_t"], du, name=f"ffn_up_dw{layer}", split_b=True)
        ordered, swap = swap_of(layer_weights(layer)[2:], g)
        dhn2, theirs = mm_nt(du, W["ffn_w_up"], layer, name=f"ffn_up_dx{layer}", split_in=True, comm=swap)
        pair_up(layer, ordered, g, theirs)
        dh, dg = rmsnorm_bwd(rec["h_ffn"], g_ffn, layer, dhn2, dh, name=f"norm_ffn_bwd{layer}")
        d_norm_ffn[layer] = dg.reshape(D)
        carried, cut = merge_comms([p[2] for p in pending])
        if layer % 2 == 0:
            g["attn_w_out"] = mm_tn(rec["o"], dh, name=f"attn_out_dw{layer}")
            do = mm_nt(dh, W["attn_w_out"], idx, name=f"attn_out_dx{layer}", out_dtype=BF16)
            do_heads = to_heads(do, n_q).reshape(n_kv, Q_PER_KV, S, HEAD_DIM)
            res = attn_bwd(rec["q"], rec["k"], rec["v"], rec["sink_col"], slope_col, do_heads,
                           name=f"attn_bwd{layer}", comm=carried)
            dq, dk, dv, dsink = res[:4]
            landed_behind(res[-1], cut)
            dproj = jnp.concatenate([from_heads(dq.reshape(n_q, S, HEAD_DIM)), from_heads(dk), from_heads(dv)], axis=1)
            d_sinks[idx] = jnp.sum(dsink.reshape(n_kv, Q_PER_KV, WINDOW), axis=-1).reshape(n_q)
            g["attn_w_in"] = mm_dw(rec["hn_t"], dproj, name=f"attn_in_dw{layer}")
            ordered, swap = swap_of(layer_weights(layer)[:2], g)
            dhn, theirs = mm_nt(dproj, W["attn_w_in"], idx, name=f"attn_in_dx{layer}", comm=swap)
        else:
            g["hgrn_w_out"] = mm_tn(rec["o"], dh, name=f"hgrn_out_dw{layer}")
            do = mm_nt(dh, W["hgrn_w_out"], idx, name=f"hgrn_out_dx{layer}")
            res = hgrn_bwd(rec["proj"], hgrn_lb_logits, rec["ng"], rec["states"], do, layer,
                           name=f"hgrn_bwd{layer}", comm=carried)
            dproj, dlg, dng = res[:3]
            landed_behind(res[-1], cut)
            d_logits = d_logits + dlg
            d_hgrn_norm[idx] = dng.reshape(HG_EXPAND)
            g["hgrn_w_in"] = mm_dw(rec["hn_t"], dproj, name=f"hgrn_in_dw{layer}")
            ordered, swap = swap_of(layer_weights(layer)[:2], g)
            dhn, theirs = mm_nt(dproj, W["hgrn_w_in"], idx, name=f"hgrn_in_dx{layer}", comm=swap)
        pair_up(layer, ordered, g, theirs)
        dh, dg = rmsnorm_bwd(rec["h_mix"], g_mix, layer, dhn, dh, name=f"norm_mix_bwd{layer}")
        d_norm_mix[layer] = dg.reshape(D)
    grad_x = dh.reshape(x.shape)
    carried, cut = merge_comms([p[2] for p in pending])
    landed_behind(run_comm(carried, name="rs_scatter_last")[1], cut)

    small_w = [norm_mix, norm_ffn, norm_final, attn_sinks, hgrn_norm, hgrn_lb_logits, ffn_conv_b]
    small_m = [m_norm_mix, m_norm_ffn, m_norm_final, m_attn_sinks, m_hgrn_norm, m_hgrn_lb_logits, m_ffn_conv_b]
    small_v = [v_norm_mix, v_norm_ffn, v_norm_final, v_attn_sinks, v_hgrn_norm, v_hgrn_lb_logits, v_ffn_conv_b]
    small_grads = [jnp.stack(d_norm_mix), jnp.stack(d_norm_ffn), d_norm_final.reshape(D), jnp.stack(d_sinks),
                   jnp.stack(d_hgrn_norm), d_logits, jnp.stack(d_conv_b)]
    reduced = all_reduce_small(_pack_small(small_grads + [loss_part]), name="all_reduce_small")
    small_g = _unpack_small(reduced, small_w + [loss_part])
    loss = small_g.pop().reshape(())
    pad = [jnp.zeros((1, 1), F32)]
    sd, sm_, sv_ = adamw(_pack_small(small_w + pad), reduced, _pack_small(small_m + pad),
                         _pack_small(small_v + pad), name="adamw_small")
    small_d = _unpack_small(sd, small_w)
    small_nm = _unpack_small(sm_, small_w)
    small_nv = _unpack_small(sv_, small_w)

    big_names = col_names + row_names
    big_w = dict(attn_w_in=attn_w_in, hgrn_w_in=hgrn_w_in, ffn_w_up=ffn_w_up, attn_w_out=attn_w_out,
                 hgrn_w_out=hgrn_w_out, ffn_w_down=ffn_w_down)
    half_shard = dict.fromkeys(big_names)
    for ordered, pair, landed in reduced_parts:
        for (n, li), ld in zip(ordered, landed):
            half_shard[n] = chip_add(pair[n], ld, pos, half_shard[n], li, big_w[n].shape[0], row_sharded=is_row[n],
                                     name=f"rs_chip_add_{n}{li}")
    shards = sibling_join_halves([half_shard[n].reshape(big_w[n].shape) for n in big_names], name="rs_sibling_join")
    big_m = dict(attn_w_in=m_attn_w_in, hgrn_w_in=m_hgrn_w_in, ffn_w_up=m_ffn_w_up, attn_w_out=m_attn_w_out,
                 hgrn_w_out=m_hgrn_w_out, ffn_w_down=m_ffn_w_down)
    big_v = dict(attn_w_in=v_attn_w_in, hgrn_w_in=v_hgrn_w_in, ffn_w_up=v_ffn_w_up, attn_w_out=v_attn_w_out,
                 hgrn_w_out=v_hgrn_w_out, ffn_w_down=v_ffn_w_down)
    grads, deltas, new_m, new_v = {}, {}, {}, {}
    for name_, gshard in zip(big_names, shards):
        w_ = big_w[name_]
        cols = w_.shape[-1]
        d_, m_, v_ = adamw(w_.reshape(-1, cols), gshard.reshape(-1, cols), big_m[name_].reshape(-1, cols),
                           big_v[name_].reshape(-1, cols), name=f"adamw_{name_}")
        grads[name_] = gshard.reshape(w_.shape)
        deltas[name_], new_m[name_], new_v[name_] = (t.reshape(w_.shape) for t in (d_, m_, v_))

    gconv = jnp.stack(d_conv_w)
    n_conv = ffn_conv_w.shape[2]
    gconv_all = _unpack_small(all_reduce_small(_pack_small([gconv]), name="all_reduce_conv_w"), [gconv])[0]
    gconv_mine = lax.dynamic_slice_in_dim(gconv_all, me * n_conv, n_conv, axis=2)
    cshape = ffn_conv_w.shape
    d_, m_, v_ = adamw(ffn_conv_w.reshape(-1, n_conv), gconv_mine.reshape(-1, n_conv),
                       m_ffn_conv_w.reshape(-1, n_conv), v_ffn_conv_w.reshape(-1, n_conv), name="adamw_ffn_conv_w")
    grads["ffn_conv_w"] = gconv_mine
    deltas["ffn_conv_w"], new_m["ffn_conv_w"], new_v["ffn_conv_w"] = (t.reshape(cshape) for t in (d_, m_, v_))

    order = ["norm_mix", "norm_ffn", "norm_final", "attn_w_in", "attn_w_out", "attn_sinks", "hgrn_w_in",
             "hgrn_w_out", "hgrn_norm", "hgrn_lb_logits", "ffn_w_up", "ffn_conv_w", "ffn_conv_b", "ffn_w_down"]
    small_names = ["norm_mix", "norm_ffn", "norm_final", "attn_sinks", "hgrn_norm", "hgrn_lb_logits", "ffn_conv_b"]
    for i, name_ in enumerate(small_names):
        grads[name_], deltas[name_], new_m[name_], new_v[name_] = small_g[i], small_d[i], small_nm[i], small_nv[i]
    return (loss, grad_x, *[grads[n] for n in order], *[deltas[n] for n in order],
            *[new_m[n] for n in order], *[new_v[n] for n in order])
```

```python
import functools
import math

import numpy as np
import jax
import jax.numpy as jnp
from jax import lax
from jax.experimental import pallas as pl
from jax.experimental.pallas import tpu as pltpu

F32 = jnp.float32
BF16 = jnp.bfloat16
MESH = pl.DeviceIdType.MESH

HEAD_DIM = 64
Q_PER_KV = 4
WINDOW = 128
HG_EXPAND = 128
HG_CHUNK = 128
EPS = 1e-6
N_CHIPS = 4
N_DEV = 8

ADAM_LR = 0.001
ADAM_B1 = 0.9
ADAM_B2 = 0.999
ADAM_EPS = 1e-08
ADAM_WD = 0.01
ADAM_STEP = 10

VMEM_LIMIT_BYTES = 48 * 1024 * 1024
LANES = 128
SUBLANES = 8
HALO_ROWS = 16


def _params(sem=None):
    return pltpu.CompilerParams(dimension_semantics=sem, vmem_limit_bytes=VMEM_LIMIT_BYTES)


def _pick(n, cands):
    for c in cands:
        if n % c == 0:
            return c
    return n


def _dot(a, b, dims):
    return lax.dot_general(a.astype(BF16), b.astype(BF16), (dims, ((), ())),
                           preferred_element_type=F32)


NN = ((1,), (0,))
NT = ((1,), (1,))
TN = ((0,), (0,))


def mm_nn(a, w, l, *, name, res=None, out_dtype=F32, split_out=False, comm=None):
    M, K = a.shape
    N = w.shape[2]
    tm = _pick(M, (1024, 512, 256, 128))
    tn = _pick(N // 2 if split_out else N, (1408, 1024, 768, 512, 256, 128))
    tk = _pick(K, (1024, 1408, 512, 256, 128))
    nk = K // tk
    nh = (N // 2) // tn

    def body(*refs):
        if res is None:
            a_ref, w_ref, o_ref = refs[:3]
            r_ref = None
        else:
            a_ref, w_ref, r_ref, o_ref = refs[:4]
        part = _dot(a_ref[...], w_ref[...], NN)

        def finish(acc):
            if r_ref is not None:
                acc = acc + r_ref[...]
            o_ref[...] = acc.astype(out_dtype)

        if nk == 1:
            finish(part)
        else:
            acc_ref = refs[-1]
            k = pl.program_id(2)

            @pl.when(k == 0)
            def _():
                acc_ref[...] = part

            @pl.when(k > 0)
            def _():
                acc_ref[...] += part

            @pl.when(k == nk - 1)
            def _():
                finish(acc_ref[...])

    in_specs = [pl.BlockSpec((tm, tk), lambda n, m, k: (m, k)),
                pl.BlockSpec((None, tk, tn), lambda n, m, k: (l, k, n))]
    args = [a, w]
    if res is not None:
        in_specs.append(pl.BlockSpec((tm, tn), lambda n, m, k: (m, n)))
        args.append(res)
    if split_out:
        out_shape = jax.ShapeDtypeStruct((2, M, N // 2), out_dtype)
        out_spec = pl.BlockSpec((None, tm, tn), lambda n, m, k: (n // nh, m, n % nh))
    else:
        out_shape = jax.ShapeDtypeStruct((M, N), out_dtype)
        out_spec = pl.BlockSpec((tm, tn), lambda n, m, k: (m, n))
    outs, updated, _ = _carried_call(
        body, args, name=name, grid=(N // tn, M // tm, nk), in_specs=in_specs, out_specs=[out_spec],
        out_shape=[out_shape], scratch_shapes=[] if nk == 1 else [pltpu.VMEM((tm, tn), F32)], comm=comm)
    return outs[0] if comm is None else (outs[0], updated)


def mm_nt(a, w, l, *, name, split_in=False, out_dtype=F32, comm=None):
    if split_in:
        _, M, Nh = a.shape
        N = 2 * Nh
    else:
        M, N = a.shape
        Nh = N
    K = w.shape[1]
    tm = _pick(M, (1024, 512, 256, 128))
    tko = _pick(K, (1024, 1408, 512, 256, 128))
    tr = _pick(Nh, (1024, 1408, 768, 512, 256, 128))
    nr = N // tr
    nh = Nh // tr

    def body(a_ref, w_ref, o_ref, *scratch):
        part = _dot(a_ref[...], w_ref[...], NT)
        if nr == 1:
            o_ref[...] = part.astype(out_dtype)
        else:
            acc_ref = scratch[0]
            r = pl.program_id(2)

            @pl.when(r == 0)
            def _():
                acc_ref[...] = part

            @pl.when(r > 0)
            def _():
                acc_ref[...] += part

            @pl.when(r == nr - 1)
            def _():
                o_ref[...] = acc_ref[...].astype(out_dtype)

    if split_in:
        a_spec = pl.BlockSpec((None, tm, tr), lambda ko, m, r: (r // nh, m, r % nh))
    else:
        a_spec = pl.BlockSpec((tm, tr), lambda ko, m, r: (m, r))
    outs, _, created = _carried_call(
        body, [a, w], name=name, grid=(K // tko, M // tm, nr),
        in_specs=[a_spec, pl.BlockSpec((None, tko, tr), lambda ko, m, r: (l, ko, r))],
        out_specs=[pl.BlockSpec((tm, tko), lambda ko, m, r: (m, ko))],
        out_shape=[jax.ShapeDtypeStruct((M, K), out_dtype)],
        scratch_shapes=[] if nr == 1 else [pltpu.VMEM((tm, tko), F32)], comm=comm)
    return outs[0] if comm is None else (outs[0], created)


def mm_tn(a, b, *, name):
    M, K = a.shape
    N = b.shape[1]
    tm = _pick(M, (512, 256, 128))
    tko = _pick(K, (1024, 1408, 704, 512, 256, 128))
    tn = _pick(N, (512, 256, 128))
    nm = M // tm

    def body(a_ref, b_ref, o_ref):
        part = _dot(a_ref[...], b_ref[...], TN)
        r = pl.program_id(2)

        @pl.when(r == 0)
        def _():
            o_ref[...] = part

        @pl.when(r > 0)
        def _():
            o_ref[...] += part

    return pl.pallas_call(
        body, name=name, grid=(K // tko, N // tn, nm),
        in_specs=[pl.BlockSpec((tm, tko), lambda ko, n, r: (r, ko)), pl.BlockSpec((tm, tn), lambda ko, n, r: (r, n))],
        out_specs=pl.BlockSpec((None, tko, tn), lambda ko, n, r: (0, ko, n)),
        out_shape=jax.ShapeDtypeStruct((1, K, N), F32),
        compiler_params=_params(("parallel", "parallel", "arbitrary")),
    )(a, b)


def mm_dw(at, b, *, name, split_b=False):
    K, M = at.shape
    if split_b:
        Nh = b.shape[2]
        N = 2 * Nh
    else:
        N = b.shape[1]
        Nh = N
    tm = _pick(M, (1024, 512, 256, 128))
    tko = _pick(K, (1024, 1408, 704, 512, 256, 128))
    tn = _pick(Nh, (1408, 1024, 768, 512, 256, 128))
    nm = M // tm
    nh = Nh // tn

    def body(a_ref, b_ref, o_ref):
        part = _dot(a_ref[...], b_ref[...], NN)
        r = pl.program_id(2)

        @pl.when(r == 0)
        def _():
            o_ref[...] = part

        @pl.when(r > 0)
        def _():
            o_ref[...] += part

    if split_b:
        b_spec = pl.BlockSpec((None, tm, tn), lambda ko, n, r: (n // nh, r, n % nh))
    else:
        b_spec = pl.BlockSpec((tm, tn), lambda ko, n, r: (r, n))
    return pl.pallas_call(
        body, name=name, grid=(K // tko, N // tn, nm),
        in_specs=[pl.BlockSpec((tko, tm), lambda ko, n, r: (ko, r)), b_spec],
        out_specs=pl.BlockSpec((None, tko, tn), lambda ko, n, r: (0, ko, n)),
        out_shape=jax.ShapeDtypeStruct((1, K, N), F32),
        compiler_params=_params(("parallel", "parallel", "arbitrary")),
    )(at, b)


def _rms(x, g):
    return x * lax.rsqrt(jnp.mean(x * x, axis=-1, keepdims=True) + EPS) * g


def rmsnorm_fwd(h, g, l, *, name):
    S, D = h.shape
    tm = _pick(S, (512, 256, 128))

    def body(h_ref, g_ref, o_ref, ot_ref):
        y = _rms(h_ref[...], g_ref[...])
        o_ref[...] = y.astype(BF16)
        ot_ref[...] = y.T.astype(BF16)

    return pl.pallas_call(
        body, name=name, grid=(S // tm,),
        in_specs=[pl.BlockSpec((tm, D), lambda m: (m, 0)),
                  pl.BlockSpec((None, 1, D), lambda m: (l, 0, 0))],
        out_specs=[pl.BlockSpec((tm, D), lambda m: (m, 0)), pl.BlockSpec((D, tm), lambda m: (0, m))],
        out_shape=[jax.ShapeDtypeStruct((S, D), BF16), jax.ShapeDtypeStruct((D, S), BF16)],
        compiler_params=_params(("parallel",)),
    )(h, g)


def rmsnorm_bwd(h, g, l, dhn, dres, *, name):
    S, D = h.shape
    tm = _pick(S, (512, 256, 128))

    def body(h_ref, g_ref, dhn_ref, dres_ref, dh_ref, dg_ref):
        _, vjp = jax.vjp(_rms, h_ref[...], g_ref[...])
        dx, dg = vjp(dhn_ref[...])
        dh_ref[...] = dres_ref[...] + dx

        @pl.when(pl.program_id(0) == 0)
        def _():
            dg_ref[...] = dg

        @pl.when(pl.program_id(0) > 0)
        def _():
            dg_ref[...] += dg

    row = pl.BlockSpec((tm, D), lambda m: (m, 0))
    return pl.pallas_call(
        body, name=name, grid=(S // tm,),
        in_specs=[row, pl.BlockSpec((None, 1, D), lambda m: (l, 0, 0)), row, row],
        out_specs=[row, pl.BlockSpec((1, D), lambda m: (0, 0))],
        out_shape=[jax.ShapeDtypeStruct((S, D), F32), jax.ShapeDtypeStruct((1, D), F32)],
        compiler_params=_params(("arbitrary",)),
    )(h, g, dhn, dres)


def loss_head(h, g, target, *, name):
    S, D = h.shape
    tm = _pick(S, (512, 256, 128))

    def body(h_ref, g_ref, t_ref, loss_ref, dh_ref, dg_ref):
        y, vjp = jax.vjp(_rms, h_ref[...], g_ref[...])
        err = y - t_ref[...]
        part = 0.5 * jnp.sum(jnp.sum(err * err, axis=-1, keepdims=True) / D, axis=0, keepdims=True)
        dx, dg = vjp(err / D)
        dh_ref[...] = dx

        @pl.when(pl.program_id(0) == 0)
        def _():
            dg_ref[...] = dg
            loss_ref[...] = part

        @pl.when(pl.program_id(0) > 0)
        def _():
            dg_ref[...] += dg
            loss_ref[...] += part

    row = pl.BlockSpec((tm, D), lambda m: (m, 0))
    return pl.pallas_call(
        body, name=name, grid=(S // tm,),
        in_specs=[row, pl.BlockSpec((1, D), lambda m: (0, 0)), row],
        out_specs=[pl.BlockSpec((1, 1), lambda m: (0, 0)), row, pl.BlockSpec((1, D), lambda m: (0, 0))],
        out_shape=[jax.ShapeDtypeStruct((1, 1), F32), jax.ShapeDtypeStruct((S, D), F32),
                   jax.ShapeDtypeStruct((1, D), F32)],
        compiler_params=_params(("arbitrary",)),
    )(h, g, target)


def _make_mm(dims_fwd, dims_da, dims_db, swap_da=False, swap_db=False):
    @jax.custom_vjp
    def mm(a, b):
        return _dot(a, b, dims_fwd)

    def fwd(a, b):
        return mm(a, b), (a, b)

    def bwd(resid, g):
        a, b = resid
        da = _dot(b, g, dims_da) if swap_da else _dot(g, b, dims_da)
        db = _dot(g, a, dims_db) if swap_db else _dot(a, g, dims_db)
        return da, db

    mm.defvjp(fwd, bwd)
    return mm


_mm = _make_mm(NN, NT, TN)
_mm_nt = _make_mm(NT, NN, TN, swap_db=True)
_mm_tn = _make_mm(TN, NT, NN, swap_da=True)


def _dot_hi(a, b):
    return jnp.dot(a, b, precision=lax.Precision.HIGHEST, preferred_element_type=F32)


def _sigmoid(x):
    return 0.5 * jnp.tanh(0.5 * x) + 0.5


def _silu(x):
    return x * _sigmoid(x)


def _row_slices(x, n):
    rows = x.shape[0] // n
    return tuple(x[i * rows:(i + 1) * rows] for i in range(n))


@functools.partial(jax.custom_vjp, nondiff_argnums=(1,))
def _split_rows(x, n):
    return _row_slices(x, n)


_split_rows.defvjp(lambda x, n: (_row_slices(x, n), None), lambda n, _, cts: (jnp.concatenate(cts, axis=0),))


def _attn_block(qs, kws, vws, sink, slope, first):
    s = jnp.concatenate([_mm_nt(q, kw) for q, kw in zip(qs, kws)], axis=0) * (HEAD_DIM ** -0.5)
    qi = lax.broadcasted_iota(jnp.int32, s.shape, 0) % WINDOW
    ki = lax.broadcasted_iota(jnp.int32, s.shape, 1)
    dist = qi + WINDOW - ki
    valid = (dist >= 0) & (dist < WINDOW) & (ki >= WINDOW * first)
    s = jnp.where(valid, s - slope * dist.astype(F32), -jnp.inf)
    m = lax.stop_gradient(jnp.maximum(jnp.max(s, axis=-1, keepdims=True), sink))
    e = jnp.exp(s - m)
    denom = jnp.sum(e, axis=-1, keepdims=True) + jnp.exp(sink - m)
    ps = _split_rows(e * (1.0 / denom), len(qs))
    return tuple(_mm(p, vw) for p, vw in zip(ps, vws))


def _attn_specs(Hkv, order):
    G = Q_PER_KV
    qspec = pl.BlockSpec((Hkv, G, WINDOW, HEAD_DIM), lambda n: (0, 0, order(n), 0))
    prev = pl.BlockSpec((Hkv, WINDOW, HEAD_DIM), lambda n: (0, jnp.maximum(order(n) - 1, 0), 0))
    cur = pl.BlockSpec((Hkv, WINDOW, HEAD_DIM), lambda n: (0, order(n), 0))
    col = pl.BlockSpec((Hkv, G * WINDOW, 1), lambda n: (0, 0, 0))
    return qspec, prev, cur, col


def attn_fwd(q, k, v, sink_col, slope_col, *, name, comm=None):
    Hkv, G, S, _ = q.shape
    qspec, prev, cur, col = _attn_specs(Hkv, lambda n: n)

    def body(q_ref, kp_ref, kc_ref, vp_ref, vc_ref, sink_ref, slope_ref, o_ref):
        first = (pl.program_id(0) == 0).astype(jnp.int32)
        qs = tuple(q_ref[h].reshape(G * WINDOW, HEAD_DIM) for h in range(Hkv))
        kws = tuple(jnp.concatenate([kp_ref[h], kc_ref[h]], axis=0) for h in range(Hkv))
        vws = tuple(jnp.concatenate([vp_ref[h], vc_ref[h]], axis=0) for h in range(Hkv))
        outs = _attn_block(qs, kws, vws, sink_ref[...].reshape(Hkv * G * WINDOW, 1),
                           slope_ref[...].reshape(Hkv * G * WINDOW, 1), first)
        o_ref[...] = jnp.stack([o.reshape(G, WINDOW, HEAD_DIM).astype(BF16) for o in outs])

    outs, updated, created = _carried_call(
        body, [q, k, k, v, v, sink_col, slope_col], name=name, grid=(S // WINDOW,),
        in_specs=[qspec, prev, cur, prev, cur, col, col], out_specs=[qspec],
        out_shape=[jax.ShapeDtypeStruct(q.shape, BF16)], comm=comm)
    return outs[0] if comm is None else (outs[0], updated, created)


def attn_bwd(q, k, v, sink_col, slope_col, do, *, name, comm=None):
    Hkv, G, S, _ = q.shape
    nb = S // WINDOW
    qspec, prev, cur, col = _attn_specs(Hkv, lambda n: nb - 1 - n)

    def body(q_ref, kp_ref, kc_ref, vp_ref, vc_ref, sink_ref, slope_ref, do_ref,
             dq_ref, dk_ref, dv_ref, dsink_ref, dk_carry, dv_carry):
        @pl.when(pl.program_id(0) == 0)
        def _():
            dk_carry[...] = jnp.zeros_like(dk_carry)
            dv_carry[...] = jnp.zeros_like(dv_carry)
            dsink_ref[...] = jnp.zeros_like(dsink_ref)

        first = (pl.program_id(0) == nb - 1).astype(jnp.int32)
        qs = tuple(q_ref[h].reshape(G * WINDOW, HEAD_DIM).astype(F32) for h in range(Hkv))
        kws = tuple(jnp.concatenate([kp_ref[h], kc_ref[h]], axis=0).astype(F32) for h in range(Hkv))
        vws = tuple(jnp.concatenate([vp_ref[h], vc_ref[h]], axis=0).astype(F32) for h in range(Hkv))
        fn = functools.partial(_attn_block, slope=slope_ref[...].reshape(Hkv * G * WINDOW, 1), first=first)
        _, vjp = jax.vjp(fn, qs, kws, vws, sink_ref[...].reshape(Hkv * G * WINDOW, 1))
        dqs, dkws, dvws, dsink = vjp(tuple(do_ref[h].reshape(G * WINDOW, HEAD_DIM).astype(F32) for h in range(Hkv)))
        dkw, dvw = jnp.stack(dkws), jnp.stack(dvws)
        dq_ref[...] = jnp.stack([dq.reshape(G, WINDOW, HEAD_DIM) for dq in dqs]).astype(BF16)
        dk_ref[...] = (dkw[:, WINDOW:] + dk_carry[...]).astype(BF16)
        dv_ref[...] = (dvw[:, WINDOW:] + dv_carry[...]).astype(BF16)
        dk_carry[...] = dkw[:, :WINDOW]
        dv_carry[...] = dvw[:, :WINDOW]
        dsink_ref[...] += dsink.reshape(Hkv, G * WINDOW, 1)

    kv_shape = jax.ShapeDtypeStruct(k.shape, BF16)
    outs, _, created = _carried_call(
        body, [q, k, k, v, v, sink_col, slope_col, do], name=name, grid=(nb,),
        in_specs=[qspec, prev, cur, prev, cur, col, col, qspec],
        out_specs=[qspec, cur, cur, col],
        out_shape=[jax.ShapeDtypeStruct(q.shape, BF16), kv_shape, kv_shape,
                   jax.ShapeDtypeStruct(sink_col.shape, F32)],
        scratch_shapes=[pltpu.VMEM((Hkv, WINDOW, HEAD_DIM), F32), pltpu.VMEM((Hkv, WINDOW, HEAD_DIM), F32)],
        comm=comm)
    return outs if comm is None else (*outs, created)


def _hgrn_consts():
    C = HG_CHUNK
    t = np.arange(C)[:, None]
    j = np.arange(C)[None, :]
    cq, ck, mk = [], [], []
    H = C // 2
    while H >= 1:
        start = (t // H) * H
        cq.append((j > start) & (j <= t))
        ck.append((j > t) & (j <= start + H))
        mk.append(((t // H) % 2 == 1) & ((j // H) == (t // H) - 1))
        H //= 2
    ltri = (j <= t)
    ops = np.concatenate([ltri] + cq + ck, axis=0).astype(np.float32)
    ops3 = np.concatenate([ops, ops, ops], axis=1)
    opst3 = np.concatenate([ops.T, ops.T, ops.T], axis=1)
    return (jnp.asarray(ops3, BF16), jnp.asarray(opst3, BF16), jnp.asarray(np.stack(mk).astype(np.float32)))


def _split3(x):
    hi = x.astype(BF16)
    r = x - hi.astype(F32)
    mid = r.astype(BF16)
    lo = (r - mid.astype(F32)).astype(BF16)
    return jnp.concatenate([hi, mid, lo], axis=0)


@jax.custom_vjp
def _seg_sums(g, ops3, opst3):
    C = g.shape[0]
    out = jnp.dot(ops3, _split3(g), preferred_element_type=F32)
    return tuple(out[i * C:(i + 1) * C] for i in range(ops3.shape[0] // C))


def _seg_sums_fwd(g, ops3, opst3):
    return _seg_sums(g, ops3, opst3), opst3


def _seg_sums_bwd(opst3, cts):
    ct = jnp.concatenate(cts, axis=0)
    return jnp.dot(opst3, _split3(ct), preferred_element_type=F32), None, None


_seg_sums.defvjp(_seg_sums_fwd, _seg_sums_bwd)


def _head_slices(x):
    return tuple(x[:, h * HG_EXPAND:(h + 1) * HG_EXPAND] for h in range(x.shape[1] // HG_EXPAND))


@jax.custom_vjp
def _split_heads(x):
    return _head_slices(x)


_split_heads.defvjp(lambda x: (_head_slices(x), None), lambda _, cts: (jnp.concatenate(cts, axis=1),))


@jax.custom_vjp
def _join_heads(xs):
    return jnp.concatenate(xs, axis=1)


_join_heads.defvjp(lambda xs: (jnp.concatenate(xs, axis=1), None), lambda _, ct: (_head_slices(ct),))


def _hgrn_chunk(qr, fr, iv, gr, sts, logits, ng, ops3, opst3, mk, layer):
    n_levels = mk.shape[0]
    depth, D = logits.shape
    e = jnp.exp(logits - lax.stop_gradient(jnp.max(logits, axis=0, keepdims=True)))
    sm = e / jnp.sum(e, axis=0, keepdims=True)
    row = lax.broadcasted_iota(jnp.int32, (depth, D), 0)
    lb = jnp.sum(jnp.where((row >= 1) & (row <= layer), sm, 0.0), axis=0, keepdims=True)

    q = _silu(qr)
    a = jnp.log(lb)
    c = jnp.log(1.0 - lb) - (jnp.maximum(-fr, 0.0) + jnp.log(1.0 + jnp.exp(-jnp.abs(fr))))
    mx = lax.stop_gradient(jnp.maximum(a, c))
    g = mx + jnp.log(jnp.exp(a - mx) + jnp.exp(c - mx))
    k = (1.0 - lb) * _sigmoid(-fr)

    sums = _seg_sums(g, ops3, opst3)
    b = sums[0]
    b_last = jnp.sum(g, axis=0, keepdims=True)

    ql = [_split_heads(q * jnp.exp(sums[1 + lv])) for lv in range(n_levels)]
    kl = [_split_heads(k * jnp.exp(sums[1 + n_levels + lv])) for lv in range(n_levels)]
    qb, kb = _split_heads(q * jnp.exp(b)), _split_heads(k * jnp.exp(b_last - b))
    qk, ivh, gate, decay = _split_heads(q * k), _split_heads(iv), _split_heads(_silu(gr)), _split_heads(jnp.exp(b_last))
    outs, new_sts = [], []
    for h, st in enumerate(sts):
        amat = None
        for lv in range(n_levels):
            part = _mm_nt(ql[lv][h], kl[lv][h]) * mk[lv]
            amat = part if amat is None else amat + part
        o = jnp.sum(qk[h], axis=-1, keepdims=True) * ivh[h] + _mm(amat, ivh[h]) + _mm_nt(qb[h], st)
        new_sts.append(st * decay[h] + _mm_tn(ivh[h], kb[h]))
        on = o * lax.rsqrt(jnp.mean(o * o, axis=-1, keepdims=True) + EPS) * ng
        outs.append(on * gate[h])
    return _join_heads(tuple(outs)), tuple(new_sts)


def _hgrn_in_specs(D, depth, order):
    C = HG_CHUNK
    nl = int(math.log2(C))
    n_ops = 1 + 2 * nl
    return [pl.BlockSpec((C, 4 * D), lambda c: (order(c), 0)),
            pl.BlockSpec((depth, D), lambda c: (0, 0)),
            pl.BlockSpec((1, HG_EXPAND), lambda c: (0, 0)),
            pl.BlockSpec((n_ops * C, 3 * C), lambda c: (0, 0)),
            pl.BlockSpec((C, 3 * n_ops * C), lambda c: (0, 0)),
            pl.BlockSpec((nl, C, C), lambda c: (0, 0, 0))]


def hgrn_fwd(proj, logits, ng, layer, *, name, comm=None):
    S = proj.shape[0]
    D = proj.shape[1] // 4
    H = D // HG_EXPAND
    C = HG_CHUNK
    nc = S // C
    consts = _hgrn_consts()

    def body(p_ref, lg_ref, ng_ref, ops3_ref, opst3_ref, mk_ref, o_ref, st_out_ref, st_ref):
        @pl.when(pl.program_id(0) == 0)
        def _():
            st_ref[...] = jnp.zeros_like(st_ref)

        st_out_ref[...] = st_ref[...]
        out, new_sts = _hgrn_chunk(*(p_ref[:, j * D:(j + 1) * D] for j in range(4)), tuple(st_ref[h] for h in range(H)),
                                   lg_ref[...], ng_ref[...], ops3_ref[...], opst3_ref[...], mk_ref[...], layer)
        o_ref[...] = out.astype(BF16)
        for h in range(H):
            st_ref[h] = new_sts[h]

    outs, updated, created = _carried_call(
        body, [proj, logits, ng, *consts], name=name, grid=(nc,),
        in_specs=_hgrn_in_specs(D, logits.shape[0], lambda c: c),
        out_specs=[pl.BlockSpec((C, D), lambda c: (c, 0)),
                   pl.BlockSpec((None, H, HG_EXPAND, HG_EXPAND), lambda c: (c, 0, 0, 0))],
        out_shape=[jax.ShapeDtypeStruct((S, D), BF16),
                   jax.ShapeDtypeStruct((nc, H, HG_EXPAND, HG_EXPAND), F32)],
        scratch_shapes=[pltpu.VMEM((H, HG_EXPAND, HG_EXPAND), F32)], comm=comm)
    return outs if comm is None else (*outs, updated, created)


def hgrn_bwd(proj, logits, ng, states, dout, layer, *, name, comm=None):
    S = proj.shape[0]
    D = proj.shape[1] // 4
    H = D // HG_EXPAND
    C = HG_CHUNK
    nc = S // C
    depth = logits.shape[0]
    consts = _hgrn_consts()
    rev = lambda c: nc - 1 - c

    def body(p_ref, lg_ref, ng_ref, ops3_ref, opst3_ref, mk_ref, st_in_ref, do_ref,
             dp_ref, dlg_ref, dng_ref, dst_ref):
        @pl.when(pl.program_id(0) == 0)
        def _():
            dst_ref[...] = jnp.zeros_like(dst_ref)
            dlg_ref[...] = jnp.zeros_like(dlg_ref)
            dng_ref[...] = jnp.zeros_like(dng_ref)

        fn = functools.partial(_hgrn_chunk, ops3=ops3_ref[...], opst3=opst3_ref[...], mk=mk_ref[...], layer=layer)
        _, vjp = jax.vjp(fn, *(p_ref[:, j * D:(j + 1) * D] for j in range(4)), tuple(st_in_ref[h] for h in range(H)),
                         lg_ref[...], ng_ref[...])
        *dproj, dsts, dlg, dng = vjp((do_ref[...], tuple(dst_ref[h] for h in range(H))))
        for j in range(4):
            dp_ref[:, j * D:(j + 1) * D] = dproj[j].astype(BF16)
        for h in range(H):
            dst_ref[h] = dsts[h]
        dlg_ref[...] += dlg
        dng_ref[...] += dng

    in_specs = _hgrn_in_specs(D, depth, rev)
    in_specs += [pl.BlockSpec((None, H, HG_EXPAND, HG_EXPAND), lambda c: (rev(c), 0, 0, 0)),
                 pl.BlockSpec((C, D), lambda c: (rev(c), 0))]
    outs, _, created = _carried_call(
        body, [proj, logits, ng, *consts, states, dout], name=name, grid=(nc,), in_specs=in_specs,
        out_specs=[pl.BlockSpec((C, 4 * D), lambda c: (rev(c), 0)),
                   pl.BlockSpec((depth, D), lambda c: (0, 0)), pl.BlockSpec((1, HG_EXPAND), lambda c: (0, 0))],
        out_shape=[jax.ShapeDtypeStruct((S, 4 * D), BF16), jax.ShapeDtypeStruct((depth, D), F32),
                   jax.ShapeDtypeStruct((1, HG_EXPAND), F32)],
        scratch_shapes=[pltpu.VMEM((H, HG_EXPAND, HG_EXPAND), F32)], comm=comm)
    return outs if comm is None else (*outs, created)


def _shift_down(u, prev, shift):
    rolled = pltpu.roll(u, shift, axis=0)
    rows = lax.broadcasted_iota(jnp.int32, prev.shape, 0)
    top = jnp.where(rows < shift, pltpu.roll(prev, shift, axis=0), rolled[:SUBLANES])
    return jnp.concatenate([top, rolled[SUBLANES:]], axis=0)


def _shift_up(x, nxt, shift):
    tm = x.shape[0]
    rolled = pltpu.roll(x, tm - shift, axis=0)
    rows = lax.broadcasted_iota(jnp.int32, nxt.shape, 0)
    bottom = jnp.where(rows >= SUBLANES - shift, pltpu.roll(nxt, SUBLANES - shift, axis=0), rolled[tm - SUBLANES:])
    return jnp.concatenate([rolled[:tm - SUBLANES], bottom], axis=0)


def _conv(u, prev, w_ref, b_ref, half):
    u1 = _shift_down(u, prev, 1)
    u2 = _shift_down(u, prev, 2)
    c = b_ref[half] + w_ref[half, 0:1, :] * u2 + w_ref[half, 1:2, :] * u1 + w_ref[half, 2:3, :] * u
    return c, u1, u2


def _halo_rows(ref, keep, last):
    full = ref.astype(F32)
    return (full[HALO_ROWS - SUBLANES:] if last else full[:SUBLANES]) * keep


def _conv_specs(tm, tc, l):
    nprev = lambda m: jnp.maximum(m * (tm // HALO_ROWS) - 1, 0)
    tile = pl.BlockSpec((2, tm, tc), lambda j, m: (0, m, j))
    halo = pl.BlockSpec((2, HALO_ROWS, tc), lambda j, m: (0, nprev(m), j))
    wspec = pl.BlockSpec((None, 2, 3, tc), lambda j, m: (l, 0, 0, j))
    bspec = pl.BlockSpec((None, 2, 1, tc), lambda j, m: (l, 0, 0, j))
    return tile, halo, wspec, bspec


def convact_fwd(u, cw, cb, l, *, name):
    _, S, Fh = u.shape
    tm = _pick(S, (256, 128))
    tc = _pick(Fh, (1408, 512, 256, 128))
    tile, halo, wspec, bspec = _conv_specs(tm, tc, l)

    def body(u_ref, p_ref, w_ref, b_ref, o_ref, ot_ref):
        keep = (pl.program_id(1) > 0).astype(F32)
        cg, _, _ = _conv(u_ref[0].astype(F32), _halo_rows(p_ref[0], keep, True), w_ref, b_ref, 0)
        cv, _, _ = _conv(u_ref[1].astype(F32), _halo_rows(p_ref[1], keep, True), w_ref, b_ref, 1)
        act = _silu(cg) * cv
        o_ref[...] = act.astype(BF16)
        ot_ref[...] = act.T.astype(BF16)

    return pl.pallas_call(
        body, name=name, grid=(Fh // tc, S // tm), in_specs=[tile, halo, wspec, bspec],
        out_specs=[pl.BlockSpec((tm, tc), lambda j, m: (m, j)), pl.BlockSpec((tc, tm), lambda j, m: (j, m))],
        out_shape=[jax.ShapeDtypeStruct((S, Fh), BF16), jax.ShapeDtypeStruct((Fh, S), BF16)],
        compiler_params=_params(("parallel", "parallel")),
    )(u, u, cw, cb)


def convact_bwd(u, cw, cb, l, da, *, name):
    _, S, Fh = u.shape
    tm = _pick(S, (256, 128))
    tc = _pick(Fh, (1408, 512, 256, 128))
    tile, halo, wspec, bspec = _conv_specs(tm, tc, l)

    def body(u_ref, p_ref, w_ref, b_ref, da_ref, dc_ref, dw_ref, db_ref):
        first = pl.program_id(1) == 0
        keep = jnp.logical_not(first).astype(F32)
        ug, uv = u_ref[0].astype(F32), u_ref[1].astype(F32)
        cg, g1, g2 = _conv(ug, _halo_rows(p_ref[0], keep, True), w_ref, b_ref, 0)
        cv, v1, v2 = _conv(uv, _halo_rows(p_ref[1], keep, True), w_ref, b_ref, 1)
        da_t = da_ref[...]
        sg = _sigmoid(cg)
        dcv = da_t * (cg * sg)
        dcg = da_t * cv * (sg * (1.0 + cg * (1.0 - sg)))
        dc_ref[0] = dcg.astype(BF16)
        dc_ref[1] = dcv.astype(BF16)

        @pl.when(first)
        def _():
            dw_ref[...] = jnp.zeros_like(dw_ref)
            db_ref[...] = jnp.zeros_like(db_ref)

        rs = lambda t: jnp.sum(t, axis=0, keepdims=True)
        for half, dc, taps in ((0, dcg, (g2, g1, ug)), (1, dcv, (v2, v1, uv))):
            for j in range(3):
                dw_ref[half, j:j + 1, :] += rs(dc * taps[j])
            db_ref[half] += rs(dc)

    return pl.pallas_call(
        body, name=name, grid=(Fh // tc, S // tm),
        in_specs=[tile, halo, wspec, bspec, pl.BlockSpec((tm, tc), lambda j, m: (m, j))],
        out_specs=[tile, pl.BlockSpec((2, 3, tc), lambda j, m: (0, 0, j)),
                   pl.BlockSpec((2, 1, tc), lambda j, m: (0, 0, j))],
        out_shape=[jax.ShapeDtypeStruct(u.shape, BF16), jax.ShapeDtypeStruct((2, 3, Fh), F32),
                   jax.ShapeDtypeStruct((2, 1, Fh), F32)],
        compiler_params=_params(("parallel", "arbitrary")),
    )(u, u, cw, cb, da)


def conv_input_grad(dc, cw, l, *, name):
    _, S, Fh = dc.shape
    tm = _pick(S, (256, 128))
    tc = _pick(Fh, (1408, 512, 256, 128))
    nblk = S // HALO_ROWS
    nm = S // tm

    def body(dc_ref, n_ref, w_ref, du_ref):
        keep = (pl.program_id(1) < nm - 1).astype(F32)
        for half in range(2):
            t = dc_ref[half].astype(F32)
            nxt = _halo_rows(n_ref[half], keep, False)
            du = (w_ref[half, 2:3, :] * t + w_ref[half, 1:2, :] * _shift_up(t, nxt, 1)
                  + w_ref[half, 0:1, :] * _shift_up(t, nxt, 2))
            du_ref[half] = du.astype(BF16)

    tile = pl.BlockSpec((2, tm, tc), lambda j, m: (0, m, j))
    halo = pl.BlockSpec((2, HALO_ROWS, tc),
                        lambda j, m: (0, jnp.minimum((m + 1) * (tm // HALO_ROWS), nblk - 1), j))
    return pl.pallas_call(
        body, name=name, grid=(Fh // tc, nm),
        in_specs=[tile, halo, pl.BlockSpec((None, 2, 3, tc), lambda j, m: (l, 0, 0, j))],
        out_specs=tile, out_shape=jax.ShapeDtypeStruct(dc.shape, BF16),
        compiler_params=_params(("parallel", "parallel")),
    )(dc, dc, cw)


def _adamw_math(w, g, m, v):
    m = ADAM_B1 * m + (1.0 - ADAM_B1) * g
    v = ADAM_B2 * v + (1.0 - ADAM_B2) * (g * g)
    m_hat = m / (1.0 - ADAM_B1 ** ADAM_STEP)
    v_hat = v / (1.0 - ADAM_B2 ** ADAM_STEP)
    delta = -ADAM_LR * (m_hat / (jnp.sqrt(v_hat) + ADAM_EPS) + ADAM_WD * w)
    return delta, m, v


def _row_tile(R, Ccols, n_streams):
    for tr in (512, 256, 128, 64, 32, 16, 8):
        if R % tr == 0 and tr * Ccols * 4 * n_streams * 2 <= VMEM_LIMIT_BYTES // 2:
            return tr
    return R


def adamw(w, g, m, v, *, name):
    R, Ccols = w.shape
    tr = _row_tile(R, Ccols, 7)

    def body(w_ref, g_ref, m_ref, v_ref, d_ref, mo_ref, vo_ref):
        d, mn, vn = _adamw_math(w_ref[...], g_ref[...], m_ref[...], v_ref[...])
        d_ref[...] = d
        mo_ref[...] = mn
        vo_ref[...] = vn

    spec = pl.BlockSpec((tr, Ccols), lambda i: (i, 0))
    shp = jax.ShapeDtypeStruct((R, Ccols), F32)
    return pl.pallas_call(
        body, name=name, grid=(R // tr,), in_specs=[spec] * 4, out_specs=[spec] * 3,
        out_shape=[shp, shp, shp], compiler_params=_params(("parallel",)),
    )(w, g, m, v)


def _prefetch_call(body, pos, args, *, name, grid, in_specs, out_specs, out_shape, sem, aliases=None):
    return pl.pallas_call(
        lambda pos_ref, *refs: body(*refs), name=name,
        grid_spec=pltpu.PrefetchScalarGridSpec(num_scalar_prefetch=1, grid=grid, in_specs=in_specs,
                                               out_specs=out_specs),
        out_shape=out_shape, input_output_aliases={1 + i: o for i, o in (aliases or {}).items()},
        compiler_params=_params(sem),
    )(pos, *args)


def cast_place(w, pos, *, row_sharded, name):
    L, R, Ccols = w.shape

    def body(w_ref, o_ref):
        o_ref[...] = w_ref[...].astype(BF16)

    if row_sharded:
        grid = (L,)
        in_spec = pl.BlockSpec((None, R, Ccols), lambda l, pos: (l, 0, 0))
        out_spec = pl.BlockSpec((None, R, Ccols), lambda l, pos: (l, pos[0], 0))
        out_shape = jax.ShapeDtypeStruct((L, N_CHIPS * R, Ccols), BF16)
        sem = ("parallel",)
    else:
        tr = _pick(R, (512, 256, 128))
        grid = (L, R // tr)
        in_spec = pl.BlockSpec((None, tr, Ccols), lambda l, i, pos: (l, i, 0))
        out_spec = pl.BlockSpec((None, tr, Ccols), lambda l, i, pos: (l, i, pos[0]))
        out_shape = jax.ShapeDtypeStruct((L, R, N_CHIPS * Ccols), BF16)
        sem = ("parallel", "parallel")
    return _prefetch_call(body, pos, [w], name=name, grid=grid, in_specs=[in_spec], out_specs=out_spec,
                          out_shape=out_shape, sem=sem)


def pair_add(g, theirs, pos, *, row_sharded, name):
    def body(g_ref, t_ref, o_ref, ob_ref):
        s = g_ref[...] + t_ref[...]
        o_ref[...] = s
        ob_ref[...] = s.astype(BF16)

    if row_sharded:
        L, _, _, r2, D = g.shape
        grid = (L, N_CHIPS)
        g_spec = pl.BlockSpec((None, None, None, r2, D), lambda l, q, pos: (l, q, pos[1], 0, 0))
        t_spec = pl.BlockSpec((None, None, r2, D), lambda l, q, pos: (l, q, 0, 0))
    else:
        L, K, N = g.shape
        n = N // N_CHIPS
        tr = _pick(K // 2, (256, 128))
        nb = (K // 2) // tr
        grid = (L, nb, N_CHIPS)
        g_spec = pl.BlockSpec((None, tr, n), lambda l, i, j, pos: (l, pos[1] * nb + i, j))
        t_spec = pl.BlockSpec((None, tr, n), lambda l, i, j, pos: (l, i, j))
    return _prefetch_call(body, pos, [g, theirs], name=name, grid=grid, in_specs=[g_spec, t_spec],
                          out_specs=[t_spec, t_spec],
                          out_shape=[jax.ShapeDtypeStruct(theirs.shape, F32), jax.ShapeDtypeStruct(theirs.shape, BF16)],
                          sem=("parallel",) * len(grid))


def chip_add(pair, landed, pos, buf, li, n_layers, *, row_sharded, name):
    def body(p_ref, a_ref, b_ref, c_ref, *rest):
        rest[-1][...] = ((p_ref[...] + a_ref[...].astype(F32)) + b_ref[...].astype(F32)) + c_ref[...].astype(F32)

    if row_sharded:
        _, _, r2, D = pair.shape
        grid = (1,)
        p_spec = pl.BlockSpec((None, None, r2, D), lambda i, pos: (0, pos[0], 0, 0))
        l_specs = [pl.BlockSpec((None, r2, D), functools.partial(lambda i, pos, k: (k, 0, 0), k=k)) for k in range(3)]
        out_spec = pl.BlockSpec((None, None, r2, D), lambda i, pos: (li, pos[1], 0, 0))
        out_shape = jax.ShapeDtypeStruct((n_layers, 2, r2, D), F32)
    else:
        _, K2, N = pair.shape
        n = N // N_CHIPS
        tr = _pick(K2, (256, 128))
        nb = K2 // tr
        grid = (nb,)
        p_spec = pl.BlockSpec((None, tr, n), lambda i, pos: (0, i, pos[0]))
        l_specs = [pl.BlockSpec((None, tr, n), functools.partial(lambda i, pos, k: (k, i, 0), k=k)) for k in range(3)]
        out_spec = pl.BlockSpec((None, tr, n), lambda i, pos: (li, pos[1] * nb + i, 0))
        out_shape = jax.ShapeDtypeStruct((n_layers, 2 * K2, n), F32)
    args, in_specs, aliases = [pair, landed, landed, landed], [p_spec] + l_specs, None
    if buf is not None:
        args.append(buf)
        in_specs.append(pl.BlockSpec(memory_space=pl.ANY))
        aliases = {4: 0}
    return _prefetch_call(body, pos, args, name=name, grid=grid, in_specs=in_specs, out_specs=out_spec,
                          out_shape=out_shape, sem=("parallel",), aliases=aliases)


def _position():
    return lax.axis_index("x"), lax.axis_index("y"), lax.axis_index("c")


def _other_chips(x, y):
    return [(1 - x, y), (x, 1 - y), (1 - x, 1 - y)]


HBM_ANY = pl.BlockSpec(memory_space=pl.ANY)
COMM_PARAMS = pltpu.CompilerParams(has_side_effects=True)


class Comm:
    def __init__(self, ro=(), inout=(), new=(), sems=(), phases=()):
        self.ro, self.inout, self.new, self.sems, self.phases = list(ro), list(inout), list(new), list(sems), list(phases)


def _carried_call(body, args, *, name, grid, in_specs, out_specs, out_shape, scratch_shapes=(), comm=None):
    comm = comm or Comm()
    n_in, n_out, n_sc = len(in_specs), len(out_shape), len(scratch_shapes)
    n_ro, n_io, n_new, n_sem = len(comm.ro), len(comm.inout), len(comm.new), len(comm.sems)
    last = int(np.prod(grid)) - 1

    def wrapped(*refs):
        bounds = np.cumsum([0, n_in, n_ro, n_io, n_out, n_io, n_new, n_sc, n_sem])
        core_in, ro, _, core_out, io, new, core_sc, sems = (refs[a:b] for a, b in zip(bounds[:-1], bounds[1:]))
        step = pl.program_id(0)
        for d in range(1, len(grid)):
            step = step * grid[d] + pl.program_id(d)
        for frac, fn in comm.phases[:-1]:
            pl.when(step == int(frac * last))(functools.partial(fn, ro, io, new, sems))
        body(*core_in, *core_out, *core_sc)
        if comm.phases:
            pl.when(step == last)(functools.partial(comm.phases[-1][1], ro, io, new, sems))

    outs = pl.pallas_call(
        wrapped, name=name, grid=grid,
        in_specs=list(in_specs) + [HBM_ANY] * (n_ro + n_io),
        out_specs=list(out_specs) + [HBM_ANY] * (n_io + n_new),
        out_shape=list(out_shape) + [jax.ShapeDtypeStruct(a.shape, a.dtype) for a in comm.inout] + comm.new,
        input_output_aliases={n_in + n_ro + j: n_out + j for j in range(n_io)},
        scratch_shapes=list(scratch_shapes) + comm.sems,
        compiler_params=pltpu.CompilerParams(dimension_semantics=("arbitrary",) * len(grid),
                                             vmem_limit_bytes=VMEM_LIMIT_BYTES, has_side_effects=bool(comm.phases)),
    )(*args, *comm.ro, *comm.inout)
    return outs[:n_out], outs[n_out:n_out + n_io], outs[n_out + n_io:]


def run_comm(comm, *, name):
    n_ro, n_io, n_new = len(comm.ro), len(comm.inout), len(comm.new)

    def body(*refs):
        bounds = np.cumsum([0, n_ro, n_io, n_io, n_new, len(comm.sems)])
        ro, _, io, new, sems = (refs[a:b] for a, b in zip(bounds[:-1], bounds[1:]))
        for _, fn in comm.phases:
            fn(ro, io, new, sems)

    outs = pl.pallas_call(
        body, name=name, in_specs=[HBM_ANY] * (n_ro + n_io), out_specs=[HBM_ANY] * (n_io + n_new),
        out_shape=[jax.ShapeDtypeStruct(a.shape, a.dtype) for a in comm.inout] + comm.new,
        input_output_aliases={n_ro + j: j for j in range(n_io)},
        scratch_shapes=comm.sems, compiler_params=COMM_PARAMS,
    )(*comm.ro, *comm.inout)
    return outs[:n_io], outs[n_io:]


def _rdma(src, dst, send_sem, recv_sem, device):
    return pltpu.make_async_remote_copy(src_ref=src, dst_ref=dst, send_sem=send_sem, recv_sem=recv_sem,
                                        device_id=device, device_id_type=MESH)


def gather_comm(bufs, slices):
    n = len(slices)

    def region(io, s, h, chip):
        b, li, row_sharded = slices[s]
        ref = io[b]
        if row_sharded:
            r = ref.shape[1] // N_CHIPS
            return ref.at[li, pl.ds(chip * r + h * (r // 2), r // 2), :]
        K, nn = ref.shape[1], ref.shape[2] // N_CHIPS
        return ref.at[li, pl.ds(h * (K // 2), K // 2), pl.ds(chip * nn, nn)]

    def copies(io, sems):
        x, y, c = _position()
        me = 2 * x + y
        chips = _other_chips(x, y)
        ids = [2 * cx + cy for cx, cy in chips]

        def ici(s, k, src):
            return _rdma(region(io, s, c, me), region(io, s, c, src), sems[0].at[s, k], sems[1].at[s, k], (*chips[k], c))

        def d2d(s, k, h):
            return _rdma(region(io, s, h, ids[k]), region(io, s, h, ids[k]), sems[2].at[s, k], sems[3].at[s, k],
                         (x, y, 1 - c))

        return me, ids, c, ici, d2d

    pairs = [(s, k) for s in range(n) for k in range(3)]

    def start(ro, io, new, sems):
        me, _, _, ici, _ = copies(io, sems)
        for s, k in pairs:
            ici(s, k, me).start()

    def pass_on(ro, io, new, sems):
        _, ids, c, ici, d2d = copies(io, sems)
        for s, k in pairs:
            ici(s, k, ids[k]).wait_recv()
            d2d(s, k, c).start()

    def finish(ro, io, new, sems):
        me, _, c, ici, d2d = copies(io, sems)
        for s, k in pairs:
            d2d(s, k, 1 - c).wait_recv()
        for s, k in pairs:
            ici(s, k, me).wait_send()
            d2d(s, k, c).wait_send()

    return Comm(inout=bufs, sems=[pltpu.SemaphoreType.DMA((n, 3))] * 4,
                phases=[(0.0, start), (0.6, pass_on), (1.0, finish)])


def gather_small(shard, *, name):
    n = shard.shape[2]

    def body(s_ref, o_ref, send_sem, recv_sem, local_sem):
        x, y, c = _position()
        me = 2 * x + y
        chips = _other_chips(x, y)
        place = lambda chip: o_ref.at[:, :, pl.ds(chip * n, n)]
        local = pltpu.make_async_copy(s_ref, place(me), local_sem)
        local.start()
        for k in range(3):
            _rdma(s_ref, place(me), send_sem.at[k], recv_sem.at[k], (*chips[k], c)).start()
        for k in range(3):
            _rdma(s_ref, place(2 * chips[k][0] + chips[k][1]), send_sem.at[k], recv_sem.at[k], (*chips[k], c)).wait()
        local.wait()

    return pl.pallas_call(
        body, name=name, in_specs=[HBM_ANY], out_specs=HBM_ANY,
        out_shape=jax.ShapeDtypeStruct((shard.shape[0], shard.shape[1], N_CHIPS * n), shard.dtype),
        scratch_shapes=[pltpu.SemaphoreType.DMA((3,)), pltpu.SemaphoreType.DMA((3,)), pltpu.SemaphoreType.DMA],
        compiler_params=COMM_PARAMS,
    )(shard)


def swap_comm(grads):
    def copies(ro, new, sems):
        x, y, c = _position()
        h = 1 - c
        cps = []
        for s, (a, row_sharded) in enumerate(grads):
            if row_sharded:
                r = a.shape[1] // N_CHIPS
                for q in range(N_CHIPS):
                    cps.append(_rdma(ro[s].at[:, pl.ds(q * r + h * (r // 2), r // 2), :], new[s].at[:, q],
                                     sems[0].at[s, q], sems[1].at[s, q], (x, y, 1 - c)))
            else:
                K = a.shape[1]
                cps.append(_rdma(ro[s].at[:, pl.ds(h * (K // 2), K // 2), :], new[s],
                                 sems[0].at[s, 0], sems[1].at[s, 0], (x, y, 1 - c)))
        return cps

    def start(ro, io, new, sems):
        for cp in copies(ro, new, sems):
            cp.start()

    def finish(ro, io, new, sems):
        for cp in copies(ro, new, sems):
            cp.wait()

    theirs = []
    for a, row_sharded in grads:
        if row_sharded:
            theirs.append(jax.ShapeDtypeStruct((1, N_CHIPS, a.shape[1] // N_CHIPS // 2, a.shape[2]), a.dtype))
        else:
            theirs.append(jax.ShapeDtypeStruct((1, a.shape[1] // 2, a.shape[2]), a.dtype))
    return Comm(ro=[a for a, _ in grads], new=theirs, sems=[pltpu.SemaphoreType.DMA((len(grads), N_CHIPS))] * 2,
                phases=[(0.0, start), (1.0, finish)])


def merge_comms(comms):
    spans, ro, io, new, sems = [], [], [], [], []
    for cm in comms:
        spans.append((len(ro), len(io), len(new), len(sems)))
        ro, io, new, sems = ro + cm.ro, io + cm.inout, new + cm.new, sems + cm.sems

    def phase(frac):
        def run(R, I, N, S):
            for cm, (a, b, d, e) in zip(comms, spans):
                for f, fn in cm.phases:
                    if f == frac:
                        fn(R[a:a + len(cm.ro)], I[b:b + len(cm.inout)], N[d:d + len(cm.new)], S[e:e + len(cm.sems)])
        return run

    fracs = sorted({f for cm in comms for f, _ in cm.phases})
    merged = Comm(ro=ro, inout=io, new=new, sems=sems, phases=[(f, phase(f)) for f in fracs])
    cut = lambda created: [created[d:d + len(cm.new)] for cm, (_, _, d, _) in zip(comms, spans)]
    return merged, cut


def scatter_comm(halves):
    def copies(ro, new, sems):
        x, y, c = _position()
        chips = _other_chips(x, y)
        cps = []
        for s, (a, row_sharded) in enumerate(halves):
            for k in range(3):
                to = 2 * chips[k][0] + chips[k][1]
                if row_sharded:
                    src = ro[s].at[0, to]
                else:
                    n = a.shape[2] // N_CHIPS
                    src = ro[s].at[0, :, pl.ds(to * n, n)]
                cps.append(_rdma(src, new[s].at[k], sems[0].at[s, k], sems[1].at[s, k], (*chips[k], c)))
        return cps

    def start(ro, io, new, sems):
        for cp in copies(ro, new, sems):
            cp.start()

    def finish(ro, io, new, sems):
        for cp in copies(ro, new, sems):
            cp.wait()

    landing = [jax.ShapeDtypeStruct((3, a.shape[2], a.shape[3]) if row_sharded else
                                    (3, a.shape[1], a.shape[2] // N_CHIPS), a.dtype) for a, row_sharded in halves]
    return Comm(ro=[a for a, _ in halves], new=landing, sems=[pltpu.SemaphoreType.DMA((len(halves), 3))] * 2,
                phases=[(0.0, start), (1.0, finish)])


def sibling_join_halves(shards, *, name):
    n_in = len(shards)

    def body(*refs):
        ins = refs[:n_in]
        outs = refs[n_in:2 * n_in]
        send_sem, recv_sem = refs[2 * n_in:]
        x, y, c = _position()
        sibling = (x, y, 1 - c)

        def rows(ref, h):
            R2 = ref.shape[1] // 2
            return ref.at[:, pl.ds(h * R2, R2), :]

        sends = [pltpu.make_async_remote_copy(src_ref=rows(ins[i], c), dst_ref=rows(outs[i], c),
                                              send_sem=send_sem.at[i], recv_sem=recv_sem.at[i],
                                              device_id=sibling, device_id_type=MESH) for i in range(n_in)]
        lands = [pltpu.make_async_remote_copy(src_ref=rows(ins[i], c), dst_ref=rows(outs[i], 1 - c),
                                              send_sem=send_sem.at[i], recv_sem=recv_sem.at[i],
                                              device_id=sibling, device_id_type=MESH) for i in range(n_in)]
        for cp in sends:
            cp.start()
        for i in range(n_in):
            sends[i].wait_send()
            lands[i].wait_recv()

    return pl.pallas_call(
        body, name=name, in_specs=[HBM_ANY] * n_in, out_specs=[HBM_ANY] * n_in,
        out_shape=[jax.ShapeDtypeStruct(a.shape, a.dtype) for a in shards],
        input_output_aliases={i: i for i in range(n_in)},
        scratch_shapes=[pltpu.SemaphoreType.DMA((n_in,)), pltpu.SemaphoreType.DMA((n_in,))],
        compiler_params=COMM_PARAMS,
    )(*shards)


def all_reduce_small(v, *, name):
    R, Ccols = v.shape

    def body(v_ref, o_ref, slots, send_sem, recv_sem):
        x, y, c = _position()
        me = 4 * x + 2 * y + c
        slots[me] = v_ref[...]

        def peer(d):
            return x ^ (d >> 2), y ^ ((d >> 1) & 1), c ^ (d & 1)

        def copy(d, slot):
            return pltpu.make_async_remote_copy(
                src_ref=v_ref, dst_ref=slots.at[slot], send_sem=send_sem.at[d], recv_sem=recv_sem.at[d],
                device_id=peer(d), device_id_type=MESH)

        sends = [copy(d, me) for d in range(1, N_DEV)]
        for cp in sends:
            cp.start()
        for d in range(1, N_DEV):
            px, py, pc = peer(d)
            copy(d, 4 * px + 2 * py + pc).wait_recv()
        for cp in sends:
            cp.wait_send()
        acc = slots[0]
        for d in range(1, N_DEV):
            acc = acc + slots[d]
        o_ref[...] = acc

    return pl.pallas_call(
        body, name=name,
        in_specs=[pl.BlockSpec(memory_space=pltpu.VMEM)], out_specs=pl.BlockSpec(memory_space=pltpu.VMEM),
        out_shape=jax.ShapeDtypeStruct((R, Ccols), F32),
        scratch_shapes=[pltpu.VMEM((N_DEV, R, Ccols), F32), pltpu.SemaphoreType.DMA((N_DEV,)),
                        pltpu.SemaphoreType.DMA((N_DEV,))],
        compiler_params=COMM_PARAMS,
    )(v)


def _pack_small(parts):
    flat = jnp.concatenate([p.reshape(-1).astype(F32) for p in parts])
    n = flat.shape[0]
    rows = -(-n // LANES)
    rows = -(-rows // SUBLANES) * SUBLANES
    return jnp.pad(flat, (0, rows * LANES - n)).reshape(rows, LANES)


def _unpack_small(packed, like):
    flat = packed.reshape(-1)
    out, off = [], 0
    for p in like:
        out.append(flat[off:off + p.size].reshape(p.shape))
        off += p.size
    return out


def kernel(x, norm_mix, norm_ffn, norm_final, attn_w_in, attn_w_out, attn_sinks, hgrn_w_in, hgrn_w_out, hgrn_norm, hgrn_lb_logits, ffn_w_up, ffn_conv_w, ffn_conv_b, ffn_w_down, loss_target, m_norm_mix, m_norm_ffn, m_norm_final, m_attn_w_in, m_attn_w_out, m_attn_sinks, m_hgrn_w_in, m_hgrn_w_out, m_hgrn_norm, m_hgrn_lb_logits, m_ffn_w_up, m_ffn_conv_w, m_ffn_conv_b, m_ffn_w_down, v_norm_mix, v_norm_ffn, v_norm_final, v_attn_w_in, v_attn_w_out, v_attn_sinks, v_hgrn_w_in, v_hgrn_w_out, v_hgrn_norm, v_hgrn_lb_logits, v_ffn_w_up, v_ffn_conv_w, v_ffn_conv_b, v_ffn_w_down):
    S, D = x.shape[1], x.shape[2]
    depth = norm_mix.shape[0]
    n_q = D // HEAD_DIM
    n_kv = n_q // Q_PER_KV
    Fh = ffn_w_down.shape[1] * N_CHIPS
    h = x.reshape(S, D)
    target = loss_target.reshape(S, D)

    col_names = ["attn_w_in", "hgrn_w_in", "ffn_w_up"]
    row_names = ["attn_w_out", "hgrn_w_out", "ffn_w_down"]
    x_, y_, c_ = _position()
    me = 2 * x_ + y_
    pos = jnp.stack([me, c_]).astype(jnp.int32)
    col_w = [cast_place(w, pos, row_sharded=False, name=f"cast_place_{n}")
             for n, w in zip(col_names, (attn_w_in, hgrn_w_in, ffn_w_up))]
    row_w = [cast_place(w, pos, row_sharded=True, name=f"cast_place_{n}")
             for n, w in zip(row_names, (attn_w_out, hgrn_w_out, ffn_w_down))]
    W = dict(zip(col_names + row_names, col_w + row_w))
    is_row = {n: n in row_names for n in W}

    def layer_weights(layer):
        mixer = "attn" if layer % 2 == 0 else "hgrn"
        return [(f"{mixer}_w_in", layer // 2), (f"{mixer}_w_out", layer // 2), ("ffn_w_up", layer), ("ffn_w_down", layer)]

    def gather_of(group):
        names = [n for n, _ in group]
        return names, gather_comm([W[n] for n in names], [(i, li, is_row[n]) for i, (n, li) in enumerate(group)])

    def mixer_weights(layer):
        return layer_weights(layer)[:2] if layer < depth else []

    names_next, comm_next = gather_of(mixer_weights(0))
    filled, _ = run_comm(comm_next, name="gather_mixer0")
    W.update(zip(names_next, filled))
    cw = gather_small(ffn_conv_w, name="gather_conv_w").reshape(depth, 3, 2, Fh).transpose(0, 2, 1, 3)
    cb = ffn_conv_b.reshape(depth, 2, 1, Fh)
    g_mix = norm_mix.reshape(depth, 1, D)
    g_ffn = norm_ffn.reshape(depth, 1, D)

    slopes = jnp.exp2(-8.0 * jnp.arange(1, n_q + 1, dtype=F32) / n_q).reshape(n_kv, Q_PER_KV)
    slope_col = jnp.repeat(slopes, WINDOW, axis=1).reshape(n_kv, Q_PER_KV * WINDOW, 1)

    def to_heads(t, nh):
        return t.reshape(S, nh, HEAD_DIM).transpose(1, 0, 2)

    def from_heads(t):
        return t.transpose(1, 0, 2).reshape(S, -1)

    saved = []
    for layer in range(depth):
        idx = layer // 2
        rec = {"h_mix": h}
        hn, rec["hn_t"] = rmsnorm_fwd(h, g_mix, layer, name=f"norm_mix_fwd{layer}")
        is_attn = layer % 2 == 0
        names_next, comm_next = gather_of(layer_weights(layer)[2:] + ([] if is_attn else mixer_weights(layer + 1)))
        if is_attn:
            proj = mm_nn(hn, W["attn_w_in"], idx, name=f"attn_in_fwd{layer}", out_dtype=BF16)
            q = to_heads(proj[:, :n_q * HEAD_DIM], n_q).reshape(n_kv, Q_PER_KV, S, HEAD_DIM)
            k = to_heads(proj[:, n_q * HEAD_DIM:(n_q + n_kv) * HEAD_DIM], n_kv)
            v = to_heads(proj[:, (n_q + n_kv) * HEAD_DIM:], n_kv)
            sink_col = jnp.repeat(attn_sinks[idx].reshape(n_kv, Q_PER_KV), WINDOW, axis=1)
            sink_col = sink_col.reshape(n_kv, Q_PER_KV * WINDOW, 1)
            o_heads, filled, _ = attn_fwd(q, k, v, sink_col, slope_col, name=f"attn_fwd{layer}", comm=comm_next)
            W.update(zip(names_next, filled))
            o = from_heads(o_heads.reshape(n_q, S, HEAD_DIM))
            rec.update(q=q, k=k, v=v, sink_col=sink_col, o=o)
            h = mm_nn(o, W["attn_w_out"], idx, name=f"attn_out_fwd{layer}", res=h)
        else:
            proj = mm_nn(hn, W["hgrn_w_in"], idx, name=f"hgrn_in_fwd{layer}")
            ng = hgrn_norm[idx].reshape(1, HG_EXPAND)
            o, states, filled, _ = hgrn_fwd(proj, hgrn_lb_logits, ng, layer, name=f"hgrn_fwd{layer}", comm=comm_next)
            W.update(zip(names_next, filled))
            rec.update(proj=proj, ng=ng, states=states, o=o)
            h = mm_nn(o, W["hgrn_w_out"], idx, name=f"hgrn_out_fwd{layer}", res=h)
        rec["h_ffn"] = h
        hn2, rec["hn2_t"] = rmsnorm_fwd(h, g_ffn, layer, name=f"norm_ffn_fwd{layer}")
        if is_attn and layer + 1 < depth:
            names_next, comm_next = gather_of(mixer_weights(layer + 1))
            u, filled = mm_nn(hn2, W["ffn_w_up"], layer, name=f"ffn_up_fwd{layer}", split_out=True, out_dtype=BF16,
                              comm=comm_next)
            W.update(zip(names_next, filled))
        else:
            u = mm_nn(hn2, W["ffn_w_up"], layer, name=f"ffn_up_fwd{layer}", split_out=True, out_dtype=BF16)
        act, rec["act_t"] = convact_fwd(u, cw, cb, layer, name=f"ffn_act_fwd{layer}")
        rec["u"] = u
        h = mm_nn(act, W["ffn_w_down"], layer, name=f"ffn_down_fwd{layer}", res=h)
        saved.append(rec)

    loss_part, dh, d_norm_final = loss_head(h, norm_final.reshape(1, D), target, name="loss_head")

    n_attn, n_hgrn = attn_w_in.shape[0], hgrn_w_in.shape[0]
    reduced_parts = []
    pending = []

    def swap_of(group, g):
        ordered = sorted(group, key=lambda nl: is_row[nl[0]])
        return ordered, swap_comm([(g[n], is_row[n]) for n, _ in ordered])

    def pair_up(tag, ordered, g, theirs):
        pair, pair_bf16 = {}, {}
        for (n, _), t in zip(ordered, theirs):
            gn = g[n]
            if is_row[n]:
                r = gn.shape[1] // N_CHIPS
                gn = gn.reshape(1, N_CHIPS, 2, r // 2, gn.shape[2])
            pair[n], pair_bf16[n] = pair_add(gn, t, pos, row_sharded=is_row[n], name=f"rs_pair_add_{n}{tag}")
        pending.append((ordered, pair, scatter_comm([(pair_bf16[n], is_row[n]) for n, _ in ordered])))

    def landed_behind(created, cut):
        for (ordered, pair, _), landed in zip(pending, cut(created)):
            reduced_parts.append((ordered, pair, landed))
        pending.clear()

    d_norm_mix, d_norm_ffn = [None] * depth, [None] * depth
    d_conv_w, d_conv_b = [None] * depth, [None] * depth
    d_sinks, d_hgrn_norm = [None] * n_attn, [None] * n_hgrn
    d_logits = jnp.zeros_like(hgrn_lb_logits)
    for layer in reversed(range(depth)):
        idx = layer // 2
        rec = saved[layer]
        g = {"ffn_w_down": mm_dw(rec["act_t"], dh, name=f"ffn_down_dw{layer}")}
        dact = mm_nt(dh, W["ffn_w_down"], layer, name=f"ffn_down_dx{layer}")
        dc, dcw, dcb = convact_bwd(rec["u"], cw, cb, layer, dact, name=f"ffn_act_bwd{layer}")
        d_conv_w[layer] = dcw.transpose(1, 0, 2).reshape(3, 2 * Fh)
        d_conv_b[layer] = dcb.reshape(2 * Fh)
        du = conv_input_grad(dc, cw, layer, name=f"ffn_conv_dx{layer}")
        g["ffn_w_up"] = mm_dw(rec["hn2_t"], du, name=f"ffn_up_dw{layer}", split_b=True)
        ordered, swap = swap_of(layer_weights(layer)[2:], g)
        dhn2, theirs = mm_nt(du, W["ffn_w_up"], layer, name=f"ffn_up_dx{layer}", split_in=True, comm=swap)
        pair_up(layer, ordered, g, theirs)
        dh, dg = rmsnorm_bwd(rec["h_ffn"], g_ffn, layer, dhn2, dh, name=f"norm_ffn_bwd{layer}")
        d_norm_ffn[layer] = dg.reshape(D)
        carried, cut = merge_comms([p[2] for p in pending])
        if layer % 2 == 0:
            g["attn_w_out"] = mm_tn(rec["o"], dh, name=f"attn_out_dw{layer}")
            do = mm_nt(dh, W["attn_w_out"], idx, name=f"attn_out_dx{layer}", out_dtype=BF16)
            do_heads = to_heads(do, n_q).reshape(n_kv, Q_PER_KV, S, HEAD_DIM)
            res = attn_bwd(rec["q"], rec["k"], rec["v"], rec["sink_col"], slope_col, do_heads,
                           name=f"attn_bwd{layer}", comm=carried)
            dq, dk, dv, dsink = res[:4]
            landed_behind(res[-1], cut)
            dproj = jnp.concatenate([from_heads(dq.reshape(n_q, S, HEAD_DIM)), from_heads(dk), from_heads(dv)], axis=1)
            d_sinks[idx] = jnp.sum(dsink.reshape(n_kv, Q_PER_KV, WINDOW), axis=-1).reshape(n_q)
            g["attn_w_in"] = mm_dw(rec["hn_t"], dproj, name=f"attn_in_dw{layer}")
            ordered, swap = swap_of(layer_weights(layer)[:2], g)
            dhn, theirs = mm_nt(dproj, W["attn_w_in"], idx, name=f"attn_in_dx{layer}", comm=swap)
        else:
            g["hgrn_w_out"] = mm_tn(rec["o"], dh, name=f"hgrn_out_dw{layer}")
            do = mm_nt(dh, W["hgrn_w_out"], idx, name=f"hgrn_out_dx{layer}")
            res = hgrn_bwd(rec["proj"], hgrn_lb_logits, rec["ng"], rec["states"], do, layer,
                           name=f"hgrn_bwd{layer}", comm=carried)
            dproj, dlg, dng = res[:3]
            landed_behind(res[-1], cut)
            d_logits = d_logits + dlg
            d_hgrn_norm[idx] = dng.reshape(HG_EXPAND)
            g["hgrn_w_in"] = mm_dw(rec["hn_t"], dproj, name=f"hgrn_in_dw{layer}")
            ordered, swap = swap_of(layer_weights(layer)[:2], g)
            dhn, theirs = mm_nt(dproj, W["hgrn_w_in"], idx, name=f"hgrn_in_dx{layer}", comm=swap)
        pair_up(layer, ordered, g, theirs)
        dh, dg = rmsnorm_bwd(rec["h_mix"], g_mix, layer, dhn, dh, name=f"norm_mix_bwd{layer}")
        d_norm_mix[layer] = dg.reshape(D)
    grad_x = dh.reshape(x.shape)
    carried, cut = merge_comms([p[2] for p in pending])
    landed_behind(run_comm(carried, name="rs_scatter_last")[1], cut)

    small_w = [norm_mix, norm_ffn, norm_final, attn_sinks, hgrn_norm, hgrn_lb_logits, ffn_conv_b]
    small_m = [m_norm_mix, m_norm_ffn, m_norm_final, m_attn_sinks, m_hgrn_norm, m_hgrn_lb_logits, m_ffn_conv_b]
    small_v = [v_norm_mix, v_norm_ffn, v_norm_final, v_attn_sinks, v_hgrn_norm, v_hgrn_lb_logits, v_ffn_conv_b]
    small_grads = [jnp.stack(d_norm_mix), jnp.stack(d_norm_ffn), d_norm_final.reshape(D), jnp.stack(d_sinks),
                   jnp.stack(d_hgrn_norm), d_logits, jnp.stack(d_conv_b)]
    reduced = all_reduce_small(_pack_small(small_grads + [loss_part]), name="all_reduce_small")
    small_g = _unpack_small(reduced, small_w + [loss_part])
    loss = small_g.pop().reshape(())
    pad = [jnp.zeros((1, 1), F32)]
    sd, sm_, sv_ = adamw(_pack_small(small_w + pad), reduced, _pack_small(small_m + pad),
                         _pack_small(small_v + pad), name="adamw_small")
    small_d = _unpack_small(sd, small_w)
    small_nm = _unpack_small(sm_, small_w)
    small_nv = _unpack_small(sv_, small_w)

    big_names = col_names + row_names
    big_w = dict(attn_w_in=attn_w_in, hgrn_w_in=hgrn_w_in, ffn_w_up=ffn_w_up, attn_w_out=attn_w_out,
                 hgrn_w_out=hgrn_w_out, ffn_w_down=ffn_w_down)
    half_shard = dict.fromkeys(big_names)
    for ordered, pair, landed in reduced_parts:
        for (n, li), ld in zip(ordered, landed):
            half_shard[n] = chip_add(pair[n], ld, pos, half_shard[n], li, big_w[n].shape[0], row_sharded=is_row[n],
                                     name=f"rs_chip_add_{n}{li}")
    shards = sibling_join_halves([half_shard[n].reshape(big_w[n].shape) for n in big_names], name="rs_sibling_join")
    big_m = dict(attn_w_in=m_attn_w_in, hgrn_w_in=m_hgrn_w_in, ffn_w_up=m_ffn_w_up, attn_w_out=m_attn_w_out,
                 hgrn_w_out=m_hgrn_w_out, ffn_w_down=m_ffn_w_down)
    big_v = dict(attn_w_in=v_attn_w_in, hgrn_w_in=v_hgrn_w_in, ffn_w_up=v_ffn_w_up, attn_w_out=v_attn_w_out,
                 hgrn_w_out=v_hgrn_w_out, ffn_w_down=v_ffn_w_down)
    grads, deltas, new_m, new_v = {}, {}, {}, {}
    for name_, gshard in zip(big_names, shards):
        w_ = big_w[name_]
        cols = w_.shape[-1]
        d_, m_, v_ = adamw(w_.reshape(-1, cols), gshard.reshape(-1, cols), big_m[name_].reshape(-1, cols),
                           big_v[name_].reshape(-1, cols), name=f"adamw_{name_}")
        grads[name_] = gshard.reshape(w_.shape)
        deltas[name_], new_m[name_], new_v[name_] = (t.reshape(w_.shape) for t in (d_, m_, v_))

    gconv = jnp.stack(d_conv_w)
    n_conv = ffn_conv_w.shape[2]
    gconv_all = _unpack_small(all_reduce_small(_pack_small([gconv]), name="all_reduce_conv_w"), [gconv])[0]
    gconv_mine = lax.dynamic_slice_in_dim(gconv_all, me * n_conv, n_conv, axis=2)
    cshape = ffn_conv_w.shape
    d_, m_, v_ = adamw(ffn_conv_w.reshape(-1, n_conv), gconv_mine.reshape(-1, n_conv),
                       m_ffn_conv_w.reshape(-1, n_conv), v_ffn_conv_w.reshape(-1, n_conv), name="adamw_ffn_conv_w")
    grads["ffn_conv_w"] = gconv_mine
    deltas["ffn_conv_w"], new_m["ffn_conv_w"], new_v["ffn_conv_w"] = (t.reshape(cshape) for t in (d_, m_, v_))

    order = ["norm_mix", "norm_ffn", "norm_final", "attn_w_in", "attn_w_out", "attn_sinks", "hgrn_w_in",
             "hgrn_w_out", "hgrn_norm", "hgrn_lb_logits", "ffn_w_up", "ffn_conv_w", "ffn_conv_b", "ffn_w_down"]
    small_names = ["norm_mix", "norm_ffn", "norm_final", "attn_sinks", "hgrn_norm", "hgrn_lb_logits", "ffn_conv_b"]
    for i, name_ in enumerate(small_names):
        grads[name_], deltas[name_], new_m[name_], new_v[name_] = small_g[i], small_d[i], small_nm[i], small_nv[i]
    return (loss, grad_x, *[grads[n] for n in order], *[deltas[n] for n in order],
            *[new_m[n] for n in order], *[new_v[n] for n in order])
```

```python
import functools
import math

import numpy as np
import jax
import jax.numpy as jnp
from jax import lax
from jax.experimental import pallas as pl
from jax.experimental.pallas import tpu as pltpu

F32 = jnp.float32
BF16 = jnp.bfloat16
MESH = pl.DeviceIdType.MESH

HEAD_DIM = 64
Q_PER_KV = 4
WINDOW = 128
HG_EXPAND = 128
HG_CHUNK = 128
EPS = 1e-6
N_CHIPS = 4
N_DEV = 8

ADAM_LR = 0.001
ADAM_B1 = 0.9
ADAM_B2 = 0.999
ADAM_EPS = 1e-08
ADAM_WD = 0.01
ADAM_STEP = 10

VMEM_LIMIT_BYTES = 48 * 1024 * 1024
LANES = 128
SUBLANES = 8
HALO_ROWS = 16


def _params(sem=None):
    return pltpu.CompilerParams(dimension_semantics=sem, vmem_limit_bytes=VMEM_LIMIT_BYTES)


def _pick(n, cands):
    for c in cands:
        if n % c == 0:
            return c
    return n


def _dot(a, b, dims):
    return lax.dot_general(a.astype(BF16), b.astype(BF16), (dims, ((), ())),
                           preferred_element_type=F32)


NN = ((1,), (0,))
NT = ((1,), (1,))
TN = ((0,), (0,))


def mm_nn(a, w, l, *, name, res=None, out_dtype=F32, split_out=False, comm=None):
    M, K = a.shape
    N = w.shape[2]
    tm = _pick(M, (1024, 512, 256, 128))
    tn = _pick(N // 2 if split_out else N, (1408, 1024, 768, 512, 256, 128))
    tk = _pick(K, (1024, 1408, 512, 256, 128))
    nk = K // tk
    nh = (N // 2) // tn

    def body(*refs):
        if res is None:
            a_ref, w_ref, o_ref = refs[:3]
            r_ref = None
        else:
            a_ref, w_ref, r_ref, o_ref = refs[:4]
        part = _dot(a_ref[...], w_ref[...], NN)

        def finish(acc):
            if r_ref is not None:
                acc = acc + r_ref[...]
            o_ref[...] = acc.astype(out_dtype)

        if nk == 1:
            finish(part)
        else:
            acc_ref = refs[-1]
            k = pl.program_id(2)

            @pl.when(k == 0)
            def _():
                acc_ref[...] = part

            @pl.when(k > 0)
            def _():
                acc_ref[...] += part

            @pl.when(k == nk - 1)
            def _():
                finish(acc_ref[...])

    in_specs = [pl.BlockSpec((tm, tk), lambda n, m, k: (m, k)),
                pl.BlockSpec((None, tk, tn), lambda n, m, k: (l, k, n))]
    args = [a, w]
    if res is not None:
        in_specs.append(pl.BlockSpec((tm, tn), lambda n, m, k: (m, n)))
        args.append(res)
    if split_out:
        out_shape = jax.ShapeDtypeStruct((2, M, N // 2), out_dtype)
        out_spec = pl.BlockSpec((None, tm, tn), lambda n, m, k: (n // nh, m, n % nh))
    else:
        out_shape = jax.ShapeDtypeStruct((M, N), out_dtype)
        out_spec = pl.BlockSpec((tm, tn), lambda n, m, k: (m, n))
    outs, updated, _ = _carried_call(
        body, args, name=name, grid=(N // tn, M // tm, nk), in_specs=in_specs, out_specs=[out_spec],
        out_shape=[out_shape], scratch_shapes=[] if nk == 1 else [pltpu.VMEM((tm, tn), F32)], comm=comm)
    return outs[0] if comm is None else (outs[0], updated)


def mm_nt(a, w, l, *, name, split_in=False, out_dtype=F32, comm=None):
    if split_in:
        _, M, Nh = a.shape
        N = 2 * Nh
    else:
        M, N = a.shape
        Nh = N
    K = w.shape[1]
    tm = _pick(M, (1024, 512, 256, 128))
    tko = _pick(K, (1024, 1408, 512, 256, 128))
    tr = _pick(Nh, (1024, 1408, 768, 512, 256, 128))
    nr = N // tr
    nh = Nh // tr

    def body(a_ref, w_ref, o_ref, *scratch):
        part = _dot(a_ref[...], w_ref[...], NT)
        if nr == 1:
            o_ref[...] = part.astype(out_dtype)
        else:
            acc_ref = scratch[0]
            r = pl.program_id(2)

            @pl.when(r == 0)
            def _():
                acc_ref[...] = part

            @pl.when(r > 0)
            def _():
                acc_ref[...] += part

            @pl.when(r == nr - 1)
            def _():
                o_ref[...] = acc_ref[...].astype(out_dtype)

    if split_in:
        a_spec = pl.BlockSpec((None, tm, tr), lambda ko, m, r: (r // nh, m, r % nh))
    else:
        a_spec = pl.BlockSpec((tm, tr), lambda ko, m, r: (m, r))
    outs, _, created = _carried_call(
        body, [a, w], name=name, grid=(K // tko, M // tm, nr),
        in_specs=[a_spec, pl.BlockSpec((None, tko, tr), lambda ko, m, r: (l, ko, r))],
        out_specs=[pl.BlockSpec((tm, tko), lambda ko, m, r: (m, ko))],
        out_shape=[jax.ShapeDtypeStruct((M, K), out_dtype)],
        scratch_shapes=[] if nr == 1 else [pltpu.VMEM((tm, tko), F32)], comm=comm)
    return outs[0] if comm is None else (outs[0], created)


def mm_tn(a, b, *, name):
    M, K = a.shape
    N = b.shape[1]
    tm = _pick(M, (512, 256, 128))
    tko = _pick(K, (1024, 1408, 704, 512, 256, 128))
    tn = _pick(N, (512, 256, 128))
    nm = M // tm

    def body(a_ref, b_ref, o_ref):
        part = _dot(a_ref[...], b_ref[...], TN)
        r = pl.program_id(2)

        @pl.when(r == 0)
        def _():
            o_ref[...] = part

        @pl.when(r > 0)
        def _():
            o_ref[...] += part

    return pl.pallas_call(
        body, name=name, grid=(K // tko, N // tn, nm),
        in_specs=[pl.BlockSpec((tm, tko), lambda ko, n, r: (r, ko)), pl.BlockSpec((tm, tn), lambda ko, n, r: (r, n))],
        out_specs=pl.BlockSpec((None, tko, tn), lambda ko, n, r: (0, ko, n)),
        out_shape=jax.ShapeDtypeStruct((1, K, N), F32),
        compiler_params=_params(("parallel", "parallel", "arbitrary")),
    )(a, b)


def mm_dw(at, b, *, name, split_b=False):
    K, M = at.shape
    if split_b:
        Nh = b.shape[2]
        N = 2 * Nh
    else:
        N = b.shape[1]
        Nh = N
    tm = _pick(M, (1024, 512, 256, 128))
    tko = _pick(K, (1024, 1408, 704, 512, 256, 128))
    tn = _pick(Nh, (1408, 1024, 768, 512, 256, 128))
    nm = M // tm
    nh = Nh // tn

    def body(a_ref, b_ref, o_ref):
        part = _dot(a_ref[...], b_ref[...], NN)
        r = pl.program_id(2)

        @pl.when(r == 0)
        def _():
            o_ref[...] = part

        @pl.when(r > 0)
        def _():
            o_ref[...] += part

    if split_b:
        b_spec = pl.BlockSpec((None, tm, tn), lambda ko, n, r: (n // nh, r, n % nh))
    else:
        b_spec = pl.BlockSpec((tm, tn), lambda ko, n, r: (r, n))
    return pl.pallas_call(
        body, name=name, grid=(K // tko, N // tn, nm),
        in_specs=[pl.BlockSpec((tko, tm), lambda ko, n, r: (ko, r)), b_spec],
        out_specs=pl.BlockSpec((None, tko, tn), lambda ko, n, r: (0, ko, n)),
        out_shape=jax.ShapeDtypeStruct((1, K, N), F32),
        compiler_params=_params(("parallel", "parallel", "arbitrary")),
    )(at, b)


def _rms(x, g):
    return x * lax.rsqrt(jnp.mean(x * x, axis=-1, keepdims=True) + EPS) * g


def rmsnorm_fwd(h, g, l, *, name):
    S, D = h.shape
    tm = _pick(S, (512, 256, 128))

    def body(h_ref, g_ref, o_ref, ot_ref):
        y = _rms(h_ref[...], g_ref[...])
        o_ref[...] = y.astype(BF16)
        ot_ref[...] = y.T.astype(BF16)

    return pl.pallas_call(
        body, name=name, grid=(S // tm,),
        in_specs=[pl.BlockSpec((tm, D), lambda m: (m, 0)),
                  pl.BlockSpec((None, 1, D), lambda m: (l, 0, 0))],
        out_specs=[pl.BlockSpec((tm, D), lambda m: (m, 0)), pl.BlockSpec((D, tm), lambda m: (0, m))],
        out_shape=[jax.ShapeDtypeStruct((S, D), BF16), jax.ShapeDtypeStruct((D, S), BF16)],
        compiler_params=_params(("parallel",)),
    )(h, g)


def rmsnorm_bwd(h, g, l, dhn, dres, *, name):
    S, D = h.shape
    tm = _pick(S, (512, 256, 128))

    def body(h_ref, g_ref, dhn_ref, dres_ref, dh_ref, dg_ref):
        _, vjp = jax.vjp(_rms, h_ref[...], g_ref[...])
        dx, dg = vjp(dhn_ref[...])
        dh_ref[...] = dres_ref[...] + dx

        @pl.when(pl.program_id(0) == 0)
        def _():
            dg_ref[...] = dg

        @pl.when(pl.program_id(0) > 0)
        def _():
            dg_ref[...] += dg

    row = pl.BlockSpec((tm, D), lambda m: (m, 0))
    return pl.pallas_call(
        body, name=name, grid=(S // tm,),
        in_specs=[row, pl.BlockSpec((None, 1, D), lambda m: (l, 0, 0)), row, row],
        out_specs=[row, pl.BlockSpec((1, D), lambda m: (0, 0))],
        out_shape=[jax.ShapeDtypeStruct((S, D), F32), jax.ShapeDtypeStruct((1, D), F32)],
        compiler_params=_params(("arbitrary",)),
    )(h, g, dhn, dres)


def loss_head(h, g, target, *, name):
    S, D = h.shape
    tm = _pick(S, (512, 256, 128))

    def body(h_ref, g_ref, t_ref, loss_ref, dh_ref, dg_ref):
        y, vjp = jax.vjp(_rms, h_ref[...], g_ref[...])
        err = y - t_ref[...]
        part = 0.5 * jnp.sum(jnp.sum(err * err, axis=-1, keepdims=True) / D, axis=0, keepdims=True)
        dx, dg = vjp(err / D)
        dh_ref[...] = dx

        @pl.when(pl.program_id(0) == 0)
        def _():
            dg_ref[...] = dg
            loss_ref[...] = part

        @pl.when(pl.program_id(0) > 0)
        def _():
            dg_ref[...] += dg
            loss_ref[...] += part

    row = pl.BlockSpec((tm, D), lambda m: (m, 0))
    return pl.pallas_call(
        body, name=name, grid=(S // tm,),
        in_specs=[row, pl.BlockSpec((1, D), lambda m: (0, 0)), row],
        out_specs=[pl.BlockSpec((1, 1), lambda m: (0, 0)), row, pl.BlockSpec((1, D), lambda m: (0, 0))],
        out_shape=[jax.ShapeDtypeStruct((1, 1), F32), jax.ShapeDtypeStruct((S, D), F32),
                   jax.ShapeDtypeStruct((1, D), F32)],
        compiler_params=_params(("arbitrary",)),
    )(h, g, target)


def _make_mm(dims_fwd, dims_da, dims_db, swap_da=False, swap_db=False):
    @jax.custom_vjp
    def mm(a, b):
        return _dot(a, b, dims_fwd)

    def fwd(a, b):
        return mm(a, b), (a, b)

    def bwd(resid, g):
        a, b = resid
        da = _dot(b, g, dims_da) if swap_da else _dot(g, b, dims_da)
        db = _dot(g, a, dims_db) if swap_db else _dot(a, g, dims_db)
        return da, db

    mm.defvjp(fwd, bwd)
    return mm


_mm = _make_mm(NN, NT, TN)
_mm_nt = _make_mm(NT, NN, TN, swap_db=True)
_mm_tn = _make_mm(TN, NT, NN, swap_da=True)


def _dot_hi(a, b):
    return jnp.dot(a, b, precision=lax.Precision.HIGHEST, preferred_element_type=F32)


def _sigmoid(x):
    return 0.5 * jnp.tanh(0.5 * x) + 0.5


def _silu(x):
    return x * _sigmoid(x)


def _row_slices(x, n):
    rows = x.shape[0] // n
    return tuple(x[i * rows:(i + 1) * rows] for i in range(n))


@functools.partial(jax.custom_vjp, nondiff_argnums=(1,))
def _split_rows(x, n):
    return _row_slices(x, n)


_split_rows.defvjp(lambda x, n: (_row_slices(x, n), None), lambda n, _, cts: (jnp.concatenate(cts, axis=0),))


def _attn_block(qs, kws, vws, sink, slope, first):
    s = jnp.concatenate([_mm_nt(q, kw) for q, kw in zip(qs, kws)], axis=0) * (HEAD_DIM ** -0.5)
    qi = lax.broadcasted_iota(jnp.int32, s.shape, 0) % WINDOW
    ki = lax.broadcasted_iota(jnp.int32, s.shape, 1)
    dist = qi + WINDOW - ki
    valid = (dist >= 0) & (dist < WINDOW) & (ki >= WINDOW * first)
    s = jnp.where(valid, s - slope * dist.astype(F32), -jnp.inf)
    m = lax.stop_gradient(jnp.maximum(jnp.max(s, axis=-1, keepdims=True), sink))
    e = jnp.exp(s - m)
    denom = jnp.sum(e, axis=-1, keepdims=True) + jnp.exp(sink - m)
    ps = _split_rows(e * (1.0 / denom), len(qs))
    return tuple(_mm(p, vw) for p, vw in zip(ps, vws))


def _attn_specs(Hkv, order):
    G = Q_PER_KV
    qspec = pl.BlockSpec((Hkv, G, WINDOW, HEAD_DIM), lambda n: (0, 0, order(n), 0))
    prev = pl.BlockSpec((Hkv, WINDOW, HEAD_DIM), lambda n: (0, jnp.maximum(order(n) - 1, 0), 0))
    cur = pl.BlockSpec((Hkv, WINDOW, HEAD_DIM), lambda n: (0, order(n), 0))
    col = pl.BlockSpec((Hkv, G * WINDOW, 1), lambda n: (0, 0, 0))
    return qspec, prev, cur, col


def attn_fwd(q, k, v, sink_col, slope_col, *, name, comm=None):
    Hkv, G, S, _ = q.shape
    qspec, prev, cur, col = _attn_specs(Hkv, lambda n: n)

    def body(q_ref, kp_ref, kc_ref, vp_ref, vc_ref, sink_ref, slope_ref, o_ref):
        first = (pl.program_id(0) == 0).astype(jnp.int32)
        qs = tuple(q_ref[h].reshape(G * WINDOW, HEAD_DIM) for h in range(Hkv))
        kws = tuple(jnp.concatenate([kp_ref[h], kc_ref[h]], axis=0) for h in range(Hkv))
        vws = tuple(jnp.concatenate([vp_ref[h], vc_ref[h]], axis=0) for h in range(Hkv))
        outs = _attn_block(qs, kws, vws, sink_ref[...].reshape(Hkv * G * WINDOW, 1),
                           slope_ref[...].reshape(Hkv * G * WINDOW, 1), first)
        o_ref[...] = jnp.stack([o.reshape(G, WINDOW, HEAD_DIM).astype(BF16) for o in outs])

    outs, updated, created = _carried_call(
        body, [q, k, k, v, v, sink_col, slope_col], name=name, grid=(S // WINDOW,),
        in_specs=[qspec, prev, cur, prev, cur, col, col], out_specs=[qspec],
        out_shape=[jax.ShapeDtypeStruct(q.shape, BF16)], comm=comm)
    return outs[0] if comm is None else (outs[0], updated, created)


def attn_bwd(q, k, v, sink_col, slope_col, do, *, name, comm=None):
    Hkv, G, S, _ = q.shape
    nb = S // WINDOW
    qspec, prev, cur, col = _attn_specs(Hkv, lambda n: nb - 1 - n)

    def body(q_ref, kp_ref, kc_ref, vp_ref, vc_ref, sink_ref, slope_ref, do_ref,
             dq_ref, dk_ref, dv_ref, dsink_ref, dk_carry, dv_carry):
        @pl.when(pl.program_id(0) == 0)
        def _():
            dk_carry[...] = jnp.zeros_like(dk_carry)
            dv_carry[...] = jnp.zeros_like(dv_carry)
            dsink_ref[...] = jnp.zeros_like(dsink_ref)

        first = (pl.program_id(0) == nb - 1).astype(jnp.int32)
        qs = tuple(q_ref[h].reshape(G * WINDOW, HEAD_DIM).astype(F32) for h in range(Hkv))
        kws = tuple(jnp.concatenate([kp_ref[h], kc_ref[h]], axis=0).astype(F32) for h in range(Hkv))
        vws = tuple(jnp.concatenate([vp_ref[h], vc_ref[h]], axis=0).astype(F32) for h in range(Hkv))
        fn = functools.partial(_attn_block, slope=slope_ref[...].reshape(Hkv * G * WINDOW, 1), first=first)
        _, vjp = jax.vjp(fn, qs, kws, vws, sink_ref[...].reshape(Hkv * G * WINDOW, 1))
        dqs, dkws, dvws, dsink = vjp(tuple(do_ref[h].reshape(G * WINDOW, HEAD_DIM).astype(F32) for h in range(Hkv)))
        dkw, dvw = jnp.stack(dkws), jnp.stack(dvws)
        dq_ref[...] = jnp.stack([dq.reshape(G, WINDOW, HEAD_DIM) for dq in dqs]).astype(BF16)
        dk_ref[...] = (dkw[:, WINDOW:] + dk_carry[...]).astype(BF16)
        dv_ref[...] = (dvw[:, WINDOW:] + dv_carry[...]).astype(BF16)
        dk_carry[...] = dkw[:, :WINDOW]
        dv_carry[...] = dvw[:, :WINDOW]
        dsink_ref[...] += dsink.reshape(Hkv, G * WINDOW, 1)

    kv_shape = jax.ShapeDtypeStruct(k.shape, BF16)
    outs, _, created = _carried_call(
        body, [q, k, k, v, v, sink_col, slope_col, do], name=name, grid=(nb,),
        in_specs=[qspec, prev, cur, prev, cur, col, col, qspec],
        out_specs=[qspec, cur, cur, col],
        out_shape=[jax.ShapeDtypeStruct(q.shape, BF16), kv_shape, kv_shape,
                   jax.ShapeDtypeStruct(sink_col.shape, F32)],
        scratch_shapes=[pltpu.VMEM((Hkv, WINDOW, HEAD_DIM), F32), pltpu.VMEM((Hkv, WINDOW, HEAD_DIM), F32)],
        comm=comm)
    return outs if comm is None else (*outs, created)


def _hgrn_consts():
    C = HG_CHUNK
    t = np.arange(C)[:, None]
    j = np.arange(C)[None, :]
    cq, ck, mk = [], [], []
    H = C // 2
    while H >= 1:
        start = (t // H) * H
        cq.append((j > start) & (j <= t))
        ck.append((j > t) & (j <= start + H))
        mk.append(((t // H) % 2 == 1) & ((j // H) == (t // H) - 1))
        H //= 2
    ltri = (j <= t)
    ops = np.concatenate([ltri] + cq + ck, axis=0).astype(np.float32)
    ops3 = np.concatenate([ops, ops, ops], axis=1)
    opst3 = np.concatenate([ops.T, ops.T, ops.T], axis=1)
    return (jnp.asarray(ops3, BF16), jnp.asarray(opst3, BF16), jnp.asarray(np.stack(mk).astype(np.float32)))


def _split3(x):
    hi = x.astype(BF16)
    r = x - hi.astype(F32)
    mid = r.astype(BF16)
    lo = (r - mid.astype(F32)).astype(BF16)
    return jnp.concatenate([hi, mid, lo], axis=0)


@jax.custom_vjp
def _seg_sums(g, ops3, opst3):
    C = g.shape[0]
    out = jnp.dot(ops3, _split3(g), preferred_element_type=F32)
    return tuple(out[i * C:(i + 1) * C] for i in range(ops3.shape[0] // C))


def _seg_sums_fwd(g, ops3, opst3):
    return _seg_sums(g, ops3, opst3), opst3


def _seg_sums_bwd(opst3, cts):
    ct = jnp.concatenate(cts, axis=0)
    return jnp.dot(opst3, _split3(ct), preferred_element_type=F32), None, None


_seg_sums.defvjp(_seg_sums_fwd, _seg_sums_bwd)


def _head_slices(x):
    return tuple(x[:, h * HG_EXPAND:(h + 1) * HG_EXPAND] for h in range(x.shape[1] // HG_EXPAND))


@jax.custom_vjp
def _split_heads(x):
    return _head_slices(x)


_split_heads.defvjp(lambda x: (_head_slices(x), None), lambda _, cts: (jnp.concatenate(cts, axis=1),))


@jax.custom_vjp
def _join_heads(xs):
    return jnp.concatenate(xs, axis=1)


_join_heads.defvjp(lambda xs: (jnp.concatenate(xs, axis=1), None), lambda _, ct: (_head_slices(ct),))


def _hgrn_chunk(qr, fr, iv, gr, sts, logits, ng, ops3, opst3, mk, layer):
    n_levels = mk.shape[0]
    depth, D = logits.shape
    e = jnp.exp(logits - lax.stop_gradient(jnp.max(logits, axis=0, keepdims=True)))
    sm = e / jnp.sum(e, axis=0, keepdims=True)
    row = lax.broadcasted_iota(jnp.int32, (depth, D), 0)
    lb = jnp.sum(jnp.where((row >= 1) & (row <= layer), sm, 0.0), axis=0, keepdims=True)

    q = _silu(qr)
    a = jnp.log(lb)
    c = jnp.log(1.0 - lb) - (jnp.maximum(-fr, 0.0) + jnp.log(1.0 + jnp.exp(-jnp.abs(fr))))
    mx = lax.stop_gradient(jnp.maximum(a, c))
    g = mx + jnp.log(jnp.exp(a - mx) + jnp.exp(c - mx))
    k = (1.0 - lb) * _sigmoid(-fr)

    sums = _seg_sums(g, ops3, opst3)
    b = sums[0]
    b_last = jnp.sum(g, axis=0, keepdims=True)

    ql = [_split_heads(q * jnp.exp(sums[1 + lv])) for lv in range(n_levels)]
    kl = [_split_heads(k * jnp.exp(sums[1 + n_levels + lv])) for lv in range(n_levels)]
    qb, kb = _split_heads(q * jnp.exp(b)), _split_heads(k * jnp.exp(b_last - b))
    qk, ivh, gate, decay = _split_heads(q * k), _split_heads(iv), _split_heads(_silu(gr)), _split_heads(jnp.exp(b_last))
    outs, new_sts = [], []
    for h, st in enumerate(sts):
        amat = None
        for lv in range(n_levels):
            part = _mm_nt(ql[lv][h], kl[lv][h]) * mk[lv]
            amat = part if amat is None else amat + part
        o = jnp.sum(qk[h], axis=-1, keepdims=True) * ivh[h] + _mm(amat, ivh[h]) + _mm_nt(qb[h], st)
        new_sts.append(st * decay[h] + _mm_tn(ivh[h], kb[h]))
        on = o * lax.rsqrt(jnp.mean(o * o, axis=-1, keepdims=True) + EPS) * ng
        outs.append(on * gate[h])
    return _join_heads(tuple(outs)), tuple(new_sts)


def _hgrn_in_specs(D, depth, order):
    C = HG_CHUNK
    nl = int(math.log2(C))
    n_ops = 1 + 2 * nl
    return [pl.BlockSpec((C, 4 * D), lambda c: (order(c), 0)),
            pl.BlockSpec((depth, D), lambda c: (0, 0)),
            pl.BlockSpec((1, HG_EXPAND), lambda c: (0, 0)),
            pl.BlockSpec((n_ops * C, 3 * C), lambda c: (0, 0)),
            pl.BlockSpec((C, 3 * n_ops * C), lambda c: (0, 0)),
            pl.BlockSpec((nl, C, C), lambda c: (0, 0, 0))]


def hgrn_fwd(proj, logits, ng, layer, *, name, comm=None):
    S = proj.shape[0]
    D = proj.shape[1] // 4
    H = D // HG_EXPAND
    C = HG_CHUNK
    nc = S // C
    consts = _hgrn_consts()

    def body(p_ref, lg_ref, ng_ref, ops3_ref, opst3_ref, mk_ref, o_ref, st_out_ref, st_ref):
        @pl.when(pl.program_id(0) == 0)
        def _():
            st_ref[...] = jnp.zeros_like(st_ref)

        st_out_ref[...] = st_ref[...]
        out, new_sts = _hgrn_chunk(*(p_ref[:, j * D:(j + 1) * D] for j in range(4)), tuple(st_ref[h] for h in range(H)),
                                   lg_ref[...], ng_ref[...], ops3_ref[...], opst3_ref[...], mk_ref[...], layer)
        o_ref[...] = out.astype(BF16)
        for h in range(H):
            st_ref[h] = new_sts[h]

    outs, updated, created = _carried_call(
        body, [proj, logits, ng, *consts], name=name, grid=(nc,),
        in_specs=_hgrn_in_specs(D, logits.shape[0], lambda c: c),
        out_specs=[pl.BlockSpec((C, D), lambda c: (c, 0)),
                   pl.BlockSpec((None, H, HG_EXPAND, HG_EXPAND), lambda c: (c, 0, 0, 0))],
        out_shape=[jax.ShapeDtypeStruct((S, D), BF16),
                   jax.ShapeDtypeStruct((nc, H, HG_EXPAND, HG_EXPAND), F32)],
        scratch_shapes=[pltpu.VMEM((H, HG_EXPAND, HG_EXPAND), F32)], comm=comm)
    return outs if comm is None else (*outs, updated, created)


def hgrn_bwd(proj, logits, ng, states, dout, layer, *, name, comm=None):
    S = proj.shape[0]
    D = proj.shape[1] // 4
    H = D // HG_EXPAND
    C = HG_CHUNK
    nc = S // C
    depth = logits.shape[0]
    consts = _hgrn_consts()
    rev = lambda c: nc - 1 - c

    def body(p_ref, lg_ref, ng_ref, ops3_ref, opst3_ref, mk_ref, st_in_ref, do_ref,
             dp_ref, dlg_ref, dng_ref, dst_ref):
        @pl.when(pl.program_id(0) == 0)
        def _():
            dst_ref[...] = jnp.zeros_like(dst_ref)
            dlg_ref[...] = jnp.zeros_like(dlg_ref)
            dng_ref[...] = jnp.zeros_like(dng_ref)

        fn = functools.partial(_hgrn_chunk, ops3=ops3_ref[...], opst3=opst3_ref[...], mk=mk_ref[...], layer=layer)
        _, vjp = jax.vjp(fn, *(p_ref[:, j * D:(j + 1) * D] for j in range(4)), tuple(st_in_ref[h] for h in range(H)),
                         lg_ref[...], ng_ref[...])
        *dproj, dsts, dlg, dng = vjp((do_ref[...], tuple(dst_ref[h] for h in range(H))))
        for j in range(4):
            dp_ref[:, j * D:(j + 1) * D] = dproj[j].astype(BF16)
        for h in range(H):
            dst_ref[h] = dsts[h]
        dlg_ref[...] += dlg
        dng_ref[...] += dng

    in_specs = _hgrn_in_specs(D, depth, rev)
    in_specs += [pl.BlockSpec((None, H, HG_EXPAND, HG_EXPAND), lambda c: (rev(c), 0, 0, 0)),
                 pl.BlockSpec((C, D), lambda c: (rev(c), 0))]
    outs, _, created = _carried_call(
        body, [proj, logits, ng, *consts, states, dout], name=name, grid=(nc,), in_specs=in_specs,
        out_specs=[pl.BlockSpec((C, 4 * D), lambda c: (rev(c), 0)),
                   pl.BlockSpec((depth, D), lambda c: (0, 0)), pl.BlockSpec((1, HG_EXPAND), lambda c: (0, 0))],
        out_shape=[jax.ShapeDtypeStruct((S, 4 * D), BF16), jax.ShapeDtypeStruct((depth, D), F32),
                   jax.ShapeDtypeStruct((1, HG_EXPAND), F32)],
        scratch_shapes=[pltpu.VMEM((H, HG_EXPAND, HG_EXPAND), F32)], comm=comm)
    return outs if comm is None else (*outs, created)


def _shift_down(u, prev, shift):
    rolled = pltpu.roll(u, shift, axis=0)
    rows = lax.broadcasted_iota(jnp.int32, prev.shape, 0)
    top = jnp.where(rows < shift, pltpu.roll(prev, shift, axis=0), rolled[:SUBLANES])
    return jnp.concatenate([top, rolled[SUBLANES:]], axis=0)


def _shift_up(x, nxt, shift):
    tm = x.shape[0]
    rolled = pltpu.roll(x, tm - shift, axis=0)
    rows = lax.broadcasted_iota(jnp.int32, nxt.shape, 0)
    bottom = jnp.where(rows >= SUBLANES - shift, pltpu.roll(nxt, SUBLANES - shift, axis=0), rolled[tm - SUBLANES:])
    return jnp.concatenate([rolled[:tm - SUBLANES], bottom], axis=0)


def _conv(u, prev, w_ref, b_ref, half):
    u1 = _shift_down(u, prev, 1)
    u2 = _shift_down(u, prev, 2)
    return b_ref[half] + w_ref[half, 0:1, :] * u2 + w_ref[half, 1:2, :] * u1 + w_ref[half, 2:3, :] * u


def _halo_rows(ref, keep, last):
    full = ref.astype(F32)
    return (full[HALO_ROWS - SUBLANES:] if last else full[:SUBLANES]) * keep


def _conv_specs(tm, tc, l):
    nprev = lambda m: jnp.maximum(m * (tm // HALO_ROWS) - 1, 0)
    tile = pl.BlockSpec((2, tm, tc), lambda j, m: (0, m, j))
    halo = pl.BlockSpec((2, HALO_ROWS, tc), lambda j, m: (0, nprev(m), j))
    wspec = pl.BlockSpec((None, 2, 3, tc), lambda j, m: (l, 0, 0, j))
    bspec = pl.BlockSpec((None, 2, 1, tc), lambda j, m: (l, 0, 0, j))
    return tile, halo, wspec, bspec


def convact_fwd(u, cw, cb, l, *, name):
    _, S, Fh = u.shape
    tm = _pick(S, (256, 128))
    tc = _pick(Fh, (1408, 512, 256, 128))
    tile, halo, wspec, bspec = _conv_specs(tm, tc, l)

    def body(u_ref, p_ref, w_ref, b_ref, o_ref, ot_ref, c_ref):
        keep = (pl.program_id(1) > 0).astype(F32)
        cg = _conv(u_ref[0].astype(F32), _halo_rows(p_ref[0], keep, True), w_ref, b_ref, 0)
        cv = _conv(u_ref[1].astype(F32), _halo_rows(p_ref[1], keep, True), w_ref, b_ref, 1)
        act = _silu(cg) * cv
        o_ref[...] = act.astype(BF16)
        ot_ref[...] = act.T.astype(BF16)
        c_ref[0] = cg.astype(BF16)
        c_ref[1] = cv.astype(BF16)

    return pl.pallas_call(
        body, name=name, grid=(Fh // tc, S // tm), in_specs=[tile, halo, wspec, bspec],
        out_specs=[pl.BlockSpec((tm, tc), lambda j, m: (m, j)), pl.BlockSpec((tc, tm), lambda j, m: (j, m)), tile],
        out_shape=[jax.ShapeDtypeStruct((S, Fh), BF16), jax.ShapeDtypeStruct((Fh, S), BF16),
                   jax.ShapeDtypeStruct(u.shape, BF16)],
        compiler_params=_params(("parallel", "parallel")),
    )(u, u, cw, cb)


def convact_bwd(c, u, cw, l, da, *, name):
    _, S, Fh = u.shape
    tm = _pick(S, (256, 128))
    tc = _pick(Fh, (1408, 512, 256, 128))
    nm = S // tm

    def body(c_ref, u_ref, w_ref, da_ref, du_ref, dw_ref, db_ref, carry):
        @pl.when(pl.program_id(1) == 0)
        def _():
            carry[...] = jnp.zeros_like(carry)
            dw_ref[...] = jnp.zeros_like(dw_ref)
            db_ref[...] = jnp.zeros_like(db_ref)

        cg, cv = c_ref[0].astype(F32), c_ref[1].astype(F32)
        da_t = da_ref[...]
        sg = _sigmoid(cg)
        dcv = da_t * (cg * sg)
        dcg = da_t * cv * (sg * (1.0 + cg * (1.0 - sg)))
        rs = lambda t: jnp.sum(t, axis=0, keepdims=True)
        for half, dc in ((0, dcg), (1, dcv)):
            nxt = carry[half]
            d1, d2 = _shift_up(dc, nxt, 1), _shift_up(dc, nxt, 2)
            du = w_ref[half, 2:3, :] * dc + w_ref[half, 1:2, :] * d1 + w_ref[half, 0:1, :] * d2
            du_ref[half] = du.astype(BF16)
            uh = u_ref[half].astype(F32)
            for j, d in enumerate((d2, d1, dc)):
                dw_ref[half, j:j + 1, :] += rs(d * uh)
            db_ref[half] += rs(dc)
            carry[half] = dc[:SUBLANES]

    tile = pl.BlockSpec((2, tm, tc), lambda j, m: (0, nm - 1 - m, j))
    return pl.pallas_call(
        body, name=name, grid=(Fh // tc, nm),
        in_specs=[tile, tile, pl.BlockSpec((None, 2, 3, tc), lambda j, m: (l, 0, 0, j)),
                  pl.BlockSpec((tm, tc), lambda j, m: (nm - 1 - m, j))],
        out_specs=[tile, pl.BlockSpec((2, 3, tc), lambda j, m: (0, 0, j)),
                   pl.BlockSpec((2, 1, tc), lambda j, m: (0, 0, j))],
        out_shape=[jax.ShapeDtypeStruct(u.shape, BF16), jax.ShapeDtypeStruct((2, 3, Fh), F32),
                   jax.ShapeDtypeStruct((2, 1, Fh), F32)],
        scratch_shapes=[pltpu.VMEM((2, SUBLANES, tc), F32)],
        compiler_params=_params(("parallel", "arbitrary")),
    )(c, u, cw, da)


def _adamw_math(w, g, m, v):
    m = ADAM_B1 * m + (1.0 - ADAM_B1) * g
    v = ADAM_B2 * v + (1.0 - ADAM_B2) * (g * g)
    m_hat = m / (1.0 - ADAM_B1 ** ADAM_STEP)
    v_hat = v / (1.0 - ADAM_B2 ** ADAM_STEP)
    delta = -ADAM_LR * (m_hat / (jnp.sqrt(v_hat) + ADAM_EPS) + ADAM_WD * w)
    return delta, m, v


def _row_tile(R, Ccols, n_streams):
    for tr in (512, 256, 128, 64, 32, 16, 8):
        if R % tr == 0 and tr * Ccols * 4 * n_streams * 2 <= VMEM_LIMIT_BYTES // 2:
            return tr
    return R


def adamw(w, g, m, v, *, name):
    R, Ccols = w.shape
    tr = _row_tile(R, Ccols, 7)

    def body(w_ref, g_ref, m_ref, v_ref, d_ref, mo_ref, vo_ref):
        d, mn, vn = _adamw_math(w_ref[...], g_ref[...], m_ref[...], v_ref[...])
        d_ref[...] = d
        mo_ref[...] = mn
        vo_ref[...] = vn

    spec = pl.BlockSpec((tr, Ccols), lambda i: (i, 0))
    shp = jax.ShapeDtypeStruct((R, Ccols), F32)
    return pl.pallas_call(
        body, name=name, grid=(R // tr,), in_specs=[spec] * 4, out_specs=[spec] * 3,
        out_shape=[shp, shp, shp], compiler_params=_params(("parallel",)),
    )(w, g, m, v)


def _prefetch_call(body, pos, args, *, name, grid, in_specs, out_specs, out_shape, sem, aliases=None):
    return pl.pallas_call(
        lambda pos_ref, *refs: body(*refs), name=name,
        grid_spec=pltpu.PrefetchScalarGridSpec(num_scalar_prefetch=1, grid=grid, in_specs=in_specs,
                                               out_specs=out_specs),
        out_shape=out_shape, input_output_aliases={1 + i: o for i, o in (aliases or {}).items()},
        compiler_params=_params(sem),
    )(pos, *args)


def cast_place(w, pos, *, row_sharded, name):
    L, R, Ccols = w.shape

    def body(w_ref, o_ref):
        o_ref[...] = w_ref[...].astype(BF16)

    if row_sharded:
        grid = (L,)
        in_spec = pl.BlockSpec((None, R, Ccols), lambda l, pos: (l, 0, 0))
        out_spec = pl.BlockSpec((None, R, Ccols), lambda l, pos: (l, pos[0], 0))
        out_shape = jax.ShapeDtypeStruct((L, N_CHIPS * R, Ccols), BF16)
        sem = ("parallel",)
    else:
        tr = _pick(R, (512, 256, 128))
        grid = (L, R // tr)
        in_spec = pl.BlockSpec((None, tr, Ccols), lambda l, i, pos: (l, i, 0))
        out_spec = pl.BlockSpec((None, tr, Ccols), lambda l, i, pos: (l, i, pos[0]))
        out_shape = jax.ShapeDtypeStruct((L, R, N_CHIPS * Ccols), BF16)
        sem = ("parallel", "parallel")
    return _prefetch_call(body, pos, [w], name=name, grid=grid, in_specs=[in_spec], out_specs=out_spec,
                          out_shape=out_shape, sem=sem)


def pair_add(g, theirs, pos, *, row_sharded, name):
    def body(g_ref, t_ref, o_ref, ob_ref):
        s = g_ref[...] + t_ref[...]
        o_ref[...] = s
        ob_ref[...] = s.astype(BF16)

    if row_sharded:
        L, _, _, r2, D = g.shape
        grid = (L, N_CHIPS)
        g_spec = pl.BlockSpec((None, None, None, r2, D), lambda l, q, pos: (l, q, pos[1], 0, 0))
        t_spec = pl.BlockSpec((None, None, r2, D), lambda l, q, pos: (l, q, 0, 0))
    else:
        L, K, N = g.shape
        n = N // N_CHIPS
        tr = _pick(K // 2, (256, 128))
        nb = (K // 2) // tr
        grid = (L, nb, N_CHIPS)
        g_spec = pl.BlockSpec((None, tr, n), lambda l, i, j, pos: (l, pos[1] * nb + i, j))
        t_spec = pl.BlockSpec((None, tr, n), lambda l, i, j, pos: (l, i, j))
    return _prefetch_call(body, pos, [g, theirs], name=name, grid=grid, in_specs=[g_spec, t_spec],
                          out_specs=[t_spec, t_spec],
                          out_shape=[jax.ShapeDtypeStruct(theirs.shape, F32), jax.ShapeDtypeStruct(theirs.shape, BF16)],
                          sem=("parallel",) * len(grid))


def chip_add(pair, landed, pos, buf, li, n_layers, *, row_sharded, name):
    def body(p_ref, a_ref, b_ref, c_ref, *rest):
        rest[-1][...] = ((p_ref[...] + a_ref[...].astype(F32)) + b_ref[...].astype(F32)) + c_ref[...].astype(F32)

    if row_sharded:
        _, _, r2, D = pair.shape
        grid = (1,)
        p_spec = pl.BlockSpec((None, None, r2, D), lambda i, pos: (0, pos[0], 0, 0))
        l_specs = [pl.BlockSpec((None, r2, D), functools.partial(lambda i, pos, k: (k, 0, 0), k=k)) for k in range(3)]
        out_spec = pl.BlockSpec((None, None, r2, D), lambda i, pos: (li, pos[1], 0, 0))
        out_shape = jax.ShapeDtypeStruct((n_layers, 2, r2, D), F32)
    else:
        _, K2, N = pair.shape
        n = N // N_CHIPS
        tr = _pick(K2, (256, 128))
        nb = K2 // tr
        grid = (nb,)
        p_spec = pl.BlockSpec((None, tr, n), lambda i, pos: (0, i, pos[0]))
        l_specs = [pl.BlockSpec((None, tr, n), functools.partial(lambda i, pos, k: (k, i, 0), k=k)) for k in range(3)]
        out_spec = pl.BlockSpec((None, tr, n), lambda i, pos: (li, pos[1] * nb + i, 0))
        out_shape = jax.ShapeDtypeStruct((n_layers, 2 * K2, n), F32)
    args, in_specs, aliases = [pair, landed, landed, landed], [p_spec] + l_specs, None
    if buf is not None:
        args.append(buf)
        in_specs.append(pl.BlockSpec(memory_space=pl.ANY))
        aliases = {4: 0}
    return _prefetch_call(body, pos, args, name=name, grid=grid, in_specs=in_specs, out_specs=out_spec,
                          out_shape=out_shape, sem=("parallel",), aliases=aliases)


def _position():
    return lax.axis_index("x"), lax.axis_index("y"), lax.axis_index("c")


def _other_chips(x, y):
    return [(1 - x, y), (x, 1 - y), (1 - x, 1 - y)]


HBM_ANY = pl.BlockSpec(memory_space=pl.ANY)
COMM_PARAMS = pltpu.CompilerParams(has_side_effects=True)


class Comm:
    def __init__(self, ro=(), inout=(), new=(), sems=(), phases=()):
        self.ro, self.inout, self.new, self.sems, self.phases = list(ro), list(inout), list(new), list(sems), list(phases)


def _carried_call(body, args, *, name, grid, in_specs, out_specs, out_shape, scratch_shapes=(), comm=None):
    comm = comm or Comm()
    n_in, n_out, n_sc = len(in_specs), len(out_shape), len(scratch_shapes)
    n_ro, n_io, n_new, n_sem = len(comm.ro), len(comm.inout), len(comm.new), len(comm.sems)
    last = int(np.prod(grid)) - 1

    def wrapped(*refs):
        bounds = np.cumsum([0, n_in, n_ro, n_io, n_out, n_io, n_new, n_sc, n_sem])
        core_in, ro, _, core_out, io, new, core_sc, sems = (refs[a:b] for a, b in zip(bounds[:-1], bounds[1:]))
        step = pl.program_id(0)
        for d in range(1, len(grid)):
            step = step * grid[d] + pl.program_id(d)
        for frac, fn in comm.phases[:-1]:
            pl.when(step == int(frac * last))(functools.partial(fn, ro, io, new, sems))
        body(*core_in, *core_out, *core_sc)
        if comm.phases:
            pl.when(step == last)(functools.partial(comm.phases[-1][1], ro, io, new, sems))

    outs = pl.pallas_call(
        wrapped, name=name, grid=grid,
        in_specs=list(in_specs) + [HBM_ANY] * (n_ro + n_io),
        out_specs=list(out_specs) + [HBM_ANY] * (n_io + n_new),
        out_shape=list(out_shape) + [jax.ShapeDtypeStruct(a.shape, a.dtype) for a in comm.inout] + comm.new,
        input_output_aliases={n_in + n_ro + j: n_out + j for j in range(n_io)},
        scratch_shapes=list(scratch_shapes) + comm.sems,
        compiler_params=pltpu.CompilerParams(dimension_semantics=("arbitrary",) * len(grid),
                                             vmem_limit_bytes=VMEM_LIMIT_BYTES, has_side_effects=bool(comm.phases)),
    )(*args, *comm.ro, *comm.inout)
    return outs[:n_out], outs[n_out:n_out + n_io], outs[n_out + n_io:]


def run_comm(comm, *, name):
    n_ro, n_io, n_new = len(comm.ro), len(comm.inout), len(comm.new)

    def body(*refs):
        bounds = np.cumsum([0, n_ro, n_io, n_io, n_new, len(comm.sems)])
        ro, _, io, new, sems = (refs[a:b] for a, b in zip(bounds[:-1], bounds[1:]))
        for _, fn in comm.phases:
            fn(ro, io, new, sems)

    outs = pl.pallas_call(
        body, name=name, in_specs=[HBM_ANY] * (n_ro + n_io), out_specs=[HBM_ANY] * (n_io + n_new),
        out_shape=[jax.ShapeDtypeStruct(a.shape, a.dtype) for a in comm.inout] + comm.new,
        input_output_aliases={n_ro + j: j for j in range(n_io)},
        scratch_shapes=comm.sems, compiler_params=COMM_PARAMS,
    )(*comm.ro, *comm.inout)
    return outs[:n_io], outs[n_io:]


def _rdma(src, dst, send_sem, recv_sem, device):
    return pltpu.make_async_remote_copy(src_ref=src, dst_ref=dst, send_sem=send_sem, recv_sem=recv_sem,
                                        device_id=device, device_id_type=MESH)


def gather_comm(bufs, slices):
    n = len(slices)

    def region(io, s, h, chip):
        b, li, row_sharded = slices[s]
        ref = io[b]
        if row_sharded:
            r = ref.shape[1] // N_CHIPS
            return ref.at[li, pl.ds(chip * r + h * (r // 2), r // 2), :]
        K, nn = ref.shape[1], ref.shape[2] // N_CHIPS
        return ref.at[li, pl.ds(h * (K // 2), K // 2), pl.ds(chip * nn, nn)]

    def copies(io, sems):
        x, y, c = _position()
        me = 2 * x + y
        chips = _other_chips(x, y)
        ids = [2 * cx + cy for cx, cy in chips]

        def ici(s, k, src):
            return _rdma(region(io, s, c, me), region(io, s, c, src), sems[0].at[s, k], sems[1].at[s, k], (*chips[k], c))

        def d2d(s, k, h):
            return _rdma(region(io, s, h, ids[k]), region(io, s, h, ids[k]), sems[2].at[s, k], sems[3].at[s, k],
                         (x, y, 1 - c))

        return me, ids, c, ici, d2d

    pairs = [(s, k) for s in range(n) for k in range(3)]

    def start(ro, io, new, sems):
        me, _, _, ici, _ = copies(io, sems)
        for s, k in pairs:
            ici(s, k, me).start()

    def pass_on(ro, io, new, sems):
        _, ids, c, ici, d2d = copies(io, sems)
        for s, k in pairs:
            ici(s, k, ids[k]).wait_recv()
            d2d(s, k, c).start()

    def finish(ro, io, new, sems):
        me, _, c, ici, d2d = copies(io, sems)
        for s, k in pairs:
            d2d(s, k, 1 - c).wait_recv()
        for s, k in pairs:
            ici(s, k, me).wait_send()
            d2d(s, k, c).wait_send()

    return Comm(inout=bufs, sems=[pltpu.SemaphoreType.DMA((n, 3))] * 4,
                phases=[(0.0, start), (0.6, pass_on), (1.0, finish)])


def gather_small(shard, *, name):
    n = shard.shape[2]

    def body(s_ref, o_ref, send_sem, recv_sem, local_sem):
        x, y, c = _position()
        me = 2 * x + y
        chips = _other_chips(x, y)
        place = lambda chip: o_ref.at[:, :, pl.ds(chip * n, n)]
        local = pltpu.make_async_copy(s_ref, place(me), local_sem)
        local.start()
        for k in range(3):
            _rdma(s_ref, place(me), send_sem.at[k], recv_sem.at[k], (*chips[k], c)).start()
        for k in range(3):
            _rdma(s_ref, place(2 * chips[k][0] + chips[k][1]), send_sem.at[k], recv_sem.at[k], (*chips[k], c)).wait()
        local.wait()

    return pl.pallas_call(
        body, name=name, in_specs=[HBM_ANY], out_specs=HBM_ANY,
        out_shape=jax.ShapeDtypeStruct((shard.shape[0], shard.shape[1], N_CHIPS * n), shard.dtype),
        scratch_shapes=[pltpu.SemaphoreType.DMA((3,)), pltpu.SemaphoreType.DMA((3,)), pltpu.SemaphoreType.DMA],
        compiler_params=COMM_PARAMS,
    )(shard)


def swap_comm(grads):
    def copies(ro, new, sems):
        x, y, c = _position()
        h = 1 - c
        cps = []
        for s, (a, row_sharded) in enumerate(grads):
            if row_sharded:
                r = a.shape[1] // N_CHIPS
                for q in range(N_CHIPS):
                    cps.append(_rdma(ro[s].at[:, pl.ds(q * r + h * (r // 2), r // 2), :], new[s].at[:, q],
                                     sems[0].at[s, q], sems[1].at[s, q], (x, y, 1 - c)))
            else:
                K = a.shape[1]
                cps.append(_rdma(ro[s].at[:, pl.ds(h * (K // 2), K // 2), :], new[s],
                                 sems[0].at[s, 0], sems[1].at[s, 0], (x, y, 1 - c)))
        return cps

    def start(ro, io, new, sems):
        for cp in copies(ro, new, sems):
            cp.start()

    def finish(ro, io, new, sems):
        for cp in copies(ro, new, sems):
            cp.wait()

    theirs = []
    for a, row_sharded in grads:
        if row_sharded:
            theirs.append(jax.ShapeDtypeStruct((1, N_CHIPS, a.shape[1] // N_CHIPS // 2, a.shape[2]), a.dtype))
        else:
            theirs.append(jax.ShapeDtypeStruct((1, a.shape[1] // 2, a.shape[2]), a.dtype))
    return Comm(ro=[a for a, _ in grads], new=theirs, sems=[pltpu.SemaphoreType.DMA((len(grads), N_CHIPS))] * 2,
                phases=[(0.0, start), (1.0, finish)])


def merge_comms(comms):
    spans, ro, io, new, sems = [], [], [], [], []
    for cm in comms:
        spans.append((len(ro), len(io), len(new), len(sems)))
        ro, io, new, sems = ro + cm.ro, io + cm.inout, new + cm.new, sems + cm.sems

    def phase(frac):
        def run(R, I, N, S):
            for cm, (a, b, d, e) in zip(comms, spans):
                for f, fn in cm.phases:
                    if f == frac:
                        fn(R[a:a + len(cm.ro)], I[b:b + len(cm.inout)], N[d:d + len(cm.new)], S[e:e + len(cm.sems)])
        return run

    fracs = sorted({f for cm in comms for f, _ in cm.phases})
    merged = Comm(ro=ro, inout=io, new=new, sems=sems, phases=[(f, phase(f)) for f in fracs])
    cut = lambda created: [created[d:d + len(cm.new)] for cm, (_, _, d, _) in zip(comms, spans)]
    return merged, cut


def scatter_comm(halves):
    def copies(ro, new, sems):
        x, y, c = _position()
        chips = _other_chips(x, y)
        cps = []
        for s, (a, row_sharded) in enumerate(halves):
            for k in range(3):
                to = 2 * chips[k][0] + chips[k][1]
                if row_sharded:
                    src = ro[s].at[0, to]
                else:
                    n = a.shape[2] // N_CHIPS
                    src = ro[s].at[0, :, pl.ds(to * n, n)]
                cps.append(_rdma(src, new[s].at[k], sems[0].at[s, k], sems[1].at[s, k], (*chips[k], c)))
        return cps

    def start(ro, io, new, sems):
        for cp in copies(ro, new, sems):
            cp.start()

    def finish(ro, io, new, sems):
        for cp in copies(ro, new, sems):
            cp.wait()

    landing = [jax.ShapeDtypeStruct((3, a.shape[2], a.shape[3]) if row_sharded else
                                    (3, a.shape[1], a.shape[2] // N_CHIPS), a.dtype) for a, row_sharded in halves]
    return Comm(ro=[a for a, _ in halves], new=landing, sems=[pltpu.SemaphoreType.DMA((len(halves), 3))] * 2,
                phases=[(0.0, start), (1.0, finish)])


def sibling_join_halves(shards, *, name):
    n_in = len(shards)

    def body(*refs):
        ins = refs[:n_in]
        outs = refs[n_in:2 * n_in]
        send_sem, recv_sem = refs[2 * n_in:]
        x, y, c = _position()
        sibling = (x, y, 1 - c)

        def rows(ref, h):
            R2 = ref.shape[1] // 2
            return ref.at[:, pl.ds(h * R2, R2), :]

        sends = [pltpu.make_async_remote_copy(src_ref=rows(ins[i], c), dst_ref=rows(outs[i], c),
                                              send_sem=send_sem.at[i], recv_sem=recv_sem.at[i],
                                              device_id=sibling, device_id_type=MESH) for i in range(n_in)]
        lands = [pltpu.make_async_remote_copy(src_ref=rows(ins[i], c), dst_ref=rows(outs[i], 1 - c),
                                              send_sem=send_sem.at[i], recv_sem=recv_sem.at[i],
                                              device_id=sibling, device_id_type=MESH) for i in range(n_in)]
        for cp in sends:
            cp.start()
        for i in range(n_in):
            sends[i].wait_send()
            lands[i].wait_recv()

    return pl.pallas_call(
        body, name=name, in_specs=[HBM_ANY] * n_in, out_specs=[HBM_ANY] * n_in,
        out_shape=[jax.ShapeDtypeStruct(a.shape, a.dtype) for a in shards],
        input_output_aliases={i: i for i in range(n_in)},
        scratch_shapes=[pltpu.SemaphoreType.DMA((n_in,)), pltpu.SemaphoreType.DMA((n_in,))],
        compiler_params=COMM_PARAMS,
    )(*shards)


def all_reduce_small(v, *, name):
    R, Ccols = v.shape

    def body(v_ref, o_ref, slots, send_sem, recv_sem):
        x, y, c = _position()
        me = 4 * x + 2 * y + c
        slots[me] = v_ref[...]

        def peer(d):
            return x ^ (d >> 2), y ^ ((d >> 1) & 1), c ^ (d & 1)

        def copy(d, slot):
            return pltpu.make_async_remote_copy(
                src_ref=v_ref, dst_ref=slots.at[slot], send_sem=send_sem.at[d], recv_sem=recv_sem.at[d],
                device_id=peer(d), device_id_type=MESH)

        sends = [copy(d, me) for d in range(1, N_DEV)]
        for cp in sends:
            cp.start()
        for d in range(1, N_DEV):
            px, py, pc = peer(d)
            copy(d, 4 * px + 2 * py + pc).wait_recv()
        for cp in sends:
            cp.wait_send()
        acc = slots[0]
        for d in range(1, N_DEV):
            acc = acc + slots[d]
        o_ref[...] = acc

    return pl.pallas_call(
        body, name=name,
        in_specs=[pl.BlockSpec(memory_space=pltpu.VMEM)], out_specs=pl.BlockSpec(memory_space=pltpu.VMEM),
        out_shape=jax.ShapeDtypeStruct((R, Ccols), F32),
        scratch_shapes=[pltpu.VMEM((N_DEV, R, Ccols), F32), pltpu.SemaphoreType.DMA((N_DEV,)),
                        pltpu.SemaphoreType.DMA((N_DEV,))],
        compiler_params=COMM_PARAMS,
    )(v)


def _pack_small(parts):
    flat = jnp.concatenate([p.reshape(-1).astype(F32) for p in parts])
    n = flat.shape[0]
    rows = -(-n // LANES)
    rows = -(-rows // SUBLANES) * SUBLANES
    return jnp.pad(flat, (0, rows * LANES - n)).reshape(rows, LANES)


def _unpack_small(packed, like):
    flat = packed.reshape(-1)
    out, off = [], 0
    for p in like:
        out.append(flat[off:off + p.size].reshape(p.shape))
        off += p.size
    return out


def kernel(x, norm_mix, norm_ffn, norm_final, attn_w_in, attn_w_out, attn_sinks, hgrn_w_in, hgrn_w_out, hgrn_norm, hgrn_lb_logits, ffn_w_up, ffn_conv_w, ffn_conv_b, ffn_w_down, loss_target, m_norm_mix, m_norm_ffn, m_norm_final, m_attn_w_in, m_attn_w_out, m_attn_sinks, m_hgrn_w_in, m_hgrn_w_out, m_hgrn_norm, m_hgrn_lb_logits, m_ffn_w_up, m_ffn_conv_w, m_ffn_conv_b, m_ffn_w_down, v_norm_mix, v_norm_ffn, v_norm_final, v_attn_w_in, v_attn_w_out, v_attn_sinks, v_hgrn_w_in, v_hgrn_w_out, v_hgrn_norm, v_hgrn_lb_logits, v_ffn_w_up, v_ffn_conv_w, v_ffn_conv_b, v_ffn_w_down):
    S, D = x.shape[1], x.shape[2]
    depth = norm_mix.shape[0]
    n_q = D // HEAD_DIM
    n_kv = n_q // Q_PER_KV
    Fh = ffn_w_down.shape[1] * N_CHIPS
    h = x.reshape(S, D)
    target = loss_target.reshape(S, D)

    col_names = ["attn_w_in", "hgrn_w_in", "ffn_w_up"]
    row_names = ["attn_w_out", "hgrn_w_out", "ffn_w_down"]
    x_, y_, c_ = _position()
    me = 2 * x_ + y_
    pos = jnp.stack([me, c_]).astype(jnp.int32)
    col_w = [cast_place(w, pos, row_sharded=False, name=f"cast_place_{n}")
             for n, w in zip(col_names, (attn_w_in, hgrn_w_in, ffn_w_up))]
    row_w = [cast_place(w, pos, row_sharded=True, name=f"cast_place_{n}")
             for n, w in zip(row_names, (attn_w_out, hgrn_w_out, ffn_w_down))]
    W = dict(zip(col_names + row_names, col_w + row_w))
    is_row = {n: n in row_names for n in W}

    def layer_weights(layer):
        mixer = "attn" if layer % 2 == 0 else "hgrn"
        return [(f"{mixer}_w_in", layer // 2), (f"{mixer}_w_out", layer // 2), ("ffn_w_up", layer), ("ffn_w_down", layer)]

    def gather_of(group):
        names = [n for n, _ in group]
        return names, gather_comm([W[n] for n in names], [(i, li, is_row[n]) for i, (n, li) in enumerate(group)])

    def mixer_weights(layer):
        return layer_weights(layer)[:2] if layer < depth else []

    names_next, comm_next = gather_of(mixer_weights(0))
    filled, _ = run_comm(comm_next, name="gather_mixer0")
    W.update(zip(names_next, filled))
    cw = gather_small(ffn_conv_w, name="gather_conv_w").reshape(depth, 3, 2, Fh).transpose(0, 2, 1, 3)
    cb = ffn_conv_b.reshape(depth, 2, 1, Fh)
    g_mix = norm_mix.reshape(depth, 1, D)
    g_ffn = norm_ffn.reshape(depth, 1, D)

    slopes = jnp.exp2(-8.0 * jnp.arange(1, n_q + 1, dtype=F32) / n_q).reshape(n_kv, Q_PER_KV)
    slope_col = jnp.repeat(slopes, WINDOW, axis=1).reshape(n_kv, Q_PER_KV * WINDOW, 1)

    def to_heads(t, nh):
        return t.reshape(S, nh, HEAD_DIM).transpose(1, 0, 2)

    def from_heads(t):
        return t.transpose(1, 0, 2).reshape(S, -1)

    saved = []
    for layer in range(depth):
        idx = layer // 2
        rec = {"h_mix": h}
        hn, rec["hn_t"] = rmsnorm_fwd(h, g_mix, layer, name=f"norm_mix_fwd{layer}")
        is_attn = layer % 2 == 0
        names_next, comm_next = gather_of(layer_weights(layer)[2:] + ([] if is_attn else mixer_weights(layer + 1)))
        if is_attn:
            proj = mm_nn(hn, W["attn_w_in"], idx, name=f"attn_in_fwd{layer}", out_dtype=BF16)
            q = to_heads(proj[:, :n_q * HEAD_DIM], n_q).reshape(n_kv, Q_PER_KV, S, HEAD_DIM)
            k = to_heads(proj[:, n_q * HEAD_DIM:(n_q + n_kv) * HEAD_DIM], n_kv)
            v = to_heads(proj[:, (n_q + n_kv) * HEAD_DIM:], n_kv)
            sink_col = jnp.repeat(attn_sinks[idx].reshape(n_kv, Q_PER_KV), WINDOW, axis=1)
            sink_col = sink_col.reshape(n_kv, Q_PER_KV * WINDOW, 1)
            o_heads, filled, _ = attn_fwd(q, k, v, sink_col, slope_col, name=f"attn_fwd{layer}", comm=comm_next)
            W.update(zip(names_next, filled))
            o = from_heads(o_heads.reshape(n_q, S, HEAD_DIM))
            rec.update(q=q, k=k, v=v, sink_col=sink_col, o=o)
            h = mm_nn(o, W["attn_w_out"], idx, name=f"attn_out_fwd{layer}", res=h)
        else:
            proj = mm_nn(hn, W["hgrn_w_in"], idx, name=f"hgrn_in_fwd{layer}")
            ng = hgrn_norm[idx].reshape(1, HG_EXPAND)
            o, states, filled, _ = hgrn_fwd(proj, hgrn_lb_logits, ng, layer, name=f"hgrn_fwd{layer}", comm=comm_next)
            W.update(zip(names_next, filled))
            rec.update(proj=proj, ng=ng, states=states, o=o)
            h = mm_nn(o, W["hgrn_w_out"], idx, name=f"hgrn_out_fwd{layer}", res=h)
        rec["h_ffn"] = h
        hn2, rec["hn2_t"] = rmsnorm_fwd(h, g_ffn, layer, name=f"norm_ffn_fwd{layer}")
        if is_attn and layer + 1 < depth:
            names_next, comm_next = gather_of(mixer_weights(layer + 1))
            u, filled = mm_nn(hn2, W["ffn_w_up"], layer, name=f"ffn_up_fwd{layer}", split_out=True, out_dtype=BF16,
                              comm=comm_next)
            W.update(zip(names_next, filled))
        else:
            u = mm_nn(hn2, W["ffn_w_up"], layer, name=f"ffn_up_fwd{layer}", split_out=True, out_dtype=BF16)
        act, rec["act_t"], rec["c"] = convact_fwd(u, cw, cb, layer, name=f"ffn_act_fwd{layer}")
        rec["u"] = u
        h = mm_nn(act, W["ffn_w_down"], layer, name=f"ffn_down_fwd{layer}", res=h)
        saved.append(rec)

    loss_part, dh, d_norm_final = loss_head(h, norm_final.reshape(1, D), target, name="loss_head")

    n_attn, n_hgrn = attn_w_in.shape[0], hgrn_w_in.shape[0]
    reduced_parts = []
    pending = []

    def swap_of(group, g):
        ordered = sorted(group, key=lambda nl: is_row[nl[0]])
        return ordered, swap_comm([(g[n], is_row[n]) for n, _ in ordered])

    def pair_up(tag, ordered, g, theirs):
        pair, pair_bf16 = {}, {}
        for (n, _), t in zip(ordered, theirs):
            gn = g[n]
            if is_row[n]:
                r = gn.shape[1] // N_CHIPS
                gn = gn.reshape(1, N_CHIPS, 2, r // 2, gn.shape[2])
            pair[n], pair_bf16[n] = pair_add(gn, t, pos, row_sharded=is_row[n], name=f"rs_pair_add_{n}{tag}")
        pending.append((ordered, pair, scatter_comm([(pair_bf16[n], is_row[n]) for n, _ in ordered])))

    def landed_behind(created, cut):
        for (ordered, pair, _), landed in zip(pending, cut(created)):
            reduced_parts.append((ordered, pair, landed))
        pending.clear()

    d_norm_mix, d_norm_ffn = [None] * depth, [None] * depth
    d_conv_w, d_conv_b = [None] * depth, [None] * depth
    d_sinks, d_hgrn_norm = [None] * n_attn, [None] * n_hgrn
    d_logits = jnp.zeros_like(hgrn_lb_logits)
    for layer in reversed(range(depth)):
        idx = layer // 2
        rec = saved[layer]
        g = {"ffn_w_down": mm_dw(rec["act_t"], dh, name=f"ffn_down_dw{layer}")}
        dact = mm_nt(dh, W["ffn_w_down"], layer, name=f"ffn_down_dx{layer}")
        du, dcw, dcb = convact_bwd(rec["c"], rec["u"], cw, layer, dact, name=f"ffn_act_bwd{layer}")
        d_conv_w[layer] = dcw.transpose(1, 0, 2).reshape(3, 2 * Fh)
        d_conv_b[layer] = dcb.reshape(2 * Fh)
        g["ffn_w_up"] = mm_dw(rec["hn2_t"], du, name=f"ffn_up_dw{layer}", split_b=True)
        ordered, swap = swap_of(layer_weights(layer)[2:], g)
        dhn2, theirs = mm_nt(du, W["ffn_w_up"], layer, name=f"ffn_up_dx{layer}", split_in=True, comm=swap)
        pair_up(layer, ordered, g, theirs)
        dh, dg = rmsnorm_bwd(rec["h_ffn"], g_ffn, layer, dhn2, dh, name=f"norm_ffn_bwd{layer}")
        d_norm_ffn[layer] = dg.reshape(D)
        carried, cut = merge_comms([p[2] for p in pending])
        if layer % 2 == 0:
            g["attn_w_out"] = mm_tn(rec["o"], dh, name=f"attn_out_dw{layer}")
            do = mm_nt(dh, W["attn_w_out"], idx, name=f"attn_out_dx{layer}", out_dtype=BF16)
            do_heads = to_heads(do, n_q).reshape(n_kv, Q_PER_KV, S, HEAD_DIM)
            res = attn_bwd(rec["q"], rec["k"], rec["v"], rec["sink_col"], slope_col, do_heads,
                           name=f"attn_bwd{layer}", comm=carried)
            dq, dk, dv, dsink = res[:4]
            landed_behind(res[-1], cut)
            dproj = jnp.concatenate([from_heads(dq.reshape(n_q, S, HEAD_DIM)), from_heads(dk), from_heads(dv)], axis=1)
            d_sinks[idx] = jnp.sum(dsink.reshape(n_kv, Q_PER_KV, WINDOW), axis=-1).reshape(n_q)
            g["attn_w_in"] = mm_dw(rec["hn_t"], dproj, name=f"attn_in_dw{layer}")
            ordered, swap = swap_of(layer_weights(layer)[:2], g)
            dhn, theirs = mm_nt(dproj, W["attn_w_in"], idx, name=f"attn_in_dx{layer}", comm=swap)
        else:
            g["hgrn_w_out"] = mm_tn(rec["o"], dh, name=f"hgrn_out_dw{layer}")
            do = mm_nt(dh, W["hgrn_w_out"], idx, name=f"hgrn_out_dx{layer}")
            res = hgrn_bwd(rec["proj"], hgrn_lb_logits, rec["ng"], rec["states"], do, layer,
                           name=f"hgrn_bwd{layer}", comm=carried)
            dproj, dlg, dng = res[:3]
            landed_behind(res[-1], cut)
            d_logits = d_logits + dlg
            d_hgrn_norm[idx] = dng.reshape(HG_EXPAND)
            g["hgrn_w_in"] = mm_dw(rec["hn_t"], dproj, name=f"hgrn_in_dw{layer}")
            ordered, swap = swap_of(layer_weights(layer)[:2], g)
            dhn, theirs = mm_nt(dproj, W["hgrn_w_in"], idx, name=f"hgrn_in_dx{layer}", comm=swap)
        pair_up(layer, ordered, g, theirs)
        dh, dg = rmsnorm_bwd(rec["h_mix"], g_mix, layer, dhn, dh, name=f"norm_mix_bwd{layer}")
        d_norm_mix[layer] = dg.reshape(D)
    grad_x = dh.reshape(x.shape)
    carried, cut = merge_comms([p[2] for p in pending])
    landed_behind(run_comm(carried, name="rs_scatter_last")[1], cut)

    small_w = [norm_mix, norm_ffn, norm_final, attn_sinks, hgrn_norm, hgrn_lb_logits, ffn_conv_b]
    small_m = [m_norm_mix, m_norm_ffn, m_norm_final, m_attn_sinks, m_hgrn_norm, m_hgrn_lb_logits, m_ffn_conv_b]
    small_v = [v_norm_mix, v_norm_ffn, v_norm_final, v_attn_sinks, v_hgrn_norm, v_hgrn_lb_logits, v_ffn_conv_b]
    small_grads = [jnp.stack(d_norm_mix), jnp.stack(d_norm_ffn), d_norm_final.reshape(D), jnp.stack(d_sinks),
                   jnp.stack(d_hgrn_norm), d_logits, jnp.stack(d_conv_b)]
    reduced = all_reduce_small(_pack_small(small_grads + [loss_part]), name="all_reduce_small")
    small_g = _unpack_small(reduced, small_w + [loss_part])
    loss = small_g.pop().reshape(())
    pad = [jnp.zeros((1, 1), F32)]
    sd, sm_, sv_ = adamw(_pack_small(small_w + pad), reduced, _pack_small(small_m + pad),
                         _pack_small(small_v + pad), name="adamw_small")
    small_d = _unpack_small(sd, small_w)
    small_nm = _unpack_small(sm_, small_w)
    small_nv = _unpack_small(sv_, small_w)

    big_names = col_names + row_names
    big_w = dict(attn_w_in=attn_w_in, hgrn_w_in=hgrn_w_in, ffn_w_up=ffn_w_up, attn_w_out=attn_w_out,
                 hgrn_w_out=hgrn_w_out, ffn_w_down=ffn_w_down)
    half_shard = dict.fromkeys(big_names)
    for ordered, pair, landed in reduced_parts:
        for (n, li), ld in zip(ordered, landed):
            half_shard[n] = chip_add(pair[n], ld, pos, half_shard[n], li, big_w[n].shape[0], row_sharded=is_row[n],
                                     name=f"rs_chip_add_{n}{li}")
    shards = sibling_join_halves([half_shard[n].reshape(big_w[n].shape) for n in big_names], name="rs_sibling_join")
    big_m = dict(attn_w_in=m_attn_w_in, hgrn_w_in=m_hgrn_w_in, ffn_w_up=m_ffn_w_up, attn_w_out=m_attn_w_out,
                 hgrn_w_out=m_hgrn_w_out, ffn_w_down=m_ffn_w_down)
    big_v = dict(attn_w_in=v_attn_w_in, hgrn_w_in=v_hgrn_w_in, ffn_w_up=v_ffn_w_up, attn_w_out=v_attn_w_out,
                 hgrn_w_out=v_hgrn_w_out, ffn_w_down=v_ffn_w_down)
    grads, deltas, new_m, new_v = {}, {}, {}, {}
    for name_, gshard in zip(big_names, shards):
        w_ = big_w[name_]
        cols = w_.shape[-1]
        d_, m_, v_ = adamw(w_.reshape(-1, cols), gshard.reshape(-1, cols), big_m[name_].reshape(-1, cols),
                           big_v[name_].reshape(-1, cols), name=f"adamw_{name_}")
        grads[name_] = gshard.reshape(w_.shape)
        deltas[name_], new_m[name_], new_v[name_] = (t.reshape(w_.shape) for t in (d_, m_, v_))

    gconv = jnp.stack(d_conv_w)
    n_conv = ffn_conv_w.shape[2]
    gconv_all = _unpack_small(all_reduce_small(_pack_small([gconv]), name="all_reduce_conv_w"), [gconv])[0]
    gconv_mine = lax.dynamic_slice_in_dim(gconv_all, me * n_conv, n_conv, axis=2)
    cshape = ffn_conv_w.shape
    d_, m_, v_ = adamw(ffn_conv_w.reshape(-1, n_conv), gconv_mine.reshape(-1, n_conv),
                       m_ffn_conv_w.reshape(-1, n_conv), v_ffn_conv_w.reshape(-1, n_conv), name="adamw_ffn_conv_w")
    grads["ffn_conv_w"] = gconv_mine
    deltas["ffn_conv_w"], new_m["ffn_conv_w"], new_v["ffn_conv_w"] = (t.reshape(cshape) for t in (d_, m_, v_))

    order = ["norm_mix", "norm_ffn", "norm_final", "attn_w_in", "attn_w_out", "attn_sinks", "hgrn_w_in",
             "hgrn_w_out", "hgrn_norm", "hgrn_lb_logits", "ffn_w_up", "ffn_conv_w", "ffn_conv_b", "ffn_w_down"]
    small_names = ["norm_mix", "norm_ffn", "norm_final", "attn_sinks", "hgrn_norm", "hgrn_lb_logits", "ffn_conv_b"]
    for i, name_ in enumerate(small_names):
        grads[name_], deltas[name_], new_m[name_], new_v[name_] = small_g[i], small_d[i], small_nm[i], small_nv[i]
    return (loss, grad_x, *[grads[n] for n in order], *[deltas[n] for n in order],
            *[new_m[n] for n in order], *[new_v[n] for n in order])
```

```python
import functools
import math

import numpy as np
import jax
import jax.numpy as jnp
from jax import lax
from jax.experimental import pallas as pl
from jax.experimental.pallas import tpu as pltpu

F32 = jnp.float32
BF16 = jnp.bfloat16
MESH = pl.DeviceIdType.MESH

HEAD_DIM = 64
Q_PER_KV = 4
WINDOW = 128
HG_EXPAND = 128
HG_CHUNK = 128
EPS = 1e-6
N_CHIPS = 4
N_DEV = 8

ADAM_LR = 0.001
ADAM_B1 = 0.9
ADAM_B2 = 0.999
ADAM_EPS = 1e-08
ADAM_WD = 0.01
ADAM_STEP = 10

VMEM_LIMIT_BYTES = 48 * 1024 * 1024
LANES = 128
SUBLANES = 8
HALO_ROWS = 16


def _params(sem=None):
    return pltpu.CompilerParams(dimension_semantics=sem, vmem_limit_bytes=VMEM_LIMIT_BYTES)


def _pick(n, cands):
    for c in cands:
        if n % c == 0:
            return c
    return n


def _dot(a, b, dims):
    return lax.dot_general(a.astype(BF16), b.astype(BF16), (dims, ((), ())),
                           preferred_element_type=F32)


NN = ((1,), (0,))
NT = ((1,), (1,))
TN = ((0,), (0,))


def mm_nn(a, w, l, *, name, res=None, out_dtype=F32, split_out=False, comm=None):
    M, K = a.shape
    N = w.shape[2]
    tm = _pick(M, (1024, 512, 256, 128))
    tn = _pick(N // 2 if split_out else N, (1408, 1024, 768, 512, 256, 128))
    tk = _pick(K, (1024, 1408, 512, 256, 128))
    nk = K // tk
    nh = (N // 2) // tn

    def body(*refs):
        if res is None:
            a_ref, w_ref, o_ref = refs[:3]
            r_ref = None
        else:
            a_ref, w_ref, r_ref, o_ref = refs[:4]
        part = _dot(a_ref[...], w_ref[...], NN)

        def finish(acc):
            if r_ref is not None:
                acc = acc + r_ref[...]
            o_ref[...] = acc.astype(out_dtype)

        if nk == 1:
            finish(part)
        else:
            acc_ref = refs[-1]
            k = pl.program_id(2)

            @pl.when(k == 0)
            def _():
                acc_ref[...] = part

            @pl.when(k > 0)
            def _():
                acc_ref[...] += part

            @pl.when(k == nk - 1)
            def _():
                finish(acc_ref[...])

    in_specs = [pl.BlockSpec((tm, tk), lambda n, m, k: (m, k)),
                pl.BlockSpec((None, tk, tn), lambda n, m, k: (l, k, n))]
    args = [a, w]
    if res is not None:
        in_specs.append(pl.BlockSpec((tm, tn), lambda n, m, k: (m, n)))
        args.append(res)
    if split_out:
        out_shape = jax.ShapeDtypeStruct((2, M, N // 2), out_dtype)
        out_spec = pl.BlockSpec((None, tm, tn), lambda n, m, k: (n // nh, m, n % nh))
    else:
        out_shape = jax.ShapeDtypeStruct((M, N), out_dtype)
        out_spec = pl.BlockSpec((tm, tn), lambda n, m, k: (m, n))
    outs, updated, _ = _carried_call(
        body, args, name=name, grid=(N // tn, M // tm, nk), in_specs=in_specs, out_specs=[out_spec],
        out_shape=[out_shape], scratch_shapes=[] if nk == 1 else [pltpu.VMEM((tm, tn), F32)], comm=comm)
    return outs[0] if comm is None else (outs[0], updated)


def mm_nt(a, w, l, *, name, split_in=False, out_dtype=F32, comm=None, norm=None):
    if split_in:
        _, M, Nh = a.shape
        N = 2 * Nh
    else:
        M, N = a.shape
        Nh = N
    K = w.shape[1]
    tm = _pick(M, (512, 256, 128) if norm is not None else (1024, 512, 256, 128))
    tko = _pick(K, (1024, 1408, 512, 256, 128))
    tr = _pick(Nh, (1024, 1408, 768, 512, 256, 128))
    nr = N // tr
    nh = Nh // tr
    n_in, n_out = (5, 2) if norm is not None else (2, 1)
    assert norm is None or tko == K

    def body(*refs):
        a_ref, w_ref = refs[:2]
        o_ref = refs[n_in]
        part = _dot(a_ref[...], w_ref[...], NT)

        def finish(acc):
            if norm is None:
                o_ref[...] = acc.astype(out_dtype)
                return
            h_ref, g_ref, dres_ref = refs[2:5]
            dg_ref = refs[n_in + 1]
            _, vjp = jax.vjp(_rms, h_ref[...], g_ref[...])
            dx, dg = vjp(acc)
            o_ref[...] = dres_ref[...] + dx
            first = pl.program_id(1) == 0

            @pl.when(first)
            def _():
                dg_ref[...] = dg

            @pl.when(jnp.logical_not(first))
            def _():
                dg_ref[...] += dg

        if nr == 1:
            finish(part)
        else:
            acc_ref = refs[n_in + n_out]
            r = pl.program_id(2)

            @pl.when(r == 0)
            def _():
                acc_ref[...] = part

            @pl.when(r > 0)
            def _():
                acc_ref[...] += part

            @pl.when(r == nr - 1)
            def _():
                finish(acc_ref[...])

    if split_in:
        a_spec = pl.BlockSpec((None, tm, tr), lambda ko, m, r: (r // nh, m, r % nh))
    else:
        a_spec = pl.BlockSpec((tm, tr), lambda ko, m, r: (m, r))
    args = [a, w]
    in_specs = [a_spec, pl.BlockSpec((None, tko, tr), lambda ko, m, r: (l, ko, r))]
    out_specs = [pl.BlockSpec((tm, tko), lambda ko, m, r: (m, ko))]
    out_shape = [jax.ShapeDtypeStruct((M, K), out_dtype)]
    if norm is not None:
        h, g, l_norm, dres = norm
        row = pl.BlockSpec((tm, K), lambda ko, m, r: (m, 0))
        args += [h, g, dres]
        in_specs += [row, pl.BlockSpec((None, 1, K), lambda ko, m, r: (l_norm, 0, 0)), row]
        out_specs.append(pl.BlockSpec((1, K), lambda ko, m, r: (0, 0)))
        out_shape.append(jax.ShapeDtypeStruct((1, K), F32))
    outs, _, created = _carried_call(
        body, args, name=name, grid=(K // tko, M // tm, nr), in_specs=in_specs, out_specs=out_specs,
        out_shape=out_shape, scratch_shapes=[] if nr == 1 else [pltpu.VMEM((tm, tko), F32)], comm=comm)
    outs = list(outs) + ([] if comm is None else [created])
    return outs[0] if len(outs) == 1 else tuple(outs)


def mm_tn(a, b, *, name):
    M, K = a.shape
    N = b.shape[1]
    tm = _pick(M, (512, 256, 128))
    tko = _pick(K, (1024, 1408, 704, 512, 256, 128))
    tn = _pick(N, (512, 256, 128))
    nm = M // tm

    def body(a_ref, b_ref, o_ref):
        part = _dot(a_ref[...], b_ref[...], TN)
        r = pl.program_id(2)

        @pl.when(r == 0)
        def _():
            o_ref[...] = part

        @pl.when(r > 0)
        def _():
            o_ref[...] += part

    return pl.pallas_call(
        body, name=name, grid=(K // tko, N // tn, nm),
        in_specs=[pl.BlockSpec((tm, tko), lambda ko, n, r: (r, ko)), pl.BlockSpec((tm, tn), lambda ko, n, r: (r, n))],
        out_specs=pl.BlockSpec((None, tko, tn), lambda ko, n, r: (0, ko, n)),
        out_shape=jax.ShapeDtypeStruct((1, K, N), F32),
        compiler_params=_params(("parallel", "parallel", "arbitrary")),
    )(a, b)


def mm_dw(at, b, *, name, split_b=False):
    K, M = at.shape
    if split_b:
        Nh = b.shape[2]
        N = 2 * Nh
    else:
        N = b.shape[1]
        Nh = N
    tm = _pick(M, (1024, 512, 256, 128))
    tko = _pick(K, (1024, 1408, 704, 512, 256, 128))
    tn = _pick(Nh, (1408, 1024, 768, 512, 256, 128))
    nm = M // tm
    nh = Nh // tn

    def body(a_ref, b_ref, o_ref):
        part = _dot(a_ref[...], b_ref[...], NN)
        r = pl.program_id(2)

        @pl.when(r == 0)
        def _():
            o_ref[...] = part

        @pl.when(r > 0)
        def _():
            o_ref[...] += part

    if split_b:
        b_spec = pl.BlockSpec((None, tm, tn), lambda ko, n, r: (n // nh, r, n % nh))
    else:
        b_spec = pl.BlockSpec((tm, tn), lambda ko, n, r: (r, n))
    return pl.pallas_call(
        body, name=name, grid=(K // tko, N // tn, nm),
        in_specs=[pl.BlockSpec((tko, tm), lambda ko, n, r: (ko, r)), b_spec],
        out_specs=pl.BlockSpec((None, tko, tn), lambda ko, n, r: (0, ko, n)),
        out_shape=jax.ShapeDtypeStruct((1, K, N), F32),
        compiler_params=_params(("parallel", "parallel", "arbitrary")),
    )(at, b)


def _rms(x, g):
    return x * lax.rsqrt(jnp.mean(x * x, axis=-1, keepdims=True) + EPS) * g


def rmsnorm_fwd(h, g, l, *, name):
    S, D = h.shape
    tm = _pick(S, (512, 256, 128))

    def body(h_ref, g_ref, o_ref, ot_ref):
        y = _rms(h_ref[...], g_ref[...])
        o_ref[...] = y.astype(BF16)
        ot_ref[...] = y.T.astype(BF16)

    return pl.pallas_call(
        body, name=name, grid=(S // tm,),
        in_specs=[pl.BlockSpec((tm, D), lambda m: (m, 0)),
                  pl.BlockSpec((None, 1, D), lambda m: (l, 0, 0))],
        out_specs=[pl.BlockSpec((tm, D), lambda m: (m, 0)), pl.BlockSpec((D, tm), lambda m: (0, m))],
        out_shape=[jax.ShapeDtypeStruct((S, D), BF16), jax.ShapeDtypeStruct((D, S), BF16)],
        compiler_params=_params(("parallel",)),
    )(h, g)


def loss_head(h, g, target, *, name):
    S, D = h.shape
    tm = _pick(S, (512, 256, 128))

    def body(h_ref, g_ref, t_ref, loss_ref, dh_ref, dg_ref):
        y, vjp = jax.vjp(_rms, h_ref[...], g_ref[...])
        err = y - t_ref[...]
        part = 0.5 * jnp.sum(jnp.sum(err * err, axis=-1, keepdims=True) / D, axis=0, keepdims=True)
        dx, dg = vjp(err / D)
        dh_ref[...] = dx

        @pl.when(pl.program_id(0) == 0)
        def _():
            dg_ref[...] = dg
            loss_ref[...] = part

        @pl.when(pl.program_id(0) > 0)
        def _():
            dg_ref[...] += dg
            loss_ref[...] += part

    row = pl.BlockSpec((tm, D), lambda m: (m, 0))
    return pl.pallas_call(
        body, name=name, grid=(S // tm,),
        in_specs=[row, pl.BlockSpec((1, D), lambda m: (0, 0)), row],
        out_specs=[pl.BlockSpec((1, 1), lambda m: (0, 0)), row, pl.BlockSpec((1, D), lambda m: (0, 0))],
        out_shape=[jax.ShapeDtypeStruct((1, 1), F32), jax.ShapeDtypeStruct((S, D), F32),
                   jax.ShapeDtypeStruct((1, D), F32)],
        compiler_params=_params(("arbitrary",)),
    )(h, g, target)


def _make_mm(dims_fwd, dims_da, dims_db, swap_da=False, swap_db=False):
    @jax.custom_vjp
    def mm(a, b):
        return _dot(a, b, dims_fwd)

    def fwd(a, b):
        return mm(a, b), (a, b)

    def bwd(resid, g):
        a, b = resid
        da = _dot(b, g, dims_da) if swap_da else _dot(g, b, dims_da)
        db = _dot(g, a, dims_db) if swap_db else _dot(a, g, dims_db)
        return da, db

    mm.defvjp(fwd, bwd)
    return mm


_mm = _make_mm(NN, NT, TN)
_mm_nt = _make_mm(NT, NN, TN, swap_db=True)
_mm_tn = _make_mm(TN, NT, NN, swap_da=True)


def _dot_hi(a, b):
    return jnp.dot(a, b, precision=lax.Precision.HIGHEST, preferred_element_type=F32)


def _sigmoid(x):
    return 0.5 * jnp.tanh(0.5 * x) + 0.5


def _silu(x):
    return x * _sigmoid(x)


def _row_slices(x, n):
    rows = x.shape[0] // n
    return tuple(x[i * rows:(i + 1) * rows] for i in range(n))


@functools.partial(jax.custom_vjp, nondiff_argnums=(1,))
def _split_rows(x, n):
    return _row_slices(x, n)


_split_rows.defvjp(lambda x, n: (_row_slices(x, n), None), lambda n, _, cts: (jnp.concatenate(cts, axis=0),))


def _attn_block(qs, kws, vws, sink, slope, first):
    s = jnp.concatenate([_mm_nt(q, kw) for q, kw in zip(qs, kws)], axis=0) * (HEAD_DIM ** -0.5)
    qi = lax.broadcasted_iota(jnp.int32, s.shape, 0) % WINDOW
    ki = lax.broadcasted_iota(jnp.int32, s.shape, 1)
    dist = qi + WINDOW - ki
    valid = (dist >= 0) & (dist < WINDOW) & (ki >= WINDOW * first)
    s = jnp.where(valid, s - slope * dist.astype(F32), -jnp.inf)
    m = lax.stop_gradient(jnp.maximum(jnp.max(s, axis=-1, keepdims=True), sink))
    e = jnp.exp(s - m)
    denom = jnp.sum(e, axis=-1, keepdims=True) + jnp.exp(sink - m)
    ps = _split_rows(e * (1.0 / denom), len(qs))
    return tuple(_mm(p, vw) for p, vw in zip(ps, vws))


def _attn_specs(Hkv, order):
    G = Q_PER_KV
    qspec = pl.BlockSpec((Hkv, G, WINDOW, HEAD_DIM), lambda n: (0, 0, order(n), 0))
    prev = pl.BlockSpec((Hkv, WINDOW, HEAD_DIM), lambda n: (0, jnp.maximum(order(n) - 1, 0), 0))
    cur = pl.BlockSpec((Hkv, WINDOW, HEAD_DIM), lambda n: (0, order(n), 0))
    col = pl.BlockSpec((Hkv, G * WINDOW, 1), lambda n: (0, 0, 0))
    return qspec, prev, cur, col


def attn_fwd(q, k, v, sink_col, slope_col, *, name, comm=None):
    Hkv, G, S, _ = q.shape
    qspec, prev, cur, col = _attn_specs(Hkv, lambda n: n)

    def body(q_ref, kp_ref, kc_ref, vp_ref, vc_ref, sink_ref, slope_ref, o_ref):
        first = (pl.program_id(0) == 0).astype(jnp.int32)
        qs = tuple(q_ref[h].reshape(G * WINDOW, HEAD_DIM) for h in range(Hkv))
        kws = tuple(jnp.concatenate([kp_ref[h], kc_ref[h]], axis=0) for h in range(Hkv))
        vws = tuple(jnp.concatenate([vp_ref[h], vc_ref[h]], axis=0) for h in range(Hkv))
        outs = _attn_block(qs, kws, vws, sink_ref[...].reshape(Hkv * G * WINDOW, 1),
                           slope_ref[...].reshape(Hkv * G * WINDOW, 1), first)
        o_ref[...] = jnp.stack([o.reshape(G, WINDOW, HEAD_DIM).astype(BF16) for o in outs])

    outs, updated, created = _carried_call(
        body, [q, k, k, v, v, sink_col, slope_col], name=name, grid=(S // WINDOW,),
        in_specs=[qspec, prev, cur, prev, cur, col, col], out_specs=[qspec],
        out_shape=[jax.ShapeDtypeStruct(q.shape, BF16)], comm=comm)
    return outs[0] if comm is None else (outs[0], updated, created)


def attn_bwd(q, k, v, sink_col, slope_col, do, *, name, comm=None):
    Hkv, G, S, _ = q.shape
    nb = S // WINDOW
    qspec, prev, cur, col = _attn_specs(Hkv, lambda n: nb - 1 - n)

    def body(q_ref, kp_ref, kc_ref, vp_ref, vc_ref, sink_ref, slope_ref, do_ref,
             dq_ref, dk_ref, dv_ref, dsink_ref, dk_carry, dv_carry):
        @pl.when(pl.program_id(0) == 0)
        def _():
            dk_carry[...] = jnp.zeros_like(dk_carry)
            dv_carry[...] = jnp.zeros_like(dv_carry)
            dsink_ref[...] = jnp.zeros_like(dsink_ref)

        first = (pl.program_id(0) == nb - 1).astype(jnp.int32)
        qs = tuple(q_ref[h].reshape(G * WINDOW, HEAD_DIM).astype(F32) for h in range(Hkv))
        kws = tuple(jnp.concatenate([kp_ref[h], kc_ref[h]], axis=0).astype(F32) for h in range(Hkv))
        vws = tuple(jnp.concatenate([vp_ref[h], vc_ref[h]], axis=0).astype(F32) for h in range(Hkv))
        fn = functools.partial(_attn_block, slope=slope_ref[...].reshape(Hkv * G * WINDOW, 1), first=first)
        _, vjp = jax.vjp(fn, qs, kws, vws, sink_ref[...].reshape(Hkv * G * WINDOW, 1))
        dqs, dkws, dvws, dsink = vjp(tuple(do_ref[h].reshape(G * WINDOW, HEAD_DIM).astype(F32) for h in range(Hkv)))
        dkw, dvw = jnp.stack(dkws), jnp.stack(dvws)
        dq_ref[...] = jnp.stack([dq.reshape(G, WINDOW, HEAD_DIM) for dq in dqs]).astype(BF16)
        dk_ref[...] = (dkw[:, WINDOW:] + dk_carry[...]).astype(BF16)
        dv_ref[...] = (dvw[:, WINDOW:] + dv_carry[...]).astype(BF16)
        dk_carry[...] = dkw[:, :WINDOW]
        dv_carry[...] = dvw[:, :WINDOW]
        dsink_ref[...] += dsink.reshape(Hkv, G * WINDOW, 1)

    kv_shape = jax.ShapeDtypeStruct(k.shape, BF16)
    outs, _, created = _carried_call(
        body, [q, k, k, v, v, sink_col, slope_col, do], name=name, grid=(nb,),
        in_specs=[qspec, prev, cur, prev, cur, col, col, qspec],
        out_specs=[qspec, cur, cur, col],
        out_shape=[jax.ShapeDtypeStruct(q.shape, BF16), kv_shape, kv_shape,
                   jax.ShapeDtypeStruct(sink_col.shape, F32)],
        scratch_shapes=[pltpu.VMEM((Hkv, WINDOW, HEAD_DIM), F32), pltpu.VMEM((Hkv, WINDOW, HEAD_DIM), F32)],
        comm=comm)
    return outs if comm is None else (*outs, created)


def _hgrn_consts():
    C = HG_CHUNK
    t = np.arange(C)[:, None]
    j = np.arange(C)[None, :]
    cq, ck, mk = [], [], []
    H = C // 2
    while H >= 1:
        start = (t // H) * H
        cq.append((j > start) & (j <= t))
        ck.append((j > t) & (j <= start + H))
        mk.append(((t // H) % 2 == 1) & ((j // H) == (t // H) - 1))
        H //= 2
    ltri = (j <= t)
    ops = np.concatenate([ltri] + cq + ck, axis=0).astype(np.float32)
    ops3 = np.concatenate([ops, ops, ops], axis=1)
    opst3 = np.concatenate([ops.T, ops.T, ops.T], axis=1)
    return (jnp.asarray(ops3, BF16), jnp.asarray(opst3, BF16), jnp.asarray(np.stack(mk).astype(np.float32)))


def _split3(x):
    hi = x.astype(BF16)
    r = x - hi.astype(F32)
    mid = r.astype(BF16)
    lo = (r - mid.astype(F32)).astype(BF16)
    return jnp.concatenate([hi, mid, lo], axis=0)


@jax.custom_vjp
def _seg_sums(g, ops3, opst3):
    C = g.shape[0]
    out = jnp.dot(ops3, _split3(g), preferred_element_type=F32)
    return tuple(out[i * C:(i + 1) * C] for i in range(ops3.shape[0] // C))


def _seg_sums_fwd(g, ops3, opst3):
    return _seg_sums(g, ops3, opst3), opst3


def _seg_sums_bwd(opst3, cts):
    ct = jnp.concatenate(cts, axis=0)
    return jnp.dot(opst3, _split3(ct), preferred_element_type=F32), None, None


_seg_sums.defvjp(_seg_sums_fwd, _seg_sums_bwd)


def _head_slices(x):
    return tuple(x[:, h * HG_EXPAND:(h + 1) * HG_EXPAND] for h in range(x.shape[1] // HG_EXPAND))


@jax.custom_vjp
def _split_heads(x):
    return _head_slices(x)


_split_heads.defvjp(lambda x: (_head_slices(x), None), lambda _, cts: (jnp.concatenate(cts, axis=1),))


@jax.custom_vjp
def _join_heads(xs):
    return jnp.concatenate(xs, axis=1)


_join_heads.defvjp(lambda xs: (jnp.concatenate(xs, axis=1), None), lambda _, ct: (_head_slices(ct),))


def _hgrn_chunk(qr, fr, iv, gr, sts, logits, ng, ops3, opst3, mk, layer):
    n_levels = mk.shape[0]
    depth, D = logits.shape
    e = jnp.exp(logits - lax.stop_gradient(jnp.max(logits, axis=0, keepdims=True)))
    sm = e / jnp.sum(e, axis=0, keepdims=True)
    row = lax.broadcasted_iota(jnp.int32, (depth, D), 0)
    lb = jnp.sum(jnp.where((row >= 1) & (row <= layer), sm, 0.0), axis=0, keepdims=True)

    q = _silu(qr)
    a = jnp.log(lb)
    c = jnp.log(1.0 - lb) - (jnp.maximum(-fr, 0.0) + jnp.log(1.0 + jnp.exp(-jnp.abs(fr))))
    mx = lax.stop_gradient(jnp.maximum(a, c))
    g = mx + jnp.log(jnp.exp(a - mx) + jnp.exp(c - mx))
    k = (1.0 - lb) * _sigmoid(-fr)

    sums = _seg_sums(g, ops3, opst3)
    b = sums[0]
    b_last = jnp.sum(g, axis=0, keepdims=True)

    ql = [_split_heads(q * jnp.exp(sums[1 + lv])) for lv in range(n_levels)]
    kl = [_split_heads(k * jnp.exp(sums[1 + n_levels + lv])) for lv in range(n_levels)]
    qb, kb = _split_heads(q * jnp.exp(b)), _split_heads(k * jnp.exp(b_last - b))
    qk, ivh, gate, decay = _split_heads(q * k), _split_heads(iv), _split_heads(_silu(gr)), _split_heads(jnp.exp(b_last))
    outs, new_sts = [], []
    for h, st in enumerate(sts):
        amat = None
        for lv in range(n_levels):
            part = _mm_nt(ql[lv][h], kl[lv][h]) * mk[lv]
            amat = part if amat is None else amat + part
        o = jnp.sum(qk[h], axis=-1, keepdims=True) * ivh[h] + _mm(amat, ivh[h]) + _mm_nt(qb[h], st)
        new_sts.append(st * decay[h] + _mm_tn(ivh[h], kb[h]))
        on = o * lax.rsqrt(jnp.mean(o * o, axis=-1, keepdims=True) + EPS) * ng
        outs.append(on * gate[h])
    return _join_heads(tuple(outs)), tuple(new_sts)


def _hgrn_in_specs(D, depth, order):
    C = HG_CHUNK
    nl = int(math.log2(C))
    n_ops = 1 + 2 * nl
    return [pl.BlockSpec((C, 4 * D), lambda c: (order(c), 0)),
            pl.BlockSpec((depth, D), lambda c: (0, 0)),
            pl.BlockSpec((1, HG_EXPAND), lambda c: (0, 0)),
            pl.BlockSpec((n_ops * C, 3 * C), lambda c: (0, 0)),
            pl.BlockSpec((C, 3 * n_ops * C), lambda c: (0, 0)),
            pl.BlockSpec((nl, C, C), lambda c: (0, 0, 0))]


def hgrn_fwd(proj, logits, ng, layer, *, name, comm=None):
    S = proj.shape[0]
    D = proj.shape[1] // 4
    H = D // HG_EXPAND
    C = HG_CHUNK
    nc = S // C
    consts = _hgrn_consts()

    def body(p_ref, lg_ref, ng_ref, ops3_ref, opst3_ref, mk_ref, o_ref, st_out_ref, st_ref):
        @pl.when(pl.program_id(0) == 0)
        def _():
            st_ref[...] = jnp.zeros_like(st_ref)

        st_out_ref[...] = st_ref[...]
        out, new_sts = _hgrn_chunk(*(p_ref[:, j * D:(j + 1) * D] for j in range(4)), tuple(st_ref[h] for h in range(H)),
                                   lg_ref[...], ng_ref[...], ops3_ref[...], opst3_ref[...], mk_ref[...], layer)
        o_ref[...] = out.astype(BF16)
        for h in range(H):
            st_ref[h] = new_sts[h]

    outs, updated, created = _carried_call(
        body, [proj, logits, ng, *consts], name=name, grid=(nc,),
        in_specs=_hgrn_in_specs(D, logits.shape[0], lambda c: c),
        out_specs=[pl.BlockSpec((C, D), lambda c: (c, 0)),
                   pl.BlockSpec((None, H, HG_EXPAND, HG_EXPAND), lambda c: (c, 0, 0, 0))],
        out_shape=[jax.ShapeDtypeStruct((S, D), BF16),
                   jax.ShapeDtypeStruct((nc, H, HG_EXPAND, HG_EXPAND), F32)],
        scratch_shapes=[pltpu.VMEM((H, HG_EXPAND, HG_EXPAND), F32)], comm=comm)
    return outs if comm is None else (*outs, updated, created)


def hgrn_bwd(proj, logits, ng, states, dout, layer, *, name, comm=None):
    S = proj.shape[0]
    D = proj.shape[1] // 4
    H = D // HG_EXPAND
    C = HG_CHUNK
    nc = S // C
    depth = logits.shape[0]
    consts = _hgrn_consts()
    rev = lambda c: nc - 1 - c

    def body(p_ref, lg_ref, ng_ref, ops3_ref, opst3_ref, mk_ref, st_in_ref, do_ref,
             dp_ref, dlg_ref, dng_ref, dst_ref):
        @pl.when(pl.program_id(0) == 0)
        def _():
            dst_ref[...] = jnp.zeros_like(dst_ref)
            dlg_ref[...] = jnp.zeros_like(dlg_ref)
            dng_ref[...] = jnp.zeros_like(dng_ref)

        fn = functools.partial(_hgrn_chunk, ops3=ops3_ref[...], opst3=opst3_ref[...], mk=mk_ref[...], layer=layer)
        _, vjp = jax.vjp(fn, *(p_ref[:, j * D:(j + 1) * D] for j in range(4)), tuple(st_in_ref[h] for h in range(H)),
                         lg_ref[...], ng_ref[...])
        *dproj, dsts, dlg, dng = vjp((do_ref[...], tuple(dst_ref[h] for h in range(H))))
        for j in range(4):
            dp_ref[:, j * D:(j + 1) * D] = dproj[j].astype(BF16)
        for h in range(H):
            dst_ref[h] = dsts[h]
        dlg_ref[...] += dlg
        dng_ref[...] += dng

    in_specs = _hgrn_in_specs(D, depth, rev)
    in_specs += [pl.BlockSpec((None, H, HG_EXPAND, HG_EXPAND), lambda c: (rev(c), 0, 0, 0)),
                 pl.BlockSpec((C, D), lambda c: (rev(c), 0))]
    outs, _, created = _carried_call(
        body, [proj, logits, ng, *consts, states, dout], name=name, grid=(nc,), in_specs=in_specs,
        out_specs=[pl.BlockSpec((C, 4 * D), lambda c: (rev(c), 0)),
                   pl.BlockSpec((depth, D), lambda c: (0, 0)), pl.BlockSpec((1, HG_EXPAND), lambda c: (0, 0))],
        out_shape=[jax.ShapeDtypeStruct((S, 4 * D), BF16), jax.ShapeDtypeStruct((depth, D), F32),
                   jax.ShapeDtypeStruct((1, HG_EXPAND), F32)],
        scratch_shapes=[pltpu.VMEM((H, HG_EXPAND, HG_EXPAND), F32)], comm=comm)
    return outs if comm is None else (*outs, created)


def _shift_down(u, prev, shift):
    rolled = pltpu.roll(u, shift, axis=0)
    rows = lax.broadcasted_iota(jnp.int32, prev.shape, 0)
    top = jnp.where(rows < shift, pltpu.roll(prev, shift, axis=0), rolled[:SUBLANES])
    return jnp.concatenate([top, rolled[SUBLANES:]], axis=0)


def _shift_up(x, nxt, shift):
    tm = x.shape[0]
    rolled = pltpu.roll(x, tm - shift, axis=0)
    rows = lax.broadcasted_iota(jnp.int32, nxt.shape, 0)
    bottom = jnp.where(rows >= SUBLANES - shift, pltpu.roll(nxt, SUBLANES - shift, axis=0), rolled[tm - SUBLANES:])
    return jnp.concatenate([rolled[:tm - SUBLANES], bottom], axis=0)


def _conv(u, prev, w_ref, b_ref, half):
    u1 = _shift_down(u, prev, 1)
    u2 = _shift_down(u, prev, 2)
    return b_ref[half] + w_ref[half, 0:1, :] * u2 + w_ref[half, 1:2, :] * u1 + w_ref[half, 2:3, :] * u


def _halo_rows(ref, keep, last):
    full = ref.astype(F32)
    return (full[HALO_ROWS - SUBLANES:] if last else full[:SUBLANES]) * keep


def _conv_specs(tm, tc, l):
    nprev = lambda m: jnp.maximum(m * (tm // HALO_ROWS) - 1, 0)
    tile = pl.BlockSpec((2, tm, tc), lambda j, m: (0, m, j))
    halo = pl.BlockSpec((2, HALO_ROWS, tc), lambda j, m: (0, nprev(m), j))
    wspec = pl.BlockSpec((None, 2, 3, tc), lambda j, m: (l, 0, 0, j))
    bspec = pl.BlockSpec((None, 2, 1, tc), lambda j, m: (l, 0, 0, j))
    return tile, halo, wspec, bspec


def convact_fwd(u, cw, cb, l, *, name):
    _, S, Fh = u.shape
    tm = _pick(S, (256, 128))
    tc = _pick(Fh, (1408, 512, 256, 128))
    tile, halo, wspec, bspec = _conv_specs(tm, tc, l)

    def body(u_ref, p_ref, w_ref, b_ref, o_ref, ot_ref, c_ref):
        keep = (pl.program_id(1) > 0).astype(F32)
        cg = _conv(u_ref[0].astype(F32), _halo_rows(p_ref[0], keep, True), w_ref, b_ref, 0)
        cv = _conv(u_ref[1].astype(F32), _halo_rows(p_ref[1], keep, True), w_ref, b_ref, 1)
        act = _silu(cg) * cv
        o_ref[...] = act.astype(BF16)
        ot_ref[...] = act.T.astype(BF16)
        c_ref[0] = cg.astype(BF16)
        c_ref[1] = cv.astype(BF16)

    return pl.pallas_call(
        body, name=name, grid=(Fh // tc, S // tm), in_specs=[tile, halo, wspec, bspec],
        out_specs=[pl.BlockSpec((tm, tc), lambda j, m: (m, j)), pl.BlockSpec((tc, tm), lambda j, m: (j, m)), tile],
        out_shape=[jax.ShapeDtypeStruct((S, Fh), BF16), jax.ShapeDtypeStruct((Fh, S), BF16),
                   jax.ShapeDtypeStruct(u.shape, BF16)],
        compiler_params=_params(("parallel", "parallel")),
    )(u, u, cw, cb)


def convact_bwd(c, u, cw, l, da, *, name):
    _, S, Fh = u.shape
    tm = _pick(S, (256, 128))
    tc = _pick(Fh, (1408, 512, 256, 128))
    nm = S // tm

    def body(c_ref, u_ref, w_ref, da_ref, du_ref, dw_ref, db_ref, carry):
        @pl.when(pl.program_id(1) == 0)
        def _():
            carry[...] = jnp.zeros_like(carry)
            dw_ref[...] = jnp.zeros_like(dw_ref)
            db_ref[...] = jnp.zeros_like(db_ref)

        cg, cv = c_ref[0].astype(F32), c_ref[1].astype(F32)
        da_t = da_ref[...]
        sg = _sigmoid(cg)
        dcv = da_t * (cg * sg)
        dcg = da_t * cv * (sg * (1.0 + cg * (1.0 - sg)))
        rs = lambda t: jnp.sum(t, axis=0, keepdims=True)
        for half, dc in ((0, dcg), (1, dcv)):
            nxt = carry[half]
            d1, d2 = _shift_up(dc, nxt, 1), _shift_up(dc, nxt, 2)
            du = w_ref[half, 2:3, :] * dc + w_ref[half, 1:2, :] * d1 + w_ref[half, 0:1, :] * d2
            du_ref[half] = du.astype(BF16)
            uh = u_ref[half].astype(F32)
            for j, d in enumerate((d2, d1, dc)):
                dw_ref[half, j:j + 1, :] += rs(d * uh)
            db_ref[half] += rs(dc)
            carry[half] = dc[:SUBLANES]

    tile = pl.BlockSpec((2, tm, tc), lambda j, m: (0, nm - 1 - m, j))
    return pl.pallas_call(
        body, name=name, grid=(Fh // tc, nm),
        in_specs=[tile, tile, pl.BlockSpec((None, 2, 3, tc), lambda j, m: (l, 0, 0, j)),
                  pl.BlockSpec((tm, tc), lambda j, m: (nm - 1 - m, j))],
        out_specs=[tile, pl.BlockSpec((2, 3, tc), lambda j, m: (0, 0, j)),
                   pl.BlockSpec((2, 1, tc), lambda j, m: (0, 0, j))],
        out_shape=[jax.ShapeDtypeStruct(u.shape, BF16), jax.ShapeDtypeStruct((2, 3, Fh), F32),
                   jax.ShapeDtypeStruct((2, 1, Fh), F32)],
        scratch_shapes=[pltpu.VMEM((2, SUBLANES, tc), F32)],
        compiler_params=_params(("parallel", "arbitrary")),
    )(c, u, cw, da)


def _adamw_math(w, g, m, v):
    m = ADAM_B1 * m + (1.0 - ADAM_B1) * g
    v = ADAM_B2 * v + (1.0 - ADAM_B2) * (g * g)
    m_hat = m / (1.0 - ADAM_B1 ** ADAM_STEP)
    v_hat = v / (1.0 - ADAM_B2 ** ADAM_STEP)
    delta = -ADAM_LR * (m_hat / (jnp.sqrt(v_hat) + ADAM_EPS) + ADAM_WD * w)
    return delta, m, v


def _row_tile(R, Ccols, n_streams):
    for tr in (512, 256, 128, 64, 32, 16, 8):
        if R % tr == 0 and tr * Ccols * 4 * n_streams * 2 <= VMEM_LIMIT_BYTES // 2:
            return tr
    return R


def adamw(w, g, m, v, *, name):
    R, Ccols = w.shape
    tr = _row_tile(R, Ccols, 7)

    def body(w_ref, g_ref, m_ref, v_ref, d_ref, mo_ref, vo_ref):
        d, mn, vn = _adamw_math(w_ref[...], g_ref[...], m_ref[...], v_ref[...])
        d_ref[...] = d
        mo_ref[...] = mn
        vo_ref[...] = vn

    spec = pl.BlockSpec((tr, Ccols), lambda i: (i, 0))
    shp = jax.ShapeDtypeStruct((R, Ccols), F32)
    return pl.pallas_call(
        body, name=name, grid=(R // tr,), in_specs=[spec] * 4, out_specs=[spec] * 3,
        out_shape=[shp, shp, shp], compiler_params=_params(("parallel",)),
    )(w, g, m, v)


def _prefetch_call(body, pos, args, *, name, grid, in_specs, out_specs, out_shape, sem, aliases=None):
    return pl.pallas_call(
        lambda pos_ref, *refs: body(*refs), name=name,
        grid_spec=pltpu.PrefetchScalarGridSpec(num_scalar_prefetch=1, grid=grid, in_specs=in_specs,
                                               out_specs=out_specs),
        out_shape=out_shape, input_output_aliases={1 + i: o for i, o in (aliases or {}).items()},
        compiler_params=_params(sem),
    )(pos, *args)


def cast_place(w, pos, *, row_sharded, name):
    L, R, Ccols = w.shape

    def body(w_ref, o_ref):
        o_ref[...] = w_ref[...].astype(BF16)

    if row_sharded:
        grid = (L,)
        in_spec = pl.BlockSpec((None, R, Ccols), lambda l, pos: (l, 0, 0))
        out_spec = pl.BlockSpec((None, R, Ccols), lambda l, pos: (l, pos[0], 0))
        out_shape = jax.ShapeDtypeStruct((L, N_CHIPS * R, Ccols), BF16)
        sem = ("parallel",)
    else:
        tr = _pick(R, (512, 256, 128))
        grid = (L, R // tr)
        in_spec = pl.BlockSpec((None, tr, Ccols), lambda l, i, pos: (l, i, 0))
        out_spec = pl.BlockSpec((None, tr, Ccols), lambda l, i, pos: (l, i, pos[0]))
        out_shape = jax.ShapeDtypeStruct((L, R, N_CHIPS * Ccols), BF16)
        sem = ("parallel", "parallel")
    return _prefetch_call(body, pos, [w], name=name, grid=grid, in_specs=[in_spec], out_specs=out_spec,
                          out_shape=out_shape, sem=sem)


def pair_add(g, theirs, pos, *, row_sharded, name):
    def body(g_ref, t_ref, o_ref, ob_ref):
        s = g_ref[...] + t_ref[...]
        o_ref[...] = s
        ob_ref[...] = s.astype(BF16)

    if row_sharded:
        L, _, _, r2, D = g.shape
        grid = (L, N_CHIPS)
        g_spec = pl.BlockSpec((None, None, None, r2, D), lambda l, q, pos: (l, q, pos[1], 0, 0))
        t_spec = pl.BlockSpec((None, None, r2, D), lambda l, q, pos: (l, q, 0, 0))
    else:
        L, K, N = g.shape
        n = N // N_CHIPS
        tr = _pick(K // 2, (256, 128))
        nb = (K // 2) // tr
        grid = (L, nb, N_CHIPS)
        g_spec = pl.BlockSpec((None, tr, n), lambda l, i, j, pos: (l, pos[1] * nb + i, j))
        t_spec = pl.BlockSpec((None, tr, n), lambda l, i, j, pos: (l, i, j))
    return _prefetch_call(body, pos, [g, theirs], name=name, grid=grid, in_specs=[g_spec, t_spec],
                          out_specs=[t_spec, t_spec],
                          out_shape=[jax.ShapeDtypeStruct(theirs.shape, F32), jax.ShapeDtypeStruct(theirs.shape, BF16)],
                          sem=("parallel",) * len(grid))


def chip_add(pair, landed, pos, buf, li, n_layers, *, row_sharded, name):
    def body(p_ref, a_ref, b_ref, c_ref, *rest):
        rest[-1][...] = ((p_ref[...] + a_ref[...].astype(F32)) + b_ref[...].astype(F32)) + c_ref[...].astype(F32)

    if row_sharded:
        _, _, r2, D = pair.shape
        grid = (1,)
        p_spec = pl.BlockSpec((None, None, r2, D), lambda i, pos: (0, pos[0], 0, 0))
        l_specs = [pl.BlockSpec((None, r2, D), functools.partial(lambda i, pos, k: (k, 0, 0), k=k)) for k in range(3)]
        out_spec = pl.BlockSpec((None, None, r2, D), lambda i, pos: (li, pos[1], 0, 0))
        out_shape = jax.ShapeDtypeStruct((n_layers, 2, r2, D), F32)
    else:
        _, K2, N = pair.shape
        n = N // N_CHIPS
        tr = _pick(K2, (256, 128))
        nb = K2 // tr
        grid = (nb,)
        p_spec = pl.BlockSpec((None, tr, n), lambda i, pos: (0, i, pos[0]))
        l_specs = [pl.BlockSpec((None, tr, n), functools.partial(lambda i, pos, k: (k, i, 0), k=k)) for k in range(3)]
        out_spec = pl.BlockSpec((None, tr, n), lambda i, pos: (li, pos[1] * nb + i, 0))
        out_shape = jax.ShapeDtypeStruct((n_layers, 2 * K2, n), F32)
    args, in_specs, aliases = [pair, landed, landed, landed], [p_spec] + l_specs, None
    if buf is not None:
        args.append(buf)
        in_specs.append(pl.BlockSpec(memory_space=pl.ANY))
        aliases = {4: 0}
    return _prefetch_call(body, pos, args, name=name, grid=grid, in_specs=in_specs, out_specs=out_spec,
                          out_shape=out_shape, sem=("parallel",), aliases=aliases)


def _position():
    return lax.axis_index("x"), lax.axis_index("y"), lax.axis_index("c")


def _other_chips(x, y):
    return [(1 - x, y), (x, 1 - y), (1 - x, 1 - y)]


HBM_ANY = pl.BlockSpec(memory_space=pl.ANY)
COMM_PARAMS = pltpu.CompilerParams(has_side_effects=True)


class Comm:
    def __init__(self, ro=(), inout=(), new=(), sems=(), phases=()):
        self.ro, self.inout, self.new, self.sems, self.phases = list(ro), list(inout), list(new), list(sems), list(phases)


def _carried_call(body, args, *, name, grid, in_specs, out_specs, out_shape, scratch_shapes=(), comm=None):
    comm = comm or Comm()
    n_in, n_out, n_sc = len(in_specs), len(out_shape), len(scratch_shapes)
    n_ro, n_io, n_new, n_sem = len(comm.ro), len(comm.inout), len(comm.new), len(comm.sems)
    last = int(np.prod(grid)) - 1

    def wrapped(*refs):
        bounds = np.cumsum([0, n_in, n_ro, n_io, n_out, n_io, n_new, n_sc, n_sem])
        core_in, ro, _, core_out, io, new, core_sc, sems = (refs[a:b] for a, b in zip(bounds[:-1], bounds[1:]))
        step = pl.program_id(0)
        for d in range(1, len(grid)):
            step = step * grid[d] + pl.program_id(d)
        for frac, fn in comm.phases[:-1]:
            pl.when(step == int(frac * last))(functools.partial(fn, ro, io, new, sems))
        body(*core_in, *core_out, *core_sc)
        if comm.phases:
            pl.when(step == last)(functools.partial(comm.phases[-1][1], ro, io, new, sems))

    outs = pl.pallas_call(
        wrapped, name=name, grid=grid,
        in_specs=list(in_specs) + [HBM_ANY] * (n_ro + n_io),
        out_specs=list(out_specs) + [HBM_ANY] * (n_io + n_new),
        out_shape=list(out_shape) + [jax.ShapeDtypeStruct(a.shape, a.dtype) for a in comm.inout] + comm.new,
        input_output_aliases={n_in + n_ro + j: n_out + j for j in range(n_io)},
        scratch_shapes=list(scratch_shapes) + comm.sems,
        compiler_params=pltpu.CompilerParams(dimension_semantics=("arbitrary",) * len(grid),
                                             vmem_limit_bytes=VMEM_LIMIT_BYTES, has_side_effects=bool(comm.phases)),
    )(*args, *comm.ro, *comm.inout)
    return outs[:n_out], outs[n_out:n_out + n_io], outs[n_out + n_io:]


def run_comm(comm, *, name):
    n_ro, n_io, n_new = len(comm.ro), len(comm.inout), len(comm.new)

    def body(*refs):
        bounds = np.cumsum([0, n_ro, n_io, n_io, n_new, len(comm.sems)])
        ro, _, io, new, sems = (refs[a:b] for a, b in zip(bounds[:-1], bounds[1:]))
        for _, fn in comm.phases:
            fn(ro, io, new, sems)

    outs = pl.pallas_call(
        body, name=name, in_specs=[HBM_ANY] * (n_ro + n_io), out_specs=[HBM_ANY] * (n_io + n_new),
        out_shape=[jax.ShapeDtypeStruct(a.shape, a.dtype) for a in comm.inout] + comm.new,
        input_output_aliases={n_ro + j: j for j in range(n_io)},
        scratch_shapes=comm.sems, compiler_params=COMM_PARAMS,
    )(*comm.ro, *comm.inout)
    return outs[:n_io], outs[n_io:]


def _rdma(src, dst, send_sem, recv_sem, device):
    return pltpu.make_async_remote_copy(src_ref=src, dst_ref=dst, send_sem=send_sem, recv_sem=recv_sem,
                                        device_id=device, device_id_type=MESH)


def gather_comm(bufs, slices):
    n = len(slices)

    def region(io, s, h, chip):
        b, li, row_sharded = slices[s]
        ref = io[b]
        if row_sharded:
            r = ref.shape[1] // N_CHIPS
            return ref.at[li, pl.ds(chip * r + h * (r // 2), r // 2), :]
        K, nn = ref.shape[1], ref.shape[2] // N_CHIPS
        return ref.at[li, pl.ds(h * (K // 2), K // 2), pl.ds(chip * nn, nn)]

    def copies(io, sems):
        x, y, c = _position()
        me = 2 * x + y
        chips = _other_chips(x, y)
        ids = [2 * cx + cy for cx, cy in chips]

        def ici(s, k, src):
            return _rdma(region(io, s, c, me), region(io, s, c, src), sems[0].at[s, k], sems[1].at[s, k], (*chips[k], c))

        def d2d(s, k, h):
            return _rdma(region(io, s, h, ids[k]), region(io, s, h, ids[k]), sems[2].at[s, k], sems[3].at[s, k],
                         (x, y, 1 - c))

        return me, ids, c, ici, d2d

    pairs = [(s, k) for s in range(n) for k in range(3)]

    def start(ro, io, new, sems):
        me, _, _, ici, _ = copies(io, sems)
        for s, k in pairs:
            ici(s, k, me).start()

    def pass_on(ro, io, new, sems):
        _, ids, c, ici, d2d = copies(io, sems)
        for s, k in pairs:
            ici(s, k, ids[k]).wait_recv()
            d2d(s, k, c).start()

    def finish(ro, io, new, sems):
        me, _, c, ici, d2d = copies(io, sems)
        for s, k in pairs:
            d2d(s, k, 1 - c).wait_recv()
        for s, k in pairs:
            ici(s, k, me).wait_send()
            d2d(s, k, c).wait_send()

    return Comm(inout=bufs, sems=[pltpu.SemaphoreType.DMA((n, 3))] * 4,
                phases=[(0.0, start), (0.6, pass_on), (1.0, finish)])


def gather_small(shard, *, name):
    n = shard.shape[2]

    def body(s_ref, o_ref, send_sem, recv_sem, local_sem):
        x, y, c = _position()
        me = 2 * x + y
        chips = _other_chips(x, y)
        place = lambda chip: o_ref.at[:, :, pl.ds(chip * n, n)]
        local = pltpu.make_async_copy(s_ref, place(me), local_sem)
        local.start()
        for k in range(3):
            _rdma(s_ref, place(me), send_sem.at[k], recv_sem.at[k], (*chips[k], c)).start()
        for k in range(3):
            _rdma(s_ref, place(2 * chips[k][0] + chips[k][1]), send_sem.at[k], recv_sem.at[k], (*chips[k], c)).wait()
        local.wait()

    return pl.pallas_call(
        body, name=name, in_specs=[HBM_ANY], out_specs=HBM_ANY,
        out_shape=jax.ShapeDtypeStruct((shard.shape[0], shard.shape[1], N_CHIPS * n), shard.dtype),
        scratch_shapes=[pltpu.SemaphoreType.DMA((3,)), pltpu.SemaphoreType.DMA((3,)), pltpu.SemaphoreType.DMA],
        compiler_params=COMM_PARAMS,
    )(shard)


def swap_comm(grads):
    def copies(ro, new, sems):
        x, y, c = _position()
        h = 1 - c
        cps = []
        for s, (a, row_sharded) in enumerate(grads):
            if row_sharded:
                r = a.shape[1] // N_CHIPS
                for q in range(N_CHIPS):
                    cps.append(_rdma(ro[s].at[:, pl.ds(q * r + h * (r // 2), r // 2), :], new[s].at[:, q],
                                     sems[0].at[s, q], sems[1].at[s, q], (x, y, 1 - c)))
            else:
                K = a.shape[1]
                cps.append(_rdma(ro[s].at[:, pl.ds(h * (K // 2), K // 2), :], new[s],
                                 sems[0].at[s, 0], sems[1].at[s, 0], (x, y, 1 - c)))
        return cps

    def start(ro, io, new, sems):
        for cp in copies(ro, new, sems):
            cp.start()

    def finish(ro, io, new, sems):
        for cp in copies(ro, new, sems):
            cp.wait()

    theirs = []
    for a, row_sharded in grads:
        if row_sharded:
            theirs.append(jax.ShapeDtypeStruct((1, N_CHIPS, a.shape[1] // N_CHIPS // 2, a.shape[2]), a.dtype))
        else:
            theirs.append(jax.ShapeDtypeStruct((1, a.shape[1] // 2, a.shape[2]), a.dtype))
    return Comm(ro=[a for a, _ in grads], new=theirs, sems=[pltpu.SemaphoreType.DMA((len(grads), N_CHIPS))] * 2,
                phases=[(0.0, start), (1.0, finish)])


def merge_comms(comms):
    spans, ro, io, new, sems = [], [], [], [], []
    for cm in comms:
        spans.append((len(ro), len(io), len(new), len(sems)))
        ro, io, new, sems = ro + cm.ro, io + cm.inout, new + cm.new, sems + cm.sems

    def phase(frac):
        def run(R, I, N, S):
            for cm, (a, b, d, e) in zip(comms, spans):
                for f, fn in cm.phases:
                    if f == frac:
                        fn(R[a:a + len(cm.ro)], I[b:b + len(cm.inout)], N[d:d + len(cm.new)], S[e:e + len(cm.sems)])
        return run

    fracs = sorted({f for cm in comms for f, _ in cm.phases})
    merged = Comm(ro=ro, inout=io, new=new, sems=sems, phases=[(f, phase(f)) for f in fracs])
    cut = lambda created: [created[d:d + len(cm.new)] for cm, (_, _, d, _) in zip(comms, spans)]
    return merged, cut


def scatter_comm(halves):
    def copies(ro, new, sems):
        x, y, c = _position()
        chips = _other_chips(x, y)
        cps = []
        for s, (a, row_sharded) in enumerate(halves):
            for k in range(3):
                to = 2 * chips[k][0] + chips[k][1]
                if row_sharded:
                    src = ro[s].at[0, to]
                else:
                    n = a.shape[2] // N_CHIPS
                    src = ro[s].at[0, :, pl.ds(to * n, n)]
                cps.append(_rdma(src, new[s].at[k], sems[0].at[s, k], sems[1].at[s, k], (*chips[k], c)))
        return cps

    def start(ro, io, new, sems):
        for cp in copies(ro, new, sems):
            cp.start()

    def finish(ro, io, new, sems):
        for cp in copies(ro, new, sems):
            cp.wait()

    landing = [jax.ShapeDtypeStruct((3, a.shape[2], a.shape[3]) if row_sharded else
                                    (3, a.shape[1], a.shape[2] // N_CHIPS), a.dtype) for a, row_sharded in halves]
    return Comm(ro=[a for a, _ in halves], new=landing, sems=[pltpu.SemaphoreType.DMA((len(halves), 3))] * 2,
                phases=[(0.0, start), (1.0, finish)])


def sibling_join_halves(shards, *, name):
    n_in = len(shards)

    def body(*refs):
        ins = refs[:n_in]
        outs = refs[n_in:2 * n_in]
        send_sem, recv_sem = refs[2 * n_in:]
        x, y, c = _position()
        sibling = (x, y, 1 - c)

        def rows(ref, h):
            R2 = ref.shape[1] // 2
            return ref.at[:, pl.ds(h * R2, R2), :]

        sends = [pltpu.make_async_remote_copy(src_ref=rows(ins[i], c), dst_ref=rows(outs[i], c),
                                              send_sem=send_sem.at[i], recv_sem=recv_sem.at[i],
                                              device_id=sibling, device_id_type=MESH) for i in range(n_in)]
        lands = [pltpu.make_async_remote_copy(src_ref=rows(ins[i], c), dst_ref=rows(outs[i], 1 - c),
                                              send_sem=send_sem.at[i], recv_sem=recv_sem.at[i],
                                              device_id=sibling, device_id_type=MESH) for i in range(n_in)]
        for cp in sends:
            cp.start()
        for i in range(n_in):
            sends[i].wait_send()
            lands[i].wait_recv()

    return pl.pallas_call(
        body, name=name, in_specs=[HBM_ANY] * n_in, out_specs=[HBM_ANY] * n_in,
        out_shape=[jax.ShapeDtypeStruct(a.shape, a.dtype) for a in shards],
        input_output_aliases={i: i for i in range(n_in)},
        scratch_shapes=[pltpu.SemaphoreType.DMA((n_in,)), pltpu.SemaphoreType.DMA((n_in,))],
        compiler_params=COMM_PARAMS,
    )(*shards)


def all_reduce_small(v, *, name):
    R, Ccols = v.shape

    def body(v_ref, o_ref, slots, send_sem, recv_sem):
        x, y, c = _position()
        me = 4 * x + 2 * y + c
        slots[me] = v_ref[...]

        def peer(d):
            return x ^ (d >> 2), y ^ ((d >> 1) & 1), c ^ (d & 1)

        def copy(d, slot):
            return pltpu.make_async_remote_copy(
                src_ref=v_ref, dst_ref=slots.at[slot], send_sem=send_sem.at[d], recv_sem=recv_sem.at[d],
                device_id=peer(d), device_id_type=MESH)

        sends = [copy(d, me) for d in range(1, N_DEV)]
        for cp in sends:
            cp.start()
        for d in range(1, N_DEV):
            px, py, pc = peer(d)
            copy(d, 4 * px + 2 * py + pc).wait_recv()
        for cp in sends:
            cp.wait_send()
        acc = slots[0]
        for d in range(1, N_DEV):
            acc = acc + slots[d]
        o_ref[...] = acc

    return pl.pallas_call(
        body, name=name,
        in_specs=[pl.BlockSpec(memory_space=pltpu.VMEM)], out_specs=pl.BlockSpec(memory_space=pltpu.VMEM),
        out_shape=jax.ShapeDtypeStruct((R, Ccols), F32),
        scratch_shapes=[pltpu.VMEM((N_DEV, R, Ccols), F32), pltpu.SemaphoreType.DMA((N_DEV,)),
                        pltpu.SemaphoreType.DMA((N_DEV,))],
        compiler_params=COMM_PARAMS,
    )(v)


def _pack_small(parts):
    flat = jnp.concatenate([p.reshape(-1).astype(F32) for p in parts])
    n = flat.shape[0]
    rows = -(-n // LANES)
    rows = -(-rows // SUBLANES) * SUBLANES
    return jnp.pad(flat, (0, rows * LANES - n)).reshape(rows, LANES)


def _unpack_small(packed, like):
    flat = packed.reshape(-1)
    out, off = [], 0
    for p in like:
        out.append(flat[off:off + p.size].reshape(p.shape))
        off += p.size
    return out


def kernel(x, norm_mix, norm_ffn, norm_final, attn_w_in, attn_w_out, attn_sinks, hgrn_w_in, hgrn_w_out, hgrn_norm, hgrn_lb_logits, ffn_w_up, ffn_conv_w, ffn_conv_b, ffn_w_down, loss_target, m_norm_mix, m_norm_ffn, m_norm_final, m_attn_w_in, m_attn_w_out, m_attn_sinks, m_hgrn_w_in, m_hgrn_w_out, m_hgrn_norm, m_hgrn_lb_logits, m_ffn_w_up, m_ffn_conv_w, m_ffn_conv_b, m_ffn_w_down, v_norm_mix, v_norm_ffn, v_norm_final, v_attn_w_in, v_attn_w_out, v_attn_sinks, v_hgrn_w_in, v_hgrn_w_out, v_hgrn_norm, v_hgrn_lb_logits, v_ffn_w_up, v_ffn_conv_w, v_ffn_conv_b, v_ffn_w_down):
    S, D = x.shape[1], x.shape[2]
    depth = norm_mix.shape[0]
    n_q = D // HEAD_DIM
    n_kv = n_q // Q_PER_KV
    Fh = ffn_w_down.shape[1] * N_CHIPS
    h = x.reshape(S, D)
    target = loss_target.reshape(S, D)

    col_names = ["attn_w_in", "hgrn_w_in", "ffn_w_up"]
    row_names = ["attn_w_out", "hgrn_w_out", "ffn_w_down"]
    x_, y_, c_ = _position()
    me = 2 * x_ + y_
    pos = jnp.stack([me, c_]).astype(jnp.int32)
    col_w = [cast_place(w, pos, row_sharded=False, name=f"cast_place_{n}")
             for n, w in zip(col_names, (attn_w_in, hgrn_w_in, ffn_w_up))]
    row_w = [cast_place(w, pos, row_sharded=True, name=f"cast_place_{n}")
             for n, w in zip(row_names, (attn_w_out, hgrn_w_out, ffn_w_down))]
    W = dict(zip(col_names + row_names, col_w + row_w))
    is_row = {n: n in row_names for n in W}

    def layer_weights(layer):
        mixer = "attn" if layer % 2 == 0 else "hgrn"
        return [(f"{mixer}_w_in", layer // 2), (f"{mixer}_w_out", layer // 2), ("ffn_w_up", layer), ("ffn_w_down", layer)]

    def gather_of(group):
        names = [n for n, _ in group]
        return names, gather_comm([W[n] for n in names], [(i, li, is_row[n]) for i, (n, li) in enumerate(group)])

    def mixer_weights(layer):
        return layer_weights(layer)[:2] if layer < depth else []

    names_next, comm_next = gather_of(mixer_weights(0))
    filled, _ = run_comm(comm_next, name="gather_mixer0")
    W.update(zip(names_next, filled))
    cw = gather_small(ffn_conv_w, name="gather_conv_w").reshape(depth, 3, 2, Fh).transpose(0, 2, 1, 3)
    cb = ffn_conv_b.reshape(depth, 2, 1, Fh)
    g_mix = norm_mix.reshape(depth, 1, D)
    g_ffn = norm_ffn.reshape(depth, 1, D)

    slopes = jnp.exp2(-8.0 * jnp.arange(1, n_q + 1, dtype=F32) / n_q).reshape(n_kv, Q_PER_KV)
    slope_col = jnp.repeat(slopes, WINDOW, axis=1).reshape(n_kv, Q_PER_KV * WINDOW, 1)

    def to_heads(t, nh):
        return t.reshape(S, nh, HEAD_DIM).transpose(1, 0, 2)

    def from_heads(t):
        return t.transpose(1, 0, 2).reshape(S, -1)

    saved = []
    for layer in range(depth):
        idx = layer // 2
        rec = {"h_mix": h}
        hn, rec["hn_t"] = rmsnorm_fwd(h, g_mix, layer, name=f"norm_mix_fwd{layer}")
        is_attn = layer % 2 == 0
        names_next, comm_next = gather_of(layer_weights(layer)[2:] + ([] if is_attn else mixer_weights(layer + 1)))
        if is_attn:
            proj = mm_nn(hn, W["attn_w_in"], idx, name=f"attn_in_fwd{layer}", out_dtype=BF16)
            q = to_heads(proj[:, :n_q * HEAD_DIM], n_q).reshape(n_kv, Q_PER_KV, S, HEAD_DIM)
            k = to_heads(proj[:, n_q * HEAD_DIM:(n_q + n_kv) * HEAD_DIM], n_kv)
            v = to_heads(proj[:, (n_q + n_kv) * HEAD_DIM:], n_kv)
            sink_col = jnp.repeat(attn_sinks[idx].reshape(n_kv, Q_PER_KV), WINDOW, axis=1)
            sink_col = sink_col.reshape(n_kv, Q_PER_KV * WINDOW, 1)
            o_heads, filled, _ = attn_fwd(q, k, v, sink_col, slope_col, name=f"attn_fwd{layer}", comm=comm_next)
            W.update(zip(names_next, filled))
            o = from_heads(o_heads.reshape(n_q, S, HEAD_DIM))
            rec.update(q=q, k=k, v=v, sink_col=sink_col, o=o)
            h = mm_nn(o, W["attn_w_out"], idx, name=f"attn_out_fwd{layer}", res=h)
        else:
            proj = mm_nn(hn, W["hgrn_w_in"], idx, name=f"hgrn_in_fwd{layer}")
            ng = hgrn_norm[idx].reshape(1, HG_EXPAND)
            o, states, filled, _ = hgrn_fwd(proj, hgrn_lb_logits, ng, layer, name=f"hgrn_fwd{layer}", comm=comm_next)
            W.update(zip(names_next, filled))
            rec.update(proj=proj, ng=ng, states=states, o=o)
            h = mm_nn(o, W["hgrn_w_out"], idx, name=f"hgrn_out_fwd{layer}", res=h)
        rec["h_ffn"] = h
        hn2, rec["hn2_t"] = rmsnorm_fwd(h, g_ffn, layer, name=f"norm_ffn_fwd{layer}")
        if is_attn and layer + 1 < depth:
            names_next, comm_next = gather_of(mixer_weights(layer + 1))
            u, filled = mm_nn(hn2, W["ffn_w_up"], layer, name=f"ffn_up_fwd{layer}", split_out=True, out_dtype=BF16,
                              comm=comm_next)
            W.update(zip(names_next, filled))
        else:
            u = mm_nn(hn2, W["ffn_w_up"], layer, name=f"ffn_up_fwd{layer}", split_out=True, out_dtype=BF16)
        act, rec["act_t"], rec["c"] = convact_fwd(u, cw, cb, layer, name=f"ffn_act_fwd{layer}")
        rec["u"] = u
        h = mm_nn(act, W["ffn_w_down"], layer, name=f"ffn_down_fwd{layer}", res=h)
        saved.append(rec)

    loss_part, dh, d_norm_final = loss_head(h, norm_final.reshape(1, D), target, name="loss_head")

    n_attn, n_hgrn = attn_w_in.shape[0], hgrn_w_in.shape[0]
    reduced_parts = []
    pending = []

    def swap_of(group, g):
        ordered = sorted(group, key=lambda nl: is_row[nl[0]])
        return ordered, swap_comm([(g[n], is_row[n]) for n, _ in ordered])

    def pair_up(tag, ordered, g, theirs):
        pair, pair_bf16 = {}, {}
        for (n, _), t in zip(ordered, theirs):
            gn = g[n]
            if is_row[n]:
                r = gn.shape[1] // N_CHIPS
                gn = gn.reshape(1, N_CHIPS, 2, r // 2, gn.shape[2])
            pair[n], pair_bf16[n] = pair_add(gn, t, pos, row_sharded=is_row[n], name=f"rs_pair_add_{n}{tag}")
        pending.append((ordered, pair, scatter_comm([(pair_bf16[n], is_row[n]) for n, _ in ordered])))

    def landed_behind(created, cut):
        for (ordered, pair, _), landed in zip(pending, cut(created)):
            reduced_parts.append((ordered, pair, landed))
        pending.clear()

    d_norm_mix, d_norm_ffn = [None] * depth, [None] * depth
    d_conv_w, d_conv_b = [None] * depth, [None] * depth
    d_sinks, d_hgrn_norm = [None] * n_attn, [None] * n_hgrn
    d_logits = jnp.zeros_like(hgrn_lb_logits)
    for layer in reversed(range(depth)):
        idx = layer // 2
        rec = saved[layer]
        g = {"ffn_w_down": mm_dw(rec["act_t"], dh, name=f"ffn_down_dw{layer}")}
        dact = mm_nt(dh, W["ffn_w_down"], layer, name=f"ffn_down_dx{layer}")
        du, dcw, dcb = convact_bwd(rec["c"], rec["u"], cw, layer, dact, name=f"ffn_act_bwd{layer}")
        d_conv_w[layer] = dcw.transpose(1, 0, 2).reshape(3, 2 * Fh)
        d_conv_b[layer] = dcb.reshape(2 * Fh)
        g["ffn_w_up"] = mm_dw(rec["hn2_t"], du, name=f"ffn_up_dw{layer}", split_b=True)
        ordered, swap = swap_of(layer_weights(layer)[2:], g)
        dh, dg, theirs = mm_nt(du, W["ffn_w_up"], layer, name=f"ffn_up_dx{layer}", split_in=True, comm=swap,
                               norm=(rec["h_ffn"], g_ffn, layer, dh))
        pair_up(layer, ordered, g, theirs)
        d_norm_ffn[layer] = dg.reshape(D)
        carried, cut = merge_comms([p[2] for p in pending])
        if layer % 2 == 0:
            g["attn_w_out"] = mm_tn(rec["o"], dh, name=f"attn_out_dw{layer}")
            do = mm_nt(dh, W["attn_w_out"], idx, name=f"attn_out_dx{layer}", out_dtype=BF16)
            do_heads = to_heads(do, n_q).reshape(n_kv, Q_PER_KV, S, HEAD_DIM)
            res = attn_bwd(rec["q"], rec["k"], rec["v"], rec["sink_col"], slope_col, do_heads,
                           name=f"attn_bwd{layer}", comm=carried)
            dq, dk, dv, dsink = res[:4]
            landed_behind(res[-1], cut)
            dproj = jnp.concatenate([from_heads(dq.reshape(n_q, S, HEAD_DIM)), from_heads(dk), from_heads(dv)], axis=1)
            d_sinks[idx] = jnp.sum(dsink.reshape(n_kv, Q_PER_KV, WINDOW), axis=-1).reshape(n_q)
            g["attn_w_in"] = mm_dw(rec["hn_t"], dproj, name=f"attn_in_dw{layer}")
            ordered, swap = swap_of(layer_weights(layer)[:2], g)
            dh, dg, theirs = mm_nt(dproj, W["attn_w_in"], idx, name=f"attn_in_dx{layer}", comm=swap,
                                   norm=(rec["h_mix"], g_mix, layer, dh))
        else:
            g["hgrn_w_out"] = mm_tn(rec["o"], dh, name=f"hgrn_out_dw{layer}")
            do = mm_nt(dh, W["hgrn_w_out"], idx, name=f"hgrn_out_dx{layer}")
            res = hgrn_bwd(rec["proj"], hgrn_lb_logits, rec["ng"], rec["states"], do, layer,
                           name=f"hgrn_bwd{layer}", comm=carried)
            dproj, dlg, dng = res[:3]
            landed_behind(res[-1], cut)
            d_logits = d_logits + dlg
            d_hgrn_norm[idx] = dng.reshape(HG_EXPAND)
            g["hgrn_w_in"] = mm_dw(rec["hn_t"], dproj, name=f"hgrn_in_dw{layer}")
            ordered, swap = swap_of(layer_weights(layer)[:2], g)
            dh, dg, theirs = mm_nt(dproj, W["hgrn_w_in"], idx, name=f"hgrn_in_dx{layer}", comm=swap,
                                   norm=(rec["h_mix"], g_mix, layer, dh))
        pair_up(layer, ordered, g, theirs)
        d_norm_mix[layer] = dg.reshape(D)
    grad_x = dh.reshape(x.shape)
    carried, cut = merge_comms([p[2] for p in pending])
    landed_behind(run_comm(carried, name="rs_scatter_last")[1], cut)

    small_w = [norm_mix, norm_ffn, norm_final, attn_sinks, hgrn_norm, hgrn_lb_logits, ffn_conv_b]
    small_m = [m_norm_mix, m_norm_ffn, m_norm_final, m_attn_sinks, m_hgrn_norm, m_hgrn_lb_logits, m_ffn_conv_b]
    small_v = [v_norm_mix, v_norm_ffn, v_norm_final, v_attn_sinks, v_hgrn_norm, v_hgrn_lb_logits, v_ffn_conv_b]
    small_grads = [jnp.stack(d_norm_mix), jnp.stack(d_norm_ffn), d_norm_final.reshape(D), jnp.stack(d_sinks),
                   jnp.stack(d_hgrn_norm), d_logits, jnp.stack(d_conv_b)]
    reduced = all_reduce_small(_pack_small(small_grads + [loss_part]), name="all_reduce_small")
    small_g = _unpack_small(reduced, small_w + [loss_part])
    loss = small_g.pop().reshape(())
    pad = [jnp.zeros((1, 1), F32)]
    sd, sm_, sv_ = adamw(_pack_small(small_w + pad), reduced, _pack_small(small_m + pad),
                         _pack_small(small_v + pad), name="adamw_small")
    small_d = _unpack_small(sd, small_w)
    small_nm = _unpack_small(sm_, small_w)
    small_nv = _unpack_small(sv_, small_w)

    big_names = col_names + row_names
    big_w = dict(attn_w_in=attn_w_in, hgrn_w_in=hgrn_w_in, ffn_w_up=ffn_w_up, attn_w_out=attn_w_out,
                 hgrn_w_out=hgrn_w_out, ffn_w_down=ffn_w_down)
    half_shard = dict.fromkeys(big_names)
    for ordered, pair, landed in reduced_parts:
        for (n, li), ld in zip(ordered, landed):
            half_shard[n] = chip_add(pair[n], ld, pos, half_shard[n], li, big_w[n].shape[0], row_sharded=is_row[n],
                                     name=f"rs_chip_add_{n}{li}")
    shards = sibling_join_halves([half_shard[n].reshape(big_w[n].shape) for n in big_names], name="rs_sibling_join")
    big_m = dict(attn_w_in=m_attn_w_in, hgrn_w_in=m_hgrn_w_in, ffn_w_up=m_ffn_w_up, attn_w_out=m_attn_w_out,
                 hgrn_w_out=m_hgrn_w_out, ffn_w_down=m_ffn_w_down)
    big_v = dict(attn_w_in=v_attn_w_in, hgrn_w_in=v_hgrn_w_in, ffn_w_up=v_ffn_w_up, attn_w_out=v_attn_w_out,
                 hgrn_w_out=v_hgrn_w_out, ffn_w_down=v_ffn_w_down)
    grads, deltas, new_m, new_v = {}, {}, {}, {}
    for name_, gshard in zip(big_names, shards):
        w_ = big_w[name_]
        cols = w_.shape[-1]
        d_, m_, v_ = adamw(w_.reshape(-1, cols), gshard.reshape(-1, cols), big_m[name_].reshape(-1, cols),
                           big_v[name_].reshape(-1, cols), name=f"adamw_{name_}")
        grads[name_] = gshard.reshape(w_.shape)
        deltas[name_], new_m[name_], new_v[name_] = (t.reshape(w_.shape) for t in (d_, m_, v_))

    gconv = jnp.stack(d_conv_w)
    n_conv = ffn_conv_w.shape[2]
    gconv_all = _unpack_small(all_reduce_small(_pack_small([gconv]), name="all_reduce_conv_w"), [gconv])[0]
    gconv_mine = lax.dynamic_slice_in_dim(gconv_all, me * n_conv, n_conv, axis=2)
    cshape = ffn_conv_w.shape
    d_, m_, v_ = adamw(ffn_conv_w.reshape(-1, n_conv), gconv_mine.reshape(-1, n_conv),
                       m_ffn_conv_w.reshape(-1, n_conv), v_ffn_conv_w.reshape(-1, n_conv), name="adamw_ffn_conv_w")
    grads["ffn_conv_w"] = gconv_mine
    deltas["ffn_conv_w"], new_m["ffn_conv_w"], new_v["ffn_conv_w"] = (t.reshape(cshape) for t in (d_, m_, v_))

    order = ["norm_mix", "norm_ffn", "norm_final", "attn_w_in", "attn_w_out", "attn_sinks", "hgrn_w_in",
             "hgrn_w_out", "hgrn_norm", "hgrn_lb_logits", "ffn_w_up", "ffn_conv_w", "ffn_conv_b", "ffn_w_down"]
    small_names = ["norm_mix", "norm_ffn", "norm_final", "attn_sinks", "hgrn_norm", "hgrn_lb_logits", "ffn_conv_b"]
    for i, name_ in enumerate(small_names):
        grads[name_], deltas[name_], new_m[name_], new_v[name_] = small_g[i], small_d[i], small_nm[i], small_nv[i]
    return (loss, grad_x, *[grads[n] for n in order], *[deltas[n] for n in order],
            *[new_m[n] for n in order], *[new_v[n] for n in order])
```

```python
import functools
import math

import numpy as np
import jax
import jax.numpy as jnp
from jax import lax
from jax.experimental import pallas as pl
from jax.experimental.pallas import tpu as pltpu

F32 = jnp.float32
BF16 = jnp.bfloat16
MESH = pl.DeviceIdType.MESH

HEAD_DIM = 64
Q_PER_KV = 4
WINDOW = 128
HG_EXPAND = 128
HG_CHUNK = 128
EPS = 1e-6
N_CHIPS = 4
N_DEV = 8

ADAM_LR = 0.001
ADAM_B1 = 0.9
ADAM_B2 = 0.999
ADAM_EPS = 1e-08
ADAM_WD = 0.01
ADAM_STEP = 10

VMEM_LIMIT_BYTES = 48 * 1024 * 1024
LANES = 128
SUBLANES = 8
HALO_ROWS = 16


def _params(sem=None):
    return pltpu.CompilerParams(dimension_semantics=sem, vmem_limit_bytes=VMEM_LIMIT_BYTES)


def _pick(n, cands):
    for c in cands:
        if n % c == 0:
            return c
    return n


def _dot(a, b, dims):
    return lax.dot_general(a.astype(BF16), b.astype(BF16), (dims, ((), ())),
                           preferred_element_type=F32)


NN = ((1,), (0,))
NT = ((1,), (1,))
TN = ((0,), (0,))


def mm_nn(a, w, l, *, name, res=None, out_dtype=F32, split_out=False, comm=None):
    M, K = a.shape
    N = w.shape[2]
    tm = _pick(M, (1024, 512, 256, 128))
    tn = _pick(N // 2 if split_out else N, (1408, 1024, 768, 512, 256, 128))
    tk = _pick(K, (1024, 1408, 512, 256, 128))
    nk = K // tk
    nh = (N // 2) // tn

    def body(*refs):
        if res is None:
            a_ref, w_ref, o_ref = refs[:3]
            r_ref = None
        else:
            a_ref, w_ref, r_ref, o_ref = refs[:4]
        part = _dot(a_ref[...], w_ref[...], NN)

        def finish(acc):
            if r_ref is not None:
                acc = acc + r_ref[...]
            o_ref[...] = acc.astype(out_dtype)

        if nk == 1:
            finish(part)
        else:
            acc_ref = refs[-1]
            k = pl.program_id(2)

            @pl.when(k == 0)
            def _():
                acc_ref[...] = part

            @pl.when(k > 0)
            def _():
                acc_ref[...] += part

            @pl.when(k == nk - 1)
            def _():
                finish(acc_ref[...])

    in_specs = [pl.BlockSpec((tm, tk), lambda n, m, k: (m, k)),
                pl.BlockSpec((None, tk, tn), lambda n, m, k: (l, k, n))]
    args = [a, w]
    if res is not None:
        in_specs.append(pl.BlockSpec((tm, tn), lambda n, m, k: (m, n)))
        args.append(res)
    if split_out:
        out_shape = jax.ShapeDtypeStruct((2, M, N // 2), out_dtype)
        out_spec = pl.BlockSpec((None, tm, tn), lambda n, m, k: (n // nh, m, n % nh))
    else:
        out_shape = jax.ShapeDtypeStruct((M, N), out_dtype)
        out_spec = pl.BlockSpec((tm, tn), lambda n, m, k: (m, n))
    outs, updated, _ = _carried_call(
        body, args, name=name, grid=(N // tn, M // tm, nk), in_specs=in_specs, out_specs=[out_spec],
        out_shape=[out_shape], scratch_shapes=[] if nk == 1 else [pltpu.VMEM((tm, tn), F32)], comm=comm)
    return outs[0] if comm is None else (outs[0], updated)


def mm_nt(a, w, l, *, name, split_in=False, out_dtype=F32, comm=None, norm=None):
    if split_in:
        _, M, Nh = a.shape
        N = 2 * Nh
    else:
        M, N = a.shape
        Nh = N
    K = w.shape[1]
    tm = _pick(M, (512, 256, 128) if norm is not None else (1024, 512, 256, 128))
    tko = _pick(K, (1024, 1408, 512, 256, 128))
    tr = _pick(Nh, (1024, 1408, 768, 512, 256, 128))
    nr = N // tr
    nh = Nh // tr
    n_in, n_out = (5, 2) if norm is not None else (2, 1)
    assert norm is None or tko == K

    def body(*refs):
        a_ref, w_ref = refs[:2]
        o_ref = refs[n_in]
        part = _dot(a_ref[...], w_ref[...], NT)

        def finish(acc):
            if norm is None:
                o_ref[...] = acc.astype(out_dtype)
                return
            h_ref, g_ref, dres_ref = refs[2:5]
            dg_ref = refs[n_in + 1]
            _, vjp = jax.vjp(_rms, h_ref[...], g_ref[...])
            dx, dg = vjp(acc)
            o_ref[...] = dres_ref[...] + dx
            first = pl.program_id(1) == 0

            @pl.when(first)
            def _():
                dg_ref[...] = dg

            @pl.when(jnp.logical_not(first))
            def _():
                dg_ref[...] += dg

        if nr == 1:
            finish(part)
        else:
            acc_ref = refs[n_in + n_out]
            r = pl.program_id(2)

            @pl.when(r == 0)
            def _():
                acc_ref[...] = part

            @pl.when(r > 0)
            def _():
                acc_ref[...] += part

            @pl.when(r == nr - 1)
            def _():
                finish(acc_ref[...])

    if split_in:
        a_spec = pl.BlockSpec((None, tm, tr), lambda ko, m, r: (r // nh, m, r % nh))
    else:
        a_spec = pl.BlockSpec((tm, tr), lambda ko, m, r: (m, r))
    args = [a, w]
    in_specs = [a_spec, pl.BlockSpec((None, tko, tr), lambda ko, m, r: (l, ko, r))]
    out_specs = [pl.BlockSpec((tm, tko), lambda ko, m, r: (m, ko))]
    out_shape = [jax.ShapeDtypeStruct((M, K), out_dtype)]
    if norm is not None:
        h, g, l_norm, dres = norm
        row = pl.BlockSpec((tm, K), lambda ko, m, r: (m, 0))
        args += [h, g, dres]
        in_specs += [row, pl.BlockSpec((None, 1, K), lambda ko, m, r: (l_norm, 0, 0)), row]
        out_specs.append(pl.BlockSpec((1, K), lambda ko, m, r: (0, 0)))
        out_shape.append(jax.ShapeDtypeStruct((1, K), F32))
    outs, _, created = _carried_call(
        body, args, name=name, grid=(K // tko, M // tm, nr), in_specs=in_specs, out_specs=out_specs,
        out_shape=out_shape, scratch_shapes=[] if nr == 1 else [pltpu.VMEM((tm, tko), F32)], comm=comm)
    outs = list(outs) + ([] if comm is None else [created])
    return outs[0] if len(outs) == 1 else tuple(outs)


def mm_tn(a, b, *, name):
    M, K = a.shape
    N = b.shape[1]
    tm = _pick(M, (512, 256, 128))
    tko = _pick(K, (1024, 1408, 704, 512, 256, 128))
    tn = _pick(N, (512, 256, 128))
    nm = M // tm

    def body(a_ref, b_ref, o_ref):
        part = _dot(a_ref[...], b_ref[...], TN)
        r = pl.program_id(2)

        @pl.when(r == 0)
        def _():
            o_ref[...] = part

        @pl.when(r > 0)
        def _():
            o_ref[...] += part

    return pl.pallas_call(
        body, name=name, grid=(K // tko, N // tn, nm),
        in_specs=[pl.BlockSpec((tm, tko), lambda ko, n, r: (r, ko)), pl.BlockSpec((tm, tn), lambda ko, n, r: (r, n))],
        out_specs=pl.BlockSpec((None, tko, tn), lambda ko, n, r: (0, ko, n)),
        out_shape=jax.ShapeDtypeStruct((1, K, N), F32),
        compiler_params=_params(("parallel", "parallel", "arbitrary")),
    )(a, b)


def mm_dw(at, b, *, name, split_b=False):
    K, M = at.shape
    if split_b:
        Nh = b.shape[2]
        N = 2 * Nh
    else:
        N = b.shape[1]
        Nh = N
    tm = _pick(M, (1024, 512, 256, 128))
    tko = _pick(K, (1024, 1408, 704, 512, 256, 128))
    tn = _pick(Nh, (1408, 1024, 768, 512, 256, 128))
    nm = M // tm
    nh = Nh // tn

    def body(a_ref, b_ref, o_ref):
        part = _dot(a_ref[...], b_ref[...], NN)
        r = pl.program_id(2)

        @pl.when(r == 0)
        def _():
            o_ref[...] = part

        @pl.when(r > 0)
        def _():
            o_ref[...] += part

    if split_b:
        b_spec = pl.BlockSpec((None, tm, tn), lambda ko, n, r: (n // nh, r, n % nh))
    else:
        b_spec = pl.BlockSpec((tm, tn), lambda ko, n, r: (r, n))
    return pl.pallas_call(
        body, name=name, grid=(K // tko, N // tn, nm),
        in_specs=[pl.BlockSpec((tko, tm), lambda ko, n, r: (ko, r)), b_spec],
        out_specs=pl.BlockSpec((None, tko, tn), lambda ko, n, r: (0, ko, n)),
        out_shape=jax.ShapeDtypeStruct((1, K, N), F32),
        compiler_params=_params(("parallel", "parallel", "arbitrary")),
    )(at, b)


def _rms(x, g):
    return x * lax.rsqrt(jnp.mean(x * x, axis=-1, keepdims=True) + EPS) * g


def rmsnorm_fwd(h, g, l, *, name):
    S, D = h.shape
    tm = _pick(S, (512, 256, 128))

    def body(h_ref, g_ref, o_ref, ot_ref):
        y = _rms(h_ref[...], g_ref[...])
        o_ref[...] = y.astype(BF16)
        ot_ref[...] = y.T.astype(BF16)

    return pl.pallas_call(
        body, name=name, grid=(S // tm,),
        in_specs=[pl.BlockSpec((tm, D), lambda m: (m, 0)),
                  pl.BlockSpec((None, 1, D), lambda m: (l, 0, 0))],
        out_specs=[pl.BlockSpec((tm, D), lambda m: (m, 0)), pl.BlockSpec((D, tm), lambda m: (0, m))],
        out_shape=[jax.ShapeDtypeStruct((S, D), BF16), jax.ShapeDtypeStruct((D, S), BF16)],
        compiler_params=_params(("parallel",)),
    )(h, g)


def loss_head(h, g, target, *, name):
    S, D = h.shape
    tm = _pick(S, (512, 256, 128))

    def body(h_ref, g_ref, t_ref, loss_ref, dh_ref, dg_ref):
        y, vjp = jax.vjp(_rms, h_ref[...], g_ref[...])
        err = y - t_ref[...]
        part = 0.5 * jnp.sum(jnp.sum(err * err, axis=-1, keepdims=True) / D, axis=0, keepdims=True)
        dx, dg = vjp(err / D)
        dh_ref[...] = dx

        @pl.when(pl.program_id(0) == 0)
        def _():
            dg_ref[...] = dg
            loss_ref[...] = part

        @pl.when(pl.program_id(0) > 0)
        def _():
            dg_ref[...] += dg
            loss_ref[...] += part

    row = pl.BlockSpec((tm, D), lambda m: (m, 0))
    return pl.pallas_call(
        body, name=name, grid=(S // tm,),
        in_specs=[row, pl.BlockSpec((1, D), lambda m: (0, 0)), row],
        out_specs=[pl.BlockSpec((1, 1), lambda m: (0, 0)), row, pl.BlockSpec((1, D), lambda m: (0, 0))],
        out_shape=[jax.ShapeDtypeStruct((1, 1), F32), jax.ShapeDtypeStruct((S, D), F32),
                   jax.ShapeDtypeStruct((1, D), F32)],
        compiler_params=_params(("arbitrary",)),
    )(h, g, target)


def _make_mm(dims_fwd, dims_da, dims_db, swap_da=False, swap_db=False):
    @jax.custom_vjp
    def mm(a, b):
        return _dot(a, b, dims_fwd)

    def fwd(a, b):
        return mm(a, b), (a, b)

    def bwd(resid, g):
        a, b = resid
        da = _dot(b, g, dims_da) if swap_da else _dot(g, b, dims_da)
        db = _dot(g, a, dims_db) if swap_db else _dot(a, g, dims_db)
        return da, db

    mm.defvjp(fwd, bwd)
    return mm


_mm = _make_mm(NN, NT, TN)
_mm_nt = _make_mm(NT, NN, TN, swap_db=True)
_mm_tn = _make_mm(TN, NT, NN, swap_da=True)


def _dot_hi(a, b):
    return jnp.dot(a, b, precision=lax.Precision.HIGHEST, preferred_element_type=F32)


def _sigmoid(x):
    return 0.5 * jnp.tanh(0.5 * x) + 0.5


def _silu(x):
    return x * _sigmoid(x)


def _row_slices(x, n):
    rows = x.shape[0] // n
    return tuple(x[i * rows:(i + 1) * rows] for i in range(n))


@functools.partial(jax.custom_vjp, nondiff_argnums=(1,))
def _split_rows(x, n):
    return _row_slices(x, n)


_split_rows.defvjp(lambda x, n: (_row_slices(x, n), None), lambda n, _, cts: (jnp.concatenate(cts, axis=0),))


def _attn_block(qs, kws, vws, sink, slope, first):
    s = jnp.concatenate([_mm_nt(q, kw) for q, kw in zip(qs, kws)], axis=0) * (HEAD_DIM ** -0.5)
    qi = lax.broadcasted_iota(jnp.int32, s.shape, 0) % WINDOW
    ki = lax.broadcasted_iota(jnp.int32, s.shape, 1)
    dist = qi + WINDOW - ki
    valid = (dist >= 0) & (dist < WINDOW) & (ki >= WINDOW * first)
    s = jnp.where(valid, s - slope * dist.astype(F32), -jnp.inf)
    m = lax.stop_gradient(jnp.maximum(jnp.max(s, axis=-1, keepdims=True), sink))
    e = jnp.exp(s - m)
    denom = jnp.sum(e, axis=-1, keepdims=True) + jnp.exp(sink - m)
    ps = _split_rows(e * (1.0 / denom), len(qs))
    return tuple(_mm(p, vw) for p, vw in zip(ps, vws))


def _attn_specs(Hkv, order):
    G = Q_PER_KV
    qspec = pl.BlockSpec((Hkv, G, WINDOW, HEAD_DIM), lambda n: (0, 0, order(n), 0))
    prev = pl.BlockSpec((Hkv, WINDOW, HEAD_DIM), lambda n: (0, jnp.maximum(order(n) - 1, 0), 0))
    cur = pl.BlockSpec((Hkv, WINDOW, HEAD_DIM), lambda n: (0, order(n), 0))
    col = pl.BlockSpec((Hkv, G * WINDOW, 1), lambda n: (0, 0, 0))
    return qspec, prev, cur, col


def attn_fwd(q, k, v, sink_col, slope_col, *, name, comm=None):
    Hkv, G, S, _ = q.shape
    qspec, prev, cur, col = _attn_specs(Hkv, lambda n: n)

    def body(q_ref, kp_ref, kc_ref, vp_ref, vc_ref, sink_ref, slope_ref, o_ref):
        first = (pl.program_id(0) == 0).astype(jnp.int32)
        qs = tuple(q_ref[h].reshape(G * WINDOW, HEAD_DIM) for h in range(Hkv))
        kws = tuple(jnp.concatenate([kp_ref[h], kc_ref[h]], axis=0) for h in range(Hkv))
        vws = tuple(jnp.concatenate([vp_ref[h], vc_ref[h]], axis=0) for h in range(Hkv))
        outs = _attn_block(qs, kws, vws, sink_ref[...].reshape(Hkv * G * WINDOW, 1),
                           slope_ref[...].reshape(Hkv * G * WINDOW, 1), first)
        o_ref[...] = jnp.stack([o.reshape(G, WINDOW, HEAD_DIM).astype(BF16) for o in outs])

    outs, updated, created = _carried_call(
        body, [q, k, k, v, v, sink_col, slope_col], name=name, grid=(S // WINDOW,),
        in_specs=[qspec, prev, cur, prev, cur, col, col], out_specs=[qspec],
        out_shape=[jax.ShapeDtypeStruct(q.shape, BF16)], comm=comm)
    return outs[0] if comm is None else (outs[0], updated, created)


def attn_bwd(q, k, v, sink_col, slope_col, do, *, name, comm=None):
    Hkv, G, S, _ = q.shape
    nb = S // WINDOW
    qspec, prev, cur, col = _attn_specs(Hkv, lambda n: nb - 1 - n)

    def body(q_ref, kp_ref, kc_ref, vp_ref, vc_ref, sink_ref, slope_ref, do_ref,
             dq_ref, dk_ref, dv_ref, dsink_ref, dk_carry, dv_carry):
        @pl.when(pl.program_id(0) == 0)
        def _():
            dk_carry[...] = jnp.zeros_like(dk_carry)
            dv_carry[...] = jnp.zeros_like(dv_carry)
            dsink_ref[...] = jnp.zeros_like(dsink_ref)

        first = (pl.program_id(0) == nb - 1).astype(jnp.int32)
        qs = tuple(q_ref[h].reshape(G * WINDOW, HEAD_DIM).astype(F32) for h in range(Hkv))
        kws = tuple(jnp.concatenate([kp_ref[h], kc_ref[h]], axis=0).astype(F32) for h in range(Hkv))
        vws = tuple(jnp.concatenate([vp_ref[h], vc_ref[h]], axis=0).astype(F32) for h in range(Hkv))
        fn = functools.partial(_attn_block, slope=slope_ref[...].reshape(Hkv * G * WINDOW, 1), first=first)
        _, vjp = jax.vjp(fn, qs, kws, vws, sink_ref[...].reshape(Hkv * G * WINDOW, 1))
        dqs, dkws, dvws, dsink = vjp(tuple(do_ref[h].reshape(G * WINDOW, HEAD_DIM).astype(F32) for h in range(Hkv)))
        dkw, dvw = jnp.stack(dkws), jnp.stack(dvws)
        dq_ref[...] = jnp.stack([dq.reshape(G, WINDOW, HEAD_DIM) for dq in dqs]).astype(BF16)
        dk_ref[...] = (dkw[:, WINDOW:] + dk_carry[...]).astype(BF16)
        dv_ref[...] = (dvw[:, WINDOW:] + dv_carry[...]).astype(BF16)
        dk_carry[...] = dkw[:, :WINDOW]
        dv_carry[...] = dvw[:, :WINDOW]
        dsink_ref[...] += dsink.reshape(Hkv, G * WINDOW, 1)

    kv_shape = jax.ShapeDtypeStruct(k.shape, BF16)
    outs, _, created = _carried_call(
        body, [q, k, k, v, v, sink_col, slope_col, do], name=name, grid=(nb,),
        in_specs=[qspec, prev, cur, prev, cur, col, col, qspec],
        out_specs=[qspec, cur, cur, col],
        out_shape=[jax.ShapeDtypeStruct(q.shape, BF16), kv_shape, kv_shape,
                   jax.ShapeDtypeStruct(sink_col.shape, F32)],
        scratch_shapes=[pltpu.VMEM((Hkv, WINDOW, HEAD_DIM), F32), pltpu.VMEM((Hkv, WINDOW, HEAD_DIM), F32)],
        comm=comm)
    return outs if comm is None else (*outs, created)


def _hgrn_consts():
    C = HG_CHUNK
    t = np.arange(C)[:, None]
    j = np.arange(C)[None, :]
    cq, ck, mk = [], [], []
    H = C // 2
    while H >= 1:
        start = (t // H) * H
        cq.append((j > start) & (j <= t))
        ck.append((j > t) & (j <= start + H))
        mk.append(((t // H) % 2 == 1) & ((j // H) == (t // H) - 1))
        H //= 2
    ltri = (j <= t)
    ops = np.concatenate([ltri] + cq + ck, axis=0).astype(np.float32)
    ops3 = np.concatenate([ops, ops, ops], axis=1)
    opst3 = np.concatenate([ops.T, ops.T, ops.T], axis=1)
    return (jnp.asarray(ops3, BF16), jnp.asarray(opst3, BF16), jnp.asarray(np.stack(mk).astype(np.float32)))


def _split3(x):
    hi = x.astype(BF16)
    r = x - hi.astype(F32)
    mid = r.astype(BF16)
    lo = (r - mid.astype(F32)).astype(BF16)
    return jnp.concatenate([hi, mid, lo], axis=0)


@jax.custom_vjp
def _seg_sums(g, ops3, opst3):
    C = g.shape[0]
    out = jnp.dot(ops3, _split3(g), preferred_element_type=F32)
    return tuple(out[i * C:(i + 1) * C] for i in range(ops3.shape[0] // C))


def _seg_sums_fwd(g, ops3, opst3):
    return _seg_sums(g, ops3, opst3), opst3


def _seg_sums_bwd(opst3, cts):
    ct = jnp.concatenate(cts, axis=0)
    return jnp.dot(opst3, _split3(ct), preferred_element_type=F32), None, None


_seg_sums.defvjp(_seg_sums_fwd, _seg_sums_bwd)


def _head_slices(x):
    return tuple(x[:, h * HG_EXPAND:(h + 1) * HG_EXPAND] for h in range(x.shape[1] // HG_EXPAND))


@jax.custom_vjp
def _split_heads(x):
    return _head_slices(x)


_split_heads.defvjp(lambda x: (_head_slices(x), None), lambda _, cts: (jnp.concatenate(cts, axis=1),))


@jax.custom_vjp
def _join_heads(xs):
    return jnp.concatenate(xs, axis=1)


_join_heads.defvjp(lambda xs: (jnp.concatenate(xs, axis=1), None), lambda _, ct: (_head_slices(ct),))


def _hgrn_chunk(qr, fr, iv, gr, sts, logits, ng, ops3, opst3, mk, layer):
    n_levels = mk.shape[0]
    depth, D = logits.shape
    e = jnp.exp(logits - lax.stop_gradient(jnp.max(logits, axis=0, keepdims=True)))
    sm = e / jnp.sum(e, axis=0, keepdims=True)
    row = lax.broadcasted_iota(jnp.int32, (depth, D), 0)
    lb = jnp.sum(jnp.where((row >= 1) & (row <= layer), sm, 0.0), axis=0, keepdims=True)

    q = _silu(qr)
    a = jnp.log(lb)
    c = jnp.log(1.0 - lb) - (jnp.maximum(-fr, 0.0) + jnp.log(1.0 + jnp.exp(-jnp.abs(fr))))
    mx = lax.stop_gradient(jnp.maximum(a, c))
    g = mx + jnp.log(jnp.exp(a - mx) + jnp.exp(c - mx))
    k = (1.0 - lb) * _sigmoid(-fr)

    sums = _seg_sums(g, ops3, opst3)
    b = sums[0]
    b_last = jnp.sum(g, axis=0, keepdims=True)

    ql = [_split_heads(q * jnp.exp(sums[1 + lv])) for lv in range(n_levels)]
    kl = [_split_heads(k * jnp.exp(sums[1 + n_levels + lv])) for lv in range(n_levels)]
    qb, kb = _split_heads(q * jnp.exp(b)), _split_heads(k * jnp.exp(b_last - b))
    qk, ivh, gate, decay = _split_heads(q * k), _split_heads(iv), _split_heads(_silu(gr)), _split_heads(jnp.exp(b_last))
    outs, new_sts = [], []
    for h, st in enumerate(sts):
        amat = None
        for lv in range(n_levels):
            part = _mm_nt(ql[lv][h], kl[lv][h]) * mk[lv]
            amat = part if amat is None else amat + part
        o = jnp.sum(qk[h], axis=-1, keepdims=True) * ivh[h] + _mm(amat, ivh[h]) + _mm_nt(qb[h], st)
        new_sts.append(st * decay[h] + _mm_tn(ivh[h], kb[h]))
        on = o * lax.rsqrt(jnp.mean(o * o, axis=-1, keepdims=True) + EPS) * ng
        outs.append(on * gate[h])
    return _join_heads(tuple(outs)), tuple(new_sts)


def _hgrn_in_specs(D, depth, order):
    C = HG_CHUNK
    nl = int(math.log2(C))
    n_ops = 1 + 2 * nl
    return [pl.BlockSpec((C, 4 * D), lambda c: (order(c), 0)),
            pl.BlockSpec((depth, D), lambda c: (0, 0)),
            pl.BlockSpec((1, HG_EXPAND), lambda c: (0, 0)),
            pl.BlockSpec((n_ops * C, 3 * C), lambda c: (0, 0)),
            pl.BlockSpec((C, 3 * n_ops * C), lambda c: (0, 0)),
            pl.BlockSpec((nl, C, C), lambda c: (0, 0, 0))]


def hgrn_fwd(proj, logits, ng, layer, *, name, comm=None):
    S = proj.shape[0]
    D = proj.shape[1] // 4
    H = D // HG_EXPAND
    C = HG_CHUNK
    nc = S // C
    consts = _hgrn_consts()

    def body(p_ref, lg_ref, ng_ref, ops3_ref, opst3_ref, mk_ref, o_ref, st_out_ref, st_ref):
        @pl.when(pl.program_id(0) == 0)
        def _():
            st_ref[...] = jnp.zeros_like(st_ref)

        st_out_ref[...] = st_ref[...]
        out, new_sts = _hgrn_chunk(*(p_ref[:, j * D:(j + 1) * D] for j in range(4)), tuple(st_ref[h] for h in range(H)),
                                   lg_ref[...], ng_ref[...], ops3_ref[...], opst3_ref[...], mk_ref[...], layer)
        o_ref[...] = out.astype(BF16)
        for h in range(H):
            st_ref[h] = new_sts[h]

    outs, updated, created = _carried_call(
        body, [proj, logits, ng, *consts], name=name, grid=(nc,),
        in_specs=_hgrn_in_specs(D, logits.shape[0], lambda c: c),
        out_specs=[pl.BlockSpec((C, D), lambda c: (c, 0)),
                   pl.BlockSpec((None, H, HG_EXPAND, HG_EXPAND), lambda c: (c, 0, 0, 0))],
        out_shape=[jax.ShapeDtypeStruct((S, D), BF16),
                   jax.ShapeDtypeStruct((nc, H, HG_EXPAND, HG_EXPAND), F32)],
        scratch_shapes=[pltpu.VMEM((H, HG_EXPAND, HG_EXPAND), F32)], comm=comm)
    return outs if comm is None else (*outs, updated, created)


def hgrn_bwd(proj, logits, ng, states, dout, layer, *, name, comm=None):
    S = proj.shape[0]
    D = proj.shape[1] // 4
    H = D // HG_EXPAND
    C = HG_CHUNK
    nc = S // C
    depth = logits.shape[0]
    consts = _hgrn_consts()
    rev = lambda c: nc - 1 - c

    def body(p_ref, lg_ref, ng_ref, ops3_ref, opst3_ref, mk_ref, st_in_ref, do_ref,
             dp_ref, dlg_ref, dng_ref, dst_ref):
        @pl.when(pl.program_id(0) == 0)
        def _():
            dst_ref[...] = jnp.zeros_like(dst_ref)
            dlg_ref[...] = jnp.zeros_like(dlg_ref)
            dng_ref[...] = jnp.zeros_like(dng_ref)

        fn = functools.partial(_hgrn_chunk, ops3=ops3_ref[...], opst3=opst3_ref[...], mk=mk_ref[...], layer=layer)
        _, vjp = jax.vjp(fn, *(p_ref[:, j * D:(j + 1) * D] for j in range(4)), tuple(st_in_ref[h] for h in range(H)),
                         lg_ref[...], ng_ref[...])
        *dproj, dsts, dlg, dng = vjp((do_ref[...], tuple(dst_ref[h] for h in range(H))))
        for j in range(4):
            dp_ref[:, j * D:(j + 1) * D] = dproj[j].astype(BF16)
        for h in range(H):
            dst_ref[h] = dsts[h]
        dlg_ref[...] += dlg
        dng_ref[...] += dng

    in_specs = _hgrn_in_specs(D, depth, rev)
    in_specs += [pl.BlockSpec((None, H, HG_EXPAND, HG_EXPAND), lambda c: (rev(c), 0, 0, 0)),
                 pl.BlockSpec((C, D), lambda c: (rev(c), 0))]
    outs, _, created = _carried_call(
        body, [proj, logits, ng, *consts, states, dout], name=name, grid=(nc,), in_specs=in_specs,
        out_specs=[pl.BlockSpec((C, 4 * D), lambda c: (rev(c), 0)),
                   pl.BlockSpec((depth, D), lambda c: (0, 0)), pl.BlockSpec((1, HG_EXPAND), lambda c: (0, 0))],
        out_shape=[jax.ShapeDtypeStruct((S, 4 * D), BF16), jax.ShapeDtypeStruct((depth, D), F32),
                   jax.ShapeDtypeStruct((1, HG_EXPAND), F32)],
        scratch_shapes=[pltpu.VMEM((H, HG_EXPAND, HG_EXPAND), F32)], comm=comm)
    return outs if comm is None else (*outs, created)


def _shift_down(u, prev, shift):
    rolled = pltpu.roll(u, shift, axis=0)
    rows = lax.broadcasted_iota(jnp.int32, prev.shape, 0)
    top = jnp.where(rows < shift, pltpu.roll(prev, shift, axis=0), rolled[:SUBLANES])
    return jnp.concatenate([top, rolled[SUBLANES:]], axis=0)


def _shift_up(x, nxt, shift):
    tm = x.shape[0]
    rolled = pltpu.roll(x, tm - shift, axis=0)
    rows = lax.broadcasted_iota(jnp.int32, nxt.shape, 0)
    bottom = jnp.where(rows >= SUBLANES - shift, pltpu.roll(nxt, SUBLANES - shift, axis=0), rolled[tm - SUBLANES:])
    return jnp.concatenate([rolled[:tm - SUBLANES], bottom], axis=0)


def _conv(u, prev, w_ref, b_ref, half):
    u1 = _shift_down(u, prev, 1)
    u2 = _shift_down(u, prev, 2)
    return b_ref[half] + w_ref[half, 0:1, :] * u2 + w_ref[half, 1:2, :] * u1 + w_ref[half, 2:3, :] * u


def _halo_rows(ref, keep, last):
    full = ref.astype(F32)
    return (full[HALO_ROWS - SUBLANES:] if last else full[:SUBLANES]) * keep


def _conv_specs(tm, tc, l):
    nprev = lambda m: jnp.maximum(m * (tm // HALO_ROWS) - 1, 0)
    tile = pl.BlockSpec((2, tm, tc), lambda j, m: (0, m, j))
    halo = pl.BlockSpec((2, HALO_ROWS, tc), lambda j, m: (0, nprev(m), j))
    wspec = pl.BlockSpec((None, 2, 3, tc), lambda j, m: (l, 0, 0, j))
    bspec = pl.BlockSpec((None, 2, 1, tc), lambda j, m: (l, 0, 0, j))
    return tile, halo, wspec, bspec


def convact_fwd(u, cw, cb, l, *, name, comm=None):
    _, S, Fh = u.shape
    tm = _pick(S, (256, 128))
    tc = _pick(Fh, (1408, 512, 256, 128))
    tile, halo, wspec, bspec = _conv_specs(tm, tc, l)

    def body(u_ref, p_ref, w_ref, b_ref, o_ref, ot_ref, c_ref):
        keep = (pl.program_id(1) > 0).astype(F32)
        cg = _conv(u_ref[0].astype(F32), _halo_rows(p_ref[0], keep, True), w_ref, b_ref, 0)
        cv = _conv(u_ref[1].astype(F32), _halo_rows(p_ref[1], keep, True), w_ref, b_ref, 1)
        act = _silu(cg) * cv
        o_ref[...] = act.astype(BF16)
        ot_ref[...] = act.T.astype(BF16)
        c_ref[0] = cg.astype(BF16)
        c_ref[1] = cv.astype(BF16)

    outs, updated, _ = _carried_call(
        body, [u, u, cw, cb], name=name, grid=(Fh // tc, S // tm), in_specs=[tile, halo, wspec, bspec],
        out_specs=[pl.BlockSpec((tm, tc), lambda j, m: (m, j)), pl.BlockSpec((tc, tm), lambda j, m: (j, m)), tile],
        out_shape=[jax.ShapeDtypeStruct((S, Fh), BF16), jax.ShapeDtypeStruct((Fh, S), BF16),
                   jax.ShapeDtypeStruct(u.shape, BF16)], comm=comm)
    return tuple(outs) if comm is None else (*outs, updated)


def convact_bwd(c, u, cw, l, da, *, name):
    _, S, Fh = u.shape
    tm = _pick(S, (256, 128))
    tc = _pick(Fh, (1408, 512, 256, 128))
    nm = S // tm

    def body(c_ref, u_ref, w_ref, da_ref, du_ref, dw_ref, db_ref, carry):
        @pl.when(pl.program_id(1) == 0)
        def _():
            carry[...] = jnp.zeros_like(carry)
            dw_ref[...] = jnp.zeros_like(dw_ref)
            db_ref[...] = jnp.zeros_like(db_ref)

        cg, cv = c_ref[0].astype(F32), c_ref[1].astype(F32)
        da_t = da_ref[...]
        sg = _sigmoid(cg)
        dcv = da_t * (cg * sg)
        dcg = da_t * cv * (sg * (1.0 + cg * (1.0 - sg)))
        rs = lambda t: jnp.sum(t, axis=0, keepdims=True)
        for half, dc in ((0, dcg), (1, dcv)):
            nxt = carry[half]
            d1, d2 = _shift_up(dc, nxt, 1), _shift_up(dc, nxt, 2)
            du = w_ref[half, 2:3, :] * dc + w_ref[half, 1:2, :] * d1 + w_ref[half, 0:1, :] * d2
            du_ref[half] = du.astype(BF16)
            uh = u_ref[half].astype(F32)
            for j, d in enumerate((d2, d1, dc)):
                dw_ref[half, j:j + 1, :] += rs(d * uh)
            db_ref[half] += rs(dc)
            carry[half] = dc[:SUBLANES]

    tile = pl.BlockSpec((2, tm, tc), lambda j, m: (0, nm - 1 - m, j))
    return pl.pallas_call(
        body, name=name, grid=(Fh // tc, nm),
        in_specs=[tile, tile, pl.BlockSpec((None, 2, 3, tc), lambda j, m: (l, 0, 0, j)),
                  pl.BlockSpec((tm, tc), lambda j, m: (nm - 1 - m, j))],
        out_specs=[tile, pl.BlockSpec((2, 3, tc), lambda j, m: (0, 0, j)),
                   pl.BlockSpec((2, 1, tc), lambda j, m: (0, 0, j))],
        out_shape=[jax.ShapeDtypeStruct(u.shape, BF16), jax.ShapeDtypeStruct((2, 3, Fh), F32),
                   jax.ShapeDtypeStruct((2, 1, Fh), F32)],
        scratch_shapes=[pltpu.VMEM((2, SUBLANES, tc), F32)],
        compiler_params=_params(("parallel", "arbitrary")),
    )(c, u, cw, da)


def _adamw_math(w, g, m, v):
    m = ADAM_B1 * m + (1.0 - ADAM_B1) * g
    v = ADAM_B2 * v + (1.0 - ADAM_B2) * (g * g)
    m_hat = m / (1.0 - ADAM_B1 ** ADAM_STEP)
    v_hat = v / (1.0 - ADAM_B2 ** ADAM_STEP)
    delta = -ADAM_LR * (m_hat / (jnp.sqrt(v_hat) + ADAM_EPS) + ADAM_WD * w)
    return delta, m, v


def _row_tile(R, Ccols, n_streams):
    for tr in (512, 256, 128, 64, 32, 16, 8):
        if R % tr == 0 and tr * Ccols * 4 * n_streams * 2 <= VMEM_LIMIT_BYTES // 2:
            return tr
    return R


def adamw(w, g, m, v, *, name):
    R, Ccols = w.shape
    tr = _row_tile(R, Ccols, 7)

    def body(w_ref, g_ref, m_ref, v_ref, d_ref, mo_ref, vo_ref):
        d, mn, vn = _adamw_math(w_ref[...], g_ref[...], m_ref[...], v_ref[...])
        d_ref[...] = d
        mo_ref[...] = mn
        vo_ref[...] = vn

    spec = pl.BlockSpec((tr, Ccols), lambda i: (i, 0))
    shp = jax.ShapeDtypeStruct((R, Ccols), F32)
    return pl.pallas_call(
        body, name=name, grid=(R // tr,), in_specs=[spec] * 4, out_specs=[spec] * 3,
        out_shape=[shp, shp, shp], compiler_params=_params(("parallel",)),
    )(w, g, m, v)


def _prefetch_call(body, pos, args, *, name, grid, in_specs, out_specs, out_shape, sem, aliases=None):
    return pl.pallas_call(
        lambda pos_ref, *refs: body(*refs), name=name,
        grid_spec=pltpu.PrefetchScalarGridSpec(num_scalar_prefetch=1, grid=grid, in_specs=in_specs,
                                               out_specs=out_specs),
        out_shape=out_shape, input_output_aliases={1 + i: o for i, o in (aliases or {}).items()},
        compiler_params=_params(sem),
    )(pos, *args)


def cast_place(w, pos, *, row_sharded, name):
    L, R, Ccols = w.shape

    def body(w_ref, o_ref):
        o_ref[...] = w_ref[...].astype(BF16)

    if row_sharded:
        grid = (L,)
        in_spec = pl.BlockSpec((None, R, Ccols), lambda l, pos: (l, 0, 0))
        out_spec = pl.BlockSpec((None, R, Ccols), lambda l, pos: (l, pos[0], 0))
        out_shape = jax.ShapeDtypeStruct((L, N_CHIPS * R, Ccols), BF16)
        sem = ("parallel",)
    else:
        tr = _pick(R, (512, 256, 128))
        grid = (L, R // tr)
        in_spec = pl.BlockSpec((None, tr, Ccols), lambda l, i, pos: (l, i, 0))
        out_spec = pl.BlockSpec((None, tr, Ccols), lambda l, i, pos: (l, i, pos[0]))
        out_shape = jax.ShapeDtypeStruct((L, R, N_CHIPS * Ccols), BF16)
        sem = ("parallel", "parallel")
    return _prefetch_call(body, pos, [w], name=name, grid=grid, in_specs=[in_spec], out_specs=out_spec,
                          out_shape=out_shape, sem=sem)


def pair_add(g, theirs, pos, *, row_sharded, name):
    def body(g_ref, t_ref, o_ref, ob_ref):
        s = g_ref[...] + t_ref[...]
        o_ref[...] = s
        ob_ref[...] = s.astype(BF16)

    if row_sharded:
        L, _, _, r2, D = g.shape
        grid = (L, N_CHIPS)
        g_spec = pl.BlockSpec((None, None, None, r2, D), lambda l, q, pos: (l, q, pos[1], 0, 0))
        t_spec = pl.BlockSpec((None, None, r2, D), lambda l, q, pos: (l, q, 0, 0))
    else:
        L, K, N = g.shape
        n = N // N_CHIPS
        tr = _pick(K // 2, (256, 128))
        nb = (K // 2) // tr
        grid = (L, nb, N_CHIPS)
        g_spec = pl.BlockSpec((None, tr, n), lambda l, i, j, pos: (l, pos[1] * nb + i, j))
        t_spec = pl.BlockSpec((None, tr, n), lambda l, i, j, pos: (l, i, j))
    return _prefetch_call(body, pos, [g, theirs], name=name, grid=grid, in_specs=[g_spec, t_spec],
                          out_specs=[t_spec, t_spec],
                          out_shape=[jax.ShapeDtypeStruct(theirs.shape, F32), jax.ShapeDtypeStruct(theirs.shape, BF16)],
                          sem=("parallel",) * len(grid))


def chip_add(pair, landed, pos, buf, li, n_layers, *, row_sharded, name):
    def body(p_ref, a_ref, b_ref, c_ref, *rest):
        rest[-1][...] = ((p_ref[...] + a_ref[...].astype(F32)) + b_ref[...].astype(F32)) + c_ref[...].astype(F32)

    if row_sharded:
        _, _, r2, D = pair.shape
        grid = (1,)
        p_spec = pl.BlockSpec((None, None, r2, D), lambda i, pos: (0, pos[0], 0, 0))
        l_specs = [pl.BlockSpec((None, r2, D), functools.partial(lambda i, pos, k: (k, 0, 0), k=k)) for k in range(3)]
        out_spec = pl.BlockSpec((None, None, r2, D), lambda i, pos: (li, pos[1], 0, 0))
        out_shape = jax.ShapeDtypeStruct((n_layers, 2, r2, D), F32)
    else:
        _, K2, N = pair.shape
        n = N // N_CHIPS
        tr = _pick(K2, (256, 128))
        nb = K2 // tr
        grid = (nb,)
        p_spec = pl.BlockSpec((None, tr, n), lambda i, pos: (0, i, pos[0]))
        l_specs = [pl.BlockSpec((None, tr, n), functools.partial(lambda i, pos, k: (k, i, 0), k=k)) for k in range(3)]
        out_spec = pl.BlockSpec((None, tr, n), lambda i, pos: (li, pos[1] * nb + i, 0))
        out_shape = jax.ShapeDtypeStruct((n_layers, 2 * K2, n), F32)
    args, in_specs, aliases = [pair, landed, landed, landed], [p_spec] + l_specs, None
    if buf is not None:
        args.append(buf)
        in_specs.append(pl.BlockSpec(memory_space=pl.ANY))
        aliases = {4: 0}
    return _prefetch_call(body, pos, args, name=name, grid=grid, in_specs=in_specs, out_specs=out_spec,
                          out_shape=out_shape, sem=("parallel",), aliases=aliases)


def _position():
    return lax.axis_index("x"), lax.axis_index("y"), lax.axis_index("c")


def _other_chips(x, y):
    return [(1 - x, y), (x, 1 - y), (1 - x, 1 - y)]


HBM_ANY = pl.BlockSpec(memory_space=pl.ANY)
COMM_PARAMS = pltpu.CompilerParams(has_side_effects=True)


class Comm:
    def __init__(self, ro=(), inout=(), new=(), sems=(), phases=()):
        self.ro, self.inout, self.new, self.sems, self.phases = list(ro), list(inout), list(new), list(sems), list(phases)


def _carried_call(body, args, *, name, grid, in_specs, out_specs, out_shape, scratch_shapes=(), comm=None):
    comm = comm or Comm()
    n_in, n_out, n_sc = len(in_specs), len(out_shape), len(scratch_shapes)
    n_ro, n_io, n_new, n_sem = len(comm.ro), len(comm.inout), len(comm.new), len(comm.sems)
    last = int(np.prod(grid)) - 1

    def wrapped(*refs):
        bounds = np.cumsum([0, n_in, n_ro, n_io, n_out, n_io, n_new, n_sc, n_sem])
        core_in, ro, _, core_out, io, new, core_sc, sems = (refs[a:b] for a, b in zip(bounds[:-1], bounds[1:]))
        step = pl.program_id(0)
        for d in range(1, len(grid)):
            step = step * grid[d] + pl.program_id(d)
        for frac, fn in comm.phases[:-1]:
            pl.when(step == int(frac * last))(functools.partial(fn, ro, io, new, sems))
        body(*core_in, *core_out, *core_sc)
        if comm.phases:
            pl.when(step == last)(functools.partial(comm.phases[-1][1], ro, io, new, sems))

    outs = pl.pallas_call(
        wrapped, name=name, grid=grid,
        in_specs=list(in_specs) + [HBM_ANY] * (n_ro + n_io),
        out_specs=list(out_specs) + [HBM_ANY] * (n_io + n_new),
        out_shape=list(out_shape) + [jax.ShapeDtypeStruct(a.shape, a.dtype) for a in comm.inout] + comm.new,
        input_output_aliases={n_in + n_ro + j: n_out + j for j in range(n_io)},
        scratch_shapes=list(scratch_shapes) + comm.sems,
        compiler_params=pltpu.CompilerParams(dimension_semantics=("arbitrary",) * len(grid),
                                             vmem_limit_bytes=VMEM_LIMIT_BYTES, has_side_effects=bool(comm.phases)),
    )(*args, *comm.ro, *comm.inout)
    return outs[:n_out], outs[n_out:n_out + n_io], outs[n_out + n_io:]


def run_comm(comm, *, name):
    n_ro, n_io, n_new = len(comm.ro), len(comm.inout), len(comm.new)

    def body(*refs):
        bounds = np.cumsum([0, n_ro, n_io, n_io, n_new, len(comm.sems)])
        ro, _, io, new, sems = (refs[a:b] for a, b in zip(bounds[:-1], bounds[1:]))
        for _, fn in comm.phases:
            fn(ro, io, new, sems)

    outs = pl.pallas_call(
        body, name=name, in_specs=[HBM_ANY] * (n_ro + n_io), out_specs=[HBM_ANY] * (n_io + n_new),
        out_shape=[jax.ShapeDtypeStruct(a.shape, a.dtype) for a in comm.inout] + comm.new,
        input_output_aliases={n_ro + j: j for j in range(n_io)},
        scratch_shapes=comm.sems, compiler_params=COMM_PARAMS,
    )(*comm.ro, *comm.inout)
    return outs[:n_io], outs[n_io:]


def _rdma(src, dst, send_sem, recv_sem, device):
    return pltpu.make_async_remote_copy(src_ref=src, dst_ref=dst, send_sem=send_sem, recv_sem=recv_sem,
                                        device_id=device, device_id_type=MESH)


def gather_comm(bufs, slices):
    n = len(slices)

    def region(io, s, h, chip):
        b, li, row_sharded = slices[s]
        ref = io[b]
        if row_sharded:
            r = ref.shape[1] // N_CHIPS
            return ref.at[li, pl.ds(chip * r + h * (r // 2), r // 2), :]
        K, nn = ref.shape[1], ref.shape[2] // N_CHIPS
        return ref.at[li, pl.ds(h * (K // 2), K // 2), pl.ds(chip * nn, nn)]

    def copies(io, sems):
        x, y, c = _position()
        me = 2 * x + y
        chips = _other_chips(x, y)
        ids = [2 * cx + cy for cx, cy in chips]

        def ici(s, k, src):
            return _rdma(region(io, s, c, me), region(io, s, c, src), sems[0].at[s, k], sems[1].at[s, k], (*chips[k], c))

        def d2d(s, k, h):
            return _rdma(region(io, s, h, ids[k]), region(io, s, h, ids[k]), sems[2].at[s, k], sems[3].at[s, k],
                         (x, y, 1 - c))

        return me, ids, c, ici, d2d

    pairs = [(s, k) for s in range(n) for k in range(3)]

    def start(ro, io, new, sems):
        me, _, _, ici, _ = copies(io, sems)
        for s, k in pairs:
            ici(s, k, me).start()

    def pass_on(ro, io, new, sems):
        _, ids, c, ici, d2d = copies(io, sems)
        for s, k in pairs:
            ici(s, k, ids[k]).wait_recv()
            d2d(s, k, c).start()

    def finish(ro, io, new, sems):
        me, _, c, ici, d2d = copies(io, sems)
        for s, k in pairs:
            d2d(s, k, 1 - c).wait_recv()
        for s, k in pairs:
            ici(s, k, me).wait_send()
            d2d(s, k, c).wait_send()

    return Comm(inout=bufs, sems=[pltpu.SemaphoreType.DMA((n, 3))] * 4,
                phases=[(0.0, start), (0.6, pass_on), (1.0, finish)])


def gather_small(shard, *, name):
    n = shard.shape[2]

    def body(s_ref, o_ref, send_sem, recv_sem, local_sem):
        x, y, c = _position()
        me = 2 * x + y
        chips = _other_chips(x, y)
        place = lambda chip: o_ref.at[:, :, pl.ds(chip * n, n)]
        local = pltpu.make_async_copy(s_ref, place(me), local_sem)
        local.start()
        for k in range(3):
            _rdma(s_ref, place(me), send_sem.at[k], recv_sem.at[k], (*chips[k], c)).start()
        for k in range(3):
            _rdma(s_ref, place(2 * chips[k][0] + chips[k][1]), send_sem.at[k], recv_sem.at[k], (*chips[k], c)).wait()
        local.wait()

    return pl.pallas_call(
        body, name=name, in_specs=[HBM_ANY], out_specs=HBM_ANY,
        out_shape=jax.ShapeDtypeStruct((shard.shape[0], shard.shape[1], N_CHIPS * n), shard.dtype),
        scratch_shapes=[pltpu.SemaphoreType.DMA((3,)), pltpu.SemaphoreType.DMA((3,)), pltpu.SemaphoreType.DMA],
        compiler_params=COMM_PARAMS,
    )(shard)


def swap_comm(grads):
    def copies(ro, new, sems):
        x, y, c = _position()
        h = 1 - c
        cps = []
        for s, (a, row_sharded) in enumerate(grads):
            if row_sharded:
                r = a.shape[1] // N_CHIPS
                for q in range(N_CHIPS):
                    cps.append(_rdma(ro[s].at[:, pl.ds(q * r + h * (r // 2), r // 2), :], new[s].at[:, q],
                                     sems[0].at[s, q], sems[1].at[s, q], (x, y, 1 - c)))
            else:
                K = a.shape[1]
                cps.append(_rdma(ro[s].at[:, pl.ds(h * (K // 2), K // 2), :], new[s],
                                 sems[0].at[s, 0], sems[1].at[s, 0], (x, y, 1 - c)))
        return cps

    def start(ro, io, new, sems):
        for cp in copies(ro, new, sems):
            cp.start()

    def finish(ro, io, new, sems):
        for cp in copies(ro, new, sems):
            cp.wait()

    theirs = []
    for a, row_sharded in grads:
        if row_sharded:
            theirs.append(jax.ShapeDtypeStruct((1, N_CHIPS, a.shape[1] // N_CHIPS // 2, a.shape[2]), a.dtype))
        else:
            theirs.append(jax.ShapeDtypeStruct((1, a.shape[1] // 2, a.shape[2]), a.dtype))
    return Comm(ro=[a for a, _ in grads], new=theirs, sems=[pltpu.SemaphoreType.DMA((len(grads), N_CHIPS))] * 2,
                phases=[(0.0, start), (1.0, finish)])


def merge_comms(comms):
    spans, ro, io, new, sems = [], [], [], [], []
    for cm in comms:
        spans.append((len(ro), len(io), len(new), len(sems)))
        ro, io, new, sems = ro + cm.ro, io + cm.inout, new + cm.new, sems + cm.sems

    def phase(frac):
        def run(R, I, N, S):
            for cm, (a, b, d, e) in zip(comms, spans):
                for f, fn in cm.phases:
                    if f == frac:
                        fn(R[a:a + len(cm.ro)], I[b:b + len(cm.inout)], N[d:d + len(cm.new)], S[e:e + len(cm.sems)])
        return run

    fracs = sorted({f for cm in comms for f, _ in cm.phases})
    merged = Comm(ro=ro, inout=io, new=new, sems=sems, phases=[(f, phase(f)) for f in fracs])
    cut = lambda created: [created[d:d + len(cm.new)] for cm, (_, _, d, _) in zip(comms, spans)]
    return merged, cut


def scatter_comm(halves):
    def copies(ro, new, sems):
        x, y, c = _position()
        chips = _other_chips(x, y)
        cps = []
        for s, (a, row_sharded) in enumerate(halves):
            for k in range(3):
                to = 2 * chips[k][0] + chips[k][1]
                if row_sharded:
                    src = ro[s].at[0, to]
                else:
                    n = a.shape[2] // N_CHIPS
                    src = ro[s].at[0, :, pl.ds(to * n, n)]
                cps.append(_rdma(src, new[s].at[k], sems[0].at[s, k], sems[1].at[s, k], (*chips[k], c)))
        return cps

    def start(ro, io, new, sems):
        for cp in copies(ro, new, sems):
            cp.start()

    def finish(ro, io, new, sems):
        for cp in copies(ro, new, sems):
            cp.wait()

    landing = [jax.ShapeDtypeStruct((3, a.shape[2], a.shape[3]) if row_sharded else
                                    (3, a.shape[1], a.shape[2] // N_CHIPS), a.dtype) for a, row_sharded in halves]
    return Comm(ro=[a for a, _ in halves], new=landing, sems=[pltpu.SemaphoreType.DMA((len(halves), 3))] * 2,
                phases=[(0.0, start), (1.0, finish)])


def sibling_join_halves(shards, *, name):
    n_in = len(shards)

    def body(*refs):
        ins = refs[:n_in]
        outs = refs[n_in:2 * n_in]
        send_sem, recv_sem = refs[2 * n_in:]
        x, y, c = _position()
        sibling = (x, y, 1 - c)

        def rows(ref, h):
            R2 = ref.shape[1] // 2
            return ref.at[:, pl.ds(h * R2, R2), :]

        sends = [pltpu.make_async_remote_copy(src_ref=rows(ins[i], c), dst_ref=rows(outs[i], c),
                                              send_sem=send_sem.at[i], recv_sem=recv_sem.at[i],
                                              device_id=sibling, device_id_type=MESH) for i in range(n_in)]
        lands = [pltpu.make_async_remote_copy(src_ref=rows(ins[i], c), dst_ref=rows(outs[i], 1 - c),
                                              send_sem=send_sem.at[i], recv_sem=recv_sem.at[i],
                                              device_id=sibling, device_id_type=MESH) for i in range(n_in)]
        for cp in sends:
            cp.start()
        for i in range(n_in):
            sends[i].wait_send()
            lands[i].wait_recv()

    return pl.pallas_call(
        body, name=name, in_specs=[HBM_ANY] * n_in, out_specs=[HBM_ANY] * n_in,
        out_shape=[jax.ShapeDtypeStruct(a.shape, a.dtype) for a in shards],
        input_output_aliases={i: i for i in range(n_in)},
        scratch_shapes=[pltpu.SemaphoreType.DMA((n_in,)), pltpu.SemaphoreType.DMA((n_in,))],
        compiler_params=COMM_PARAMS,
    )(*shards)


def all_reduce_small(v, *, name):
    R, Ccols = v.shape

    def body(v_ref, o_ref, slots, send_sem, recv_sem):
        x, y, c = _position()
        me = 4 * x + 2 * y + c
        slots[me] = v_ref[...]

        def peer(d):
            return x ^ (d >> 2), y ^ ((d >> 1) & 1), c ^ (d & 1)

        def copy(d, slot):
            return pltpu.make_async_remote_copy(
                src_ref=v_ref, dst_ref=slots.at[slot], send_sem=send_sem.at[d], recv_sem=recv_sem.at[d],
                device_id=peer(d), device_id_type=MESH)

        sends = [copy(d, me) for d in range(1, N_DEV)]
        for cp in sends:
            cp.start()
        for d in range(1, N_DEV):
            px, py, pc = peer(d)
            copy(d, 4 * px + 2 * py + pc).wait_recv()
        for cp in sends:
            cp.wait_send()
        acc = slots[0]
        for d in range(1, N_DEV):
            acc = acc + slots[d]
        o_ref[...] = acc

    return pl.pallas_call(
        body, name=name,
        in_specs=[pl.BlockSpec(memory_space=pltpu.VMEM)], out_specs=pl.BlockSpec(memory_space=pltpu.VMEM),
        out_shape=jax.ShapeDtypeStruct((R, Ccols), F32),
        scratch_shapes=[pltpu.VMEM((N_DEV, R, Ccols), F32), pltpu.SemaphoreType.DMA((N_DEV,)),
                        pltpu.SemaphoreType.DMA((N_DEV,))],
        compiler_params=COMM_PARAMS,
    )(v)


def _pack_small(parts):
    flat = jnp.concatenate([p.reshape(-1).astype(F32) for p in parts])
    n = flat.shape[0]
    rows = -(-n // LANES)
    rows = -(-rows // SUBLANES) * SUBLANES
    return jnp.pad(flat, (0, rows * LANES - n)).reshape(rows, LANES)


def _unpack_small(packed, like):
    flat = packed.reshape(-1)
    out, off = [], 0
    for p in like:
        out.append(flat[off:off + p.size].reshape(p.shape))
        off += p.size
    return out


def kernel(x, norm_mix, norm_ffn, norm_final, attn_w_in, attn_w_out, attn_sinks, hgrn_w_in, hgrn_w_out, hgrn_norm, hgrn_lb_logits, ffn_w_up, ffn_conv_w, ffn_conv_b, ffn_w_down, loss_target, m_norm_mix, m_norm_ffn, m_norm_final, m_attn_w_in, m_attn_w_out, m_attn_sinks, m_hgrn_w_in, m_hgrn_w_out, m_hgrn_norm, m_hgrn_lb_logits, m_ffn_w_up, m_ffn_conv_w, m_ffn_conv_b, m_ffn_w_down, v_norm_mix, v_norm_ffn, v_norm_final, v_attn_w_in, v_attn_w_out, v_attn_sinks, v_hgrn_w_in, v_hgrn_w_out, v_hgrn_norm, v_hgrn_lb_logits, v_ffn_w_up, v_ffn_conv_w, v_ffn_conv_b, v_ffn_w_down):
    S, D = x.shape[1], x.shape[2]
    depth = norm_mix.shape[0]
    n_q = D // HEAD_DIM
    n_kv = n_q // Q_PER_KV
    Fh = ffn_w_down.shape[1] * N_CHIPS
    h = x.reshape(S, D)
    target = loss_target.reshape(S, D)

    col_names = ["attn_w_in", "hgrn_w_in", "ffn_w_up"]
    row_names = ["attn_w_out", "hgrn_w_out", "ffn_w_down"]
    x_, y_, c_ = _position()
    me = 2 * x_ + y_
    pos = jnp.stack([me, c_]).astype(jnp.int32)
    col_w = [cast_place(w, pos, row_sharded=False, name=f"cast_place_{n}")
             for n, w in zip(col_names, (attn_w_in, hgrn_w_in, ffn_w_up))]
    row_w = [cast_place(w, pos, row_sharded=True, name=f"cast_place_{n}")
             for n, w in zip(row_names, (attn_w_out, hgrn_w_out, ffn_w_down))]
    W = dict(zip(col_names + row_names, col_w + row_w))
    is_row = {n: n in row_names for n in W}

    def layer_weights(layer):
        mixer = "attn" if layer % 2 == 0 else "hgrn"
        return [(f"{mixer}_w_in", layer // 2), (f"{mixer}_w_out", layer // 2), ("ffn_w_up", layer), ("ffn_w_down", layer)]

    def gather_of(group):
        names = [n for n, _ in group]
        return names, gather_comm([W[n] for n in names], [(i, li, is_row[n]) for i, (n, li) in enumerate(group)])

    def mixer_weights(layer):
        return layer_weights(layer)[:2] if layer < depth else []

    names_next, comm_next = gather_of(mixer_weights(0))
    filled, _ = run_comm(comm_next, name="gather_mixer0")
    W.update(zip(names_next, filled))
    cw = gather_small(ffn_conv_w, name="gather_conv_w").reshape(depth, 3, 2, Fh).transpose(0, 2, 1, 3)
    cb = ffn_conv_b.reshape(depth, 2, 1, Fh)
    g_mix = norm_mix.reshape(depth, 1, D)
    g_ffn = norm_ffn.reshape(depth, 1, D)

    slopes = jnp.exp2(-8.0 * jnp.arange(1, n_q + 1, dtype=F32) / n_q).reshape(n_kv, Q_PER_KV)
    slope_col = jnp.repeat(slopes, WINDOW, axis=1).reshape(n_kv, Q_PER_KV * WINDOW, 1)

    def to_heads(t, nh):
        return t.reshape(S, nh, HEAD_DIM).transpose(1, 0, 2)

    def from_heads(t):
        return t.transpose(1, 0, 2).reshape(S, -1)

    saved = []
    for layer in range(depth):
        idx = layer // 2
        rec = {"h_mix": h}
        hn, rec["hn_t"] = rmsnorm_fwd(h, g_mix, layer, name=f"norm_mix_fwd{layer}")
        is_attn = layer % 2 == 0
        up_l, down_l = layer_weights(layer)[2:]
        on_mixer = [up_l] if is_attn else [up_l, down_l]
        on_up = [down_l] if is_attn else mixer_weights(layer + 1)
        on_act = mixer_weights(layer + 1) if is_attn else []
        names_next, comm_next = gather_of(on_mixer)
        if is_attn:
            proj = mm_nn(hn, W["attn_w_in"], idx, name=f"attn_in_fwd{layer}", out_dtype=BF16)
            q = to_heads(proj[:, :n_q * HEAD_DIM], n_q).reshape(n_kv, Q_PER_KV, S, HEAD_DIM)
            k = to_heads(proj[:, n_q * HEAD_DIM:(n_q + n_kv) * HEAD_DIM], n_kv)
            v = to_heads(proj[:, (n_q + n_kv) * HEAD_DIM:], n_kv)
            sink_col = jnp.repeat(attn_sinks[idx].reshape(n_kv, Q_PER_KV), WINDOW, axis=1)
            sink_col = sink_col.reshape(n_kv, Q_PER_KV * WINDOW, 1)
            o_heads, filled, _ = attn_fwd(q, k, v, sink_col, slope_col, name=f"attn_fwd{layer}", comm=comm_next)
            W.update(zip(names_next, filled))
            o = from_heads(o_heads.reshape(n_q, S, HEAD_DIM))
            rec.update(q=q, k=k, v=v, sink_col=sink_col, o=o)
            h = mm_nn(o, W["attn_w_out"], idx, name=f"attn_out_fwd{layer}", res=h)
        else:
            proj = mm_nn(hn, W["hgrn_w_in"], idx, name=f"hgrn_in_fwd{layer}")
            ng = hgrn_norm[idx].reshape(1, HG_EXPAND)
            o, states, filled, _ = hgrn_fwd(proj, hgrn_lb_logits, ng, layer, name=f"hgrn_fwd{layer}", comm=comm_next)
            W.update(zip(names_next, filled))
            rec.update(proj=proj, ng=ng, states=states, o=o)
            h = mm_nn(o, W["hgrn_w_out"], idx, name=f"hgrn_out_fwd{layer}", res=h)
        rec["h_ffn"] = h
        hn2, rec["hn2_t"] = rmsnorm_fwd(h, g_ffn, layer, name=f"norm_ffn_fwd{layer}")
        if on_up:
            names_next, comm_next = gather_of(on_up)
            u, filled = mm_nn(hn2, W["ffn_w_up"], layer, name=f"ffn_up_fwd{layer}", split_out=True, out_dtype=BF16,
                              comm=comm_next)
            W.update(zip(names_next, filled))
        else:
            u = mm_nn(hn2, W["ffn_w_up"], layer, name=f"ffn_up_fwd{layer}", split_out=True, out_dtype=BF16)
        if on_act:
            names_next, comm_next = gather_of(on_act)
            act, rec["act_t"], rec["c"], filled = convact_fwd(u, cw, cb, layer, name=f"ffn_act_fwd{layer}",
                                                              comm=comm_next)
            W.update(zip(names_next, filled))
        else:
            act, rec["act_t"], rec["c"] = convact_fwd(u, cw, cb, layer, name=f"ffn_act_fwd{layer}")
        rec["u"] = u
        h = mm_nn(act, W["ffn_w_down"], layer, name=f"ffn_down_fwd{layer}", res=h)
        saved.append(rec)

    loss_part, dh, d_norm_final = loss_head(h, norm_final.reshape(1, D), target, name="loss_head")

    n_attn, n_hgrn = attn_w_in.shape[0], hgrn_w_in.shape[0]
    reduced_parts = []
    pending = []

    def swap_of(group, g):
        ordered = sorted(group, key=lambda nl: is_row[nl[0]])
        return ordered, swap_comm([(g[n], is_row[n]) for n, _ in ordered])

    def pair_up(tag, ordered, g, theirs):
        pair, pair_bf16 = {}, {}
        for (n, _), t in zip(ordered, theirs):
            gn = g[n]
            if is_row[n]:
                r = gn.shape[1] // N_CHIPS
                gn = gn.reshape(1, N_CHIPS, 2, r // 2, gn.shape[2])
            pair[n], pair_bf16[n] = pair_add(gn, t, pos, row_sharded=is_row[n], name=f"rs_pair_add_{n}{tag}")
        pending.append((ordered, pair, scatter_comm([(pair_bf16[n], is_row[n]) for n, _ in ordered])))

    def landed_behind(created, cut):
        for (ordered, pair, _), landed in zip(pending, cut(created)):
            reduced_parts.append((ordered, pair, landed))
        pending.clear()

    d_norm_mix, d_norm_ffn = [None] * depth, [None] * depth
    d_conv_w, d_conv_b = [None] * depth, [None] * depth
    d_sinks, d_hgrn_norm = [None] * n_attn, [None] * n_hgrn
    d_logits = jnp.zeros_like(hgrn_lb_logits)
    for layer in reversed(range(depth)):
        idx = layer // 2
        rec = saved[layer]
        g = {"ffn_w_down": mm_dw(rec["act_t"], dh, name=f"ffn_down_dw{layer}")}
        dact = mm_nt(dh, W["ffn_w_down"], layer, name=f"ffn_down_dx{layer}")
        du, dcw, dcb = convact_bwd(rec["c"], rec["u"], cw, layer, dact, name=f"ffn_act_bwd{layer}")
        d_conv_w[layer] = dcw.transpose(1, 0, 2).reshape(3, 2 * Fh)
        d_conv_b[layer] = dcb.reshape(2 * Fh)
        g["ffn_w_up"] = mm_dw(rec["hn2_t"], du, name=f"ffn_up_dw{layer}", split_b=True)
        ordered, swap = swap_of(layer_weights(layer)[2:], g)
        dh, dg, theirs = mm_nt(du, W["ffn_w_up"], layer, name=f"ffn_up_dx{layer}", split_in=True, comm=swap,
                               norm=(rec["h_ffn"], g_ffn, layer, dh))
        pair_up(layer, ordered, g, theirs)
        d_norm_ffn[layer] = dg.reshape(D)
        carried, cut = merge_comms([p[2] for p in pending])
        if layer % 2 == 0:
            g["attn_w_out"] = mm_tn(rec["o"], dh, name=f"attn_out_dw{layer}")
            do = mm_nt(dh, W["attn_w_out"], idx, name=f"attn_out_dx{layer}", out_dtype=BF16)
            do_heads = to_heads(do, n_q).reshape(n_kv, Q_PER_KV, S, HEAD_DIM)
            res = attn_bwd(rec["q"], rec["k"], rec["v"], rec["sink_col"], slope_col, do_heads,
                           name=f"attn_bwd{layer}", comm=carried)
            dq, dk, dv, dsink = res[:4]
            landed_behind(res[-1], cut)
            dproj = jnp.concatenate([from_heads(dq.reshape(n_q, S, HEAD_DIM)), from_heads(dk), from_heads(dv)], axis=1)
            d_sinks[idx] = jnp.sum(dsink.reshape(n_kv, Q_PER_KV, WINDOW), axis=-1).reshape(n_q)
            g["attn_w_in"] = mm_dw(rec["hn_t"], dproj, name=f"attn_in_dw{layer}")
            ordered, swap = swap_of(layer_weights(layer)[:2], g)
            dh, dg, theirs = mm_nt(dproj, W["attn_w_in"], idx, name=f"attn_in_dx{layer}", comm=swap,
                                   norm=(rec["h_mix"], g_mix, layer, dh))
        else:
            g["hgrn_w_out"] = mm_tn(rec["o"], dh, name=f"hgrn_out_dw{layer}")
            do = mm_nt(dh, W["hgrn_w_out"], idx, name=f"hgrn_out_dx{layer}")
            res = hgrn_bwd(rec["proj"], hgrn_lb_logits, rec["ng"], rec["states"], do, layer,
                           name=f"hgrn_bwd{layer}", comm=carried)
            dproj, dlg, dng = res[:3]
            landed_behind(res[-1], cut)
            d_logits = d_logits + dlg
            d_hgrn_norm[idx] = dng.reshape(HG_EXPAND)
            g["hgrn_w_in"] = mm_dw(rec["hn_t"], dproj, name=f"hgrn_in_dw{layer}")
            ordered, swap = swap_of(layer_weights(layer)[:2], g)
            dh, dg, theirs = mm_nt(dproj, W["hgrn_w_in"], idx, name=f"hgrn_in_dx{layer}", comm=swap,
                                   norm=(rec["h_mix"], g_mix, layer, dh))
        pair_up(layer, ordered, g, theirs)
        d_norm_mix[layer] = dg.reshape(D)
    grad_x = dh.reshape(x.shape)
    carried, cut = merge_comms([p[2] for p in pending])
    landed_behind(run_comm(carried, name="rs_scatter_last")[1], cut)

    small_w = [norm_mix, norm_ffn, norm_final, attn_sinks, hgrn_norm, hgrn_lb_logits, ffn_conv_b]
    small_m = [m_norm_mix, m_norm_ffn, m_norm_final, m_attn_sinks, m_hgrn_norm, m_hgrn_lb_logits, m_ffn_conv_b]
    small_v = [v_norm_mix, v_norm_ffn, v_norm_final, v_attn_sinks, v_hgrn_norm, v_hgrn_lb_logits, v_ffn_conv_b]
    small_grads = [jnp.stack(d_norm_mix), jnp.stack(d_norm_ffn), d_norm_final.reshape(D), jnp.stack(d_sinks),
                   jnp.stack(d_hgrn_norm), d_logits, jnp.stack(d_conv_b)]
    reduced = all_reduce_small(_pack_small(small_grads + [loss_part]), name="all_reduce_small")
    small_g = _unpack_small(reduced, small_w + [loss_part])
    loss = small_g.pop().reshape(())
    pad = [jnp.zeros((1, 1), F32)]
    sd, sm_, sv_ = adamw(_pack_small(small_w + pad), reduced, _pack_small(small_m + pad),
                         _pack_small(small_v + pad), name="adamw_small")
    small_d = _unpack_small(sd, small_w)
    small_nm = _unpack_small(sm_, small_w)
    small_nv = _unpack_small(sv_, small_w)

    big_names = col_names + row_names
    big_w = dict(attn_w_in=attn_w_in, hgrn_w_in=hgrn_w_in, ffn_w_up=ffn_w_up, attn_w_out=attn_w_out,
                 hgrn_w_out=hgrn_w_out, ffn_w_down=ffn_w_down)
    half_shard = dict.fromkeys(big_names)
    for ordered, pair, landed in reduced_parts:
        for (n, li), ld in zip(ordered, landed):
            half_shard[n] = chip_add(pair[n], ld, pos, half_shard[n], li, big_w[n].shape[0], row_sharded=is_row[n],
                                     name=f"rs_chip_add_{n}{li}")
    shards = sibling_join_halves([half_shard[n].reshape(big_w[n].shape) for n in big_names], name="rs_sibling_join")
    big_m = dict(attn_w_in=m_attn_w_in, hgrn_w_in=m_hgrn_w_in, ffn_w_up=m_ffn_w_up, attn_w_out=m_attn_w_out,
                 hgrn_w_out=m_hgrn_w_out, ffn_w_down=m_ffn_w_down)
    big_v = dict(attn_w_in=v_attn_w_in, hgrn_w_in=v_hgrn_w_in, ffn_w_up=v_ffn_w_up, attn_w_out=v_attn_w_out,
                 hgrn_w_out=v_hgrn_w_out, ffn_w_down=v_ffn_w_down)
    grads, deltas, new_m, new_v = {}, {}, {}, {}
    for name_, gshard in zip(big_names, shards):
        w_ = big_w[name_]
        cols = w_.shape[-1]
        d_, m_, v_ = adamw(w_.reshape(-1, cols), gshard.reshape(-1, cols), big_m[name_].reshape(-1, cols),
                           big_v[name_].reshape(-1, cols), name=f"adamw_{name_}")
        grads[name_] = gshard.reshape(w_.shape)
        deltas[name_], new_m[name_], new_v[name_] = (t.reshape(w_.shape) for t in (d_, m_, v_))

    gconv = jnp.stack(d_conv_w)
    n_conv = ffn_conv_w.shape[2]
    gconv_all = _unpack_small(all_reduce_small(_pack_small([gconv]), name="all_reduce_conv_w"), [gconv])[0]
    gconv_mine = lax.dynamic_slice_in_dim(gconv_all, me * n_conv, n_conv, axis=2)
    cshape = ffn_conv_w.shape
    d_, m_, v_ = adamw(ffn_conv_w.reshape(-1, n_conv), gconv_mine.reshape(-1, n_conv),
                       m_ffn_conv_w.reshape(-1, n_conv), v_ffn_conv_w.reshape(-1, n_conv), name="adamw_ffn_conv_w")
    grads["ffn_conv_w"] = gconv_mine
    deltas["ffn_conv_w"], new_m["ffn_conv_w"], new_v["ffn_conv_w"] = (t.reshape(cshape) for t in (d_, m_, v_))

    order = ["norm_mix", "norm_ffn", "norm_final", "attn_w_in", "attn_w_out", "attn_sinks", "hgrn_w_in",
             "hgrn_w_out", "hgrn_norm", "hgrn_lb_logits", "ffn_w_up", "ffn_conv_w", "ffn_conv_b", "ffn_w_down"]
    small_names = ["norm_mix", "norm_ffn", "norm_final", "attn_sinks", "hgrn_norm", "hgrn_lb_logits", "ffn_conv_b"]
    for i, name_ in enumerate(small_names):
        grads[name_], deltas[name_], new_m[name_], new_v[name_] = small_g[i], small_d[i], small_nm[i], small_nv[i]
    return (loss, grad_x, *[grads[n] for n in order], *[deltas[n] for n in order],
            *[new_m[n] for n in order], *[new_v[n] for n in order])
```

```python
import functools
import math

import numpy as np
import jax
import jax.numpy as jnp
from jax import lax
from jax.experimental import pallas as pl
from jax.experimental.pallas import tpu as pltpu

F32 = jnp.float32
BF16 = jnp.bfloat16
MESH = pl.DeviceIdType.MESH

HEAD_DIM = 64
Q_PER_KV = 4
WINDOW = 128
HG_EXPAND = 128
HG_CHUNK = 128
EPS = 1e-6
N_CHIPS = 4
N_DEV = 8

ADAM_LR = 0.001
ADAM_B1 = 0.9
ADAM_B2 = 0.999
ADAM_EPS = 1e-08
ADAM_WD = 0.01
ADAM_STEP = 10

VMEM_LIMIT_BYTES = 48 * 1024 * 1024
LANES = 128
SUBLANES = 8
HALO_ROWS = 16


def _params(sem=None):
    return pltpu.CompilerParams(dimension_semantics=sem, vmem_limit_bytes=VMEM_LIMIT_BYTES)


def _pick(n, cands):
    for c in cands:
        if n % c == 0:
            return c
    return n


def _dot(a, b, dims):
    return lax.dot_general(a.astype(BF16), b.astype(BF16), (dims, ((), ())),
                           preferred_element_type=F32)


NN = ((1,), (0,))
NT = ((1,), (1,))
TN = ((0,), (0,))


def mm_nn(a, w, l, *, name, res=None, out_dtype=F32, split_out=False, comm=None):
    M, K = a.shape
    N = w.shape[2]
    tm = _pick(M, (1024, 512, 256, 128))
    tn = _pick(N // 2 if split_out else N, (1408, 1024, 768, 512, 256, 128))
    tk = _pick(K, (1024, 1408, 512, 256, 128))
    nk = K // tk
    nh = (N // 2) // tn

    def body(*refs):
        if res is None:
            a_ref, w_ref, o_ref = refs[:3]
            r_ref = None
        else:
            a_ref, w_ref, r_ref, o_ref = refs[:4]
        part = _dot(a_ref[...], w_ref[...], NN)

        def finish(acc):
            if r_ref is not None:
                acc = acc + r_ref[...]
            o_ref[...] = acc.astype(out_dtype)

        if nk == 1:
            finish(part)
        else:
            acc_ref = refs[-1]
            k = pl.program_id(2)

            @pl.when(k == 0)
            def _():
                acc_ref[...] = part

            @pl.when(k > 0)
            def _():
                acc_ref[...] += part

            @pl.when(k == nk - 1)
            def _():
                finish(acc_ref[...])

    in_specs = [pl.BlockSpec((tm, tk), lambda n, m, k: (m, k)),
                pl.BlockSpec((None, tk, tn), lambda n, m, k: (l, k, n))]
    args = [a, w]
    if res is not None:
        in_specs.append(pl.BlockSpec((tm, tn), lambda n, m, k: (m, n)))
        args.append(res)
    if split_out:
        out_shape = jax.ShapeDtypeStruct((2, M, N // 2), out_dtype)
        out_spec = pl.BlockSpec((None, tm, tn), lambda n, m, k: (n // nh, m, n % nh))
    else:
        out_shape = jax.ShapeDtypeStruct((M, N), out_dtype)
        out_spec = pl.BlockSpec((tm, tn), lambda n, m, k: (m, n))
    outs, updated, _ = _carried_call(
        body, args, name=name, grid=(N // tn, M // tm, nk), in_specs=in_specs, out_specs=[out_spec],
        out_shape=[out_shape], scratch_shapes=[] if nk == 1 else [pltpu.VMEM((tm, tn), F32)], comm=comm)
    return outs[0] if comm is None else (outs[0], updated)


def mm_nt(a, w, l, *, name, split_in=False, out_dtype=F32, comm=None, norm=None):
    if split_in:
        _, M, Nh = a.shape
        N = 2 * Nh
    else:
        M, N = a.shape
        Nh = N
    K = w.shape[1]
    tm = _pick(M, (512, 256, 128) if norm is not None else (1024, 512, 256, 128))
    tko = _pick(K, (1024, 1408, 512, 256, 128))
    tr = _pick(Nh, (1024, 1408, 768, 512, 256, 128))
    nr = N // tr
    nh = Nh // tr
    n_in, n_out = (5, 2) if norm is not None else (2, 1)
    assert norm is None or tko == K

    def body(*refs):
        a_ref, w_ref = refs[:2]
        o_ref = refs[n_in]
        part = _dot(a_ref[...], w_ref[...], NT)

        def finish(acc):
            if norm is None:
                o_ref[...] = acc.astype(out_dtype)
                return
            h_ref, g_ref, dres_ref = refs[2:5]
            dg_ref = refs[n_in + 1]
            _, vjp = jax.vjp(_rms, h_ref[...], g_ref[...])
            dx, dg = vjp(acc)
            o_ref[...] = dres_ref[...] + dx
            first = pl.program_id(1) == 0

            @pl.when(first)
            def _():
                dg_ref[...] = dg

            @pl.when(jnp.logical_not(first))
            def _():
                dg_ref[...] += dg

        if nr == 1:
            finish(part)
        else:
            acc_ref = refs[n_in + n_out]
            r = pl.program_id(2)

            @pl.when(r == 0)
            def _():
                acc_ref[...] = part

            @pl.when(r > 0)
            def _():
                acc_ref[...] += part

            @pl.when(r == nr - 1)
            def _():
                finish(acc_ref[...])

    if split_in:
        a_spec = pl.BlockSpec((None, tm, tr), lambda ko, m, r: (r // nh, m, r % nh))
    else:
        a_spec = pl.BlockSpec((tm, tr), lambda ko, m, r: (m, r))
    args = [a, w]
    in_specs = [a_spec, pl.BlockSpec((None, tko, tr), lambda ko, m, r: (l, ko, r))]
    out_specs = [pl.BlockSpec((tm, tko), lambda ko, m, r: (m, ko))]
    out_shape = [jax.ShapeDtypeStruct((M, K), out_dtype)]
    if norm is not None:
        h, g, l_norm, dres = norm
        row = pl.BlockSpec((tm, K), lambda ko, m, r: (m, 0))
        args += [h, g, dres]
        in_specs += [row, pl.BlockSpec((None, 1, K), lambda ko, m, r: (l_norm, 0, 0)), row]
        out_specs.append(pl.BlockSpec((1, K), lambda ko, m, r: (0, 0)))
        out_shape.append(jax.ShapeDtypeStruct((1, K), F32))
    outs, _, created = _carried_call(
        body, args, name=name, grid=(K // tko, M // tm, nr), in_specs=in_specs, out_specs=out_specs,
        out_shape=out_shape, scratch_shapes=[] if nr == 1 else [pltpu.VMEM((tm, tko), F32)], comm=comm)
    outs = list(outs) + ([] if comm is None else [created])
    return outs[0] if len(outs) == 1 else tuple(outs)


def mm_tn(a, b, *, name):
    M, K = a.shape
    N = b.shape[1]
    tm = _pick(M, (512, 256, 128))
    tko = _pick(K, (1024, 1408, 704, 512, 256, 128))
    tn = _pick(N, (512, 256, 128))
    nm = M // tm

    def body(a_ref, b_ref, o_ref):
        part = _dot(a_ref[...], b_ref[...], TN)
        r = pl.program_id(2)

        @pl.when(r == 0)
        def _():
            o_ref[...] = part

        @pl.when(r > 0)
        def _():
            o_ref[...] += part

    return pl.pallas_call(
        body, name=name, grid=(K // tko, N // tn, nm),
        in_specs=[pl.BlockSpec((tm, tko), lambda ko, n, r: (r, ko)), pl.BlockSpec((tm, tn), lambda ko, n, r: (r, n))],
        out_specs=pl.BlockSpec((None, tko, tn), lambda ko, n, r: (0, ko, n)),
        out_shape=jax.ShapeDtypeStruct((1, K, N), F32),
        compiler_params=_params(("parallel", "parallel", "arbitrary")),
    )(a, b)


def mm_dw(at, b, *, name, split_b=False):
    K, M = at.shape
    if split_b:
        Nh = b.shape[2]
        N = 2 * Nh
    else:
        N = b.shape[1]
        Nh = N
    tm = _pick(M, (2048, 1024, 512, 256, 128) if b.dtype.itemsize == 2 else (1024, 512, 256, 128))
    tko = _pick(K, (1024, 1408, 704, 512, 256, 128))
    tn = _pick(Nh, (1408, 1024, 768, 512, 256, 128))
    nm = M // tm
    nh = Nh // tn

    def body(a_ref, b_ref, o_ref):
        part = _dot(a_ref[...], b_ref[...], NN)
        r = pl.program_id(2)

        @pl.when(r == 0)
        def _():
            o_ref[...] = part

        @pl.when(r > 0)
        def _():
            o_ref[...] += part

    if split_b:
        b_spec = pl.BlockSpec((None, tm, tn), lambda ko, n, r: (n // nh, r, n % nh))
    else:
        b_spec = pl.BlockSpec((tm, tn), lambda ko, n, r: (r, n))
    return pl.pallas_call(
        body, name=name, grid=(K // tko, N // tn, nm),
        in_specs=[pl.BlockSpec((tko, tm), lambda ko, n, r: (ko, r)), b_spec],
        out_specs=pl.BlockSpec((None, tko, tn), lambda ko, n, r: (0, ko, n)),
        out_shape=jax.ShapeDtypeStruct((1, K, N), F32),
        compiler_params=_params(("parallel", "parallel", "arbitrary")),
    )(at, b)


def _rms(x, g):
    return x * lax.rsqrt(jnp.mean(x * x, axis=-1, keepdims=True) + EPS) * g


def rmsnorm_fwd(h, g, l, *, name):
    S, D = h.shape
    tm = _pick(S, (512, 256, 128))

    def body(h_ref, g_ref, o_ref, ot_ref):
        y = _rms(h_ref[...], g_ref[...])
        o_ref[...] = y.astype(BF16)
        ot_ref[...] = y.T.astype(BF16)

    return pl.pallas_call(
        body, name=name, grid=(S // tm,),
        in_specs=[pl.BlockSpec((tm, D), lambda m: (m, 0)),
                  pl.BlockSpec((None, 1, D), lambda m: (l, 0, 0))],
        out_specs=[pl.BlockSpec((tm, D), lambda m: (m, 0)), pl.BlockSpec((D, tm), lambda m: (0, m))],
        out_shape=[jax.ShapeDtypeStruct((S, D), BF16), jax.ShapeDtypeStruct((D, S), BF16)],
        compiler_params=_params(("parallel",)),
    )(h, g)


def loss_head(h, g, target, *, name):
    S, D = h.shape
    tm = _pick(S, (512, 256, 128))

    def body(h_ref, g_ref, t_ref, loss_ref, dh_ref, dg_ref):
        y, vjp = jax.vjp(_rms, h_ref[...], g_ref[...])
        err = y - t_ref[...]
        part = 0.5 * jnp.sum(jnp.sum(err * err, axis=-1, keepdims=True) / D, axis=0, keepdims=True)
        dx, dg = vjp(err / D)
        dh_ref[...] = dx

        @pl.when(pl.program_id(0) == 0)
        def _():
            dg_ref[...] = dg
            loss_ref[...] = part

        @pl.when(pl.program_id(0) > 0)
        def _():
            dg_ref[...] += dg
            loss_ref[...] += part

    row = pl.BlockSpec((tm, D), lambda m: (m, 0))
    return pl.pallas_call(
        body, name=name, grid=(S // tm,),
        in_specs=[row, pl.BlockSpec((1, D), lambda m: (0, 0)), row],
        out_specs=[pl.BlockSpec((1, 1), lambda m: (0, 0)), row, pl.BlockSpec((1, D), lambda m: (0, 0))],
        out_shape=[jax.ShapeDtypeStruct((1, 1), F32), jax.ShapeDtypeStruct((S, D), F32),
                   jax.ShapeDtypeStruct((1, D), F32)],
        compiler_params=_params(("arbitrary",)),
    )(h, g, target)


def _make_mm(dims_fwd, dims_da, dims_db, swap_da=False, swap_db=False):
    @jax.custom_vjp
    def mm(a, b):
        return _dot(a, b, dims_fwd)

    def fwd(a, b):
        return mm(a, b), (a, b)

    def bwd(resid, g):
        a, b = resid
        da = _dot(b, g, dims_da) if swap_da else _dot(g, b, dims_da)
        db = _dot(g, a, dims_db) if swap_db else _dot(a, g, dims_db)
        return da, db

    mm.defvjp(fwd, bwd)
    return mm


_mm = _make_mm(NN, NT, TN)
_mm_nt = _make_mm(NT, NN, TN, swap_db=True)
_mm_tn = _make_mm(TN, NT, NN, swap_da=True)


def _dot_hi(a, b):
    return jnp.dot(a, b, precision=lax.Precision.HIGHEST, preferred_element_type=F32)


def _sigmoid(x):
    return 0.5 * jnp.tanh(0.5 * x) + 0.5


def _silu(x):
    return x * _sigmoid(x)


def _row_slices(x, n):
    rows = x.shape[0] // n
    return tuple(x[i * rows:(i + 1) * rows] for i in range(n))


@functools.partial(jax.custom_vjp, nondiff_argnums=(1,))
def _split_rows(x, n):
    return _row_slices(x, n)


_split_rows.defvjp(lambda x, n: (_row_slices(x, n), None), lambda n, _, cts: (jnp.concatenate(cts, axis=0),))


def _attn_block(qs, kws, vws, sink, slope, first):
    s = jnp.concatenate([_mm_nt(q, kw) for q, kw in zip(qs, kws)], axis=0) * (HEAD_DIM ** -0.5)
    qi = lax.broadcasted_iota(jnp.int32, s.shape, 0) % WINDOW
    ki = lax.broadcasted_iota(jnp.int32, s.shape, 1)
    dist = qi + WINDOW - ki
    valid = (dist >= 0) & (dist < WINDOW) & (ki >= WINDOW * first)
    s = jnp.where(valid, s - slope * dist.astype(F32), -jnp.inf)
    m = lax.stop_gradient(jnp.maximum(jnp.max(s, axis=-1, keepdims=True), sink))
    e = jnp.exp(s - m)
    denom = jnp.sum(e, axis=-1, keepdims=True) + jnp.exp(sink - m)
    ps = _split_rows(e * (1.0 / denom), len(qs))
    return tuple(_mm(p, vw) for p, vw in zip(ps, vws))


def _attn_specs(Hkv, order):
    G = Q_PER_KV
    qspec = pl.BlockSpec((Hkv, G, WINDOW, HEAD_DIM), lambda n: (0, 0, order(n), 0))
    prev = pl.BlockSpec((Hkv, WINDOW, HEAD_DIM), lambda n: (0, jnp.maximum(order(n) - 1, 0), 0))
    cur = pl.BlockSpec((Hkv, WINDOW, HEAD_DIM), lambda n: (0, order(n), 0))
    col = pl.BlockSpec((Hkv, G * WINDOW, 1), lambda n: (0, 0, 0))
    return qspec, prev, cur, col


def attn_fwd(q, k, v, sink_col, slope_col, *, name, comm=None):
    Hkv, G, S, _ = q.shape
    qspec, prev, cur, col = _attn_specs(Hkv, lambda n: n)

    def body(q_ref, kp_ref, kc_ref, vp_ref, vc_ref, sink_ref, slope_ref, o_ref):
        first = (pl.program_id(0) == 0).astype(jnp.int32)
        qs = tuple(q_ref[h].reshape(G * WINDOW, HEAD_DIM) for h in range(Hkv))
        kws = tuple(jnp.concatenate([kp_ref[h], kc_ref[h]], axis=0) for h in range(Hkv))
        vws = tuple(jnp.concatenate([vp_ref[h], vc_ref[h]], axis=0) for h in range(Hkv))
        outs = _attn_block(qs, kws, vws, sink_ref[...].reshape(Hkv * G * WINDOW, 1),
                           slope_ref[...].reshape(Hkv * G * WINDOW, 1), first)
        o_ref[...] = jnp.stack([o.reshape(G, WINDOW, HEAD_DIM).astype(BF16) for o in outs])

    outs, updated, created = _carried_call(
        body, [q, k, k, v, v, sink_col, slope_col], name=name, grid=(S // WINDOW,),
        in_specs=[qspec, prev, cur, prev, cur, col, col], out_specs=[qspec],
        out_shape=[jax.ShapeDtypeStruct(q.shape, BF16)], comm=comm)
    return outs[0] if comm is None else (outs[0], updated, created)


def attn_bwd(q, k, v, sink_col, slope_col, do, *, name, comm=None):
    Hkv, G, S, _ = q.shape
    nb = S // WINDOW
    qspec, prev, cur, col = _attn_specs(Hkv, lambda n: nb - 1 - n)

    def body(q_ref, kp_ref, kc_ref, vp_ref, vc_ref, sink_ref, slope_ref, do_ref,
             dq_ref, dk_ref, dv_ref, dsink_ref, dk_carry, dv_carry):
        @pl.when(pl.program_id(0) == 0)
        def _():
            dk_carry[...] = jnp.zeros_like(dk_carry)
            dv_carry[...] = jnp.zeros_like(dv_carry)
            dsink_ref[...] = jnp.zeros_like(dsink_ref)

        first = (pl.program_id(0) == nb - 1).astype(jnp.int32)
        qs = tuple(q_ref[h].reshape(G * WINDOW, HEAD_DIM).astype(F32) for h in range(Hkv))
        kws = tuple(jnp.concatenate([kp_ref[h], kc_ref[h]], axis=0).astype(F32) for h in range(Hkv))
        vws = tuple(jnp.concatenate([vp_ref[h], vc_ref[h]], axis=0).astype(F32) for h in range(Hkv))
        fn = functools.partial(_attn_block, slope=slope_ref[...].reshape(Hkv * G * WINDOW, 1), first=first)
        _, vjp = jax.vjp(fn, qs, kws, vws, sink_ref[...].reshape(Hkv * G * WINDOW, 1))
        dqs, dkws, dvws, dsink = vjp(tuple(do_ref[h].reshape(G * WINDOW, HEAD_DIM).astype(F32) for h in range(Hkv)))
        dkw, dvw = jnp.stack(dkws), jnp.stack(dvws)
        dq_ref[...] = jnp.stack([dq.reshape(G, WINDOW, HEAD_DIM) for dq in dqs]).astype(BF16)
        dk_ref[...] = (dkw[:, WINDOW:] + dk_carry[...]).astype(BF16)
        dv_ref[...] = (dvw[:, WINDOW:] + dv_carry[...]).astype(BF16)
        dk_carry[...] = dkw[:, :WINDOW]
        dv_carry[...] = dvw[:, :WINDOW]
        dsink_ref[...] += dsink.reshape(Hkv, G * WINDOW, 1)

    kv_shape = jax.ShapeDtypeStruct(k.shape, BF16)
    outs, _, created = _carried_call(
        body, [q, k, k, v, v, sink_col, slope_col, do], name=name, grid=(nb,),
        in_specs=[qspec, prev, cur, prev, cur, col, col, qspec],
        out_specs=[qspec, cur, cur, col],
        out_shape=[jax.ShapeDtypeStruct(q.shape, BF16), kv_shape, kv_shape,
                   jax.ShapeDtypeStruct(sink_col.shape, F32)],
        scratch_shapes=[pltpu.VMEM((Hkv, WINDOW, HEAD_DIM), F32), pltpu.VMEM((Hkv, WINDOW, HEAD_DIM), F32)],
        comm=comm)
    return outs if comm is None else (*outs, created)


def _hgrn_consts():
    C = HG_CHUNK
    t = np.arange(C)[:, None]
    j = np.arange(C)[None, :]
    cq, ck, mk = [], [], []
    H = C // 2
    while H >= 1:
        start = (t // H) * H
        cq.append((j > start) & (j <= t))
        ck.append((j > t) & (j <= start + H))
        mk.append(((t // H) % 2 == 1) & ((j // H) == (t // H) - 1))
        H //= 2
    ltri = (j <= t)
    ops = np.concatenate([ltri] + cq + ck, axis=0).astype(np.float32)
    ops3 = np.concatenate([ops, ops, ops], axis=1)
    opst3 = np.concatenate([ops.T, ops.T, ops.T], axis=1)
    return (jnp.asarray(ops3, BF16), jnp.asarray(opst3, BF16), jnp.asarray(np.stack(mk).astype(np.float32)))


def _split3(x):
    hi = x.astype(BF16)
    r = x - hi.astype(F32)
    mid = r.astype(BF16)
    lo = (r - mid.astype(F32)).astype(BF16)
    return jnp.concatenate([hi, mid, lo], axis=0)


@jax.custom_vjp
def _seg_sums(g, ops3, opst3):
    C = g.shape[0]
    out = jnp.dot(ops3, _split3(g), preferred_element_type=F32)
    return tuple(out[i * C:(i + 1) * C] for i in range(ops3.shape[0] // C))


def _seg_sums_fwd(g, ops3, opst3):
    return _seg_sums(g, ops3, opst3), opst3


def _seg_sums_bwd(opst3, cts):
    ct = jnp.concatenate(cts, axis=0)
    return jnp.dot(opst3, _split3(ct), preferred_element_type=F32), None, None


_seg_sums.defvjp(_seg_sums_fwd, _seg_sums_bwd)


def _head_slices(x):
    return tuple(x[:, h * HG_EXPAND:(h + 1) * HG_EXPAND] for h in range(x.shape[1] // HG_EXPAND))


@jax.custom_vjp
def _split_heads(x):
    return _head_slices(x)


_split_heads.defvjp(lambda x: (_head_slices(x), None), lambda _, cts: (jnp.concatenate(cts, axis=1),))


@jax.custom_vjp
def _join_heads(xs):
    return jnp.concatenate(xs, axis=1)


_join_heads.defvjp(lambda xs: (jnp.concatenate(xs, axis=1), None), lambda _, ct: (_head_slices(ct),))


def _hgrn_chunk(qr, fr, iv, gr, sts, logits, ng, ops3, opst3, mk, layer):
    n_levels = mk.shape[0]
    depth, D = logits.shape
    e = jnp.exp(logits - lax.stop_gradient(jnp.max(logits, axis=0, keepdims=True)))
    sm = e / jnp.sum(e, axis=0, keepdims=True)
    row = lax.broadcasted_iota(jnp.int32, (depth, D), 0)
    lb = jnp.sum(jnp.where((row >= 1) & (row <= layer), sm, 0.0), axis=0, keepdims=True)

    q = _silu(qr)
    a = jnp.log(lb)
    c = jnp.log(1.0 - lb) - (jnp.maximum(-fr, 0.0) + jnp.log(1.0 + jnp.exp(-jnp.abs(fr))))
    mx = lax.stop_gradient(jnp.maximum(a, c))
    g = mx + jnp.log(jnp.exp(a - mx) + jnp.exp(c - mx))
    k = (1.0 - lb) * _sigmoid(-fr)

    sums = _seg_sums(g, ops3, opst3)
    b = sums[0]
    b_last = jnp.sum(g, axis=0, keepdims=True)

    ql = [_split_heads(q * jnp.exp(sums[1 + lv])) for lv in range(n_levels)]
    kl = [_split_heads(k * jnp.exp(sums[1 + n_levels + lv])) for lv in range(n_levels)]
    qb, kb = _split_heads(q * jnp.exp(b)), _split_heads(k * jnp.exp(b_last - b))
    qk, ivh, gate, decay = _split_heads(q * k), _split_heads(iv), _split_heads(_silu(gr)), _split_heads(jnp.exp(b_last))
    outs, new_sts = [], []
    for h, st in enumerate(sts):
        amat = None
        for lv in range(n_levels):
            part = _mm_nt(ql[lv][h], kl[lv][h]) * mk[lv]
            amat = part if amat is None else amat + part
        o = jnp.sum(qk[h], axis=-1, keepdims=True) * ivh[h] + _mm(amat, ivh[h]) + _mm_nt(qb[h], st)
        new_sts.append(st * decay[h] + _mm_tn(ivh[h], kb[h]))
        on = o * lax.rsqrt(jnp.mean(o * o, axis=-1, keepdims=True) + EPS) * ng
        outs.append(on * gate[h])
    return _join_heads(tuple(outs)), tuple(new_sts)


def _hgrn_in_specs(D, depth, order):
    C = HG_CHUNK
    nl = int(math.log2(C))
    n_ops = 1 + 2 * nl
    return [pl.BlockSpec((C, 4 * D), lambda c: (order(c), 0)),
            pl.BlockSpec((depth, D), lambda c: (0, 0)),
            pl.BlockSpec((1, HG_EXPAND), lambda c: (0, 0)),
            pl.BlockSpec((n_ops * C, 3 * C), lambda c: (0, 0)),
            pl.BlockSpec((C, 3 * n_ops * C), lambda c: (0, 0)),
            pl.BlockSpec((nl, C, C), lambda c: (0, 0, 0))]


def hgrn_fwd(proj, logits, ng, layer, *, name, comm=None):
    S = proj.shape[0]
    D = proj.shape[1] // 4
    H = D // HG_EXPAND
    C = HG_CHUNK
    nc = S // C
    consts = _hgrn_consts()

    def body(p_ref, lg_ref, ng_ref, ops3_ref, opst3_ref, mk_ref, o_ref, st_out_ref, st_ref):
        @pl.when(pl.program_id(0) == 0)
        def _():
            st_ref[...] = jnp.zeros_like(st_ref)

        st_out_ref[...] = st_ref[...]
        out, new_sts = _hgrn_chunk(*(p_ref[:, j * D:(j + 1) * D] for j in range(4)), tuple(st_ref[h] for h in range(H)),
                                   lg_ref[...], ng_ref[...], ops3_ref[...], opst3_ref[...], mk_ref[...], layer)
        o_ref[...] = out.astype(BF16)
        for h in range(H):
            st_ref[h] = new_sts[h]

    outs, updated, created = _carried_call(
        body, [proj, logits, ng, *consts], name=name, grid=(nc,),
        in_specs=_hgrn_in_specs(D, logits.shape[0], lambda c: c),
        out_specs=[pl.BlockSpec((C, D), lambda c: (c, 0)),
                   pl.BlockSpec((None, H, HG_EXPAND, HG_EXPAND), lambda c: (c, 0, 0, 0))],
        out_shape=[jax.ShapeDtypeStruct((S, D), BF16),
                   jax.ShapeDtypeStruct((nc, H, HG_EXPAND, HG_EXPAND), F32)],
        scratch_shapes=[pltpu.VMEM((H, HG_EXPAND, HG_EXPAND), F32)], comm=comm)
    return outs if comm is None else (*outs, updated, created)


def hgrn_bwd(proj, logits, ng, states, dout, layer, *, name, comm=None):
    S = proj.shape[0]
    D = proj.shape[1] // 4
    H = D // HG_EXPAND
    C = HG_CHUNK
    nc = S // C
    depth = logits.shape[0]
    consts = _hgrn_consts()
    rev = lambda c: nc - 1 - c

    def body(p_ref, lg_ref, ng_ref, ops3_ref, opst3_ref, mk_ref, st_in_ref, do_ref,
             dp_ref, dlg_ref, dng_ref, dst_ref):
        @pl.when(pl.program_id(0) == 0)
        def _():
            dst_ref[...] = jnp.zeros_like(dst_ref)
            dlg_ref[...] = jnp.zeros_like(dlg_ref)
            dng_ref[...] = jnp.zeros_like(dng_ref)

        fn = functools.partial(_hgrn_chunk, ops3=ops3_ref[...], opst3=opst3_ref[...], mk=mk_ref[...], layer=layer)
        _, vjp = jax.vjp(fn, *(p_ref[:, j * D:(j + 1) * D] for j in range(4)), tuple(st_in_ref[h] for h in range(H)),
                         lg_ref[...], ng_ref[...])
        *dproj, dsts, dlg, dng = vjp((do_ref[...], tuple(dst_ref[h] for h in range(H))))
        for j in range(4):
            dp_ref[:, j * D:(j + 1) * D] = dproj[j].astype(BF16)
        for h in range(H):
            dst_ref[h] = dsts[h]
        dlg_ref[...] += dlg
        dng_ref[...] += dng

    in_specs = _hgrn_in_specs(D, depth, rev)
    in_specs += [pl.BlockSpec((None, H, HG_EXPAND, HG_EXPAND), lambda c: (rev(c), 0, 0, 0)),
                 pl.BlockSpec((C, D), lambda c: (rev(c), 0))]
    outs, _, created = _carried_call(
        body, [proj, logits, ng, *consts, states, dout], name=name, grid=(nc,), in_specs=in_specs,
        out_specs=[pl.BlockSpec((C, 4 * D), lambda c: (rev(c), 0)),
                   pl.BlockSpec((depth, D), lambda c: (0, 0)), pl.BlockSpec((1, HG_EXPAND), lambda c: (0, 0))],
        out_shape=[jax.ShapeDtypeStruct((S, 4 * D), BF16), jax.ShapeDtypeStruct((depth, D), F32),
                   jax.ShapeDtypeStruct((1, HG_EXPAND), F32)],
        scratch_shapes=[pltpu.VMEM((H, HG_EXPAND, HG_EXPAND), F32)], comm=comm)
    return outs if comm is None else (*outs, created)


def _shift_down(u, prev, shift):
    rolled = pltpu.roll(u, shift, axis=0)
    rows = lax.broadcasted_iota(jnp.int32, prev.shape, 0)
    top = jnp.where(rows < shift, pltpu.roll(prev, shift, axis=0), rolled[:SUBLANES])
    return jnp.concatenate([top, rolled[SUBLANES:]], axis=0)


def _shift_up(x, nxt, shift):
    tm = x.shape[0]
    rolled = pltpu.roll(x, tm - shift, axis=0)
    rows = lax.broadcasted_iota(jnp.int32, nxt.shape, 0)
    bottom = jnp.where(rows >= SUBLANES - shift, pltpu.roll(nxt, SUBLANES - shift, axis=0), rolled[tm - SUBLANES:])
    return jnp.concatenate([rolled[:tm - SUBLANES], bottom], axis=0)


def _conv(u, prev, w_ref, b_ref, half):
    u1 = _shift_down(u, prev, 1)
    u2 = _shift_down(u, prev, 2)
    return b_ref[half] + w_ref[half, 0:1, :] * u2 + w_ref[half, 1:2, :] * u1 + w_ref[half, 2:3, :] * u


def _halo_rows(ref, keep, last):
    full = ref.astype(F32)
    return (full[HALO_ROWS - SUBLANES:] if last else full[:SUBLANES]) * keep


def _conv_specs(tm, tc, l):
    nprev = lambda m: jnp.maximum(m * (tm // HALO_ROWS) - 1, 0)
    tile = pl.BlockSpec((2, tm, tc), lambda j, m: (0, m, j))
    halo = pl.BlockSpec((2, HALO_ROWS, tc), lambda j, m: (0, nprev(m), j))
    wspec = pl.BlockSpec((None, 2, 3, tc), lambda j, m: (l, 0, 0, j))
    bspec = pl.BlockSpec((None, 2, 1, tc), lambda j, m: (l, 0, 0, j))
    return tile, halo, wspec, bspec


def convact_fwd(u, cw, cb, l, *, name, comm=None):
    _, S, Fh = u.shape
    tm = _pick(S, (256, 128))
    tc = _pick(Fh, (1408, 512, 256, 128))
    tile, halo, wspec, bspec = _conv_specs(tm, tc, l)

    def body(u_ref, p_ref, w_ref, b_ref, o_ref, ot_ref, c_ref):
        keep = (pl.program_id(1) > 0).astype(F32)
        cg = _conv(u_ref[0].astype(F32), _halo_rows(p_ref[0], keep, True), w_ref, b_ref, 0)
        cv = _conv(u_ref[1].astype(F32), _halo_rows(p_ref[1], keep, True), w_ref, b_ref, 1)
        act = _silu(cg) * cv
        o_ref[...] = act.astype(BF16)
        ot_ref[...] = act.T.astype(BF16)
        c_ref[0] = cg.astype(BF16)
        c_ref[1] = cv.astype(BF16)

    outs, updated, _ = _carried_call(
        body, [u, u, cw, cb], name=name, grid=(Fh // tc, S // tm), in_specs=[tile, halo, wspec, bspec],
        out_specs=[pl.BlockSpec((tm, tc), lambda j, m: (m, j)), pl.BlockSpec((tc, tm), lambda j, m: (j, m)), tile],
        out_shape=[jax.ShapeDtypeStruct((S, Fh), BF16), jax.ShapeDtypeStruct((Fh, S), BF16),
                   jax.ShapeDtypeStruct(u.shape, BF16)], comm=comm)
    return tuple(outs) if comm is None else (*outs, updated)


def convact_bwd(c, u, cw, l, da, *, name):
    _, S, Fh = u.shape
    tm = _pick(S, (256, 128))
    tc = _pick(Fh, (1408, 512, 256, 128))
    nm = S // tm

    def body(c_ref, u_ref, w_ref, da_ref, du_ref, dw_ref, db_ref, carry):
        @pl.when(pl.program_id(1) == 0)
        def _():
            carry[...] = jnp.zeros_like(carry)
            dw_ref[...] = jnp.zeros_like(dw_ref)
            db_ref[...] = jnp.zeros_like(db_ref)

        cg, cv = c_ref[0].astype(F32), c_ref[1].astype(F32)
        da_t = da_ref[...]
        sg = _sigmoid(cg)
        dcv = da_t * (cg * sg)
        dcg = da_t * cv * (sg * (1.0 + cg * (1.0 - sg)))
        rs = lambda t: jnp.sum(t, axis=0, keepdims=True)
        for half, dc in ((0, dcg), (1, dcv)):
            nxt = carry[half]
            d1, d2 = _shift_up(dc, nxt, 1), _shift_up(dc, nxt, 2)
            du = w_ref[half, 2:3, :] * dc + w_ref[half, 1:2, :] * d1 + w_ref[half, 0:1, :] * d2
            du_ref[half] = du.astype(BF16)
            uh = u_ref[half].astype(F32)
            for j, d in enumerate((d2, d1, dc)):
                dw_ref[half, j:j + 1, :] += rs(d * uh)
            db_ref[half] += rs(dc)
            carry[half] = dc[:SUBLANES]

    tile = pl.BlockSpec((2, tm, tc), lambda j, m: (0, nm - 1 - m, j))
    return pl.pallas_call(
        body, name=name, grid=(Fh // tc, nm),
        in_specs=[tile, tile, pl.BlockSpec((None, 2, 3, tc), lambda j, m: (l, 0, 0, j)),
                  pl.BlockSpec((tm, tc), lambda j, m: (nm - 1 - m, j))],
        out_specs=[tile, pl.BlockSpec((2, 3, tc), lambda j, m: (0, 0, j)),
                   pl.BlockSpec((2, 1, tc), lambda j, m: (0, 0, j))],
        out_shape=[jax.ShapeDtypeStruct(u.shape, BF16), jax.ShapeDtypeStruct((2, 3, Fh), F32),
                   jax.ShapeDtypeStruct((2, 1, Fh), F32)],
        scratch_shapes=[pltpu.VMEM((2, SUBLANES, tc), F32)],
        compiler_params=_params(("parallel", "arbitrary")),
    )(c, u, cw, da)


def _adamw_math(w, g, m, v):
    m = ADAM_B1 * m + (1.0 - ADAM_B1) * g
    v = ADAM_B2 * v + (1.0 - ADAM_B2) * (g * g)
    m_hat = m / (1.0 - ADAM_B1 ** ADAM_STEP)
    v_hat = v / (1.0 - ADAM_B2 ** ADAM_STEP)
    delta = -ADAM_LR * (m_hat / (jnp.sqrt(v_hat) + ADAM_EPS) + ADAM_WD * w)
    return delta, m, v


def _row_tile(R, Ccols, n_streams):
    for tr in (512, 256, 128, 64, 32, 16, 8):
        if R % tr == 0 and tr * Ccols * 4 * n_streams * 2 <= VMEM_LIMIT_BYTES // 2:
            return tr
    return R


def adamw(w, g, m, v, *, name):
    R, Ccols = w.shape
    tr = _row_tile(R, Ccols, 7)

    def body(w_ref, g_ref, m_ref, v_ref, d_ref, mo_ref, vo_ref):
        d, mn, vn = _adamw_math(w_ref[...], g_ref[...], m_ref[...], v_ref[...])
        d_ref[...] = d
        mo_ref[...] = mn
        vo_ref[...] = vn

    spec = pl.BlockSpec((tr, Ccols), lambda i: (i, 0))
    shp = jax.ShapeDtypeStruct((R, Ccols), F32)
    return pl.pallas_call(
        body, name=name, grid=(R // tr,), in_specs=[spec] * 4, out_specs=[spec] * 3,
        out_shape=[shp, shp, shp], compiler_params=_params(("parallel",)),
    )(w, g, m, v)


def _prefetch_call(body, pos, args, *, name, grid, in_specs, out_specs, out_shape, sem, aliases=None):
    return pl.pallas_call(
        lambda pos_ref, *refs: body(*refs), name=name,
        grid_spec=pltpu.PrefetchScalarGridSpec(num_scalar_prefetch=1, grid=grid, in_specs=in_specs,
                                               out_specs=out_specs),
        out_shape=out_shape, input_output_aliases={1 + i: o for i, o in (aliases or {}).items()},
        compiler_params=_params(sem),
    )(pos, *args)


def cast_place(w, pos, *, row_sharded, name):
    L, R, Ccols = w.shape

    def body(w_ref, o_ref):
        o_ref[...] = w_ref[...].astype(BF16)

    if row_sharded:
        grid = (L,)
        in_spec = pl.BlockSpec((None, R, Ccols), lambda l, pos: (l, 0, 0))
        out_spec = pl.BlockSpec((None, R, Ccols), lambda l, pos: (l, pos[0], 0))
        out_shape = jax.ShapeDtypeStruct((L, N_CHIPS * R, Ccols), BF16)
        sem = ("parallel",)
    else:
        tr = _pick(R, (512, 256, 128))
        grid = (L, R // tr)
        in_spec = pl.BlockSpec((None, tr, Ccols), lambda l, i, pos: (l, i, 0))
        out_spec = pl.BlockSpec((None, tr, Ccols), lambda l, i, pos: (l, i, pos[0]))
        out_shape = jax.ShapeDtypeStruct((L, R, N_CHIPS * Ccols), BF16)
        sem = ("parallel", "parallel")
    return _prefetch_call(body, pos, [w], name=name, grid=grid, in_specs=[in_spec], out_specs=out_spec,
                          out_shape=out_shape, sem=sem)


def pair_add(g, theirs, pos, *, row_sharded, name):
    def body(g_ref, t_ref, o_ref, ob_ref):
        s = g_ref[...] + t_ref[...]
        o_ref[...] = s
        ob_ref[...] = s.astype(BF16)

    if row_sharded:
        L, _, _, r2, D = g.shape
        grid = (L, N_CHIPS)
        g_spec = pl.BlockSpec((None, None, None, r2, D), lambda l, q, pos: (l, q, pos[1], 0, 0))
        t_spec = pl.BlockSpec((None, None, r2, D), lambda l, q, pos: (l, q, 0, 0))
    else:
        L, K, N = g.shape
        n = N // N_CHIPS
        tr = _pick(K // 2, (256, 128))
        nb = (K // 2) // tr
        grid = (L, nb, N_CHIPS)
        g_spec = pl.BlockSpec((None, tr, n), lambda l, i, j, pos: (l, pos[1] * nb + i, j))
        t_spec = pl.BlockSpec((None, tr, n), lambda l, i, j, pos: (l, i, j))
    return _prefetch_call(body, pos, [g, theirs], name=name, grid=grid, in_specs=[g_spec, t_spec],
                          out_specs=[t_spec, t_spec],
                          out_shape=[jax.ShapeDtypeStruct(theirs.shape, F32), jax.ShapeDtypeStruct(theirs.shape, BF16)],
                          sem=("parallel",) * len(grid))


def chip_add(pair, landed, pos, buf, li, n_layers, *, row_sharded, name):
    def body(p_ref, a_ref, b_ref, c_ref, *rest):
        rest[-1][...] = ((p_ref[...] + a_ref[...].astype(F32)) + b_ref[...].astype(F32)) + c_ref[...].astype(F32)

    if row_sharded:
        _, _, r2, D = pair.shape
        grid = (1,)
        p_spec = pl.BlockSpec((None, None, r2, D), lambda i, pos: (0, pos[0], 0, 0))
        l_specs = [pl.BlockSpec((None, r2, D), functools.partial(lambda i, pos, k: (k, 0, 0), k=k)) for k in range(3)]
        out_spec = pl.BlockSpec((None, None, r2, D), lambda i, pos: (li, pos[1], 0, 0))
        out_shape = jax.ShapeDtypeStruct((n_layers, 2, r2, D), F32)
    else:
        _, K2, N = pair.shape
        n = N // N_CHIPS
        tr = _pick(K2, (256, 128))
        nb = K2 // tr
        grid = (nb,)
        p_spec = pl.BlockSpec((None, tr, n), lambda i, pos: (0, i, pos[0]))
        l_specs = [pl.BlockSpec((None, tr, n), functools.partial(lambda i, pos, k: (k, i, 0), k=k)) for k in range(3)]
        out_spec = pl.BlockSpec((None, tr, n), lambda i, pos: (li, pos[1] * nb + i, 0))
        out_shape = jax.ShapeDtypeStruct((n_layers, 2 * K2, n), F32)
    args, in_specs, aliases = [pair, landed, landed, landed], [p_spec] + l_specs, None
    if buf is not None:
        args.append(buf)
        in_specs.append(pl.BlockSpec(memory_space=pl.ANY))
        aliases = {4: 0}
    return _prefetch_call(body, pos, args, name=name, grid=grid, in_specs=in_specs, out_specs=out_spec,
                          out_shape=out_shape, sem=("parallel",), aliases=aliases)


def _position():
    return lax.axis_index("x"), lax.axis_index("y"), lax.axis_index("c")


def _other_chips(x, y):
    return [(1 - x, y), (x, 1 - y), (1 - x, 1 - y)]


HBM_ANY = pl.BlockSpec(memory_space=pl.ANY)
COMM_PARAMS = pltpu.CompilerParams(has_side_effects=True)


class Comm:
    def __init__(self, ro=(), inout=(), new=(), sems=(), phases=()):
        self.ro, self.inout, self.new, self.sems, self.phases = list(ro), list(inout), list(new), list(sems), list(phases)


def _carried_call(body, args, *, name, grid, in_specs, out_specs, out_shape, scratch_shapes=(), comm=None):
    comm = comm or Comm()
    n_in, n_out, n_sc = len(in_specs), len(out_shape), len(scratch_shapes)
    n_ro, n_io, n_new, n_sem = len(comm.ro), len(comm.inout), len(comm.new), len(comm.sems)
    last = int(np.prod(grid)) - 1

    def wrapped(*refs):
        bounds = np.cumsum([0, n_in, n_ro, n_io, n_out, n_io, n_new, n_sc, n_sem])
        core_in, ro, _, core_out, io, new, core_sc, sems = (refs[a:b] for a, b in zip(bounds[:-1], bounds[1:]))
        step = pl.program_id(0)
        for d in range(1, len(grid)):
            step = step * grid[d] + pl.program_id(d)
        for frac, fn in comm.phases[:-1]:
            pl.when(step == int(frac * last))(functools.partial(fn, ro, io, new, sems))
        body(*core_in, *core_out, *core_sc)
        if comm.phases:
            pl.when(step == last)(functools.partial(comm.phases[-1][1], ro, io, new, sems))

    outs = pl.pallas_call(
        wrapped, name=name, grid=grid,
        in_specs=list(in_specs) + [HBM_ANY] * (n_ro + n_io),
        out_specs=list(out_specs) + [HBM_ANY] * (n_io + n_new),
        out_shape=list(out_shape) + [jax.ShapeDtypeStruct(a.shape, a.dtype) for a in comm.inout] + comm.new,
        input_output_aliases={n_in + n_ro + j: n_out + j for j in range(n_io)},
        scratch_shapes=list(scratch_shapes) + comm.sems,
        compiler_params=pltpu.CompilerParams(dimension_semantics=("arbitrary",) * len(grid),
                                             vmem_limit_bytes=VMEM_LIMIT_BYTES, has_side_effects=bool(comm.phases)),
    )(*args, *comm.ro, *comm.inout)
    return outs[:n_out], outs[n_out:n_out + n_io], outs[n_out + n_io:]


def run_comm(comm, *, name):
    n_ro, n_io, n_new = len(comm.ro), len(comm.inout), len(comm.new)

    def body(*refs):
        bounds = np.cumsum([0, n_ro, n_io, n_io, n_new, len(comm.sems)])
        ro, _, io, new, sems = (refs[a:b] for a, b in zip(bounds[:-1], bounds[1:]))
        for _, fn in comm.phases:
            fn(ro, io, new, sems)

    outs = pl.pallas_call(
        body, name=name, in_specs=[HBM_ANY] * (n_ro + n_io), out_specs=[HBM_ANY] * (n_io + n_new),
        out_shape=[jax.ShapeDtypeStruct(a.shape, a.dtype) for a in comm.inout] + comm.new,
        input_output_aliases={n_ro + j: j for j in range(n_io)},
        scratch_shapes=comm.sems, compiler_params=COMM_PARAMS,
    )(*comm.ro, *comm.inout)
    return outs[:n_io], outs[n_io:]


def _rdma(src, dst, send_sem, recv_sem, device):
    return pltpu.make_async_remote_copy(src_ref=src, dst_ref=dst, send_sem=send_sem, recv_sem=recv_sem,
                                        device_id=device, device_id_type=MESH)


def gather_comm(bufs, slices):
    n = len(slices)

    def region(io, s, h, chip):
        b, li, row_sharded = slices[s]
        ref = io[b]
        if row_sharded:
            r = ref.shape[1] // N_CHIPS
            return ref.at[li, pl.ds(chip * r + h * (r // 2), r // 2), :]
        K, nn = ref.shape[1], ref.shape[2] // N_CHIPS
        return ref.at[li, pl.ds(h * (K // 2), K // 2), pl.ds(chip * nn, nn)]

    def copies(io, sems):
        x, y, c = _position()
        me = 2 * x + y
        chips = _other_chips(x, y)
        ids = [2 * cx + cy for cx, cy in chips]

        def ici(s, k, src):
            return _rdma(region(io, s, c, me), region(io, s, c, src), sems[0].at[s, k], sems[1].at[s, k], (*chips[k], c))

        def d2d(s, k, h):
            return _rdma(region(io, s, h, ids[k]), region(io, s, h, ids[k]), sems[2].at[s, k], sems[3].at[s, k],
                         (x, y, 1 - c))

        return me, ids, c, ici, d2d

    pairs = [(s, k) for s in range(n) for k in range(3)]

    def start(ro, io, new, sems):
        me, _, _, ici, _ = copies(io, sems)
        for s, k in pairs:
            ici(s, k, me).start()

    def pass_on(ro, io, new, sems):
        _, ids, c, ici, d2d = copies(io, sems)
        for s, k in pairs:
            ici(s, k, ids[k]).wait_recv()
            d2d(s, k, c).start()

    def finish(ro, io, new, sems):
        me, _, c, ici, d2d = copies(io, sems)
        for s, k in pairs:
            d2d(s, k, 1 - c).wait_recv()
        for s, k in pairs:
            ici(s, k, me).wait_send()
            d2d(s, k, c).wait_send()

    return Comm(inout=bufs, sems=[pltpu.SemaphoreType.DMA((n, 3))] * 4,
                phases=[(0.0, start), (0.6, pass_on), (1.0, finish)])


def gather_small(shard, *, name):
    n = shard.shape[2]

    def body(s_ref, o_ref, send_sem, recv_sem, local_sem):
        x, y, c = _position()
        me = 2 * x + y
        chips = _other_chips(x, y)
        place = lambda chip: o_ref.at[:, :, pl.ds(chip * n, n)]
        local = pltpu.make_async_copy(s_ref, place(me), local_sem)
        local.start()
        for k in range(3):
            _rdma(s_ref, place(me), send_sem.at[k], recv_sem.at[k], (*chips[k], c)).start()
        for k in range(3):
            _rdma(s_ref, place(2 * chips[k][0] + chips[k][1]), send_sem.at[k], recv_sem.at[k], (*chips[k], c)).wait()
        local.wait()

    return pl.pallas_call(
        body, name=name, in_specs=[HBM_ANY], out_specs=HBM_ANY,
        out_shape=jax.ShapeDtypeStruct((shard.shape[0], shard.shape[1], N_CHIPS * n), shard.dtype),
        scratch_shapes=[pltpu.SemaphoreType.DMA((3,)), pltpu.SemaphoreType.DMA((3,)), pltpu.SemaphoreType.DMA],
        compiler_params=COMM_PARAMS,
    )(shard)


def swap_comm(grads):
    def copies(ro, new, sems):
        x, y, c = _position()
        h = 1 - c
        cps = []
        for s, (a, row_sharded) in enumerate(grads):
            if row_sharded:
                r = a.shape[1] // N_CHIPS
                for q in range(N_CHIPS):
                    cps.append(_rdma(ro[s].at[:, pl.ds(q * r + h * (r // 2), r // 2), :], new[s].at[:, q],
                                     sems[0].at[s, q], sems[1].at[s, q], (x, y, 1 - c)))
            else:
                K = a.shape[1]
                cps.append(_rdma(ro[s].at[:, pl.ds(h * (K // 2), K // 2), :], new[s],
                                 sems[0].at[s, 0], sems[1].at[s, 0], (x, y, 1 - c)))
        return cps

    def start(ro, io, new, sems):
        for cp in copies(ro, new, sems):
            cp.start()

    def finish(ro, io, new, sems):
        for cp in copies(ro, new, sems):
            cp.wait()

    theirs = []
    for a, row_sharded in grads:
        if row_sharded:
            theirs.append(jax.ShapeDtypeStruct((1, N_CHIPS, a.shape[1] // N_CHIPS // 2, a.shape[2]), a.dtype))
        else:
            theirs.append(jax.ShapeDtypeStruct((1, a.shape[1] // 2, a.shape[2]), a.dtype))
    return Comm(ro=[a for a, _ in grads], new=theirs, sems=[pltpu.SemaphoreType.DMA((len(grads), N_CHIPS))] * 2,
                phases=[(0.0, start), (1.0, finish)])


def merge_comms(comms):
    spans, ro, io, new, sems = [], [], [], [], []
    for cm in comms:
        spans.append((len(ro), len(io), len(new), len(sems)))
        ro, io, new, sems = ro + cm.ro, io + cm.inout, new + cm.new, sems + cm.sems

    def phase(frac):
        def run(R, I, N, S):
            for cm, (a, b, d, e) in zip(comms, spans):
                for f, fn in cm.phases:
                    if f == frac:
                        fn(R[a:a + len(cm.ro)], I[b:b + len(cm.inout)], N[d:d + len(cm.new)], S[e:e + len(cm.sems)])
        return run

    fracs = sorted({f for cm in comms for f, _ in cm.phases})
    merged = Comm(ro=ro, inout=io, new=new, sems=sems, phases=[(f, phase(f)) for f in fracs])
    cut = lambda created: [created[d:d + len(cm.new)] for cm, (_, _, d, _) in zip(comms, spans)]
    return merged, cut


def scatter_comm(halves):
    def copies(ro, new, sems):
        x, y, c = _position()
        chips = _other_chips(x, y)
        cps = []
        for s, (a, row_sharded) in enumerate(halves):
            for k in range(3):
                to = 2 * chips[k][0] + chips[k][1]
                if row_sharded:
                    src = ro[s].at[0, to]
                else:
                    n = a.shape[2] // N_CHIPS
                    src = ro[s].at[0, :, pl.ds(to * n, n)]
                cps.append(_rdma(src, new[s].at[k], sems[0].at[s, k], sems[1].at[s, k], (*chips[k], c)))
        return cps

    def start(ro, io, new, sems):
        for cp in copies(ro, new, sems):
            cp.start()

    def finish(ro, io, new, sems):
        for cp in copies(ro, new, sems):
            cp.wait()

    landing = [jax.ShapeDtypeStruct((3, a.shape[2], a.shape[3]) if row_sharded else
                                    (3, a.shape[1], a.shape[2] // N_CHIPS), a.dtype) for a, row_sharded in halves]
    return Comm(ro=[a for a, _ in halves], new=landing, sems=[pltpu.SemaphoreType.DMA((len(halves), 3))] * 2,
                phases=[(0.0, start), (1.0, finish)])


def sibling_join_halves(shards, *, name):
    n_in = len(shards)

    def body(*refs):
        ins = refs[:n_in]
        outs = refs[n_in:2 * n_in]
        send_sem, recv_sem = refs[2 * n_in:]
        x, y, c = _position()
        sibling = (x, y, 1 - c)

        def rows(ref, h):
            R2 = ref.shape[1] // 2
            return ref.at[:, pl.ds(h * R2, R2), :]

        sends = [pltpu.make_async_remote_copy(src_ref=rows(ins[i], c), dst_ref=rows(outs[i], c),
                                              send_sem=send_sem.at[i], recv_sem=recv_sem.at[i],
                                              device_id=sibling, device_id_type=MESH) for i in range(n_in)]
        lands = [pltpu.make_async_remote_copy(src_ref=rows(ins[i], c), dst_ref=rows(outs[i], 1 - c),
                                              send_sem=send_sem.at[i], recv_sem=recv_sem.at[i],
                                              device_id=sibling, device_id_type=MESH) for i in range(n_in)]
        for cp in sends:
            cp.start()
        for i in range(n_in):
            sends[i].wait_send()
            lands[i].wait_recv()

    return pl.pallas_call(
        body, name=name, in_specs=[HBM_ANY] * n_in, out_specs=[HBM_ANY] * n_in,
        out_shape=[jax.ShapeDtypeStruct(a.shape, a.dtype) for a in shards],
        input_output_aliases={i: i for i in range(n_in)},
        scratch_shapes=[pltpu.SemaphoreType.DMA((n_in,)), pltpu.SemaphoreType.DMA((n_in,))],
        compiler_params=COMM_PARAMS,
    )(*shards)


def all_reduce_small(v, *, name):
    R, Ccols = v.shape

    def body(v_ref, o_ref, slots, send_sem, recv_sem):
        x, y, c = _position()
        me = 4 * x + 2 * y + c
        slots[me] = v_ref[...]

        def peer(d):
            return x ^ (d >> 2), y ^ ((d >> 1) & 1), c ^ (d & 1)

        def copy(d, slot):
            return pltpu.make_async_remote_copy(
                src_ref=v_ref, dst_ref=slots.at[slot], send_sem=send_sem.at[d], recv_sem=recv_sem.at[d],
                device_id=peer(d), device_id_type=MESH)

        sends = [copy(d, me) for d in range(1, N_DEV)]
        for cp in sends:
            cp.start()
        for d in range(1, N_DEV):
            px, py, pc = peer(d)
            copy(d, 4 * px + 2 * py + pc).wait_recv()
        for cp in sends:
            cp.wait_send()
        acc = slots[0]
        for d in range(1, N_DEV):
            acc = acc + slots[d]
        o_ref[...] = acc

    return pl.pallas_call(
        body, name=name,
        in_specs=[pl.BlockSpec(memory_space=pltpu.VMEM)], out_specs=pl.BlockSpec(memory_space=pltpu.VMEM),
        out_shape=jax.ShapeDtypeStruct((R, Ccols), F32),
        scratch_shapes=[pltpu.VMEM((N_DEV, R, Ccols), F32), pltpu.SemaphoreType.DMA((N_DEV,)),
                        pltpu.SemaphoreType.DMA((N_DEV,))],
        compiler_params=COMM_PARAMS,
    )(v)


def _pack_small(parts):
    flat = jnp.concatenate([p.reshape(-1).astype(F32) for p in parts])
    n = flat.shape[0]
    rows = -(-n // LANES)
    rows = -(-rows // SUBLANES) * SUBLANES
    return jnp.pad(flat, (0, rows * LANES - n)).reshape(rows, LANES)


def _unpack_small(packed, like):
    flat = packed.reshape(-1)
    out, off = [], 0
    for p in like:
        out.append(flat[off:off + p.size].reshape(p.shape))
        off += p.size
    return out


def kernel(x, norm_mix, norm_ffn, norm_final, attn_w_in, attn_w_out, attn_sinks, hgrn_w_in, hgrn_w_out, hgrn_norm, hgrn_lb_logits, ffn_w_up, ffn_conv_w, ffn_conv_b, ffn_w_down, loss_target, m_norm_mix, m_norm_ffn, m_norm_final, m_attn_w_in, m_attn_w_out, m_attn_sinks, m_hgrn_w_in, m_hgrn_w_out, m_hgrn_norm, m_hgrn_lb_logits, m_ffn_w_up, m_ffn_conv_w, m_ffn_conv_b, m_ffn_w_down, v_norm_mix, v_norm_ffn, v_norm_final, v_attn_w_in, v_attn_w_out, v_attn_sinks, v_hgrn_w_in, v_hgrn_w_out, v_hgrn_norm, v_hgrn_lb_logits, v_ffn_w_up, v_ffn_conv_w, v_ffn_conv_b, v_ffn_w_down):
    S, D = x.shape[1], x.shape[2]
    depth = norm_mix.shape[0]
    n_q = D // HEAD_DIM
    n_kv = n_q // Q_PER_KV
    Fh = ffn_w_down.shape[1] * N_CHIPS
    h = x.reshape(S, D)
    target = loss_target.reshape(S, D)

    col_names = ["attn_w_in", "hgrn_w_in", "ffn_w_up"]
    row_names = ["attn_w_out", "hgrn_w_out", "ffn_w_down"]
    x_, y_, c_ = _position()
    me = 2 * x_ + y_
    pos = jnp.stack([me, c_]).astype(jnp.int32)
    col_w = [cast_place(w, pos, row_sharded=False, name=f"cast_place_{n}")
             for n, w in zip(col_names, (attn_w_in, hgrn_w_in, ffn_w_up))]
    row_w = [cast_place(w, pos, row_sharded=True, name=f"cast_place_{n}")
             for n, w in zip(row_names, (attn_w_out, hgrn_w_out, ffn_w_down))]
    W = dict(zip(col_names + row_names, col_w + row_w))
    is_row = {n: n in row_names for n in W}

    def layer_weights(layer):
        mixer = "attn" if layer % 2 == 0 else "hgrn"
        return [(f"{mixer}_w_in", layer // 2), (f"{mixer}_w_out", layer // 2), ("ffn_w_up", layer), ("ffn_w_down", layer)]

    def gather_of(group):
        names = [n for n, _ in group]
        return names, gather_comm([W[n] for n in names], [(i, li, is_row[n]) for i, (n, li) in enumerate(group)])

    def mixer_weights(layer):
        return layer_weights(layer)[:2] if layer < depth else []

    names_next, comm_next = gather_of(mixer_weights(0))
    filled, _ = run_comm(comm_next, name="gather_mixer0")
    W.update(zip(names_next, filled))
    cw = gather_small(ffn_conv_w, name="gather_conv_w").reshape(depth, 3, 2, Fh).transpose(0, 2, 1, 3)
    cb = ffn_conv_b.reshape(depth, 2, 1, Fh)
    g_mix = norm_mix.reshape(depth, 1, D)
    g_ffn = norm_ffn.reshape(depth, 1, D)

    slopes = jnp.exp2(-8.0 * jnp.arange(1, n_q + 1, dtype=F32) / n_q).reshape(n_kv, Q_PER_KV)
    slope_col = jnp.repeat(slopes, WINDOW, axis=1).reshape(n_kv, Q_PER_KV * WINDOW, 1)

    def to_heads(t, nh):
        return t.reshape(S, nh, HEAD_DIM).transpose(1, 0, 2)

    def from_heads(t):
        return t.transpose(1, 0, 2).reshape(S, -1)

    saved = []
    for layer in range(depth):
        idx = layer // 2
        rec = {"h_mix": h}
        hn, rec["hn_t"] = rmsnorm_fwd(h, g_mix, layer, name=f"norm_mix_fwd{layer}")
        is_attn = layer % 2 == 0
        up_l, down_l = layer_weights(layer)[2:]
        on_mixer = [up_l] if is_attn else [up_l, down_l]
        on_up = [down_l] if is_attn else mixer_weights(layer + 1)
        on_act = mixer_weights(layer + 1)[:1] if is_attn else []
        on_down = mixer_weights(layer + 1)[1:] if is_attn else []
        names_next, comm_next = gather_of(on_mixer)
        if is_attn:
            proj = mm_nn(hn, W["attn_w_in"], idx, name=f"attn_in_fwd{layer}", out_dtype=BF16)
            q = to_heads(proj[:, :n_q * HEAD_DIM], n_q).reshape(n_kv, Q_PER_KV, S, HEAD_DIM)
            k = to_heads(proj[:, n_q * HEAD_DIM:(n_q + n_kv) * HEAD_DIM], n_kv)
            v = to_heads(proj[:, (n_q + n_kv) * HEAD_DIM:], n_kv)
            sink_col = jnp.repeat(attn_sinks[idx].reshape(n_kv, Q_PER_KV), WINDOW, axis=1)
            sink_col = sink_col.reshape(n_kv, Q_PER_KV * WINDOW, 1)
            o_heads, filled, _ = attn_fwd(q, k, v, sink_col, slope_col, name=f"attn_fwd{layer}", comm=comm_next)
            W.update(zip(names_next, filled))
            o = from_heads(o_heads.reshape(n_q, S, HEAD_DIM))
            rec.update(q=q, k=k, v=v, sink_col=sink_col, o=o)
            h = mm_nn(o, W["attn_w_out"], idx, name=f"attn_out_fwd{layer}", res=h)
        else:
            proj = mm_nn(hn, W["hgrn_w_in"], idx, name=f"hgrn_in_fwd{layer}")
            ng = hgrn_norm[idx].reshape(1, HG_EXPAND)
            o, states, filled, _ = hgrn_fwd(proj, hgrn_lb_logits, ng, layer, name=f"hgrn_fwd{layer}", comm=comm_next)
            W.update(zip(names_next, filled))
            rec.update(proj=proj, ng=ng, states=states, o=o)
            h = mm_nn(o, W["hgrn_w_out"], idx, name=f"hgrn_out_fwd{layer}", res=h)
        rec["h_ffn"] = h
        hn2, rec["hn2_t"] = rmsnorm_fwd(h, g_ffn, layer, name=f"norm_ffn_fwd{layer}")
        if on_up:
            names_next, comm_next = gather_of(on_up)
            u, filled = mm_nn(hn2, W["ffn_w_up"], layer, name=f"ffn_up_fwd{layer}", split_out=True, out_dtype=BF16,
                              comm=comm_next)
            W.update(zip(names_next, filled))
        else:
            u = mm_nn(hn2, W["ffn_w_up"], layer, name=f"ffn_up_fwd{layer}", split_out=True, out_dtype=BF16)
        if on_act:
            names_next, comm_next = gather_of(on_act)
            act, rec["act_t"], rec["c"], filled = convact_fwd(u, cw, cb, layer, name=f"ffn_act_fwd{layer}",
                                                              comm=comm_next)
            W.update(zip(names_next, filled))
        else:
            act, rec["act_t"], rec["c"] = convact_fwd(u, cw, cb, layer, name=f"ffn_act_fwd{layer}")
        rec["u"] = u
        if on_down:
            names_next, comm_next = gather_of(on_down)
            h, filled = mm_nn(act, W["ffn_w_down"], layer, name=f"ffn_down_fwd{layer}", res=h, comm=comm_next)
            W.update(zip(names_next, filled))
        else:
            h = mm_nn(act, W["ffn_w_down"], layer, name=f"ffn_down_fwd{layer}", res=h)
        saved.append(rec)

    loss_part, dh, d_norm_final = loss_head(h, norm_final.reshape(1, D), target, name="loss_head")

    n_attn, n_hgrn = attn_w_in.shape[0], hgrn_w_in.shape[0]
    reduced_parts = []
    pending = []

    def swap_of(group, g):
        ordered = sorted(group, key=lambda nl: is_row[nl[0]])
        return ordered, swap_comm([(g[n], is_row[n]) for n, _ in ordered])

    def pair_up(tag, ordered, g, theirs):
        pair, pair_bf16 = {}, {}
        for (n, _), t in zip(ordered, theirs):
            gn = g[n]
            if is_row[n]:
                r = gn.shape[1] // N_CHIPS
                gn = gn.reshape(1, N_CHIPS, 2, r // 2, gn.shape[2])
            pair[n], pair_bf16[n] = pair_add(gn, t, pos, row_sharded=is_row[n], name=f"rs_pair_add_{n}{tag}")
        pending.append((ordered, pair, scatter_comm([(pair_bf16[n], is_row[n]) for n, _ in ordered])))

    def landed_behind(created, cut):
        for (ordered, pair, _), landed in zip(pending, cut(created)):
            reduced_parts.append((ordered, pair, landed))
        pending.clear()

    d_norm_mix, d_norm_ffn = [None] * depth, [None] * depth
    d_conv_w, d_conv_b = [None] * depth, [None] * depth
    d_sinks, d_hgrn_norm = [None] * n_attn, [None] * n_hgrn
    d_logits = jnp.zeros_like(hgrn_lb_logits)
    for layer in reversed(range(depth)):
        idx = layer // 2
        rec = saved[layer]
        g = {"ffn_w_down": mm_dw(rec["act_t"], dh, name=f"ffn_down_dw{layer}")}
        dact = mm_nt(dh, W["ffn_w_down"], layer, name=f"ffn_down_dx{layer}")
        du, dcw, dcb = convact_bwd(rec["c"], rec["u"], cw, layer, dact, name=f"ffn_act_bwd{layer}")
        d_conv_w[layer] = dcw.transpose(1, 0, 2).reshape(3, 2 * Fh)
        d_conv_b[layer] = dcb.reshape(2 * Fh)
        g["ffn_w_up"] = mm_dw(rec["hn2_t"], du, name=f"ffn_up_dw{layer}", split_b=True)
        ordered, swap = swap_of(layer_weights(layer)[2:], g)
        dh, dg, theirs = mm_nt(du, W["ffn_w_up"], layer, name=f"ffn_up_dx{layer}", split_in=True, comm=swap,
                               norm=(rec["h_ffn"], g_ffn, layer, dh))
        pair_up(layer, ordered, g, theirs)
        d_norm_ffn[layer] = dg.reshape(D)
        carried, cut = merge_comms([p[2] for p in pending])
        if layer % 2 == 0:
            g["attn_w_out"] = mm_tn(rec["o"], dh, name=f"attn_out_dw{layer}")
            do = mm_nt(dh, W["attn_w_out"], idx, name=f"attn_out_dx{layer}", out_dtype=BF16)
            do_heads = to_heads(do, n_q).reshape(n_kv, Q_PER_KV, S, HEAD_DIM)
            res = attn_bwd(rec["q"], rec["k"], rec["v"], rec["sink_col"], slope_col, do_heads,
                           name=f"attn_bwd{layer}", comm=carried)
            dq, dk, dv, dsink = res[:4]
            landed_behind(res[-1], cut)
            dproj = jnp.concatenate([from_heads(dq.reshape(n_q, S, HEAD_DIM)), from_heads(dk), from_heads(dv)], axis=1)
            d_sinks[idx] = jnp.sum(dsink.reshape(n_kv, Q_PER_KV, WINDOW), axis=-1).reshape(n_q)
            g["attn_w_in"] = mm_dw(rec["hn_t"], dproj, name=f"attn_in_dw{layer}")
            ordered, swap = swap_of(layer_weights(layer)[:2], g)
            dh, dg, theirs = mm_nt(dproj, W["attn_w_in"], idx, name=f"attn_in_dx{layer}", comm=swap,
                                   norm=(rec["h_mix"], g_mix, layer, dh))
        else:
            g["hgrn_w_out"] = mm_tn(rec["o"], dh, name=f"hgrn_out_dw{layer}")
            do = mm_nt(dh, W["hgrn_w_out"], idx, name=f"hgrn_out_dx{layer}")
            res = hgrn_bwd(rec["proj"], hgrn_lb_logits, rec["ng"], rec["states"], do, layer,
                           name=f"hgrn_bwd{layer}", comm=carried)
            dproj, dlg, dng = res[:3]
            landed_behind(res[-1], cut)
            d_logits = d_logits + dlg
            d_hgrn_norm[idx] = dng.reshape(HG_EXPAND)
            g["hgrn_w_in"] = mm_dw(rec["hn_t"], dproj, name=f"hgrn_in_dw{layer}")
            ordered, swap = swap_of(layer_weights(layer)[:2], g)
            dh, dg, theirs = mm_nt(dproj, W["hgrn_w_in"], idx, name=f"hgrn_in_dx{layer}", comm=swap,
                                   norm=(rec["h_mix"], g_mix, layer, dh))
        pair_up(layer, ordered, g, theirs)
        d_norm_mix[layer] = dg.reshape(D)
    grad_x = dh.reshape(x.shape)
    carried, cut = merge_comms([p[2] for p in pending])
    landed_behind(run_comm(carried, name="rs_scatter_last")[1], cut)

    small_w = [norm_mix, norm_ffn, norm_final, attn_sinks, hgrn_norm, hgrn_lb_logits, ffn_conv_b]
    small_m = [m_norm_mix, m_norm_ffn, m_norm_final, m_attn_sinks, m_hgrn_norm, m_hgrn_lb_logits, m_ffn_conv_b]
    small_v = [v_norm_mix, v_norm_ffn, v_norm_final, v_attn_sinks, v_hgrn_norm, v_hgrn_lb_logits, v_ffn_conv_b]
    small_grads = [jnp.stack(d_norm_mix), jnp.stack(d_norm_ffn), d_norm_final.reshape(D), jnp.stack(d_sinks),
                   jnp.stack(d_hgrn_norm), d_logits, jnp.stack(d_conv_b)]
    gconv = jnp.stack(d_conv_w)
    packed_small = _pack_small(small_grads + [loss_part])
    summed = all_reduce_small(jnp.concatenate([packed_small, _pack_small([gconv])], axis=0), name="all_reduce_small")
    reduced, gconv_rows = summed[:packed_small.shape[0]], summed[packed_small.shape[0]:]
    small_g = _unpack_small(reduced, small_w + [loss_part])
    loss = small_g.pop().reshape(())
    pad = [jnp.zeros((1, 1), F32)]
    sd, sm_, sv_ = adamw(_pack_small(small_w + pad), reduced, _pack_small(small_m + pad),
                         _pack_small(small_v + pad), name="adamw_small")
    small_d = _unpack_small(sd, small_w)
    small_nm = _unpack_small(sm_, small_w)
    small_nv = _unpack_small(sv_, small_w)

    big_names = col_names + row_names
    big_w = dict(attn_w_in=attn_w_in, hgrn_w_in=hgrn_w_in, ffn_w_up=ffn_w_up, attn_w_out=attn_w_out,
                 hgrn_w_out=hgrn_w_out, ffn_w_down=ffn_w_down)
    half_shard = dict.fromkeys(big_names)
    for ordered, pair, landed in reduced_parts:
        for (n, li), ld in zip(ordered, landed):
            half_shard[n] = chip_add(pair[n], ld, pos, half_shard[n], li, big_w[n].shape[0], row_sharded=is_row[n],
                                     name=f"rs_chip_add_{n}{li}")
    shards = sibling_join_halves([half_shard[n].reshape(big_w[n].shape) for n in big_names], name="rs_sibling_join")
    big_m = dict(attn_w_in=m_attn_w_in, hgrn_w_in=m_hgrn_w_in, ffn_w_up=m_ffn_w_up, attn_w_out=m_attn_w_out,
                 hgrn_w_out=m_hgrn_w_out, ffn_w_down=m_ffn_w_down)
    big_v = dict(attn_w_in=v_attn_w_in, hgrn_w_in=v_hgrn_w_in, ffn_w_up=v_ffn_w_up, attn_w_out=v_attn_w_out,
                 hgrn_w_out=v_hgrn_w_out, ffn_w_down=v_ffn_w_down)
    grads, deltas, new_m, new_v = {}, {}, {}, {}
    for name_, gshard in zip(big_names, shards):
        w_ = big_w[name_]
        cols = w_.shape[-1]
        d_, m_, v_ = adamw(w_.reshape(-1, cols), gshard.reshape(-1, cols), big_m[name_].reshape(-1, cols),
                           big_v[name_].reshape(-1, cols), name=f"adamw_{name_}")
        grads[name_] = gshard.reshape(w_.shape)
        deltas[name_], new_m[name_], new_v[name_] = (t.reshape(w_.shape) for t in (d_, m_, v_))

    n_conv = ffn_conv_w.shape[2]
    gconv_all = _unpack_small(gconv_rows, [gconv])[0]
    gconv_mine = lax.dynamic_slice_in_dim(gconv_all, me * n_conv, n_conv, axis=2)
    cshape = ffn_conv_w.shape
    d_, m_, v_ = adamw(ffn_conv_w.reshape(-1, n_conv), gconv_mine.reshape(-1, n_conv),
                       m_ffn_conv_w.reshape(-1, n_conv), v_ffn_conv_w.reshape(-1, n_conv), name="adamw_ffn_conv_w")
    grads["ffn_conv_w"] = gconv_mine
    deltas["ffn_conv_w"], new_m["ffn_conv_w"], new_v["ffn_conv_w"] = (t.reshape(cshape) for t in (d_, m_, v_))

    order = ["norm_mix", "norm_ffn", "norm_final", "attn_w_in", "attn_w_out", "attn_sinks", "hgrn_w_in",
             "hgrn_w_out", "hgrn_norm", "hgrn_lb_logits", "ffn_w_up", "ffn_conv_w", "ffn_conv_b", "ffn_w_down"]
    small_names = ["norm_mix", "norm_ffn", "norm_final", "attn_sinks", "hgrn_norm", "hgrn_lb_logits", "ffn_conv_b"]
    for i, name_ in enumerate(small_names):
        grads[name_], deltas[name_], new_m[name_], new_v[name_] = small_g[i], small_d[i], small_nm[i], small_nv[i]
    return (loss, grad_x, *[grads[n] for n in order], *[deltas[n] for n in order],
            *[new_m[n] for n in order], *[new_v[n] for n in order])
```

```python
import functools
import math

import numpy as np
import jax
import jax.numpy as jnp
from jax import lax
from jax.experimental import pallas as pl
from jax.experimental.pallas import tpu as pltpu

F32 = jnp.float32
BF16 = jnp.bfloat16
MESH = pl.DeviceIdType.MESH

HEAD_DIM = 64
Q_PER_KV = 4
WINDOW = 128
HG_EXPAND = 128
HG_CHUNK = 128
EPS = 1e-6
N_CHIPS = 4
N_DEV = 8

ADAM_LR = 0.001
ADAM_B1 = 0.9
ADAM_B2 = 0.999
ADAM_EPS = 1e-08
ADAM_WD = 0.01
ADAM_STEP = 10

VMEM_LIMIT_BYTES = 48 * 1024 * 1024
LANES = 128
SUBLANES = 8
HALO_ROWS = 16


def _params(sem=None):
    return pltpu.CompilerParams(dimension_semantics=sem, vmem_limit_bytes=VMEM_LIMIT_BYTES)


def _pick(n, cands):
    for c in cands:
        if n % c == 0:
            return c
    return n


def _dot(a, b, dims):
    return lax.dot_general(a.astype(BF16), b.astype(BF16), (dims, ((), ())),
                           preferred_element_type=F32)


NN = ((1,), (0,))
NT = ((1,), (1,))
TN = ((0,), (0,))


def mm_nn(a, w, l, *, name, res=None, out_dtype=F32, split_out=False, comm=None):
    M, K = a.shape
    N = w.shape[2]
    tm = _pick(M, (1024, 512, 256, 128))
    tn = _pick(N // 2 if split_out else N, (1408, 1024, 768, 512, 256, 128))
    tk = _pick(K, (1024, 1408, 512, 256, 128))
    nk = K // tk
    nh = (N // 2) // tn

    def body(*refs):
        if res is None:
            a_ref, w_ref, o_ref = refs[:3]
            r_ref = None
        else:
            a_ref, w_ref, r_ref, o_ref = refs[:4]
        part = _dot(a_ref[...], w_ref[...], NN)

        def finish(acc):
            if r_ref is not None:
                acc = acc + r_ref[...]
            o_ref[...] = acc.astype(out_dtype)

        if nk == 1:
            finish(part)
        else:
            acc_ref = refs[-1]
            k = pl.program_id(2)

            @pl.when(k == 0)
            def _():
                acc_ref[...] = part

            @pl.when(k > 0)
            def _():
                acc_ref[...] += part

            @pl.when(k == nk - 1)
            def _():
                finish(acc_ref[...])

    in_specs = [pl.BlockSpec((tm, tk), lambda n, m, k: (m, k)),
                pl.BlockSpec((None, tk, tn), lambda n, m, k: (l, k, n))]
    args = [a, w]
    if res is not None:
        in_specs.append(pl.BlockSpec((tm, tn), lambda n, m, k: (m, n)))
        args.append(res)
    if split_out:
        out_shape = jax.ShapeDtypeStruct((2, M, N // 2), out_dtype)
        out_spec = pl.BlockSpec((None, tm, tn), lambda n, m, k: (n // nh, m, n % nh))
    else:
        out_shape = jax.ShapeDtypeStruct((M, N), out_dtype)
        out_spec = pl.BlockSpec((tm, tn), lambda n, m, k: (m, n))
    outs, updated, _ = _carried_call(
        body, args, name=name, grid=(N // tn, M // tm, nk), in_specs=in_specs, out_specs=[out_spec],
        out_shape=[out_shape], scratch_shapes=[] if nk == 1 else [pltpu.VMEM((tm, tn), F32)], comm=comm)
    return outs[0] if comm is None else (outs[0], updated)


def mm_nt(a, w, l, *, name, split_in=False, out_dtype=F32, comm=None, norm=None):
    if split_in:
        _, M, Nh = a.shape
        N = 2 * Nh
    else:
        M, N = a.shape
        Nh = N
    K = w.shape[1]
    tm = _pick(M, (512, 256, 128) if norm is not None else (1024, 512, 256, 128))
    tko = _pick(K, (1024, 1408, 512, 256, 128))
    tr = _pick(Nh, (2816, 2048, 1536, 1408, 1024, 768, 512, 256, 128) if tm <= 512 else (1024, 1408, 768, 512, 256, 128))
    nr = N // tr
    nh = Nh // tr
    n_in, n_out = (5, 2) if norm is not None else (2, 1)
    assert norm is None or tko == K

    def body(*refs):
        a_ref, w_ref = refs[:2]
        o_ref = refs[n_in]
        part = _dot(a_ref[...], w_ref[...], NT)

        def finish(acc):
            if norm is None:
                o_ref[...] = acc.astype(out_dtype)
                return
            h_ref, g_ref, dres_ref = refs[2:5]
            dg_ref = refs[n_in + 1]
            _, vjp = jax.vjp(_rms, h_ref[...], g_ref[...])
            dx, dg = vjp(acc)
            o_ref[...] = dres_ref[...] + dx
            first = pl.program_id(1) == 0

            @pl.when(first)
            def _():
                dg_ref[...] = dg

            @pl.when(jnp.logical_not(first))
            def _():
                dg_ref[...] += dg

        if nr == 1:
            finish(part)
        else:
            acc_ref = refs[n_in + n_out]
            r = pl.program_id(2)

            @pl.when(r == 0)
            def _():
                acc_ref[...] = part

            @pl.when(r > 0)
            def _():
                acc_ref[...] += part

            @pl.when(r == nr - 1)
            def _():
                finish(acc_ref[...])

    if split_in:
        a_spec = pl.BlockSpec((None, tm, tr), lambda ko, m, r: (r // nh, m, r % nh))
    else:
        a_spec = pl.BlockSpec((tm, tr), lambda ko, m, r: (m, r))
    args = [a, w]
    in_specs = [a_spec, pl.BlockSpec((None, tko, tr), lambda ko, m, r: (l, ko, r))]
    out_specs = [pl.BlockSpec((tm, tko), lambda ko, m, r: (m, ko))]
    out_shape = [jax.ShapeDtypeStruct((M, K), out_dtype)]
    if norm is not None:
        h, g, l_norm, dres = norm
        row = pl.BlockSpec((tm, K), lambda ko, m, r: (m, 0))
        args += [h, g, dres]
        in_specs += [row, pl.BlockSpec((None, 1, K), lambda ko, m, r: (l_norm, 0, 0)), row]
        out_specs.append(pl.BlockSpec((1, K), lambda ko, m, r: (0, 0)))
        out_shape.append(jax.ShapeDtypeStruct((1, K), F32))
    outs, _, created = _carried_call(
        body, args, name=name, grid=(K // tko, M // tm, nr), in_specs=in_specs, out_specs=out_specs,
        out_shape=out_shape, scratch_shapes=[] if nr == 1 else [pltpu.VMEM((tm, tko), F32)], comm=comm)
    outs = list(outs) + ([] if comm is None else [created])
    return outs[0] if len(outs) == 1 else tuple(outs)


def mm_tn(a, b, *, name):
    M, K = a.shape
    N = b.shape[1]
    tm = _pick(M, (512, 256, 128))
    tko = _pick(K, (1024, 1408, 704, 512, 256, 128))
    tn = _pick(N, (512, 256, 128))
    nm = M // tm

    def body(a_ref, b_ref, o_ref):
        part = _dot(a_ref[...], b_ref[...], TN)
        r = pl.program_id(2)

        @pl.when(r == 0)
        def _():
            o_ref[...] = part

        @pl.when(r > 0)
        def _():
            o_ref[...] += part

    return pl.pallas_call(
        body, name=name, grid=(K // tko, N // tn, nm),
        in_specs=[pl.BlockSpec((tm, tko), lambda ko, n, r: (r, ko)), pl.BlockSpec((tm, tn), lambda ko, n, r: (r, n))],
        out_specs=pl.BlockSpec((None, tko, tn), lambda ko, n, r: (0, ko, n)),
        out_shape=jax.ShapeDtypeStruct((1, K, N), F32),
        compiler_params=_params(("parallel", "parallel", "arbitrary")),
    )(a, b)


def mm_dw(at, b, *, name, split_b=False):
    K, M = at.shape
    if split_b:
        Nh = b.shape[2]
        N = 2 * Nh
    else:
        N = b.shape[1]
        Nh = N
    tm = _pick(M, (2048, 1024, 512, 256, 128) if b.dtype.itemsize == 2 else (1024, 512, 256, 128))
    tko = _pick(K, (1024, 1408, 704, 512, 256, 128))
    tn = _pick(Nh, (1408, 1024, 768, 512, 256, 128))
    nm = M // tm
    nh = Nh // tn

    def body(a_ref, b_ref, o_ref):
        part = _dot(a_ref[...], b_ref[...], NN)
        r = pl.program_id(2)

        @pl.when(r == 0)
        def _():
            o_ref[...] = part

        @pl.when(r > 0)
        def _():
            o_ref[...] += part

    if split_b:
        b_spec = pl.BlockSpec((None, tm, tn), lambda ko, n, r: (n // nh, r, n % nh))
    else:
        b_spec = pl.BlockSpec((tm, tn), lambda ko, n, r: (r, n))
    return pl.pallas_call(
        body, name=name, grid=(K // tko, N // tn, nm),
        in_specs=[pl.BlockSpec((tko, tm), lambda ko, n, r: (ko, r)), b_spec],
        out_specs=pl.BlockSpec((None, tko, tn), lambda ko, n, r: (0, ko, n)),
        out_shape=jax.ShapeDtypeStruct((1, K, N), F32),
        compiler_params=_params(("parallel", "parallel", "arbitrary")),
    )(at, b)


def _rms(x, g):
    return x * lax.rsqrt(jnp.mean(x * x, axis=-1, keepdims=True) + EPS) * g


def rmsnorm_fwd(h, g, l, *, name):
    S, D = h.shape
    tm = _pick(S, (512, 256, 128))

    def body(h_ref, g_ref, o_ref, ot_ref):
        y = _rms(h_ref[...], g_ref[...])
        o_ref[...] = y.astype(BF16)
        ot_ref[...] = y.T.astype(BF16)

    return pl.pallas_call(
        body, name=name, grid=(S // tm,),
        in_specs=[pl.BlockSpec((tm, D), lambda m: (m, 0)),
                  pl.BlockSpec((None, 1, D), lambda m: (l, 0, 0))],
        out_specs=[pl.BlockSpec((tm, D), lambda m: (m, 0)), pl.BlockSpec((D, tm), lambda m: (0, m))],
        out_shape=[jax.ShapeDtypeStruct((S, D), BF16), jax.ShapeDtypeStruct((D, S), BF16)],
        compiler_params=_params(("parallel",)),
    )(h, g)


def loss_head(h, g, target, *, name):
    S, D = h.shape
    tm = _pick(S, (512, 256, 128))

    def body(h_ref, g_ref, t_ref, loss_ref, dh_ref, dg_ref):
        y, vjp = jax.vjp(_rms, h_ref[...], g_ref[...])
        err = y - t_ref[...]
        part = 0.5 * jnp.sum(jnp.sum(err * err, axis=-1, keepdims=True) / D, axis=0, keepdims=True)
        dx, dg = vjp(err / D)
        dh_ref[...] = dx

        @pl.when(pl.program_id(0) == 0)
        def _():
            dg_ref[...] = dg
            loss_ref[...] = part

        @pl.when(pl.program_id(0) > 0)
        def _():
            dg_ref[...] += dg
            loss_ref[...] += part

    row = pl.BlockSpec((tm, D), lambda m: (m, 0))
    return pl.pallas_call(
        body, name=name, grid=(S // tm,),
        in_specs=[row, pl.BlockSpec((1, D), lambda m: (0, 0)), row],
        out_specs=[pl.BlockSpec((1, 1), lambda m: (0, 0)), row, pl.BlockSpec((1, D), lambda m: (0, 0))],
        out_shape=[jax.ShapeDtypeStruct((1, 1), F32), jax.ShapeDtypeStruct((S, D), F32),
                   jax.ShapeDtypeStruct((1, D), F32)],
        compiler_params=_params(("arbitrary",)),
    )(h, g, target)


def _make_mm(dims_fwd, dims_da, dims_db, swap_da=False, swap_db=False):
    @jax.custom_vjp
    def mm(a, b):
        return _dot(a, b, dims_fwd)

    def fwd(a, b):
        return mm(a, b), (a, b)

    def bwd(resid, g):
        a, b = resid
        da = _dot(b, g, dims_da) if swap_da else _dot(g, b, dims_da)
        db = _dot(g, a, dims_db) if swap_db else _dot(a, g, dims_db)
        return da, db

    mm.defvjp(fwd, bwd)
    return mm


_mm = _make_mm(NN, NT, TN)
_mm_nt = _make_mm(NT, NN, TN, swap_db=True)
_mm_tn = _make_mm(TN, NT, NN, swap_da=True)


def _dot_hi(a, b):
    return jnp.dot(a, b, precision=lax.Precision.HIGHEST, preferred_element_type=F32)


def _sigmoid(x):
    return 0.5 * jnp.tanh(0.5 * x) + 0.5


def _silu(x):
    return x * _sigmoid(x)


def _row_slices(x, n):
    rows = x.shape[0] // n
    return tuple(x[i * rows:(i + 1) * rows] for i in range(n))


@functools.partial(jax.custom_vjp, nondiff_argnums=(1,))
def _split_rows(x, n):
    return _row_slices(x, n)


_split_rows.defvjp(lambda x, n: (_row_slices(x, n), None), lambda n, _, cts: (jnp.concatenate(cts, axis=0),))


def _attn_block(qs, kws, vws, sink, slope, first):
    s = jnp.concatenate([_mm_nt(q, kw) for q, kw in zip(qs, kws)], axis=0) * (HEAD_DIM ** -0.5)
    qi = lax.broadcasted_iota(jnp.int32, s.shape, 0) % WINDOW
    ki = lax.broadcasted_iota(jnp.int32, s.shape, 1)
    dist = qi + WINDOW - ki
    valid = (dist >= 0) & (dist < WINDOW) & (ki >= WINDOW * first)
    s = jnp.where(valid, s - slope * dist.astype(F32), -jnp.inf)
    m = lax.stop_gradient(jnp.maximum(jnp.max(s, axis=-1, keepdims=True), sink))
    e = jnp.exp(s - m)
    denom = jnp.sum(e, axis=-1, keepdims=True) + jnp.exp(sink - m)
    ps = _split_rows(e * (1.0 / denom), len(qs))
    return tuple(_mm(p, vw) for p, vw in zip(ps, vws))


def _attn_specs(Hkv, order):
    G = Q_PER_KV
    qspec = pl.BlockSpec((Hkv, G, WINDOW, HEAD_DIM), lambda n: (0, 0, order(n), 0))
    prev = pl.BlockSpec((Hkv, WINDOW, HEAD_DIM), lambda n: (0, jnp.maximum(order(n) - 1, 0), 0))
    cur = pl.BlockSpec((Hkv, WINDOW, HEAD_DIM), lambda n: (0, order(n), 0))
    col = pl.BlockSpec((Hkv, G * WINDOW, 1), lambda n: (0, 0, 0))
    return qspec, prev, cur, col


def attn_fwd(q, k, v, sink_col, slope_col, *, name, comm=None):
    Hkv, G, S, _ = q.shape
    qspec, prev, cur, col = _attn_specs(Hkv, lambda n: n)

    def body(q_ref, kp_ref, kc_ref, vp_ref, vc_ref, sink_ref, slope_ref, o_ref):
        first = (pl.program_id(0) == 0).astype(jnp.int32)
        qs = tuple(q_ref[h].reshape(G * WINDOW, HEAD_DIM) for h in range(Hkv))
        kws = tuple(jnp.concatenate([kp_ref[h], kc_ref[h]], axis=0) for h in range(Hkv))
        vws = tuple(jnp.concatenate([vp_ref[h], vc_ref[h]], axis=0) for h in range(Hkv))
        outs = _attn_block(qs, kws, vws, sink_ref[...].reshape(Hkv * G * WINDOW, 1),
                           slope_ref[...].reshape(Hkv * G * WINDOW, 1), first)
        o_ref[...] = jnp.stack([o.reshape(G, WINDOW, HEAD_DIM).astype(BF16) for o in outs])

    outs, updated, created = _carried_call(
        body, [q, k, k, v, v, sink_col, slope_col], name=name, grid=(S // WINDOW,),
        in_specs=[qspec, prev, cur, prev, cur, col, col], out_specs=[qspec],
        out_shape=[jax.ShapeDtypeStruct(q.shape, BF16)], comm=comm)
    return outs[0] if comm is None else (outs[0], updated, created)


def attn_bwd(q, k, v, sink_col, slope_col, do, *, name, comm=None):
    Hkv, G, S, _ = q.shape
    nb = S // WINDOW
    qspec, prev, cur, col = _attn_specs(Hkv, lambda n: nb - 1 - n)

    def body(q_ref, kp_ref, kc_ref, vp_ref, vc_ref, sink_ref, slope_ref, do_ref,
             dq_ref, dk_ref, dv_ref, dsink_ref, dk_carry, dv_carry):
        @pl.when(pl.program_id(0) == 0)
        def _():
            dk_carry[...] = jnp.zeros_like(dk_carry)
            dv_carry[...] = jnp.zeros_like(dv_carry)
            dsink_ref[...] = jnp.zeros_like(dsink_ref)

        first = (pl.program_id(0) == nb - 1).astype(jnp.int32)
        qs = tuple(q_ref[h].reshape(G * WINDOW, HEAD_DIM).astype(F32) for h in range(Hkv))
        kws = tuple(jnp.concatenate([kp_ref[h], kc_ref[h]], axis=0).astype(F32) for h in range(Hkv))
        vws = tuple(jnp.concatenate([vp_ref[h], vc_ref[h]], axis=0).astype(F32) for h in range(Hkv))
        fn = functools.partial(_attn_block, slope=slope_ref[...].reshape(Hkv * G * WINDOW, 1), first=first)
        _, vjp = jax.vjp(fn, qs, kws, vws, sink_ref[...].reshape(Hkv * G * WINDOW, 1))
        dqs, dkws, dvws, dsink = vjp(tuple(do_ref[h].reshape(G * WINDOW, HEAD_DIM).astype(F32) for h in range(Hkv)))
        dkw, dvw = jnp.stack(dkws), jnp.stack(dvws)
        dq_ref[...] = jnp.stack([dq.reshape(G, WINDOW, HEAD_DIM) for dq in dqs]).astype(BF16)
        dk_ref[...] = (dkw[:, WINDOW:] + dk_carry[...]).astype(BF16)
        dv_ref[...] = (dvw[:, WINDOW:] + dv_carry[...]).astype(BF16)
        dk_carry[...] = dkw[:, :WINDOW]
        dv_carry[...] = dvw[:, :WINDOW]
        dsink_ref[...] += dsink.reshape(Hkv, G * WINDOW, 1)

    kv_shape = jax.ShapeDtypeStruct(k.shape, BF16)
    outs, _, created = _carried_call(
        body, [q, k, k, v, v, sink_col, slope_col, do], name=name, grid=(nb,),
        in_specs=[qspec, prev, cur, prev, cur, col, col, qspec],
        out_specs=[qspec, cur, cur, col],
        out_shape=[jax.ShapeDtypeStruct(q.shape, BF16), kv_shape, kv_shape,
                   jax.ShapeDtypeStruct(sink_col.shape, F32)],
        scratch_shapes=[pltpu.VMEM((Hkv, WINDOW, HEAD_DIM), F32), pltpu.VMEM((Hkv, WINDOW, HEAD_DIM), F32)],
        comm=comm)
    return outs if comm is None else (*outs, created)


def _hgrn_consts():
    C = HG_CHUNK
    t = np.arange(C)[:, None]
    j = np.arange(C)[None, :]
    cq, ck, mk = [], [], []
    H = C // 2
    while H >= 1:
        start = (t // H) * H
        cq.append((j > start) & (j <= t))
        ck.append((j > t) & (j <= start + H))
        mk.append(((t // H) % 2 == 1) & ((j // H) == (t // H) - 1))
        H //= 2
    ltri = (j <= t)
    ops = np.concatenate([ltri] + cq + ck, axis=0).astype(np.float32)
    ops3 = np.concatenate([ops, ops, ops], axis=1)
    opst3 = np.concatenate([ops.T, ops.T, ops.T], axis=1)
    return (jnp.asarray(ops3, BF16), jnp.asarray(opst3, BF16), jnp.asarray(np.stack(mk).astype(np.float32)))


def _split3(x):
    hi = x.astype(BF16)
    r = x - hi.astype(F32)
    mid = r.astype(BF16)
    lo = (r - mid.astype(F32)).astype(BF16)
    return jnp.concatenate([hi, mid, lo], axis=0)


@jax.custom_vjp
def _seg_sums(g, ops3, opst3):
    C = g.shape[0]
    out = jnp.dot(ops3, _split3(g), preferred_element_type=F32)
    return tuple(out[i * C:(i + 1) * C] for i in range(ops3.shape[0] // C))


def _seg_sums_fwd(g, ops3, opst3):
    return _seg_sums(g, ops3, opst3), opst3


def _seg_sums_bwd(opst3, cts):
    ct = jnp.concatenate(cts, axis=0)
    return jnp.dot(opst3, _split3(ct), preferred_element_type=F32), None, None


_seg_sums.defvjp(_seg_sums_fwd, _seg_sums_bwd)


def _head_slices(x):
    return tuple(x[:, h * HG_EXPAND:(h + 1) * HG_EXPAND] for h in range(x.shape[1] // HG_EXPAND))


@jax.custom_vjp
def _split_heads(x):
    return _head_slices(x)


_split_heads.defvjp(lambda x: (_head_slices(x), None), lambda _, cts: (jnp.concatenate(cts, axis=1),))


@jax.custom_vjp
def _join_heads(xs):
    return jnp.concatenate(xs, axis=1)


_join_heads.defvjp(lambda xs: (jnp.concatenate(xs, axis=1), None), lambda _, ct: (_head_slices(ct),))


def _hgrn_chunk(qr, fr, iv, gr, sts, logits, ng, ops3, opst3, mk, layer):
    n_levels = mk.shape[0]
    depth, D = logits.shape
    e = jnp.exp(logits - lax.stop_gradient(jnp.max(logits, axis=0, keepdims=True)))
    sm = e / jnp.sum(e, axis=0, keepdims=True)
    row = lax.broadcasted_iota(jnp.int32, (depth, D), 0)
    lb = jnp.sum(jnp.where((row >= 1) & (row <= layer), sm, 0.0), axis=0, keepdims=True)

    q = _silu(qr)
    a = jnp.log(lb)
    c = jnp.log(1.0 - lb) - (jnp.maximum(-fr, 0.0) + jnp.log(1.0 + jnp.exp(-jnp.abs(fr))))
    mx = lax.stop_gradient(jnp.maximum(a, c))
    g = mx + jnp.log(jnp.exp(a - mx) + jnp.exp(c - mx))
    k = (1.0 - lb) * _sigmoid(-fr)

    sums = _seg_sums(g, ops3, opst3)
    b = sums[0]
    b_last = jnp.sum(g, axis=0, keepdims=True)

    ql = [_split_heads(q * jnp.exp(sums[1 + lv])) for lv in range(n_levels)]
    kl = [_split_heads(k * jnp.exp(sums[1 + n_levels + lv])) for lv in range(n_levels)]
    qb, kb = _split_heads(q * jnp.exp(b)), _split_heads(k * jnp.exp(b_last - b))
    qk, ivh, gate, decay = _split_heads(q * k), _split_heads(iv), _split_heads(_silu(gr)), _split_heads(jnp.exp(b_last))
    outs, new_sts = [], []
    for h, st in enumerate(sts):
        amat = None
        for lv in range(n_levels):
            part = _mm_nt(ql[lv][h], kl[lv][h]) * mk[lv]
            amat = part if amat is None else amat + part
        o = jnp.sum(qk[h], axis=-1, keepdims=True) * ivh[h] + _mm(amat, ivh[h]) + _mm_nt(qb[h], st)
        new_sts.append(st * decay[h] + _mm_tn(ivh[h], kb[h]))
        on = o * lax.rsqrt(jnp.mean(o * o, axis=-1, keepdims=True) + EPS) * ng
        outs.append(on * gate[h])
    return _join_heads(tuple(outs)), tuple(new_sts)


def _hgrn_in_specs(D, depth, order):
    C = HG_CHUNK
    nl = int(math.log2(C))
    n_ops = 1 + 2 * nl
    return [pl.BlockSpec((C, 4 * D), lambda c: (order(c), 0)),
            pl.BlockSpec((depth, D), lambda c: (0, 0)),
            pl.BlockSpec((1, HG_EXPAND), lambda c: (0, 0)),
            pl.BlockSpec((n_ops * C, 3 * C), lambda c: (0, 0)),
            pl.BlockSpec((C, 3 * n_ops * C), lambda c: (0, 0)),
            pl.BlockSpec((nl, C, C), lambda c: (0, 0, 0))]


def hgrn_fwd(proj, logits, ng, layer, *, name, comm=None):
    S = proj.shape[0]
    D = proj.shape[1] // 4
    H = D // HG_EXPAND
    C = HG_CHUNK
    nc = S // C
    consts = _hgrn_consts()

    def body(p_ref, lg_ref, ng_ref, ops3_ref, opst3_ref, mk_ref, o_ref, st_out_ref, st_ref):
        @pl.when(pl.program_id(0) == 0)
        def _():
            st_ref[...] = jnp.zeros_like(st_ref)

        st_out_ref[...] = st_ref[...]
        out, new_sts = _hgrn_chunk(*(p_ref[:, j * D:(j + 1) * D] for j in range(4)), tuple(st_ref[h] for h in range(H)),
                                   lg_ref[...], ng_ref[...], ops3_ref[...], opst3_ref[...], mk_ref[...], layer)
        o_ref[...] = out.astype(BF16)
        for h in range(H):
            st_ref[h] = new_sts[h]

    outs, updated, created = _carried_call(
        body, [proj, logits, ng, *consts], name=name, grid=(nc,),
        in_specs=_hgrn_in_specs(D, logits.shape[0], lambda c: c),
        out_specs=[pl.BlockSpec((C, D), lambda c: (c, 0)),
                   pl.BlockSpec((None, H, HG_EXPAND, HG_EXPAND), lambda c: (c, 0, 0, 0))],
        out_shape=[jax.ShapeDtypeStruct((S, D), BF16),
                   jax.ShapeDtypeStruct((nc, H, HG_EXPAND, HG_EXPAND), F32)],
        scratch_shapes=[pltpu.VMEM((H, HG_EXPAND, HG_EXPAND), F32)], comm=comm)
    return outs if comm is None else (*outs, updated, created)


def hgrn_bwd(proj, logits, ng, states, dout, layer, *, name, comm=None):
    S = proj.shape[0]
    D = proj.shape[1] // 4
    H = D // HG_EXPAND
    C = HG_CHUNK
    nc = S // C
    depth = logits.shape[0]
    consts = _hgrn_consts()
    rev = lambda c: nc - 1 - c

    def body(p_ref, lg_ref, ng_ref, ops3_ref, opst3_ref, mk_ref, st_in_ref, do_ref,
             dp_ref, dlg_ref, dng_ref, dst_ref):
        @pl.when(pl.program_id(0) == 0)
        def _():
            dst_ref[...] = jnp.zeros_like(dst_ref)
            dlg_ref[...] = jnp.zeros_like(dlg_ref)
            dng_ref[...] = jnp.zeros_like(dng_ref)

        fn = functools.partial(_hgrn_chunk, ops3=ops3_ref[...], opst3=opst3_ref[...], mk=mk_ref[...], layer=layer)
        _, vjp = jax.vjp(fn, *(p_ref[:, j * D:(j + 1) * D] for j in range(4)), tuple(st_in_ref[h] for h in range(H)),
                         lg_ref[...], ng_ref[...])
        *dproj, dsts, dlg, dng = vjp((do_ref[...], tuple(dst_ref[h] for h in range(H))))
        for j in range(4):
            dp_ref[:, j * D:(j + 1) * D] = dproj[j].astype(BF16)
        for h in range(H):
            dst_ref[h] = dsts[h]
        dlg_ref[...] += dlg
        dng_ref[...] += dng

    in_specs = _hgrn_in_specs(D, depth, rev)
    in_specs += [pl.BlockSpec((None, H, HG_EXPAND, HG_EXPAND), lambda c: (rev(c), 0, 0, 0)),
                 pl.BlockSpec((C, D), lambda c: (rev(c), 0))]
    outs, _, created = _carried_call(
        body, [proj, logits, ng, *consts, states, dout], name=name, grid=(nc,), in_specs=in_specs,
        out_specs=[pl.BlockSpec((C, 4 * D), lambda c: (rev(c), 0)),
                   pl.BlockSpec((depth, D), lambda c: (0, 0)), pl.BlockSpec((1, HG_EXPAND), lambda c: (0, 0))],
        out_shape=[jax.ShapeDtypeStruct((S, 4 * D), BF16), jax.ShapeDtypeStruct((depth, D), F32),
                   jax.ShapeDtypeStruct((1, HG_EXPAND), F32)],
        scratch_shapes=[pltpu.VMEM((H, HG_EXPAND, HG_EXPAND), F32)], comm=comm)
    return outs if comm is None else (*outs, created)


def _shift_down(u, prev, shift):
    rolled = pltpu.roll(u, shift, axis=0)
    rows = lax.broadcasted_iota(jnp.int32, prev.shape, 0)
    top = jnp.where(rows < shift, pltpu.roll(prev, shift, axis=0), rolled[:SUBLANES])
    return jnp.concatenate([top, rolled[SUBLANES:]], axis=0)


def _shift_up(x, nxt, shift):
    tm = x.shape[0]
    rolled = pltpu.roll(x, tm - shift, axis=0)
    rows = lax.broadcasted_iota(jnp.int32, nxt.shape, 0)
    bottom = jnp.where(rows >= SUBLANES - shift, pltpu.roll(nxt, SUBLANES - shift, axis=0), rolled[tm - SUBLANES:])
    return jnp.concatenate([rolled[:tm - SUBLANES], bottom], axis=0)


def _conv(u, prev, w_ref, b_ref, half):
    u1 = _shift_down(u, prev, 1)
    u2 = _shift_down(u, prev, 2)
    return b_ref[half] + w_ref[half, 0:1, :] * u2 + w_ref[half, 1:2, :] * u1 + w_ref[half, 2:3, :] * u


def _halo_rows(ref, keep, last):
    full = ref.astype(F32)
    return (full[HALO_ROWS - SUBLANES:] if last else full[:SUBLANES]) * keep


def _conv_specs(tm, tc, l):
    nprev = lambda m: jnp.maximum(m * (tm // HALO_ROWS) - 1, 0)
    tile = pl.BlockSpec((2, tm, tc), lambda j, m: (0, m, j))
    halo = pl.BlockSpec((2, HALO_ROWS, tc), lambda j, m: (0, nprev(m), j))
    wspec = pl.BlockSpec((None, 2, 3, tc), lambda j, m: (l, 0, 0, j))
    bspec = pl.BlockSpec((None, 2, 1, tc), lambda j, m: (l, 0, 0, j))
    return tile, halo, wspec, bspec


def convact_fwd(u, cw, cb, l, *, name, comm=None):
    _, S, Fh = u.shape
    tm = _pick(S, (256, 128))
    tc = _pick(Fh, (1408, 512, 256, 128))
    tile, halo, wspec, bspec = _conv_specs(tm, tc, l)

    def body(u_ref, p_ref, w_ref, b_ref, o_ref, ot_ref, c_ref):
        keep = (pl.program_id(1) > 0).astype(F32)
        cg = _conv(u_ref[0].astype(F32), _halo_rows(p_ref[0], keep, True), w_ref, b_ref, 0)
        cv = _conv(u_ref[1].astype(F32), _halo_rows(p_ref[1], keep, True), w_ref, b_ref, 1)
        act = _silu(cg) * cv
        o_ref[...] = act.astype(BF16)
        ot_ref[...] = act.T.astype(BF16)
        c_ref[0] = cg.astype(BF16)
        c_ref[1] = cv.astype(BF16)

    outs, updated, _ = _carried_call(
        body, [u, u, cw, cb], name=name, grid=(Fh // tc, S // tm), in_specs=[tile, halo, wspec, bspec],
        out_specs=[pl.BlockSpec((tm, tc), lambda j, m: (m, j)), pl.BlockSpec((tc, tm), lambda j, m: (j, m)), tile],
        out_shape=[jax.ShapeDtypeStruct((S, Fh), BF16), jax.ShapeDtypeStruct((Fh, S), BF16),
                   jax.ShapeDtypeStruct(u.shape, BF16)], comm=comm)
    return tuple(outs) if comm is None else (*outs, updated)


def convact_bwd(c, u, cw, l, da, *, name):
    _, S, Fh = u.shape
    tm = _pick(S, (256, 128))
    tc = _pick(Fh, (1408, 512, 256, 128))
    nm = S // tm

    def body(c_ref, u_ref, w_ref, da_ref, du_ref, dw_ref, db_ref, carry):
        @pl.when(pl.program_id(1) == 0)
        def _():
            carry[...] = jnp.zeros_like(carry)
            dw_ref[...] = jnp.zeros_like(dw_ref)
            db_ref[...] = jnp.zeros_like(db_ref)

        cg, cv = c_ref[0].astype(F32), c_ref[1].astype(F32)
        da_t = da_ref[...]
        sg = _sigmoid(cg)
        dcv = da_t * (cg * sg)
        dcg = da_t * cv * (sg * (1.0 + cg * (1.0 - sg)))
        rs = lambda t: jnp.sum(t, axis=0, keepdims=True)
        for half, dc in ((0, dcg), (1, dcv)):
            nxt = carry[half]
            d1, d2 = _shift_up(dc, nxt, 1), _shift_up(dc, nxt, 2)
            du = w_ref[half, 2:3, :] * dc + w_ref[half, 1:2, :] * d1 + w_ref[half, 0:1, :] * d2
            du_ref[half] = du.astype(BF16)
            uh = u_ref[half].astype(F32)
            for j, d in enumerate((d2, d1, dc)):
                dw_ref[half, j:j + 1, :] += rs(d * uh)
            db_ref[half] += rs(dc)
            carry[half] = dc[:SUBLANES]

    tile = pl.BlockSpec((2, tm, tc), lambda j, m: (0, nm - 1 - m, j))
    return pl.pallas_call(
        body, name=name, grid=(Fh // tc, nm),
        in_specs=[tile, tile, pl.BlockSpec((None, 2, 3, tc), lambda j, m: (l, 0, 0, j)),
                  pl.BlockSpec((tm, tc), lambda j, m: (nm - 1 - m, j))],
        out_specs=[tile, pl.BlockSpec((2, 3, tc), lambda j, m: (0, 0, j)),
                   pl.BlockSpec((2, 1, tc), lambda j, m: (0, 0, j))],
        out_shape=[jax.ShapeDtypeStruct(u.shape, BF16), jax.ShapeDtypeStruct((2, 3, Fh), F32),
                   jax.ShapeDtypeStruct((2, 1, Fh), F32)],
        scratch_shapes=[pltpu.VMEM((2, SUBLANES, tc), F32)],
        compiler_params=_params(("parallel", "arbitrary")),
    )(c, u, cw, da)


def _adamw_math(w, g, m, v):
    m = ADAM_B1 * m + (1.0 - ADAM_B1) * g
    v = ADAM_B2 * v + (1.0 - ADAM_B2) * (g * g)
    m_hat = m / (1.0 - ADAM_B1 ** ADAM_STEP)
    v_hat = v / (1.0 - ADAM_B2 ** ADAM_STEP)
    delta = -ADAM_LR * (m_hat / (jnp.sqrt(v_hat) + ADAM_EPS) + ADAM_WD * w)
    return delta, m, v


def _row_tile(R, Ccols, n_streams):
    for tr in (512, 256, 128, 64, 32, 16, 8):
        if R % tr == 0 and tr * Ccols * 4 * n_streams * 2 <= VMEM_LIMIT_BYTES // 2:
            return tr
    return R


def adamw(w, g, m, v, *, name):
    R, Ccols = w.shape
    tr = _row_tile(R, Ccols, 7)

    def body(w_ref, g_ref, m_ref, v_ref, d_ref, mo_ref, vo_ref):
        d, mn, vn = _adamw_math(w_ref[...], g_ref[...], m_ref[...], v_ref[...])
        d_ref[...] = d
        mo_ref[...] = mn
        vo_ref[...] = vn

    spec = pl.BlockSpec((tr, Ccols), lambda i: (i, 0))
    shp = jax.ShapeDtypeStruct((R, Ccols), F32)
    return pl.pallas_call(
        body, name=name, grid=(R // tr,), in_specs=[spec] * 4, out_specs=[spec] * 3,
        out_shape=[shp, shp, shp], compiler_params=_params(("parallel",)),
    )(w, g, m, v)


def _prefetch_call(body, pos, args, *, name, grid, in_specs, out_specs, out_shape, sem, aliases=None):
    return pl.pallas_call(
        lambda pos_ref, *refs: body(*refs), name=name,
        grid_spec=pltpu.PrefetchScalarGridSpec(num_scalar_prefetch=1, grid=grid, in_specs=in_specs,
                                               out_specs=out_specs),
        out_shape=out_shape, input_output_aliases={1 + i: o for i, o in (aliases or {}).items()},
        compiler_params=_params(sem),
    )(pos, *args)


def cast_place(w, pos, *, row_sharded, name):
    L, R, Ccols = w.shape

    def body(w_ref, o_ref):
        o_ref[...] = w_ref[...].astype(BF16)

    if row_sharded:
        grid = (L,)
        in_spec = pl.BlockSpec((None, R, Ccols), lambda l, pos: (l, 0, 0))
        out_spec = pl.BlockSpec((None, R, Ccols), lambda l, pos: (l, pos[0], 0))
        out_shape = jax.ShapeDtypeStruct((L, N_CHIPS * R, Ccols), BF16)
        sem = ("parallel",)
    else:
        tr = _pick(R, (512, 256, 128))
        grid = (L, R // tr)
        in_spec = pl.BlockSpec((None, tr, Ccols), lambda l, i, pos: (l, i, 0))
        out_spec = pl.BlockSpec((None, tr, Ccols), lambda l, i, pos: (l, i, pos[0]))
        out_shape = jax.ShapeDtypeStruct((L, R, N_CHIPS * Ccols), BF16)
        sem = ("parallel", "parallel")
    return _prefetch_call(body, pos, [w], name=name, grid=grid, in_specs=[in_spec], out_specs=out_spec,
                          out_shape=out_shape, sem=sem)


def pair_add(g, theirs, pos, *, row_sharded, name):
    def body(g_ref, t_ref, o_ref, ob_ref):
        s = g_ref[...] + t_ref[...]
        o_ref[...] = s
        ob_ref[...] = s.astype(BF16)

    if row_sharded:
        L, _, _, r2, D = g.shape
        grid = (L, N_CHIPS)
        g_spec = pl.BlockSpec((None, None, None, r2, D), lambda l, q, pos: (l, q, pos[1], 0, 0))
        t_spec = pl.BlockSpec((None, None, r2, D), lambda l, q, pos: (l, q, 0, 0))
    else:
        L, K, N = g.shape
        n = N // N_CHIPS
        tr = _pick(K // 2, (256, 128))
        nb = (K // 2) // tr
        grid = (L, nb, N_CHIPS)
        g_spec = pl.BlockSpec((None, tr, n), lambda l, i, j, pos: (l, pos[1] * nb + i, j))
        t_spec = pl.BlockSpec((None, tr, n), lambda l, i, j, pos: (l, i, j))
    return _prefetch_call(body, pos, [g, theirs], name=name, grid=grid, in_specs=[g_spec, t_spec],
                          out_specs=[t_spec, t_spec],
                          out_shape=[jax.ShapeDtypeStruct(theirs.shape, F32), jax.ShapeDtypeStruct(theirs.shape, BF16)],
                          sem=("parallel",) * len(grid))


def chip_add(pair, landed, pos, buf, li, n_layers, *, row_sharded, name):
    def body(p_ref, a_ref, b_ref, c_ref, *rest):
        rest[-1][...] = ((p_ref[...] + a_ref[...].astype(F32)) + b_ref[...].astype(F32)) + c_ref[...].astype(F32)

    if row_sharded:
        _, _, r2, D = pair.shape
        grid = (1,)
        p_spec = pl.BlockSpec((None, None, r2, D), lambda i, pos: (0, pos[0], 0, 0))
        l_specs = [pl.BlockSpec((None, r2, D), functools.partial(lambda i, pos, k: (k, 0, 0), k=k)) for k in range(3)]
        out_spec = pl.BlockSpec((None, None, r2, D), lambda i, pos: (li, pos[1], 0, 0))
        out_shape = jax.ShapeDtypeStruct((n_layers, 2, r2, D), F32)
    else:
        _, K2, N = pair.shape
        n = N // N_CHIPS
        tr = _pick(K2, (256, 128))
        nb = K2 // tr
        grid = (nb,)
        p_spec = pl.BlockSpec((None, tr, n), lambda i, pos: (0, i, pos[0]))
        l_specs = [pl.BlockSpec((None, tr, n), functools.partial(lambda i, pos, k: (k, i, 0), k=k)) for k in range(3)]
        out_spec = pl.BlockSpec((None, tr, n), lambda i, pos: (li, pos[1] * nb + i, 0))
        out_shape = jax.ShapeDtypeStruct((n_layers, 2 * K2, n), F32)
    args, in_specs, aliases = [pair, landed, landed, landed], [p_spec] + l_specs, None
    if buf is not None:
        args.append(buf)
        in_specs.append(pl.BlockSpec(memory_space=pl.ANY))
        aliases = {4: 0}
    return _prefetch_call(body, pos, args, name=name, grid=grid, in_specs=in_specs, out_specs=out_spec,
                          out_shape=out_shape, sem=("parallel",), aliases=aliases)


def _position():
    return lax.axis_index("x"), lax.axis_index("y"), lax.axis_index("c")


def _other_chips(x, y):
    return [(1 - x, y), (x, 1 - y), (1 - x, 1 - y)]


HBM_ANY = pl.BlockSpec(memory_space=pl.ANY)
COMM_PARAMS = pltpu.CompilerParams(has_side_effects=True)


class Comm:
    def __init__(self, ro=(), inout=(), new=(), sems=(), phases=()):
        self.ro, self.inout, self.new, self.sems, self.phases = list(ro), list(inout), list(new), list(sems), list(phases)


def _carried_call(body, args, *, name, grid, in_specs, out_specs, out_shape, scratch_shapes=(), comm=None):
    comm = comm or Comm()
    n_in, n_out, n_sc = len(in_specs), len(out_shape), len(scratch_shapes)
    n_ro, n_io, n_new, n_sem = len(comm.ro), len(comm.inout), len(comm.new), len(comm.sems)
    last = int(np.prod(grid)) - 1

    def wrapped(*refs):
        bounds = np.cumsum([0, n_in, n_ro, n_io, n_out, n_io, n_new, n_sc, n_sem])
        core_in, ro, _, core_out, io, new, core_sc, sems = (refs[a:b] for a, b in zip(bounds[:-1], bounds[1:]))
        step = pl.program_id(0)
        for d in range(1, len(grid)):
            step = step * grid[d] + pl.program_id(d)
        for frac, fn in comm.phases[:-1]:
            pl.when(step == int(frac * last))(functools.partial(fn, ro, io, new, sems))
        body(*core_in, *core_out, *core_sc)
        if comm.phases:
            pl.when(step == last)(functools.partial(comm.phases[-1][1], ro, io, new, sems))

    outs = pl.pallas_call(
        wrapped, name=name, grid=grid,
        in_specs=list(in_specs) + [HBM_ANY] * (n_ro + n_io),
        out_specs=list(out_specs) + [HBM_ANY] * (n_io + n_new),
        out_shape=list(out_shape) + [jax.ShapeDtypeStruct(a.shape, a.dtype) for a in comm.inout] + comm.new,
        input_output_aliases={n_in + n_ro + j: n_out + j for j in range(n_io)},
        scratch_shapes=list(scratch_shapes) + comm.sems,
        compiler_params=pltpu.CompilerParams(dimension_semantics=("arbitrary",) * len(grid),
                                             vmem_limit_bytes=VMEM_LIMIT_BYTES, has_side_effects=bool(comm.phases)),
    )(*args, *comm.ro, *comm.inout)
    return outs[:n_out], outs[n_out:n_out + n_io], outs[n_out + n_io:]


def run_comm(comm, *, name):
    n_ro, n_io, n_new = len(comm.ro), len(comm.inout), len(comm.new)

    def body(*refs):
        bounds = np.cumsum([0, n_ro, n_io, n_io, n_new, len(comm.sems)])
        ro, _, io, new, sems = (refs[a:b] for a, b in zip(bounds[:-1], bounds[1:]))
        for _, fn in comm.phases:
            fn(ro, io, new, sems)

    outs = pl.pallas_call(
        body, name=name, in_specs=[HBM_ANY] * (n_ro + n_io), out_specs=[HBM_ANY] * (n_io + n_new),
        out_shape=[jax.ShapeDtypeStruct(a.shape, a.dtype) for a in comm.inout] + comm.new,
        input_output_aliases={n_ro + j: j for j in range(n_io)},
        scratch_shapes=comm.sems, compiler_params=COMM_PARAMS,
    )(*comm.ro, *comm.inout)
    return outs[:n_io], outs[n_io:]


def _rdma(src, dst, send_sem, recv_sem, device):
    return pltpu.make_async_remote_copy(src_ref=src, dst_ref=dst, send_sem=send_sem, recv_sem=recv_sem,
                                        device_id=device, device_id_type=MESH)


def gather_comm(bufs, slices):
    n = len(slices)

    def region(io, s, h, chip):
        b, li, row_sharded = slices[s]
        ref = io[b]
        if row_sharded:
            r = ref.shape[1] // N_CHIPS
            return ref.at[li, pl.ds(chip * r + h * (r // 2), r // 2), :]
        K, nn = ref.shape[1], ref.shape[2] // N_CHIPS
        return ref.at[li, pl.ds(h * (K // 2), K // 2), pl.ds(chip * nn, nn)]

    def copies(io, sems):
        x, y, c = _position()
        me = 2 * x + y
        chips = _other_chips(x, y)
        ids = [2 * cx + cy for cx, cy in chips]

        def ici(s, k, src):
            return _rdma(region(io, s, c, me), region(io, s, c, src), sems[0].at[s, k], sems[1].at[s, k], (*chips[k], c))

        def d2d(s, k, h):
            return _rdma(region(io, s, h, ids[k]), region(io, s, h, ids[k]), sems[2].at[s, k], sems[3].at[s, k],
                         (x, y, 1 - c))

        return me, ids, c, ici, d2d

    pairs = [(s, k) for s in range(n) for k in range(3)]

    def start(ro, io, new, sems):
        me, _, _, ici, _ = copies(io, sems)
        for s, k in pairs:
            ici(s, k, me).start()

    def pass_on(ro, io, new, sems):
        _, ids, c, ici, d2d = copies(io, sems)
        for s, k in pairs:
            ici(s, k, ids[k]).wait_recv()
            d2d(s, k, c).start()

    def finish(ro, io, new, sems):
        me, _, c, ici, d2d = copies(io, sems)
        for s, k in pairs:
            d2d(s, k, 1 - c).wait_recv()
        for s, k in pairs:
            ici(s, k, me).wait_send()
            d2d(s, k, c).wait_send()

    return Comm(inout=bufs, sems=[pltpu.SemaphoreType.DMA((n, 3))] * 4,
                phases=[(0.0, start), (0.6, pass_on), (1.0, finish)])


def gather_small(shard, *, name):
    n = shard.shape[2]

    def body(s_ref, o_ref, send_sem, recv_sem, local_sem):
        x, y, c = _position()
        me = 2 * x + y
        chips = _other_chips(x, y)
        place = lambda chip: o_ref.at[:, :, pl.ds(chip * n, n)]
        local = pltpu.make_async_copy(s_ref, place(me), local_sem)
        local.start()
        for k in range(3):
            _rdma(s_ref, place(me), send_sem.at[k], recv_sem.at[k], (*chips[k], c)).start()
        for k in range(3):
            _rdma(s_ref, place(2 * chips[k][0] + chips[k][1]), send_sem.at[k], recv_sem.at[k], (*chips[k], c)).wait()
        local.wait()

    return pl.pallas_call(
        body, name=name, in_specs=[HBM_ANY], out_specs=HBM_ANY,
        out_shape=jax.ShapeDtypeStruct((shard.shape[0], shard.shape[1], N_CHIPS * n), shard.dtype),
        scratch_shapes=[pltpu.SemaphoreType.DMA((3,)), pltpu.SemaphoreType.DMA((3,)), pltpu.SemaphoreType.DMA],
        compiler_params=COMM_PARAMS,
    )(shard)


def swap_comm(grads):
    def copies(ro, new, sems):
        x, y, c = _position()
        h = 1 - c
        cps = []
        for s, (a, row_sharded) in enumerate(grads):
            if row_sharded:
                r = a.shape[1] // N_CHIPS
                for q in range(N_CHIPS):
                    cps.append(_rdma(ro[s].at[:, pl.ds(q * r + h * (r // 2), r // 2), :], new[s].at[:, q],
                                     sems[0].at[s, q], sems[1].at[s, q], (x, y, 1 - c)))
            else:
                K = a.shape[1]
                cps.append(_rdma(ro[s].at[:, pl.ds(h * (K // 2), K // 2), :], new[s],
                                 sems[0].at[s, 0], sems[1].at[s, 0], (x, y, 1 - c)))
        return cps

    def start(ro, io, new, sems):
        for cp in copies(ro, new, sems):
            cp.start()

    def finish(ro, io, new, sems):
        for cp in copies(ro, new, sems):
            cp.wait()

    theirs = []
    for a, row_sharded in grads:
        if row_sharded:
            theirs.append(jax.ShapeDtypeStruct((1, N_CHIPS, a.shape[1] // N_CHIPS // 2, a.shape[2]), a.dtype))
        else:
            theirs.append(jax.ShapeDtypeStruct((1, a.shape[1] // 2, a.shape[2]), a.dtype))
    return Comm(ro=[a for a, _ in grads], new=theirs, sems=[pltpu.SemaphoreType.DMA((len(grads), N_CHIPS))] * 2,
                phases=[(0.0, start), (1.0, finish)])


def merge_comms(comms):
    spans, ro, io, new, sems = [], [], [], [], []
    for cm in comms:
        spans.append((len(ro), len(io), len(new), len(sems)))
        ro, io, new, sems = ro + cm.ro, io + cm.inout, new + cm.new, sems + cm.sems

    def phase(frac):
        def run(R, I, N, S):
            for cm, (a, b, d, e) in zip(comms, spans):
                for f, fn in cm.phases:
                    if f == frac:
                        fn(R[a:a + len(cm.ro)], I[b:b + len(cm.inout)], N[d:d + len(cm.new)], S[e:e + len(cm.sems)])
        return run

    fracs = sorted({f for cm in comms for f, _ in cm.phases})
    merged = Comm(ro=ro, inout=io, new=new, sems=sems, phases=[(f, phase(f)) for f in fracs])
    cut = lambda created: [created[d:d + len(cm.new)] for cm, (_, _, d, _) in zip(comms, spans)]
    return merged, cut


def scatter_comm(halves):
    def copies(ro, new, sems):
        x, y, c = _position()
        chips = _other_chips(x, y)
        cps = []
        for s, (a, row_sharded) in enumerate(halves):
            for k in range(3):
                to = 2 * chips[k][0] + chips[k][1]
                if row_sharded:
                    src = ro[s].at[0, to]
                else:
                    n = a.shape[2] // N_CHIPS
                    src = ro[s].at[0, :, pl.ds(to * n, n)]
                cps.append(_rdma(src, new[s].at[k], sems[0].at[s, k], sems[1].at[s, k], (*chips[k], c)))
        return cps

    def start(ro, io, new, sems):
        for cp in copies(ro, new, sems):
            cp.start()

    def finish(ro, io, new, sems):
        for cp in copies(ro, new, sems):
            cp.wait()

    landing = [jax.ShapeDtypeStruct((3, a.shape[2], a.shape[3]) if row_sharded else
                                    (3, a.shape[1], a.shape[2] // N_CHIPS), a.dtype) for a, row_sharded in halves]
    return Comm(ro=[a for a, _ in halves], new=landing, sems=[pltpu.SemaphoreType.DMA((len(halves), 3))] * 2,
                phases=[(0.0, start), (1.0, finish)])


def sibling_join_halves(shards, *, name):
    n_in = len(shards)

    def body(*refs):
        ins = refs[:n_in]
        outs = refs[n_in:2 * n_in]
        send_sem, recv_sem = refs[2 * n_in:]
        x, y, c = _position()
        sibling = (x, y, 1 - c)

        def rows(ref, h):
            R2 = ref.shape[1] // 2
            return ref.at[:, pl.ds(h * R2, R2), :]

        sends = [pltpu.make_async_remote_copy(src_ref=rows(ins[i], c), dst_ref=rows(outs[i], c),
                                              send_sem=send_sem.at[i], recv_sem=recv_sem.at[i],
                                              device_id=sibling, device_id_type=MESH) for i in range(n_in)]
        lands = [pltpu.make_async_remote_copy(src_ref=rows(ins[i], c), dst_ref=rows(outs[i], 1 - c),
                                              send_sem=send_sem.at[i], recv_sem=recv_sem.at[i],
                                              device_id=sibling, device_id_type=MESH) for i in range(n_in)]
        for cp in sends:
            cp.start()
        for i in range(n_in):
            sends[i].wait_send()
            lands[i].wait_recv()

    return pl.pallas_call(
        body, name=name, in_specs=[HBM_ANY] * n_in, out_specs=[HBM_ANY] * n_in,
        out_shape=[jax.ShapeDtypeStruct(a.shape, a.dtype) for a in shards],
        input_output_aliases={i: i for i in range(n_in)},
        scratch_shapes=[pltpu.SemaphoreType.DMA((n_in,)), pltpu.SemaphoreType.DMA((n_in,))],
        compiler_params=COMM_PARAMS,
    )(*shards)


def all_reduce_small(v, *, name):
    R, Ccols = v.shape

    def body(v_ref, o_ref, slots, send_sem, recv_sem):
        x, y, c = _position()
        me = 4 * x + 2 * y + c
        slots[me] = v_ref[...]

        def peer(d):
            return x ^ (d >> 2), y ^ ((d >> 1) & 1), c ^ (d & 1)

        def copy(d, slot):
            return pltpu.make_async_remote_copy(
                src_ref=v_ref, dst_ref=slots.at[slot], send_sem=send_sem.at[d], recv_sem=recv_sem.at[d],
                device_id=peer(d), device_id_type=MESH)

        sends = [copy(d, me) for d in range(1, N_DEV)]
        for cp in sends:
            cp.start()
        for d in range(1, N_DEV):
            px, py, pc = peer(d)
            copy(d, 4 * px + 2 * py + pc).wait_recv()
        for cp in sends:
            cp.wait_send()
        acc = slots[0]
        for d in range(1, N_DEV):
            acc = acc + slots[d]
        o_ref[...] = acc

    return pl.pallas_call(
        body, name=name,
        in_specs=[pl.BlockSpec(memory_space=pltpu.VMEM)], out_specs=pl.BlockSpec(memory_space=pltpu.VMEM),
        out_shape=jax.ShapeDtypeStruct((R, Ccols), F32),
        scratch_shapes=[pltpu.VMEM((N_DEV, R, Ccols), F32), pltpu.SemaphoreType.DMA((N_DEV,)),
                        pltpu.SemaphoreType.DMA((N_DEV,))],
        compiler_params=COMM_PARAMS,
    )(v)


def _pack_small(parts):
    flat = jnp.concatenate([p.reshape(-1).astype(F32) for p in parts])
    n = flat.shape[0]
    rows = -(-n // LANES)
    rows = -(-rows // SUBLANES) * SUBLANES
    return jnp.pad(flat, (0, rows * LANES - n)).reshape(rows, LANES)


def _unpack_small(packed, like):
    flat = packed.reshape(-1)
    out, off = [], 0
    for p in like:
        out.append(flat[off:off + p.size].reshape(p.shape))
        off += p.size
    return out


def kernel(x, norm_mix, norm_ffn, norm_final, attn_w_in, attn_w_out, attn_sinks, hgrn_w_in, hgrn_w_out, hgrn_norm, hgrn_lb_logits, ffn_w_up, ffn_conv_w, ffn_conv_b, ffn_w_down, loss_target, m_norm_mix, m_norm_ffn, m_norm_final, m_attn_w_in, m_attn_w_out, m_attn_sinks, m_hgrn_w_in, m_hgrn_w_out, m_hgrn_norm, m_hgrn_lb_logits, m_ffn_w_up, m_ffn_conv_w, m_ffn_conv_b, m_ffn_w_down, v_norm_mix, v_norm_ffn, v_norm_final, v_attn_w_in, v_attn_w_out, v_attn_sinks, v_hgrn_w_in, v_hgrn_w_out, v_hgrn_norm, v_hgrn_lb_logits, v_ffn_w_up, v_ffn_conv_w, v_ffn_conv_b, v_ffn_w_down):
    S, D = x.shape[1], x.shape[2]
    depth = norm_mix.shape[0]
    n_q = D // HEAD_DIM
    n_kv = n_q // Q_PER_KV
    Fh = ffn_w_down.shape[1] * N_CHIPS
    h = x.reshape(S, D)
    target = loss_target.reshape(S, D)

    col_names = ["attn_w_in", "hgrn_w_in", "ffn_w_up"]
    row_names = ["attn_w_out", "hgrn_w_out", "ffn_w_down"]
    x_, y_, c_ = _position()
    me = 2 * x_ + y_
    pos = jnp.stack([me, c_]).astype(jnp.int32)
    col_w = [cast_place(w, pos, row_sharded=False, name=f"cast_place_{n}")
             for n, w in zip(col_names, (attn_w_in, hgrn_w_in, ffn_w_up))]
    row_w = [cast_place(w, pos, row_sharded=True, name=f"cast_place_{n}")
             for n, w in zip(row_names, (attn_w_out, hgrn_w_out, ffn_w_down))]
    W = dict(zip(col_names + row_names, col_w + row_w))
    is_row = {n: n in row_names for n in W}

    def layer_weights(layer):
        mixer = "attn" if layer % 2 == 0 else "hgrn"
        return [(f"{mixer}_w_in", layer // 2), (f"{mixer}_w_out", layer // 2), ("ffn_w_up", layer), ("ffn_w_down", layer)]

    def gather_of(group):
        names = [n for n, _ in group]
        return names, gather_comm([W[n] for n in names], [(i, li, is_row[n]) for i, (n, li) in enumerate(group)])

    def mixer_weights(layer):
        return layer_weights(layer)[:2] if layer < depth else []

    names_next, comm_next = gather_of(mixer_weights(0))
    filled, _ = run_comm(comm_next, name="gather_mixer0")
    W.update(zip(names_next, filled))
    cw = gather_small(ffn_conv_w, name="gather_conv_w").reshape(depth, 3, 2, Fh).transpose(0, 2, 1, 3)
    cb = ffn_conv_b.reshape(depth, 2, 1, Fh)
    g_mix = norm_mix.reshape(depth, 1, D)
    g_ffn = norm_ffn.reshape(depth, 1, D)

    slopes = jnp.exp2(-8.0 * jnp.arange(1, n_q + 1, dtype=F32) / n_q).reshape(n_kv, Q_PER_KV)
    slope_col = jnp.repeat(slopes, WINDOW, axis=1).reshape(n_kv, Q_PER_KV * WINDOW, 1)

    def to_heads(t, nh):
        return t.reshape(S, nh, HEAD_DIM).transpose(1, 0, 2)

    def from_heads(t):
        return t.transpose(1, 0, 2).reshape(S, -1)

    saved = []
    for layer in range(depth):
        idx = layer // 2
        rec = {"h_mix": h}
        hn, rec["hn_t"] = rmsnorm_fwd(h, g_mix, layer, name=f"norm_mix_fwd{layer}")
        is_attn = layer % 2 == 0
        up_l, down_l = layer_weights(layer)[2:]
        on_mixer = [up_l] if is_attn else [up_l, down_l]
        on_up = [down_l] if is_attn else mixer_weights(layer + 1)
        on_act = mixer_weights(layer + 1)[:1] if is_attn else []
        on_down = mixer_weights(layer + 1)[1:] if is_attn else []
        names_next, comm_next = gather_of(on_mixer)
        if is_attn:
            proj = mm_nn(hn, W["attn_w_in"], idx, name=f"attn_in_fwd{layer}", out_dtype=BF16)
            q = to_heads(proj[:, :n_q * HEAD_DIM], n_q).reshape(n_kv, Q_PER_KV, S, HEAD_DIM)
            k = to_heads(proj[:, n_q * HEAD_DIM:(n_q + n_kv) * HEAD_DIM], n_kv)
            v = to_heads(proj[:, (n_q + n_kv) * HEAD_DIM:], n_kv)
            sink_col = jnp.repeat(attn_sinks[idx].reshape(n_kv, Q_PER_KV), WINDOW, axis=1)
            sink_col = sink_col.reshape(n_kv, Q_PER_KV * WINDOW, 1)
            o_heads, filled, _ = attn_fwd(q, k, v, sink_col, slope_col, name=f"attn_fwd{layer}", comm=comm_next)
            W.update(zip(names_next, filled))
            o = from_heads(o_heads.reshape(n_q, S, HEAD_DIM))
            rec.update(q=q, k=k, v=v, sink_col=sink_col, o=o)
            h = mm_nn(o, W["attn_w_out"], idx, name=f"attn_out_fwd{layer}", res=h)
        else:
            proj = mm_nn(hn, W["hgrn_w_in"], idx, name=f"hgrn_in_fwd{layer}")
            ng = hgrn_norm[idx].reshape(1, HG_EXPAND)
            o, states, filled, _ = hgrn_fwd(proj, hgrn_lb_logits, ng, layer, name=f"hgrn_fwd{layer}", comm=comm_next)
            W.update(zip(names_next, filled))
            rec.update(proj=proj, ng=ng, states=states, o=o)
            h = mm_nn(o, W["hgrn_w_out"], idx, name=f"hgrn_out_fwd{layer}", res=h)
        rec["h_ffn"] = h
        hn2, rec["hn2_t"] = rmsnorm_fwd(h, g_ffn, layer, name=f"norm_ffn_fwd{layer}")
        if on_up:
            names_next, comm_next = gather_of(on_up)
            u, filled = mm_nn(hn2, W["ffn_w_up"], layer, name=f"ffn_up_fwd{layer}", split_out=True, out_dtype=BF16,
                              comm=comm_next)
            W.update(zip(names_next, filled))
        else:
            u = mm_nn(hn2, W["ffn_w_up"], layer, name=f"ffn_up_fwd{layer}", split_out=True, out_dtype=BF16)
        if on_act:
            names_next, comm_next = gather_of(on_act)
            act, rec["act_t"], rec["c"], filled = convact_fwd(u, cw, cb, layer, name=f"ffn_act_fwd{layer}",
                                                              comm=comm_next)
            W.update(zip(names_next, filled))
        else:
            act, rec["act_t"], rec["c"] = convact_fwd(u, cw, cb, layer, name=f"ffn_act_fwd{layer}")
        rec["u"] = u
        if on_down:
            names_next, comm_next = gather_of(on_down)
            h, filled = mm_nn(act, W["ffn_w_down"], layer, name=f"ffn_down_fwd{layer}", res=h, comm=comm_next)
            W.update(zip(names_next, filled))
        else:
            h = mm_nn(act, W["ffn_w_down"], layer, name=f"ffn_down_fwd{layer}", res=h)
        saved.append(rec)

    loss_part, dh, d_norm_final = loss_head(h, norm_final.reshape(1, D), target, name="loss_head")

    n_attn, n_hgrn = attn_w_in.shape[0], hgrn_w_in.shape[0]
    reduced_parts = []
    pending = []

    def swap_of(group, g):
        ordered = sorted(group, key=lambda nl: is_row[nl[0]])
        return ordered, swap_comm([(g[n], is_row[n]) for n, _ in ordered])

    def pair_up(tag, ordered, g, theirs):
        pair, pair_bf16 = {}, {}
        for (n, _), t in zip(ordered, theirs):
            gn = g[n]
            if is_row[n]:
                r = gn.shape[1] // N_CHIPS
                gn = gn.reshape(1, N_CHIPS, 2, r // 2, gn.shape[2])
            pair[n], pair_bf16[n] = pair_add(gn, t, pos, row_sharded=is_row[n], name=f"rs_pair_add_{n}{tag}")
        pending.append((ordered, pair, scatter_comm([(pair_bf16[n], is_row[n]) for n, _ in ordered])))

    def landed_behind(created, cut):
        for (ordered, pair, _), landed in zip(pending, cut(created)):
            reduced_parts.append((ordered, pair, landed))
        pending.clear()

    d_norm_mix, d_norm_ffn = [None] * depth, [None] * depth
    d_conv_w, d_conv_b = [None] * depth, [None] * depth
    d_sinks, d_hgrn_norm = [None] * n_attn, [None] * n_hgrn
    d_logits = jnp.zeros_like(hgrn_lb_logits)
    for layer in reversed(range(depth)):
        idx = layer // 2
        rec = saved[layer]
        g = {"ffn_w_down": mm_dw(rec["act_t"], dh, name=f"ffn_down_dw{layer}")}
        dact = mm_nt(dh, W["ffn_w_down"], layer, name=f"ffn_down_dx{layer}")
        du, dcw, dcb = convact_bwd(rec["c"], rec["u"], cw, layer, dact, name=f"ffn_act_bwd{layer}")
        d_conv_w[layer] = dcw.transpose(1, 0, 2).reshape(3, 2 * Fh)
        d_conv_b[layer] = dcb.reshape(2 * Fh)
        g["ffn_w_up"] = mm_dw(rec["hn2_t"], du, name=f"ffn_up_dw{layer}", split_b=True)
        ordered, swap = swap_of(layer_weights(layer)[2:], g)
        dh, dg, theirs = mm_nt(du, W["ffn_w_up"], layer, name=f"ffn_up_dx{layer}", split_in=True, comm=swap,
                               norm=(rec["h_ffn"], g_ffn, layer, dh))
        pair_up(layer, ordered, g, theirs)
        d_norm_ffn[layer] = dg.reshape(D)
        carried, cut = merge_comms([p[2] for p in pending])
        if layer % 2 == 0:
            g["attn_w_out"] = mm_tn(rec["o"], dh, name=f"attn_out_dw{layer}")
            do = mm_nt(dh, W["attn_w_out"], idx, name=f"attn_out_dx{layer}", out_dtype=BF16)
            do_heads = to_heads(do, n_q).reshape(n_kv, Q_PER_KV, S, HEAD_DIM)
            res = attn_bwd(rec["q"], rec["k"], rec["v"], rec["sink_col"], slope_col, do_heads,
                           name=f"attn_bwd{layer}", comm=carried)
            dq, dk, dv, dsink = res[:4]
            landed_behind(res[-1], cut)
            dproj = jnp.concatenate([from_heads(dq.reshape(n_q, S, HEAD_DIM)), from_heads(dk), from_heads(dv)], axis=1)
            d_sinks[idx] = jnp.sum(dsink.reshape(n_kv, Q_PER_KV, WINDOW), axis=-1).reshape(n_q)
            g["attn_w_in"] = mm_dw(rec["hn_t"], dproj, name=f"attn_in_dw{layer}")
            ordered, swap = swap_of(layer_weights(layer)[:2], g)
            dh, dg, theirs = mm_nt(dproj, W["attn_w_in"], idx, name=f"attn_in_dx{layer}", comm=swap,
                                   norm=(rec["h_mix"], g_mix, layer, dh))
        else:
            g["hgrn_w_out"] = mm_tn(rec["o"], dh, name=f"hgrn_out_dw{layer}")
            do = mm_nt(dh, W["hgrn_w_out"], idx, name=f"hgrn_out_dx{layer}")
            res = hgrn_bwd(rec["proj"], hgrn_lb_logits, rec["ng"], rec["states"], do, layer,
                           name=f"hgrn_bwd{layer}", comm=carried)
            dproj, dlg, dng = res[:3]
            landed_behind(res[-1], cut)
            d_logits = d_logits + dlg
            d_hgrn_norm[idx] = dng.reshape(HG_EXPAND)
            g["hgrn_w_in"] = mm_dw(rec["hn_t"], dproj, name=f"hgrn_in_dw{layer}")
            ordered, swap = swap_of(layer_weights(layer)[:2], g)
            dh, dg, theirs = mm_nt(dproj, W["hgrn_w_in"], idx, name=f"hgrn_in_dx{layer}", comm=swap,
                                   norm=(rec["h_mix"], g_mix, layer, dh))
        pair_up(layer, ordered, g, theirs)
        d_norm_mix[layer] = dg.reshape(D)
    grad_x = dh.reshape(x.shape)
    carried, cut = merge_comms([p[2] for p in pending])
    landed_behind(run_comm(carried, name="rs_scatter_last")[1], cut)

    small_w = [norm_mix, norm_ffn, norm_final, attn_sinks, hgrn_norm, hgrn_lb_logits, ffn_conv_b]
    small_m = [m_norm_mix, m_norm_ffn, m_norm_final, m_attn_sinks, m_hgrn_norm, m_hgrn_lb_logits, m_ffn_conv_b]
    small_v = [v_norm_mix, v_norm_ffn, v_norm_final, v_attn_sinks, v_hgrn_norm, v_hgrn_lb_logits, v_ffn_conv_b]
    small_grads = [jnp.stack(d_norm_mix), jnp.stack(d_norm_ffn), d_norm_final.reshape(D), jnp.stack(d_sinks),
                   jnp.stack(d_hgrn_norm), d_logits, jnp.stack(d_conv_b)]
    gconv = jnp.stack(d_conv_w)
    packed_small = _pack_small(small_grads + [loss_part])
    summed = all_reduce_small(jnp.concatenate([packed_small, _pack_small([gconv])], axis=0), name="all_reduce_small")
    reduced, gconv_rows = summed[:packed_small.shape[0]], summed[packed_small.shape[0]:]
    small_g = _unpack_small(reduced, small_w + [loss_part])
    loss = small_g.pop().reshape(())
    pad = [jnp.zeros((1, 1), F32)]
    sd, sm_, sv_ = adamw(_pack_small(small_w + pad), reduced, _pack_small(small_m + pad),
                         _pack_small(small_v + pad), name="adamw_small")
    small_d = _unpack_small(sd, small_w)
    small_nm = _unpack_small(sm_, small_w)
    small_nv = _unpack_small(sv_, small_w)

    big_names = col_names + row_names
    big_w = dict(attn_w_in=attn_w_in, hgrn_w_in=hgrn_w_in, ffn_w_up=ffn_w_up, attn_w_out=attn_w_out,
                 hgrn_w_out=hgrn_w_out, ffn_w_down=ffn_w_down)
    half_shard = dict.fromkeys(big_names)
    for ordered, pair, landed in reduced_parts:
        for (n, li), ld in zip(ordered, landed):
            half_shard[n] = chip_add(pair[n], ld, pos, half_shard[n], li, big_w[n].shape[0], row_sharded=is_row[n],
                                     name=f"rs_chip_add_{n}{li}")
    shards = sibling_join_halves([half_shard[n].reshape(big_w[n].shape) for n in big_names], name="rs_sibling_join")
    big_m = dict(attn_w_in=m_attn_w_in, hgrn_w_in=m_hgrn_w_in, ffn_w_up=m_ffn_w_up, attn_w_out=m_attn_w_out,
                 hgrn_w_out=m_hgrn_w_out, ffn_w_down=m_ffn_w_down)
    big_v = dict(attn_w_in=v_attn_w_in, hgrn_w_in=v_hgrn_w_in, ffn_w_up=v_ffn_w_up, attn_w_out=v_attn_w_out,
                 hgrn_w_out=v_hgrn_w_out, ffn_w_down=v_ffn_w_down)
    grads, deltas, new_m, new_v = {}, {}, {}, {}
    for name_, gshard in zip(big_names, shards):
        w_ = big_w[name_]
        cols = w_.shape[-1]
        d_, m_, v_ = adamw(w_.reshape(-1, cols), gshard.reshape(-1, cols), big_m[name_].reshape(-1, cols),
                           big_v[name_].reshape(-1, cols), name=f"adamw_{name_}")
        grads[name_] = gshard.reshape(w_.shape)
        deltas[name_], new_m[name_], new_v[name_] = (t.reshape(w_.shape) for t in (d_, m_, v_))

    n_conv = ffn_conv_w.shape[2]
    gconv_all = _unpack_small(gconv_rows, [gconv])[0]
    gconv_mine = lax.dynamic_slice_in_dim(gconv_all, me * n_conv, n_conv, axis=2)
    cshape = ffn_conv_w.shape
    d_, m_, v_ = adamw(ffn_conv_w.reshape(-1, n_conv), gconv_mine.reshape(-1, n_conv),
                       m_ffn_conv_w.reshape(-1, n_conv), v_ffn_conv_w.reshape(-1, n_conv), name="adamw_ffn_conv_w")
    grads["ffn_conv_w"] = gconv_mine
    deltas["ffn_conv_w"], new_m["ffn_conv_w"], new_v["ffn_conv_w"] = (t.reshape(cshape) for t in (d_, m_, v_))

    order = ["norm_mix", "norm_ffn", "norm_final", "attn_w_in", "attn_w_out", "attn_sinks", "hgrn_w_in",
             "hgrn_w_out", "hgrn_norm", "hgrn_lb_logits", "ffn_w_up", "ffn_conv_w", "ffn_conv_b", "ffn_w_down"]
    small_names = ["norm_mix", "norm_ffn", "norm_final", "attn_sinks", "hgrn_norm", "hgrn_lb_logits", "ffn_conv_b"]
    for i, name_ in enumerate(small_names):
        grads[name_], deltas[name_], new_m[name_], new_v[name_] = small_g[i], small_d[i], small_nm[i], small_nv[i]
    return (loss, grad_x, *[grads[n] for n in order], *[deltas[n] for n in order],
            *[new_m[n] for n in order], *[new_v[n] for n in order])
```

```python
import functools
import math

import numpy as np
import jax
import jax.numpy as jnp
from jax import lax
from jax.experimental import pallas as pl
from jax.experimental.pallas import tpu as pltpu

F32 = jnp.float32
BF16 = jnp.bfloat16
MESH = pl.DeviceIdType.MESH

HEAD_DIM = 64
Q_PER_KV = 4
WINDOW = 128
HG_EXPAND = 128
HG_CHUNK = 128
EPS = 1e-6
N_CHIPS = 4
N_DEV = 8

ADAM_LR = 0.001
ADAM_B1 = 0.9
ADAM_B2 = 0.999
ADAM_EPS = 1e-08
ADAM_WD = 0.01
ADAM_STEP = 10

VMEM_LIMIT_BYTES = 48 * 1024 * 1024
LANES = 128
SUBLANES = 8
HALO_ROWS = 16


def _params(sem=None):
    return pltpu.CompilerParams(dimension_semantics=sem, vmem_limit_bytes=VMEM_LIMIT_BYTES)


def _pick(n, cands):
    for c in cands:
        if n % c == 0:
            return c
    return n


def _dot(a, b, dims):
    return lax.dot_general(a.astype(BF16), b.astype(BF16), (dims, ((), ())),
                           preferred_element_type=F32)


NN = ((1,), (0,))
NT = ((1,), (1,))
TN = ((0,), (0,))


def mm_nn(a, w, l, *, name, res=None, out_dtype=F32, split_out=False, comm=None):
    M, K = a.shape
    N = w.shape[2]
    tk = K if K <= 2816 else _pick(K, (1024, 1408, 512, 256, 128))
    tm = _pick(M, (1024, 512, 256, 128) if tk <= 1408 else (512, 256, 128))
    tn = _pick(N // 2 if split_out else N, (1408, 1024, 768, 512, 256, 128))
    nk = K // tk
    nh = (N // 2) // tn

    def body(*refs):
        if res is None:
            a_ref, w_ref, o_ref = refs[:3]
            r_ref = None
        else:
            a_ref, w_ref, r_ref, o_ref = refs[:4]
        part = _dot(a_ref[...], w_ref[...], NN)

        def finish(acc):
            if r_ref is not None:
                acc = acc + r_ref[...]
            o_ref[...] = acc.astype(out_dtype)

        if nk == 1:
            finish(part)
        else:
            acc_ref = refs[-1]
            k = pl.program_id(2)

            @pl.when(k == 0)
            def _():
                acc_ref[...] = part

            @pl.when(k > 0)
            def _():
                acc_ref[...] += part

            @pl.when(k == nk - 1)
            def _():
                finish(acc_ref[...])

    in_specs = [pl.BlockSpec((tm, tk), lambda n, m, k: (m, k)),
                pl.BlockSpec((None, tk, tn), lambda n, m, k: (l, k, n))]
    args = [a, w]
    if res is not None:
        in_specs.append(pl.BlockSpec((tm, tn), lambda n, m, k: (m, n)))
        args.append(res)
    if split_out:
        out_shape = jax.ShapeDtypeStruct((2, M, N // 2), out_dtype)
        out_spec = pl.BlockSpec((None, tm, tn), lambda n, m, k: (n // nh, m, n % nh))
    else:
        out_shape = jax.ShapeDtypeStruct((M, N), out_dtype)
        out_spec = pl.BlockSpec((tm, tn), lambda n, m, k: (m, n))
    outs, updated, _ = _carried_call(
        body, args, name=name, grid=(N // tn, M // tm, nk), in_specs=in_specs, out_specs=[out_spec],
        out_shape=[out_shape], scratch_shapes=[] if nk == 1 else [pltpu.VMEM((tm, tn), F32)], comm=comm)
    return outs[0] if comm is None else (outs[0], updated)


def mm_nt(a, w, l, *, name, split_in=False, out_dtype=F32, comm=None, norm=None):
    if split_in:
        _, M, Nh = a.shape
        N = 2 * Nh
    else:
        M, N = a.shape
        Nh = N
    K = w.shape[1]
    tm = _pick(M, (512, 256, 128) if norm is not None else (1024, 512, 256, 128))
    tko = _pick(K, (1024, 1408, 512, 256, 128))
    tr = _pick(Nh, (2816, 2048, 1536, 1408, 1024, 768, 512, 256, 128) if tm <= 512 else (1024, 1408, 768, 512, 256, 128))
    nr = N // tr
    nh = Nh // tr
    n_in, n_out = (5, 2) if norm is not None else (2, 1)
    assert norm is None or tko == K

    def body(*refs):
        a_ref, w_ref = refs[:2]
        o_ref = refs[n_in]
        part = _dot(a_ref[...], w_ref[...], NT)

        def finish(acc):
            if norm is None:
                o_ref[...] = acc.astype(out_dtype)
                return
            h_ref, g_ref, dres_ref = refs[2:5]
            dg_ref = refs[n_in + 1]
            _, vjp = jax.vjp(_rms, h_ref[...], g_ref[...])
            dx, dg = vjp(acc)
            o_ref[...] = dres_ref[...] + dx
            first = pl.program_id(1) == 0

            @pl.when(first)
            def _():
                dg_ref[...] = dg

            @pl.when(jnp.logical_not(first))
            def _():
                dg_ref[...] += dg

        if nr == 1:
            finish(part)
        else:
            acc_ref = refs[n_in + n_out]
            r = pl.program_id(2)

            @pl.when(r == 0)
            def _():
                acc_ref[...] = part

            @pl.when(r > 0)
            def _():
                acc_ref[...] += part

            @pl.when(r == nr - 1)
            def _():
                finish(acc_ref[...])

    if split_in:
        a_spec = pl.BlockSpec((None, tm, tr), lambda ko, m, r: (r // nh, m, r % nh))
    else:
        a_spec = pl.BlockSpec((tm, tr), lambda ko, m, r: (m, r))
    args = [a, w]
    in_specs = [a_spec, pl.BlockSpec((None, tko, tr), lambda ko, m, r: (l, ko, r))]
    out_specs = [pl.BlockSpec((tm, tko), lambda ko, m, r: (m, ko))]
    out_shape = [jax.ShapeDtypeStruct((M, K), out_dtype)]
    if norm is not None:
        h, g, l_norm, dres = norm
        row = pl.BlockSpec((tm, K), lambda ko, m, r: (m, 0))
        args += [h, g, dres]
        in_specs += [row, pl.BlockSpec((None, 1, K), lambda ko, m, r: (l_norm, 0, 0)), row]
        out_specs.append(pl.BlockSpec((1, K), lambda ko, m, r: (0, 0)))
        out_shape.append(jax.ShapeDtypeStruct((1, K), F32))
    outs, _, created = _carried_call(
        body, args, name=name, grid=(K // tko, M // tm, nr), in_specs=in_specs, out_specs=out_specs,
        out_shape=out_shape, scratch_shapes=[] if nr == 1 else [pltpu.VMEM((tm, tko), F32)], comm=comm)
    outs = list(outs) + ([] if comm is None else [created])
    return outs[0] if len(outs) == 1 else tuple(outs)


def mm_tn(a, b, *, name):
    M, K = a.shape
    N = b.shape[1]
    tm = _pick(M, (512, 256, 128))
    tko = _pick(K, (1024, 1408, 704, 512, 256, 128))
    tn = _pick(N, (512, 256, 128))
    nm = M // tm

    def body(a_ref, b_ref, o_ref):
        part = _dot(a_ref[...], b_ref[...], TN)
        r = pl.program_id(2)

        @pl.when(r == 0)
        def _():
            o_ref[...] = part

        @pl.when(r > 0)
        def _():
            o_ref[...] += part

    return pl.pallas_call(
        body, name=name, grid=(K // tko, N // tn, nm),
        in_specs=[pl.BlockSpec((tm, tko), lambda ko, n, r: (r, ko)), pl.BlockSpec((tm, tn), lambda ko, n, r: (r, n))],
        out_specs=pl.BlockSpec((None, tko, tn), lambda ko, n, r: (0, ko, n)),
        out_shape=jax.ShapeDtypeStruct((1, K, N), F32),
        compiler_params=_params(("parallel", "parallel", "arbitrary")),
    )(a, b)


def mm_dw(at, b, *, name, split_b=False):
    K, M = at.shape
    if split_b:
        Nh = b.shape[2]
        N = 2 * Nh
    else:
        N = b.shape[1]
        Nh = N
    tm = _pick(M, (4096, 2048, 1024, 512, 256, 128))
    tko = _pick(K, (1024, 704, 512, 256, 128))
    tn = _pick(Nh, (512, 256, 128))
    nm = M // tm
    nh = Nh // tn

    def body(a_ref, b_ref, o_ref):
        part = _dot(a_ref[...], b_ref[...], NN)
        r = pl.program_id(2)

        @pl.when(r == 0)
        def _():
            o_ref[...] = part

        @pl.when(r > 0)
        def _():
            o_ref[...] += part

    if split_b:
        b_spec = pl.BlockSpec((None, tm, tn), lambda ko, n, r: (n // nh, r, n % nh))
    else:
        b_spec = pl.BlockSpec((tm, tn), lambda ko, n, r: (r, n))
    return pl.pallas_call(
        body, name=name, grid=(K // tko, N // tn, nm),
        in_specs=[pl.BlockSpec((tko, tm), lambda ko, n, r: (ko, r)), b_spec],
        out_specs=pl.BlockSpec((None, tko, tn), lambda ko, n, r: (0, ko, n)),
        out_shape=jax.ShapeDtypeStruct((1, K, N), F32),
        compiler_params=_params(("parallel", "parallel", "arbitrary")),
    )(at, b)


def _rms(x, g):
    return x * lax.rsqrt(jnp.mean(x * x, axis=-1, keepdims=True) + EPS) * g


def rmsnorm_fwd(h, g, l, *, name):
    S, D = h.shape
    tm = _pick(S, (512, 256, 128))

    def body(h_ref, g_ref, o_ref, ot_ref):
        y = _rms(h_ref[...], g_ref[...])
        o_ref[...] = y.astype(BF16)
        ot_ref[...] = y.T.astype(BF16)

    return pl.pallas_call(
        body, name=name, grid=(S // tm,),
        in_specs=[pl.BlockSpec((tm, D), lambda m: (m, 0)),
                  pl.BlockSpec((None, 1, D), lambda m: (l, 0, 0))],
        out_specs=[pl.BlockSpec((tm, D), lambda m: (m, 0)), pl.BlockSpec((D, tm), lambda m: (0, m))],
        out_shape=[jax.ShapeDtypeStruct((S, D), BF16), jax.ShapeDtypeStruct((D, S), BF16)],
        compiler_params=_params(("parallel",)),
    )(h, g)


def loss_head(h, g, target, *, name):
    S, D = h.shape
    tm = _pick(S, (512, 256, 128))

    def body(h_ref, g_ref, t_ref, loss_ref, dh_ref, dg_ref):
        y, vjp = jax.vjp(_rms, h_ref[...], g_ref[...])
        err = y - t_ref[...]
        part = 0.5 * jnp.sum(jnp.sum(err * err, axis=-1, keepdims=True) / D, axis=0, keepdims=True)
        dx, dg = vjp(err / D)
        dh_ref[...] = dx

        @pl.when(pl.program_id(0) == 0)
        def _():
            dg_ref[...] = dg
            loss_ref[...] = part

        @pl.when(pl.program_id(0) > 0)
        def _():
            dg_ref[...] += dg
            loss_ref[...] += part

    row = pl.BlockSpec((tm, D), lambda m: (m, 0))
    return pl.pallas_call(
        body, name=name, grid=(S // tm,),
        in_specs=[row, pl.BlockSpec((1, D), lambda m: (0, 0)), row],
        out_specs=[pl.BlockSpec((1, 1), lambda m: (0, 0)), row, pl.BlockSpec((1, D), lambda m: (0, 0))],
        out_shape=[jax.ShapeDtypeStruct((1, 1), F32), jax.ShapeDtypeStruct((S, D), F32),
                   jax.ShapeDtypeStruct((1, D), F32)],
        compiler_params=_params(("arbitrary",)),
    )(h, g, target)


def _make_mm(dims_fwd, dims_da, dims_db, swap_da=False, swap_db=False):
    @jax.custom_vjp
    def mm(a, b):
        return _dot(a, b, dims_fwd)

    def fwd(a, b):
        return mm(a, b), (a, b)

    def bwd(resid, g):
        a, b = resid
        da = _dot(b, g, dims_da) if swap_da else _dot(g, b, dims_da)
        db = _dot(g, a, dims_db) if swap_db else _dot(a, g, dims_db)
        return da, db

    mm.defvjp(fwd, bwd)
    return mm


_mm = _make_mm(NN, NT, TN)
_mm_nt = _make_mm(NT, NN, TN, swap_db=True)
_mm_tn = _make_mm(TN, NT, NN, swap_da=True)


def _dot_hi(a, b):
    return jnp.dot(a, b, precision=lax.Precision.HIGHEST, preferred_element_type=F32)


def _sigmoid(x):
    return 0.5 * jnp.tanh(0.5 * x) + 0.5


def _silu(x):
    return x * _sigmoid(x)


def _row_slices(x, n):
    rows = x.shape[0] // n
    return tuple(x[i * rows:(i + 1) * rows] for i in range(n))


@functools.partial(jax.custom_vjp, nondiff_argnums=(1,))
def _split_rows(x, n):
    return _row_slices(x, n)


_split_rows.defvjp(lambda x, n: (_row_slices(x, n), None), lambda n, _, cts: (jnp.concatenate(cts, axis=0),))


def _attn_block(qs, kws, vws, sink, slope, first):
    s = jnp.concatenate([_mm_nt(q, kw) for q, kw in zip(qs, kws)], axis=0) * (HEAD_DIM ** -0.5)
    qi = lax.broadcasted_iota(jnp.int32, s.shape, 0) % WINDOW
    ki = lax.broadcasted_iota(jnp.int32, s.shape, 1)
    dist = qi + WINDOW - ki
    valid = (dist >= 0) & (dist < WINDOW) & (ki >= WINDOW * first)
    s = jnp.where(valid, s - slope * dist.astype(F32), -jnp.inf)
    m = lax.stop_gradient(jnp.maximum(jnp.max(s, axis=-1, keepdims=True), sink))
    e = jnp.exp(s - m)
    denom = jnp.sum(e, axis=-1, keepdims=True) + jnp.exp(sink - m)
    ps = _split_rows(e * (1.0 / denom), len(qs))
    return tuple(_mm(p, vw) for p, vw in zip(ps, vws))


def _attn_specs(Hkv, order):
    G = Q_PER_KV
    qspec = pl.BlockSpec((Hkv, G, WINDOW, HEAD_DIM), lambda n: (0, 0, order(n), 0))
    prev = pl.BlockSpec((Hkv, WINDOW, HEAD_DIM), lambda n: (0, jnp.maximum(order(n) - 1, 0), 0))
    cur = pl.BlockSpec((Hkv, WINDOW, HEAD_DIM), lambda n: (0, order(n), 0))
    col = pl.BlockSpec((Hkv, G * WINDOW, 1), lambda n: (0, 0, 0))
    return qspec, prev, cur, col


def attn_fwd(q, k, v, sink_col, slope_col, *, name, comm=None):
    Hkv, G, S, _ = q.shape
    qspec, prev, cur, col = _attn_specs(Hkv, lambda n: n)

    def body(q_ref, kp_ref, kc_ref, vp_ref, vc_ref, sink_ref, slope_ref, o_ref):
        first = (pl.program_id(0) == 0).astype(jnp.int32)
        qs = tuple(q_ref[h].reshape(G * WINDOW, HEAD_DIM) for h in range(Hkv))
        kws = tuple(jnp.concatenate([kp_ref[h], kc_ref[h]], axis=0) for h in range(Hkv))
        vws = tuple(jnp.concatenate([vp_ref[h], vc_ref[h]], axis=0) for h in range(Hkv))
        outs = _attn_block(qs, kws, vws, sink_ref[...].reshape(Hkv * G * WINDOW, 1),
                           slope_ref[...].reshape(Hkv * G * WINDOW, 1), first)
        o_ref[...] = jnp.stack([o.reshape(G, WINDOW, HEAD_DIM).astype(BF16) for o in outs])

    outs, updated, created = _carried_call(
        body, [q, k, k, v, v, sink_col, slope_col], name=name, grid=(S // WINDOW,),
        in_specs=[qspec, prev, cur, prev, cur, col, col], out_specs=[qspec],
        out_shape=[jax.ShapeDtypeStruct(q.shape, BF16)], comm=comm)
    return outs[0] if comm is None else (outs[0], updated, created)


def attn_bwd(q, k, v, sink_col, slope_col, do, *, name, comm=None):
    Hkv, G, S, _ = q.shape
    nb = S // WINDOW
    qspec, prev, cur, col = _attn_specs(Hkv, lambda n: nb - 1 - n)

    def body(q_ref, kp_ref, kc_ref, vp_ref, vc_ref, sink_ref, slope_ref, do_ref,
             dq_ref, dk_ref, dv_ref, dsink_ref, dk_carry, dv_carry):
        @pl.when(pl.program_id(0) == 0)
        def _():
            dk_carry[...] = jnp.zeros_like(dk_carry)
            dv_carry[...] = jnp.zeros_like(dv_carry)
            dsink_ref[...] = jnp.zeros_like(dsink_ref)

        first = (pl.program_id(0) == nb - 1).astype(jnp.int32)
        qs = tuple(q_ref[h].reshape(G * WINDOW, HEAD_DIM).astype(F32) for h in range(Hkv))
        kws = tuple(jnp.concatenate([kp_ref[h], kc_ref[h]], axis=0).astype(F32) for h in range(Hkv))
        vws = tuple(jnp.concatenate([vp_ref[h], vc_ref[h]], axis=0).astype(F32) for h in range(Hkv))
        fn = functools.partial(_attn_block, slope=slope_ref[...].reshape(Hkv * G * WINDOW, 1), first=first)
        _, vjp = jax.vjp(fn, qs, kws, vws, sink_ref[...].reshape(Hkv * G * WINDOW, 1))
        dqs, dkws, dvws, dsink = vjp(tuple(do_ref[h].reshape(G * WINDOW, HEAD_DIM).astype(F32) for h in range(Hkv)))
        dkw, dvw = jnp.stack(dkws), jnp.stack(dvws)
        dq_ref[...] = jnp.stack([dq.reshape(G, WINDOW, HEAD_DIM) for dq in dqs]).astype(BF16)
        dk_ref[...] = (dkw[:, WINDOW:] + dk_carry[...]).astype(BF16)
        dv_ref[...] = (dvw[:, WINDOW:] + dv_carry[...]).astype(BF16)
        dk_carry[...] = dkw[:, :WINDOW]
        dv_carry[...] = dvw[:, :WINDOW]
        dsink_ref[...] += dsink.reshape(Hkv, G * WINDOW, 1)

    kv_shape = jax.ShapeDtypeStruct(k.shape, BF16)
    outs, _, created = _carried_call(
        body, [q, k, k, v, v, sink_col, slope_col, do], name=name, grid=(nb,),
        in_specs=[qspec, prev, cur, prev, cur, col, col, qspec],
        out_specs=[qspec, cur, cur, col],
        out_shape=[jax.ShapeDtypeStruct(q.shape, BF16), kv_shape, kv_shape,
                   jax.ShapeDtypeStruct(sink_col.shape, F32)],
        scratch_shapes=[pltpu.VMEM((Hkv, WINDOW, HEAD_DIM), F32), pltpu.VMEM((Hkv, WINDOW, HEAD_DIM), F32)],
        comm=comm)
    return outs if comm is None else (*outs, created)


def _hgrn_consts():
    C = HG_CHUNK
    t = np.arange(C)[:, None]
    j = np.arange(C)[None, :]
    cq, ck, mk = [], [], []
    H = C // 2
    while H >= 1:
        start = (t // H) * H
        cq.append((j > start) & (j <= t))
        ck.append((j > t) & (j <= start + H))
        mk.append(((t // H) % 2 == 1) & ((j // H) == (t // H) - 1))
        H //= 2
    ltri = (j <= t)
    ops = np.concatenate([ltri] + cq + ck, axis=0).astype(np.float32)
    ops3 = np.concatenate([ops, ops, ops], axis=1)
    opst3 = np.concatenate([ops.T, ops.T, ops.T], axis=1)
    return (jnp.asarray(ops3, BF16), jnp.asarray(opst3, BF16), jnp.asarray(np.stack(mk).astype(np.float32)))


def _split3(x):
    hi = x.astype(BF16)
    r = x - hi.astype(F32)
    mid = r.astype(BF16)
    lo = (r - mid.astype(F32)).astype(BF16)
    return jnp.concatenate([hi, mid, lo], axis=0)


@jax.custom_vjp
def _seg_sums(g, ops3, opst3):
    C = g.shape[0]
    out = jnp.dot(ops3, _split3(g), preferred_element_type=F32)
    return tuple(out[i * C:(i + 1) * C] for i in range(ops3.shape[0] // C))


def _seg_sums_fwd(g, ops3, opst3):
    return _seg_sums(g, ops3, opst3), opst3


def _seg_sums_bwd(opst3, cts):
    ct = jnp.concatenate(cts, axis=0)
    return jnp.dot(opst3, _split3(ct), preferred_element_type=F32), None, None


_seg_sums.defvjp(_seg_sums_fwd, _seg_sums_bwd)


def _head_slices(x):
    return tuple(x[:, h * HG_EXPAND:(h + 1) * HG_EXPAND] for h in range(x.shape[1] // HG_EXPAND))


@jax.custom_vjp
def _split_heads(x):
    return _head_slices(x)


_split_heads.defvjp(lambda x: (_head_slices(x), None), lambda _, cts: (jnp.concatenate(cts, axis=1),))


@jax.custom_vjp
def _join_heads(xs):
    return jnp.concatenate(xs, axis=1)


_join_heads.defvjp(lambda xs: (jnp.concatenate(xs, axis=1), None), lambda _, ct: (_head_slices(ct),))


def _hgrn_chunk(qr, fr, iv, gr, sts, logits, ng, ops3, opst3, mk, layer):
    n_levels = mk.shape[0]
    depth, D = logits.shape
    e = jnp.exp(logits - lax.stop_gradient(jnp.max(logits, axis=0, keepdims=True)))
    sm = e / jnp.sum(e, axis=0, keepdims=True)
    row = lax.broadcasted_iota(jnp.int32, (depth, D), 0)
    lb = jnp.sum(jnp.where((row >= 1) & (row <= layer), sm, 0.0), axis=0, keepdims=True)

    q = _silu(qr)
    a = jnp.log(lb)
    c = jnp.log(1.0 - lb) - (jnp.maximum(-fr, 0.0) + jnp.log(1.0 + jnp.exp(-jnp.abs(fr))))
    mx = lax.stop_gradient(jnp.maximum(a, c))
    g = mx + jnp.log(jnp.exp(a - mx) + jnp.exp(c - mx))
    k = (1.0 - lb) * _sigmoid(-fr)

    sums = _seg_sums(g, ops3, opst3)
    b = sums[0]
    b_last = jnp.sum(g, axis=0, keepdims=True)

    ql = [_split_heads(q * jnp.exp(sums[1 + lv])) for lv in range(n_levels)]
    kl = [_split_heads(k * jnp.exp(sums[1 + n_levels + lv])) for lv in range(n_levels)]
    qb, kb = _split_heads(q * jnp.exp(b)), _split_heads(k * jnp.exp(b_last - b))
    qk, ivh, gate, decay = _split_heads(q * k), _split_heads(iv), _split_heads(_silu(gr)), _split_heads(jnp.exp(b_last))
    outs, new_sts = [], []
    for h, st in enumerate(sts):
        amat = None
        for lv in range(n_levels):
            part = _mm_nt(ql[lv][h], kl[lv][h]) * mk[lv]
            amat = part if amat is None else amat + part
        o = jnp.sum(qk[h], axis=-1, keepdims=True) * ivh[h] + _mm(amat, ivh[h]) + _mm_nt(qb[h], st)
        new_sts.append(st * decay[h] + _mm_tn(ivh[h], kb[h]))
        on = o * lax.rsqrt(jnp.mean(o * o, axis=-1, keepdims=True) + EPS) * ng
        outs.append(on * gate[h])
    return _join_heads(tuple(outs)), tuple(new_sts)


def _hgrn_in_specs(D, depth, order):
    C = HG_CHUNK
    nl = int(math.log2(C))
    n_ops = 1 + 2 * nl
    return [pl.BlockSpec((C, 4 * D), lambda c: (order(c), 0)),
            pl.BlockSpec((depth, D), lambda c: (0, 0)),
            pl.BlockSpec((1, HG_EXPAND), lambda c: (0, 0)),
            pl.BlockSpec((n_ops * C, 3 * C), lambda c: (0, 0)),
            pl.BlockSpec((C, 3 * n_ops * C), lambda c: (0, 0)),
            pl.BlockSpec((nl, C, C), lambda c: (0, 0, 0))]


def hgrn_fwd(proj, logits, ng, layer, *, name, comm=None):
    S = proj.shape[0]
    D = proj.shape[1] // 4
    H = D // HG_EXPAND
    C = HG_CHUNK
    nc = S // C
    consts = _hgrn_consts()

    def body(p_ref, lg_ref, ng_ref, ops3_ref, opst3_ref, mk_ref, o_ref, st_out_ref, st_ref):
        @pl.when(pl.program_id(0) == 0)
        def _():
            st_ref[...] = jnp.zeros_like(st_ref)

        st_out_ref[...] = st_ref[...]
        out, new_sts = _hgrn_chunk(*(p_ref[:, j * D:(j + 1) * D] for j in range(4)), tuple(st_ref[h] for h in range(H)),
                                   lg_ref[...], ng_ref[...], ops3_ref[...], opst3_ref[...], mk_ref[...], layer)
        o_ref[...] = out.astype(BF16)
        for h in range(H):
            st_ref[h] = new_sts[h]

    outs, updated, created = _carried_call(
        body, [proj, logits, ng, *consts], name=name, grid=(nc,),
        in_specs=_hgrn_in_specs(D, logits.shape[0], lambda c: c),
        out_specs=[pl.BlockSpec((C, D), lambda c: (c, 0)),
                   pl.BlockSpec((None, H, HG_EXPAND, HG_EXPAND), lambda c: (c, 0, 0, 0))],
        out_shape=[jax.ShapeDtypeStruct((S, D), BF16),
                   jax.ShapeDtypeStruct((nc, H, HG_EXPAND, HG_EXPAND), F32)],
        scratch_shapes=[pltpu.VMEM((H, HG_EXPAND, HG_EXPAND), F32)], comm=comm)
    return outs if comm is None else (*outs, updated, created)


def hgrn_bwd(proj, logits, ng, states, dout, layer, *, name, comm=None):
    S = proj.shape[0]
    D = proj.shape[1] // 4
    H = D // HG_EXPAND
    C = HG_CHUNK
    nc = S // C
    depth = logits.shape[0]
    consts = _hgrn_consts()
    rev = lambda c: nc - 1 - c

    def body(p_ref, lg_ref, ng_ref, ops3_ref, opst3_ref, mk_ref, st_in_ref, do_ref,
             dp_ref, dlg_ref, dng_ref, dst_ref):
        @pl.when(pl.program_id(0) == 0)
        def _():
            dst_ref[...] = jnp.zeros_like(dst_ref)
            dlg_ref[...] = jnp.zeros_like(dlg_ref)
            dng_ref[...] = jnp.zeros_like(dng_ref)

        fn = functools.partial(_hgrn_chunk, ops3=ops3_ref[...], opst3=opst3_ref[...], mk=mk_ref[...], layer=layer)
        _, vjp = jax.vjp(fn, *(p_ref[:, j * D:(j + 1) * D] for j in range(4)), tuple(st_in_ref[h] for h in range(H)),
                         lg_ref[...], ng_ref[...])
        *dproj, dsts, dlg, dng = vjp((do_ref[...], tuple(dst_ref[h] for h in range(H))))
        for j in range(4):
            dp_ref[:, j * D:(j + 1) * D] = dproj[j].astype(BF16)
        for h in range(H):
            dst_ref[h] = dsts[h]
        dlg_ref[...] += dlg
        dng_ref[...] += dng

    in_specs = _hgrn_in_specs(D, depth, rev)
    in_specs += [pl.BlockSpec((None, H, HG_EXPAND, HG_EXPAND), lambda c: (rev(c), 0, 0, 0)),
                 pl.BlockSpec((C, D), lambda c: (rev(c), 0))]
    outs, _, created = _carried_call(
        body, [proj, logits, ng, *consts, states, dout], name=name, grid=(nc,), in_specs=in_specs,
        out_specs=[pl.BlockSpec((C, 4 * D), lambda c: (rev(c), 0)),
                   pl.BlockSpec((depth, D), lambda c: (0, 0)), pl.BlockSpec((1, HG_EXPAND), lambda c: (0, 0))],
        out_shape=[jax.ShapeDtypeStruct((S, 4 * D), BF16), jax.ShapeDtypeStruct((depth, D), F32),
                   jax.ShapeDtypeStruct((1, HG_EXPAND), F32)],
        scratch_shapes=[pltpu.VMEM((H, HG_EXPAND, HG_EXPAND), F32)], comm=comm)
    return outs if comm is None else (*outs, created)


def _shift_down(u, prev, shift):
    rolled = pltpu.roll(u, shift, axis=0)
    rows = lax.broadcasted_iota(jnp.int32, prev.shape, 0)
    top = jnp.where(rows < shift, pltpu.roll(prev, shift, axis=0), rolled[:SUBLANES])
    return jnp.concatenate([top, rolled[SUBLANES:]], axis=0)


def _shift_up(x, nxt, shift):
    tm = x.shape[0]
    rolled = pltpu.roll(x, tm - shift, axis=0)
    rows = lax.broadcasted_iota(jnp.int32, nxt.shape, 0)
    bottom = jnp.where(rows >= SUBLANES - shift, pltpu.roll(nxt, SUBLANES - shift, axis=0), rolled[tm - SUBLANES:])
    return jnp.concatenate([rolled[:tm - SUBLANES], bottom], axis=0)


def _conv(u, prev, w_ref, b_ref, half):
    u1 = _shift_down(u, prev, 1)
    u2 = _shift_down(u, prev, 2)
    return b_ref[half] + w_ref[half, 0:1, :] * u2 + w_ref[half, 1:2, :] * u1 + w_ref[half, 2:3, :] * u


def _halo_rows(ref, keep, last):
    full = ref.astype(F32)
    return (full[HALO_ROWS - SUBLANES:] if last else full[:SUBLANES]) * keep


def _conv_specs(tm, tc, l):
    nprev = lambda m: jnp.maximum(m * (tm // HALO_ROWS) - 1, 0)
    tile = pl.BlockSpec((2, tm, tc), lambda j, m: (0, m, j))
    halo = pl.BlockSpec((2, HALO_ROWS, tc), lambda j, m: (0, nprev(m), j))
    wspec = pl.BlockSpec((None, 2, 3, tc), lambda j, m: (l, 0, 0, j))
    bspec = pl.BlockSpec((None, 2, 1, tc), lambda j, m: (l, 0, 0, j))
    return tile, halo, wspec, bspec


def convact_fwd(u, cw, cb, l, *, name, comm=None):
    _, S, Fh = u.shape
    tm = _pick(S, (256, 128))
    tc = _pick(Fh, (1408, 512, 256, 128))
    tile, halo, wspec, bspec = _conv_specs(tm, tc, l)

    def body(u_ref, p_ref, w_ref, b_ref, o_ref, ot_ref, c_ref):
        keep = (pl.program_id(1) > 0).astype(F32)
        cg = _conv(u_ref[0].astype(F32), _halo_rows(p_ref[0], keep, True), w_ref, b_ref, 0)
        cv = _conv(u_ref[1].astype(F32), _halo_rows(p_ref[1], keep, True), w_ref, b_ref, 1)
        act = _silu(cg) * cv
        o_ref[...] = act.astype(BF16)
        ot_ref[...] = act.T.astype(BF16)
        c_ref[0] = cg.astype(BF16)
        c_ref[1] = cv.astype(BF16)

    outs, updated, _ = _carried_call(
        body, [u, u, cw, cb], name=name, grid=(Fh // tc, S // tm), in_specs=[tile, halo, wspec, bspec],
        out_specs=[pl.BlockSpec((tm, tc), lambda j, m: (m, j)), pl.BlockSpec((tc, tm), lambda j, m: (j, m)), tile],
        out_shape=[jax.ShapeDtypeStruct((S, Fh), BF16), jax.ShapeDtypeStruct((Fh, S), BF16),
                   jax.ShapeDtypeStruct(u.shape, BF16)], comm=comm)
    return tuple(outs) if comm is None else (*outs, updated)


def convact_bwd(c, u, cw, l, da, *, name):
    _, S, Fh = u.shape
    tm = _pick(S, (256, 128))
    tc = _pick(Fh, (1408, 512, 256, 128))
    nm = S // tm

    def body(c_ref, u_ref, w_ref, da_ref, du_ref, dw_ref, db_ref, carry):
        @pl.when(pl.program_id(1) == 0)
        def _():
            carry[...] = jnp.zeros_like(carry)
            dw_ref[...] = jnp.zeros_like(dw_ref)
            db_ref[...] = jnp.zeros_like(db_ref)

        cg, cv = c_ref[0].astype(F32), c_ref[1].astype(F32)
        da_t = da_ref[...]
        sg = _sigmoid(cg)
        dcv = da_t * (cg * sg)
        dcg = da_t * cv * (sg * (1.0 + cg * (1.0 - sg)))
        rs = lambda t: jnp.sum(t, axis=0, keepdims=True)
        for half, dc in ((0, dcg), (1, dcv)):
            nxt = carry[half]
            d1, d2 = _shift_up(dc, nxt, 1), _shift_up(dc, nxt, 2)
            du = w_ref[half, 2:3, :] * dc + w_ref[half, 1:2, :] * d1 + w_ref[half, 0:1, :] * d2
            du_ref[half] = du.astype(BF16)
            uh = u_ref[half].astype(F32)
            for j, d in enumerate((d2, d1, dc)):
                dw_ref[half, j:j + 1, :] += rs(d * uh)
            db_ref[half] += rs(dc)
            carry[half] = dc[:SUBLANES]

    tile = pl.BlockSpec((2, tm, tc), lambda j, m: (0, nm - 1 - m, j))
    return pl.pallas_call(
        body, name=name, grid=(Fh // tc, nm),
        in_specs=[tile, tile, pl.BlockSpec((None, 2, 3, tc), lambda j, m: (l, 0, 0, j)),
                  pl.BlockSpec((tm, tc), lambda j, m: (nm - 1 - m, j))],
        out_specs=[tile, pl.BlockSpec((2, 3, tc), lambda j, m: (0, 0, j)),
                   pl.BlockSpec((2, 1, tc), lambda j, m: (0, 0, j))],
        out_shape=[jax.ShapeDtypeStruct(u.shape, BF16), jax.ShapeDtypeStruct((2, 3, Fh), F32),
                   jax.ShapeDtypeStruct((2, 1, Fh), F32)],
        scratch_shapes=[pltpu.VMEM((2, SUBLANES, tc), F32)],
        compiler_params=_params(("parallel", "arbitrary")),
    )(c, u, cw, da)


def _adamw_math(w, g, m, v):
    m = ADAM_B1 * m + (1.0 - ADAM_B1) * g
    v = ADAM_B2 * v + (1.0 - ADAM_B2) * (g * g)
    m_hat = m / (1.0 - ADAM_B1 ** ADAM_STEP)
    v_hat = v / (1.0 - ADAM_B2 ** ADAM_STEP)
    delta = -ADAM_LR * (m_hat / (jnp.sqrt(v_hat) + ADAM_EPS) + ADAM_WD * w)
    return delta, m, v


def _row_tile(R, Ccols, n_streams):
    for tr in (512, 256, 128, 64, 32, 16, 8):
        if R % tr == 0 and tr * Ccols * 4 * n_streams * 2 <= VMEM_LIMIT_BYTES // 2:
            return tr
    return R


def adamw(w, g, m, v, *, name):
    R, Ccols = w.shape
    tr = _row_tile(R, Ccols, 7)

    def body(w_ref, g_ref, m_ref, v_ref, d_ref, mo_ref, vo_ref):
        d, mn, vn = _adamw_math(w_ref[...], g_ref[...], m_ref[...], v_ref[...])
        d_ref[...] = d
        mo_ref[...] = mn
        vo_ref[...] = vn

    spec = pl.BlockSpec((tr, Ccols), lambda i: (i, 0))
    shp = jax.ShapeDtypeStruct((R, Ccols), F32)
    return pl.pallas_call(
        body, name=name, grid=(R // tr,), in_specs=[spec] * 4, out_specs=[spec] * 3,
        out_shape=[shp, shp, shp], compiler_params=_params(("parallel",)),
    )(w, g, m, v)


def _prefetch_call(body, pos, args, *, name, grid, in_specs, out_specs, out_shape, sem, aliases=None):
    return pl.pallas_call(
        lambda pos_ref, *refs: body(*refs), name=name,
        grid_spec=pltpu.PrefetchScalarGridSpec(num_scalar_prefetch=1, grid=grid, in_specs=in_specs,
                                               out_specs=out_specs),
        out_shape=out_shape, input_output_aliases={1 + i: o for i, o in (aliases or {}).items()},
        compiler_params=_params(sem),
    )(pos, *args)


def cast_place(w, pos, *, row_sharded, name):
    L, R, Ccols = w.shape

    def body(w_ref, o_ref):
        o_ref[...] = w_ref[...].astype(BF16)

    if row_sharded:
        grid = (L,)
        in_spec = pl.BlockSpec((None, R, Ccols), lambda l, pos: (l, 0, 0))
        out_spec = pl.BlockSpec((None, R, Ccols), lambda l, pos: (l, pos[0], 0))
        out_shape = jax.ShapeDtypeStruct((L, N_CHIPS * R, Ccols), BF16)
        sem = ("parallel",)
    else:
        tr = _pick(R, (512, 256, 128))
        grid = (L, R // tr)
        in_spec = pl.BlockSpec((None, tr, Ccols), lambda l, i, pos: (l, i, 0))
        out_spec = pl.BlockSpec((None, tr, Ccols), lambda l, i, pos: (l, i, pos[0]))
        out_shape = jax.ShapeDtypeStruct((L, R, N_CHIPS * Ccols), BF16)
        sem = ("parallel", "parallel")
    return _prefetch_call(body, pos, [w], name=name, grid=grid, in_specs=[in_spec], out_specs=out_spec,
                          out_shape=out_shape, sem=sem)


def pair_add(g, theirs, pos, *, row_sharded, name):
    def body(g_ref, t_ref, o_ref, ob_ref):
        s = g_ref[...] + t_ref[...]
        o_ref[...] = s
        ob_ref[...] = s.astype(BF16)

    if row_sharded:
        L, _, _, r2, D = g.shape
        grid = (L, N_CHIPS)
        g_spec = pl.BlockSpec((None, None, None, r2, D), lambda l, q, pos: (l, q, pos[1], 0, 0))
        t_spec = pl.BlockSpec((None, None, r2, D), lambda l, q, pos: (l, q, 0, 0))
    else:
        L, K, N = g.shape
        n = N // N_CHIPS
        tr = _pick(K // 2, (256, 128))
        nb = (K // 2) // tr
        grid = (L, nb, N_CHIPS)
        g_spec = pl.BlockSpec((None, tr, n), lambda l, i, j, pos: (l, pos[1] * nb + i, j))
        t_spec = pl.BlockSpec((None, tr, n), lambda l, i, j, pos: (l, i, j))
    return _prefetch_call(body, pos, [g, theirs], name=name, grid=grid, in_specs=[g_spec, t_spec],
                          out_specs=[t_spec, t_spec],
                          out_shape=[jax.ShapeDtypeStruct(theirs.shape, F32), jax.ShapeDtypeStruct(theirs.shape, BF16)],
                          sem=("parallel",) * len(grid))


def chip_add(pair, landed, pos, buf, li, n_layers, *, row_sharded, name):
    def body(p_ref, a_ref, b_ref, c_ref, *rest):
        rest[-1][...] = ((p_ref[...] + a_ref[...].astype(F32)) + b_ref[...].astype(F32)) + c_ref[...].astype(F32)

    if row_sharded:
        _, _, r2, D = pair.shape
        grid = (1,)
        p_spec = pl.BlockSpec((None, None, r2, D), lambda i, pos: (0, pos[0], 0, 0))
        l_specs = [pl.BlockSpec((None, r2, D), functools.partial(lambda i, pos, k: (k, 0, 0), k=k)) for k in range(3)]
        out_spec = pl.BlockSpec((None, None, r2, D), lambda i, pos: (li, pos[1], 0, 0))
        out_shape = jax.ShapeDtypeStruct((n_layers, 2, r2, D), F32)
    else:
        _, K2, N = pair.shape
        n = N // N_CHIPS
        tr = _pick(K2, (256, 128))
        nb = K2 // tr
        grid = (nb,)
        p_spec = pl.BlockSpec((None, tr, n), lambda i, pos: (0, i, pos[0]))
        l_specs = [pl.BlockSpec((None, tr, n), functools.partial(lambda i, pos, k: (k, i, 0), k=k)) for k in range(3)]
        out_spec = pl.BlockSpec((None, tr, n), lambda i, pos: (li, pos[1] * nb + i, 0))
        out_shape = jax.ShapeDtypeStruct((n_layers, 2 * K2, n), F32)
    args, in_specs, aliases = [pair, landed, landed, landed], [p_spec] + l_specs, None
    if buf is not None:
        args.append(buf)
        in_specs.append(pl.BlockSpec(memory_space=pl.ANY))
        aliases = {4: 0}
    return _prefetch_call(body, pos, args, name=name, grid=grid, in_specs=in_specs, out_specs=out_spec,
                          out_shape=out_shape, sem=("parallel",), aliases=aliases)


def _position():
    return lax.axis_index("x"), lax.axis_index("y"), lax.axis_index("c")


def _other_chips(x, y):
    return [(1 - x, y), (x, 1 - y), (1 - x, 1 - y)]


HBM_ANY = pl.BlockSpec(memory_space=pl.ANY)
COMM_PARAMS = pltpu.CompilerParams(has_side_effects=True)


class Comm:
    def __init__(self, ro=(), inout=(), new=(), sems=(), phases=()):
        self.ro, self.inout, self.new, self.sems, self.phases = list(ro), list(inout), list(new), list(sems), list(phases)


def _carried_call(body, args, *, name, grid, in_specs, out_specs, out_shape, scratch_shapes=(), comm=None):
    comm = comm or Comm()
    n_in, n_out, n_sc = len(in_specs), len(out_shape), len(scratch_shapes)
    n_ro, n_io, n_new, n_sem = len(comm.ro), len(comm.inout), len(comm.new), len(comm.sems)
    last = int(np.prod(grid)) - 1

    def wrapped(*refs):
        bounds = np.cumsum([0, n_in, n_ro, n_io, n_out, n_io, n_new, n_sc, n_sem])
        core_in, ro, _, core_out, io, new, core_sc, sems = (refs[a:b] for a, b in zip(bounds[:-1], bounds[1:]))
        step = pl.program_id(0)
        for d in range(1, len(grid)):
            step = step * grid[d] + pl.program_id(d)
        for frac, fn in comm.phases[:-1]:
            pl.when(step == int(frac * last))(functools.partial(fn, ro, io, new, sems))
        body(*core_in, *core_out, *core_sc)
        if comm.phases:
            pl.when(step == last)(functools.partial(comm.phases[-1][1], ro, io, new, sems))

    outs = pl.pallas_call(
        wrapped, name=name, grid=grid,
        in_specs=list(in_specs) + [HBM_ANY] * (n_ro + n_io),
        out_specs=list(out_specs) + [HBM_ANY] * (n_io + n_new),
        out_shape=list(out_shape) + [jax.ShapeDtypeStruct(a.shape, a.dtype) for a in comm.inout] + comm.new,
        input_output_aliases={n_in + n_ro + j: n_out + j for j in range(n_io)},
        scratch_shapes=list(scratch_shapes) + comm.sems,
        compiler_params=pltpu.CompilerParams(dimension_semantics=("arbitrary",) * len(grid),
                                             vmem_limit_bytes=VMEM_LIMIT_BYTES, has_side_effects=bool(comm.phases)),
    )(*args, *comm.ro, *comm.inout)
    return outs[:n_out], outs[n_out:n_out + n_io], outs[n_out + n_io:]


def run_comm(comm, *, name):
    n_ro, n_io, n_new = len(comm.ro), len(comm.inout), len(comm.new)

    def body(*refs):
        bounds = np.cumsum([0, n_ro, n_io, n_io, n_new, len(comm.sems)])
        ro, _, io, new, sems = (refs[a:b] for a, b in zip(bounds[:-1], bounds[1:]))
        for _, fn in comm.phases:
            fn(ro, io, new, sems)

    outs = pl.pallas_call(
        body, name=name, in_specs=[HBM_ANY] * (n_ro + n_io), out_specs=[HBM_ANY] * (n_io + n_new),
        out_shape=[jax.ShapeDtypeStruct(a.shape, a.dtype) for a in comm.inout] + comm.new,
        input_output_aliases={n_ro + j: j for j in range(n_io)},
        scratch_shapes=comm.sems, compiler_params=COMM_PARAMS,
    )(*comm.ro, *comm.inout)
    return outs[:n_io], outs[n_io:]


def _rdma(src, dst, send_sem, recv_sem, device):
    return pltpu.make_async_remote_copy(src_ref=src, dst_ref=dst, send_sem=send_sem, recv_sem=recv_sem,
                                        device_id=device, device_id_type=MESH)


def gather_comm(bufs, slices):
    n = len(slices)

    def region(io, s, h, chip):
        b, li, row_sharded = slices[s]
        ref = io[b]
        if row_sharded:
            r = ref.shape[1] // N_CHIPS
            return ref.at[li, pl.ds(chip * r + h * (r // 2), r // 2), :]
        K, nn = ref.shape[1], ref.shape[2] // N_CHIPS
        return ref.at[li, pl.ds(h * (K // 2), K // 2), pl.ds(chip * nn, nn)]

    def copies(io, sems):
        x, y, c = _position()
        me = 2 * x + y
        chips = _other_chips(x, y)
        ids = [2 * cx + cy for cx, cy in chips]

        def ici(s, k, src):
            return _rdma(region(io, s, c, me), region(io, s, c, src), sems[0].at[s, k], sems[1].at[s, k], (*chips[k], c))

        def d2d(s, k, h):
            return _rdma(region(io, s, h, ids[k]), region(io, s, h, ids[k]), sems[2].at[s, k], sems[3].at[s, k],
                         (x, y, 1 - c))

        return me, ids, c, ici, d2d

    pairs = [(s, k) for s in range(n) for k in range(3)]

    def start(ro, io, new, sems):
        me, _, _, ici, _ = copies(io, sems)
        for s, k in pairs:
            ici(s, k, me).start()

    def pass_on(ro, io, new, sems):
        _, ids, c, ici, d2d = copies(io, sems)
        for s, k in pairs:
            ici(s, k, ids[k]).wait_recv()
            d2d(s, k, c).start()

    def finish(ro, io, new, sems):
        me, _, c, ici, d2d = copies(io, sems)
        for s, k in pairs:
            d2d(s, k, 1 - c).wait_recv()
        for s, k in pairs:
            ici(s, k, me).wait_send()
            d2d(s, k, c).wait_send()

    return Comm(inout=bufs, sems=[pltpu.SemaphoreType.DMA((n, 3))] * 4,
                phases=[(0.0, start), (0.6, pass_on), (1.0, finish)])


def gather_small(shard, *, name):
    n = shard.shape[2]

    def body(s_ref, o_ref, send_sem, recv_sem, local_sem):
        x, y, c = _position()
        me = 2 * x + y
        chips = _other_chips(x, y)
        place = lambda chip: o_ref.at[:, :, pl.ds(chip * n, n)]
        local = pltpu.make_async_copy(s_ref, place(me), local_sem)
        local.start()
        for k in range(3):
            _rdma(s_ref, place(me), send_sem.at[k], recv_sem.at[k], (*chips[k], c)).start()
        for k in range(3):
            _rdma(s_ref, place(2 * chips[k][0] + chips[k][1]), send_sem.at[k], recv_sem.at[k], (*chips[k], c)).wait()
        local.wait()

    return pl.pallas_call(
        body, name=name, in_specs=[HBM_ANY], out_specs=HBM_ANY,
        out_shape=jax.ShapeDtypeStruct((shard.shape[0], shard.shape[1], N_CHIPS * n), shard.dtype),
        scratch_shapes=[pltpu.SemaphoreType.DMA((3,)), pltpu.SemaphoreType.DMA((3,)), pltpu.SemaphoreType.DMA],
        compiler_params=COMM_PARAMS,
    )(shard)


def swap_comm(grads):
    def copies(ro, new, sems):
        x, y, c = _position()
        h = 1 - c
        cps = []
        for s, (a, row_sharded) in enumerate(grads):
            if row_sharded:
                r = a.shape[1] // N_CHIPS
                for q in range(N_CHIPS):
                    cps.append(_rdma(ro[s].at[:, pl.ds(q * r + h * (r // 2), r // 2), :], new[s].at[:, q],
                                     sems[0].at[s, q], sems[1].at[s, q], (x, y, 1 - c)))
            else:
                K = a.shape[1]
                cps.append(_rdma(ro[s].at[:, pl.ds(h * (K // 2), K // 2), :], new[s],
                                 sems[0].at[s, 0], sems[1].at[s, 0], (x, y, 1 - c)))
        return cps

    def start(ro, io, new, sems):
        for cp in copies(ro, new, sems):
            cp.start()

    def finish(ro, io, new, sems):
        for cp in copies(ro, new, sems):
            cp.wait()

    theirs = []
    for a, row_sharded in grads:
        if row_sharded:
            theirs.append(jax.ShapeDtypeStruct((1, N_CHIPS, a.shape[1] // N_CHIPS // 2, a.shape[2]), a.dtype))
        else:
            theirs.append(jax.ShapeDtypeStruct((1, a.shape[1] // 2, a.shape[2]), a.dtype))
    return Comm(ro=[a for a, _ in grads], new=theirs, sems=[pltpu.SemaphoreType.DMA((len(grads), N_CHIPS))] * 2,
                phases=[(0.0, start), (1.0, finish)])


def merge_comms(comms):
    spans, ro, io, new, sems = [], [], [], [], []
    for cm in comms:
        spans.append((len(ro), len(io), len(new), len(sems)))
        ro, io, new, sems = ro + cm.ro, io + cm.inout, new + cm.new, sems + cm.sems

    def phase(frac):
        def run(R, I, N, S):
            for cm, (a, b, d, e) in zip(comms, spans):
                for f, fn in cm.phases:
                    if f == frac:
                        fn(R[a:a + len(cm.ro)], I[b:b + len(cm.inout)], N[d:d + len(cm.new)], S[e:e + len(cm.sems)])
        return run

    fracs = sorted({f for cm in comms for f, _ in cm.phases})
    merged = Comm(ro=ro, inout=io, new=new, sems=sems, phases=[(f, phase(f)) for f in fracs])
    cut = lambda created: [created[d:d + len(cm.new)] for cm, (_, _, d, _) in zip(comms, spans)]
    return merged, cut


def scatter_comm(halves):
    def copies(ro, new, sems):
        x, y, c = _position()
        chips = _other_chips(x, y)
        cps = []
        for s, (a, row_sharded) in enumerate(halves):
            for k in range(3):
                to = 2 * chips[k][0] + chips[k][1]
                if row_sharded:
                    src = ro[s].at[0, to]
                else:
                    n = a.shape[2] // N_CHIPS
                    src = ro[s].at[0, :, pl.ds(to * n, n)]
                cps.append(_rdma(src, new[s].at[k], sems[0].at[s, k], sems[1].at[s, k], (*chips[k], c)))
        return cps

    def start(ro, io, new, sems):
        for cp in copies(ro, new, sems):
            cp.start()

    def finish(ro, io, new, sems):
        for cp in copies(ro, new, sems):
            cp.wait()

    landing = [jax.ShapeDtypeStruct((3, a.shape[2], a.shape[3]) if row_sharded else
                                    (3, a.shape[1], a.shape[2] // N_CHIPS), a.dtype) for a, row_sharded in halves]
    return Comm(ro=[a for a, _ in halves], new=landing, sems=[pltpu.SemaphoreType.DMA((len(halves), 3))] * 2,
                phases=[(0.0, start), (1.0, finish)])


def sibling_join_halves(shards, *, name):
    n_in = len(shards)

    def body(*refs):
        ins = refs[:n_in]
        outs = refs[n_in:2 * n_in]
        send_sem, recv_sem = refs[2 * n_in:]
        x, y, c = _position()
        sibling = (x, y, 1 - c)

        def rows(ref, h):
            R2 = ref.shape[1] // 2
            return ref.at[:, pl.ds(h * R2, R2), :]

        sends = [pltpu.make_async_remote_copy(src_ref=rows(ins[i], c), dst_ref=rows(outs[i], c),
                                              send_sem=send_sem.at[i], recv_sem=recv_sem.at[i],
                                              device_id=sibling, device_id_type=MESH) for i in range(n_in)]
        lands = [pltpu.make_async_remote_copy(src_ref=rows(ins[i], c), dst_ref=rows(outs[i], 1 - c),
                                              send_sem=send_sem.at[i], recv_sem=recv_sem.at[i],
                                              device_id=sibling, device_id_type=MESH) for i in range(n_in)]
        for cp in sends:
            cp.start()
        for i in range(n_in):
            sends[i].wait_send()
            lands[i].wait_recv()

    return pl.pallas_call(
        body, name=name, in_specs=[HBM_ANY] * n_in, out_specs=[HBM_ANY] * n_in,
        out_shape=[jax.ShapeDtypeStruct(a.shape, a.dtype) for a in shards],
        input_output_aliases={i: i for i in range(n_in)},
        scratch_shapes=[pltpu.SemaphoreType.DMA((n_in,)), pltpu.SemaphoreType.DMA((n_in,))],
        compiler_params=COMM_PARAMS,
    )(*shards)


def all_reduce_small(v, *, name):
    R, Ccols = v.shape

    def body(v_ref, o_ref, slots, send_sem, recv_sem):
        x, y, c = _position()
        me = 4 * x + 2 * y + c
        slots[me] = v_ref[...]

        def peer(d):
            return x ^ (d >> 2), y ^ ((d >> 1) & 1), c ^ (d & 1)

        def copy(d, slot):
            return pltpu.make_async_remote_copy(
                src_ref=v_ref, dst_ref=slots.at[slot], send_sem=send_sem.at[d], recv_sem=recv_sem.at[d],
                device_id=peer(d), device_id_type=MESH)

        sends = [copy(d, me) for d in range(1, N_DEV)]
        for cp in sends:
            cp.start()
        for d in range(1, N_DEV):
            px, py, pc = peer(d)
            copy(d, 4 * px + 2 * py + pc).wait_recv()
        for cp in sends:
            cp.wait_send()
        acc = slots[0]
        for d in range(1, N_DEV):
            acc = acc + slots[d]
        o_ref[...] = acc

    return pl.pallas_call(
        body, name=name,
        in_specs=[pl.BlockSpec(memory_space=pltpu.VMEM)], out_specs=pl.BlockSpec(memory_space=pltpu.VMEM),
        out_shape=jax.ShapeDtypeStruct((R, Ccols), F32),
        scratch_shapes=[pltpu.VMEM((N_DEV, R, Ccols), F32), pltpu.SemaphoreType.DMA((N_DEV,)),
                        pltpu.SemaphoreType.DMA((N_DEV,))],
        compiler_params=COMM_PARAMS,
    )(v)


def _pack_small(parts):
    flat = jnp.concatenate([p.reshape(-1).astype(F32) for p in parts])
    n = flat.shape[0]
    rows = -(-n // LANES)
    rows = -(-rows // SUBLANES) * SUBLANES
    return jnp.pad(flat, (0, rows * LANES - n)).reshape(rows, LANES)


def _unpack_small(packed, like):
    flat = packed.reshape(-1)
    out, off = [], 0
    for p in like:
        out.append(flat[off:off + p.size].reshape(p.shape))
        off += p.size
    return out


def kernel(x, norm_mix, norm_ffn, norm_final, attn_w_in, attn_w_out, attn_sinks, hgrn_w_in, hgrn_w_out, hgrn_norm, hgrn_lb_logits, ffn_w_up, ffn_conv_w, ffn_conv_b, ffn_w_down, loss_target, m_norm_mix, m_norm_ffn, m_norm_final, m_attn_w_in, m_attn_w_out, m_attn_sinks, m_hgrn_w_in, m_hgrn_w_out, m_hgrn_norm, m_hgrn_lb_logits, m_ffn_w_up, m_ffn_conv_w, m_ffn_conv_b, m_ffn_w_down, v_norm_mix, v_norm_ffn, v_norm_final, v_attn_w_in, v_attn_w_out, v_attn_sinks, v_hgrn_w_in, v_hgrn_w_out, v_hgrn_norm, v_hgrn_lb_logits, v_ffn_w_up, v_ffn_conv_w, v_ffn_conv_b, v_ffn_w_down):
    S, D = x.shape[1], x.shape[2]
    depth = norm_mix.shape[0]
    n_q = D // HEAD_DIM
    n_kv = n_q // Q_PER_KV
    Fh = ffn_w_down.shape[1] * N_CHIPS
    h = x.reshape(S, D)
    target = loss_target.reshape(S, D)

    col_names = ["attn_w_in", "hgrn_w_in", "ffn_w_up"]
    row_names = ["attn_w_out", "hgrn_w_out", "ffn_w_down"]
    x_, y_, c_ = _position()
    me = 2 * x_ + y_
    pos = jnp.stack([me, c_]).astype(jnp.int32)
    col_w = [cast_place(w, pos, row_sharded=False, name=f"cast_place_{n}")
             for n, w in zip(col_names, (attn_w_in, hgrn_w_in, ffn_w_up))]
    row_w = [cast_place(w, pos, row_sharded=True, name=f"cast_place_{n}")
             for n, w in zip(row_names, (attn_w_out, hgrn_w_out, ffn_w_down))]
    W = dict(zip(col_names + row_names, col_w + row_w))
    is_row = {n: n in row_names for n in W}

    def layer_weights(layer):
        mixer = "attn" if layer % 2 == 0 else "hgrn"
        return [(f"{mixer}_w_in", layer // 2), (f"{mixer}_w_out", layer // 2), ("ffn_w_up", layer), ("ffn_w_down", layer)]

    def gather_of(group):
        names = [n for n, _ in group]
        return names, gather_comm([W[n] for n in names], [(i, li, is_row[n]) for i, (n, li) in enumerate(group)])

    def mixer_weights(layer):
        return layer_weights(layer)[:2] if layer < depth else []

    names_next, comm_next = gather_of(mixer_weights(0))
    filled, _ = run_comm(comm_next, name="gather_mixer0")
    W.update(zip(names_next, filled))
    cw = gather_small(ffn_conv_w, name="gather_conv_w").reshape(depth, 3, 2, Fh).transpose(0, 2, 1, 3)
    cb = ffn_conv_b.reshape(depth, 2, 1, Fh)
    g_mix = norm_mix.reshape(depth, 1, D)
    g_ffn = norm_ffn.reshape(depth, 1, D)

    slopes = jnp.exp2(-8.0 * jnp.arange(1, n_q + 1, dtype=F32) / n_q).reshape(n_kv, Q_PER_KV)
    slope_col = jnp.repeat(slopes, WINDOW, axis=1).reshape(n_kv, Q_PER_KV * WINDOW, 1)

    def to_heads(t, nh):
        return t.reshape(S, nh, HEAD_DIM).transpose(1, 0, 2)

    def from_heads(t):
        return t.transpose(1, 0, 2).reshape(S, -1)

    saved = []
    for layer in range(depth):
        idx = layer // 2
        rec = {"h_mix": h}
        hn, rec["hn_t"] = rmsnorm_fwd(h, g_mix, layer, name=f"norm_mix_fwd{layer}")
        is_attn = layer % 2 == 0
        up_l, down_l = layer_weights(layer)[2:]
        on_mixer = [up_l] if is_attn else [up_l, down_l]
        on_up = [down_l] if is_attn else mixer_weights(layer + 1)
        on_act = mixer_weights(layer + 1)[:1] if is_attn else []
        on_down = mixer_weights(layer + 1)[1:] if is_attn else []
        names_next, comm_next = gather_of(on_mixer)
        if is_attn:
            proj = mm_nn(hn, W["attn_w_in"], idx, name=f"attn_in_fwd{layer}", out_dtype=BF16)
            q = to_heads(proj[:, :n_q * HEAD_DIM], n_q).reshape(n_kv, Q_PER_KV, S, HEAD_DIM)
            k = to_heads(proj[:, n_q * HEAD_DIM:(n_q + n_kv) * HEAD_DIM], n_kv)
            v = to_heads(proj[:, (n_q + n_kv) * HEAD_DIM:], n_kv)
            sink_col = jnp.repeat(attn_sinks[idx].reshape(n_kv, Q_PER_KV), WINDOW, axis=1)
            sink_col = sink_col.reshape(n_kv, Q_PER_KV * WINDOW, 1)
            o_heads, filled, _ = attn_fwd(q, k, v, sink_col, slope_col, name=f"attn_fwd{layer}", comm=comm_next)
            W.update(zip(names_next, filled))
            o = from_heads(o_heads.reshape(n_q, S, HEAD_DIM))
            rec.update(q=q, k=k, v=v, sink_col=sink_col, o=o)
            h = mm_nn(o, W["attn_w_out"], idx, name=f"attn_out_fwd{layer}", res=h)
        else:
            proj = mm_nn(hn, W["hgrn_w_in"], idx, name=f"hgrn_in_fwd{layer}")
            ng = hgrn_norm[idx].reshape(1, HG_EXPAND)
            o, states, filled, _ = hgrn_fwd(proj, hgrn_lb_logits, ng, layer, name=f"hgrn_fwd{layer}", comm=comm_next)
            W.update(zip(names_next, filled))
            rec.update(proj=proj, ng=ng, states=states, o=o)
            h = mm_nn(o, W["hgrn_w_out"], idx, name=f"hgrn_out_fwd{layer}", res=h)
        rec["h_ffn"] = h
        hn2, rec["hn2_t"] = rmsnorm_fwd(h, g_ffn, layer, name=f"norm_ffn_fwd{layer}")
        if on_up:
            names_next, comm_next = gather_of(on_up)
            u, filled = mm_nn(hn2, W["ffn_w_up"], layer, name=f"ffn_up_fwd{layer}", split_out=True, out_dtype=BF16,
                              comm=comm_next)
            W.update(zip(names_next, filled))
        else:
            u = mm_nn(hn2, W["ffn_w_up"], layer, name=f"ffn_up_fwd{layer}", split_out=True, out_dtype=BF16)
        if on_act:
            names_next, comm_next = gather_of(on_act)
            act, rec["act_t"], rec["c"], filled = convact_fwd(u, cw, cb, layer, name=f"ffn_act_fwd{layer}",
                                                              comm=comm_next)
            W.update(zip(names_next, filled))
        else:
            act, rec["act_t"], rec["c"] = convact_fwd(u, cw, cb, layer, name=f"ffn_act_fwd{layer}")
        rec["u"] = u
        if on_down:
            names_next, comm_next = gather_of(on_down)
            h, filled = mm_nn(act, W["ffn_w_down"], layer, name=f"ffn_down_fwd{layer}", res=h, comm=comm_next)
            W.update(zip(names_next, filled))
        else:
            h = mm_nn(act, W["ffn_w_down"], layer, name=f"ffn_down_fwd{layer}", res=h)
        saved.append(rec)

    loss_part, dh, d_norm_final = loss_head(h, norm_final.reshape(1, D), target, name="loss_head")

    n_attn, n_hgrn = attn_w_in.shape[0], hgrn_w_in.shape[0]
    reduced_parts = []
    pending = []

    def swap_of(group, g):
        ordered = sorted(group, key=lambda nl: is_row[nl[0]])
        return ordered, swap_comm([(g[n], is_row[n]) for n, _ in ordered])

    def pair_up(tag, ordered, g, theirs):
        pair, pair_bf16 = {}, {}
        for (n, _), t in zip(ordered, theirs):
            gn = g[n]
            if is_row[n]:
                r = gn.shape[1] // N_CHIPS
                gn = gn.reshape(1, N_CHIPS, 2, r // 2, gn.shape[2])
            pair[n], pair_bf16[n] = pair_add(gn, t, pos, row_sharded=is_row[n], name=f"rs_pair_add_{n}{tag}")
        pending.append((ordered, pair, scatter_comm([(pair_bf16[n], is_row[n]) for n, _ in ordered])))

    def landed_behind(created, cut):
        for (ordered, pair, _), landed in zip(pending, cut(created)):
            reduced_parts.append((ordered, pair, landed))
        pending.clear()

    d_norm_mix, d_norm_ffn = [None] * depth, [None] * depth
    d_conv_w, d_conv_b = [None] * depth, [None] * depth
    d_sinks, d_hgrn_norm = [None] * n_attn, [None] * n_hgrn
    d_logits = jnp.zeros_like(hgrn_lb_logits)
    for layer in reversed(range(depth)):
        idx = layer // 2
        rec = saved[layer]
        g = {"ffn_w_down": mm_dw(rec["act_t"], dh, name=f"ffn_down_dw{layer}")}
        dact = mm_nt(dh, W["ffn_w_down"], layer, name=f"ffn_down_dx{layer}")
        du, dcw, dcb = convact_bwd(rec["c"], rec["u"], cw, layer, dact, name=f"ffn_act_bwd{layer}")
        d_conv_w[layer] = dcw.transpose(1, 0, 2).reshape(3, 2 * Fh)
        d_conv_b[layer] = dcb.reshape(2 * Fh)
        g["ffn_w_up"] = mm_dw(rec["hn2_t"], du, name=f"ffn_up_dw{layer}", split_b=True)
        ordered, swap = swap_of(layer_weights(layer)[2:], g)
        dh, dg, theirs = mm_nt(du, W["ffn_w_up"], layer, name=f"ffn_up_dx{layer}", split_in=True, comm=swap,
                               norm=(rec["h_ffn"], g_ffn, layer, dh))
        pair_up(layer, ordered, g, theirs)
        d_norm_ffn[layer] = dg.reshape(D)
        carried, cut = merge_comms([p[2] for p in pending])
        if layer % 2 == 0:
            g["attn_w_out"] = mm_tn(rec["o"], dh, name=f"attn_out_dw{layer}")
            do = mm_nt(dh, W["attn_w_out"], idx, name=f"attn_out_dx{layer}", out_dtype=BF16)
            do_heads = to_heads(do, n_q).reshape(n_kv, Q_PER_KV, S, HEAD_DIM)
            res = attn_bwd(rec["q"], rec["k"], rec["v"], rec["sink_col"], slope_col, do_heads,
                           name=f"attn_bwd{layer}", comm=carried)
            dq, dk, dv, dsink = res[:4]
            landed_behind(res[-1], cut)
            dproj = jnp.concatenate([from_heads(dq.reshape(n_q, S, HEAD_DIM)), from_heads(dk), from_heads(dv)], axis=1)
            d_sinks[idx] = jnp.sum(dsink.reshape(n_kv, Q_PER_KV, WINDOW), axis=-1).reshape(n_q)
            g["attn_w_in"] = mm_dw(rec["hn_t"], dproj, name=f"attn_in_dw{layer}")
            ordered, swap = swap_of(layer_weights(layer)[:2], g)
            dh, dg, theirs = mm_nt(dproj, W["attn_w_in"], idx, name=f"attn_in_dx{layer}", comm=swap,
                                   norm=(rec["h_mix"], g_mix, layer, dh))
        else:
            g["hgrn_w_out"] = mm_tn(rec["o"], dh, name=f"hgrn_out_dw{layer}")
            do = mm_nt(dh, W["hgrn_w_out"], idx, name=f"hgrn_out_dx{layer}")
            res = hgrn_bwd(rec["proj"], hgrn_lb_logits, rec["ng"], rec["states"], do, layer,
                           name=f"hgrn_bwd{layer}", comm=carried)
            dproj, dlg, dng = res[:3]
            landed_behind(res[-1], cut)
            d_logits = d_logits + dlg
            d_hgrn_norm[idx] = dng.reshape(HG_EXPAND)
            g["hgrn_w_in"] = mm_dw(rec["hn_t"], dproj, name=f"hgrn_in_dw{layer}")
            ordered, swap = swap_of(layer_weights(layer)[:2], g)
            dh, dg, theirs = mm_nt(dproj, W["hgrn_w_in"], idx, name=f"hgrn_in_dx{layer}", comm=swap,
                                   norm=(rec["h_mix"], g_mix, layer, dh))
        pair_up(layer, ordered, g, theirs)
        d_norm_mix[layer] = dg.reshape(D)
    grad_x = dh.reshape(x.shape)
    carried, cut = merge_comms([p[2] for p in pending])
    landed_behind(run_comm(carried, name="rs_scatter_last")[1], cut)

    small_w = [norm_mix, norm_ffn, norm_final, attn_sinks, hgrn_norm, hgrn_lb_logits, ffn_conv_b]
    small_m = [m_norm_mix, m_norm_ffn, m_norm_final, m_attn_sinks, m_hgrn_norm, m_hgrn_lb_logits, m_ffn_conv_b]
    small_v = [v_norm_mix, v_norm_ffn, v_norm_final, v_attn_sinks, v_hgrn_norm, v_hgrn_lb_logits, v_ffn_conv_b]
    small_grads = [jnp.stack(d_norm_mix), jnp.stack(d_norm_ffn), d_norm_final.reshape(D), jnp.stack(d_sinks),
                   jnp.stack(d_hgrn_norm), d_logits, jnp.stack(d_conv_b)]
    gconv = jnp.stack(d_conv_w)
    packed_small = _pack_small(small_grads + [loss_part])
    summed = all_reduce_small(jnp.concatenate([packed_small, _pack_small([gconv])], axis=0), name="all_reduce_small")
    reduced, gconv_rows = summed[:packed_small.shape[0]], summed[packed_small.shape[0]:]
    small_g = _unpack_small(reduced, small_w + [loss_part])
    loss = small_g.pop().reshape(())
    pad = [jnp.zeros((1, 1), F32)]
    sd, sm_, sv_ = adamw(_pack_small(small_w + pad), reduced, _pack_small(small_m + pad),
                         _pack_small(small_v + pad), name="adamw_small")
    small_d = _unpack_small(sd, small_w)
    small_nm = _unpack_small(sm_, small_w)
    small_nv = _unpack_small(sv_, small_w)

    big_names = col_names + row_names
    big_w = dict(attn_w_in=attn_w_in, hgrn_w_in=hgrn_w_in, ffn_w_up=ffn_w_up, attn_w_out=attn_w_out,
                 hgrn_w_out=hgrn_w_out, ffn_w_down=ffn_w_down)
    half_shard = dict.fromkeys(big_names)
    for ordered, pair, landed in reduced_parts:
        for (n, li), ld in zip(ordered, landed):
            half_shard[n] = chip_add(pair[n], ld, pos, half_shard[n], li, big_w[n].shape[0], row_sharded=is_row[n],
                                     name=f"rs_chip_add_{n}{li}")
    shards = sibling_join_halves([half_shard[n].reshape(big_w[n].shape) for n in big_names], name="rs_sibling_join")
    big_m = dict(attn_w_in=m_attn_w_in, hgrn_w_in=m_hgrn_w_in, ffn_w_up=m_ffn_w_up, attn_w_out=m_attn_w_out,
                 hgrn_w_out=m_hgrn_w_out, ffn_w_down=m_ffn_w_down)
    big_v = dict(attn_w_in=v_attn_w_in, hgrn_w_in=v_hgrn_w_in, ffn_w_up=v_ffn_w_up, attn_w_out=v_attn_w_out,
                 hgrn_w_out=v_hgrn_w_out, ffn_w_down=v_ffn_w_down)
    grads, deltas, new_m, new_v = {}, {}, {}, {}
    for name_, gshard in zip(big_names, shards):
        w_ = big_w[name_]
        cols = w_.shape[-1]
        d_, m_, v_ = adamw(w_.reshape(-1, cols), gshard.reshape(-1, cols), big_m[name_].reshape(-1, cols),
                           big_v[name_].reshape(-1, cols), name=f"adamw_{name_}")
        grads[name_] = gshard.reshape(w_.shape)
        deltas[name_], new_m[name_], new_v[name_] = (t.reshape(w_.shape) for t in (d_, m_, v_))

    n_conv = ffn_conv_w.shape[2]
    gconv_all = _unpack_small(gconv_rows, [gconv])[0]
    gconv_mine = lax.dynamic_slice_in_dim(gconv_all, me * n_conv, n_conv, axis=2)
    cshape = ffn_conv_w.shape
    d_, m_, v_ = adamw(ffn_conv_w.reshape(-1, n_conv), gconv_mine.reshape(-1, n_conv),
                       m_ffn_conv_w.reshape(-1, n_conv), v_ffn_conv_w.reshape(-1, n_conv), name="adamw_ffn_conv_w")
    grads["ffn_conv_w"] = gconv_mine
    deltas["ffn_conv_w"], new_m["ffn_conv_w"], new_v["ffn_conv_w"] = (t.reshape(cshape) for t in (d_, m_, v_))

    order = ["norm_mix", "norm_ffn", "norm_final", "attn_w_in", "attn_w_out", "attn_sinks", "hgrn_w_in",
             "hgrn_w_out", "hgrn_norm", "hgrn_lb_logits", "ffn_w_up", "ffn_conv_w", "ffn_conv_b", "ffn_w_down"]
    small_names = ["norm_mix", "norm_ffn", "norm_final", "attn_sinks", "hgrn_norm", "hgrn_lb_logits", "ffn_conv_b"]
    for i, name_ in enumerate(small_names):
        grads[name_], deltas[name_], new_m[name_], new_v[name_] = small_g[i], small_d[i], small_nm[i], small_nv[i]
    return (loss, grad_x, *[grads[n] for n in order], *[deltas[n] for n in order],
            *[new_m[n] for n in order], *[new_v[n] for n in order])
```

```python
import functools
import math

import numpy as np
import jax
import jax.numpy as jnp
from jax import lax
from jax.experimental import pallas as pl
from jax.experimental.pallas import tpu as pltpu

F32 = jnp.float32
BF16 = jnp.bfloat16
MESH = pl.DeviceIdType.MESH

HEAD_DIM = 64
Q_PER_KV = 4
WINDOW = 128
HG_EXPAND = 128
HG_CHUNK = 128
EPS = 1e-6
N_CHIPS = 4
N_DEV = 8

ADAM_LR = 0.001
ADAM_B1 = 0.9
ADAM_B2 = 0.999
ADAM_EPS = 1e-08
ADAM_WD = 0.01
ADAM_STEP = 10

VMEM_LIMIT_BYTES = 48 * 1024 * 1024
LANES = 128
SUBLANES = 8
HALO_ROWS = 16


def _params(sem=None):
    return pltpu.CompilerParams(dimension_semantics=sem, vmem_limit_bytes=VMEM_LIMIT_BYTES)


def _pick(n, cands):
    for c in cands:
        if n % c == 0:
            return c
    return n


def _dot(a, b, dims):
    return lax.dot_general(a.astype(BF16), b.astype(BF16), (dims, ((), ())),
                           preferred_element_type=F32)


NN = ((1,), (0,))
NT = ((1,), (1,))
TN = ((0,), (0,))


def mm_nn(a, w, l, *, name, res=None, out_dtype=F32, split_out=False, comm=None):
    M, K = a.shape
    N = w.shape[2]
    tk = K if K <= 2816 else _pick(K, (1024, 1408, 512, 256, 128))
    tm = _pick(M, (1024, 512, 256, 128) if tk <= 1408 else (512, 256, 128))
    tn = _pick(N // 2 if split_out else N, (1408, 1024, 768, 512, 256, 128))
    nk = K // tk
    nh = (N // 2) // tn

    def body(*refs):
        if res is None:
            a_ref, w_ref, o_ref = refs[:3]
            r_ref = None
        else:
            a_ref, w_ref, r_ref, o_ref = refs[:4]
        part = _dot(a_ref[...], w_ref[...], NN)

        def finish(acc):
            if r_ref is not None:
                acc = acc + r_ref[...]
            o_ref[...] = acc.astype(out_dtype)

        if nk == 1:
            finish(part)
        else:
            acc_ref = refs[-1]
            k = pl.program_id(2)

            @pl.when(k == 0)
            def _():
                acc_ref[...] = part

            @pl.when(k > 0)
            def _():
                acc_ref[...] += part

            @pl.when(k == nk - 1)
            def _():
                finish(acc_ref[...])

    in_specs = [pl.BlockSpec((tm, tk), lambda n, m, k: (m, k)),
                pl.BlockSpec((None, tk, tn), lambda n, m, k: (l, k, n))]
    args = [a, w]
    if res is not None:
        in_specs.append(pl.BlockSpec((tm, tn), lambda n, m, k: (m, n)))
        args.append(res)
    if split_out:
        out_shape = jax.ShapeDtypeStruct((2, M, N // 2), out_dtype)
        out_spec = pl.BlockSpec((None, tm, tn), lambda n, m, k: (n // nh, m, n % nh))
    else:
        out_shape = jax.ShapeDtypeStruct((M, N), out_dtype)
        out_spec = pl.BlockSpec((tm, tn), lambda n, m, k: (m, n))
    outs, updated, _ = _carried_call(
        body, args, name=name, grid=(N // tn, M // tm, nk), in_specs=in_specs, out_specs=[out_spec],
        out_shape=[out_shape], scratch_shapes=[] if nk == 1 else [pltpu.VMEM((tm, tn), F32)], comm=comm)
    return outs[0] if comm is None else (outs[0], updated)


def mm_nt(a, w, l, *, name, split_in=False, out_dtype=F32, comm=None, norm=None):
    if split_in:
        _, M, Nh = a.shape
        N = 2 * Nh
    else:
        M, N = a.shape
        Nh = N
    K = w.shape[1]
    tm = _pick(M, (512, 256, 128) if norm is not None else (1024, 512, 256, 128))
    tko = _pick(K, (1024, 1408, 512, 256, 128))
    tr = _pick(Nh, (2816, 2048, 1536, 1408, 1024, 768, 512, 256, 128) if tm <= 512 else (1024, 1408, 768, 512, 256, 128))
    nr = N // tr
    nh = Nh // tr
    n_in, n_out = (5, 2) if norm is not None else (2, 1)
    assert norm is None or tko == K

    def body(*refs):
        a_ref, w_ref = refs[:2]
        o_ref = refs[n_in]
        part = _dot(a_ref[...], w_ref[...], NT)

        def finish(acc):
            if norm is None:
                o_ref[...] = acc.astype(out_dtype)
                return
            h_ref, g_ref, dres_ref = refs[2:5]
            dg_ref = refs[n_in + 1]
            _, vjp = jax.vjp(_rms, h_ref[...], g_ref[...])
            dx, dg = vjp(acc)
            o_ref[...] = dres_ref[...] + dx
            first = pl.program_id(1) == 0

            @pl.when(first)
            def _():
                dg_ref[...] = dg

            @pl.when(jnp.logical_not(first))
            def _():
                dg_ref[...] += dg

        if nr == 1:
            finish(part)
        else:
            acc_ref = refs[n_in + n_out]
            r = pl.program_id(2)

            @pl.when(r == 0)
            def _():
                acc_ref[...] = part

            @pl.when(r > 0)
            def _():
                acc_ref[...] += part

            @pl.when(r == nr - 1)
            def _():
                finish(acc_ref[...])

    if split_in:
        a_spec = pl.BlockSpec((None, tm, tr), lambda ko, m, r: (r // nh, m, r % nh))
    else:
        a_spec = pl.BlockSpec((tm, tr), lambda ko, m, r: (m, r))
    args = [a, w]
    in_specs = [a_spec, pl.BlockSpec((None, tko, tr), lambda ko, m, r: (l, ko, r))]
    out_specs = [pl.BlockSpec((tm, tko), lambda ko, m, r: (m, ko))]
    out_shape = [jax.ShapeDtypeStruct((M, K), out_dtype)]
    if norm is not None:
        h, g, l_norm, dres = norm
        row = pl.BlockSpec((tm, K), lambda ko, m, r: (m, 0))
        args += [h, g, dres]
        in_specs += [row, pl.BlockSpec((None, 1, K), lambda ko, m, r: (l_norm, 0, 0)), row]
        out_specs.append(pl.BlockSpec((1, K), lambda ko, m, r: (0, 0)))
        out_shape.append(jax.ShapeDtypeStruct((1, K), F32))
    outs, _, created = _carried_call(
        body, args, name=name, grid=(K // tko, M // tm, nr), in_specs=in_specs, out_specs=out_specs,
        out_shape=out_shape, scratch_shapes=[] if nr == 1 else [pltpu.VMEM((tm, tko), F32)], comm=comm)
    outs = list(outs) + ([] if comm is None else [created])
    return outs[0] if len(outs) == 1 else tuple(outs)


def mm_tn(a, b, *, name):
    M, K = a.shape
    N = b.shape[1]
    tm = _pick(M, (4096, 2048, 1024, 512, 256, 128))
    tko = _pick(K, (512, 256, 128))
    tn = _pick(N, (512, 256, 128))
    nm = M // tm

    def body(a_ref, b_ref, o_ref):
        part = _dot(a_ref[...], b_ref[...], TN)
        r = pl.program_id(2)

        @pl.when(r == 0)
        def _():
            o_ref[...] = part

        @pl.when(r > 0)
        def _():
            o_ref[...] += part

    return pl.pallas_call(
        body, name=name, grid=(K // tko, N // tn, nm),
        in_specs=[pl.BlockSpec((tm, tko), lambda ko, n, r: (r, ko)), pl.BlockSpec((tm, tn), lambda ko, n, r: (r, n))],
        out_specs=pl.BlockSpec((None, tko, tn), lambda ko, n, r: (0, ko, n)),
        out_shape=jax.ShapeDtypeStruct((1, K, N), F32),
        compiler_params=_params(("parallel", "parallel", "arbitrary")),
    )(a, b)


def mm_dw(at, b, *, name, split_b=False):
    K, M = at.shape
    if split_b:
        Nh = b.shape[2]
        N = 2 * Nh
    else:
        N = b.shape[1]
        Nh = N
    if b.dtype.itemsize == 2:
        tm = _pick(M, (4096, 2048, 1024, 512, 256, 128))
        tko = _pick(K, (1024, 704, 512, 256, 128))
        tn = _pick(Nh, (512, 256, 128))
    else:
        tm = _pick(M, (1024, 512, 256, 128))
        tko = _pick(K, (1024, 1408, 704, 512, 256, 128))
        tn = _pick(Nh, (1408, 1024, 768, 512, 256, 128))
    nm = M // tm
    nh = Nh // tn

    def body(a_ref, b_ref, o_ref):
        part = _dot(a_ref[...], b_ref[...], NN)
        r = pl.program_id(2)

        @pl.when(r == 0)
        def _():
            o_ref[...] = part

        @pl.when(r > 0)
        def _():
            o_ref[...] += part

    if split_b:
        b_spec = pl.BlockSpec((None, tm, tn), lambda ko, n, r: (n // nh, r, n % nh))
    else:
        b_spec = pl.BlockSpec((tm, tn), lambda ko, n, r: (r, n))
    return pl.pallas_call(
        body, name=name, grid=(K // tko, N // tn, nm),
        in_specs=[pl.BlockSpec((tko, tm), lambda ko, n, r: (ko, r)), b_spec],
        out_specs=pl.BlockSpec((None, tko, tn), lambda ko, n, r: (0, ko, n)),
        out_shape=jax.ShapeDtypeStruct((1, K, N), F32),
        compiler_params=_params(("parallel", "parallel", "arbitrary")),
    )(at, b)


def _rms(x, g):
    return x * lax.rsqrt(jnp.mean(x * x, axis=-1, keepdims=True) + EPS) * g


def rmsnorm_fwd(h, g, l, *, name):
    S, D = h.shape
    tm = _pick(S, (512, 256, 128))

    def body(h_ref, g_ref, o_ref, ot_ref):
        y = _rms(h_ref[...], g_ref[...])
        o_ref[...] = y.astype(BF16)
        ot_ref[...] = y.T.astype(BF16)

    return pl.pallas_call(
        body, name=name, grid=(S // tm,),
        in_specs=[pl.BlockSpec((tm, D), lambda m: (m, 0)),
                  pl.BlockSpec((None, 1, D), lambda m: (l, 0, 0))],
        out_specs=[pl.BlockSpec((tm, D), lambda m: (m, 0)), pl.BlockSpec((D, tm), lambda m: (0, m))],
        out_shape=[jax.ShapeDtypeStruct((S, D), BF16), jax.ShapeDtypeStruct((D, S), BF16)],
        compiler_params=_params(("parallel",)),
    )(h, g)


def loss_head(h, g, target, *, name):
    S, D = h.shape
    tm = _pick(S, (512, 256, 128))

    def body(h_ref, g_ref, t_ref, loss_ref, dh_ref, dg_ref):
        y, vjp = jax.vjp(_rms, h_ref[...], g_ref[...])
        err = y - t_ref[...]
        part = 0.5 * jnp.sum(jnp.sum(err * err, axis=-1, keepdims=True) / D, axis=0, keepdims=True)
        dx, dg = vjp(err / D)
        dh_ref[...] = dx

        @pl.when(pl.program_id(0) == 0)
        def _():
            dg_ref[...] = dg
            loss_ref[...] = part

        @pl.when(pl.program_id(0) > 0)
        def _():
            dg_ref[...] += dg
            loss_ref[...] += part

    row = pl.BlockSpec((tm, D), lambda m: (m, 0))
    return pl.pallas_call(
        body, name=name, grid=(S // tm,),
        in_specs=[row, pl.BlockSpec((1, D), lambda m: (0, 0)), row],
        out_specs=[pl.BlockSpec((1, 1), lambda m: (0, 0)), row, pl.BlockSpec((1, D), lambda m: (0, 0))],
        out_shape=[jax.ShapeDtypeStruct((1, 1), F32), jax.ShapeDtypeStruct((S, D), F32),
                   jax.ShapeDtypeStruct((1, D), F32)],
        compiler_params=_params(("arbitrary",)),
    )(h, g, target)


def _make_mm(dims_fwd, dims_da, dims_db, swap_da=False, swap_db=False):
    @jax.custom_vjp
    def mm(a, b):
        return _dot(a, b, dims_fwd)

    def fwd(a, b):
        return mm(a, b), (a, b)

    def bwd(resid, g):
        a, b = resid
        da = _dot(b, g, dims_da) if swap_da else _dot(g, b, dims_da)
        db = _dot(g, a, dims_db) if swap_db else _dot(a, g, dims_db)
        return da, db

    mm.defvjp(fwd, bwd)
    return mm


_mm = _make_mm(NN, NT, TN)
_mm_nt = _make_mm(NT, NN, TN, swap_db=True)
_mm_tn = _make_mm(TN, NT, NN, swap_da=True)


def _dot_hi(a, b):
    return jnp.dot(a, b, precision=lax.Precision.HIGHEST, preferred_element_type=F32)


def _sigmoid(x):
    return 0.5 * jnp.tanh(0.5 * x) + 0.5


def _silu(x):
    return x * _sigmoid(x)


def _row_slices(x, n):
    rows = x.shape[0] // n
    return tuple(x[i * rows:(i + 1) * rows] for i in range(n))


@functools.partial(jax.custom_vjp, nondiff_argnums=(1,))
def _split_rows(x, n):
    return _row_slices(x, n)


_split_rows.defvjp(lambda x, n: (_row_slices(x, n), None), lambda n, _, cts: (jnp.concatenate(cts, axis=0),))


def _attn_block(qs, kws, vws, sink, slope, first):
    s = jnp.concatenate([_mm_nt(q, kw) for q, kw in zip(qs, kws)], axis=0) * (HEAD_DIM ** -0.5)
    qi = lax.broadcasted_iota(jnp.int32, s.shape, 0) % WINDOW
    ki = lax.broadcasted_iota(jnp.int32, s.shape, 1)
    dist = qi + WINDOW - ki
    valid = (dist >= 0) & (dist < WINDOW) & (ki >= WINDOW * first)
    s = jnp.where(valid, s - slope * dist.astype(F32), -jnp.inf)
    m = lax.stop_gradient(jnp.maximum(jnp.max(s, axis=-1, keepdims=True), sink))
    e = jnp.exp(s - m)
    denom = jnp.sum(e, axis=-1, keepdims=True) + jnp.exp(sink - m)
    ps = _split_rows(e * (1.0 / denom), len(qs))
    return tuple(_mm(p, vw) for p, vw in zip(ps, vws))


def _attn_specs(Hkv, order):
    G = Q_PER_KV
    qspec = pl.BlockSpec((Hkv, G, WINDOW, HEAD_DIM), lambda n: (0, 0, order(n), 0))
    prev = pl.BlockSpec((Hkv, WINDOW, HEAD_DIM), lambda n: (0, jnp.maximum(order(n) - 1, 0), 0))
    cur = pl.BlockSpec((Hkv, WINDOW, HEAD_DIM), lambda n: (0, order(n), 0))
    col = pl.BlockSpec((Hkv, G * WINDOW, 1), lambda n: (0, 0, 0))
    return qspec, prev, cur, col


def attn_fwd(q, k, v, sink_col, slope_col, *, name, comm=None):
    Hkv, G, S, _ = q.shape
    qspec, prev, cur, col = _attn_specs(Hkv, lambda n: n)

    def body(q_ref, kp_ref, kc_ref, vp_ref, vc_ref, sink_ref, slope_ref, o_ref):
        first = (pl.program_id(0) == 0).astype(jnp.int32)
        qs = tuple(q_ref[h].reshape(G * WINDOW, HEAD_DIM) for h in range(Hkv))
        kws = tuple(jnp.concatenate([kp_ref[h], kc_ref[h]], axis=0) for h in range(Hkv))
        vws = tuple(jnp.concatenate([vp_ref[h], vc_ref[h]], axis=0) for h in range(Hkv))
        outs = _attn_block(qs, kws, vws, sink_ref[...].reshape(Hkv * G * WINDOW, 1),
                           slope_ref[...].reshape(Hkv * G * WINDOW, 1), first)
        o_ref[...] = jnp.stack([o.reshape(G, WINDOW, HEAD_DIM).astype(BF16) for o in outs])

    outs, updated, created = _carried_call(
        body, [q, k, k, v, v, sink_col, slope_col], name=name, grid=(S // WINDOW,),
        in_specs=[qspec, prev, cur, prev, cur, col, col], out_specs=[qspec],
        out_shape=[jax.ShapeDtypeStruct(q.shape, BF16)], comm=comm)
    return outs[0] if comm is None else (outs[0], updated, created)


def attn_bwd(q, k, v, sink_col, slope_col, do, *, name, comm=None):
    Hkv, G, S, _ = q.shape
    nb = S // WINDOW
    qspec, prev, cur, col = _attn_specs(Hkv, lambda n: nb - 1 - n)

    def body(q_ref, kp_ref, kc_ref, vp_ref, vc_ref, sink_ref, slope_ref, do_ref,
             dq_ref, dk_ref, dv_ref, dsink_ref, dk_carry, dv_carry):
        @pl.when(pl.program_id(0) == 0)
        def _():
            dk_carry[...] = jnp.zeros_like(dk_carry)
            dv_carry[...] = jnp.zeros_like(dv_carry)
            dsink_ref[...] = jnp.zeros_like(dsink_ref)

        first = (pl.program_id(0) == nb - 1).astype(jnp.int32)
        qs = tuple(q_ref[h].reshape(G * WINDOW, HEAD_DIM).astype(F32) for h in range(Hkv))
        kws = tuple(jnp.concatenate([kp_ref[h], kc_ref[h]], axis=0).astype(F32) for h in range(Hkv))
        vws = tuple(jnp.concatenate([vp_ref[h], vc_ref[h]], axis=0).astype(F32) for h in range(Hkv))
        fn = functools.partial(_attn_block, slope=slope_ref[...].reshape(Hkv * G * WINDOW, 1), first=first)
        _, vjp = jax.vjp(fn, qs, kws, vws, sink_ref[...].reshape(Hkv * G * WINDOW, 1))
        dqs, dkws, dvws, dsink = vjp(tuple(do_ref[h].reshape(G * WINDOW, HEAD_DIM).astype(F32) for h in range(Hkv)))
        dkw, dvw = jnp.stack(dkws), jnp.stack(dvws)
        dq_ref[...] = jnp.stack([dq.reshape(G, WINDOW, HEAD_DIM) for dq in dqs]).astype(BF16)
        dk_ref[...] = (dkw[:, WINDOW:] + dk_carry[...]).astype(BF16)
        dv_ref[...] = (dvw[:, WINDOW:] + dv_carry[...]).astype(BF16)
        dk_carry[...] = dkw[:, :WINDOW]
        dv_carry[...] = dvw[:, :WINDOW]
        dsink_ref[...] += dsink.reshape(Hkv, G * WINDOW, 1)

    kv_shape = jax.ShapeDtypeStruct(k.shape, BF16)
    outs, _, created = _carried_call(
        body, [q, k, k, v, v, sink_col, slope_col, do], name=name, grid=(nb,),
        in_specs=[qspec, prev, cur, prev, cur, col, col, qspec],
        out_specs=[qspec, cur, cur, col],
        out_shape=[jax.ShapeDtypeStruct(q.shape, BF16), kv_shape, kv_shape,
                   jax.ShapeDtypeStruct(sink_col.shape, F32)],
        scratch_shapes=[pltpu.VMEM((Hkv, WINDOW, HEAD_DIM), F32), pltpu.VMEM((Hkv, WINDOW, HEAD_DIM), F32)],
        comm=comm)
    return outs if comm is None else (*outs, created)


def _hgrn_consts():
    C = HG_CHUNK
    t = np.arange(C)[:, None]
    j = np.arange(C)[None, :]
    cq, ck, mk = [], [], []
    H = C // 2
    while H >= 1:
        start = (t // H) * H
        cq.append((j > start) & (j <= t))
        ck.append((j > t) & (j <= start + H))
        mk.append(((t // H) % 2 == 1) & ((j // H) == (t // H) - 1))
        H //= 2
    ltri = (j <= t)
    ops = np.concatenate([ltri] + cq + ck, axis=0).astype(np.float32)
    ops3 = np.concatenate([ops, ops, ops], axis=1)
    opst3 = np.concatenate([ops.T, ops.T, ops.T], axis=1)
    return (jnp.asarray(ops3, BF16), jnp.asarray(opst3, BF16), jnp.asarray(np.stack(mk).astype(np.float32)))


def _split3(x):
    hi = x.astype(BF16)
    r = x - hi.astype(F32)
    mid = r.astype(BF16)
    lo = (r - mid.astype(F32)).astype(BF16)
    return jnp.concatenate([hi, mid, lo], axis=0)


@jax.custom_vjp
def _seg_sums(g, ops3, opst3):
    C = g.shape[0]
    out = jnp.dot(ops3, _split3(g), preferred_element_type=F32)
    return tuple(out[i * C:(i + 1) * C] for i in range(ops3.shape[0] // C))


def _seg_sums_fwd(g, ops3, opst3):
    return _seg_sums(g, ops3, opst3), opst3


def _seg_sums_bwd(opst3, cts):
    ct = jnp.concatenate(cts, axis=0)
    return jnp.dot(opst3, _split3(ct), preferred_element_type=F32), None, None


_seg_sums.defvjp(_seg_sums_fwd, _seg_sums_bwd)


def _head_slices(x):
    return tuple(x[:, h * HG_EXPAND:(h + 1) * HG_EXPAND] for h in range(x.shape[1] // HG_EXPAND))


@jax.custom_vjp
def _split_heads(x):
    return _head_slices(x)


_split_heads.defvjp(lambda x: (_head_slices(x), None), lambda _, cts: (jnp.concatenate(cts, axis=1),))


@jax.custom_vjp
def _join_heads(xs):
    return jnp.concatenate(xs, axis=1)


_join_heads.defvjp(lambda xs: (jnp.concatenate(xs, axis=1), None), lambda _, ct: (_head_slices(ct),))


def _hgrn_chunk(qr, fr, iv, gr, sts, logits, ng, ops3, opst3, mk, layer):
    n_levels = mk.shape[0]
    depth, D = logits.shape
    e = jnp.exp(logits - lax.stop_gradient(jnp.max(logits, axis=0, keepdims=True)))
    sm = e / jnp.sum(e, axis=0, keepdims=True)
    row = lax.broadcasted_iota(jnp.int32, (depth, D), 0)
    lb = jnp.sum(jnp.where((row >= 1) & (row <= layer), sm, 0.0), axis=0, keepdims=True)

    q = _silu(qr)
    a = jnp.log(lb)
    c = jnp.log(1.0 - lb) - (jnp.maximum(-fr, 0.0) + jnp.log(1.0 + jnp.exp(-jnp.abs(fr))))
    mx = lax.stop_gradient(jnp.maximum(a, c))
    g = mx + jnp.log(jnp.exp(a - mx) + jnp.exp(c - mx))
    k = (1.0 - lb) * _sigmoid(-fr)

    sums = _seg_sums(g, ops3, opst3)
    b = sums[0]
    b_last = jnp.sum(g, axis=0, keepdims=True)

    ql = [_split_heads(q * jnp.exp(sums[1 + lv])) for lv in range(n_levels)]
    kl = [_split_heads(k * jnp.exp(sums[1 + n_levels + lv])) for lv in range(n_levels)]
    qb, kb = _split_heads(q * jnp.exp(b)), _split_heads(k * jnp.exp(b_last - b))
    qk, ivh, gate, decay = _split_heads(q * k), _split_heads(iv), _split_heads(_silu(gr)), _split_heads(jnp.exp(b_last))
    outs, new_sts = [], []
    for h, st in enumerate(sts):
        amat = None
        for lv in range(n_levels):
            part = _mm_nt(ql[lv][h], kl[lv][h]) * mk[lv]
            amat = part if amat is None else amat + part
        o = jnp.sum(qk[h], axis=-1, keepdims=True) * ivh[h] + _mm(amat, ivh[h]) + _mm_nt(qb[h], st)
        new_sts.append(st * decay[h] + _mm_tn(ivh[h], kb[h]))
        on = o * lax.rsqrt(jnp.mean(o * o, axis=-1, keepdims=True) + EPS) * ng
        outs.append(on * gate[h])
    return _join_heads(tuple(outs)), tuple(new_sts)


def _hgrn_in_specs(D, depth, order):
    C = HG_CHUNK
    nl = int(math.log2(C))
    n_ops = 1 + 2 * nl
    return [pl.BlockSpec((C, 4 * D), lambda c: (order(c), 0)),
            pl.BlockSpec((depth, D), lambda c: (0, 0)),
            pl.BlockSpec((1, HG_EXPAND), lambda c: (0, 0)),
            pl.BlockSpec((n_ops * C, 3 * C), lambda c: (0, 0)),
            pl.BlockSpec((C, 3 * n_ops * C), lambda c: (0, 0)),
            pl.BlockSpec((nl, C, C), lambda c: (0, 0, 0))]


def hgrn_fwd(proj, logits, ng, layer, *, name, comm=None):
    S = proj.shape[0]
    D = proj.shape[1] // 4
    H = D // HG_EXPAND
    C = HG_CHUNK
    nc = S // C
    consts = _hgrn_consts()

    def body(p_ref, lg_ref, ng_ref, ops3_ref, opst3_ref, mk_ref, o_ref, st_out_ref, st_ref):
        @pl.when(pl.program_id(0) == 0)
        def _():
            st_ref[...] = jnp.zeros_like(st_ref)

        st_out_ref[...] = st_ref[...]
        out, new_sts = _hgrn_chunk(*(p_ref[:, j * D:(j + 1) * D] for j in range(4)), tuple(st_ref[h] for h in range(H)),
                                   lg_ref[...], ng_ref[...], ops3_ref[...], opst3_ref[...], mk_ref[...], layer)
        o_ref[...] = out.astype(BF16)
        for h in range(H):
            st_ref[h] = new_sts[h]

    outs, updated, created = _carried_call(
        body, [proj, logits, ng, *consts], name=name, grid=(nc,),
        in_specs=_hgrn_in_specs(D, logits.shape[0], lambda c: c),
        out_specs=[pl.BlockSpec((C, D), lambda c: (c, 0)),
                   pl.BlockSpec((None, H, HG_EXPAND, HG_EXPAND), lambda c: (c, 0, 0, 0))],
        out_shape=[jax.ShapeDtypeStruct((S, D), BF16),
                   jax.ShapeDtypeStruct((nc, H, HG_EXPAND, HG_EXPAND), F32)],
        scratch_shapes=[pltpu.VMEM((H, HG_EXPAND, HG_EXPAND), F32)], comm=comm)
    return outs if comm is None else (*outs, updated, created)


def hgrn_bwd(proj, logits, ng, states, dout, layer, *, name, comm=None):
    S = proj.shape[0]
    D = proj.shape[1] // 4
    H = D // HG_EXPAND
    C = HG_CHUNK
    nc = S // C
    depth = logits.shape[0]
    consts = _hgrn_consts()
    rev = lambda c: nc - 1 - c

    def body(p_ref, lg_ref, ng_ref, ops3_ref, opst3_ref, mk_ref, st_in_ref, do_ref,
             dp_ref, dlg_ref, dng_ref, dst_ref):
        @pl.when(pl.program_id(0) == 0)
        def _():
            dst_ref[...] = jnp.zeros_like(dst_ref)
            dlg_ref[...] = jnp.zeros_like(dlg_ref)
            dng_ref[...] = jnp.zeros_like(dng_ref)

        fn = functools.partial(_hgrn_chunk, ops3=ops3_ref[...], opst3=opst3_ref[...], mk=mk_ref[...], layer=layer)
        _, vjp = jax.vjp(fn, *(p_ref[:, j * D:(j + 1) * D] for j in range(4)), tuple(st_in_ref[h] for h in range(H)),
                         lg_ref[...], ng_ref[...])
        *dproj, dsts, dlg, dng = vjp((do_ref[...], tuple(dst_ref[h] for h in range(H))))
        for j in range(4):
            dp_ref[:, j * D:(j + 1) * D] = dproj[j].astype(BF16)
        for h in range(H):
            dst_ref[h] = dsts[h]
        dlg_ref[...] += dlg
        dng_ref[...] += dng

    in_specs = _hgrn_in_specs(D, depth, rev)
    in_specs += [pl.BlockSpec((None, H, HG_EXPAND, HG_EXPAND), lambda c: (rev(c), 0, 0, 0)),
                 pl.BlockSpec((C, D), lambda c: (rev(c), 0))]
    outs, _, created = _carried_call(
        body, [proj, logits, ng, *consts, states, dout], name=name, grid=(nc,), in_specs=in_specs,
        out_specs=[pl.BlockSpec((C, 4 * D), lambda c: (rev(c), 0)),
                   pl.BlockSpec((depth, D), lambda c: (0, 0)), pl.BlockSpec((1, HG_EXPAND), lambda c: (0, 0))],
        out_shape=[jax.ShapeDtypeStruct((S, 4 * D), BF16), jax.ShapeDtypeStruct((depth, D), F32),
                   jax.ShapeDtypeStruct((1, HG_EXPAND), F32)],
        scratch_shapes=[pltpu.VMEM((H, HG_EXPAND, HG_EXPAND), F32)], comm=comm)
    return outs if comm is None else (*outs, created)


def _shift_down(u, prev, shift):
    rolled = pltpu.roll(u, shift, axis=0)
    rows = lax.broadcasted_iota(jnp.int32, prev.shape, 0)
    top = jnp.where(rows < shift, pltpu.roll(prev, shift, axis=0), rolled[:SUBLANES])
    return jnp.concatenate([top, rolled[SUBLANES:]], axis=0)


def _shift_up(x, nxt, shift):
    tm = x.shape[0]
    rolled = pltpu.roll(x, tm - shift, axis=0)
    rows = lax.broadcasted_iota(jnp.int32, nxt.shape, 0)
    bottom = jnp.where(rows >= SUBLANES - shift, pltpu.roll(nxt, SUBLANES - shift, axis=0), rolled[tm - SUBLANES:])
    return jnp.concatenate([rolled[:tm - SUBLANES], bottom], axis=0)


def _conv(u, prev, w_ref, b_ref, half):
    u1 = _shift_down(u, prev, 1)
    u2 = _shift_down(u, prev, 2)
    return b_ref[half] + w_ref[half, 0:1, :] * u2 + w_ref[half, 1:2, :] * u1 + w_ref[half, 2:3, :] * u


def _halo_rows(ref, keep, last):
    full = ref.astype(F32)
    return (full[HALO_ROWS - SUBLANES:] if last else full[:SUBLANES]) * keep


def _conv_specs(tm, tc, l):
    nprev = lambda m: jnp.maximum(m * (tm // HALO_ROWS) - 1, 0)
    tile = pl.BlockSpec((2, tm, tc), lambda j, m: (0, m, j))
    halo = pl.BlockSpec((2, HALO_ROWS, tc), lambda j, m: (0, nprev(m), j))
    wspec = pl.BlockSpec((None, 2, 3, tc), lambda j, m: (l, 0, 0, j))
    bspec = pl.BlockSpec((None, 2, 1, tc), lambda j, m: (l, 0, 0, j))
    return tile, halo, wspec, bspec


def convact_fwd(u, cw, cb, l, *, name, comm=None):
    _, S, Fh = u.shape
    tm = _pick(S, (256, 128))
    tc = _pick(Fh, (1408, 512, 256, 128))
    tile, halo, wspec, bspec = _conv_specs(tm, tc, l)

    def body(u_ref, p_ref, w_ref, b_ref, o_ref, ot_ref, c_ref):
        keep = (pl.program_id(1) > 0).astype(F32)
        cg = _conv(u_ref[0].astype(F32), _halo_rows(p_ref[0], keep, True), w_ref, b_ref, 0)
        cv = _conv(u_ref[1].astype(F32), _halo_rows(p_ref[1], keep, True), w_ref, b_ref, 1)
        act = _silu(cg) * cv
        o_ref[...] = act.astype(BF16)
        ot_ref[...] = act.T.astype(BF16)
        c_ref[0] = cg.astype(BF16)
        c_ref[1] = cv.astype(BF16)

    outs, updated, _ = _carried_call(
        body, [u, u, cw, cb], name=name, grid=(Fh // tc, S // tm), in_specs=[tile, halo, wspec, bspec],
        out_specs=[pl.BlockSpec((tm, tc), lambda j, m: (m, j)), pl.BlockSpec((tc, tm), lambda j, m: (j, m)), tile],
        out_shape=[jax.ShapeDtypeStruct((S, Fh), BF16), jax.ShapeDtypeStruct((Fh, S), BF16),
                   jax.ShapeDtypeStruct(u.shape, BF16)], comm=comm)
    return tuple(outs) if comm is None else (*outs, updated)


def convact_bwd(c, u, cw, l, da, *, name):
    _, S, Fh = u.shape
    tm = _pick(S, (256, 128))
    tc = _pick(Fh, (1408, 512, 256, 128))
    nm = S // tm

    def body(c_ref, u_ref, w_ref, da_ref, du_ref, dw_ref, db_ref, carry):
        @pl.when(pl.program_id(1) == 0)
        def _():
            carry[...] = jnp.zeros_like(carry)
            dw_ref[...] = jnp.zeros_like(dw_ref)
            db_ref[...] = jnp.zeros_like(db_ref)

        cg, cv = c_ref[0].astype(F32), c_ref[1].astype(F32)
        da_t = da_ref[...]
        sg = _sigmoid(cg)
        dcv = da_t * (cg * sg)
        dcg = da_t * cv * (sg * (1.0 + cg * (1.0 - sg)))
        rs = lambda t: jnp.sum(t, axis=0, keepdims=True)
        for half, dc in ((0, dcg), (1, dcv)):
            nxt = carry[half]
            d1, d2 = _shift_up(dc, nxt, 1), _shift_up(dc, nxt, 2)
            du = w_ref[half, 2:3, :] * dc + w_ref[half, 1:2, :] * d1 + w_ref[half, 0:1, :] * d2
            du_ref[half] = du.astype(BF16)
            uh = u_ref[half].astype(F32)
            for j, d in enumerate((d2, d1, dc)):
                dw_ref[half, j:j + 1, :] += rs(d * uh)
            db_ref[half] += rs(dc)
            carry[half] = dc[:SUBLANES]

    tile = pl.BlockSpec((2, tm, tc), lambda j, m: (0, nm - 1 - m, j))
    return pl.pallas_call(
        body, name=name, grid=(Fh // tc, nm),
        in_specs=[tile, tile, pl.BlockSpec((None, 2, 3, tc), lambda j, m: (l, 0, 0, j)),
                  pl.BlockSpec((tm, tc), lambda j, m: (nm - 1 - m, j))],
        out_specs=[tile, pl.BlockSpec((2, 3, tc), lambda j, m: (0, 0, j)),
                   pl.BlockSpec((2, 1, tc), lambda j, m: (0, 0, j))],
        out_shape=[jax.ShapeDtypeStruct(u.shape, BF16), jax.ShapeDtypeStruct((2, 3, Fh), F32),
                   jax.ShapeDtypeStruct((2, 1, Fh), F32)],
        scratch_shapes=[pltpu.VMEM((2, SUBLANES, tc), F32)],
        compiler_params=_params(("parallel", "arbitrary")),
    )(c, u, cw, da)


def _adamw_math(w, g, m, v):
    m = ADAM_B1 * m + (1.0 - ADAM_B1) * g
    v = ADAM_B2 * v + (1.0 - ADAM_B2) * (g * g)
    m_hat = m / (1.0 - ADAM_B1 ** ADAM_STEP)
    v_hat = v / (1.0 - ADAM_B2 ** ADAM_STEP)
    delta = -ADAM_LR * (m_hat / (jnp.sqrt(v_hat) + ADAM_EPS) + ADAM_WD * w)
    return delta, m, v


def _row_tile(R, Ccols, n_streams):
    for tr in (512, 256, 128, 64, 32, 16, 8):
        if R % tr == 0 and tr * Ccols * 4 * n_streams * 2 <= VMEM_LIMIT_BYTES // 2:
            return tr
    return R


def adamw(w, g, m, v, *, name):
    R, Ccols = w.shape
    tr = _row_tile(R, Ccols, 7)

    def body(w_ref, g_ref, m_ref, v_ref, d_ref, mo_ref, vo_ref):
        d, mn, vn = _adamw_math(w_ref[...], g_ref[...], m_ref[...], v_ref[...])
        d_ref[...] = d
        mo_ref[...] = mn
        vo_ref[...] = vn

    spec = pl.BlockSpec((tr, Ccols), lambda i: (i, 0))
    shp = jax.ShapeDtypeStruct((R, Ccols), F32)
    return pl.pallas_call(
        body, name=name, grid=(R // tr,), in_specs=[spec] * 4, out_specs=[spec] * 3,
        out_shape=[shp, shp, shp], compiler_params=_params(("parallel",)),
    )(w, g, m, v)


def _prefetch_call(body, pos, args, *, name, grid, in_specs, out_specs, out_shape, sem, aliases=None):
    return pl.pallas_call(
        lambda pos_ref, *refs: body(*refs), name=name,
        grid_spec=pltpu.PrefetchScalarGridSpec(num_scalar_prefetch=1, grid=grid, in_specs=in_specs,
                                               out_specs=out_specs),
        out_shape=out_shape, input_output_aliases={1 + i: o for i, o in (aliases or {}).items()},
        compiler_params=_params(sem),
    )(pos, *args)


def cast_place(w, pos, *, row_sharded, name):
    L, R, Ccols = w.shape

    def body(w_ref, o_ref):
        o_ref[...] = w_ref[...].astype(BF16)

    if row_sharded:
        grid = (L,)
        in_spec = pl.BlockSpec((None, R, Ccols), lambda l, pos: (l, 0, 0))
        out_spec = pl.BlockSpec((None, R, Ccols), lambda l, pos: (l, pos[0], 0))
        out_shape = jax.ShapeDtypeStruct((L, N_CHIPS * R, Ccols), BF16)
        sem = ("parallel",)
    else:
        tr = _pick(R, (512, 256, 128))
        grid = (L, R // tr)
        in_spec = pl.BlockSpec((None, tr, Ccols), lambda l, i, pos: (l, i, 0))
        out_spec = pl.BlockSpec((None, tr, Ccols), lambda l, i, pos: (l, i, pos[0]))
        out_shape = jax.ShapeDtypeStruct((L, R, N_CHIPS * Ccols), BF16)
        sem = ("parallel", "parallel")
    return _prefetch_call(body, pos, [w], name=name, grid=grid, in_specs=[in_spec], out_specs=out_spec,
                          out_shape=out_shape, sem=sem)


def pair_add(g, theirs, pos, *, row_sharded, name):
    def body(g_ref, t_ref, o_ref, ob_ref):
        s = g_ref[...] + t_ref[...]
        o_ref[...] = s
        ob_ref[...] = s.astype(BF16)

    if row_sharded:
        L, _, _, r2, D = g.shape
        grid = (L, N_CHIPS)
        g_spec = pl.BlockSpec((None, None, None, r2, D), lambda l, q, pos: (l, q, pos[1], 0, 0))
        t_spec = pl.BlockSpec((None, None, r2, D), lambda l, q, pos: (l, q, 0, 0))
    else:
        L, K, N = g.shape
        n = N // N_CHIPS
        tr = _pick(K // 2, (256, 128))
        nb = (K // 2) // tr
        grid = (L, nb, N_CHIPS)
        g_spec = pl.BlockSpec((None, tr, n), lambda l, i, j, pos: (l, pos[1] * nb + i, j))
        t_spec = pl.BlockSpec((None, tr, n), lambda l, i, j, pos: (l, i, j))
    return _prefetch_call(body, pos, [g, theirs], name=name, grid=grid, in_specs=[g_spec, t_spec],
                          out_specs=[t_spec, t_spec],
                          out_shape=[jax.ShapeDtypeStruct(theirs.shape, F32), jax.ShapeDtypeStruct(theirs.shape, BF16)],
                          sem=("parallel",) * len(grid))


def chip_add(pair, landed, pos, buf, li, n_layers, *, row_sharded, name):
    def body(p_ref, a_ref, b_ref, c_ref, *rest):
        rest[-1][...] = ((p_ref[...] + a_ref[...].astype(F32)) + b_ref[...].astype(F32)) + c_ref[...].astype(F32)

    if row_sharded:
        _, _, r2, D = pair.shape
        grid = (1,)
        p_spec = pl.BlockSpec((None, None, r2, D), lambda i, pos: (0, pos[0], 0, 0))
        l_specs = [pl.BlockSpec((None, r2, D), functools.partial(lambda i, pos, k: (k, 0, 0), k=k)) for k in range(3)]
        out_spec = pl.BlockSpec((None, None, r2, D), lambda i, pos: (li, pos[1], 0, 0))
        out_shape = jax.ShapeDtypeStruct((n_layers, 2, r2, D), F32)
    else:
        _, K2, N = pair.shape
        n = N // N_CHIPS
        tr = _pick(K2, (256, 128))
        nb = K2 // tr
        grid = (nb,)
        p_spec = pl.BlockSpec((None, tr, n), lambda i, pos: (0, i, pos[0]))
        l_specs = [pl.BlockSpec((None, tr, n), functools.partial(lambda i, pos, k: (k, i, 0), k=k)) for k in range(3)]
        out_spec = pl.BlockSpec((None, tr, n), lambda i, pos: (li, pos[1] * nb + i, 0))
        out_shape = jax.ShapeDtypeStruct((n_layers, 2 * K2, n), F32)
    args, in_specs, aliases = [pair, landed, landed, landed], [p_spec] + l_specs, None
    if buf is not None:
        args.append(buf)
        in_specs.append(pl.BlockSpec(memory_space=pl.ANY))
        aliases = {4: 0}
    return _prefetch_call(body, pos, args, name=name, grid=grid, in_specs=in_specs, out_specs=out_spec,
                          out_shape=out_shape, sem=("parallel",), aliases=aliases)


def _position():
    return lax.axis_index("x"), lax.axis_index("y"), lax.axis_index("c")


def _other_chips(x, y):
    return [(1 - x, y), (x, 1 - y), (1 - x, 1 - y)]


HBM_ANY = pl.BlockSpec(memory_space=pl.ANY)
COMM_PARAMS = pltpu.CompilerParams(has_side_effects=True)


class Comm:
    def __init__(self, ro=(), inout=(), new=(), sems=(), phases=()):
        self.ro, self.inout, self.new, self.sems, self.phases = list(ro), list(inout), list(new), list(sems), list(phases)


def _carried_call(body, args, *, name, grid, in_specs, out_specs, out_shape, scratch_shapes=(), comm=None):
    comm = comm or Comm()
    n_in, n_out, n_sc = len(in_specs), len(out_shape), len(scratch_shapes)
    n_ro, n_io, n_new, n_sem = len(comm.ro), len(comm.inout), len(comm.new), len(comm.sems)
    last = int(np.prod(grid)) - 1

    def wrapped(*refs):
        bounds = np.cumsum([0, n_in, n_ro, n_io, n_out, n_io, n_new, n_sc, n_sem])
        core_in, ro, _, core_out, io, new, core_sc, sems = (refs[a:b] for a, b in zip(bounds[:-1], bounds[1:]))
        step = pl.program_id(0)
        for d in range(1, len(grid)):
            step = step * grid[d] + pl.program_id(d)
        for frac, fn in comm.phases[:-1]:
            pl.when(step == int(frac * last))(functools.partial(fn, ro, io, new, sems))
        body(*core_in, *core_out, *core_sc)
        if comm.phases:
            pl.when(step == last)(functools.partial(comm.phases[-1][1], ro, io, new, sems))

    outs = pl.pallas_call(
        wrapped, name=name, grid=grid,
        in_specs=list(in_specs) + [HBM_ANY] * (n_ro + n_io),
        out_specs=list(out_specs) + [HBM_ANY] * (n_io + n_new),
        out_shape=list(out_shape) + [jax.ShapeDtypeStruct(a.shape, a.dtype) for a in comm.inout] + comm.new,
        input_output_aliases={n_in + n_ro + j: n_out + j for j in range(n_io)},
        scratch_shapes=list(scratch_shapes) + comm.sems,
        compiler_params=pltpu.CompilerParams(dimension_semantics=("arbitrary",) * len(grid),
                                             vmem_limit_bytes=VMEM_LIMIT_BYTES, has_side_effects=bool(comm.phases)),
    )(*args, *comm.ro, *comm.inout)
    return outs[:n_out], outs[n_out:n_out + n_io], outs[n_out + n_io:]


def run_comm(comm, *, name):
    n_ro, n_io, n_new = len(comm.ro), len(comm.inout), len(comm.new)

    def body(*refs):
        bounds = np.cumsum([0, n_ro, n_io, n_io, n_new, len(comm.sems)])
        ro, _, io, new, sems = (refs[a:b] for a, b in zip(bounds[:-1], bounds[1:]))
        for _, fn in comm.phases:
            fn(ro, io, new, sems)

    outs = pl.pallas_call(
        body, name=name, in_specs=[HBM_ANY] * (n_ro + n_io), out_specs=[HBM_ANY] * (n_io + n_new),
        out_shape=[jax.ShapeDtypeStruct(a.shape, a.dtype) for a in comm.inout] + comm.new,
        input_output_aliases={n_ro + j: j for j in range(n_io)},
        scratch_shapes=comm.sems, compiler_params=COMM_PARAMS,
    )(*comm.ro, *comm.inout)
    return outs[:n_io], outs[n_io:]


def _rdma(src, dst, send_sem, recv_sem, device):
    return pltpu.make_async_remote_copy(src_ref=src, dst_ref=dst, send_sem=send_sem, recv_sem=recv_sem,
                                        device_id=device, device_id_type=MESH)


def gather_comm(bufs, slices):
    n = len(slices)

    def region(io, s, h, chip):
        b, li, row_sharded = slices[s]
        ref = io[b]
        if row_sharded:
            r = ref.shape[1] // N_CHIPS
            return ref.at[li, pl.ds(chip * r + h * (r // 2), r // 2), :]
        K, nn = ref.shape[1], ref.shape[2] // N_CHIPS
        return ref.at[li, pl.ds(h * (K // 2), K // 2), pl.ds(chip * nn, nn)]

    def copies(io, sems):
        x, y, c = _position()
        me = 2 * x + y
        chips = _other_chips(x, y)
        ids = [2 * cx + cy for cx, cy in chips]

        def ici(s, k, src):
            return _rdma(region(io, s, c, me), region(io, s, c, src), sems[0].at[s, k], sems[1].at[s, k], (*chips[k], c))

        def d2d(s, k, h):
            return _rdma(region(io, s, h, ids[k]), region(io, s, h, ids[k]), sems[2].at[s, k], sems[3].at[s, k],
                         (x, y, 1 - c))

        return me, ids, c, ici, d2d

    pairs = [(s, k) for s in range(n) for k in range(3)]

    def start(ro, io, new, sems):
        me, _, _, ici, _ = copies(io, sems)
        for s, k in pairs:
            ici(s, k, me).start()

    def pass_on(ro, io, new, sems):
        _, ids, c, ici, d2d = copies(io, sems)
        for s, k in pairs:
            ici(s, k, ids[k]).wait_recv()
            d2d(s, k, c).start()

    def finish(ro, io, new, sems):
        me, _, c, ici, d2d = copies(io, sems)
        for s, k in pairs:
            d2d(s, k, 1 - c).wait_recv()
        for s, k in pairs:
            ici(s, k, me).wait_send()
            d2d(s, k, c).wait_send()

    return Comm(inout=bufs, sems=[pltpu.SemaphoreType.DMA((n, 3))] * 4,
                phases=[(0.0, start), (0.6, pass_on), (1.0, finish)])


def gather_small(shard, *, name):
    n = shard.shape[2]

    def body(s_ref, o_ref, send_sem, recv_sem, local_sem):
        x, y, c = _position()
        me = 2 * x + y
        chips = _other_chips(x, y)
        place = lambda chip: o_ref.at[:, :, pl.ds(chip * n, n)]
        local = pltpu.make_async_copy(s_ref, place(me), local_sem)
        local.start()
        for k in range(3):
            _rdma(s_ref, place(me), send_sem.at[k], recv_sem.at[k], (*chips[k], c)).start()
        for k in range(3):
            _rdma(s_ref, place(2 * chips[k][0] + chips[k][1]), send_sem.at[k], recv_sem.at[k], (*chips[k], c)).wait()
        local.wait()

    return pl.pallas_call(
        body, name=name, in_specs=[HBM_ANY], out_specs=HBM_ANY,
        out_shape=jax.ShapeDtypeStruct((shard.shape[0], shard.shape[1], N_CHIPS * n), shard.dtype),
        scratch_shapes=[pltpu.SemaphoreType.DMA((3,)), pltpu.SemaphoreType.DMA((3,)), pltpu.SemaphoreType.DMA],
        compiler_params=COMM_PARAMS,
    )(shard)


def swap_comm(grads):
    def copies(ro, new, sems):
        x, y, c = _position()
        h = 1 - c
        cps = []
        for s, (a, row_sharded) in enumerate(grads):
            if row_sharded:
                r = a.shape[1] // N_CHIPS
                for q in range(N_CHIPS):
                    cps.append(_rdma(ro[s].at[:, pl.ds(q * r + h * (r // 2), r // 2), :], new[s].at[:, q],
                                     sems[0].at[s, q], sems[1].at[s, q], (x, y, 1 - c)))
            else:
                K = a.shape[1]
                cps.append(_rdma(ro[s].at[:, pl.ds(h * (K // 2), K // 2), :], new[s],
                                 sems[0].at[s, 0], sems[1].at[s, 0], (x, y, 1 - c)))
        return cps

    def start(ro, io, new, sems):
        for cp in copies(ro, new, sems):
            cp.start()

    def finish(ro, io, new, sems):
        for cp in copies(ro, new, sems):
            cp.wait()

    theirs = []
    for a, row_sharded in grads:
        if row_sharded:
            theirs.append(jax.ShapeDtypeStruct((1, N_CHIPS, a.shape[1] // N_CHIPS // 2, a.shape[2]), a.dtype))
        else:
            theirs.append(jax.ShapeDtypeStruct((1, a.shape[1] // 2, a.shape[2]), a.dtype))
    return Comm(ro=[a for a, _ in grads], new=theirs, sems=[pltpu.SemaphoreType.DMA((len(grads), N_CHIPS))] * 2,
                phases=[(0.0, start), (1.0, finish)])


def merge_comms(comms):
    spans, ro, io, new, sems = [], [], [], [], []
    for cm in comms:
        spans.append((len(ro), len(io), len(new), len(sems)))
        ro, io, new, sems = ro + cm.ro, io + cm.inout, new + cm.new, sems + cm.sems

    def phase(frac):
        def run(R, I, N, S):
            for cm, (a, b, d, e) in zip(comms, spans):
                for f, fn in cm.phases:
                    if f == frac:
                        fn(R[a:a + len(cm.ro)], I[b:b + len(cm.inout)], N[d:d + len(cm.new)], S[e:e + len(cm.sems)])
        return run

    fracs = sorted({f for cm in comms for f, _ in cm.phases})
    merged = Comm(ro=ro, inout=io, new=new, sems=sems, phases=[(f, phase(f)) for f in fracs])
    cut = lambda created: [created[d:d + len(cm.new)] for cm, (_, _, d, _) in zip(comms, spans)]
    return merged, cut


def scatter_comm(halves):
    def copies(ro, new, sems):
        x, y, c = _position()
        chips = _other_chips(x, y)
        cps = []
        for s, (a, row_sharded) in enumerate(halves):
            for k in range(3):
                to = 2 * chips[k][0] + chips[k][1]
                if row_sharded:
                    src = ro[s].at[0, to]
                else:
                    n = a.shape[2] // N_CHIPS
                    src = ro[s].at[0, :, pl.ds(to * n, n)]
                cps.append(_rdma(src, new[s].at[k], sems[0].at[s, k], sems[1].at[s, k], (*chips[k], c)))
        return cps

    def start(ro, io, new, sems):
        for cp in copies(ro, new, sems):
            cp.start()

    def finish(ro, io, new, sems):
        for cp in copies(ro, new, sems):
            cp.wait()

    landing = [jax.ShapeDtypeStruct((3, a.shape[2], a.shape[3]) if row_sharded else
                                    (3, a.shape[1], a.shape[2] // N_CHIPS), a.dtype) for a, row_sharded in halves]
    return Comm(ro=[a for a, _ in halves], new=landing, sems=[pltpu.SemaphoreType.DMA((len(halves), 3))] * 2,
                phases=[(0.0, start), (1.0, finish)])


def sibling_join_halves(shards, *, name):
    n_in = len(shards)

    def body(*refs):
        ins = refs[:n_in]
        outs = refs[n_in:2 * n_in]
        send_sem, recv_sem = refs[2 * n_in:]
        x, y, c = _position()
        sibling = (x, y, 1 - c)

        def rows(ref, h):
            R2 = ref.shape[1] // 2
            return ref.at[:, pl.ds(h * R2, R2), :]

        sends = [pltpu.make_async_remote_copy(src_ref=rows(ins[i], c), dst_ref=rows(outs[i], c),
                                              send_sem=send_sem.at[i], recv_sem=recv_sem.at[i],
                                              device_id=sibling, device_id_type=MESH) for i in range(n_in)]
        lands = [pltpu.make_async_remote_copy(src_ref=rows(ins[i], c), dst_ref=rows(outs[i], 1 - c),
                                              send_sem=send_sem.at[i], recv_sem=recv_sem.at[i],
                                              device_id=sibling, device_id_type=MESH) for i in range(n_in)]
        for cp in sends:
            cp.start()
        for i in range(n_in):
            sends[i].wait_send()
            lands[i].wait_recv()

    return pl.pallas_call(
        body, name=name, in_specs=[HBM_ANY] * n_in, out_specs=[HBM_ANY] * n_in,
        out_shape=[jax.ShapeDtypeStruct(a.shape, a.dtype) for a in shards],
        input_output_aliases={i: i for i in range(n_in)},
        scratch_shapes=[pltpu.SemaphoreType.DMA((n_in,)), pltpu.SemaphoreType.DMA((n_in,))],
        compiler_params=COMM_PARAMS,
    )(*shards)


def all_reduce_small(v, *, name):
    R, Ccols = v.shape

    def body(v_ref, o_ref, slots, send_sem, recv_sem):
        x, y, c = _position()
        me = 4 * x + 2 * y + c
        slots[me] = v_ref[...]

        def peer(d):
            return x ^ (d >> 2), y ^ ((d >> 1) & 1), c ^ (d & 1)

        def copy(d, slot):
            return pltpu.make_async_remote_copy(
                src_ref=v_ref, dst_ref=slots.at[slot], send_sem=send_sem.at[d], recv_sem=recv_sem.at[d],
                device_id=peer(d), device_id_type=MESH)

        sends = [copy(d, me) for d in range(1, N_DEV)]
        for cp in sends:
            cp.start()
        for d in range(1, N_DEV):
            px, py, pc = peer(d)
            copy(d, 4 * px + 2 * py + pc).wait_recv()
        for cp in sends:
            cp.wait_send()
        acc = slots[0]
        for d in range(1, N_DEV):
            acc = acc + slots[d]
        o_ref[...] = acc

    return pl.pallas_call(
        body, name=name,
        in_specs=[pl.BlockSpec(memory_space=pltpu.VMEM)], out_specs=pl.BlockSpec(memory_space=pltpu.VMEM),
        out_shape=jax.ShapeDtypeStruct((R, Ccols), F32),
        scratch_shapes=[pltpu.VMEM((N_DEV, R, Ccols), F32), pltpu.SemaphoreType.DMA((N_DEV,)),
                        pltpu.SemaphoreType.DMA((N_DEV,))],
        compiler_params=COMM_PARAMS,
    )(v)


def _pack_small(parts):
    flat = jnp.concatenate([p.reshape(-1).astype(F32) for p in parts])
    n = flat.shape[0]
    rows = -(-n // LANES)
    rows = -(-rows // SUBLANES) * SUBLANES
    return jnp.pad(flat, (0, rows * LANES - n)).reshape(rows, LANES)


def _unpack_small(packed, like):
    flat = packed.reshape(-1)
    out, off = [], 0
    for p in like:
        out.append(flat[off:off + p.size].reshape(p.shape))
        off += p.size
    return out


def kernel(x, norm_mix, norm_ffn, norm_final, attn_w_in, attn_w_out, attn_sinks, hgrn_w_in, hgrn_w_out, hgrn_norm, hgrn_lb_logits, ffn_w_up, ffn_conv_w, ffn_conv_b, ffn_w_down, loss_target, m_norm_mix, m_norm_ffn, m_norm_final, m_attn_w_in, m_attn_w_out, m_attn_sinks, m_hgrn_w_in, m_hgrn_w_out, m_hgrn_norm, m_hgrn_lb_logits, m_ffn_w_up, m_ffn_conv_w, m_ffn_conv_b, m_ffn_w_down, v_norm_mix, v_norm_ffn, v_norm_final, v_attn_w_in, v_attn_w_out, v_attn_sinks, v_hgrn_w_in, v_hgrn_w_out, v_hgrn_norm, v_hgrn_lb_logits, v_ffn_w_up, v_ffn_conv_w, v_ffn_conv_b, v_ffn_w_down):
    S, D = x.shape[1], x.shape[2]
    depth = norm_mix.shape[0]
    n_q = D // HEAD_DIM
    n_kv = n_q // Q_PER_KV
    Fh = ffn_w_down.shape[1] * N_CHIPS
    h = x.reshape(S, D)
    target = loss_target.reshape(S, D)

    col_names = ["attn_w_in", "hgrn_w_in", "ffn_w_up"]
    row_names = ["attn_w_out", "hgrn_w_out", "ffn_w_down"]
    x_, y_, c_ = _position()
    me = 2 * x_ + y_
    pos = jnp.stack([me, c_]).astype(jnp.int32)
    col_w = [cast_place(w, pos, row_sharded=False, name=f"cast_place_{n}")
             for n, w in zip(col_names, (attn_w_in, hgrn_w_in, ffn_w_up))]
    row_w = [cast_place(w, pos, row_sharded=True, name=f"cast_place_{n}")
             for n, w in zip(row_names, (attn_w_out, hgrn_w_out, ffn_w_down))]
    W = dict(zip(col_names + row_names, col_w + row_w))
    is_row = {n: n in row_names for n in W}

    def layer_weights(layer):
        mixer = "attn" if layer % 2 == 0 else "hgrn"
        return [(f"{mixer}_w_in", layer // 2), (f"{mixer}_w_out", layer // 2), ("ffn_w_up", layer), ("ffn_w_down", layer)]

    def gather_of(group):
        names = [n for n, _ in group]
        return names, gather_comm([W[n] for n in names], [(i, li, is_row[n]) for i, (n, li) in enumerate(group)])

    def mixer_weights(layer):
        return layer_weights(layer)[:2] if layer < depth else []

    names_next, comm_next = gather_of(mixer_weights(0))
    filled, _ = run_comm(comm_next, name="gather_mixer0")
    W.update(zip(names_next, filled))
    cw = gather_small(ffn_conv_w, name="gather_conv_w").reshape(depth, 3, 2, Fh).transpose(0, 2, 1, 3)
    cb = ffn_conv_b.reshape(depth, 2, 1, Fh)
    g_mix = norm_mix.reshape(depth, 1, D)
    g_ffn = norm_ffn.reshape(depth, 1, D)

    slopes = jnp.exp2(-8.0 * jnp.arange(1, n_q + 1, dtype=F32) / n_q).reshape(n_kv, Q_PER_KV)
    slope_col = jnp.repeat(slopes, WINDOW, axis=1).reshape(n_kv, Q_PER_KV * WINDOW, 1)

    def to_heads(t, nh):
        return t.reshape(S, nh, HEAD_DIM).transpose(1, 0, 2)

    def from_heads(t):
        return t.transpose(1, 0, 2).reshape(S, -1)

    saved = []
    for layer in range(depth):
        idx = layer // 2
        rec = {"h_mix": h}
        hn, rec["hn_t"] = rmsnorm_fwd(h, g_mix, layer, name=f"norm_mix_fwd{layer}")
        is_attn = layer % 2 == 0
        up_l, down_l = layer_weights(layer)[2:]
        on_mixer = [up_l] if is_attn else [up_l, down_l]
        on_up = [down_l] if is_attn else mixer_weights(layer + 1)
        on_act = mixer_weights(layer + 1)[:1] if is_attn else []
        on_down = mixer_weights(layer + 1)[1:] if is_attn else []
        names_next, comm_next = gather_of(on_mixer)
        if is_attn:
            proj = mm_nn(hn, W["attn_w_in"], idx, name=f"attn_in_fwd{layer}", out_dtype=BF16)
            q = to_heads(proj[:, :n_q * HEAD_DIM], n_q).reshape(n_kv, Q_PER_KV, S, HEAD_DIM)
            k = to_heads(proj[:, n_q * HEAD_DIM:(n_q + n_kv) * HEAD_DIM], n_kv)
            v = to_heads(proj[:, (n_q + n_kv) * HEAD_DIM:], n_kv)
            sink_col = jnp.repeat(attn_sinks[idx].reshape(n_kv, Q_PER_KV), WINDOW, axis=1)
            sink_col = sink_col.reshape(n_kv, Q_PER_KV * WINDOW, 1)
            o_heads, filled, _ = attn_fwd(q, k, v, sink_col, slope_col, name=f"attn_fwd{layer}", comm=comm_next)
            W.update(zip(names_next, filled))
            o = from_heads(o_heads.reshape(n_q, S, HEAD_DIM))
            rec.update(q=q, k=k, v=v, sink_col=sink_col, o=o)
            h = mm_nn(o, W["attn_w_out"], idx, name=f"attn_out_fwd{layer}", res=h)
        else:
            proj = mm_nn(hn, W["hgrn_w_in"], idx, name=f"hgrn_in_fwd{layer}")
            ng = hgrn_norm[idx].reshape(1, HG_EXPAND)
            o, states, filled, _ = hgrn_fwd(proj, hgrn_lb_logits, ng, layer, name=f"hgrn_fwd{layer}", comm=comm_next)
            W.update(zip(names_next, filled))
            rec.update(proj=proj, ng=ng, states=states, o=o)
            h = mm_nn(o, W["hgrn_w_out"], idx, name=f"hgrn_out_fwd{layer}", res=h)
        rec["h_ffn"] = h
        hn2, rec["hn2_t"] = rmsnorm_fwd(h, g_ffn, layer, name=f"norm_ffn_fwd{layer}")
        if on_up:
            names_next, comm_next = gather_of(on_up)
            u, filled = mm_nn(hn2, W["ffn_w_up"], layer, name=f"ffn_up_fwd{layer}", split_out=True, out_dtype=BF16,
                              comm=comm_next)
            W.update(zip(names_next, filled))
        else:
            u = mm_nn(hn2, W["ffn_w_up"], layer, name=f"ffn_up_fwd{layer}", split_out=True, out_dtype=BF16)
        if on_act:
            names_next, comm_next = gather_of(on_act)
            act, rec["act_t"], rec["c"], filled = convact_fwd(u, cw, cb, layer, name=f"ffn_act_fwd{layer}",
                                                              comm=comm_next)
            W.update(zip(names_next, filled))
        else:
            act, rec["act_t"], rec["c"] = convact_fwd(u, cw, cb, layer, name=f"ffn_act_fwd{layer}")
        rec["u"] = u
        if on_down:
            names_next, comm_next = gather_of(on_down)
            h, filled = mm_nn(act, W["ffn_w_down"], layer, name=f"ffn_down_fwd{layer}", res=h, comm=comm_next)
            W.update(zip(names_next, filled))
        else:
            h = mm_nn(act, W["ffn_w_down"], layer, name=f"ffn_down_fwd{layer}", res=h)
        saved.append(rec)

    loss_part, dh, d_norm_final = loss_head(h, norm_final.reshape(1, D), target, name="loss_head")

    n_attn, n_hgrn = attn_w_in.shape[0], hgrn_w_in.shape[0]
    reduced_parts = []
    pending = []

    def swap_of(group, g):
        ordered = sorted(group, key=lambda nl: is_row[nl[0]])
        return ordered, swap_comm([(g[n], is_row[n]) for n, _ in ordered])

    def pair_up(tag, ordered, g, theirs):
        pair, pair_bf16 = {}, {}
        for (n, _), t in zip(ordered, theirs):
            gn = g[n]
            if is_row[n]:
                r = gn.shape[1] // N_CHIPS
                gn = gn.reshape(1, N_CHIPS, 2, r // 2, gn.shape[2])
            pair[n], pair_bf16[n] = pair_add(gn, t, pos, row_sharded=is_row[n], name=f"rs_pair_add_{n}{tag}")
        pending.append((ordered, pair, scatter_comm([(pair_bf16[n], is_row[n]) for n, _ in ordered])))

    def landed_behind(created, cut):
        for (ordered, pair, _), landed in zip(pending, cut(created)):
            reduced_parts.append((ordered, pair, landed))
        pending.clear()

    d_norm_mix, d_norm_ffn = [None] * depth, [None] * depth
    d_conv_w, d_conv_b = [None] * depth, [None] * depth
    d_sinks, d_hgrn_norm = [None] * n_attn, [None] * n_hgrn
    d_logits = jnp.zeros_like(hgrn_lb_logits)
    for layer in reversed(range(depth)):
        idx = layer // 2
        rec = saved[layer]
        g = {"ffn_w_down": mm_dw(rec["act_t"], dh, name=f"ffn_down_dw{layer}")}
        dact = mm_nt(dh, W["ffn_w_down"], layer, name=f"ffn_down_dx{layer}")
        du, dcw, dcb = convact_bwd(rec["c"], rec["u"], cw, layer, dact, name=f"ffn_act_bwd{layer}")
        d_conv_w[layer] = dcw.transpose(1, 0, 2).reshape(3, 2 * Fh)
        d_conv_b[layer] = dcb.reshape(2 * Fh)
        g["ffn_w_up"] = mm_dw(rec["hn2_t"], du, name=f"ffn_up_dw{layer}", split_b=True)
        ordered, swap = swap_of(layer_weights(layer)[2:], g)
        dh, dg, theirs = mm_nt(du, W["ffn_w_up"], layer, name=f"ffn_up_dx{layer}", split_in=True, comm=swap,
                               norm=(rec["h_ffn"], g_ffn, layer, dh))
        pair_up(layer, ordered, g, theirs)
        d_norm_ffn[layer] = dg.reshape(D)
        carried, cut = merge_comms([p[2] for p in pending])
        if layer % 2 == 0:
            g["attn_w_out"] = mm_tn(rec["o"], dh, name=f"attn_out_dw{layer}")
            do = mm_nt(dh, W["attn_w_out"], idx, name=f"attn_out_dx{layer}", out_dtype=BF16)
            do_heads = to_heads(do, n_q).reshape(n_kv, Q_PER_KV, S, HEAD_DIM)
            res = attn_bwd(rec["q"], rec["k"], rec["v"], rec["sink_col"], slope_col, do_heads,
                           name=f"attn_bwd{layer}", comm=carried)
            dq, dk, dv, dsink = res[:4]
            landed_behind(res[-1], cut)
            dproj = jnp.concatenate([from_heads(dq.reshape(n_q, S, HEAD_DIM)), from_heads(dk), from_heads(dv)], axis=1)
            d_sinks[idx] = jnp.sum(dsink.reshape(n_kv, Q_PER_KV, WINDOW), axis=-1).reshape(n_q)
            g["attn_w_in"] = mm_dw(rec["hn_t"], dproj, name=f"attn_in_dw{layer}")
            ordered, swap = swap_of(layer_weights(layer)[:2], g)
            dh, dg, theirs = mm_nt(dproj, W["attn_w_in"], idx, name=f"attn_in_dx{layer}", comm=swap,
                                   norm=(rec["h_mix"], g_mix, layer, dh))
        else:
            g["hgrn_w_out"] = mm_tn(rec["o"], dh, name=f"hgrn_out_dw{layer}")
            do = mm_nt(dh, W["hgrn_w_out"], idx, name=f"hgrn_out_dx{layer}")
            res = hgrn_bwd(rec["proj"], hgrn_lb_logits, rec["ng"], rec["states"], do, layer,
                           name=f"hgrn_bwd{layer}", comm=carried)
            dproj, dlg, dng = res[:3]
            landed_behind(res[-1], cut)
            d_logits = d_logits + dlg
            d_hgrn_norm[idx] = dng.reshape(HG_EXPAND)
            g["hgrn_w_in"] = mm_dw(rec["hn_t"], dproj, name=f"hgrn_in_dw{layer}")
            ordered, swap = swap_of(layer_weights(layer)[:2], g)
            dh, dg, theirs = mm_nt(dproj, W["hgrn_w_in"], idx, name=f"hgrn_in_dx{layer}", comm=swap,
                                   norm=(rec["h_mix"], g_mix, layer, dh))
        pair_up(layer, ordered, g, theirs)
        d_norm_mix[layer] = dg.reshape(D)
    grad_x = dh.reshape(x.shape)
    carried, cut = merge_comms([p[2] for p in pending])
    landed_behind(run_comm(carried, name="rs_scatter_last")[1], cut)

    small_w = [norm_mix, norm_ffn, norm_final, attn_sinks, hgrn_norm, hgrn_lb_logits, ffn_conv_b]
    small_m = [m_norm_mix, m_norm_ffn, m_norm_final, m_attn_sinks, m_hgrn_norm, m_hgrn_lb_logits, m_ffn_conv_b]
    small_v = [v_norm_mix, v_norm_ffn, v_norm_final, v_attn_sinks, v_hgrn_norm, v_hgrn_lb_logits, v_ffn_conv_b]
    small_grads = [jnp.stack(d_norm_mix), jnp.stack(d_norm_ffn), d_norm_final.reshape(D), jnp.stack(d_sinks),
                   jnp.stack(d_hgrn_norm), d_logits, jnp.stack(d_conv_b)]
    gconv = jnp.stack(d_conv_w)
    packed_small = _pack_small(small_grads + [loss_part])
    summed = all_reduce_small(jnp.concatenate([packed_small, _pack_small([gconv])], axis=0), name="all_reduce_small")
    reduced, gconv_rows = summed[:packed_small.shape[0]], summed[packed_small.shape[0]:]
    small_g = _unpack_small(reduced, small_w + [loss_part])
    loss = small_g.pop().reshape(())
    pad = [jnp.zeros((1, 1), F32)]
    sd, sm_, sv_ = adamw(_pack_small(small_w + pad), reduced, _pack_small(small_m + pad),
                         _pack_small(small_v + pad), name="adamw_small")
    small_d = _unpack_small(sd, small_w)
    small_nm = _unpack_small(sm_, small_w)
    small_nv = _unpack_small(sv_, small_w)

    big_names = col_names + row_names
    big_w = dict(attn_w_in=attn_w_in, hgrn_w_in=hgrn_w_in, ffn_w_up=ffn_w_up, attn_w_out=attn_w_out,
                 hgrn_w_out=hgrn_w_out, ffn_w_down=ffn_w_down)
    half_shard = dict.fromkeys(big_names)
    for ordered, pair, landed in reduced_parts:
        for (n, li), ld in zip(ordered, landed):
            half_shard[n] = chip_add(pair[n], ld, pos, half_shard[n], li, big_w[n].shape[0], row_sharded=is_row[n],
                                     name=f"rs_chip_add_{n}{li}")
    shards = sibling_join_halves([half_shard[n].reshape(big_w[n].shape) for n in big_names], name="rs_sibling_join")
    big_m = dict(attn_w_in=m_attn_w_in, hgrn_w_in=m_hgrn_w_in, ffn_w_up=m_ffn_w_up, attn_w_out=m_attn_w_out,
                 hgrn_w_out=m_hgrn_w_out, ffn_w_down=m_ffn_w_down)
    big_v = dict(attn_w_in=v_attn_w_in, hgrn_w_in=v_hgrn_w_in, ffn_w_up=v_ffn_w_up, attn_w_out=v_attn_w_out,
                 hgrn_w_out=v_hgrn_w_out, ffn_w_down=v_ffn_w_down)
    grads, deltas, new_m, new_v = {}, {}, {}, {}
    for name_, gshard in zip(big_names, shards):
        w_ = big_w[name_]
        cols = w_.shape[-1]
        d_, m_, v_ = adamw(w_.reshape(-1, cols), gshard.reshape(-1, cols), big_m[name_].reshape(-1, cols),
                           big_v[name_].reshape(-1, cols), name=f"adamw_{name_}")
        grads[name_] = gshard.reshape(w_.shape)
        deltas[name_], new_m[name_], new_v[name_] = (t.reshape(w_.shape) for t in (d_, m_, v_))

    n_conv = ffn_conv_w.shape[2]
    gconv_all = _unpack_small(gconv_rows, [gconv])[0]
    gconv_mine = lax.dynamic_slice_in_dim(gconv_all, me * n_conv, n_conv, axis=2)
    cshape = ffn_conv_w.shape
    d_, m_, v_ = adamw(ffn_conv_w.reshape(-1, n_conv), gconv_mine.reshape(-1, n_conv),
                       m_ffn_conv_w.reshape(-1, n_conv), v_ffn_conv_w.reshape(-1, n_conv), name="adamw_ffn_conv_w")
    grads["ffn_conv_w"] = gconv_mine
    deltas["ffn_conv_w"], new_m["ffn_conv_w"], new_v["ffn_conv_w"] = (t.reshape(cshape) for t in (d_, m_, v_))

    order = ["norm_mix", "norm_ffn", "norm_final", "attn_w_in", "attn_w_out", "attn_sinks", "hgrn_w_in",
             "hgrn_w_out", "hgrn_norm", "hgrn_lb_logits", "ffn_w_up", "ffn_conv_w", "ffn_conv_b", "ffn_w_down"]
    small_names = ["norm_mix", "norm_ffn", "norm_final", "attn_sinks", "hgrn_norm", "hgrn_lb_logits", "ffn_conv_b"]
    for i, name_ in enumerate(small_names):
        grads[name_], deltas[name_], new_m[name_], new_v[name_] = small_g[i], small_d[i], small_nm[i], small_nv[i]
    return (loss, grad_x, *[grads[n] for n in order], *[deltas[n] for n in order],
            *[new_m[n] for n in order], *[new_v[n] for n in order])
```

```python
import functools
import math

import numpy as np
import jax
import jax.numpy as jnp
from jax import lax
from jax.experimental import pallas as pl
from jax.experimental.pallas import tpu as pltpu

F32 = jnp.float32
BF16 = jnp.bfloat16
MESH = pl.DeviceIdType.MESH

HEAD_DIM = 64
Q_PER_KV = 4
WINDOW = 128
HG_EXPAND = 128
HG_CHUNK = 128
EPS = 1e-6
N_CHIPS = 4
N_DEV = 8

ADAM_LR = 0.001
ADAM_B1 = 0.9
ADAM_B2 = 0.999
ADAM_EPS = 1e-08
ADAM_WD = 0.01
ADAM_STEP = 10

VMEM_LIMIT_BYTES = 48 * 1024 * 1024
LANES = 128
SUBLANES = 8
HALO_ROWS = 16


def _params(sem=None):
    return pltpu.CompilerParams(dimension_semantics=sem, vmem_limit_bytes=VMEM_LIMIT_BYTES)


def _pick(n, cands):
    for c in cands:
        if n % c == 0:
            return c
    return n


def _dot(a, b, dims):
    return lax.dot_general(a.astype(BF16), b.astype(BF16), (dims, ((), ())),
                           preferred_element_type=F32)


NN = ((1,), (0,))
NT = ((1,), (1,))
TN = ((0,), (0,))


def mm_nn(a, w, l, *, name, res=None, out_dtype=F32, split_out=False, comm=None):
    M, K = a.shape
    N = w.shape[2]
    tk = K if K <= 2816 else _pick(K, (1024, 1408, 512, 256, 128))
    tm = _pick(M, (1024, 512, 256, 128) if tk <= 1408 else (512, 256, 128))
    tn = _pick(N // 2 if split_out else N, (1408, 1024, 768, 512, 256, 128))
    nk = K // tk
    nh = (N // 2) // tn

    def body(*refs):
        if res is None:
            a_ref, w_ref, o_ref = refs[:3]
            r_ref = None
        else:
            a_ref, w_ref, r_ref, o_ref = refs[:4]
        part = _dot(a_ref[...], w_ref[...], NN)

        def finish(acc):
            if r_ref is not None:
                acc = acc + r_ref[...]
            o_ref[...] = acc.astype(out_dtype)

        if nk == 1:
            finish(part)
        else:
            acc_ref = refs[-1]
            k = pl.program_id(2)

            @pl.when(k == 0)
            def _():
                acc_ref[...] = part

            @pl.when(k > 0)
            def _():
                acc_ref[...] += part

            @pl.when(k == nk - 1)
            def _():
                finish(acc_ref[...])

    in_specs = [pl.BlockSpec((tm, tk), lambda n, m, k: (m, k)),
                pl.BlockSpec((None, tk, tn), lambda n, m, k: (l, k, n))]
    args = [a, w]
    if res is not None:
        in_specs.append(pl.BlockSpec((tm, tn), lambda n, m, k: (m, n)))
        args.append(res)
    if split_out:
        out_shape = jax.ShapeDtypeStruct((2, M, N // 2), out_dtype)
        out_spec = pl.BlockSpec((None, tm, tn), lambda n, m, k: (n // nh, m, n % nh))
    else:
        out_shape = jax.ShapeDtypeStruct((M, N), out_dtype)
        out_spec = pl.BlockSpec((tm, tn), lambda n, m, k: (m, n))
    outs, updated, _ = _carried_call(
        body, args, name=name, grid=(N // tn, M // tm, nk), in_specs=in_specs, out_specs=[out_spec],
        out_shape=[out_shape], scratch_shapes=[] if nk == 1 else [pltpu.VMEM((tm, tn), F32)], comm=comm)
    return outs[0] if comm is None else (outs[0], updated)


def mm_nt(a, w, l, *, name, split_in=False, out_dtype=F32, comm=None, norm=None):
    if split_in:
        _, M, Nh = a.shape
        N = 2 * Nh
    else:
        M, N = a.shape
        Nh = N
    K = w.shape[1]
    tm = _pick(M, (512, 256, 128) if norm is not None else (1024, 512, 256, 128))
    tko = _pick(K, (1024, 1408, 512, 256, 128))
    tr = _pick(Nh, (2816, 2048, 1536, 1408, 1024, 768, 512, 256, 128) if tm <= 512 else (1024, 1408, 768, 512, 256, 128))
    nr = N // tr
    nh = Nh // tr
    n_in, n_out = (5, 2) if norm is not None else (2, 1)
    assert norm is None or tko == K

    def body(*refs):
        a_ref, w_ref = refs[:2]
        o_ref = refs[n_in]
        part = _dot(a_ref[...], w_ref[...], NT)

        def finish(acc):
            if norm is None:
                o_ref[...] = acc.astype(out_dtype)
                return
            h_ref, g_ref, dres_ref = refs[2:5]
            dg_ref = refs[n_in + 1]
            _, vjp = jax.vjp(_rms, h_ref[...], g_ref[...])
            dx, dg = vjp(acc)
            o_ref[...] = dres_ref[...] + dx
            first = pl.program_id(1) == 0

            @pl.when(first)
            def _():
                dg_ref[...] = dg

            @pl.when(jnp.logical_not(first))
            def _():
                dg_ref[...] += dg

        if nr == 1:
            finish(part)
        else:
            acc_ref = refs[n_in + n_out]
            r = pl.program_id(2)

            @pl.when(r == 0)
            def _():
                acc_ref[...] = part

            @pl.when(r > 0)
            def _():
                acc_ref[...] += part

            @pl.when(r == nr - 1)
            def _():
                finish(acc_ref[...])

    if split_in:
        a_spec = pl.BlockSpec((None, tm, tr), lambda ko, m, r: (r // nh, m, r % nh))
    else:
        a_spec = pl.BlockSpec((tm, tr), lambda ko, m, r: (m, r))
    args = [a, w]
    in_specs = [a_spec, pl.BlockSpec((None, tko, tr), lambda ko, m, r: (l, ko, r))]
    out_specs = [pl.BlockSpec((tm, tko), lambda ko, m, r: (m, ko))]
    out_shape = [jax.ShapeDtypeStruct((M, K), out_dtype)]
    if norm is not None:
        h, g, l_norm, dres = norm
        row = pl.BlockSpec((tm, K), lambda ko, m, r: (m, 0))
        args += [h, g, dres]
        in_specs += [row, pl.BlockSpec((None, 1, K), lambda ko, m, r: (l_norm, 0, 0)), row]
        out_specs.append(pl.BlockSpec((1, K), lambda ko, m, r: (0, 0)))
        out_shape.append(jax.ShapeDtypeStruct((1, K), F32))
    outs, _, created = _carried_call(
        body, args, name=name, grid=(K // tko, M // tm, nr), in_specs=in_specs, out_specs=out_specs,
        out_shape=out_shape, scratch_shapes=[] if nr == 1 else [pltpu.VMEM((tm, tko), F32)], comm=comm)
    outs = list(outs) + ([] if comm is None else [created])
    return outs[0] if len(outs) == 1 else tuple(outs)


def mm_tn(a, b, *, name):
    M, K = a.shape
    N = b.shape[1]
    tm = _pick(M, (4096, 2048, 1024, 512, 256, 128))
    tko = _pick(K, (512, 256, 128))
    tn = _pick(N, (512, 256, 128))
    nm = M // tm

    def body(a_ref, b_ref, o_ref):
        part = _dot(a_ref[...], b_ref[...], TN)
        r = pl.program_id(2)

        @pl.when(r == 0)
        def _():
            o_ref[...] = part

        @pl.when(r > 0)
        def _():
            o_ref[...] += part

    return pl.pallas_call(
        body, name=name, grid=(K // tko, N // tn, nm),
        in_specs=[pl.BlockSpec((tm, tko), lambda ko, n, r: (r, ko)), pl.BlockSpec((tm, tn), lambda ko, n, r: (r, n))],
        out_specs=pl.BlockSpec((None, tko, tn), lambda ko, n, r: (0, ko, n)),
        out_shape=jax.ShapeDtypeStruct((1, K, N), F32),
        compiler_params=_params(("parallel", "parallel", "arbitrary")),
    )(a, b)


def mm_dw(at, b, *, name, split_b=False):
    K, M = at.shape
    if split_b:
        Nh = b.shape[2]
        N = 2 * Nh
    else:
        N = b.shape[1]
        Nh = N
    if b.dtype.itemsize == 2:
        tm = _pick(M, (4096, 2048, 1024, 512, 256, 128))
        tko = _pick(K, (1024, 704, 512, 256, 128))
        tn = _pick(Nh, (512, 256, 128))
    else:
        tm = _pick(M, (1024, 512, 256, 128))
        tko = _pick(K, (1024, 1408, 704, 512, 256, 128))
        tn = _pick(Nh, (1408, 1024, 768, 512, 256, 128))
    nm = M // tm
    nh = Nh // tn

    def body(a_ref, b_ref, o_ref):
        part = _dot(a_ref[...], b_ref[...], NN)
        r = pl.program_id(2)

        @pl.when(r == 0)
        def _():
            o_ref[...] = part

        @pl.when(r > 0)
        def _():
            o_ref[...] += part

    if split_b:
        b_spec = pl.BlockSpec((None, tm, tn), lambda ko, n, r: (n // nh, r, n % nh))
    else:
        b_spec = pl.BlockSpec((tm, tn), lambda ko, n, r: (r, n))
    return pl.pallas_call(
        body, name=name, grid=(K // tko, N // tn, nm),
        in_specs=[pl.BlockSpec((tko, tm), lambda ko, n, r: (ko, r)), b_spec],
        out_specs=pl.BlockSpec((None, tko, tn), lambda ko, n, r: (0, ko, n)),
        out_shape=jax.ShapeDtypeStruct((1, K, N), F32),
        compiler_params=_params(("parallel", "parallel", "arbitrary")),
    )(at, b)


def _rms(x, g):
    return x * lax.rsqrt(jnp.mean(x * x, axis=-1, keepdims=True) + EPS) * g


def rmsnorm_fwd(h, g, l, *, name):
    S, D = h.shape
    tm = _pick(S, (512, 256, 128))

    def body(h_ref, g_ref, o_ref, ot_ref):
        y = _rms(h_ref[...], g_ref[...])
        o_ref[...] = y.astype(BF16)
        ot_ref[...] = y.T.astype(BF16)

    return pl.pallas_call(
        body, name=name, grid=(S // tm,),
        in_specs=[pl.BlockSpec((tm, D), lambda m: (m, 0)),
                  pl.BlockSpec((None, 1, D), lambda m: (l, 0, 0))],
        out_specs=[pl.BlockSpec((tm, D), lambda m: (m, 0)), pl.BlockSpec((D, tm), lambda m: (0, m))],
        out_shape=[jax.ShapeDtypeStruct((S, D), BF16), jax.ShapeDtypeStruct((D, S), BF16)],
        compiler_params=_params(("parallel",)),
    )(h, g)


def loss_head(h, g, target, *, name):
    S, D = h.shape
    tm = _pick(S, (512, 256, 128))

    def body(h_ref, g_ref, t_ref, loss_ref, dh_ref, dg_ref):
        y, vjp = jax.vjp(_rms, h_ref[...], g_ref[...])
        err = y - t_ref[...]
        part = 0.5 * jnp.sum(jnp.sum(err * err, axis=-1, keepdims=True) / D, axis=0, keepdims=True)
        dx, dg = vjp(err / D)
        dh_ref[...] = dx

        @pl.when(pl.program_id(0) == 0)
        def _():
            dg_ref[...] = dg
            loss_ref[...] = part

        @pl.when(pl.program_id(0) > 0)
        def _():
            dg_ref[...] += dg
            loss_ref[...] += part

    row = pl.BlockSpec((tm, D), lambda m: (m, 0))
    return pl.pallas_call(
        body, name=name, grid=(S // tm,),
        in_specs=[row, pl.BlockSpec((1, D), lambda m: (0, 0)), row],
        out_specs=[pl.BlockSpec((1, 1), lambda m: (0, 0)), row, pl.BlockSpec((1, D), lambda m: (0, 0))],
        out_shape=[jax.ShapeDtypeStruct((1, 1), F32), jax.ShapeDtypeStruct((S, D), F32),
                   jax.ShapeDtypeStruct((1, D), F32)],
        compiler_params=_params(("arbitrary",)),
    )(h, g, target)


def _make_mm(dims_fwd, dims_da, dims_db, swap_da=False, swap_db=False):
    @jax.custom_vjp
    def mm(a, b):
        return _dot(a, b, dims_fwd)

    def fwd(a, b):
        return mm(a, b), (a, b)

    def bwd(resid, g):
        a, b = resid
        da = _dot(b, g, dims_da) if swap_da else _dot(g, b, dims_da)
        db = _dot(g, a, dims_db) if swap_db else _dot(a, g, dims_db)
        return da, db

    mm.defvjp(fwd, bwd)
    return mm


_mm = _make_mm(NN, NT, TN)
_mm_nt = _make_mm(NT, NN, TN, swap_db=True)
_mm_tn = _make_mm(TN, NT, NN, swap_da=True)


def _dot_hi(a, b):
    return jnp.dot(a, b, precision=lax.Precision.HIGHEST, preferred_element_type=F32)


def _sigmoid(x):
    return 0.5 * jnp.tanh(0.5 * x) + 0.5


def _silu(x):
    return x * _sigmoid(x)


def _row_slices(x, n):
    rows = x.shape[0] // n
    return tuple(x[i * rows:(i + 1) * rows] for i in range(n))


@functools.partial(jax.custom_vjp, nondiff_argnums=(1,))
def _split_rows(x, n):
    return _row_slices(x, n)


_split_rows.defvjp(lambda x, n: (_row_slices(x, n), None), lambda n, _, cts: (jnp.concatenate(cts, axis=0),))


def _attn_block(qs, kws, vws, sink, slope, first):
    s = jnp.concatenate([_mm_nt(q, kw) for q, kw in zip(qs, kws)], axis=0) * (HEAD_DIM ** -0.5)
    qi = lax.broadcasted_iota(jnp.int32, s.shape, 0) % WINDOW
    ki = lax.broadcasted_iota(jnp.int32, s.shape, 1)
    dist = qi + WINDOW - ki
    valid = (dist >= 0) & (dist < WINDOW) & (ki >= WINDOW * first)
    s = jnp.where(valid, s - slope * dist.astype(F32), -jnp.inf)
    m = lax.stop_gradient(jnp.maximum(jnp.max(s, axis=-1, keepdims=True), sink))
    e = jnp.exp(s - m)
    denom = jnp.sum(e, axis=-1, keepdims=True) + jnp.exp(sink - m)
    ps = _split_rows(e * (1.0 / denom), len(qs))
    return tuple(_mm(p, vw) for p, vw in zip(ps, vws))


def _attn_specs(Hkv, order):
    G = Q_PER_KV
    qspec = pl.BlockSpec((Hkv, G, WINDOW, HEAD_DIM), lambda n: (0, 0, order(n), 0))
    prev = pl.BlockSpec((Hkv, WINDOW, HEAD_DIM), lambda n: (0, jnp.maximum(order(n) - 1, 0), 0))
    cur = pl.BlockSpec((Hkv, WINDOW, HEAD_DIM), lambda n: (0, order(n), 0))
    col = pl.BlockSpec((Hkv, G * WINDOW, 1), lambda n: (0, 0, 0))
    return qspec, prev, cur, col


def attn_fwd(q, k, v, sink_col, slope_col, *, name, comm=None):
    Hkv, G, S, _ = q.shape
    qspec, prev, cur, col = _attn_specs(Hkv, lambda n: n)

    def body(q_ref, kp_ref, kc_ref, vp_ref, vc_ref, sink_ref, slope_ref, o_ref):
        first = (pl.program_id(0) == 0).astype(jnp.int32)
        qs = tuple(q_ref[h].reshape(G * WINDOW, HEAD_DIM) for h in range(Hkv))
        kws = tuple(jnp.concatenate([kp_ref[h], kc_ref[h]], axis=0) for h in range(Hkv))
        vws = tuple(jnp.concatenate([vp_ref[h], vc_ref[h]], axis=0) for h in range(Hkv))
        outs = _attn_block(qs, kws, vws, sink_ref[...].reshape(Hkv * G * WINDOW, 1),
                           slope_ref[...].reshape(Hkv * G * WINDOW, 1), first)
        o_ref[...] = jnp.stack([o.reshape(G, WINDOW, HEAD_DIM).astype(BF16) for o in outs])

    outs, updated, created = _carried_call(
        body, [q, k, k, v, v, sink_col, slope_col], name=name, grid=(S // WINDOW,),
        in_specs=[qspec, prev, cur, prev, cur, col, col], out_specs=[qspec],
        out_shape=[jax.ShapeDtypeStruct(q.shape, BF16)], comm=comm)
    return outs[0] if comm is None else (outs[0], updated, created)


def attn_bwd(q, k, v, sink_col, slope_col, do, *, name, comm=None):
    Hkv, G, S, _ = q.shape
    nb = S // WINDOW
    qspec, prev, cur, col = _attn_specs(Hkv, lambda n: nb - 1 - n)

    def body(q_ref, kp_ref, kc_ref, vp_ref, vc_ref, sink_ref, slope_ref, do_ref,
             dq_ref, dk_ref, dv_ref, dsink_ref, dk_carry, dv_carry):
        @pl.when(pl.program_id(0) == 0)
        def _():
            dk_carry[...] = jnp.zeros_like(dk_carry)
            dv_carry[...] = jnp.zeros_like(dv_carry)
            dsink_ref[...] = jnp.zeros_like(dsink_ref)

        first = (pl.program_id(0) == nb - 1).astype(jnp.int32)
        qs = tuple(q_ref[h].reshape(G * WINDOW, HEAD_DIM).astype(F32) for h in range(Hkv))
        kws = tuple(jnp.concatenate([kp_ref[h], kc_ref[h]], axis=0).astype(F32) for h in range(Hkv))
        vws = tuple(jnp.concatenate([vp_ref[h], vc_ref[h]], axis=0).astype(F32) for h in range(Hkv))
        fn = functools.partial(_attn_block, slope=slope_ref[...].reshape(Hkv * G * WINDOW, 1), first=first)
        _, vjp = jax.vjp(fn, qs, kws, vws, sink_ref[...].reshape(Hkv * G * WINDOW, 1))
        dqs, dkws, dvws, dsink = vjp(tuple(do_ref[h].reshape(G * WINDOW, HEAD_DIM).astype(F32) for h in range(Hkv)))
        dkw, dvw = jnp.stack(dkws), jnp.stack(dvws)
        dq_ref[...] = jnp.stack([dq.reshape(G, WINDOW, HEAD_DIM) for dq in dqs]).astype(BF16)
        dk_ref[...] = (dkw[:, WINDOW:] + dk_carry[...]).astype(BF16)
        dv_ref[...] = (dvw[:, WINDOW:] + dv_carry[...]).astype(BF16)
        dk_carry[...] = dkw[:, :WINDOW]
        dv_carry[...] = dvw[:, :WINDOW]
        dsink_ref[...] += dsink.reshape(Hkv, G * WINDOW, 1)

    kv_shape = jax.ShapeDtypeStruct(k.shape, BF16)
    outs, _, created = _carried_call(
        body, [q, k, k, v, v, sink_col, slope_col, do], name=name, grid=(nb,),
        in_specs=[qspec, prev, cur, prev, cur, col, col, qspec],
        out_specs=[qspec, cur, cur, col],
        out_shape=[jax.ShapeDtypeStruct(q.shape, BF16), kv_shape, kv_shape,
                   jax.ShapeDtypeStruct(sink_col.shape, F32)],
        scratch_shapes=[pltpu.VMEM((Hkv, WINDOW, HEAD_DIM), F32), pltpu.VMEM((Hkv, WINDOW, HEAD_DIM), F32)],
        comm=comm)
    return outs if comm is None else (*outs, created)


def _hgrn_consts():
    C = HG_CHUNK
    t = np.arange(C)[:, None]
    j = np.arange(C)[None, :]
    cq, ck, mk = [], [], []
    H = C // 2
    while H >= 1:
        start = (t // H) * H
        cq.append((j > start) & (j <= t))
        ck.append((j > t) & (j <= start + H))
        mk.append(((t // H) % 2 == 1) & ((j // H) == (t // H) - 1))
        H //= 2
    ltri = (j <= t)
    ops = np.concatenate([ltri] + cq + ck, axis=0).astype(np.float32)
    ops3 = np.concatenate([ops, ops, ops], axis=1)
    opst3 = np.concatenate([ops.T, ops.T, ops.T], axis=1)
    return (jnp.asarray(ops3, BF16), jnp.asarray(opst3, BF16), jnp.asarray(np.stack(mk).astype(np.float32)))


def _split3(x):
    hi = x.astype(BF16)
    r = x - hi.astype(F32)
    mid = r.astype(BF16)
    lo = (r - mid.astype(F32)).astype(BF16)
    return jnp.concatenate([hi, mid, lo], axis=0)


@jax.custom_vjp
def _seg_sums(g, ops3, opst3):
    C = g.shape[0]
    out = jnp.dot(ops3, _split3(g), preferred_element_type=F32)
    return tuple(out[i * C:(i + 1) * C] for i in range(ops3.shape[0] // C))


def _seg_sums_fwd(g, ops3, opst3):
    return _seg_sums(g, ops3, opst3), opst3


def _seg_sums_bwd(opst3, cts):
    ct = jnp.concatenate(cts, axis=0)
    return jnp.dot(opst3, _split3(ct), preferred_element_type=F32), None, None


_seg_sums.defvjp(_seg_sums_fwd, _seg_sums_bwd)


def _head_slices(x):
    return tuple(x[:, h * HG_EXPAND:(h + 1) * HG_EXPAND] for h in range(x.shape[1] // HG_EXPAND))


@jax.custom_vjp
def _split_heads(x):
    return _head_slices(x)


_split_heads.defvjp(lambda x: (_head_slices(x), None), lambda _, cts: (jnp.concatenate(cts, axis=1),))


@jax.custom_vjp
def _join_heads(xs):
    return jnp.concatenate(xs, axis=1)


_join_heads.defvjp(lambda xs: (jnp.concatenate(xs, axis=1), None), lambda _, ct: (_head_slices(ct),))


def _hgrn_chunk(qr, fr, iv, gr, sts, logits, ng, ops3, opst3, mk, layer):
    n_levels = mk.shape[0]
    depth, D = logits.shape
    e = jnp.exp(logits - lax.stop_gradient(jnp.max(logits, axis=0, keepdims=True)))
    sm = e / jnp.sum(e, axis=0, keepdims=True)
    row = lax.broadcasted_iota(jnp.int32, (depth, D), 0)
    lb = jnp.sum(jnp.where((row >= 1) & (row <= layer), sm, 0.0), axis=0, keepdims=True)

    q = _silu(qr)
    a = jnp.log(lb)
    c = jnp.log(1.0 - lb) - (jnp.maximum(-fr, 0.0) + jnp.log(1.0 + jnp.exp(-jnp.abs(fr))))
    mx = lax.stop_gradient(jnp.maximum(a, c))
    g = mx + jnp.log(jnp.exp(a - mx) + jnp.exp(c - mx))
    k = (1.0 - lb) * _sigmoid(-fr)

    sums = _seg_sums(g, ops3, opst3)
    b = sums[0]
    b_last = jnp.sum(g, axis=0, keepdims=True)

    ql = [_split_heads(q * jnp.exp(sums[1 + lv])) for lv in range(n_levels)]
    kl = [_split_heads(k * jnp.exp(sums[1 + n_levels + lv])) for lv in range(n_levels)]
    qb, kb = _split_heads(q * jnp.exp(b)), _split_heads(k * jnp.exp(b_last - b))
    qk, ivh, gate, decay = _split_heads(q * k), _split_heads(iv), _split_heads(_silu(gr)), _split_heads(jnp.exp(b_last))
    outs, new_sts = [], []
    for h, st in enumerate(sts):
        amat = None
        for lv in range(n_levels):
            part = _mm_nt(ql[lv][h], kl[lv][h]) * mk[lv]
            amat = part if amat is None else amat + part
        o = jnp.sum(qk[h], axis=-1, keepdims=True) * ivh[h] + _mm(amat, ivh[h]) + _mm_nt(qb[h], st)
        new_sts.append(st * decay[h] + _mm_tn(ivh[h], kb[h]))
        on = o * lax.rsqrt(jnp.mean(o * o, axis=-1, keepdims=True) + EPS) * ng
        outs.append(on * gate[h])
    return _join_heads(tuple(outs)), tuple(new_sts)


def _hgrn_in_specs(D, depth, order):
    C = HG_CHUNK
    nl = int(math.log2(C))
    n_ops = 1 + 2 * nl
    return [pl.BlockSpec((C, 4 * D), lambda c: (order(c), 0)),
            pl.BlockSpec((depth, D), lambda c: (0, 0)),
            pl.BlockSpec((1, HG_EXPAND), lambda c: (0, 0)),
            pl.BlockSpec((n_ops * C, 3 * C), lambda c: (0, 0)),
            pl.BlockSpec((C, 3 * n_ops * C), lambda c: (0, 0)),
            pl.BlockSpec((nl, C, C), lambda c: (0, 0, 0))]


def hgrn_fwd(proj, logits, ng, layer, *, name, comm=None):
    S = proj.shape[0]
    D = proj.shape[1] // 4
    H = D // HG_EXPAND
    C = HG_CHUNK
    nc = S // C
    consts = _hgrn_consts()

    def body(p_ref, lg_ref, ng_ref, ops3_ref, opst3_ref, mk_ref, o_ref, st_out_ref, st_ref):
        @pl.when(pl.program_id(0) == 0)
        def _():
            st_ref[...] = jnp.zeros_like(st_ref)

        st_out_ref[...] = st_ref[...]
        out, new_sts = _hgrn_chunk(*(p_ref[:, j * D:(j + 1) * D] for j in range(4)), tuple(st_ref[h] for h in range(H)),
                                   lg_ref[...], ng_ref[...], ops3_ref[...], opst3_ref[...], mk_ref[...], layer)
        o_ref[...] = out.astype(BF16)
        for h in range(H):
            st_ref[h] = new_sts[h]

    outs, updated, created = _carried_call(
        body, [proj, logits, ng, *consts], name=name, grid=(nc,),
        in_specs=_hgrn_in_specs(D, logits.shape[0], lambda c: c),
        out_specs=[pl.BlockSpec((C, D), lambda c: (c, 0)),
                   pl.BlockSpec((None, H, HG_EXPAND, HG_EXPAND), lambda c: (c, 0, 0, 0))],
        out_shape=[jax.ShapeDtypeStruct((S, D), BF16),
                   jax.ShapeDtypeStruct((nc, H, HG_EXPAND, HG_EXPAND), F32)],
        scratch_shapes=[pltpu.VMEM((H, HG_EXPAND, HG_EXPAND), F32)], comm=comm)
    return outs if comm is None else (*outs, updated, created)


def hgrn_bwd(proj, logits, ng, states, dout, layer, *, name, comm=None):
    S = proj.shape[0]
    D = proj.shape[1] // 4
    H = D // HG_EXPAND
    C = HG_CHUNK
    nc = S // C
    depth = logits.shape[0]
    consts = _hgrn_consts()
    rev = lambda c: nc - 1 - c

    def body(p_ref, lg_ref, ng_ref, ops3_ref, opst3_ref, mk_ref, st_in_ref, do_ref,
             dp_ref, dlg_ref, dng_ref, dst_ref):
        @pl.when(pl.program_id(0) == 0)
        def _():
            dst_ref[...] = jnp.zeros_like(dst_ref)
            dlg_ref[...] = jnp.zeros_like(dlg_ref)
            dng_ref[...] = jnp.zeros_like(dng_ref)

        fn = functools.partial(_hgrn_chunk, ops3=ops3_ref[...], opst3=opst3_ref[...], mk=mk_ref[...], layer=layer)
        _, vjp = jax.vjp(fn, *(p_ref[:, j * D:(j + 1) * D] for j in range(4)), tuple(st_in_ref[h] for h in range(H)),
                         lg_ref[...], ng_ref[...])
        *dproj, dsts, dlg, dng = vjp((do_ref[...], tuple(dst_ref[h] for h in range(H))))
        for j in range(4):
            dp_ref[:, j * D:(j + 1) * D] = dproj[j].astype(BF16)
        for h in range(H):
            dst_ref[h] = dsts[h]
        dlg_ref[...] += dlg
        dng_ref[...] += dng

    in_specs = _hgrn_in_specs(D, depth, rev)
    in_specs += [pl.BlockSpec((None, H, HG_EXPAND, HG_EXPAND), lambda c: (rev(c), 0, 0, 0)),
                 pl.BlockSpec((C, D), lambda c: (rev(c), 0))]
    outs, _, created = _carried_call(
        body, [proj, logits, ng, *consts, states, dout], name=name, grid=(nc,), in_specs=in_specs,
        out_specs=[pl.BlockSpec((C, 4 * D), lambda c: (rev(c), 0)),
                   pl.BlockSpec((depth, D), lambda c: (0, 0)), pl.BlockSpec((1, HG_EXPAND), lambda c: (0, 0))],
        out_shape=[jax.ShapeDtypeStruct((S, 4 * D), BF16), jax.ShapeDtypeStruct((depth, D), F32),
                   jax.ShapeDtypeStruct((1, HG_EXPAND), F32)],
        scratch_shapes=[pltpu.VMEM((H, HG_EXPAND, HG_EXPAND), F32)], comm=comm)
    return outs if comm is None else (*outs, created)


def _shift_down(u, prev, shift):
    rolled = pltpu.roll(u, shift, axis=0)
    rows = lax.broadcasted_iota(jnp.int32, prev.shape, 0)
    top = jnp.where(rows < shift, pltpu.roll(prev, shift, axis=0), rolled[:SUBLANES])
    return jnp.concatenate([top, rolled[SUBLANES:]], axis=0)


def _shift_up(x, nxt, shift):
    tm = x.shape[0]
    rolled = pltpu.roll(x, tm - shift, axis=0)
    rows = lax.broadcasted_iota(jnp.int32, nxt.shape, 0)
    bottom = jnp.where(rows >= SUBLANES - shift, pltpu.roll(nxt, SUBLANES - shift, axis=0), rolled[tm - SUBLANES:])
    return jnp.concatenate([rolled[:tm - SUBLANES], bottom], axis=0)


def _conv(u, prev, w_ref, b_ref, half):
    u1 = _shift_down(u, prev, 1)
    u2 = _shift_down(u, prev, 2)
    return b_ref[half] + w_ref[half, 0:1, :] * u2 + w_ref[half, 1:2, :] * u1 + w_ref[half, 2:3, :] * u


def _halo_rows(ref, keep, last):
    full = ref.astype(F32)
    return (full[HALO_ROWS - SUBLANES:] if last else full[:SUBLANES]) * keep


def _conv_specs(tm, tc, l):
    nprev = lambda m: jnp.maximum(m * (tm // HALO_ROWS) - 1, 0)
    tile = pl.BlockSpec((2, tm, tc), lambda j, m: (0, m, j))
    halo = pl.BlockSpec((2, HALO_ROWS, tc), lambda j, m: (0, nprev(m), j))
    wspec = pl.BlockSpec((None, 2, 3, tc), lambda j, m: (l, 0, 0, j))
    bspec = pl.BlockSpec((None, 2, 1, tc), lambda j, m: (l, 0, 0, j))
    return tile, halo, wspec, bspec


def convact_fwd(u, cw, cb, l, *, name, comm=None):
    _, S, Fh = u.shape
    tm = _pick(S, (256, 128))
    tc = _pick(Fh, (1408, 512, 256, 128))
    tile, halo, wspec, bspec = _conv_specs(tm, tc, l)

    def body(u_ref, p_ref, w_ref, b_ref, o_ref, ot_ref, c_ref):
        keep = (pl.program_id(1) > 0).astype(F32)
        cg = _conv(u_ref[0].astype(F32), _halo_rows(p_ref[0], keep, True), w_ref, b_ref, 0)
        cv = _conv(u_ref[1].astype(F32), _halo_rows(p_ref[1], keep, True), w_ref, b_ref, 1)
        act = _silu(cg) * cv
        o_ref[...] = act.astype(BF16)
        ot_ref[...] = act.T.astype(BF16)
        c_ref[0] = cg.astype(BF16)
        c_ref[1] = cv.astype(BF16)

    outs, updated, _ = _carried_call(
        body, [u, u, cw, cb], name=name, grid=(Fh // tc, S // tm), in_specs=[tile, halo, wspec, bspec],
        out_specs=[pl.BlockSpec((tm, tc), lambda j, m: (m, j)), pl.BlockSpec((tc, tm), lambda j, m: (j, m)), tile],
        out_shape=[jax.ShapeDtypeStruct((S, Fh), BF16), jax.ShapeDtypeStruct((Fh, S), BF16),
                   jax.ShapeDtypeStruct(u.shape, BF16)], comm=comm)
    return tuple(outs) if comm is None else (*outs, updated)


def convact_bwd(c, u, cw, l, da, *, name):
    _, S, Fh = u.shape
    tm = _pick(S, (256, 128))
    tc = _pick(Fh, (1408, 512, 256, 128))
    nm = S // tm

    def body(c_ref, u_ref, w_ref, da_ref, du_ref, dw_ref, db_ref, carry):
        @pl.when(pl.program_id(1) == 0)
        def _():
            carry[...] = jnp.zeros_like(carry)
            dw_ref[...] = jnp.zeros_like(dw_ref)
            db_ref[...] = jnp.zeros_like(db_ref)

        cg, cv = c_ref[0].astype(F32), c_ref[1].astype(F32)
        da_t = da_ref[...]
        sg = _sigmoid(cg)
        dcv = da_t * (cg * sg)
        dcg = da_t * cv * (sg * (1.0 + cg * (1.0 - sg)))
        rs = lambda t: jnp.sum(t, axis=0, keepdims=True)
        for half, dc in ((0, dcg), (1, dcv)):
            nxt = carry[half]
            d1, d2 = _shift_up(dc, nxt, 1), _shift_up(dc, nxt, 2)
            du = w_ref[half, 2:3, :] * dc + w_ref[half, 1:2, :] * d1 + w_ref[half, 0:1, :] * d2
            du_ref[half] = du.astype(BF16)
            uh = u_ref[half].astype(F32)
            for j, d in enumerate((d2, d1, dc)):
                dw_ref[half, j:j + 1, :] += rs(d * uh)
            db_ref[half] += rs(dc)
            carry[half] = dc[:SUBLANES]

    tile = pl.BlockSpec((2, tm, tc), lambda j, m: (0, nm - 1 - m, j))
    return pl.pallas_call(
        body, name=name, grid=(Fh // tc, nm),
        in_specs=[tile, tile, pl.BlockSpec((None, 2, 3, tc), lambda j, m: (l, 0, 0, j)),
                  pl.BlockSpec((tm, tc), lambda j, m: (nm - 1 - m, j))],
        out_specs=[tile, pl.BlockSpec((2, 3, tc), lambda j, m: (0, 0, j)),
                   pl.BlockSpec((2, 1, tc), lambda j, m: (0, 0, j))],
        out_shape=[jax.ShapeDtypeStruct(u.shape, BF16), jax.ShapeDtypeStruct((2, 3, Fh), F32),
                   jax.ShapeDtypeStruct((2, 1, Fh), F32)],
        scratch_shapes=[pltpu.VMEM((2, SUBLANES, tc), F32)],
        compiler_params=_params(("parallel", "arbitrary")),
    )(c, u, cw, da)


def _adamw_math(w, g, m, v):
    m = ADAM_B1 * m + (1.0 - ADAM_B1) * g
    v = ADAM_B2 * v + (1.0 - ADAM_B2) * (g * g)
    m_hat = m / (1.0 - ADAM_B1 ** ADAM_STEP)
    v_hat = v / (1.0 - ADAM_B2 ** ADAM_STEP)
    delta = -ADAM_LR * (m_hat / (jnp.sqrt(v_hat) + ADAM_EPS) + ADAM_WD * w)
    return delta, m, v


def _row_tile(R, Ccols, n_streams):
    for tr in (512, 256, 128, 64, 32, 16, 8):
        if R % tr == 0 and tr * Ccols * 4 * n_streams * 2 <= VMEM_LIMIT_BYTES // 2:
            return tr
    return R


def adamw(w, g, m, v, *, name):
    R, Ccols = w.shape
    tr = _row_tile(R, Ccols, 7)

    def body(w_ref, g_ref, m_ref, v_ref, d_ref, mo_ref, vo_ref):
        d, mn, vn = _adamw_math(w_ref[...], g_ref[...], m_ref[...], v_ref[...])
        d_ref[...] = d
        mo_ref[...] = mn
        vo_ref[...] = vn

    spec = pl.BlockSpec((tr, Ccols), lambda i: (i, 0))
    shp = jax.ShapeDtypeStruct((R, Ccols), F32)
    return pl.pallas_call(
        body, name=name, grid=(R // tr,), in_specs=[spec] * 4, out_specs=[spec] * 3,
        out_shape=[shp, shp, shp], compiler_params=_params(("parallel",)),
    )(w, g, m, v)


def _prefetch_call(body, pos, args, *, name, grid, in_specs, out_specs, out_shape, sem, aliases=None):
    return pl.pallas_call(
        lambda pos_ref, *refs: body(*refs), name=name,
        grid_spec=pltpu.PrefetchScalarGridSpec(num_scalar_prefetch=1, grid=grid, in_specs=in_specs,
                                               out_specs=out_specs),
        out_shape=out_shape, input_output_aliases={1 + i: o for i, o in (aliases or {}).items()},
        compiler_params=_params(sem),
    )(pos, *args)


def cast_place(w, pos, *, row_sharded, name):
    L, R, Ccols = w.shape

    def body(w_ref, o_ref):
        o_ref[...] = w_ref[...].astype(BF16)

    if row_sharded:
        grid = (L,)
        in_spec = pl.BlockSpec((None, R, Ccols), lambda l, pos: (l, 0, 0))
        out_spec = pl.BlockSpec((None, R, Ccols), lambda l, pos: (l, pos[0], 0))
        out_shape = jax.ShapeDtypeStruct((L, N_CHIPS * R, Ccols), BF16)
        sem = ("parallel",)
    else:
        tr = _pick(R, (512, 256, 128))
        grid = (L, R // tr)
        in_spec = pl.BlockSpec((None, tr, Ccols), lambda l, i, pos: (l, i, 0))
        out_spec = pl.BlockSpec((None, tr, Ccols), lambda l, i, pos: (l, i, pos[0]))
        out_shape = jax.ShapeDtypeStruct((L, R, N_CHIPS * Ccols), BF16)
        sem = ("parallel", "parallel")
    return _prefetch_call(body, pos, [w], name=name, grid=grid, in_specs=[in_spec], out_specs=out_spec,
                          out_shape=out_shape, sem=sem)


def pair_add(g, theirs, pos, *, row_sharded, name):
    def body(g_ref, t_ref, o_ref, ob_ref):
        s = g_ref[...] + t_ref[...]
        o_ref[...] = s
        ob_ref[...] = s.astype(BF16)

    if row_sharded:
        L, _, _, r2, D = g.shape
        grid = (L, N_CHIPS)
        g_spec = pl.BlockSpec((None, None, None, r2, D), lambda l, q, pos: (l, q, pos[1], 0, 0))
        t_spec = pl.BlockSpec((None, None, r2, D), lambda l, q, pos: (l, q, 0, 0))
    else:
        L, K, N = g.shape
        n = N // N_CHIPS
        tr = _pick(K // 2, (256, 128))
        nb = (K // 2) // tr
        grid = (L, nb, N_CHIPS)
        g_spec = pl.BlockSpec((None, tr, n), lambda l, i, j, pos: (l, pos[1] * nb + i, j))
        t_spec = pl.BlockSpec((None, tr, n), lambda l, i, j, pos: (l, i, j))
    return _prefetch_call(body, pos, [g, theirs], name=name, grid=grid, in_specs=[g_spec, t_spec],
                          out_specs=[t_spec, t_spec],
                          out_shape=[jax.ShapeDtypeStruct(theirs.shape, F32), jax.ShapeDtypeStruct(theirs.shape, BF16)],
                          sem=("parallel",) * len(grid))


def chip_add(pair, landed, pos, buf, li, n_layers, *, row_sharded, name):
    def body(p_ref, a_ref, b_ref, c_ref, *rest):
        rest[-1][...] = ((p_ref[...] + a_ref[...].astype(F32)) + b_ref[...].astype(F32)) + c_ref[...].astype(F32)

    if row_sharded:
        _, _, r2, D = pair.shape
        grid = (1,)
        p_spec = pl.BlockSpec((None, None, r2, D), lambda i, pos: (0, pos[0], 0, 0))
        l_specs = [pl.BlockSpec((None, r2, D), functools.partial(lambda i, pos, k: (k, 0, 0), k=k)) for k in range(3)]
        out_spec = pl.BlockSpec((None, None, r2, D), lambda i, pos: (li, pos[1], 0, 0))
        out_shape = jax.ShapeDtypeStruct((n_layers, 2, r2, D), F32)
    else:
        _, K2, N = pair.shape
        n = N // N_CHIPS
        tr = _pick(K2, (256, 128))
        nb = K2 // tr
        grid = (nb,)
        p_spec = pl.BlockSpec((None, tr, n), lambda i, pos: (0, i, pos[0]))
        l_specs = [pl.BlockSpec((None, tr, n), functools.partial(lambda i, pos, k: (k, i, 0), k=k)) for k in range(3)]
        out_spec = pl.BlockSpec((None, tr, n), lambda i, pos: (li, pos[1] * nb + i, 0))
        out_shape = jax.ShapeDtypeStruct((n_layers, 2 * K2, n), F32)
    args, in_specs, aliases = [pair, landed, landed, landed], [p_spec] + l_specs, None
    if buf is not None:
        args.append(buf)
        in_specs.append(pl.BlockSpec(memory_space=pl.ANY))
        aliases = {4: 0}
    return _prefetch_call(body, pos, args, name=name, grid=grid, in_specs=in_specs, out_specs=out_spec,
                          out_shape=out_shape, sem=("parallel",), aliases=aliases)


def _position():
    return lax.axis_index("x"), lax.axis_index("y"), lax.axis_index("c")


def _other_chips(x, y):
    return [(1 - x, y), (x, 1 - y), (1 - x, 1 - y)]


HBM_ANY = pl.BlockSpec(memory_space=pl.ANY)
COMM_PARAMS = pltpu.CompilerParams(has_side_effects=True)


class Comm:
    def __init__(self, ro=(), inout=(), new=(), sems=(), phases=()):
        self.ro, self.inout, self.new, self.sems, self.phases = list(ro), list(inout), list(new), list(sems), list(phases)


def _carried_call(body, args, *, name, grid, in_specs, out_specs, out_shape, scratch_shapes=(), comm=None):
    comm = comm or Comm()
    n_in, n_out, n_sc = len(in_specs), len(out_shape), len(scratch_shapes)
    n_ro, n_io, n_new, n_sem = len(comm.ro), len(comm.inout), len(comm.new), len(comm.sems)
    last = int(np.prod(grid)) - 1

    def wrapped(*refs):
        bounds = np.cumsum([0, n_in, n_ro, n_io, n_out, n_io, n_new, n_sc, n_sem])
        core_in, ro, _, core_out, io, new, core_sc, sems = (refs[a:b] for a, b in zip(bounds[:-1], bounds[1:]))
        step = pl.program_id(0)
        for d in range(1, len(grid)):
            step = step * grid[d] + pl.program_id(d)
        for frac, fn in comm.phases[:-1]:
            pl.when(step == int(frac * last))(functools.partial(fn, ro, io, new, sems))
        body(*core_in, *core_out, *core_sc)
        if comm.phases:
            pl.when(step == last)(functools.partial(comm.phases[-1][1], ro, io, new, sems))

    outs = pl.pallas_call(
        wrapped, name=name, grid=grid,
        in_specs=list(in_specs) + [HBM_ANY] * (n_ro + n_io),
        out_specs=list(out_specs) + [HBM_ANY] * (n_io + n_new),
        out_shape=list(out_shape) + [jax.ShapeDtypeStruct(a.shape, a.dtype) for a in comm.inout] + comm.new,
        input_output_aliases={n_in + n_ro + j: n_out + j for j in range(n_io)},
        scratch_shapes=list(scratch_shapes) + comm.sems,
        compiler_params=pltpu.CompilerParams(dimension_semantics=("arbitrary",) * len(grid),
                                             vmem_limit_bytes=VMEM_LIMIT_BYTES, has_side_effects=bool(comm.phases)),
    )(*args, *comm.ro, *comm.inout)
    return outs[:n_out], outs[n_out:n_out + n_io], outs[n_out + n_io:]


def run_comm(comm, *, name):
    n_ro, n_io, n_new = len(comm.ro), len(comm.inout), len(comm.new)

    def body(*refs):
        bounds = np.cumsum([0, n_ro, n_io, n_io, n_new, len(comm.sems)])
        ro, _, io, new, sems = (refs[a:b] for a, b in zip(bounds[:-1], bounds[1:]))
        for _, fn in comm.phases:
            fn(ro, io, new, sems)

    outs = pl.pallas_call(
        body, name=name, in_specs=[HBM_ANY] * (n_ro + n_io), out_specs=[HBM_ANY] * (n_io + n_new),
        out_shape=[jax.ShapeDtypeStruct(a.shape, a.dtype) for a in comm.inout] + comm.new,
        input_output_aliases={n_ro + j: j for j in range(n_io)},
        scratch_shapes=comm.sems, compiler_params=COMM_PARAMS,
    )(*comm.ro, *comm.inout)
    return outs[:n_io], outs[n_io:]


def _rdma(src, dst, send_sem, recv_sem, device):
    return pltpu.make_async_remote_copy(src_ref=src, dst_ref=dst, send_sem=send_sem, recv_sem=recv_sem,
                                        device_id=device, device_id_type=MESH)


def gather_comm(bufs, slices):
    n = len(slices)

    def region(io, s, h, chip):
        b, li, row_sharded = slices[s]
        ref = io[b]
        if row_sharded:
            r = ref.shape[1] // N_CHIPS
            return ref.at[li, pl.ds(chip * r + h * (r // 2), r // 2), :]
        K, nn = ref.shape[1], ref.shape[2] // N_CHIPS
        return ref.at[li, pl.ds(h * (K // 2), K // 2), pl.ds(chip * nn, nn)]

    def copies(io, sems):
        x, y, c = _position()
        me = 2 * x + y
        chips = _other_chips(x, y)
        ids = [2 * cx + cy for cx, cy in chips]

        def ici(s, k, src):
            return _rdma(region(io, s, c, me), region(io, s, c, src), sems[0].at[s, k], sems[1].at[s, k], (*chips[k], c))

        def d2d(s, k, h):
            return _rdma(region(io, s, h, ids[k]), region(io, s, h, ids[k]), sems[2].at[s, k], sems[3].at[s, k],
                         (x, y, 1 - c))

        return me, ids, c, ici, d2d

    pairs = [(s, k) for s in range(n) for k in range(3)]

    def start(ro, io, new, sems):
        me, _, _, ici, _ = copies(io, sems)
        for s, k in pairs:
            ici(s, k, me).start()

    def pass_on(ro, io, new, sems):
        _, ids, c, ici, d2d = copies(io, sems)
        for s, k in pairs:
            ici(s, k, ids[k]).wait_recv()
            d2d(s, k, c).start()

    def finish(ro, io, new, sems):
        me, _, c, ici, d2d = copies(io, sems)
        for s, k in pairs:
            d2d(s, k, 1 - c).wait_recv()
        for s, k in pairs:
            ici(s, k, me).wait_send()
            d2d(s, k, c).wait_send()

    return Comm(inout=bufs, sems=[pltpu.SemaphoreType.DMA((n, 3))] * 4,
                phases=[(0.0, start), (0.85, pass_on), (1.0, finish)])


def gather_small(shard, *, name):
    n = shard.shape[2]

    def body(s_ref, o_ref, send_sem, recv_sem, local_sem):
        x, y, c = _position()
        me = 2 * x + y
        chips = _other_chips(x, y)
        place = lambda chip: o_ref.at[:, :, pl.ds(chip * n, n)]
        local = pltpu.make_async_copy(s_ref, place(me), local_sem)
        local.start()
        for k in range(3):
            _rdma(s_ref, place(me), send_sem.at[k], recv_sem.at[k], (*chips[k], c)).start()
        for k in range(3):
            _rdma(s_ref, place(2 * chips[k][0] + chips[k][1]), send_sem.at[k], recv_sem.at[k], (*chips[k], c)).wait()
        local.wait()

    return pl.pallas_call(
        body, name=name, in_specs=[HBM_ANY], out_specs=HBM_ANY,
        out_shape=jax.ShapeDtypeStruct((shard.shape[0], shard.shape[1], N_CHIPS * n), shard.dtype),
        scratch_shapes=[pltpu.SemaphoreType.DMA((3,)), pltpu.SemaphoreType.DMA((3,)), pltpu.SemaphoreType.DMA],
        compiler_params=COMM_PARAMS,
    )(shard)


def swap_comm(grads):
    def copies(ro, new, sems):
        x, y, c = _position()
        h = 1 - c
        cps = []
        for s, (a, row_sharded) in enumerate(grads):
            if row_sharded:
                r = a.shape[1] // N_CHIPS
                for q in range(N_CHIPS):
                    cps.append(_rdma(ro[s].at[:, pl.ds(q * r + h * (r // 2), r // 2), :], new[s].at[:, q],
                                     sems[0].at[s, q], sems[1].at[s, q], (x, y, 1 - c)))
            else:
                K = a.shape[1]
                cps.append(_rdma(ro[s].at[:, pl.ds(h * (K // 2), K // 2), :], new[s],
                                 sems[0].at[s, 0], sems[1].at[s, 0], (x, y, 1 - c)))
        return cps

    def start(ro, io, new, sems):
        for cp in copies(ro, new, sems):
            cp.start()

    def finish(ro, io, new, sems):
        for cp in copies(ro, new, sems):
            cp.wait()

    theirs = []
    for a, row_sharded in grads:
        if row_sharded:
            theirs.append(jax.ShapeDtypeStruct((1, N_CHIPS, a.shape[1] // N_CHIPS // 2, a.shape[2]), a.dtype))
        else:
            theirs.append(jax.ShapeDtypeStruct((1, a.shape[1] // 2, a.shape[2]), a.dtype))
    return Comm(ro=[a for a, _ in grads], new=theirs, sems=[pltpu.SemaphoreType.DMA((len(grads), N_CHIPS))] * 2,
                phases=[(0.0, start), (1.0, finish)])


def merge_comms(comms):
    spans, ro, io, new, sems = [], [], [], [], []
    for cm in comms:
        spans.append((len(ro), len(io), len(new), len(sems)))
        ro, io, new, sems = ro + cm.ro, io + cm.inout, new + cm.new, sems + cm.sems

    def phase(frac):
        def run(R, I, N, S):
            for cm, (a, b, d, e) in zip(comms, spans):
                for f, fn in cm.phases:
                    if f == frac:
                        fn(R[a:a + len(cm.ro)], I[b:b + len(cm.inout)], N[d:d + len(cm.new)], S[e:e + len(cm.sems)])
        return run

    fracs = sorted({f for cm in comms for f, _ in cm.phases})
    merged = Comm(ro=ro, inout=io, new=new, sems=sems, phases=[(f, phase(f)) for f in fracs])
    cut = lambda created: [created[d:d + len(cm.new)] for cm, (_, _, d, _) in zip(comms, spans)]
    return merged, cut


def scatter_comm(halves):
    def copies(ro, new, sems):
        x, y, c = _position()
        chips = _other_chips(x, y)
        cps = []
        for s, (a, row_sharded) in enumerate(halves):
            for k in range(3):
                to = 2 * chips[k][0] + chips[k][1]
                if row_sharded:
                    src = ro[s].at[0, to]
                else:
                    n = a.shape[2] // N_CHIPS
                    src = ro[s].at[0, :, pl.ds(to * n, n)]
                cps.append(_rdma(src, new[s].at[k], sems[0].at[s, k], sems[1].at[s, k], (*chips[k], c)))
        return cps

    def start(ro, io, new, sems):
        for cp in copies(ro, new, sems):
            cp.start()

    def finish(ro, io, new, sems):
        for cp in copies(ro, new, sems):
            cp.wait()

    landing = [jax.ShapeDtypeStruct((3, a.shape[2], a.shape[3]) if row_sharded else
                                    (3, a.shape[1], a.shape[2] // N_CHIPS), a.dtype) for a, row_sharded in halves]
    return Comm(ro=[a for a, _ in halves], new=landing, sems=[pltpu.SemaphoreType.DMA((len(halves), 3))] * 2,
                phases=[(0.0, start), (1.0, finish)])


def sibling_join_halves(shards, *, name):
    n_in = len(shards)

    def body(*refs):
        ins = refs[:n_in]
        outs = refs[n_in:2 * n_in]
        send_sem, recv_sem = refs[2 * n_in:]
        x, y, c = _position()
        sibling = (x, y, 1 - c)

        def rows(ref, h):
            R2 = ref.shape[1] // 2
            return ref.at[:, pl.ds(h * R2, R2), :]

        sends = [pltpu.make_async_remote_copy(src_ref=rows(ins[i], c), dst_ref=rows(outs[i], c),
                                              send_sem=send_sem.at[i], recv_sem=recv_sem.at[i],
                                              device_id=sibling, device_id_type=MESH) for i in range(n_in)]
        lands = [pltpu.make_async_remote_copy(src_ref=rows(ins[i], c), dst_ref=rows(outs[i], 1 - c),
                                              send_sem=send_sem.at[i], recv_sem=recv_sem.at[i],
                                              device_id=sibling, device_id_type=MESH) for i in range(n_in)]
        for cp in sends:
            cp.start()
        for i in range(n_in):
            sends[i].wait_send()
            lands[i].wait_recv()

    return pl.pallas_call(
        body, name=name, in_specs=[HBM_ANY] * n_in, out_specs=[HBM_ANY] * n_in,
        out_shape=[jax.ShapeDtypeStruct(a.shape, a.dtype) for a in shards],
        input_output_aliases={i: i for i in range(n_in)},
        scratch_shapes=[pltpu.SemaphoreType.DMA((n_in,)), pltpu.SemaphoreType.DMA((n_in,))],
        compiler_params=COMM_PARAMS,
    )(*shards)


def all_reduce_small(v, *, name):
    R, Ccols = v.shape

    def body(v_ref, o_ref, slots, send_sem, recv_sem):
        x, y, c = _position()
        me = 4 * x + 2 * y + c
        slots[me] = v_ref[...]

        def peer(d):
            return x ^ (d >> 2), y ^ ((d >> 1) & 1), c ^ (d & 1)

        def copy(d, slot):
            return pltpu.make_async_remote_copy(
                src_ref=v_ref, dst_ref=slots.at[slot], send_sem=send_sem.at[d], recv_sem=recv_sem.at[d],
                device_id=peer(d), device_id_type=MESH)

        sends = [copy(d, me) for d in range(1, N_DEV)]
        for cp in sends:
            cp.start()
        for d in range(1, N_DEV):
            px, py, pc = peer(d)
            copy(d, 4 * px + 2 * py + pc).wait_recv()
        for cp in sends:
            cp.wait_send()
        acc = slots[0]
        for d in range(1, N_DEV):
            acc = acc + slots[d]
        o_ref[...] = acc

    return pl.pallas_call(
        body, name=name,
        in_specs=[pl.BlockSpec(memory_space=pltpu.VMEM)], out_specs=pl.BlockSpec(memory_space=pltpu.VMEM),
        out_shape=jax.ShapeDtypeStruct((R, Ccols), F32),
        scratch_shapes=[pltpu.VMEM((N_DEV, R, Ccols), F32), pltpu.SemaphoreType.DMA((N_DEV,)),
                        pltpu.SemaphoreType.DMA((N_DEV,))],
        compiler_params=COMM_PARAMS,
    )(v)


def _pack_small(parts):
    flat = jnp.concatenate([p.reshape(-1).astype(F32) for p in parts])
    n = flat.shape[0]
    rows = -(-n // LANES)
    rows = -(-rows // SUBLANES) * SUBLANES
    return jnp.pad(flat, (0, rows * LANES - n)).reshape(rows, LANES)


def _unpack_small(packed, like):
    flat = packed.reshape(-1)
    out, off = [], 0
    for p in like:
        out.append(flat[off:off + p.size].reshape(p.shape))
        off += p.size
    return out


def kernel(x, norm_mix, norm_ffn, norm_final, attn_w_in, attn_w_out, attn_sinks, hgrn_w_in, hgrn_w_out, hgrn_norm, hgrn_lb_logits, ffn_w_up, ffn_conv_w, ffn_conv_b, ffn_w_down, loss_target, m_norm_mix, m_norm_ffn, m_norm_final, m_attn_w_in, m_attn_w_out, m_attn_sinks, m_hgrn_w_in, m_hgrn_w_out, m_hgrn_norm, m_hgrn_lb_logits, m_ffn_w_up, m_ffn_conv_w, m_ffn_conv_b, m_ffn_w_down, v_norm_mix, v_norm_ffn, v_norm_final, v_attn_w_in, v_attn_w_out, v_attn_sinks, v_hgrn_w_in, v_hgrn_w_out, v_hgrn_norm, v_hgrn_lb_logits, v_ffn_w_up, v_ffn_conv_w, v_ffn_conv_b, v_ffn_w_down):
    S, D = x.shape[1], x.shape[2]
    depth = norm_mix.shape[0]
    n_q = D // HEAD_DIM
    n_kv = n_q // Q_PER_KV
    Fh = ffn_w_down.shape[1] * N_CHIPS
    h = x.reshape(S, D)
    target = loss_target.reshape(S, D)

    col_names = ["attn_w_in", "hgrn_w_in", "ffn_w_up"]
    row_names = ["attn_w_out", "hgrn_w_out", "ffn_w_down"]
    x_, y_, c_ = _position()
    me = 2 * x_ + y_
    pos = jnp.stack([me, c_]).astype(jnp.int32)
    col_w = [cast_place(w, pos, row_sharded=False, name=f"cast_place_{n}")
             for n, w in zip(col_names, (attn_w_in, hgrn_w_in, ffn_w_up))]
    row_w = [cast_place(w, pos, row_sharded=True, name=f"cast_place_{n}")
             for n, w in zip(row_names, (attn_w_out, hgrn_w_out, ffn_w_down))]
    W = dict(zip(col_names + row_names, col_w + row_w))
    is_row = {n: n in row_names for n in W}

    def layer_weights(layer):
        mixer = "attn" if layer % 2 == 0 else "hgrn"
        return [(f"{mixer}_w_in", layer // 2), (f"{mixer}_w_out", layer // 2), ("ffn_w_up", layer), ("ffn_w_down", layer)]

    def gather_of(group):
        names = [n for n, _ in group]
        return names, gather_comm([W[n] for n in names], [(i, li, is_row[n]) for i, (n, li) in enumerate(group)])

    def mixer_weights(layer):
        return layer_weights(layer)[:2] if layer < depth else []

    names_next, comm_next = gather_of(mixer_weights(0))
    filled, _ = run_comm(comm_next, name="gather_mixer0")
    W.update(zip(names_next, filled))
    cw = gather_small(ffn_conv_w, name="gather_conv_w").reshape(depth, 3, 2, Fh).transpose(0, 2, 1, 3)
    cb = ffn_conv_b.reshape(depth, 2, 1, Fh)
    g_mix = norm_mix.reshape(depth, 1, D)
    g_ffn = norm_ffn.reshape(depth, 1, D)

    slopes = jnp.exp2(-8.0 * jnp.arange(1, n_q + 1, dtype=F32) / n_q).reshape(n_kv, Q_PER_KV)
    slope_col = jnp.repeat(slopes, WINDOW, axis=1).reshape(n_kv, Q_PER_KV * WINDOW, 1)

    def to_heads(t, nh):
        return t.reshape(S, nh, HEAD_DIM).transpose(1, 0, 2)

    def from_heads(t):
        return t.transpose(1, 0, 2).reshape(S, -1)

    saved = []
    for layer in range(depth):
        idx = layer // 2
        rec = {"h_mix": h}
        hn, rec["hn_t"] = rmsnorm_fwd(h, g_mix, layer, name=f"norm_mix_fwd{layer}")
        is_attn = layer % 2 == 0
        up_l, down_l = layer_weights(layer)[2:]
        on_mixer = [up_l] if is_attn else [up_l, down_l]
        on_up = [down_l] if is_attn else mixer_weights(layer + 1)
        on_act = mixer_weights(layer + 1)[:1] if is_attn else []
        on_down = mixer_weights(layer + 1)[1:] if is_attn else []
        names_next, comm_next = gather_of(on_mixer)
        if is_attn:
            proj = mm_nn(hn, W["attn_w_in"], idx, name=f"attn_in_fwd{layer}", out_dtype=BF16)
            q = to_heads(proj[:, :n_q * HEAD_DIM], n_q).reshape(n_kv, Q_PER_KV, S, HEAD_DIM)
            k = to_heads(proj[:, n_q * HEAD_DIM:(n_q + n_kv) * HEAD_DIM], n_kv)
            v = to_heads(proj[:, (n_q + n_kv) * HEAD_DIM:], n_kv)
            sink_col = jnp.repeat(attn_sinks[idx].reshape(n_kv, Q_PER_KV), WINDOW, axis=1)
            sink_col = sink_col.reshape(n_kv, Q_PER_KV * WINDOW, 1)
            o_heads, filled, _ = attn_fwd(q, k, v, sink_col, slope_col, name=f"attn_fwd{layer}", comm=comm_next)
            W.update(zip(names_next, filled))
            o = from_heads(o_heads.reshape(n_q, S, HEAD_DIM))
            rec.update(q=q, k=k, v=v, sink_col=sink_col, o=o)
            h = mm_nn(o, W["attn_w_out"], idx, name=f"attn_out_fwd{layer}", res=h)
        else:
            proj = mm_nn(hn, W["hgrn_w_in"], idx, name=f"hgrn_in_fwd{layer}")
            ng = hgrn_norm[idx].reshape(1, HG_EXPAND)
            o, states, filled, _ = hgrn_fwd(proj, hgrn_lb_logits, ng, layer, name=f"hgrn_fwd{layer}", comm=comm_next)
            W.update(zip(names_next, filled))
            rec.update(proj=proj, ng=ng, states=states, o=o)
            h = mm_nn(o, W["hgrn_w_out"], idx, name=f"hgrn_out_fwd{layer}", res=h)
        rec["h_ffn"] = h
        hn2, rec["hn2_t"] = rmsnorm_fwd(h, g_ffn, layer, name=f"norm_ffn_fwd{layer}")
        if on_up:
            names_next, comm_next = gather_of(on_up)
            u, filled = mm_nn(hn2, W["ffn_w_up"], layer, name=f"ffn_up_fwd{layer}", split_out=True, out_dtype=BF16,
                              comm=comm_next)
            W.update(zip(names_next, filled))
        else:
            u = mm_nn(hn2, W["ffn_w_up"], layer, name=f"ffn_up_fwd{layer}", split_out=True, out_dtype=BF16)
        if on_act:
            names_next, comm_next = gather_of(on_act)
            act, rec["act_t"], rec["c"], filled = convact_fwd(u, cw, cb, layer, name=f"ffn_act_fwd{layer}",
                                                              comm=comm_next)
            W.update(zip(names_next, filled))
        else:
            act, rec["act_t"], rec["c"] = convact_fwd(u, cw, cb, layer, name=f"ffn_act_fwd{layer}")
        rec["u"] = u
        if on_down:
            names_next, comm_next = gather_of(on_down)
            h, filled = mm_nn(act, W["ffn_w_down"], layer, name=f"ffn_down_fwd{layer}", res=h, comm=comm_next)
            W.update(zip(names_next, filled))
        else:
            h = mm_nn(act, W["ffn_w_down"], layer, name=f"ffn_down_fwd{layer}", res=h)
        saved.append(rec)

    loss_part, dh, d_norm_final = loss_head(h, norm_final.reshape(1, D), target, name="loss_head")

    n_attn, n_hgrn = attn_w_in.shape[0], hgrn_w_in.shape[0]
    reduced_parts = []
    pending = []

    def swap_of(group, g):
        ordered = sorted(group, key=lambda nl: is_row[nl[0]])
        return ordered, swap_comm([(g[n], is_row[n]) for n, _ in ordered])

    def pair_up(tag, ordered, g, theirs):
        pair, pair_bf16 = {}, {}
        for (n, _), t in zip(ordered, theirs):
            gn = g[n]
            if is_row[n]:
                r = gn.shape[1] // N_CHIPS
                gn = gn.reshape(1, N_CHIPS, 2, r // 2, gn.shape[2])
            pair[n], pair_bf16[n] = pair_add(gn, t, pos, row_sharded=is_row[n], name=f"rs_pair_add_{n}{tag}")
        pending.append((ordered, pair, scatter_comm([(pair_bf16[n], is_row[n]) for n, _ in ordered])))

    def landed_behind(created, cut):
        for (ordered, pair, _), landed in zip(pending, cut(created)):
            reduced_parts.append((ordered, pair, landed))
        pending.clear()

    d_norm_mix, d_norm_ffn = [None] * depth, [None] * depth
    d_conv_w, d_conv_b = [None] * depth, [None] * depth
    d_sinks, d_hgrn_norm = [None] * n_attn, [None] * n_hgrn
    d_logits = jnp.zeros_like(hgrn_lb_logits)
    for layer in reversed(range(depth)):
        idx = layer // 2
        rec = saved[layer]
        g = {"ffn_w_down": mm_dw(rec["act_t"], dh, name=f"ffn_down_dw{layer}")}
        dact = mm_nt(dh, W["ffn_w_down"], layer, name=f"ffn_down_dx{layer}")
        du, dcw, dcb = convact_bwd(rec["c"], rec["u"], cw, layer, dact, name=f"ffn_act_bwd{layer}")
        d_conv_w[layer] = dcw.transpose(1, 0, 2).reshape(3, 2 * Fh)
        d_conv_b[layer] = dcb.reshape(2 * Fh)
        g["ffn_w_up"] = mm_dw(rec["hn2_t"], du, name=f"ffn_up_dw{layer}", split_b=True)
        ordered, swap = swap_of(layer_weights(layer)[2:], g)
        dh, dg, theirs = mm_nt(du, W["ffn_w_up"], layer, name=f"ffn_up_dx{layer}", split_in=True, comm=swap,
                               norm=(rec["h_ffn"], g_ffn, layer, dh))
        pair_up(layer, ordered, g, theirs)
        d_norm_ffn[layer] = dg.reshape(D)
        carried, cut = merge_comms([p[2] for p in pending])
        if layer % 2 == 0:
            g["attn_w_out"] = mm_tn(rec["o"], dh, name=f"attn_out_dw{layer}")
            do = mm_nt(dh, W["attn_w_out"], idx, name=f"attn_out_dx{layer}", out_dtype=BF16)
            do_heads = to_heads(do, n_q).reshape(n_kv, Q_PER_KV, S, HEAD_DIM)
            res = attn_bwd(rec["q"], rec["k"], rec["v"], rec["sink_col"], slope_col, do_heads,
                           name=f"attn_bwd{layer}", comm=carried)
            dq, dk, dv, dsink = res[:4]
            landed_behind(res[-1], cut)
            dproj = jnp.concatenate([from_heads(dq.reshape(n_q, S, HEAD_DIM)), from_heads(dk), from_heads(dv)], axis=1)
            d_sinks[idx] = jnp.sum(dsink.reshape(n_kv, Q_PER_KV, WINDOW), axis=-1).reshape(n_q)
            g["attn_w_in"] = mm_dw(rec["hn_t"], dproj, name=f"attn_in_dw{layer}")
            ordered, swap = swap_of(layer_weights(layer)[:2], g)
            dh, dg, theirs = mm_nt(dproj, W["attn_w_in"], idx, name=f"attn_in_dx{layer}", comm=swap,
                                   norm=(rec["h_mix"], g_mix, layer, dh))
        else:
            g["hgrn_w_out"] = mm_tn(rec["o"], dh, name=f"hgrn_out_dw{layer}")
            do = mm_nt(dh, W["hgrn_w_out"], idx, name=f"hgrn_out_dx{layer}")
            res = hgrn_bwd(rec["proj"], hgrn_lb_logits, rec["ng"], rec["states"], do, layer,
                           name=f"hgrn_bwd{layer}", comm=carried)
            dproj, dlg, dng = res[:3]
            landed_behind(res[-1], cut)
            d_logits = d_logits + dlg
            d_hgrn_norm[idx] = dng.reshape(HG_EXPAND)
            g["hgrn_w_in"] = mm_dw(rec["hn_t"], dproj, name=f"hgrn_in_dw{layer}")
            ordered, swap = swap_of(layer_weights(layer)[:2], g)
            dh, dg, theirs = mm_nt(dproj, W["hgrn_w_in"], idx, name=f"hgrn_in_dx{layer}", comm=swap,
                                   norm=(rec["h_mix"], g_mix, layer, dh))
        pair_up(layer, ordered, g, theirs)
        d_norm_mix[layer] = dg.reshape(D)
    grad_x = dh.reshape(x.shape)
    carried, cut = merge_comms([p[2] for p in pending])
    landed_behind(run_comm(carried, name="rs_scatter_last")[1], cut)

    small_w = [norm_mix, norm_ffn, norm_final, attn_sinks, hgrn_norm, hgrn_lb_logits, ffn_conv_b]
    small_m = [m_norm_mix, m_norm_ffn, m_norm_final, m_attn_sinks, m_hgrn_norm, m_hgrn_lb_logits, m_ffn_conv_b]
    small_v = [v_norm_mix, v_norm_ffn, v_norm_final, v_attn_sinks, v_hgrn_norm, v_hgrn_lb_logits, v_ffn_conv_b]
    small_grads = [jnp.stack(d_norm_mix), jnp.stack(d_norm_ffn), d_norm_final.reshape(D), jnp.stack(d_sinks),
                   jnp.stack(d_hgrn_norm), d_logits, jnp.stack(d_conv_b)]
    gconv = jnp.stack(d_conv_w)
    packed_small = _pack_small(small_grads + [loss_part])
    summed = all_reduce_small(jnp.concatenate([packed_small, _pack_small([gconv])], axis=0), name="all_reduce_small")
    reduced, gconv_rows = summed[:packed_small.shape[0]], summed[packed_small.shape[0]:]
    small_g = _unpack_small(reduced, small_w + [loss_part])
    loss = small_g.pop().reshape(())
    pad = [jnp.zeros((1, 1), F32)]
    sd, sm_, sv_ = adamw(_pack_small(small_w + pad), reduced, _pack_small(small_m + pad),
                         _pack_small(small_v + pad), name="adamw_small")
    small_d = _unpack_small(sd, small_w)
    small_nm = _unpack_small(sm_, small_w)
    small_nv = _unpack_small(sv_, small_w)

    big_names = col_names + row_names
    big_w = dict(attn_w_in=attn_w_in, hgrn_w_in=hgrn_w_in, ffn_w_up=ffn_w_up, attn_w_out=attn_w_out,
                 hgrn_w_out=hgrn_w_out, ffn_w_down=ffn_w_down)
    half_shard = dict.fromkeys(big_names)
    for ordered, pair, landed in reduced_parts:
        for (n, li), ld in zip(ordered, landed):
            half_shard[n] = chip_add(pair[n], ld, pos, half_shard[n], li, big_w[n].shape[0], row_sharded=is_row[n],
                                     name=f"rs_chip_add_{n}{li}")
    shards = sibling_join_halves([half_shard[n].reshape(big_w[n].shape) for n in big_names], name="rs_sibling_join")
    big_m = dict(attn_w_in=m_attn_w_in, hgrn_w_in=m_hgrn_w_in, ffn_w_up=m_ffn_w_up, attn_w_out=m_attn_w_out,
                 hgrn_w_out=m_hgrn_w_out, ffn_w_down=m_ffn_w_down)
    big_v = dict(attn_w_in=v_attn_w_in, hgrn_w_in=v_hgrn_w_in, ffn_w_up=v_ffn_w_up, attn_w_out=v_attn_w_out,
                 hgrn_w_out=v_hgrn_w_out, ffn_w_down=v_ffn_w_down)
    grads, deltas, new_m, new_v = {}, {}, {}, {}
    for name_, gshard in zip(big_names, shards):
        w_ = big_w[name_]
        cols = w_.shape[-1]
        d_, m_, v_ = adamw(w_.reshape(-1, cols), gshard.reshape(-1, cols), big_m[name_].reshape(-1, cols),
                           big_v[name_].reshape(-1, cols), name=f"adamw_{name_}")
        grads[name_] = gshard.reshape(w_.shape)
        deltas[name_], new_m[name_], new_v[name_] = (t.reshape(w_.shape) for t in (d_, m_, v_))

    n_conv = ffn_conv_w.shape[2]
    gconv_all = _unpack_small(gconv_rows, [gconv])[0]
    gconv_mine = lax.dynamic_slice_in_dim(gconv_all, me * n_conv, n_conv, axis=2)
    cshape = ffn_conv_w.shape
    d_, m_, v_ = adamw(ffn_conv_w.reshape(-1, n_conv), gconv_mine.reshape(-1, n_conv),
                       m_ffn_conv_w.reshape(-1, n_conv), v_ffn_conv_w.reshape(-1, n_conv), name="adamw_ffn_conv_w")
    grads["ffn_conv_w"] = gconv_mine
    deltas["ffn_conv_w"], new_m["ffn_conv_w"], new_v["ffn_conv_w"] = (t.reshape(cshape) for t in (d_, m_, v_))

    order = ["norm_mix", "norm_ffn", "norm_final", "attn_w_in", "attn_w_out", "attn_sinks", "hgrn_w_in",
             "hgrn_w_out", "hgrn_norm", "hgrn_lb_logits", "ffn_w_up", "ffn_conv_w", "ffn_conv_b", "ffn_w_down"]
    small_names = ["norm_mix", "norm_ffn", "norm_final", "attn_sinks", "hgrn_norm", "hgrn_lb_logits", "ffn_conv_b"]
    for i, name_ in enumerate(small_names):
        grads[name_], deltas[name_], new_m[name_], new_v[name_] = small_g[i], small_d[i], small_nm[i], small_nv[i]
    return (loss, grad_x, *[grads[n] for n in order], *[deltas[n] for n in order],
            *[new_m[n] for n in order], *[new_v[n] for n in order])
```

```python
import functools
import math

import numpy as np
import jax
import jax.numpy as jnp
from jax import lax
from jax.experimental import pallas as pl
from jax.experimental.pallas import tpu as pltpu

F32 = jnp.float32
BF16 = jnp.bfloat16
MESH = pl.DeviceIdType.MESH

HEAD_DIM = 64
Q_PER_KV = 4
WINDOW = 128
HG_EXPAND = 128
HG_CHUNK = 128
EPS = 1e-6
N_CHIPS = 4
N_DEV = 8

ADAM_LR = 0.001
ADAM_B1 = 0.9
ADAM_B2 = 0.999
ADAM_EPS = 1e-08
ADAM_WD = 0.01
ADAM_STEP = 10

VMEM_LIMIT_BYTES = 48 * 1024 * 1024
LANES = 128
SUBLANES = 8
HALO_ROWS = 16


def _params(sem=None):
    return pltpu.CompilerParams(dimension_semantics=sem, vmem_limit_bytes=VMEM_LIMIT_BYTES)


def _pick(n, cands):
    for c in cands:
        if n % c == 0:
            return c
    return n


def _dot(a, b, dims):
    return lax.dot_general(a.astype(BF16), b.astype(BF16), (dims, ((), ())),
                           preferred_element_type=F32)


NN = ((1,), (0,))
NT = ((1,), (1,))
TN = ((0,), (0,))


def mm_nn(a, w, l, *, name, res=None, out_dtype=F32, split_out=False, comm=None):
    M, K = a.shape
    N = w.shape[2]
    tk = K if K <= 2816 else _pick(K, (1024, 1408, 512, 256, 128))
    tm = _pick(M, (1024, 512, 256, 128) if tk <= 1408 else (512, 256, 128))
    wide = (2816, 2048, 1536) if tk <= 1024 and res is None else ()
    tn = _pick(N // 2 if split_out else N, wide + (1408, 1024, 768, 512, 256, 128))
    nk = K // tk
    nh = (N // 2) // tn

    def body(*refs):
        if res is None:
            a_ref, w_ref, o_ref = refs[:3]
            r_ref = None
        else:
            a_ref, w_ref, r_ref, o_ref = refs[:4]
        part = _dot(a_ref[...], w_ref[...], NN)

        def finish(acc):
            if r_ref is not None:
                acc = acc + r_ref[...]
            o_ref[...] = acc.astype(out_dtype)

        if nk == 1:
            finish(part)
        else:
            acc_ref = refs[-1]
            k = pl.program_id(2)

            @pl.when(k == 0)
            def _():
                acc_ref[...] = part

            @pl.when(k > 0)
            def _():
                acc_ref[...] += part

            @pl.when(k == nk - 1)
            def _():
                finish(acc_ref[...])

    in_specs = [pl.BlockSpec((tm, tk), lambda n, m, k: (m, k)),
                pl.BlockSpec((None, tk, tn), lambda n, m, k: (l, k, n))]
    args = [a, w]
    if res is not None:
        in_specs.append(pl.BlockSpec((tm, tn), lambda n, m, k: (m, n)))
        args.append(res)
    if split_out:
        out_shape = jax.ShapeDtypeStruct((2, M, N // 2), out_dtype)
        out_spec = pl.BlockSpec((None, tm, tn), lambda n, m, k: (n // nh, m, n % nh))
    else:
        out_shape = jax.ShapeDtypeStruct((M, N), out_dtype)
        out_spec = pl.BlockSpec((tm, tn), lambda n, m, k: (m, n))
    outs, updated, _ = _carried_call(
        body, args, name=name, grid=(N // tn, M // tm, nk), in_specs=in_specs, out_specs=[out_spec],
        out_shape=[out_shape], scratch_shapes=[] if nk == 1 else [pltpu.VMEM((tm, tn), F32)], comm=comm)
    return outs[0] if comm is None else (outs[0], updated)


def mm_nt(a, w, l, *, name, split_in=False, out_dtype=F32, comm=None, norm=None):
    if split_in:
        _, M, Nh = a.shape
        N = 2 * Nh
    else:
        M, N = a.shape
        Nh = N
    K = w.shape[1]
    tm = _pick(M, (512, 256, 128) if norm is not None else (1024, 512, 256, 128))
    tko = _pick(K, (1024, 1408, 512, 256, 128))
    tr = _pick(Nh, (2816, 2048, 1536, 1408, 1024, 768, 512, 256, 128) if tm <= 512 else (1024, 1408, 768, 512, 256, 128))
    nr = N // tr
    nh = Nh // tr
    n_in, n_out = (5, 2) if norm is not None else (2, 1)
    assert norm is None or tko == K

    def body(*refs):
        a_ref, w_ref = refs[:2]
        o_ref = refs[n_in]
        part = _dot(a_ref[...], w_ref[...], NT)

        def finish(acc):
            if norm is None:
                o_ref[...] = acc.astype(out_dtype)
                return
            h_ref, g_ref, dres_ref = refs[2:5]
            dg_ref = refs[n_in + 1]
            _, vjp = jax.vjp(_rms, h_ref[...], g_ref[...])
            dx, dg = vjp(acc)
            o_ref[...] = dres_ref[...] + dx
            first = pl.program_id(1) == 0

            @pl.when(first)
            def _():
                dg_ref[...] = dg

            @pl.when(jnp.logical_not(first))
            def _():
                dg_ref[...] += dg

        if nr == 1:
            finish(part)
        else:
            acc_ref = refs[n_in + n_out]
            r = pl.program_id(2)

            @pl.when(r == 0)
            def _():
                acc_ref[...] = part

            @pl.when(r > 0)
            def _():
                acc_ref[...] += part

            @pl.when(r == nr - 1)
            def _():
                finish(acc_ref[...])

    if split_in:
        a_spec = pl.BlockSpec((None, tm, tr), lambda ko, m, r: (r // nh, m, r % nh))
    else:
        a_spec = pl.BlockSpec((tm, tr), lambda ko, m, r: (m, r))
    args = [a, w]
    in_specs = [a_spec, pl.BlockSpec((None, tko, tr), lambda ko, m, r: (l, ko, r))]
    out_specs = [pl.BlockSpec((tm, tko), lambda ko, m, r: (m, ko))]
    out_shape = [jax.ShapeDtypeStruct((M, K), out_dtype)]
    if norm is not None:
        h, g, l_norm, dres = norm
        row = pl.BlockSpec((tm, K), lambda ko, m, r: (m, 0))
        args += [h, g, dres]
        in_specs += [row, pl.BlockSpec((None, 1, K), lambda ko, m, r: (l_norm, 0, 0)), row]
        out_specs.append(pl.BlockSpec((1, K), lambda ko, m, r: (0, 0)))
        out_shape.append(jax.ShapeDtypeStruct((1, K), F32))
    outs, _, created = _carried_call(
        body, args, name=name, grid=(K // tko, M // tm, nr), in_specs=in_specs, out_specs=out_specs,
        out_shape=out_shape, scratch_shapes=[] if nr == 1 else [pltpu.VMEM((tm, tko), F32)], comm=comm)
    outs = list(outs) + ([] if comm is None else [created])
    return outs[0] if len(outs) == 1 else tuple(outs)


def mm_tn(a, b, *, name):
    M, K = a.shape
    N = b.shape[1]
    tm = _pick(M, (4096, 2048, 1024, 512, 256, 128))
    tko = _pick(K, (512, 256, 128))
    tn = _pick(N, (512, 256, 128))
    nm = M // tm

    def body(a_ref, b_ref, o_ref):
        part = _dot(a_ref[...], b_ref[...], TN)
        r = pl.program_id(2)

        @pl.when(r == 0)
        def _():
            o_ref[...] = part

        @pl.when(r > 0)
        def _():
            o_ref[...] += part

    return pl.pallas_call(
        body, name=name, grid=(K // tko, N // tn, nm),
        in_specs=[pl.BlockSpec((tm, tko), lambda ko, n, r: (r, ko)), pl.BlockSpec((tm, tn), lambda ko, n, r: (r, n))],
        out_specs=pl.BlockSpec((None, tko, tn), lambda ko, n, r: (0, ko, n)),
        out_shape=jax.ShapeDtypeStruct((1, K, N), F32),
        compiler_params=_params(("parallel", "parallel", "arbitrary")),
    )(a, b)


def mm_dw(at, b, *, name, split_b=False):
    K, M = at.shape
    if split_b:
        Nh = b.shape[2]
        N = 2 * Nh
    else:
        N = b.shape[1]
        Nh = N
    if b.dtype.itemsize == 2:
        tm = _pick(M, (4096, 2048, 1024, 512, 256, 128))
        tko = _pick(K, (1024, 704, 512, 256, 128))
        tn = _pick(Nh, (512, 256, 128))
    else:
        tm = _pick(M, (1024, 512, 256, 128))
        tko = _pick(K, (1024, 1408, 704, 512, 256, 128))
        tn = _pick(Nh, (1408, 1024, 768, 512, 256, 128))
    nm = M // tm
    nh = Nh // tn

    def body(a_ref, b_ref, o_ref):
        part = _dot(a_ref[...], b_ref[...], NN)
        r = pl.program_id(2)

        @pl.when(r == 0)
        def _():
            o_ref[...] = part

        @pl.when(r > 0)
        def _():
            o_ref[...] += part

    if split_b:
        b_spec = pl.BlockSpec((None, tm, tn), lambda ko, n, r: (n // nh, r, n % nh))
    else:
        b_spec = pl.BlockSpec((tm, tn), lambda ko, n, r: (r, n))
    return pl.pallas_call(
        body, name=name, grid=(K // tko, N // tn, nm),
        in_specs=[pl.BlockSpec((tko, tm), lambda ko, n, r: (ko, r)), b_spec],
        out_specs=pl.BlockSpec((None, tko, tn), lambda ko, n, r: (0, ko, n)),
        out_shape=jax.ShapeDtypeStruct((1, K, N), F32),
        compiler_params=_params(("parallel", "parallel", "arbitrary")),
    )(at, b)


def _rms(x, g):
    return x * lax.rsqrt(jnp.mean(x * x, axis=-1, keepdims=True) + EPS) * g


def rmsnorm_fwd(h, g, l, *, name):
    S, D = h.shape
    tm = _pick(S, (512, 256, 128))

    def body(h_ref, g_ref, o_ref, ot_ref):
        y = _rms(h_ref[...], g_ref[...])
        o_ref[...] = y.astype(BF16)
        ot_ref[...] = y.T.astype(BF16)

    return pl.pallas_call(
        body, name=name, grid=(S // tm,),
        in_specs=[pl.BlockSpec((tm, D), lambda m: (m, 0)),
                  pl.BlockSpec((None, 1, D), lambda m: (l, 0, 0))],
        out_specs=[pl.BlockSpec((tm, D), lambda m: (m, 0)), pl.BlockSpec((D, tm), lambda m: (0, m))],
        out_shape=[jax.ShapeDtypeStruct((S, D), BF16), jax.ShapeDtypeStruct((D, S), BF16)],
        compiler_params=_params(("parallel",)),
    )(h, g)


def loss_head(h, g, target, *, name):
    S, D = h.shape
    tm = _pick(S, (512, 256, 128))

    def body(h_ref, g_ref, t_ref, loss_ref, dh_ref, dg_ref):
        y, vjp = jax.vjp(_rms, h_ref[...], g_ref[...])
        err = y - t_ref[...]
        part = 0.5 * jnp.sum(jnp.sum(err * err, axis=-1, keepdims=True) / D, axis=0, keepdims=True)
        dx, dg = vjp(err / D)
        dh_ref[...] = dx

        @pl.when(pl.program_id(0) == 0)
        def _():
            dg_ref[...] = dg
            loss_ref[...] = part

        @pl.when(pl.program_id(0) > 0)
        def _():
            dg_ref[...] += dg
            loss_ref[...] += part

    row = pl.BlockSpec((tm, D), lambda m: (m, 0))
    return pl.pallas_call(
        body, name=name, grid=(S // tm,),
        in_specs=[row, pl.BlockSpec((1, D), lambda m: (0, 0)), row],
        out_specs=[pl.BlockSpec((1, 1), lambda m: (0, 0)), row, pl.BlockSpec((1, D), lambda m: (0, 0))],
        out_shape=[jax.ShapeDtypeStruct((1, 1), F32), jax.ShapeDtypeStruct((S, D), F32),
                   jax.ShapeDtypeStruct((1, D), F32)],
        compiler_params=_params(("arbitrary",)),
    )(h, g, target)


def _make_mm(dims_fwd, dims_da, dims_db, swap_da=False, swap_db=False):
    @jax.custom_vjp
    def mm(a, b):
        return _dot(a, b, dims_fwd)

    def fwd(a, b):
        return mm(a, b), (a, b)

    def bwd(resid, g):
        a, b = resid
        da = _dot(b, g, dims_da) if swap_da else _dot(g, b, dims_da)
        db = _dot(g, a, dims_db) if swap_db else _dot(a, g, dims_db)
        return da, db

    mm.defvjp(fwd, bwd)
    return mm


_mm = _make_mm(NN, NT, TN)
_mm_nt = _make_mm(NT, NN, TN, swap_db=True)
_mm_tn = _make_mm(TN, NT, NN, swap_da=True)


def _dot_hi(a, b):
    return jnp.dot(a, b, precision=lax.Precision.HIGHEST, preferred_element_type=F32)


def _sigmoid(x):
    return 0.5 * jnp.tanh(0.5 * x) + 0.5


def _silu(x):
    return x * _sigmoid(x)


def _row_slices(x, n):
    rows = x.shape[0] // n
    return tuple(x[i * rows:(i + 1) * rows] for i in range(n))


@functools.partial(jax.custom_vjp, nondiff_argnums=(1,))
def _split_rows(x, n):
    return _row_slices(x, n)


_split_rows.defvjp(lambda x, n: (_row_slices(x, n), None), lambda n, _, cts: (jnp.concatenate(cts, axis=0),))


def _attn_block(qs, kws, vws, sink, slope, first):
    s = jnp.concatenate([_mm_nt(q, kw) for q, kw in zip(qs, kws)], axis=0) * (HEAD_DIM ** -0.5)
    qi = lax.broadcasted_iota(jnp.int32, s.shape, 0) % WINDOW
    ki = lax.broadcasted_iota(jnp.int32, s.shape, 1)
    dist = qi + WINDOW - ki
    valid = (dist >= 0) & (dist < WINDOW) & (ki >= WINDOW * first)
    s = jnp.where(valid, s - slope * dist.astype(F32), -jnp.inf)
    m = lax.stop_gradient(jnp.maximum(jnp.max(s, axis=-1, keepdims=True), sink))
    e = jnp.exp(s - m)
    denom = jnp.sum(e, axis=-1, keepdims=True) + jnp.exp(sink - m)
    ps = _split_rows(e * (1.0 / denom), len(qs))
    return tuple(_mm(p, vw) for p, vw in zip(ps, vws))


def _attn_specs(Hkv, order):
    G = Q_PER_KV
    qspec = pl.BlockSpec((Hkv, G, WINDOW, HEAD_DIM), lambda n: (0, 0, order(n), 0))
    prev = pl.BlockSpec((Hkv, WINDOW, HEAD_DIM), lambda n: (0, jnp.maximum(order(n) - 1, 0), 0))
    cur = pl.BlockSpec((Hkv, WINDOW, HEAD_DIM), lambda n: (0, order(n), 0))
    col = pl.BlockSpec((Hkv, G * WINDOW, 1), lambda n: (0, 0, 0))
    return qspec, prev, cur, col


def attn_fwd(q, k, v, sink_col, slope_col, *, name, comm=None):
    Hkv, G, S, _ = q.shape
    qspec, prev, cur, col = _attn_specs(Hkv, lambda n: n)

    def body(q_ref, kp_ref, kc_ref, vp_ref, vc_ref, sink_ref, slope_ref, o_ref):
        first = (pl.program_id(0) == 0).astype(jnp.int32)
        qs = tuple(q_ref[h].reshape(G * WINDOW, HEAD_DIM) for h in range(Hkv))
        kws = tuple(jnp.concatenate([kp_ref[h], kc_ref[h]], axis=0) for h in range(Hkv))
        vws = tuple(jnp.concatenate([vp_ref[h], vc_ref[h]], axis=0) for h in range(Hkv))
        outs = _attn_block(qs, kws, vws, sink_ref[...].reshape(Hkv * G * WINDOW, 1),
                           slope_ref[...].reshape(Hkv * G * WINDOW, 1), first)
        o_ref[...] = jnp.stack([o.reshape(G, WINDOW, HEAD_DIM).astype(BF16) for o in outs])

    outs, updated, created = _carried_call(
        body, [q, k, k, v, v, sink_col, slope_col], name=name, grid=(S // WINDOW,),
        in_specs=[qspec, prev, cur, prev, cur, col, col], out_specs=[qspec],
        out_shape=[jax.ShapeDtypeStruct(q.shape, BF16)], comm=comm)
    return outs[0] if comm is None else (outs[0], updated, created)


def attn_bwd(q, k, v, sink_col, slope_col, do, *, name, comm=None):
    Hkv, G, S, _ = q.shape
    nb = S // WINDOW
    qspec, prev, cur, col = _attn_specs(Hkv, lambda n: nb - 1 - n)

    def body(q_ref, kp_ref, kc_ref, vp_ref, vc_ref, sink_ref, slope_ref, do_ref,
             dq_ref, dk_ref, dv_ref, dsink_ref, dk_carry, dv_carry):
        @pl.when(pl.program_id(0) == 0)
        def _():
            dk_carry[...] = jnp.zeros_like(dk_carry)
            dv_carry[...] = jnp.zeros_like(dv_carry)
            dsink_ref[...] = jnp.zeros_like(dsink_ref)

        first = (pl.program_id(0) == nb - 1).astype(jnp.int32)
        qs = tuple(q_ref[h].reshape(G * WINDOW, HEAD_DIM).astype(F32) for h in range(Hkv))
        kws = tuple(jnp.concatenate([kp_ref[h], kc_ref[h]], axis=0).astype(F32) for h in range(Hkv))
        vws = tuple(jnp.concatenate([vp_ref[h], vc_ref[h]], axis=0).astype(F32) for h in range(Hkv))
        fn = functools.partial(_attn_block, slope=slope_ref[...].reshape(Hkv * G * WINDOW, 1), first=first)
        _, vjp = jax.vjp(fn, qs, kws, vws, sink_ref[...].reshape(Hkv * G * WINDOW, 1))
        dqs, dkws, dvws, dsink = vjp(tuple(do_ref[h].reshape(G * WINDOW, HEAD_DIM).astype(F32) for h in range(Hkv)))
        dkw, dvw = jnp.stack(dkws), jnp.stack(dvws)
        dq_ref[...] = jnp.stack([dq.reshape(G, WINDOW, HEAD_DIM) for dq in dqs]).astype(BF16)
        dk_ref[...] = (dkw[:, WINDOW:] + dk_carry[...]).astype(BF16)
        dv_ref[...] = (dvw[:, WINDOW:] + dv_carry[...]).astype(BF16)
        dk_carry[...] = dkw[:, :WINDOW]
        dv_carry[...] = dvw[:, :WINDOW]
        dsink_ref[...] += dsink.reshape(Hkv, G * WINDOW, 1)

    kv_shape = jax.ShapeDtypeStruct(k.shape, BF16)
    outs, _, created = _carried_call(
        body, [q, k, k, v, v, sink_col, slope_col, do], name=name, grid=(nb,),
        in_specs=[qspec, prev, cur, prev, cur, col, col, qspec],
        out_specs=[qspec, cur, cur, col],
        out_shape=[jax.ShapeDtypeStruct(q.shape, BF16), kv_shape, kv_shape,
                   jax.ShapeDtypeStruct(sink_col.shape, F32)],
        scratch_shapes=[pltpu.VMEM((Hkv, WINDOW, HEAD_DIM), F32), pltpu.VMEM((Hkv, WINDOW, HEAD_DIM), F32)],
        comm=comm)
    return outs if comm is None else (*outs, created)


def _hgrn_consts():
    C = HG_CHUNK
    t = np.arange(C)[:, None]
    j = np.arange(C)[None, :]
    cq, ck, mk = [], [], []
    H = C // 2
    while H >= 1:
        start = (t // H) * H
        cq.append((j > start) & (j <= t))
        ck.append((j > t) & (j <= start + H))
        mk.append(((t // H) % 2 == 1) & ((j // H) == (t // H) - 1))
        H //= 2
    ltri = (j <= t)
    ops = np.concatenate([ltri] + cq + ck, axis=0).astype(np.float32)
    ops3 = np.concatenate([ops, ops, ops], axis=1)
    opst3 = np.concatenate([ops.T, ops.T, ops.T], axis=1)
    return (jnp.asarray(ops3, BF16), jnp.asarray(opst3, BF16), jnp.asarray(np.stack(mk).astype(np.float32)))


def _split3(x):
    hi = x.astype(BF16)
    r = x - hi.astype(F32)
    mid = r.astype(BF16)
    lo = (r - mid.astype(F32)).astype(BF16)
    return jnp.concatenate([hi, mid, lo], axis=0)


@jax.custom_vjp
def _seg_sums(g, ops3, opst3):
    C = g.shape[0]
    out = jnp.dot(ops3, _split3(g), preferred_element_type=F32)
    return tuple(out[i * C:(i + 1) * C] for i in range(ops3.shape[0] // C))


def _seg_sums_fwd(g, ops3, opst3):
    return _seg_sums(g, ops3, opst3), opst3


def _seg_sums_bwd(opst3, cts):
    ct = jnp.concatenate(cts, axis=0)
    return jnp.dot(opst3, _split3(ct), preferred_element_type=F32), None, None


_seg_sums.defvjp(_seg_sums_fwd, _seg_sums_bwd)


def _head_slices(x):
    return tuple(x[:, h * HG_EXPAND:(h + 1) * HG_EXPAND] for h in range(x.shape[1] // HG_EXPAND))


@jax.custom_vjp
def _split_heads(x):
    return _head_slices(x)


_split_heads.defvjp(lambda x: (_head_slices(x), None), lambda _, cts: (jnp.concatenate(cts, axis=1),))


@jax.custom_vjp
def _join_heads(xs):
    return jnp.concatenate(xs, axis=1)


_join_heads.defvjp(lambda xs: (jnp.concatenate(xs, axis=1), None), lambda _, ct: (_head_slices(ct),))


def _hgrn_chunk(qr, fr, iv, gr, sts, logits, ng, ops3, opst3, mk, layer):
    n_levels = mk.shape[0]
    depth, D = logits.shape
    e = jnp.exp(logits - lax.stop_gradient(jnp.max(logits, axis=0, keepdims=True)))
    sm = e / jnp.sum(e, axis=0, keepdims=True)
    row = lax.broadcasted_iota(jnp.int32, (depth, D), 0)
    lb = jnp.sum(jnp.where((row >= 1) & (row <= layer), sm, 0.0), axis=0, keepdims=True)

    q = _silu(qr)
    a = jnp.log(lb)
    c = jnp.log(1.0 - lb) - (jnp.maximum(-fr, 0.0) + jnp.log(1.0 + jnp.exp(-jnp.abs(fr))))
    mx = lax.stop_gradient(jnp.maximum(a, c))
    g = mx + jnp.log(jnp.exp(a - mx) + jnp.exp(c - mx))
    k = (1.0 - lb) * _sigmoid(-fr)

    sums = _seg_sums(g, ops3, opst3)
    b = sums[0]
    b_last = jnp.sum(g, axis=0, keepdims=True)

    ql = [_split_heads(q * jnp.exp(sums[1 + lv])) for lv in range(n_levels)]
    kl = [_split_heads(k * jnp.exp(sums[1 + n_levels + lv])) for lv in range(n_levels)]
    qb, kb = _split_heads(q * jnp.exp(b)), _split_heads(k * jnp.exp(b_last - b))
    qk, ivh, gate, decay = _split_heads(q * k), _split_heads(iv), _split_heads(_silu(gr)), _split_heads(jnp.exp(b_last))
    outs, new_sts = [], []
    for h, st in enumerate(sts):
        amat = None
        for lv in range(n_levels):
            part = _mm_nt(ql[lv][h], kl[lv][h]) * mk[lv]
            amat = part if amat is None else amat + part
        o = jnp.sum(qk[h], axis=-1, keepdims=True) * ivh[h] + _mm(amat, ivh[h]) + _mm_nt(qb[h], st)
        new_sts.append(st * decay[h] + _mm_tn(ivh[h], kb[h]))
        on = o * lax.rsqrt(jnp.mean(o * o, axis=-1, keepdims=True) + EPS) * ng
        outs.append(on * gate[h])
    return _join_heads(tuple(outs)), tuple(new_sts)


def _hgrn_in_specs(D, depth, order):
    C = HG_CHUNK
    nl = int(math.log2(C))
    n_ops = 1 + 2 * nl
    return [pl.BlockSpec((C, 4 * D), lambda c: (order(c), 0)),
            pl.BlockSpec((depth, D), lambda c: (0, 0)),
            pl.BlockSpec((1, HG_EXPAND), lambda c: (0, 0)),
            pl.BlockSpec((n_ops * C, 3 * C), lambda c: (0, 0)),
            pl.BlockSpec((C, 3 * n_ops * C), lambda c: (0, 0)),
            pl.BlockSpec((nl, C, C), lambda c: (0, 0, 0))]


def hgrn_fwd(proj, logits, ng, layer, *, name, comm=None):
    S = proj.shape[0]
    D = proj.shape[1] // 4
    H = D // HG_EXPAND
    C = HG_CHUNK
    nc = S // C
    consts = _hgrn_consts()

    def body(p_ref, lg_ref, ng_ref, ops3_ref, opst3_ref, mk_ref, o_ref, st_out_ref, st_ref):
        @pl.when(pl.program_id(0) == 0)
        def _():
            st_ref[...] = jnp.zeros_like(st_ref)

        st_out_ref[...] = st_ref[...]
        out, new_sts = _hgrn_chunk(*(p_ref[:, j * D:(j + 1) * D] for j in range(4)), tuple(st_ref[h] for h in range(H)),
                                   lg_ref[...], ng_ref[...], ops3_ref[...], opst3_ref[...], mk_ref[...], layer)
        o_ref[...] = out.astype(BF16)
        for h in range(H):
            st_ref[h] = new_sts[h]

    outs, updated, created = _carried_call(
        body, [proj, logits, ng, *consts], name=name, grid=(nc,),
        in_specs=_hgrn_in_specs(D, logits.shape[0], lambda c: c),
        out_specs=[pl.BlockSpec((C, D), lambda c: (c, 0)),
                   pl.BlockSpec((None, H, HG_EXPAND, HG_EXPAND), lambda c: (c, 0, 0, 0))],
        out_shape=[jax.ShapeDtypeStruct((S, D), BF16),
                   jax.ShapeDtypeStruct((nc, H, HG_EXPAND, HG_EXPAND), F32)],
        scratch_shapes=[pltpu.VMEM((H, HG_EXPAND, HG_EXPAND), F32)], comm=comm)
    return outs if comm is None else (*outs, updated, created)


def hgrn_bwd(proj, logits, ng, states, dout, layer, *, name, comm=None):
    S = proj.shape[0]
    D = proj.shape[1] // 4
    H = D // HG_EXPAND
    C = HG_CHUNK
    nc = S // C
    depth = logits.shape[0]
    consts = _hgrn_consts()
    rev = lambda c: nc - 1 - c

    def body(p_ref, lg_ref, ng_ref, ops3_ref, opst3_ref, mk_ref, st_in_ref, do_ref,
             dp_ref, dlg_ref, dng_ref, dst_ref):
        @pl.when(pl.program_id(0) == 0)
        def _():
            dst_ref[...] = jnp.zeros_like(dst_ref)
            dlg_ref[...] = jnp.zeros_like(dlg_ref)
            dng_ref[...] = jnp.zeros_like(dng_ref)

        fn = functools.partial(_hgrn_chunk, ops3=ops3_ref[...], opst3=opst3_ref[...], mk=mk_ref[...], layer=layer)
        _, vjp = jax.vjp(fn, *(p_ref[:, j * D:(j + 1) * D] for j in range(4)), tuple(st_in_ref[h] for h in range(H)),
                         lg_ref[...], ng_ref[...])
        *dproj, dsts, dlg, dng = vjp((do_ref[...], tuple(dst_ref[h] for h in range(H))))
        for j in range(4):
            dp_ref[:, j * D:(j + 1) * D] = dproj[j].astype(BF16)
        for h in range(H):
            dst_ref[h] = dsts[h]
        dlg_ref[...] += dlg
        dng_ref[...] += dng

    in_specs = _hgrn_in_specs(D, depth, rev)
    in_specs += [pl.BlockSpec((None, H, HG_EXPAND, HG_EXPAND), lambda c: (rev(c), 0, 0, 0)),
                 pl.BlockSpec((C, D), lambda c: (rev(c), 0))]
    outs, _, created = _carried_call(
        body, [proj, logits, ng, *consts, states, dout], name=name, grid=(nc,), in_specs=in_specs,
        out_specs=[pl.BlockSpec((C, 4 * D), lambda c: (rev(c), 0)),
                   pl.BlockSpec((depth, D), lambda c: (0, 0)), pl.BlockSpec((1, HG_EXPAND), lambda c: (0, 0))],
        out_shape=[jax.ShapeDtypeStruct((S, 4 * D), BF16), jax.ShapeDtypeStruct((depth, D), F32),
                   jax.ShapeDtypeStruct((1, HG_EXPAND), F32)],
        scratch_shapes=[pltpu.VMEM((H, HG_EXPAND, HG_EXPAND), F32)], comm=comm)
    return outs if comm is None else (*outs, created)


def _shift_down(u, prev, shift):
    rolled = pltpu.roll(u, shift, axis=0)
    rows = lax.broadcasted_iota(jnp.int32, prev.shape, 0)
    top = jnp.where(rows < shift, pltpu.roll(prev, shift, axis=0), rolled[:SUBLANES])
    return jnp.concatenate([top, rolled[SUBLANES:]], axis=0)


def _shift_up(x, nxt, shift):
    tm = x.shape[0]
    rolled = pltpu.roll(x, tm - shift, axis=0)
    rows = lax.broadcasted_iota(jnp.int32, nxt.shape, 0)
    bottom = jnp.where(rows >= SUBLANES - shift, pltpu.roll(nxt, SUBLANES - shift, axis=0), rolled[tm - SUBLANES:])
    return jnp.concatenate([rolled[:tm - SUBLANES], bottom], axis=0)


def _conv(u, prev, w_ref, b_ref, half):
    u1 = _shift_down(u, prev, 1)
    u2 = _shift_down(u, prev, 2)
    return b_ref[half] + w_ref[half, 0:1, :] * u2 + w_ref[half, 1:2, :] * u1 + w_ref[half, 2:3, :] * u


def _halo_rows(ref, keep, last):
    full = ref.astype(F32)
    return (full[HALO_ROWS - SUBLANES:] if last else full[:SUBLANES]) * keep


def _conv_specs(tm, tc, l):
    nprev = lambda m: jnp.maximum(m * (tm // HALO_ROWS) - 1, 0)
    tile = pl.BlockSpec((2, tm, tc), lambda j, m: (0, m, j))
    halo = pl.BlockSpec((2, HALO_ROWS, tc), lambda j, m: (0, nprev(m), j))
    wspec = pl.BlockSpec((None, 2, 3, tc), lambda j, m: (l, 0, 0, j))
    bspec = pl.BlockSpec((None, 2, 1, tc), lambda j, m: (l, 0, 0, j))
    return tile, halo, wspec, bspec


def convact_fwd(u, cw, cb, l, *, name, comm=None):
    _, S, Fh = u.shape
    tm = _pick(S, (256, 128))
    tc = _pick(Fh, (1408, 512, 256, 128))
    tile, halo, wspec, bspec = _conv_specs(tm, tc, l)

    def body(u_ref, p_ref, w_ref, b_ref, o_ref, ot_ref, c_ref):
        keep = (pl.program_id(1) > 0).astype(F32)
        cg = _conv(u_ref[0].astype(F32), _halo_rows(p_ref[0], keep, True), w_ref, b_ref, 0)
        cv = _conv(u_ref[1].astype(F32), _halo_rows(p_ref[1], keep, True), w_ref, b_ref, 1)
        act = _silu(cg) * cv
        o_ref[...] = act.astype(BF16)
        ot_ref[...] = act.T.astype(BF16)
        c_ref[0] = cg.astype(BF16)
        c_ref[1] = cv.astype(BF16)

    outs, updated, _ = _carried_call(
        body, [u, u, cw, cb], name=name, grid=(Fh // tc, S // tm), in_specs=[tile, halo, wspec, bspec],
        out_specs=[pl.BlockSpec((tm, tc), lambda j, m: (m, j)), pl.BlockSpec((tc, tm), lambda j, m: (j, m)), tile],
        out_shape=[jax.ShapeDtypeStruct((S, Fh), BF16), jax.ShapeDtypeStruct((Fh, S), BF16),
                   jax.ShapeDtypeStruct(u.shape, BF16)], comm=comm)
    return tuple(outs) if comm is None else (*outs, updated)


def convact_bwd(c, u, cw, l, da, *, name):
    _, S, Fh = u.shape
    tm = _pick(S, (256, 128))
    tc = _pick(Fh, (1408, 512, 256, 128))
    nm = S // tm

    def body(c_ref, u_ref, w_ref, da_ref, du_ref, dw_ref, db_ref, carry):
        @pl.when(pl.program_id(1) == 0)
        def _():
            carry[...] = jnp.zeros_like(carry)
            dw_ref[...] = jnp.zeros_like(dw_ref)
            db_ref[...] = jnp.zeros_like(db_ref)

        cg, cv = c_ref[0].astype(F32), c_ref[1].astype(F32)
        da_t = da_ref[...]
        sg = _sigmoid(cg)
        dcv = da_t * (cg * sg)
        dcg = da_t * cv * (sg * (1.0 + cg * (1.0 - sg)))
        rs = lambda t: jnp.sum(t, axis=0, keepdims=True)
        for half, dc in ((0, dcg), (1, dcv)):
            nxt = carry[half]
            d1, d2 = _shift_up(dc, nxt, 1), _shift_up(dc, nxt, 2)
            du = w_ref[half, 2:3, :] * dc + w_ref[half, 1:2, :] * d1 + w_ref[half, 0:1, :] * d2
            du_ref[half] = du.astype(BF16)
            uh = u_ref[half].astype(F32)
            for j, d in enumerate((d2, d1, dc)):
                dw_ref[half, j:j + 1, :] += rs(d * uh)
            db_ref[half] += rs(dc)
            carry[half] = dc[:SUBLANES]

    tile = pl.BlockSpec((2, tm, tc), lambda j, m: (0, nm - 1 - m, j))
    return pl.pallas_call(
        body, name=name, grid=(Fh // tc, nm),
        in_specs=[tile, tile, pl.BlockSpec((None, 2, 3, tc), lambda j, m: (l, 0, 0, j)),
                  pl.BlockSpec((tm, tc), lambda j, m: (nm - 1 - m, j))],
        out_specs=[tile, pl.BlockSpec((2, 3, tc), lambda j, m: (0, 0, j)),
                   pl.BlockSpec((2, 1, tc), lambda j, m: (0, 0, j))],
        out_shape=[jax.ShapeDtypeStruct(u.shape, BF16), jax.ShapeDtypeStruct((2, 3, Fh), F32),
                   jax.ShapeDtypeStruct((2, 1, Fh), F32)],
        scratch_shapes=[pltpu.VMEM((2, SUBLANES, tc), F32)],
        compiler_params=_params(("parallel", "arbitrary")),
    )(c, u, cw, da)


def _adamw_math(w, g, m, v):
    m = ADAM_B1 * m + (1.0 - ADAM_B1) * g
    v = ADAM_B2 * v + (1.0 - ADAM_B2) * (g * g)
    m_hat = m / (1.0 - ADAM_B1 ** ADAM_STEP)
    v_hat = v / (1.0 - ADAM_B2 ** ADAM_STEP)
    delta = -ADAM_LR * (m_hat / (jnp.sqrt(v_hat) + ADAM_EPS) + ADAM_WD * w)
    return delta, m, v


def _row_tile(R, Ccols, n_streams):
    for tr in (512, 256, 128, 64, 32, 16, 8):
        if R % tr == 0 and tr * Ccols * 4 * n_streams * 2 <= VMEM_LIMIT_BYTES // 2:
            return tr
    return R


def adamw(w, g, m, v, *, name):
    R, Ccols = w.shape
    tr = _row_tile(R, Ccols, 7)

    def body(w_ref, g_ref, m_ref, v_ref, d_ref, mo_ref, vo_ref):
        d, mn, vn = _adamw_math(w_ref[...], g_ref[...], m_ref[...], v_ref[...])
        d_ref[...] = d
        mo_ref[...] = mn
        vo_ref[...] = vn

    spec = pl.BlockSpec((tr, Ccols), lambda i: (i, 0))
    shp = jax.ShapeDtypeStruct((R, Ccols), F32)
    return pl.pallas_call(
        body, name=name, grid=(R // tr,), in_specs=[spec] * 4, out_specs=[spec] * 3,
        out_shape=[shp, shp, shp], compiler_params=_params(("parallel",)),
    )(w, g, m, v)


def _prefetch_call(body, pos, args, *, name, grid, in_specs, out_specs, out_shape, sem, aliases=None):
    return pl.pallas_call(
        lambda pos_ref, *refs: body(*refs), name=name,
        grid_spec=pltpu.PrefetchScalarGridSpec(num_scalar_prefetch=1, grid=grid, in_specs=in_specs,
                                               out_specs=out_specs),
        out_shape=out_shape, input_output_aliases={1 + i: o for i, o in (aliases or {}).items()},
        compiler_params=_params(sem),
    )(pos, *args)


def cast_place(w, pos, *, row_sharded, name):
    L, R, Ccols = w.shape

    def body(w_ref, o_ref):
        o_ref[...] = w_ref[...].astype(BF16)

    if row_sharded:
        grid = (L,)
        in_spec = pl.BlockSpec((None, R, Ccols), lambda l, pos: (l, 0, 0))
        out_spec = pl.BlockSpec((None, R, Ccols), lambda l, pos: (l, pos[0], 0))
        out_shape = jax.ShapeDtypeStruct((L, N_CHIPS * R, Ccols), BF16)
        sem = ("parallel",)
    else:
        tr = _pick(R, (512, 256, 128))
        grid = (L, R // tr)
        in_spec = pl.BlockSpec((None, tr, Ccols), lambda l, i, pos: (l, i, 0))
        out_spec = pl.BlockSpec((None, tr, Ccols), lambda l, i, pos: (l, i, pos[0]))
        out_shape = jax.ShapeDtypeStruct((L, R, N_CHIPS * Ccols), BF16)
        sem = ("parallel", "parallel")
    return _prefetch_call(body, pos, [w], name=name, grid=grid, in_specs=[in_spec], out_specs=out_spec,
                          out_shape=out_shape, sem=sem)


def pair_add(g, theirs, pos, *, row_sharded, name):
    def body(g_ref, t_ref, o_ref, ob_ref):
        s = g_ref[...] + t_ref[...]
        o_ref[...] = s
        ob_ref[...] = s.astype(BF16)

    if row_sharded:
        L, _, _, r2, D = g.shape
        grid = (L, N_CHIPS)
        g_spec = pl.BlockSpec((None, None, None, r2, D), lambda l, q, pos: (l, q, pos[1], 0, 0))
        t_spec = pl.BlockSpec((None, None, r2, D), lambda l, q, pos: (l, q, 0, 0))
    else:
        L, K, N = g.shape
        n = N // N_CHIPS
        tr = _pick(K // 2, (256, 128))
        nb = (K // 2) // tr
        grid = (L, nb, N_CHIPS)
        g_spec = pl.BlockSpec((None, tr, n), lambda l, i, j, pos: (l, pos[1] * nb + i, j))
        t_spec = pl.BlockSpec((None, tr, n), lambda l, i, j, pos: (l, i, j))
    return _prefetch_call(body, pos, [g, theirs], name=name, grid=grid, in_specs=[g_spec, t_spec],
                          out_specs=[t_spec, t_spec],
                          out_shape=[jax.ShapeDtypeStruct(theirs.shape, F32), jax.ShapeDtypeStruct(theirs.shape, BF16)],
                          sem=("parallel",) * len(grid))


def chip_add(pair, landed, pos, buf, li, n_layers, *, row_sharded, name):
    def body(p_ref, a_ref, b_ref, c_ref, *rest):
        rest[-1][...] = ((p_ref[...] + a_ref[...].astype(F32)) + b_ref[...].astype(F32)) + c_ref[...].astype(F32)

    if row_sharded:
        _, _, r2, D = pair.shape
        grid = (1,)
        p_spec = pl.BlockSpec((None, None, r2, D), lambda i, pos: (0, pos[0], 0, 0))
        l_specs = [pl.BlockSpec((None, r2, D), functools.partial(lambda i, pos, k: (k, 0, 0), k=k)) for k in range(3)]
        out_spec = pl.BlockSpec((None, None, r2, D), lambda i, pos: (li, pos[1], 0, 0))
        out_shape = jax.ShapeDtypeStruct((n_layers, 2, r2, D), F32)
    else:
        _, K2, N = pair.shape
        n = N // N_CHIPS
        tr = _pick(K2, (256, 128))
        nb = K2 // tr
        grid = (nb,)
        p_spec = pl.BlockSpec((None, tr, n), lambda i, pos: (0, i, pos[0]))
        l_specs = [pl.BlockSpec((None, tr, n), functools.partial(lambda i, pos, k: (k, i, 0), k=k)) for k in range(3)]
        out_spec = pl.BlockSpec((None, tr, n), lambda i, pos: (li, pos[1] * nb + i, 0))
        out_shape = jax.ShapeDtypeStruct((n_layers, 2 * K2, n), F32)
    args, in_specs, aliases = [pair, landed, landed, landed], [p_spec] + l_specs, None
    if buf is not None:
        args.append(buf)
        in_specs.append(pl.BlockSpec(memory_space=pl.ANY))
        aliases = {4: 0}
    return _prefetch_call(body, pos, args, name=name, grid=grid, in_specs=in_specs, out_specs=out_spec,
                          out_shape=out_shape, sem=("parallel",), aliases=aliases)


def _position():
    return lax.axis_index("x"), lax.axis_index("y"), lax.axis_index("c")


def _other_chips(x, y):
    return [(1 - x, y), (x, 1 - y), (1 - x, 1 - y)]


HBM_ANY = pl.BlockSpec(memory_space=pl.ANY)
COMM_PARAMS = pltpu.CompilerParams(has_side_effects=True)


class Comm:
    def __init__(self, ro=(), inout=(), new=(), sems=(), phases=()):
        self.ro, self.inout, self.new, self.sems, self.phases = list(ro), list(inout), list(new), list(sems), list(phases)


def _carried_call(body, args, *, name, grid, in_specs, out_specs, out_shape, scratch_shapes=(), comm=None):
    comm = comm or Comm()
    n_in, n_out, n_sc = len(in_specs), len(out_shape), len(scratch_shapes)
    n_ro, n_io, n_new, n_sem = len(comm.ro), len(comm.inout), len(comm.new), len(comm.sems)
    last = int(np.prod(grid)) - 1

    def wrapped(*refs):
        bounds = np.cumsum([0, n_in, n_ro, n_io, n_out, n_io, n_new, n_sc, n_sem])
        core_in, ro, _, core_out, io, new, core_sc, sems = (refs[a:b] for a, b in zip(bounds[:-1], bounds[1:]))
        step = pl.program_id(0)
        for d in range(1, len(grid)):
            step = step * grid[d] + pl.program_id(d)
        for frac, fn in comm.phases[:-1]:
            pl.when(step == int(frac * last))(functools.partial(fn, ro, io, new, sems))
        body(*core_in, *core_out, *core_sc)
        if comm.phases:
            pl.when(step == last)(functools.partial(comm.phases[-1][1], ro, io, new, sems))

    outs = pl.pallas_call(
        wrapped, name=name, grid=grid,
        in_specs=list(in_specs) + [HBM_ANY] * (n_ro + n_io),
        out_specs=list(out_specs) + [HBM_ANY] * (n_io + n_new),
        out_shape=list(out_shape) + [jax.ShapeDtypeStruct(a.shape, a.dtype) for a in comm.inout] + comm.new,
        input_output_aliases={n_in + n_ro + j: n_out + j for j in range(n_io)},
        scratch_shapes=list(scratch_shapes) + comm.sems,
        compiler_params=pltpu.CompilerParams(dimension_semantics=("arbitrary",) * len(grid),
                                             vmem_limit_bytes=VMEM_LIMIT_BYTES, has_side_effects=bool(comm.phases)),
    )(*args, *comm.ro, *comm.inout)
    return outs[:n_out], outs[n_out:n_out + n_io], outs[n_out + n_io:]


def run_comm(comm, *, name):
    n_ro, n_io, n_new = len(comm.ro), len(comm.inout), len(comm.new)

    def body(*refs):
        bounds = np.cumsum([0, n_ro, n_io, n_io, n_new, len(comm.sems)])
        ro, _, io, new, sems = (refs[a:b] for a, b in zip(bounds[:-1], bounds[1:]))
        for _, fn in comm.phases:
            fn(ro, io, new, sems)

    outs = pl.pallas_call(
        body, name=name, in_specs=[HBM_ANY] * (n_ro + n_io), out_specs=[HBM_ANY] * (n_io + n_new),
        out_shape=[jax.ShapeDtypeStruct(a.shape, a.dtype) for a in comm.inout] + comm.new,
        input_output_aliases={n_ro + j: j for j in range(n_io)},
        scratch_shapes=comm.sems, compiler_params=COMM_PARAMS,
    )(*comm.ro, *comm.inout)
    return outs[:n_io], outs[n_io:]


def _rdma(src, dst, send_sem, recv_sem, device):
    return pltpu.make_async_remote_copy(src_ref=src, dst_ref=dst, send_sem=send_sem, recv_sem=recv_sem,
                                        device_id=device, device_id_type=MESH)


def gather_comm(bufs, slices):
    n = len(slices)

    def region(io, s, h, chip):
        b, li, row_sharded = slices[s]
        ref = io[b]
        if row_sharded:
            r = ref.shape[1] // N_CHIPS
            return ref.at[li, pl.ds(chip * r + h * (r // 2), r // 2), :]
        K, nn = ref.shape[1], ref.shape[2] // N_CHIPS
        return ref.at[li, pl.ds(h * (K // 2), K // 2), pl.ds(chip * nn, nn)]

    def copies(io, sems):
        x, y, c = _position()
        me = 2 * x + y
        chips = _other_chips(x, y)
        ids = [2 * cx + cy for cx, cy in chips]

        def ici(s, k, src):
            return _rdma(region(io, s, c, me), region(io, s, c, src), sems[0].at[s, k], sems[1].at[s, k], (*chips[k], c))

        def d2d(s, k, h):
            return _rdma(region(io, s, h, ids[k]), region(io, s, h, ids[k]), sems[2].at[s, k], sems[3].at[s, k],
                         (x, y, 1 - c))

        return me, ids, c, ici, d2d

    pairs = [(s, k) for s in range(n) for k in range(3)]

    def start(ro, io, new, sems):
        me, _, _, ici, _ = copies(io, sems)
        for s, k in pairs:
            ici(s, k, me).start()

    def pass_on(ro, io, new, sems):
        _, ids, c, ici, d2d = copies(io, sems)
        for s, k in pairs:
            ici(s, k, ids[k]).wait_recv()
            d2d(s, k, c).start()

    def finish(ro, io, new, sems):
        me, _, c, ici, d2d = copies(io, sems)
        for s, k in pairs:
            d2d(s, k, 1 - c).wait_recv()
        for s, k in pairs:
            ici(s, k, me).wait_send()
            d2d(s, k, c).wait_send()

    return Comm(inout=bufs, sems=[pltpu.SemaphoreType.DMA((n, 3))] * 4,
                phases=[(0.0, start), (0.85, pass_on), (1.0, finish)])


def gather_small(shard, *, name):
    n = shard.shape[2]

    def body(s_ref, o_ref, send_sem, recv_sem, local_sem):
        x, y, c = _position()
        me = 2 * x + y
        chips = _other_chips(x, y)
        place = lambda chip: o_ref.at[:, :, pl.ds(chip * n, n)]
        local = pltpu.make_async_copy(s_ref, place(me), local_sem)
        local.start()
        for k in range(3):
            _rdma(s_ref, place(me), send_sem.at[k], recv_sem.at[k], (*chips[k], c)).start()
        for k in range(3):
            _rdma(s_ref, place(2 * chips[k][0] + chips[k][1]), send_sem.at[k], recv_sem.at[k], (*chips[k], c)).wait()
        local.wait()

    return pl.pallas_call(
        body, name=name, in_specs=[HBM_ANY], out_specs=HBM_ANY,
        out_shape=jax.ShapeDtypeStruct((shard.shape[0], shard.shape[1], N_CHIPS * n), shard.dtype),
        scratch_shapes=[pltpu.SemaphoreType.DMA((3,)), pltpu.SemaphoreType.DMA((3,)), pltpu.SemaphoreType.DMA],
        compiler_params=COMM_PARAMS,
    )(shard)


def swap_comm(grads):
    def copies(ro, new, sems):
        x, y, c = _position()
        h = 1 - c
        cps = []
        for s, (a, row_sharded) in enumerate(grads):
            if row_sharded:
                r = a.shape[1] // N_CHIPS
                for q in range(N_CHIPS):
                    cps.append(_rdma(ro[s].at[:, pl.ds(q * r + h * (r // 2), r // 2), :], new[s].at[:, q],
                                     sems[0].at[s, q], sems[1].at[s, q], (x, y, 1 - c)))
            else:
                K = a.shape[1]
                cps.append(_rdma(ro[s].at[:, pl.ds(h * (K // 2), K // 2), :], new[s],
                                 sems[0].at[s, 0], sems[1].at[s, 0], (x, y, 1 - c)))
        return cps

    def start(ro, io, new, sems):
        for cp in copies(ro, new, sems):
            cp.start()

    def finish(ro, io, new, sems):
        for cp in copies(ro, new, sems):
            cp.wait()

    theirs = []
    for a, row_sharded in grads:
        if row_sharded:
            theirs.append(jax.ShapeDtypeStruct((1, N_CHIPS, a.shape[1] // N_CHIPS // 2, a.shape[2]), a.dtype))
        else:
            theirs.append(jax.ShapeDtypeStruct((1, a.shape[1] // 2, a.shape[2]), a.dtype))
    return Comm(ro=[a for a, _ in grads], new=theirs, sems=[pltpu.SemaphoreType.DMA((len(grads), N_CHIPS))] * 2,
                phases=[(0.0, start), (1.0, finish)])


def merge_comms(comms):
    spans, ro, io, new, sems = [], [], [], [], []
    for cm in comms:
        spans.append((len(ro), len(io), len(new), len(sems)))
        ro, io, new, sems = ro + cm.ro, io + cm.inout, new + cm.new, sems + cm.sems

    def phase(frac):
        def run(R, I, N, S):
            for cm, (a, b, d, e) in zip(comms, spans):
                for f, fn in cm.phases:
                    if f == frac:
                        fn(R[a:a + len(cm.ro)], I[b:b + len(cm.inout)], N[d:d + len(cm.new)], S[e:e + len(cm.sems)])
        return run

    fracs = sorted({f for cm in comms for f, _ in cm.phases})
    merged = Comm(ro=ro, inout=io, new=new, sems=sems, phases=[(f, phase(f)) for f in fracs])
    cut = lambda created: [created[d:d + len(cm.new)] for cm, (_, _, d, _) in zip(comms, spans)]
    return merged, cut


def scatter_comm(halves):
    def copies(ro, new, sems):
        x, y, c = _position()
        chips = _other_chips(x, y)
        cps = []
        for s, (a, row_sharded) in enumerate(halves):
            for k in range(3):
                to = 2 * chips[k][0] + chips[k][1]
                if row_sharded:
                    src = ro[s].at[0, to]
                else:
                    n = a.shape[2] // N_CHIPS
                    src = ro[s].at[0, :, pl.ds(to * n, n)]
                cps.append(_rdma(src, new[s].at[k], sems[0].at[s, k], sems[1].at[s, k], (*chips[k], c)))
        return cps

    def start(ro, io, new, sems):
        for cp in copies(ro, new, sems):
            cp.start()

    def finish(ro, io, new, sems):
        for cp in copies(ro, new, sems):
            cp.wait()

    landing = [jax.ShapeDtypeStruct((3, a.shape[2], a.shape[3]) if row_sharded else
                                    (3, a.shape[1], a.shape[2] // N_CHIPS), a.dtype) for a, row_sharded in halves]
    return Comm(ro=[a for a, _ in halves], new=landing, sems=[pltpu.SemaphoreType.DMA((len(halves), 3))] * 2,
                phases=[(0.0, start), (1.0, finish)])


def sibling_join_halves(shards, *, name):
    n_in = len(shards)

    def body(*refs):
        ins = refs[:n_in]
        outs = refs[n_in:2 * n_in]
        send_sem, recv_sem = refs[2 * n_in:]
        x, y, c = _position()
        sibling = (x, y, 1 - c)

        def rows(ref, h):
            R2 = ref.shape[1] // 2
            return ref.at[:, pl.ds(h * R2, R2), :]

        sends = [pltpu.make_async_remote_copy(src_ref=rows(ins[i], c), dst_ref=rows(outs[i], c),
                                              send_sem=send_sem.at[i], recv_sem=recv_sem.at[i],
                                              device_id=sibling, device_id_type=MESH) for i in range(n_in)]
        lands = [pltpu.make_async_remote_copy(src_ref=rows(ins[i], c), dst_ref=rows(outs[i], 1 - c),
                                              send_sem=send_sem.at[i], recv_sem=recv_sem.at[i],
                                              device_id=sibling, device_id_type=MESH) for i in range(n_in)]
        for cp in sends:
            cp.start()
        for i in range(n_in):
            sends[i].wait_send()
            lands[i].wait_recv()

    return pl.pallas_call(
        body, name=name, in_specs=[HBM_ANY] * n_in, out_specs=[HBM_ANY] * n_in,
        out_shape=[jax.ShapeDtypeStruct(a.shape, a.dtype) for a in shards],
        input_output_aliases={i: i for i in range(n_in)},
        scratch_shapes=[pltpu.SemaphoreType.DMA((n_in,)), pltpu.SemaphoreType.DMA((n_in,))],
        compiler_params=COMM_PARAMS,
    )(*shards)


def all_reduce_small(v, *, name):
    R, Ccols = v.shape

    def body(v_ref, o_ref, slots, send_sem, recv_sem):
        x, y, c = _position()
        me = 4 * x + 2 * y + c
        slots[me] = v_ref[...]

        def peer(d):
            return x ^ (d >> 2), y ^ ((d >> 1) & 1), c ^ (d & 1)

        def copy(d, slot):
            return pltpu.make_async_remote_copy(
                src_ref=v_ref, dst_ref=slots.at[slot], send_sem=send_sem.at[d], recv_sem=recv_sem.at[d],
                device_id=peer(d), device_id_type=MESH)

        sends = [copy(d, me) for d in range(1, N_DEV)]
        for cp in sends:
            cp.start()
        for d in range(1, N_DEV):
            px, py, pc = peer(d)
            copy(d, 4 * px + 2 * py + pc).wait_recv()
        for cp in sends:
            cp.wait_send()
        acc = slots[0]
        for d in range(1, N_DEV):
            acc = acc + slots[d]
        o_ref[...] = acc

    return pl.pallas_call(
        body, name=name,
        in_specs=[pl.BlockSpec(memory_space=pltpu.VMEM)], out_specs=pl.BlockSpec(memory_space=pltpu.VMEM),
        out_shape=jax.ShapeDtypeStruct((R, Ccols), F32),
        scratch_shapes=[pltpu.VMEM((N_DEV, R, Ccols), F32), pltpu.SemaphoreType.DMA((N_DEV,)),
                        pltpu.SemaphoreType.DMA((N_DEV,))],
        compiler_params=COMM_PARAMS,
    )(v)


def _pack_small(parts):
    flat = jnp.concatenate([p.reshape(-1).astype(F32) for p in parts])
    n = flat.shape[0]
    rows = -(-n // LANES)
    rows = -(-rows // SUBLANES) * SUBLANES
    return jnp.pad(flat, (0, rows * LANES - n)).reshape(rows, LANES)


def _unpack_small(packed, like):
    flat = packed.reshape(-1)
    out, off = [], 0
    for p in like:
        out.append(flat[off:off + p.size].reshape(p.shape))
        off += p.size
    return out


def kernel(x, norm_mix, norm_ffn, norm_final, attn_w_in, attn_w_out, attn_sinks, hgrn_w_in, hgrn_w_out, hgrn_norm, hgrn_lb_logits, ffn_w_up, ffn_conv_w, ffn_conv_b, ffn_w_down, loss_target, m_norm_mix, m_norm_ffn, m_norm_final, m_attn_w_in, m_attn_w_out, m_attn_sinks, m_hgrn_w_in, m_hgrn_w_out, m_hgrn_norm, m_hgrn_lb_logits, m_ffn_w_up, m_ffn_conv_w, m_ffn_conv_b, m_ffn_w_down, v_norm_mix, v_norm_ffn, v_norm_final, v_attn_w_in, v_attn_w_out, v_attn_sinks, v_hgrn_w_in, v_hgrn_w_out, v_hgrn_norm, v_hgrn_lb_logits, v_ffn_w_up, v_ffn_conv_w, v_ffn_conv_b, v_ffn_w_down):
    S, D = x.shape[1], x.shape[2]
    depth = norm_mix.shape[0]
    n_q = D // HEAD_DIM
    n_kv = n_q // Q_PER_KV
    Fh = ffn_w_down.shape[1] * N_CHIPS
    h = x.reshape(S, D)
    target = loss_target.reshape(S, D)

    col_names = ["attn_w_in", "hgrn_w_in", "ffn_w_up"]
    row_names = ["attn_w_out", "hgrn_w_out", "ffn_w_down"]
    x_, y_, c_ = _position()
    me = 2 * x_ + y_
    pos = jnp.stack([me, c_]).astype(jnp.int32)
    col_w = [cast_place(w, pos, row_sharded=False, name=f"cast_place_{n}")
             for n, w in zip(col_names, (attn_w_in, hgrn_w_in, ffn_w_up))]
    row_w = [cast_place(w, pos, row_sharded=True, name=f"cast_place_{n}")
             for n, w in zip(row_names, (attn_w_out, hgrn_w_out, ffn_w_down))]
    W = dict(zip(col_names + row_names, col_w + row_w))
    is_row = {n: n in row_names for n in W}

    def layer_weights(layer):
        mixer = "attn" if layer % 2 == 0 else "hgrn"
        return [(f"{mixer}_w_in", layer // 2), (f"{mixer}_w_out", layer // 2), ("ffn_w_up", layer), ("ffn_w_down", layer)]

    def gather_of(group):
        names = [n for n, _ in group]
        return names, gather_comm([W[n] for n in names], [(i, li, is_row[n]) for i, (n, li) in enumerate(group)])

    def mixer_weights(layer):
        return layer_weights(layer)[:2] if layer < depth else []

    names_next, comm_next = gather_of(mixer_weights(0))
    filled, _ = run_comm(comm_next, name="gather_mixer0")
    W.update(zip(names_next, filled))
    cw = gather_small(ffn_conv_w, name="gather_conv_w").reshape(depth, 3, 2, Fh).transpose(0, 2, 1, 3)
    cb = ffn_conv_b.reshape(depth, 2, 1, Fh)
    g_mix = norm_mix.reshape(depth, 1, D)
    g_ffn = norm_ffn.reshape(depth, 1, D)

    slopes = jnp.exp2(-8.0 * jnp.arange(1, n_q + 1, dtype=F32) / n_q).reshape(n_kv, Q_PER_KV)
    slope_col = jnp.repeat(slopes, WINDOW, axis=1).reshape(n_kv, Q_PER_KV * WINDOW, 1)

    def to_heads(t, nh):
        return t.reshape(S, nh, HEAD_DIM).transpose(1, 0, 2)

    def from_heads(t):
        return t.transpose(1, 0, 2).reshape(S, -1)

    saved = []
    for layer in range(depth):
        idx = layer // 2
        rec = {"h_mix": h}
        hn, rec["hn_t"] = rmsnorm_fwd(h, g_mix, layer, name=f"norm_mix_fwd{layer}")
        is_attn = layer % 2 == 0
        up_l, down_l = layer_weights(layer)[2:]
        on_mixer = [up_l] if is_attn else [up_l, down_l]
        on_up = [down_l] if is_attn else mixer_weights(layer + 1)
        on_act = mixer_weights(layer + 1)[:1] if is_attn else []
        on_down = mixer_weights(layer + 1)[1:] if is_attn else []
        names_next, comm_next = gather_of(on_mixer)
        if is_attn:
            proj = mm_nn(hn, W["attn_w_in"], idx, name=f"attn_in_fwd{layer}", out_dtype=BF16)
            q = to_heads(proj[:, :n_q * HEAD_DIM], n_q).reshape(n_kv, Q_PER_KV, S, HEAD_DIM)
            k = to_heads(proj[:, n_q * HEAD_DIM:(n_q + n_kv) * HEAD_DIM], n_kv)
            v = to_heads(proj[:, (n_q + n_kv) * HEAD_DIM:], n_kv)
            sink_col = jnp.repeat(attn_sinks[idx].reshape(n_kv, Q_PER_KV), WINDOW, axis=1)
            sink_col = sink_col.reshape(n_kv, Q_PER_KV * WINDOW, 1)
            o_heads, filled, _ = attn_fwd(q, k, v, sink_col, slope_col, name=f"attn_fwd{layer}", comm=comm_next)
            W.update(zip(names_next, filled))
            o = from_heads(o_heads.reshape(n_q, S, HEAD_DIM))
            rec.update(q=q, k=k, v=v, sink_col=sink_col, o=o)
            h = mm_nn(o, W["attn_w_out"], idx, name=f"attn_out_fwd{layer}", res=h)
        else:
            proj = mm_nn(hn, W["hgrn_w_in"], idx, name=f"hgrn_in_fwd{layer}")
            ng = hgrn_norm[idx].reshape(1, HG_EXPAND)
            o, states, filled, _ = hgrn_fwd(proj, hgrn_lb_logits, ng, layer, name=f"hgrn_fwd{layer}", comm=comm_next)
            W.update(zip(names_next, filled))
            rec.update(proj=proj, ng=ng, states=states, o=o)
            h = mm_nn(o, W["hgrn_w_out"], idx, name=f"hgrn_out_fwd{layer}", res=h)
        rec["h_ffn"] = h
        hn2, rec["hn2_t"] = rmsnorm_fwd(h, g_ffn, layer, name=f"norm_ffn_fwd{layer}")
        if on_up:
            names_next, comm_next = gather_of(on_up)
            u, filled = mm_nn(hn2, W["ffn_w_up"], layer, name=f"ffn_up_fwd{layer}", split_out=True, out_dtype=BF16,
                              comm=comm_next)
            W.update(zip(names_next, filled))
        else:
            u = mm_nn(hn2, W["ffn_w_up"], layer, name=f"ffn_up_fwd{layer}", split_out=True, out_dtype=BF16)
        if on_act:
            names_next, comm_next = gather_of(on_act)
            act, rec["act_t"], rec["c"], filled = convact_fwd(u, cw, cb, layer, name=f"ffn_act_fwd{layer}",
                                                              comm=comm_next)
            W.update(zip(names_next, filled))
        else:
            act, rec["act_t"], rec["c"] = convact_fwd(u, cw, cb, layer, name=f"ffn_act_fwd{layer}")
        rec["u"] = u
        if on_down:
            names_next, comm_next = gather_of(on_down)
            h, filled = mm_nn(act, W["ffn_w_down"], layer, name=f"ffn_down_fwd{layer}", res=h, comm=comm_next)
            W.update(zip(names_next, filled))
        else:
            h = mm_nn(act, W["ffn_w_down"], layer, name=f"ffn_down_fwd{layer}", res=h)
        saved.append(rec)

    loss_part, dh, d_norm_final = loss_head(h, norm_final.reshape(1, D), target, name="loss_head")

    n_attn, n_hgrn = attn_w_in.shape[0], hgrn_w_in.shape[0]
    reduced_parts = []
    pending = []

    def swap_of(group, g):
        ordered = sorted(group, key=lambda nl: is_row[nl[0]])
        return ordered, swap_comm([(g[n], is_row[n]) for n, _ in ordered])

    def pair_up(tag, ordered, g, theirs):
        pair, pair_bf16 = {}, {}
        for (n, _), t in zip(ordered, theirs):
            gn = g[n]
            if is_row[n]:
                r = gn.shape[1] // N_CHIPS
                gn = gn.reshape(1, N_CHIPS, 2, r // 2, gn.shape[2])
            pair[n], pair_bf16[n] = pair_add(gn, t, pos, row_sharded=is_row[n], name=f"rs_pair_add_{n}{tag}")
        pending.append((ordered, pair, scatter_comm([(pair_bf16[n], is_row[n]) for n, _ in ordered])))

    def landed_behind(created, cut):
        for (ordered, pair, _), landed in zip(pending, cut(created)):
            reduced_parts.append((ordered, pair, landed))
        pending.clear()

    d_norm_mix, d_norm_ffn = [None] * depth, [None] * depth
    d_conv_w, d_conv_b = [None] * depth, [None] * depth
    d_sinks, d_hgrn_norm = [None] * n_attn, [None] * n_hgrn
    d_logits = jnp.zeros_like(hgrn_lb_logits)
    for layer in reversed(range(depth)):
        idx = layer // 2
        rec = saved[layer]
        g = {"ffn_w_down": mm_dw(rec["act_t"], dh, name=f"ffn_down_dw{layer}")}
        dact = mm_nt(dh, W["ffn_w_down"], layer, name=f"ffn_down_dx{layer}")
        du, dcw, dcb = convact_bwd(rec["c"], rec["u"], cw, layer, dact, name=f"ffn_act_bwd{layer}")
        d_conv_w[layer] = dcw.transpose(1, 0, 2).reshape(3, 2 * Fh)
        d_conv_b[layer] = dcb.reshape(2 * Fh)
        g["ffn_w_up"] = mm_dw(rec["hn2_t"], du, name=f"ffn_up_dw{layer}", split_b=True)
        ordered, swap = swap_of(layer_weights(layer)[2:], g)
        dh, dg, theirs = mm_nt(du, W["ffn_w_up"], layer, name=f"ffn_up_dx{layer}", split_in=True, comm=swap,
                               norm=(rec["h_ffn"], g_ffn, layer, dh))
        pair_up(layer, ordered, g, theirs)
        d_norm_ffn[layer] = dg.reshape(D)
        carried, cut = merge_comms([p[2] for p in pending])
        if layer % 2 == 0:
            g["attn_w_out"] = mm_tn(rec["o"], dh, name=f"attn_out_dw{layer}")
            do = mm_nt(dh, W["attn_w_out"], idx, name=f"attn_out_dx{layer}", out_dtype=BF16)
            do_heads = to_heads(do, n_q).reshape(n_kv, Q_PER_KV, S, HEAD_DIM)
            res = attn_bwd(rec["q"], rec["k"], rec["v"], rec["sink_col"], slope_col, do_heads,
                           name=f"attn_bwd{layer}", comm=carried)
            dq, dk, dv, dsink = res[:4]
            landed_behind(res[-1], cut)
            dproj = jnp.concatenate([from_heads(dq.reshape(n_q, S, HEAD_DIM)), from_heads(dk), from_heads(dv)], axis=1)
            d_sinks[idx] = jnp.sum(dsink.reshape(n_kv, Q_PER_KV, WINDOW), axis=-1).reshape(n_q)
            g["attn_w_in"] = mm_dw(rec["hn_t"], dproj, name=f"attn_in_dw{layer}")
            ordered, swap = swap_of(layer_weights(layer)[:2], g)
            dh, dg, theirs = mm_nt(dproj, W["attn_w_in"], idx, name=f"attn_in_dx{layer}", comm=swap,
                                   norm=(rec["h_mix"], g_mix, layer, dh))
        else:
            g["hgrn_w_out"] = mm_tn(rec["o"], dh, name=f"hgrn_out_dw{layer}")
            do = mm_nt(dh, W["hgrn_w_out"], idx, name=f"hgrn_out_dx{layer}")
            res = hgrn_bwd(rec["proj"], hgrn_lb_logits, rec["ng"], rec["states"], do, layer,
                           name=f"hgrn_bwd{layer}", comm=carried)
            dproj, dlg, dng = res[:3]
            landed_behind(res[-1], cut)
            d_logits = d_logits + dlg
            d_hgrn_norm[idx] = dng.reshape(HG_EXPAND)
            g["hgrn_w_in"] = mm_dw(rec["hn_t"], dproj, name=f"hgrn_in_dw{layer}")
            ordered, swap = swap_of(layer_weights(layer)[:2], g)
            dh, dg, theirs = mm_nt(dproj, W["hgrn_w_in"], idx, name=f"hgrn_in_dx{layer}", comm=swap,
                                   norm=(rec["h_mix"], g_mix, layer, dh))
        pair_up(layer, ordered, g, theirs)
        d_norm_mix[layer] = dg.reshape(D)
    grad_x = dh.reshape(x.shape)
    carried, cut = merge_comms([p[2] for p in pending])
    landed_behind(run_comm(carried, name="rs_scatter_last")[1], cut)

    small_w = [norm_mix, norm_ffn, norm_final, attn_sinks, hgrn_norm, hgrn_lb_logits, ffn_conv_b]
    small_m = [m_norm_mix, m_norm_ffn, m_norm_final, m_attn_sinks, m_hgrn_norm, m_hgrn_lb_logits, m_ffn_conv_b]
    small_v = [v_norm_mix, v_norm_ffn, v_norm_final, v_attn_sinks, v_hgrn_norm, v_hgrn_lb_logits, v_ffn_conv_b]
    small_grads = [jnp.stack(d_norm_mix), jnp.stack(d_norm_ffn), d_norm_final.reshape(D), jnp.stack(d_sinks),
                   jnp.stack(d_hgrn_norm), d_logits, jnp.stack(d_conv_b)]
    gconv = jnp.stack(d_conv_w)
    packed_small = _pack_small(small_grads + [loss_part])
    summed = all_reduce_small(jnp.concatenate([packed_small, _pack_small([gconv])], axis=0), name="all_reduce_small")
    reduced, gconv_rows = summed[:packed_small.shape[0]], summed[packed_small.shape[0]:]
    small_g = _unpack_small(reduced, small_w + [loss_part])
    loss = small_g.pop().reshape(())
    pad = [jnp.zeros((1, 1), F32)]
    sd, sm_, sv_ = adamw(_pack_small(small_w + pad), reduced, _pack_small(small_m + pad),
                         _pack_small(small_v + pad), name="adamw_small")
    small_d = _unpack_small(sd, small_w)
    small_nm = _unpack_small(sm_, small_w)
    small_nv = _unpack_small(sv_, small_w)

    big_names = col_names + row_names
    big_w = dict(attn_w_in=attn_w_in, hgrn_w_in=hgrn_w_in, ffn_w_up=ffn_w_up, attn_w_out=attn_w_out,
                 hgrn_w_out=hgrn_w_out, ffn_w_down=ffn_w_down)
    half_shard = dict.fromkeys(big_names)
    for ordered, pair, landed in reduced_parts:
        for (n, li), ld in zip(ordered, landed):
            half_shard[n] = chip_add(pair[n], ld, pos, half_shard[n], li, big_w[n].shape[0], row_sharded=is_row[n],
                                     name=f"rs_chip_add_{n}{li}")
    shards = sibling_join_halves([half_shard[n].reshape(big_w[n].shape) for n in big_names], name="rs_sibling_join")
    big_m = dict(attn_w_in=m_attn_w_in, hgrn_w_in=m_hgrn_w_in, ffn_w_up=m_ffn_w_up, attn_w_out=m_attn_w_out,
                 hgrn_w_out=m_hgrn_w_out, ffn_w_down=m_ffn_w_down)
    big_v = dict(attn_w_in=v_attn_w_in, hgrn_w_in=v_hgrn_w_in, ffn_w_up=v_ffn_w_up, attn_w_out=v_attn_w_out,
                 hgrn_w_out=v_hgrn_w_out, ffn_w_down=v_ffn_w_down)
    grads, deltas, new_m, new_v = {}, {}, {}, {}
    for name_, gshard in zip(big_names, shards):
        w_ = big_w[name_]
        cols = w_.shape[-1]
        d_, m_, v_ = adamw(w_.reshape(-1, cols), gshard.reshape(-1, cols), big_m[name_].reshape(-1, cols),
                           big_v[name_].reshape(-1, cols), name=f"adamw_{name_}")
        grads[name_] = gshard.reshape(w_.shape)
        deltas[name_], new_m[name_], new_v[name_] = (t.reshape(w_.shape) for t in (d_, m_, v_))

    n_conv = ffn_conv_w.shape[2]
    gconv_all = _unpack_small(gconv_rows, [gconv])[0]
    gconv_mine = lax.dynamic_slice_in_dim(gconv_all, me * n_conv, n_conv, axis=2)
    cshape = ffn_conv_w.shape
    d_, m_, v_ = adamw(ffn_conv_w.reshape(-1, n_conv), gconv_mine.reshape(-1, n_conv),
                       m_ffn_conv_w.reshape(-1, n_conv), v_ffn_conv_w.reshape(-1, n_conv), name="adamw_ffn_conv_w")
    grads["ffn_conv_w"] = gconv_mine
    deltas["ffn_conv_w"], new_m["ffn_conv_w"], new_v["ffn_conv_w"] = (t.reshape(cshape) for t in (d_, m_, v_))

    order = ["norm_mix", "norm_ffn", "norm_final", "attn_w_in", "attn_w_out", "attn_sinks", "hgrn_w_in",
             "hgrn_w_out", "hgrn_norm", "hgrn_lb_logits", "ffn_w_up", "ffn_conv_w", "ffn_conv_b", "ffn_w_down"]
    small_names = ["norm_mix", "norm_ffn", "norm_final", "attn_sinks", "hgrn_norm", "hgrn_lb_logits", "ffn_conv_b"]
    for i, name_ in enumerate(small_names):
        grads[name_], deltas[name_], new_m[name_], new_v[name_] = small_g[i], small_d[i], small_nm[i], small_nv[i]
    return (loss, grad_x, *[grads[n] for n in order], *[deltas[n] for n in order],
            *[new_m[n] for n in order], *[new_v[n] for n in order])
```

```python
import functools
import math

import numpy as np
import jax
import jax.numpy as jnp
from jax import lax
from jax.experimental import pallas as pl
from jax.experimental.pallas import tpu as pltpu

F32 = jnp.float32
BF16 = jnp.bfloat16
MESH = pl.DeviceIdType.MESH

HEAD_DIM = 64
Q_PER_KV = 4
WINDOW = 128
HG_EXPAND = 128
HG_CHUNK = 128
EPS = 1e-6
N_CHIPS = 4
N_DEV = 8

ADAM_LR = 0.001
ADAM_B1 = 0.9
ADAM_B2 = 0.999
ADAM_EPS = 1e-08
ADAM_WD = 0.01
ADAM_STEP = 10

VMEM_LIMIT_BYTES = 48 * 1024 * 1024
LANES = 128
SUBLANES = 8
HALO_ROWS = 16


def _params(sem=None):
    return pltpu.CompilerParams(dimension_semantics=sem, vmem_limit_bytes=VMEM_LIMIT_BYTES)


def _pick(n, cands):
    for c in cands:
        if n % c == 0:
            return c
    return n


def _dot(a, b, dims):
    return lax.dot_general(a.astype(BF16), b.astype(BF16), (dims, ((), ())),
                           preferred_element_type=F32)


NN = ((1,), (0,))
NT = ((1,), (1,))
TN = ((0,), (0,))


def mm_nn(a, w, l, *, name, res=None, out_dtype=F32, split_out=False, comm=None):
    M, K = a.shape
    N = w.shape[2]
    tk = K if K <= 2816 else _pick(K, (1024, 1408, 512, 256, 128))
    tm = _pick(M, (1024, 512, 256, 128) if tk <= 1408 else (512, 256, 128))
    wide = (2816, 2048, 1536) if tk <= 1024 and res is None else ()
    tn = _pick(N // 2 if split_out else N, wide + (1408, 1024, 768, 512, 256, 128))
    nk = K // tk
    nh = (N // 2) // tn

    def body(*refs):
        if res is None:
            a_ref, w_ref, o_ref = refs[:3]
            r_ref = None
        else:
            a_ref, w_ref, r_ref, o_ref = refs[:4]
        part = _dot(a_ref[...], w_ref[...], NN)

        def finish(acc):
            if r_ref is not None:
                acc = acc + r_ref[...]
            o_ref[...] = acc.astype(out_dtype)

        if nk == 1:
            finish(part)
        else:
            acc_ref = refs[-1]
            k = pl.program_id(2)

            @pl.when(k == 0)
            def _():
                acc_ref[...] = part

            @pl.when(k > 0)
            def _():
                acc_ref[...] += part

            @pl.when(k == nk - 1)
            def _():
                finish(acc_ref[...])

    in_specs = [pl.BlockSpec((tm, tk), lambda n, m, k: (m, k)),
                pl.BlockSpec((None, tk, tn), lambda n, m, k: (l, k, n))]
    args = [a, w]
    if res is not None:
        in_specs.append(pl.BlockSpec((tm, tn), lambda n, m, k: (m, n)))
        args.append(res)
    if split_out:
        out_shape = jax.ShapeDtypeStruct((2, M, N // 2), out_dtype)
        out_spec = pl.BlockSpec((None, tm, tn), lambda n, m, k: (n // nh, m, n % nh))
    else:
        out_shape = jax.ShapeDtypeStruct((M, N), out_dtype)
        out_spec = pl.BlockSpec((tm, tn), lambda n, m, k: (m, n))
    outs, updated, _ = _carried_call(
        body, args, name=name, grid=(N // tn, M // tm, nk), in_specs=in_specs, out_specs=[out_spec],
        out_shape=[out_shape], scratch_shapes=[] if nk == 1 else [pltpu.VMEM((tm, tn), F32)], comm=comm)
    return outs[0] if comm is None else (outs[0], updated)


def mm_nt(a, w, l, *, name, split_in=False, out_dtype=F32, comm=None, norm=None):
    if split_in:
        _, M, Nh = a.shape
        N = 2 * Nh
    else:
        M, N = a.shape
        Nh = N
    K = w.shape[1]
    tm = _pick(M, (512, 256, 128) if norm is not None else (1024, 512, 256, 128))
    tko = _pick(K, (1024, 1408, 512, 256, 128))
    tr = _pick(Nh, (2816, 2048, 1536, 1408, 1024, 768, 512, 256, 128) if tm <= 512 else (1024, 1408, 768, 512, 256, 128))
    nr = N // tr
    nh = Nh // tr
    n_in, n_out = (5, 2) if norm is not None else (2, 1)
    assert norm is None or tko == K

    def body(*refs):
        a_ref, w_ref = refs[:2]
        o_ref = refs[n_in]
        part = _dot(a_ref[...], w_ref[...], NT)

        def finish(acc):
            if norm is None:
                o_ref[...] = acc.astype(out_dtype)
                return
            h_ref, g_ref, dres_ref = refs[2:5]
            dg_ref = refs[n_in + 1]
            _, vjp = jax.vjp(_rms, h_ref[...], g_ref[...])
            dx, dg = vjp(acc)
            o_ref[...] = dres_ref[...] + dx
            first = pl.program_id(1) == 0

            @pl.when(first)
            def _():
                dg_ref[...] = dg

            @pl.when(jnp.logical_not(first))
            def _():
                dg_ref[...] += dg

        if nr == 1:
            finish(part)
        else:
            acc_ref = refs[n_in + n_out]
            r = pl.program_id(2)

            @pl.when(r == 0)
            def _():
                acc_ref[...] = part

            @pl.when(r > 0)
            def _():
                acc_ref[...] += part

            @pl.when(r == nr - 1)
            def _():
                finish(acc_ref[...])

    if split_in:
        a_spec = pl.BlockSpec((None, tm, tr), lambda ko, m, r: (r // nh, m, r % nh))
    else:
        a_spec = pl.BlockSpec((tm, tr), lambda ko, m, r: (m, r))
    args = [a, w]
    in_specs = [a_spec, pl.BlockSpec((None, tko, tr), lambda ko, m, r: (l, ko, r))]
    out_specs = [pl.BlockSpec((tm, tko), lambda ko, m, r: (m, ko))]
    out_shape = [jax.ShapeDtypeStruct((M, K), out_dtype)]
    if norm is not None:
        h, g, l_norm, dres = norm
        row = pl.BlockSpec((tm, K), lambda ko, m, r: (m, 0))
        args += [h, g, dres]
        in_specs += [row, pl.BlockSpec((None, 1, K), lambda ko, m, r: (l_norm, 0, 0)), row]
        out_specs.append(pl.BlockSpec((1, K), lambda ko, m, r: (0, 0)))
        out_shape.append(jax.ShapeDtypeStruct((1, K), F32))
    outs, _, created = _carried_call(
        body, args, name=name, grid=(K // tko, M // tm, nr), in_specs=in_specs, out_specs=out_specs,
        out_shape=out_shape, scratch_shapes=[] if nr == 1 else [pltpu.VMEM((tm, tko), F32)], comm=comm)
    outs = list(outs) + ([] if comm is None else [created])
    return outs[0] if len(outs) == 1 else tuple(outs)


def mm_tn(a, b, *, name):
    M, K = a.shape
    N = b.shape[1]
    tm = _pick(M, (4096, 2048, 1024, 512, 256, 128))
    tko = _pick(K, (512, 256, 128))
    tn = _pick(N, (512, 256, 128))
    nm = M // tm

    def body(a_ref, b_ref, o_ref):
        part = _dot(a_ref[...], b_ref[...], TN)
        r = pl.program_id(2)

        @pl.when(r == 0)
        def _():
            o_ref[...] = part

        @pl.when(r > 0)
        def _():
            o_ref[...] += part

    return pl.pallas_call(
        body, name=name, grid=(K // tko, N // tn, nm),
        in_specs=[pl.BlockSpec((tm, tko), lambda ko, n, r: (r, ko)), pl.BlockSpec((tm, tn), lambda ko, n, r: (r, n))],
        out_specs=pl.BlockSpec((None, tko, tn), lambda ko, n, r: (0, ko, n)),
        out_shape=jax.ShapeDtypeStruct((1, K, N), F32),
        compiler_params=_params(("parallel", "parallel", "arbitrary")),
    )(a, b)


def mm_dw(at, b, *, name, split_b=False):
    K, M = at.shape
    if split_b:
        Nh = b.shape[2]
        N = 2 * Nh
    else:
        N = b.shape[1]
        Nh = N
    if b.dtype.itemsize == 2:
        tm = _pick(M, (4096, 2048, 1024, 512, 256, 128))
        tko = _pick(K, (1024, 704, 512, 256, 128))
        tn = _pick(Nh, (512, 256, 128))
    else:
        tm = _pick(M, (1024, 512, 256, 128))
        tko = _pick(K, (1024, 1408, 704, 512, 256, 128))
        tn = _pick(Nh, (1408, 1024, 768, 512, 256, 128))
    nm = M // tm
    nh = Nh // tn

    def body(a_ref, b_ref, o_ref):
        part = _dot(a_ref[...], b_ref[...], NN)
        r = pl.program_id(2)

        @pl.when(r == 0)
        def _():
            o_ref[...] = part

        @pl.when(r > 0)
        def _():
            o_ref[...] += part

    if split_b:
        b_spec = pl.BlockSpec((None, tm, tn), lambda ko, n, r: (n // nh, r, n % nh))
    else:
        b_spec = pl.BlockSpec((tm, tn), lambda ko, n, r: (r, n))
    return pl.pallas_call(
        body, name=name, grid=(K // tko, N // tn, nm),
        in_specs=[pl.BlockSpec((tko, tm), lambda ko, n, r: (ko, r)), b_spec],
        out_specs=pl.BlockSpec((None, tko, tn), lambda ko, n, r: (0, ko, n)),
        out_shape=jax.ShapeDtypeStruct((1, K, N), F32),
        compiler_params=_params(("parallel", "parallel", "arbitrary")),
    )(at, b)


def _rms(x, g):
    return x * lax.rsqrt(jnp.mean(x * x, axis=-1, keepdims=True) + EPS) * g


def rmsnorm_fwd(h, g, l, *, name):
    S, D = h.shape
    tm = _pick(S, (512, 256, 128))

    def body(h_ref, g_ref, o_ref, ot_ref):
        y = _rms(h_ref[...], g_ref[...])
        o_ref[...] = y.astype(BF16)
        ot_ref[...] = y.T.astype(BF16)

    return pl.pallas_call(
        body, name=name, grid=(S // tm,),
        in_specs=[pl.BlockSpec((tm, D), lambda m: (m, 0)),
                  pl.BlockSpec((None, 1, D), lambda m: (l, 0, 0))],
        out_specs=[pl.BlockSpec((tm, D), lambda m: (m, 0)), pl.BlockSpec((D, tm), lambda m: (0, m))],
        out_shape=[jax.ShapeDtypeStruct((S, D), BF16), jax.ShapeDtypeStruct((D, S), BF16)],
        compiler_params=_params(("parallel",)),
    )(h, g)


def loss_head(h, g, target, *, name):
    S, D = h.shape
    tm = _pick(S, (512, 256, 128))

    def body(h_ref, g_ref, t_ref, loss_ref, dh_ref, dg_ref):
        y, vjp = jax.vjp(_rms, h_ref[...], g_ref[...])
        err = y - t_ref[...]
        part = 0.5 * jnp.sum(jnp.sum(err * err, axis=-1, keepdims=True) / D, axis=0, keepdims=True)
        dx, dg = vjp(err / D)
        dh_ref[...] = dx

        @pl.when(pl.program_id(0) == 0)
        def _():
            dg_ref[...] = dg
            loss_ref[...] = part

        @pl.when(pl.program_id(0) > 0)
        def _():
            dg_ref[...] += dg
            loss_ref[...] += part

    row = pl.BlockSpec((tm, D), lambda m: (m, 0))
    return pl.pallas_call(
        body, name=name, grid=(S // tm,),
        in_specs=[row, pl.BlockSpec((1, D), lambda m: (0, 0)), row],
        out_specs=[pl.BlockSpec((1, 1), lambda m: (0, 0)), row, pl.BlockSpec((1, D), lambda m: (0, 0))],
        out_shape=[jax.ShapeDtypeStruct((1, 1), F32), jax.ShapeDtypeStruct((S, D), F32),
                   jax.ShapeDtypeStruct((1, D), F32)],
        compiler_params=_params(("arbitrary",)),
    )(h, g, target)


def _make_mm(dims_fwd, dims_da, dims_db, swap_da=False, swap_db=False):
    @jax.custom_vjp
    def mm(a, b):
        return _dot(a, b, dims_fwd)

    def fwd(a, b):
        return mm(a, b), (a, b)

    def bwd(resid, g):
        a, b = resid
        da = _dot(b, g, dims_da) if swap_da else _dot(g, b, dims_da)
        db = _dot(g, a, dims_db) if swap_db else _dot(a, g, dims_db)
        return da, db

    mm.defvjp(fwd, bwd)
    return mm


_mm = _make_mm(NN, NT, TN)
_mm_nt = _make_mm(NT, NN, TN, swap_db=True)
_mm_tn = _make_mm(TN, NT, NN, swap_da=True)


def _dot_hi(a, b):
    return jnp.dot(a, b, precision=lax.Precision.HIGHEST, preferred_element_type=F32)


def _sigmoid(x):
    return 0.5 * jnp.tanh(0.5 * x) + 0.5


def _silu(x):
    return x * _sigmoid(x)


def _row_slices(x, n):
    rows = x.shape[0] // n
    return tuple(x[i * rows:(i + 1) * rows] for i in range(n))


@functools.partial(jax.custom_vjp, nondiff_argnums=(1,))
def _split_rows(x, n):
    return _row_slices(x, n)


_split_rows.defvjp(lambda x, n: (_row_slices(x, n), None), lambda n, _, cts: (jnp.concatenate(cts, axis=0),))


def _attn_block(qs, kws, vws, sink, slope, first):
    s = jnp.concatenate([_mm_nt(q, kw) for q, kw in zip(qs, kws)], axis=0) * (HEAD_DIM ** -0.5)
    qi = lax.broadcasted_iota(jnp.int32, s.shape, 0) % WINDOW
    ki = lax.broadcasted_iota(jnp.int32, s.shape, 1)
    dist = qi + WINDOW - ki
    valid = (dist >= 0) & (dist < WINDOW) & (ki >= WINDOW * first)
    s = jnp.where(valid, s - slope * dist.astype(F32), -jnp.inf)
    m = lax.stop_gradient(jnp.maximum(jnp.max(s, axis=-1, keepdims=True), sink))
    e = jnp.exp(s - m)
    denom = jnp.sum(e, axis=-1, keepdims=True) + jnp.exp(sink - m)
    ps = _split_rows(e * (1.0 / denom), len(qs))
    return tuple(_mm(p, vw) for p, vw in zip(ps, vws))


def _attn_specs(Hkv, order):
    G = Q_PER_KV
    qspec = pl.BlockSpec((Hkv, G, WINDOW, HEAD_DIM), lambda n: (0, 0, order(n), 0))
    prev = pl.BlockSpec((Hkv, WINDOW, HEAD_DIM), lambda n: (0, jnp.maximum(order(n) - 1, 0), 0))
    cur = pl.BlockSpec((Hkv, WINDOW, HEAD_DIM), lambda n: (0, order(n), 0))
    col = pl.BlockSpec((Hkv, G * WINDOW, 1), lambda n: (0, 0, 0))
    return qspec, prev, cur, col


def attn_fwd(q, k, v, sink_col, slope_col, *, name, comm=None):
    Hkv, G, S, _ = q.shape
    qspec, prev, cur, col = _attn_specs(Hkv, lambda n: n)

    def body(q_ref, kp_ref, kc_ref, vp_ref, vc_ref, sink_ref, slope_ref, o_ref):
        first = (pl.program_id(0) == 0).astype(jnp.int32)
        qs = tuple(q_ref[h].reshape(G * WINDOW, HEAD_DIM) for h in range(Hkv))
        kws = tuple(jnp.concatenate([kp_ref[h], kc_ref[h]], axis=0) for h in range(Hkv))
        vws = tuple(jnp.concatenate([vp_ref[h], vc_ref[h]], axis=0) for h in range(Hkv))
        outs = _attn_block(qs, kws, vws, sink_ref[...].reshape(Hkv * G * WINDOW, 1),
                           slope_ref[...].reshape(Hkv * G * WINDOW, 1), first)
        o_ref[...] = jnp.stack([o.reshape(G, WINDOW, HEAD_DIM).astype(BF16) for o in outs])

    outs, updated, created = _carried_call(
        body, [q, k, k, v, v, sink_col, slope_col], name=name, grid=(S // WINDOW,),
        in_specs=[qspec, prev, cur, prev, cur, col, col], out_specs=[qspec],
        out_shape=[jax.ShapeDtypeStruct(q.shape, BF16)], comm=comm)
    return outs[0] if comm is None else (outs[0], updated, created)


def attn_bwd(q, k, v, sink_col, slope_col, do, *, name, comm=None):
    Hkv, G, S, _ = q.shape
    nb = S // WINDOW
    qspec, prev, cur, col = _attn_specs(Hkv, lambda n: nb - 1 - n)

    def body(q_ref, kp_ref, kc_ref, vp_ref, vc_ref, sink_ref, slope_ref, do_ref,
             dq_ref, dk_ref, dv_ref, dsink_ref, dk_carry, dv_carry):
        @pl.when(pl.program_id(0) == 0)
        def _():
            dk_carry[...] = jnp.zeros_like(dk_carry)
            dv_carry[...] = jnp.zeros_like(dv_carry)
            dsink_ref[...] = jnp.zeros_like(dsink_ref)

        first = (pl.program_id(0) == nb - 1).astype(jnp.int32)
        qs = tuple(q_ref[h].reshape(G * WINDOW, HEAD_DIM).astype(F32) for h in range(Hkv))
        kws = tuple(jnp.concatenate([kp_ref[h], kc_ref[h]], axis=0).astype(F32) for h in range(Hkv))
        vws = tuple(jnp.concatenate([vp_ref[h], vc_ref[h]], axis=0).astype(F32) for h in range(Hkv))
        fn = functools.partial(_attn_block, slope=slope_ref[...].reshape(Hkv * G * WINDOW, 1), first=first)
        _, vjp = jax.vjp(fn, qs, kws, vws, sink_ref[...].reshape(Hkv * G * WINDOW, 1))
        dqs, dkws, dvws, dsink = vjp(tuple(do_ref[h].reshape(G * WINDOW, HEAD_DIM).astype(F32) for h in range(Hkv)))
        dkw, dvw = jnp.stack(dkws), jnp.stack(dvws)
        dq_ref[...] = jnp.stack([dq.reshape(G, WINDOW, HEAD_DIM) for dq in dqs]).astype(BF16)
        dk_ref[...] = (dkw[:, WINDOW:] + dk_carry[...]).astype(BF16)
        dv_ref[...] = (dvw[:, WINDOW:] + dv_carry[...]).astype(BF16)
        dk_carry[...] = dkw[:, :WINDOW]
        dv_carry[...] = dvw[:, :WINDOW]
        dsink_ref[...] += dsink.reshape(Hkv, G * WINDOW, 1)

    kv_shape = jax.ShapeDtypeStruct(k.shape, BF16)
    outs, _, created = _carried_call(
        body, [q, k, k, v, v, sink_col, slope_col, do], name=name, grid=(nb,),
        in_specs=[qspec, prev, cur, prev, cur, col, col, qspec],
        out_specs=[qspec, cur, cur, col],
        out_shape=[jax.ShapeDtypeStruct(q.shape, BF16), kv_shape, kv_shape,
                   jax.ShapeDtypeStruct(sink_col.shape, F32)],
        scratch_shapes=[pltpu.VMEM((Hkv, WINDOW, HEAD_DIM), F32), pltpu.VMEM((Hkv, WINDOW, HEAD_DIM), F32)],
        comm=comm)
    return outs if comm is None else (*outs, created)


def _hgrn_consts():
    C = HG_CHUNK
    t = np.arange(C)[:, None]
    j = np.arange(C)[None, :]
    cq, ck, mk = [], [], []
    H = C // 2
    while H >= 1:
        start = (t // H) * H
        cq.append((j > start) & (j <= t))
        ck.append((j > t) & (j <= start + H))
        mk.append(((t // H) % 2 == 1) & ((j // H) == (t // H) - 1))
        H //= 2
    ltri = (j <= t)
    ops = np.concatenate([ltri] + cq + ck, axis=0).astype(np.float32)
    ops3 = np.concatenate([ops, ops, ops], axis=1)
    opst3 = np.concatenate([ops.T, ops.T, ops.T], axis=1)
    return (jnp.asarray(ops3, BF16), jnp.asarray(opst3, BF16), jnp.asarray(np.stack(mk).astype(np.float32)))


def _split3(x):
    hi = x.astype(BF16)
    r = x - hi.astype(F32)
    mid = r.astype(BF16)
    lo = (r - mid.astype(F32)).astype(BF16)
    return jnp.concatenate([hi, mid, lo], axis=0)


@jax.custom_vjp
def _seg_sums(g, ops3, opst3):
    C = g.shape[0]
    out = jnp.dot(ops3, _split3(g), preferred_element_type=F32)
    return tuple(out[i * C:(i + 1) * C] for i in range(ops3.shape[0] // C))


def _seg_sums_fwd(g, ops3, opst3):
    return _seg_sums(g, ops3, opst3), opst3


def _seg_sums_bwd(opst3, cts):
    ct = jnp.concatenate(cts, axis=0)
    return jnp.dot(opst3, _split3(ct), preferred_element_type=F32), None, None


_seg_sums.defvjp(_seg_sums_fwd, _seg_sums_bwd)


def _head_slices(x):
    return tuple(x[:, h * HG_EXPAND:(h + 1) * HG_EXPAND] for h in range(x.shape[1] // HG_EXPAND))


@jax.custom_vjp
def _split_heads(x):
    return _head_slices(x)


_split_heads.defvjp(lambda x: (_head_slices(x), None), lambda _, cts: (jnp.concatenate(cts, axis=1),))


@jax.custom_vjp
def _join_heads(xs):
    return jnp.concatenate(xs, axis=1)


_join_heads.defvjp(lambda xs: (jnp.concatenate(xs, axis=1), None), lambda _, ct: (_head_slices(ct),))


def _hgrn_chunk(qr, fr, iv, gr, sts, logits, ng, ops3, opst3, mk, layer):
    n_levels = mk.shape[0]
    depth, D = logits.shape
    e = jnp.exp(logits - lax.stop_gradient(jnp.max(logits, axis=0, keepdims=True)))
    sm = e / jnp.sum(e, axis=0, keepdims=True)
    row = lax.broadcasted_iota(jnp.int32, (depth, D), 0)
    lb = jnp.sum(jnp.where((row >= 1) & (row <= layer), sm, 0.0), axis=0, keepdims=True)

    q = _silu(qr)
    a = jnp.log(lb)
    c = jnp.log(1.0 - lb) - (jnp.maximum(-fr, 0.0) + jnp.log(1.0 + jnp.exp(-jnp.abs(fr))))
    mx = lax.stop_gradient(jnp.maximum(a, c))
    g = mx + jnp.log(jnp.exp(a - mx) + jnp.exp(c - mx))
    k = (1.0 - lb) * _sigmoid(-fr)

    sums = _seg_sums(g, ops3, opst3)
    b = sums[0]
    b_last = jnp.sum(g, axis=0, keepdims=True)

    ql = [_split_heads(q * jnp.exp(sums[1 + lv])) for lv in range(n_levels)]
    kl = [_split_heads(k * jnp.exp(sums[1 + n_levels + lv])) for lv in range(n_levels)]
    qb, kb = _split_heads(q * jnp.exp(b)), _split_heads(k * jnp.exp(b_last - b))
    qk, ivh, gate, decay = _split_heads(q * k), _split_heads(iv), _split_heads(_silu(gr)), _split_heads(jnp.exp(b_last))
    outs, new_sts = [], []
    for h, st in enumerate(sts):
        amat = None
        for lv in range(n_levels):
            part = _mm_nt(ql[lv][h], kl[lv][h]) * mk[lv]
            amat = part if amat is None else amat + part
        o = jnp.sum(qk[h], axis=-1, keepdims=True) * ivh[h] + _mm(amat, ivh[h]) + _mm_nt(qb[h], st)
        new_sts.append(st * decay[h] + _mm_tn(ivh[h], kb[h]))
        on = o * lax.rsqrt(jnp.mean(o * o, axis=-1, keepdims=True) + EPS) * ng
        outs.append(on * gate[h])
    return _join_heads(tuple(outs)), tuple(new_sts)


def _hgrn_in_specs(D, depth, order):
    C = HG_CHUNK
    nl = int(math.log2(C))
    n_ops = 1 + 2 * nl
    return [pl.BlockSpec((C, 4 * D), lambda c: (order(c), 0)),
            pl.BlockSpec((depth, D), lambda c: (0, 0)),
            pl.BlockSpec((1, HG_EXPAND), lambda c: (0, 0)),
            pl.BlockSpec((n_ops * C, 3 * C), lambda c: (0, 0)),
            pl.BlockSpec((C, 3 * n_ops * C), lambda c: (0, 0)),
            pl.BlockSpec((nl, C, C), lambda c: (0, 0, 0))]


def hgrn_fwd(proj, logits, ng, layer, *, name, comm=None):
    S = proj.shape[0]
    D = proj.shape[1] // 4
    H = D // HG_EXPAND
    C = HG_CHUNK
    nc = S // C
    consts = _hgrn_consts()

    def body(p_ref, lg_ref, ng_ref, ops3_ref, opst3_ref, mk_ref, o_ref, st_out_ref, st_ref):
        @pl.when(pl.program_id(0) == 0)
        def _():
            st_ref[...] = jnp.zeros_like(st_ref)

        st_out_ref[...] = st_ref[...]
        out, new_sts = _hgrn_chunk(*(p_ref[:, j * D:(j + 1) * D] for j in range(4)), tuple(st_ref[h] for h in range(H)),
                                   lg_ref[...], ng_ref[...], ops3_ref[...], opst3_ref[...], mk_ref[...], layer)
        o_ref[...] = out.astype(BF16)
        for h in range(H):
            st_ref[h] = new_sts[h]

    outs, updated, created = _carried_call(
        body, [proj, logits, ng, *consts], name=name, grid=(nc,),
        in_specs=_hgrn_in_specs(D, logits.shape[0], lambda c: c),
        out_specs=[pl.BlockSpec((C, D), lambda c: (c, 0)),
                   pl.BlockSpec((None, H, HG_EXPAND, HG_EXPAND), lambda c: (c, 0, 0, 0))],
        out_shape=[jax.ShapeDtypeStruct((S, D), BF16),
                   jax.ShapeDtypeStruct((nc, H, HG_EXPAND, HG_EXPAND), F32)],
        scratch_shapes=[pltpu.VMEM((H, HG_EXPAND, HG_EXPAND), F32)], comm=comm)
    return outs if comm is None else (*outs, updated, created)


def hgrn_bwd(proj, logits, ng, states, dout, layer, *, name, comm=None):
    S = proj.shape[0]
    D = proj.shape[1] // 4
    H = D // HG_EXPAND
    C = HG_CHUNK
    nc = S // C
    depth = logits.shape[0]
    consts = _hgrn_consts()
    rev = lambda c: nc - 1 - c

    def body(p_ref, lg_ref, ng_ref, ops3_ref, opst3_ref, mk_ref, st_in_ref, do_ref,
             dp_ref, dlg_ref, dng_ref, dst_ref):
        @pl.when(pl.program_id(0) == 0)
        def _():
            dst_ref[...] = jnp.zeros_like(dst_ref)
            dlg_ref[...] = jnp.zeros_like(dlg_ref)
            dng_ref[...] = jnp.zeros_like(dng_ref)

        fn = functools.partial(_hgrn_chunk, ops3=ops3_ref[...], opst3=opst3_ref[...], mk=mk_ref[...], layer=layer)
        _, vjp = jax.vjp(fn, *(p_ref[:, j * D:(j + 1) * D] for j in range(4)), tuple(st_in_ref[h] for h in range(H)),
                         lg_ref[...], ng_ref[...])
        *dproj, dsts, dlg, dng = vjp((do_ref[...], tuple(dst_ref[h] for h in range(H))))
        for j in range(4):
            dp_ref[:, j * D:(j + 1) * D] = dproj[j].astype(BF16)
        for h in range(H):
            dst_ref[h] = dsts[h]
        dlg_ref[...] += dlg
        dng_ref[...] += dng

    in_specs = _hgrn_in_specs(D, depth, rev)
    in_specs += [pl.BlockSpec((None, H, HG_EXPAND, HG_EXPAND), lambda c: (rev(c), 0, 0, 0)),
                 pl.BlockSpec((C, D), lambda c: (rev(c), 0))]
    outs, _, created = _carried_call(
        body, [proj, logits, ng, *consts, states, dout], name=name, grid=(nc,), in_specs=in_specs,
        out_specs=[pl.BlockSpec((C, 4 * D), lambda c: (rev(c), 0)),
                   pl.BlockSpec((depth, D), lambda c: (0, 0)), pl.BlockSpec((1, HG_EXPAND), lambda c: (0, 0))],
        out_shape=[jax.ShapeDtypeStruct((S, 4 * D), BF16), jax.ShapeDtypeStruct((depth, D), F32),
                   jax.ShapeDtypeStruct((1, HG_EXPAND), F32)],
        scratch_shapes=[pltpu.VMEM((H, HG_EXPAND, HG_EXPAND), F32)], comm=comm)
    return outs if comm is None else (*outs, created)


def _shift_down(u, prev, shift):
    rolled = pltpu.roll(u, shift, axis=0)
    rows = lax.broadcasted_iota(jnp.int32, prev.shape, 0)
    top = jnp.where(rows < shift, pltpu.roll(prev, shift, axis=0), rolled[:SUBLANES])
    return jnp.concatenate([top, rolled[SUBLANES:]], axis=0)


def _shift_up(x, nxt, shift):
    tm = x.shape[0]
    rolled = pltpu.roll(x, tm - shift, axis=0)
    rows = lax.broadcasted_iota(jnp.int32, nxt.shape, 0)
    bottom = jnp.where(rows >= SUBLANES - shift, pltpu.roll(nxt, SUBLANES - shift, axis=0), rolled[tm - SUBLANES:])
    return jnp.concatenate([rolled[:tm - SUBLANES], bottom], axis=0)


def _conv(u, prev, w_ref, b_ref, half):
    u1 = _shift_down(u, prev, 1)
    u2 = _shift_down(u, prev, 2)
    return b_ref[half] + w_ref[half, 0:1, :] * u2 + w_ref[half, 1:2, :] * u1 + w_ref[half, 2:3, :] * u


def _halo_rows(ref, keep, last):
    full = ref.astype(F32)
    return (full[HALO_ROWS - SUBLANES:] if last else full[:SUBLANES]) * keep


def _conv_specs(tm, tc, l):
    nprev = lambda m: jnp.maximum(m * (tm // HALO_ROWS) - 1, 0)
    tile = pl.BlockSpec((2, tm, tc), lambda j, m: (0, m, j))
    halo = pl.BlockSpec((2, HALO_ROWS, tc), lambda j, m: (0, nprev(m), j))
    wspec = pl.BlockSpec((None, 2, 3, tc), lambda j, m: (l, 0, 0, j))
    bspec = pl.BlockSpec((None, 2, 1, tc), lambda j, m: (l, 0, 0, j))
    return tile, halo, wspec, bspec


def convact_fwd(u, cw, cb, l, *, name, comm=None):
    _, S, Fh = u.shape
    tm = _pick(S, (512, 256, 128))
    tc = _pick(Fh, (1408, 512, 256, 128))
    tile, halo, wspec, bspec = _conv_specs(tm, tc, l)

    def body(u_ref, p_ref, w_ref, b_ref, o_ref, ot_ref, c_ref):
        keep = (pl.program_id(1) > 0).astype(F32)
        cg = _conv(u_ref[0].astype(F32), _halo_rows(p_ref[0], keep, True), w_ref, b_ref, 0)
        cv = _conv(u_ref[1].astype(F32), _halo_rows(p_ref[1], keep, True), w_ref, b_ref, 1)
        act = _silu(cg) * cv
        o_ref[...] = act.astype(BF16)
        ot_ref[...] = act.T.astype(BF16)
        c_ref[0] = cg.astype(BF16)
        c_ref[1] = cv.astype(BF16)

    outs, updated, _ = _carried_call(
        body, [u, u, cw, cb], name=name, grid=(Fh // tc, S // tm), in_specs=[tile, halo, wspec, bspec],
        out_specs=[pl.BlockSpec((tm, tc), lambda j, m: (m, j)), pl.BlockSpec((tc, tm), lambda j, m: (j, m)), tile],
        out_shape=[jax.ShapeDtypeStruct((S, Fh), BF16), jax.ShapeDtypeStruct((Fh, S), BF16),
                   jax.ShapeDtypeStruct(u.shape, BF16)], comm=comm)
    return tuple(outs) if comm is None else (*outs, updated)


def convact_bwd(c, u, cw, l, da, *, name):
    _, S, Fh = u.shape
    tm = _pick(S, (512, 256, 128))
    tc = _pick(Fh, (1408, 512, 256, 128))
    nm = S // tm

    def body(c_ref, u_ref, w_ref, da_ref, du_ref, dw_ref, db_ref, carry):
        @pl.when(pl.program_id(1) == 0)
        def _():
            carry[...] = jnp.zeros_like(carry)
            dw_ref[...] = jnp.zeros_like(dw_ref)
            db_ref[...] = jnp.zeros_like(db_ref)

        cg, cv = c_ref[0].astype(F32), c_ref[1].astype(F32)
        da_t = da_ref[...]
        sg = _sigmoid(cg)
        dcv = da_t * (cg * sg)
        dcg = da_t * cv * (sg * (1.0 + cg * (1.0 - sg)))
        rs = lambda t: jnp.sum(t, axis=0, keepdims=True)
        for half, dc in ((0, dcg), (1, dcv)):
            nxt = carry[half]
            d1, d2 = _shift_up(dc, nxt, 1), _shift_up(dc, nxt, 2)
            du = w_ref[half, 2:3, :] * dc + w_ref[half, 1:2, :] * d1 + w_ref[half, 0:1, :] * d2
            du_ref[half] = du.astype(BF16)
            uh = u_ref[half].astype(F32)
            for j, d in enumerate((d2, d1, dc)):
                dw_ref[half, j:j + 1, :] += rs(d * uh)
            db_ref[half] += rs(dc)
            carry[half] = dc[:SUBLANES]

    tile = pl.BlockSpec((2, tm, tc), lambda j, m: (0, nm - 1 - m, j))
    return pl.pallas_call(
        body, name=name, grid=(Fh // tc, nm),
        in_specs=[tile, tile, pl.BlockSpec((None, 2, 3, tc), lambda j, m: (l, 0, 0, j)),
                  pl.BlockSpec((tm, tc), lambda j, m: (nm - 1 - m, j))],
        out_specs=[tile, pl.BlockSpec((2, 3, tc), lambda j, m: (0, 0, j)),
                   pl.BlockSpec((2, 1, tc), lambda j, m: (0, 0, j))],
        out_shape=[jax.ShapeDtypeStruct(u.shape, BF16), jax.ShapeDtypeStruct((2, 3, Fh), F32),
                   jax.ShapeDtypeStruct((2, 1, Fh), F32)],
        scratch_shapes=[pltpu.VMEM((2, SUBLANES, tc), F32)],
        compiler_params=_params(("parallel", "arbitrary")),
    )(c, u, cw, da)


def _adamw_math(w, g, m, v):
    m = ADAM_B1 * m + (1.0 - ADAM_B1) * g
    v = ADAM_B2 * v + (1.0 - ADAM_B2) * (g * g)
    m_hat = m / (1.0 - ADAM_B1 ** ADAM_STEP)
    v_hat = v / (1.0 - ADAM_B2 ** ADAM_STEP)
    delta = -ADAM_LR * (m_hat / (jnp.sqrt(v_hat) + ADAM_EPS) + ADAM_WD * w)
    return delta, m, v


def _row_tile(R, Ccols, n_streams):
    for tr in (512, 256, 128, 64, 32, 16, 8):
        if R % tr == 0 and tr * Ccols * 4 * n_streams * 2 <= VMEM_LIMIT_BYTES // 2:
            return tr
    return R


def adamw(w, g, m, v, *, name):
    R, Ccols = w.shape
    tr = _row_tile(R, Ccols, 7)

    def body(w_ref, g_ref, m_ref, v_ref, d_ref, mo_ref, vo_ref):
        d, mn, vn = _adamw_math(w_ref[...], g_ref[...], m_ref[...], v_ref[...])
        d_ref[...] = d
        mo_ref[...] = mn
        vo_ref[...] = vn

    spec = pl.BlockSpec((tr, Ccols), lambda i: (i, 0))
    shp = jax.ShapeDtypeStruct((R, Ccols), F32)
    return pl.pallas_call(
        body, name=name, grid=(R // tr,), in_specs=[spec] * 4, out_specs=[spec] * 3,
        out_shape=[shp, shp, shp], compiler_params=_params(("parallel",)),
    )(w, g, m, v)


def _prefetch_call(body, pos, args, *, name, grid, in_specs, out_specs, out_shape, sem, aliases=None):
    return pl.pallas_call(
        lambda pos_ref, *refs: body(*refs), name=name,
        grid_spec=pltpu.PrefetchScalarGridSpec(num_scalar_prefetch=1, grid=grid, in_specs=in_specs,
                                               out_specs=out_specs),
        out_shape=out_shape, input_output_aliases={1 + i: o for i, o in (aliases or {}).items()},
        compiler_params=_params(sem),
    )(pos, *args)


def cast_place(w, pos, *, row_sharded, name):
    L, R, Ccols = w.shape

    def body(w_ref, o_ref):
        o_ref[...] = w_ref[...].astype(BF16)

    if row_sharded:
        grid = (L,)
        in_spec = pl.BlockSpec((None, R, Ccols), lambda l, pos: (l, 0, 0))
        out_spec = pl.BlockSpec((None, R, Ccols), lambda l, pos: (l, pos[0], 0))
        out_shape = jax.ShapeDtypeStruct((L, N_CHIPS * R, Ccols), BF16)
        sem = ("parallel",)
    else:
        tr = _pick(R, (512, 256, 128))
        grid = (L, R // tr)
        in_spec = pl.BlockSpec((None, tr, Ccols), lambda l, i, pos: (l, i, 0))
        out_spec = pl.BlockSpec((None, tr, Ccols), lambda l, i, pos: (l, i, pos[0]))
        out_shape = jax.ShapeDtypeStruct((L, R, N_CHIPS * Ccols), BF16)
        sem = ("parallel", "parallel")
    return _prefetch_call(body, pos, [w], name=name, grid=grid, in_specs=[in_spec], out_specs=out_spec,
                          out_shape=out_shape, sem=sem)


def pair_add(g, theirs, pos, *, row_sharded, name):
    def body(g_ref, t_ref, o_ref, ob_ref):
        s = g_ref[...] + t_ref[...]
        o_ref[...] = s
        ob_ref[...] = s.astype(BF16)

    if row_sharded:
        L, _, _, r2, D = g.shape
        grid = (L, N_CHIPS)
        g_spec = pl.BlockSpec((None, None, None, r2, D), lambda l, q, pos: (l, q, pos[1], 0, 0))
        t_spec = pl.BlockSpec((None, None, r2, D), lambda l, q, pos: (l, q, 0, 0))
    else:
        L, K, N = g.shape
        n = N // N_CHIPS
        tr = _pick(K // 2, (256, 128))
        nb = (K // 2) // tr
        grid = (L, nb, N_CHIPS)
        g_spec = pl.BlockSpec((None, tr, n), lambda l, i, j, pos: (l, pos[1] * nb + i, j))
        t_spec = pl.BlockSpec((None, tr, n), lambda l, i, j, pos: (l, i, j))
    return _prefetch_call(body, pos, [g, theirs], name=name, grid=grid, in_specs=[g_spec, t_spec],
                          out_specs=[t_spec, t_spec],
                          out_shape=[jax.ShapeDtypeStruct(theirs.shape, F32), jax.ShapeDtypeStruct(theirs.shape, BF16)],
                          sem=("parallel",) * len(grid))


def chip_add(pair, landed, pos, buf, li, n_layers, *, row_sharded, name):
    def body(p_ref, a_ref, b_ref, c_ref, *rest):
        rest[-1][...] = ((p_ref[...] + a_ref[...].astype(F32)) + b_ref[...].astype(F32)) + c_ref[...].astype(F32)

    if row_sharded:
        _, _, r2, D = pair.shape
        grid = (1,)
        p_spec = pl.BlockSpec((None, None, r2, D), lambda i, pos: (0, pos[0], 0, 0))
        l_specs = [pl.BlockSpec((None, r2, D), functools.partial(lambda i, pos, k: (k, 0, 0), k=k)) for k in range(3)]
        out_spec = pl.BlockSpec((None, None, r2, D), lambda i, pos: (li, pos[1], 0, 0))
        out_shape = jax.ShapeDtypeStruct((n_layers, 2, r2, D), F32)
    else:
        _, K2, N = pair.shape
        n = N // N_CHIPS
        tr = _pick(K2, (256, 128))
        nb = K2 // tr
        grid = (nb,)
        p_spec = pl.BlockSpec((None, tr, n), lambda i, pos: (0, i, pos[0]))
        l_specs = [pl.BlockSpec((None, tr, n), functools.partial(lambda i, pos, k: (k, i, 0), k=k)) for k in range(3)]
        out_spec = pl.BlockSpec((None, tr, n), lambda i, pos: (li, pos[1] * nb + i, 0))
        out_shape = jax.ShapeDtypeStruct((n_layers, 2 * K2, n), F32)
    args, in_specs, aliases = [pair, landed, landed, landed], [p_spec] + l_specs, None
    if buf is not None:
        args.append(buf)
        in_specs.append(pl.BlockSpec(memory_space=pl.ANY))
        aliases = {4: 0}
    return _prefetch_call(body, pos, args, name=name, grid=grid, in_specs=in_specs, out_specs=out_spec,
                          out_shape=out_shape, sem=("parallel",), aliases=aliases)


def _position():
    return lax.axis_index("x"), lax.axis_index("y"), lax.axis_index("c")


def _other_chips(x, y):
    return [(1 - x, y), (x, 1 - y), (1 - x, 1 - y)]


HBM_ANY = pl.BlockSpec(memory_space=pl.ANY)
COMM_PARAMS = pltpu.CompilerParams(has_side_effects=True)


class Comm:
    def __init__(self, ro=(), inout=(), new=(), sems=(), phases=()):
        self.ro, self.inout, self.new, self.sems, self.phases = list(ro), list(inout), list(new), list(sems), list(phases)


def _carried_call(body, args, *, name, grid, in_specs, out_specs, out_shape, scratch_shapes=(), comm=None):
    comm = comm or Comm()
    n_in, n_out, n_sc = len(in_specs), len(out_shape), len(scratch_shapes)
    n_ro, n_io, n_new, n_sem = len(comm.ro), len(comm.inout), len(comm.new), len(comm.sems)
    last = int(np.prod(grid)) - 1

    def wrapped(*refs):
        bounds = np.cumsum([0, n_in, n_ro, n_io, n_out, n_io, n_new, n_sc, n_sem])
        core_in, ro, _, core_out, io, new, core_sc, sems = (refs[a:b] for a, b in zip(bounds[:-1], bounds[1:]))
        step = pl.program_id(0)
        for d in range(1, len(grid)):
            step = step * grid[d] + pl.program_id(d)
        for frac, fn in comm.phases[:-1]:
            pl.when(step == int(frac * last))(functools.partial(fn, ro, io, new, sems))
        body(*core_in, *core_out, *core_sc)
        if comm.phases:
            pl.when(step == last)(functools.partial(comm.phases[-1][1], ro, io, new, sems))

    outs = pl.pallas_call(
        wrapped, name=name, grid=grid,
        in_specs=list(in_specs) + [HBM_ANY] * (n_ro + n_io),
        out_specs=list(out_specs) + [HBM_ANY] * (n_io + n_new),
        out_shape=list(out_shape) + [jax.ShapeDtypeStruct(a.shape, a.dtype) for a in comm.inout] + comm.new,
        input_output_aliases={n_in + n_ro + j: n_out + j for j in range(n_io)},
        scratch_shapes=list(scratch_shapes) + comm.sems,
        compiler_params=pltpu.CompilerParams(dimension_semantics=("arbitrary",) * len(grid),
                                             vmem_limit_bytes=VMEM_LIMIT_BYTES, has_side_effects=bool(comm.phases)),
    )(*args, *comm.ro, *comm.inout)
    return outs[:n_out], outs[n_out:n_out + n_io], outs[n_out + n_io:]


def run_comm(comm, *, name):
    n_ro, n_io, n_new = len(comm.ro), len(comm.inout), len(comm.new)

    def body(*refs):
        bounds = np.cumsum([0, n_ro, n_io, n_io, n_new, len(comm.sems)])
        ro, _, io, new, sems = (refs[a:b] for a, b in zip(bounds[:-1], bounds[1:]))
        for _, fn in comm.phases:
            fn(ro, io, new, sems)

    outs = pl.pallas_call(
        body, name=name, in_specs=[HBM_ANY] * (n_ro + n_io), out_specs=[HBM_ANY] * (n_io + n_new),
        out_shape=[jax.ShapeDtypeStruct(a.shape, a.dtype) for a in comm.inout] + comm.new,
        input_output_aliases={n_ro + j: j for j in range(n_io)},
        scratch_shapes=comm.sems, compiler_params=COMM_PARAMS,
    )(*comm.ro, *comm.inout)
    return outs[:n_io], outs[n_io:]


def _rdma(src, dst, send_sem, recv_sem, device):
    return pltpu.make_async_remote_copy(src_ref=src, dst_ref=dst, send_sem=send_sem, recv_sem=recv_sem,
                                        device_id=device, device_id_type=MESH)


def gather_comm(bufs, slices):
    n = len(slices)

    def region(io, s, h, chip):
        b, li, row_sharded = slices[s]
        ref = io[b]
        if row_sharded:
            r = ref.shape[1] // N_CHIPS
            return ref.at[li, pl.ds(chip * r + h * (r // 2), r // 2), :]
        K, nn = ref.shape[1], ref.shape[2] // N_CHIPS
        return ref.at[li, pl.ds(h * (K // 2), K // 2), pl.ds(chip * nn, nn)]

    def copies(io, sems):
        x, y, c = _position()
        me = 2 * x + y
        chips = _other_chips(x, y)
        ids = [2 * cx + cy for cx, cy in chips]

        def ici(s, k, src):
            return _rdma(region(io, s, c, me), region(io, s, c, src), sems[0].at[s, k], sems[1].at[s, k], (*chips[k], c))

        def d2d(s, k, h):
            return _rdma(region(io, s, h, ids[k]), region(io, s, h, ids[k]), sems[2].at[s, k], sems[3].at[s, k],
                         (x, y, 1 - c))

        return me, ids, c, ici, d2d

    pairs = [(s, k) for s in range(n) for k in range(3)]

    def start(ro, io, new, sems):
        me, _, _, ici, _ = copies(io, sems)
        for s, k in pairs:
            ici(s, k, me).start()

    def pass_on(ro, io, new, sems):
        _, ids, c, ici, d2d = copies(io, sems)
        for s, k in pairs:
            ici(s, k, ids[k]).wait_recv()
            d2d(s, k, c).start()

    def finish(ro, io, new, sems):
        me, _, c, ici, d2d = copies(io, sems)
        for s, k in pairs:
            d2d(s, k, 1 - c).wait_recv()
        for s, k in pairs:
            ici(s, k, me).wait_send()
            d2d(s, k, c).wait_send()

    return Comm(inout=bufs, sems=[pltpu.SemaphoreType.DMA((n, 3))] * 4,
                phases=[(0.0, start), (0.85, pass_on), (1.0, finish)])


def gather_small(shard, *, name):
    n = shard.shape[2]

    def body(s_ref, o_ref, send_sem, recv_sem, local_sem):
        x, y, c = _position()
        me = 2 * x + y
        chips = _other_chips(x, y)
        place = lambda chip: o_ref.at[:, :, pl.ds(chip * n, n)]
        local = pltpu.make_async_copy(s_ref, place(me), local_sem)
        local.start()
        for k in range(3):
            _rdma(s_ref, place(me), send_sem.at[k], recv_sem.at[k], (*chips[k], c)).start()
        for k in range(3):
            _rdma(s_ref, place(2 * chips[k][0] + chips[k][1]), send_sem.at[k], recv_sem.at[k], (*chips[k], c)).wait()
        local.wait()

    return pl.pallas_call(
        body, name=name, in_specs=[HBM_ANY], out_specs=HBM_ANY,
        out_shape=jax.ShapeDtypeStruct((shard.shape[0], shard.shape[1], N_CHIPS * n), shard.dtype),
        scratch_shapes=[pltpu.SemaphoreType.DMA((3,)), pltpu.SemaphoreType.DMA((3,)), pltpu.SemaphoreType.DMA],
        compiler_params=COMM_PARAMS,
    )(shard)


def swap_comm(grads):
    def copies(ro, new, sems):
        x, y, c = _position()
        h = 1 - c
        cps = []
        for s, (a, row_sharded) in enumerate(grads):
            if row_sharded:
                r = a.shape[1] // N_CHIPS
                for q in range(N_CHIPS):
                    cps.append(_rdma(ro[s].at[:, pl.ds(q * r + h * (r // 2), r // 2), :], new[s].at[:, q],
                                     sems[0].at[s, q], sems[1].at[s, q], (x, y, 1 - c)))
            else:
                K = a.shape[1]
                cps.append(_rdma(ro[s].at[:, pl.ds(h * (K // 2), K // 2), :], new[s],
                                 sems[0].at[s, 0], sems[1].at[s, 0], (x, y, 1 - c)))
        return cps

    def start(ro, io, new, sems):
        for cp in copies(ro, new, sems):
            cp.start()

    def finish(ro, io, new, sems):
        for cp in copies(ro, new, sems):
            cp.wait()

    theirs = []
    for a, row_sharded in grads:
        if row_sharded:
            theirs.append(jax.ShapeDtypeStruct((1, N_CHIPS, a.shape[1] // N_CHIPS // 2, a.shape[2]), a.dtype))
        else:
            theirs.append(jax.ShapeDtypeStruct((1, a.shape[1] // 2, a.shape[2]), a.dtype))
    return Comm(ro=[a for a, _ in grads], new=theirs, sems=[pltpu.SemaphoreType.DMA((len(grads), N_CHIPS))] * 2,
                phases=[(0.0, start), (1.0, finish)])


def merge_comms(comms):
    spans, ro, io, new, sems = [], [], [], [], []
    for cm in comms:
        spans.append((len(ro), len(io), len(new), len(sems)))
        ro, io, new, sems = ro + cm.ro, io + cm.inout, new + cm.new, sems + cm.sems

    def phase(frac):
        def run(R, I, N, S):
            for cm, (a, b, d, e) in zip(comms, spans):
                for f, fn in cm.phases:
                    if f == frac:
                        fn(R[a:a + len(cm.ro)], I[b:b + len(cm.inout)], N[d:d + len(cm.new)], S[e:e + len(cm.sems)])
        return run

    fracs = sorted({f for cm in comms for f, _ in cm.phases})
    merged = Comm(ro=ro, inout=io, new=new, sems=sems, phases=[(f, phase(f)) for f in fracs])
    cut = lambda created: [created[d:d + len(cm.new)] for cm, (_, _, d, _) in zip(comms, spans)]
    return merged, cut


def scatter_comm(halves):
    def copies(ro, new, sems):
        x, y, c = _position()
        chips = _other_chips(x, y)
        cps = []
        for s, (a, row_sharded) in enumerate(halves):
            for k in range(3):
                to = 2 * chips[k][0] + chips[k][1]
                if row_sharded:
                    src = ro[s].at[0, to]
                else:
                    n = a.shape[2] // N_CHIPS
                    src = ro[s].at[0, :, pl.ds(to * n, n)]
                cps.append(_rdma(src, new[s].at[k], sems[0].at[s, k], sems[1].at[s, k], (*chips[k], c)))
        return cps

    def start(ro, io, new, sems):
        for cp in copies(ro, new, sems):
            cp.start()

    def finish(ro, io, new, sems):
        for cp in copies(ro, new, sems):
            cp.wait()

    landing = [jax.ShapeDtypeStruct((3, a.shape[2], a.shape[3]) if row_sharded else
                                    (3, a.shape[1], a.shape[2] // N_CHIPS), a.dtype) for a, row_sharded in halves]
    return Comm(ro=[a for a, _ in halves], new=landing, sems=[pltpu.SemaphoreType.DMA((len(halves), 3))] * 2,
                phases=[(0.0, start), (1.0, finish)])


def sibling_join_halves(shards, *, name):
    n_in = len(shards)

    def body(*refs):
        ins = refs[:n_in]
        outs = refs[n_in:2 * n_in]
        send_sem, recv_sem = refs[2 * n_in:]
        x, y, c = _position()
        sibling = (x, y, 1 - c)

        def rows(ref, h):
            R2 = ref.shape[1] // 2
            return ref.at[:, pl.ds(h * R2, R2), :]

        sends = [pltpu.make_async_remote_copy(src_ref=rows(ins[i], c), dst_ref=rows(outs[i], c),
                                              send_sem=send_sem.at[i], recv_sem=recv_sem.at[i],
                                              device_id=sibling, device_id_type=MESH) for i in range(n_in)]
        lands = [pltpu.make_async_remote_copy(src_ref=rows(ins[i], c), dst_ref=rows(outs[i], 1 - c),
                                              send_sem=send_sem.at[i], recv_sem=recv_sem.at[i],
                                              device_id=sibling, device_id_type=MESH) for i in range(n_in)]
        for cp in sends:
            cp.start()
        for i in range(n_in):
            sends[i].wait_send()
            lands[i].wait_recv()

    return pl.pallas_call(
        body, name=name, in_specs=[HBM_ANY] * n_in, out_specs=[HBM_ANY] * n_in,
        out_shape=[jax.ShapeDtypeStruct(a.shape, a.dtype) for a in shards],
        input_output_aliases={i: i for i in range(n_in)},
        scratch_shapes=[pltpu.SemaphoreType.DMA((n_in,)), pltpu.SemaphoreType.DMA((n_in,))],
        compiler_params=COMM_PARAMS,
    )(*shards)


def all_reduce_small(v, *, name):
    R, Ccols = v.shape

    def body(v_ref, o_ref, slots, send_sem, recv_sem):
        x, y, c = _position()
        me = 4 * x + 2 * y + c
        slots[me] = v_ref[...]

        def peer(d):
            return x ^ (d >> 2), y ^ ((d >> 1) & 1), c ^ (d & 1)

        def copy(d, slot):
            return pltpu.make_async_remote_copy(
                src_ref=v_ref, dst_ref=slots.at[slot], send_sem=send_sem.at[d], recv_sem=recv_sem.at[d],
                device_id=peer(d), device_id_type=MESH)

        sends = [copy(d, me) for d in range(1, N_DEV)]
        for cp in sends:
            cp.start()
        for d in range(1, N_DEV):
            px, py, pc = peer(d)
            copy(d, 4 * px + 2 * py + pc).wait_recv()
        for cp in sends:
            cp.wait_send()
        acc = slots[0]
        for d in range(1, N_DEV):
            acc = acc + slots[d]
        o_ref[...] = acc

    return pl.pallas_call(
        body, name=name,
        in_specs=[pl.BlockSpec(memory_space=pltpu.VMEM)], out_specs=pl.BlockSpec(memory_space=pltpu.VMEM),
        out_shape=jax.ShapeDtypeStruct((R, Ccols), F32),
        scratch_shapes=[pltpu.VMEM((N_DEV, R, Ccols), F32), pltpu.SemaphoreType.DMA((N_DEV,)),
                        pltpu.SemaphoreType.DMA((N_DEV,))],
        compiler_params=COMM_PARAMS,
    )(v)


def _pack_small(parts):
    flat = jnp.concatenate([p.reshape(-1).astype(F32) for p in parts])
    n = flat.shape[0]
    rows = -(-n // LANES)
    rows = -(-rows // SUBLANES) * SUBLANES
    return jnp.pad(flat, (0, rows * LANES - n)).reshape(rows, LANES)


def _unpack_small(packed, like):
    flat = packed.reshape(-1)
    out, off = [], 0
    for p in like:
        out.append(flat[off:off + p.size].reshape(p.shape))
        off += p.size
    return out


def kernel(x, norm_mix, norm_ffn, norm_final, attn_w_in, attn_w_out, attn_sinks, hgrn_w_in, hgrn_w_out, hgrn_norm, hgrn_lb_logits, ffn_w_up, ffn_conv_w, ffn_conv_b, ffn_w_down, loss_target, m_norm_mix, m_norm_ffn, m_norm_final, m_attn_w_in, m_attn_w_out, m_attn_sinks, m_hgrn_w_in, m_hgrn_w_out, m_hgrn_norm, m_hgrn_lb_logits, m_ffn_w_up, m_ffn_conv_w, m_ffn_conv_b, m_ffn_w_down, v_norm_mix, v_norm_ffn, v_norm_final, v_attn_w_in, v_attn_w_out, v_attn_sinks, v_hgrn_w_in, v_hgrn_w_out, v_hgrn_norm, v_hgrn_lb_logits, v_ffn_w_up, v_ffn_conv_w, v_ffn_conv_b, v_ffn_w_down):
    S, D = x.shape[1], x.shape[2]
    depth = norm_mix.shape[0]
    n_q = D // HEAD_DIM
    n_kv = n_q // Q_PER_KV
    Fh = ffn_w_down.shape[1] * N_CHIPS
    h = x.reshape(S, D)
    target = loss_target.reshape(S, D)

    col_names = ["attn_w_in", "hgrn_w_in", "ffn_w_up"]
    row_names = ["attn_w_out", "hgrn_w_out", "ffn_w_down"]
    x_, y_, c_ = _position()
    me = 2 * x_ + y_
    pos = jnp.stack([me, c_]).astype(jnp.int32)
    col_w = [cast_place(w, pos, row_sharded=False, name=f"cast_place_{n}")
             for n, w in zip(col_names, (attn_w_in, hgrn_w_in, ffn_w_up))]
    row_w = [cast_place(w, pos, row_sharded=True, name=f"cast_place_{n}")
             for n, w in zip(row_names, (attn_w_out, hgrn_w_out, ffn_w_down))]
    W = dict(zip(col_names + row_names, col_w + row_w))
    is_row = {n: n in row_names for n in W}

    def layer_weights(layer):
        mixer = "attn" if layer % 2 == 0 else "hgrn"
        return [(f"{mixer}_w_in", layer // 2), (f"{mixer}_w_out", layer // 2), ("ffn_w_up", layer), ("ffn_w_down", layer)]

    def gather_of(group):
        names = [n for n, _ in group]
        return names, gather_comm([W[n] for n in names], [(i, li, is_row[n]) for i, (n, li) in enumerate(group)])

    def mixer_weights(layer):
        return layer_weights(layer)[:2] if layer < depth else []

    names_next, comm_next = gather_of(mixer_weights(0))
    filled, _ = run_comm(comm_next, name="gather_mixer0")
    W.update(zip(names_next, filled))
    cw = gather_small(ffn_conv_w, name="gather_conv_w").reshape(depth, 3, 2, Fh).transpose(0, 2, 1, 3)
    cb = ffn_conv_b.reshape(depth, 2, 1, Fh)
    g_mix = norm_mix.reshape(depth, 1, D)
    g_ffn = norm_ffn.reshape(depth, 1, D)

    slopes = jnp.exp2(-8.0 * jnp.arange(1, n_q + 1, dtype=F32) / n_q).reshape(n_kv, Q_PER_KV)
    slope_col = jnp.repeat(slopes, WINDOW, axis=1).reshape(n_kv, Q_PER_KV * WINDOW, 1)

    def to_heads(t, nh):
        return t.reshape(S, nh, HEAD_DIM).transpose(1, 0, 2)

    def from_heads(t):
        return t.transpose(1, 0, 2).reshape(S, -1)

    saved = []
    for layer in range(depth):
        idx = layer // 2
        rec = {"h_mix": h}
        hn, rec["hn_t"] = rmsnorm_fwd(h, g_mix, layer, name=f"norm_mix_fwd{layer}")
        is_attn = layer % 2 == 0
        up_l, down_l = layer_weights(layer)[2:]
        on_mixer = [up_l] if is_attn else [up_l, down_l]
        on_up = [down_l] if is_attn else mixer_weights(layer + 1)
        on_act = mixer_weights(layer + 1)[:1] if is_attn else []
        on_down = mixer_weights(layer + 1)[1:] if is_attn else []
        names_next, comm_next = gather_of(on_mixer)
        if is_attn:
            proj = mm_nn(hn, W["attn_w_in"], idx, name=f"attn_in_fwd{layer}", out_dtype=BF16)
            q = to_heads(proj[:, :n_q * HEAD_DIM], n_q).reshape(n_kv, Q_PER_KV, S, HEAD_DIM)
            k = to_heads(proj[:, n_q * HEAD_DIM:(n_q + n_kv) * HEAD_DIM], n_kv)
            v = to_heads(proj[:, (n_q + n_kv) * HEAD_DIM:], n_kv)
            sink_col = jnp.repeat(attn_sinks[idx].reshape(n_kv, Q_PER_KV), WINDOW, axis=1)
            sink_col = sink_col.reshape(n_kv, Q_PER_KV * WINDOW, 1)
            o_heads, filled, _ = attn_fwd(q, k, v, sink_col, slope_col, name=f"attn_fwd{layer}", comm=comm_next)
            W.update(zip(names_next, filled))
            o = from_heads(o_heads.reshape(n_q, S, HEAD_DIM))
            rec.update(q=q, k=k, v=v, sink_col=sink_col, o=o)
            h = mm_nn(o, W["attn_w_out"], idx, name=f"attn_out_fwd{layer}", res=h)
        else:
            proj = mm_nn(hn, W["hgrn_w_in"], idx, name=f"hgrn_in_fwd{layer}")
            ng = hgrn_norm[idx].reshape(1, HG_EXPAND)
            o, states, filled, _ = hgrn_fwd(proj, hgrn_lb_logits, ng, layer, name=f"hgrn_fwd{layer}", comm=comm_next)
            W.update(zip(names_next, filled))
            rec.update(proj=proj, ng=ng, states=states, o=o)
            h = mm_nn(o, W["hgrn_w_out"], idx, name=f"hgrn_out_fwd{layer}", res=h)
        rec["h_ffn"] = h
        hn2, rec["hn2_t"] = rmsnorm_fwd(h, g_ffn, layer, name=f"norm_ffn_fwd{layer}")
        if on_up:
            names_next, comm_next = gather_of(on_up)
            u, filled = mm_nn(hn2, W["ffn_w_up"], layer, name=f"ffn_up_fwd{layer}", split_out=True, out_dtype=BF16,
                              comm=comm_next)
            W.update(zip(names_next, filled))
        else:
            u = mm_nn(hn2, W["ffn_w_up"], layer, name=f"ffn_up_fwd{layer}", split_out=True, out_dtype=BF16)
        if on_act:
            names_next, comm_next = gather_of(on_act)
            act, rec["act_t"], rec["c"], filled = convact_fwd(u, cw, cb, layer, name=f"ffn_act_fwd{layer}",
                                                              comm=comm_next)
            W.update(zip(names_next, filled))
        else:
            act, rec["act_t"], rec["c"] = convact_fwd(u, cw, cb, layer, name=f"ffn_act_fwd{layer}")
        rec["u"] = u
        if on_down:
            names_next, comm_next = gather_of(on_down)
            h, filled = mm_nn(act, W["ffn_w_down"], layer, name=f"ffn_down_fwd{layer}", res=h, comm=comm_next)
            W.update(zip(names_next, filled))
        else:
            h = mm_nn(act, W["ffn_w_down"], layer, name=f"ffn_down_fwd{layer}", res=h)
        saved.append(rec)

    loss_part, dh, d_norm_final = loss_head(h, norm_final.reshape(1, D), target, name="loss_head")

    n_attn, n_hgrn = attn_w_in.shape[0], hgrn_w_in.shape[0]
    reduced_parts = []
    pending = []

    def swap_of(group, g):
        ordered = sorted(group, key=lambda nl: is_row[nl[0]])
        return ordered, swap_comm([(g[n], is_row[n]) for n, _ in ordered])

    def pair_up(tag, ordered, g, theirs):
        pair, pair_bf16 = {}, {}
        for (n, _), t in zip(ordered, theirs):
            gn = g[n]
            if is_row[n]:
                r = gn.shape[1] // N_CHIPS
                gn = gn.reshape(1, N_CHIPS, 2, r // 2, gn.shape[2])
            pair[n], pair_bf16[n] = pair_add(gn, t, pos, row_sharded=is_row[n], name=f"rs_pair_add_{n}{tag}")
        pending.append((ordered, pair, scatter_comm([(pair_bf16[n], is_row[n]) for n, _ in ordered])))

    def landed_behind(created, cut):
        for (ordered, pair, _), landed in zip(pending, cut(created)):
            reduced_parts.append((ordered, pair, landed))
        pending.clear()

    d_norm_mix, d_norm_ffn = [None] * depth, [None] * depth
    d_conv_w, d_conv_b = [None] * depth, [None] * depth
    d_sinks, d_hgrn_norm = [None] * n_attn, [None] * n_hgrn
    d_logits = jnp.zeros_like(hgrn_lb_logits)
    for layer in reversed(range(depth)):
        idx = layer // 2
        rec = saved[layer]
        g = {"ffn_w_down": mm_dw(rec["act_t"], dh, name=f"ffn_down_dw{layer}")}
        dact = mm_nt(dh, W["ffn_w_down"], layer, name=f"ffn_down_dx{layer}")
        du, dcw, dcb = convact_bwd(rec["c"], rec["u"], cw, layer, dact, name=f"ffn_act_bwd{layer}")
        d_conv_w[layer] = dcw.transpose(1, 0, 2).reshape(3, 2 * Fh)
        d_conv_b[layer] = dcb.reshape(2 * Fh)
        g["ffn_w_up"] = mm_dw(rec["hn2_t"], du, name=f"ffn_up_dw{layer}", split_b=True)
        ordered, swap = swap_of(layer_weights(layer)[2:], g)
        dh, dg, theirs = mm_nt(du, W["ffn_w_up"], layer, name=f"ffn_up_dx{layer}", split_in=True, comm=swap,
                               norm=(rec["h_ffn"], g_ffn, layer, dh))
        pair_up(layer, ordered, g, theirs)
        d_norm_ffn[layer] = dg.reshape(D)
        carried, cut = merge_comms([p[2] for p in pending])
        if layer % 2 == 0:
            g["attn_w_out"] = mm_tn(rec["o"], dh, name=f"attn_out_dw{layer}")
            do = mm_nt(dh, W["attn_w_out"], idx, name=f"attn_out_dx{layer}", out_dtype=BF16)
            do_heads = to_heads(do, n_q).reshape(n_kv, Q_PER_KV, S, HEAD_DIM)
            res = attn_bwd(rec["q"], rec["k"], rec["v"], rec["sink_col"], slope_col, do_heads,
                           name=f"attn_bwd{layer}", comm=carried)
            dq, dk, dv, dsink = res[:4]
            landed_behind(res[-1], cut)
            dproj = jnp.concatenate([from_heads(dq.reshape(n_q, S, HEAD_DIM)), from_heads(dk), from_heads(dv)], axis=1)
            d_sinks[idx] = jnp.sum(dsink.reshape(n_kv, Q_PER_KV, WINDOW), axis=-1).reshape(n_q)
            g["attn_w_in"] = mm_dw(rec["hn_t"], dproj, name=f"attn_in_dw{layer}")
            ordered, swap = swap_of(layer_weights(layer)[:2], g)
            dh, dg, theirs = mm_nt(dproj, W["attn_w_in"], idx, name=f"attn_in_dx{layer}", comm=swap,
                                   norm=(rec["h_mix"], g_mix, layer, dh))
        else:
            g["hgrn_w_out"] = mm_tn(rec["o"], dh, name=f"hgrn_out_dw{layer}")
            do = mm_nt(dh, W["hgrn_w_out"], idx, name=f"hgrn_out_dx{layer}")
            res = hgrn_bwd(rec["proj"], hgrn_lb_logits, rec["ng"], rec["states"], do, layer,
                           name=f"hgrn_bwd{layer}", comm=carried)
            dproj, dlg, dng = res[:3]
            landed_behind(res[-1], cut)
            d_logits = d_logits + dlg
            d_hgrn_norm[idx] = dng.reshape(HG_EXPAND)
            g["hgrn_w_in"] = mm_dw(rec["hn_t"], dproj, name=f"hgrn_in_dw{layer}")
            ordered, swap = swap_of(layer_weights(layer)[:2], g)
            dh, dg, theirs = mm_nt(dproj, W["hgrn_w_in"], idx, name=f"hgrn_in_dx{layer}", comm=swap,
                                   norm=(rec["h_mix"], g_mix, layer, dh))
        pair_up(layer, ordered, g, theirs)
        d_norm_mix[layer] = dg.reshape(D)
    grad_x = dh.reshape(x.shape)
    carried, cut = merge_comms([p[2] for p in pending])
    landed_behind(run_comm(carried, name="rs_scatter_last")[1], cut)

    small_w = [norm_mix, norm_ffn, norm_final, attn_sinks, hgrn_norm, hgrn_lb_logits, ffn_conv_b]
    small_m = [m_norm_mix, m_norm_ffn, m_norm_final, m_attn_sinks, m_hgrn_norm, m_hgrn_lb_logits, m_ffn_conv_b]
    small_v = [v_norm_mix, v_norm_ffn, v_norm_final, v_attn_sinks, v_hgrn_norm, v_hgrn_lb_logits, v_ffn_conv_b]
    small_grads = [jnp.stack(d_norm_mix), jnp.stack(d_norm_ffn), d_norm_final.reshape(D), jnp.stack(d_sinks),
                   jnp.stack(d_hgrn_norm), d_logits, jnp.stack(d_conv_b)]
    gconv = jnp.stack(d_conv_w)
    packed_small = _pack_small(small_grads + [loss_part])
    summed = all_reduce_small(jnp.concatenate([packed_small, _pack_small([gconv])], axis=0), name="all_reduce_small")
    reduced, gconv_rows = summed[:packed_small.shape[0]], summed[packed_small.shape[0]:]
    small_g = _unpack_small(reduced, small_w + [loss_part])
    loss = small_g.pop().reshape(())
    pad = [jnp.zeros((1, 1), F32)]
    sd, sm_, sv_ = adamw(_pack_small(small_w + pad), reduced, _pack_small(small_m + pad),
                         _pack_small(small_v + pad), name="adamw_small")
    small_d = _unpack_small(sd, small_w)
    small_nm = _unpack_small(sm_, small_w)
    small_nv = _unpack_small(sv_, small_w)

    big_names = col_names + row_names
    big_w = dict(attn_w_in=attn_w_in, hgrn_w_in=hgrn_w_in, ffn_w_up=ffn_w_up, attn_w_out=attn_w_out,
                 hgrn_w_out=hgrn_w_out, ffn_w_down=ffn_w_down)
    half_shard = dict.fromkeys(big_names)
    for ordered, pair, landed in reduced_parts:
        for (n, li), ld in zip(ordered, landed):
            half_shard[n] = chip_add(pair[n], ld, pos, half_shard[n], li, big_w[n].shape[0], row_sharded=is_row[n],
                                     name=f"rs_chip_add_{n}{li}")
    shards = sibling_join_halves([half_shard[n].reshape(big_w[n].shape) for n in big_names], name="rs_sibling_join")
    big_m = dict(attn_w_in=m_attn_w_in, hgrn_w_in=m_hgrn_w_in, ffn_w_up=m_ffn_w_up, attn_w_out=m_attn_w_out,
                 hgrn_w_out=m_hgrn_w_out, ffn_w_down=m_ffn_w_down)
    big_v = dict(attn_w_in=v_attn_w_in, hgrn_w_in=v_hgrn_w_in, ffn_w_up=v_ffn_w_up, attn_w_out=v_attn_w_out,
                 hgrn_w_out=v_hgrn_w_out, ffn_w_down=v_ffn_w_down)
    grads, deltas, new_m, new_v = {}, {}, {}, {}
    for name_, gshard in zip(big_names, shards):
        w_ = big_w[name_]
        cols = w_.shape[-1]
        d_, m_, v_ = adamw(w_.reshape(-1, cols), gshard.reshape(-1, cols), big_m[name_].reshape(-1, cols),
                           big_v[name_].reshape(-1, cols), name=f"adamw_{name_}")
        grads[name_] = gshard.reshape(w_.shape)
        deltas[name_], new_m[name_], new_v[name_] = (t.reshape(w_.shape) for t in (d_, m_, v_))

    n_conv = ffn_conv_w.shape[2]
    gconv_all = _unpack_small(gconv_rows, [gconv])[0]
    gconv_mine = lax.dynamic_slice_in_dim(gconv_all, me * n_conv, n_conv, axis=2)
    cshape = ffn_conv_w.shape
    d_, m_, v_ = adamw(ffn_conv_w.reshape(-1, n_conv), gconv_mine.reshape(-1, n_conv),
                       m_ffn_conv_w.reshape(-1, n_conv), v_ffn_conv_w.reshape(-1, n_conv), name="adamw_ffn_conv_w")
    grads["ffn_conv_w"] = gconv_mine
    deltas["ffn_conv_w"], new_m["ffn_conv_w"], new_v["ffn_conv_w"] = (t.reshape(cshape) for t in (d_, m_, v_))

    order = ["norm_mix", "norm_ffn", "norm_final", "attn_w_in", "attn_w_out", "attn_sinks", "hgrn_w_in",
             "hgrn_w_out", "hgrn_norm", "hgrn_lb_logits", "ffn_w_up", "ffn_conv_w", "ffn_conv_b", "ffn_w_down"]
    small_names = ["norm_mix", "norm_ffn", "norm_final", "attn_sinks", "hgrn_norm", "hgrn_lb_logits", "ffn_conv_b"]
    for i, name_ in enumerate(small_names):
        grads[name_], deltas[name_], new_m[name_], new_v[name_] = small_g[i], small_d[i], small_nm[i], small_nv[i]
    return (loss, grad_x, *[grads[n] for n in order], *[deltas[n] for n in order],
            *[new_m[n] for n in order], *[new_v[n] for n in order])
```
